```python
import math
import jax
import jax.numpy as jnp
from jax import lax
import numpy as np

D_MODEL = 1024
BATCH = 8
SEQ = 2048
DEPTH = 1
DEC_BATCH = 128
DEC_SEQ = 8
PAST_LEN = 16384
PAGE_SIZE = 128

SSD_D_INNER = D_MODEL
SSD_HEADDIM = 64
SSD_HEADS = SSD_D_INNER // SSD_HEADDIM
SSD_GROUPS = 2
SSD_STATE = 128
SSD_CONV = 4
SSD_CHUNK = 128
SSD_CONV_DIM = SSD_D_INNER + 2 * SSD_GROUPS * SSD_STATE
S5_WIDTH = D_MODEL
S5_GROUP = 16
S5_GROUPS = S5_WIDTH // S5_GROUP
S5_STATE = 64
IN_PROJ_SPLITS = (
    SSD_D_INNER,
    SSD_D_INNER + SSD_CONV_DIM,
    SSD_D_INNER + SSD_CONV_DIM + SSD_HEADS,
    SSD_D_INNER + SSD_CONV_DIM + SSD_HEADS + S5_WIDTH,
    SSD_D_INNER + SSD_CONV_DIM + SSD_HEADS + S5_WIDTH + D_MODEL,
)
IN_PROJ_DIM = SSD_D_INNER + SSD_CONV_DIM + SSD_HEADS + S5_WIDTH + 2 * D_MODEL
MEM_LEN = 256
MEM_HEADS = 4
MEM_HEAD_DIM = D_MODEL // MEM_HEADS
MOE_GROUPS = 4
MOE_EXPERTS_PER_GROUP = 8
MOE_TOP_K = 2
MOE_D_FF = D_MODEL // 2
NORM_EPS = 1e-6

kernel_name = 'hybrid_ssd_s5_memxattn_hmoe_step'


def rms_norm(x, g):
    xf = x.astype(jnp.float32)
    y = xf * lax.rsqrt(jnp.mean(xf * xf, axis=-1, keepdims=True) + NORM_EPS)
    return (y * g.astype(jnp.float32)).astype(x.dtype)


def causal_depthwise_conv(u, buf, w, b):
    L = u.shape[1]
    up = jnp.concatenate([buf.astype(u.dtype), u], axis=1)
    out = b
    for k in range(SSD_CONV):
        out = out + up[:, k:k + L] * w[k]
    return jax.nn.silu(out), up[:, L:]


def ssd_chunked_scan(xh, dt, a, bm, cm, h0):
    bt, L = xh.shape[0], xh.shape[1]
    q = math.gcd(L, SSD_CHUNK)
    nc = L // q
    r = SSD_HEADS // SSD_GROUPS
    x = xh.astype(jnp.float32).reshape(bt, nc, q, SSD_GROUPS, r, SSD_HEADDIM)
    dt = dt.reshape(bt, nc, q, SSD_GROUPS, r)
    b = bm.astype(jnp.float32).reshape(bt, nc, q, SSD_GROUPS, SSD_STATE)
    c = cm.astype(jnp.float32).reshape(bt, nc, q, SSD_GROUPS, SSD_STATE)
    da_cs = jnp.cumsum(dt * a.reshape(SSD_GROUPS, r), axis=2)
    cs_t = jnp.moveaxis(da_cs, 2, -1)
    causal = jnp.tril(jnp.ones((q, q), dtype=bool))
    decay = jnp.exp(jnp.where(causal, cs_t[..., :, None] - cs_t[..., None, :], -jnp.inf))
    cb = jnp.einsum('bclgn,bcsgn->bcgls', c, b)
    dt_t = jnp.moveaxis(dt, 2, -1)
    m = cb[:, :, :, None] * decay * dt_t[..., None, :]
    y = jnp.einsum('bcgrls,bcsgrp->bclgrp', m, x)
    w_end = jnp.exp(da_cs[:, :, -1:] - da_cs) * dt
    s_chunk = jnp.einsum('bcsgn,bcsgr,bcsgrp->bcgrpn', b, w_end, x)
    chunk_decay = jnp.exp(da_cs[:, :, -1])

    def carry_fn(h, inp):
        s_c, d_c = inp
        return d_c[..., None, None] * h + s_c, h

    h_init = h0.astype(jnp.float32).reshape(bt, SSD_GROUPS, r, SSD_HEADDIM, SSD_STATE)
    h_last, h_in = lax.scan(carry_fn, h_init, (jnp.moveaxis(s_chunk, 1, 0), jnp.moveaxis(chunk_decay, 1, 0)))
    h_in = jnp.moveaxis(h_in, 0, 1)
    y = y + jnp.einsum('bclgn,bcgrpn,bclgr->bclgrp', c, h_in, jnp.exp(da_cs))
    return (y.reshape(bt, L, SSD_HEADS, SSD_HEADDIM),
            h_last.reshape(bt, SSD_HEADS, SSD_HEADDIM, SSD_STATE))


def s5_scan(u, h0_re, h0_im, lam_re, lam_im, log_step, b_re, b_im, c_re, c_im, d):
    bt, L = u.shape[0], u.shape[1]
    lam = lax.complex(lam_re.astype(jnp.float32), lam_im.astype(jnp.float32))
    step = jnp.exp(log_step.astype(jnp.float32))[:, None]
    lam_bar = jnp.exp(lam * step)
    b_bar = ((lam_bar - 1.0) / lam)[..., None] * lax.complex(b_re.astype(jnp.float32), b_im.astype(jnp.float32))
    ug = u.astype(jnp.float32).reshape(bt, L, S5_GROUPS, S5_GROUP)
    bu = jnp.einsum('gnk,blgk->blgn', b_bar, ug.astype(jnp.complex64))
    h0 = lax.complex(h0_re.astype(jnp.float32), h0_im.astype(jnp.float32))
    bu = bu.at[:, 0].add(lam_bar * h0)
    a = jnp.broadcast_to(lam_bar, (1, L) + lam_bar.shape)

    def combine(e1, e2):
        a1, b1 = e1
        a2, b2 = e2
        return a1 * a2, a2 * b1 + b2

    _, states = lax.associative_scan(combine, (a, bu), axis=1)
    c = lax.complex(c_re.astype(jnp.float32), c_im.astype(jnp.float32))
    y = jnp.real(jnp.einsum('gkn,blgn->blgk', c, states)) + d.astype(jnp.float32).reshape(S5_GROUPS, S5_GROUP) * ug
    h_last = states[:, -1]
    return y.reshape(bt, L, S5_WIDTH), jnp.real(h_last), jnp.imag(h_last)


def token_mixer(hn, conv_buf, ssd_h0, s5_re0, s5_im0, w_in, conv_w, conv_b, ssd_dt_bias, ssd_a_log, ssd_d,
                ssd_norm_g, w_ssd_branch, s5_lambda_re, s5_lambda_im, s5_log_step, s5_b_re, s5_b_im,
                s5_c_re, s5_c_im, s5_d, w_glu, b_glu, w_mix_out):
    bt, L = hn.shape[0], hn.shape[1]
    dtype = hn.dtype
    proj = hn @ w_in
    z, xbc, dt_raw, u5, gate_a, gate_b = jnp.split(proj, IN_PROJ_SPLITS, axis=-1)
    xbc_c, new_conv = causal_depthwise_conv(xbc, conv_buf, conv_w, conv_b)
    xs, bm, cm = jnp.split(xbc_c, [SSD_D_INNER, SSD_D_INNER + SSD_GROUPS * SSD_STATE], axis=-1)
    xh = xs.reshape(bt, L, SSD_HEADS, SSD_HEADDIM)
    dt = jax.nn.softplus(dt_raw.astype(jnp.float32) + ssd_dt_bias.astype(jnp.float32))
    a = -jnp.exp(ssd_a_log.astype(jnp.float32))
    y_ssd, new_ssd = ssd_chunked_scan(xh, dt, a, bm.reshape(bt, L, SSD_GROUPS, SSD_STATE),
                                      cm.reshape(bt, L, SSD_GROUPS, SSD_STATE), ssd_h0)
    y_ssd = y_ssd + ssd_d.astype(jnp.float32)[:, None] * xh.astype(jnp.float32)
    y_ssd = rms_norm(y_ssd.reshape(bt, L, SSD_D_INNER) * jax.nn.silu(z.astype(jnp.float32)), ssd_norm_g).astype(dtype)
    branch_a = y_ssd @ w_ssd_branch
    y_s5, new_re, new_im = s5_scan(u5, s5_re0, s5_im0, s5_lambda_re, s5_lambda_im, s5_log_step,
                                   s5_b_re, s5_b_im, s5_c_re, s5_c_im, s5_d)
    glu = y_s5.astype(dtype) @ w_glu + b_glu
    branch_b = glu[..., :D_MODEL] * jax.nn.sigmoid(glu[..., D_MODEL:])
    merged = jax.nn.sigmoid(gate_a) * branch_a + jax.nn.sigmoid(gate_b) * branch_b
    return merged @ w_mix_out, new_conv, new_ssd, new_re, new_im


def memory_kv(mem, g, w_k, w_v):
    bt, m = mem.shape[0], mem.shape[1]
    mn = rms_norm(mem, g)
    k = (mn @ w_k).reshape(bt, m, MEM_HEADS, MEM_HEAD_DIM)
    v = (mn @ w_v).reshape(bt, m, MEM_HEADS, MEM_HEAD_DIM)
    return k, v


def memory_attention(hn, k, v, w_q, w_o):
    bt, L = hn.shape[0], hn.shape[1]
    q = (hn @ w_q).reshape(bt, L, MEM_HEADS, MEM_HEAD_DIM)
    s = jnp.einsum('blhd,bmhd->bhlm', q, k.astype(q.dtype)).astype(jnp.float32) * (MEM_HEAD_DIM ** -0.5)
    p = jax.nn.softmax(s, axis=-1).astype(hn.dtype)
    o = jnp.einsum('bhlm,bmhd->blhd', p, v.astype(hn.dtype))
    return o.reshape(bt, L, D_MODEL) @ w_o


def hier_moe(x, w_rg, b_rg, w_re, b_re, w_gate, w_up, w_down):
    shp = x.shape
    t = x.reshape(-1, D_MODEL)
    g_prob = jax.nn.softmax((t @ w_rg).astype(jnp.float32) + b_rg.astype(jnp.float32), axis=-1)
    g_idx = jnp.argmax(g_prob, axis=-1)
    g_w = jnp.take_along_axis(g_prob, g_idx[:, None], axis=-1)
    e_logits = ((t @ w_re).astype(jnp.float32) + b_re.astype(jnp.float32)).reshape(-1, MOE_GROUPS, MOE_EXPERTS_PER_GROUP)
    e_logits = jnp.take_along_axis(e_logits, g_idx[:, None, None], axis=1)[:, 0]
    e_prob = jax.nn.softmax(e_logits, axis=-1)
    top_v, top_i = lax.top_k(e_prob, MOE_TOP_K)
    top_v = top_v / jnp.sum(top_v, axis=-1, keepdims=True)
    e_w = jnp.sum(jax.nn.one_hot(top_i, MOE_EXPERTS_PER_GROUP, dtype=jnp.float32) * top_v[..., None], axis=1)
    comb = ((g_w * jax.nn.one_hot(g_idx, MOE_GROUPS, dtype=jnp.float32))[:, :, None] * e_w[:, None, :]).astype(t.dtype)
    out = jnp.zeros_like(t)
    for g in range(MOE_GROUPS):
        hg = jax.nn.silu(jnp.einsum('td,edf->tef', t, w_gate[g])) * jnp.einsum('td,edf->tef', t, w_up[g])
        out = out + jnp.einsum('tef,efd->td', hg * comb[:, g, :, None], w_down[g])
    return out.reshape(shp)


def decoder_layer(x, mem_k, mem_v, conv_buf, ssd_h0, s5_re0, s5_im0,
                  norm_mix_g, w_in, conv_w, conv_b, ssd_dt_bias, ssd_a_log, ssd_d, ssd_norm_g, w_ssd_branch,
                  s5_lambda_re, s5_lambda_im, s5_log_step, s5_b_re, s5_b_im, s5_c_re, s5_c_im, s5_d,
                  w_glu, b_glu, w_mix_out, norm_mem_q_g, w_mem_q, w_mem_o, norm_ffn_g,
                  w_router_group, b_router_group, w_router_expert, b_router_expert,
                  w_exp_gate, w_exp_up, w_exp_down):
    mix, new_conv, new_ssd, new_re, new_im = token_mixer(
        rms_norm(x, norm_mix_g), conv_buf, ssd_h0, s5_re0, s5_im0, w_in, conv_w, conv_b, ssd_dt_bias,
        ssd_a_log, ssd_d, ssd_norm_g, w_ssd_branch, s5_lambda_re, s5_lambda_im, s5_log_step,
        s5_b_re, s5_b_im, s5_c_re, s5_c_im, s5_d, w_glu, b_glu, w_mix_out)
    h = x + mix
    h = h + memory_attention(rms_norm(h, norm_mem_q_g), mem_k, mem_v, w_mem_q, w_mem_o)
    h = h + hier_moe(rms_norm(h, norm_ffn_g), w_router_group, b_router_group, w_router_expert,
                     b_router_expert, w_exp_gate, w_exp_up, w_exp_down)
    return h, new_conv, new_ssd, new_re, new_im


def setup_inputs(seed: int = 0) -> dict:
    key = jax.random.key(seed)
    ks = jax.random.split(key, 48)

    def nrm(i, shape, scale):
        return scale * jax.random.normal(ks[i], shape, jnp.float32)

    def gain(i, shape):
        return 1.0 + 0.05 * jax.random.normal(ks[i], shape, jnp.float32)

    L = DEPTH
    dt0 = jnp.exp(jax.random.uniform(ks[10], (L, SSD_HEADS), jnp.float32, math.log(1e-3), math.log(1e-1)))
    s5_n = jnp.arange(S5_STATE, dtype=jnp.float32)
    mem_shape = (L, DEC_BATCH, MEM_LEN, MEM_HEADS, MEM_HEAD_DIM)
    s5_state_shape = (L, DEC_BATCH, S5_GROUPS, S5_STATE)
    expert_in = (L, MOE_GROUPS, MOE_EXPERTS_PER_GROUP, D_MODEL, MOE_D_FF)
    return {
        'x_prompt': nrm(0, (BATCH, SEQ, D_MODEL), 1.0),
        'x_sample': nrm(1, (DEC_BATCH, DEC_SEQ, D_MODEL), 1.0),
        'state_conv': nrm(2, (L, DEC_BATCH, SSD_CONV - 1, SSD_CONV_DIM), 1.0),
        'state_ssd': nrm(3, (L, DEC_BATCH, SSD_HEADS, SSD_HEADDIM, SSD_STATE), 0.3),
        'state_s5_re': nrm(4, s5_state_shape, 0.5),
        'state_s5_im': nrm(5, s5_state_shape, 0.5),
        'cache_mem_k': nrm(6, mem_shape, 1.0),
        'cache_mem_v': nrm(7, mem_shape, 1.0),
        'mem_prompt': nrm(8, (BATCH, MEM_LEN, D_MODEL), 1.0),
        'norm_mix_g': gain(9, (L, D_MODEL)),
        'w_in': nrm(11, (L, D_MODEL, IN_PROJ_DIM), D_MODEL ** -0.5),
        'conv_w': nrm(12, (L, SSD_CONV, SSD_CONV_DIM), SSD_CONV ** -0.5),
        'conv_b': nrm(13, (L, SSD_CONV_DIM), 0.02),
        'ssd_dt_bias': dt0 + jnp.log(-jnp.expm1(-dt0)),
        'ssd_a_log': jnp.log(jax.random.uniform(ks[14], (L, SSD_HEADS), jnp.float32, 1.0, 16.0)),
        'ssd_d': gain(15, (L, SSD_HEADS)),
        'ssd_norm_g': gain(16, (L, SSD_D_INNER)),
        'w_ssd_branch': nrm(17, (L, SSD_D_INNER, D_MODEL), SSD_D_INNER ** -0.5),
        's5_lambda_re': -0.5 + nrm(18, (L, S5_GROUPS, S5_STATE), 0.01),
        's5_lambda_im': math.pi * s5_n + nrm(19, (L, S5_GROUPS, S5_STATE), 0.01),
        's5_log_step': jax.random.uniform(ks[20], (L, S5_GROUPS), jnp.float32, math.log(1e-3), math.log(1e-1)),
        's5_b_re': nrm(21, (L, S5_GROUPS, S5_STATE, S5_GROUP), (2 * S5_GROUP) ** -0.5),
        's5_b_im': nrm(22, (L, S5_GROUPS, S5_STATE, S5_GROUP), (2 * S5_GROUP) ** -0.5),
        's5_c_re': nrm(23, (L, S5_GROUPS, S5_GROUP, S5_STATE), (2 * S5_STATE) ** -0.5),
        's5_c_im': nrm(24, (L, S5_GROUPS, S5_GROUP, S5_STATE), (2 * S5_STATE) ** -0.5),
        's5_d': nrm(25, (L, S5_WIDTH), 0.5),
        'w_glu': nrm(26, (L, S5_WIDTH, 2 * D_MODEL), S5_WIDTH ** -0.5),
        'b_glu': nrm(27, (L, 2 * D_MODEL), 0.02),
        'w_mix_out': nrm(28, (L, D_MODEL, D_MODEL), D_MODEL ** -0.5),
        'norm_mem_q_g': gain(29, (L, D_MODEL)),
        'norm_mem_kv_g': gain(30, (L, D_MODEL)),
        'w_mem_q': nrm(31, (L, D_MODEL, D_MODEL), D_MODEL ** -0.5),
        'w_mem_k': nrm(32, (L, D_MODEL, D_MODEL), D_MODEL ** -0.5),
        'w_mem_v': nrm(33, (L, D_MODEL, D_MODEL), D_MODEL ** -0.5),
        'w_mem_o': nrm(34, (L, D_MODEL, D_MODEL), D_MODEL ** -0.5),
        'norm_ffn_g': gain(35, (L, D_MODEL)),
        'w_router_group': nrm(36, (L, D_MODEL, MOE_GROUPS), D_MODEL ** -0.5),
        'b_router_group': nrm(37, (L, MOE_GROUPS), 0.01),
        'w_router_expert': nrm(38, (L, D_MODEL, MOE_GROUPS * MOE_EXPERTS_PER_GROUP), D_MODEL ** -0.5),
        'b_router_expert': nrm(39, (L, MOE_GROUPS * MOE_EXPERTS_PER_GROUP), 0.01),
        'w_exp_gate': nrm(40, expert_in, D_MODEL ** -0.5),
        'w_exp_up': nrm(41, expert_in, D_MODEL ** -0.5),
        'w_exp_down': nrm(42, (L, MOE_GROUPS, MOE_EXPERTS_PER_GROUP, MOE_D_FF, D_MODEL), MOE_D_FF ** -0.5),
        'norm_final_g': gain(43, (D_MODEL,)),
    }


def reference(x_prompt, x_sample, state_conv, state_ssd, state_s5_re, state_s5_im, cache_mem_k, cache_mem_v,
              mem_prompt, norm_mix_g, w_in, conv_w, conv_b, ssd_dt_bias, ssd_a_log, ssd_d, ssd_norm_g,
              w_ssd_branch, s5_lambda_re, s5_lambda_im, s5_log_step, s5_b_re, s5_b_im, s5_c_re, s5_c_im,
              s5_d, w_glu, b_glu, w_mix_out, norm_mem_q_g, norm_mem_kv_g, w_mem_q, w_mem_k, w_mem_v,
              w_mem_o, norm_ffn_g, w_router_group, b_router_group, w_router_expert, b_router_expert,
              w_exp_gate, w_exp_up, w_exp_down, norm_final_g):
    bp = x_prompt.shape[0]
    dtype = x_prompt.dtype
    zero_conv = jnp.zeros((bp, SSD_CONV - 1, SSD_CONV_DIM), dtype)
    zero_ssd = jnp.zeros((bp, SSD_HEADS, SSD_HEADDIM, SSD_STATE), dtype)
    zero_s5 = jnp.zeros((bp, S5_GROUPS, S5_STATE), dtype)
    hp, hs = x_prompt, x_sample
    conv_p, ssd_p, re_p, im_p, mk_p_l, mv_p_l = [], [], [], [], [], []
    conv_s, ssd_s, re_s, im_s = [], [], [], []
    for l in range(DEPTH):
        lw = (norm_mix_g[l], w_in[l], conv_w[l], conv_b[l], ssd_dt_bias[l], ssd_a_log[l], ssd_d[l],
              ssd_norm_g[l], w_ssd_branch[l], s5_lambda_re[l], s5_lambda_im[l], s5_log_step[l],
              s5_b_re[l], s5_b_im[l], s5_c_re[l], s5_c_im[l], s5_d[l], w_glu[l], b_glu[l], w_mix_out[l],
              norm_mem_q_g[l], w_mem_q[l], w_mem_o[l], norm_ffn_g[l], w_router_group[l], b_router_group[l],
              w_router_expert[l], b_router_expert[l], w_exp_gate[l], w_exp_up[l], w_exp_down[l])
        mk_p, mv_p = memory_kv(mem_prompt, norm_mem_kv_g[l], w_mem_k[l], w_mem_v[l])
        hp, c1, s1, r1, i1 = decoder_layer(hp, mk_p, mv_p, zero_conv, zero_ssd, zero_s5, zero_s5, *lw)
        hs, c2, s2, r2, i2 = decoder_layer(hs, cache_mem_k[l], cache_mem_v[l], state_conv[l], state_ssd[l],
                                           state_s5_re[l], state_s5_im[l], *lw)
        conv_p.append(c1); ssd_p.append(s1); re_p.append(r1); im_p.append(i1)
        mk_p_l.append(mk_p); mv_p_l.append(mv_p)
        conv_s.append(c2); ssd_s.append(s2); re_s.append(r2); im_s.append(i2)
    y_prompt = rms_norm(hp, norm_final_g)
    y_sample = rms_norm(hs, norm_final_g)
    return (y_prompt, y_sample,
            jnp.stack(conv_p), jnp.stack(ssd_p), jnp.stack(re_p), jnp.stack(im_p),
            jnp.stack(mk_p_l), jnp.stack(mv_p_l),
            jnp.stack(conv_s), jnp.stack(ssd_s), jnp.stack(re_s), jnp.stack(im_s))
```

```python
import functools
import math

import numpy as np
import jax
import jax.numpy as jnp
from jax import lax
from jax.experimental import pallas as pl
from jax.experimental.pallas import tpu as pltpu

F32 = jnp.float32
BF = jnp.bfloat16
I32 = jnp.int32

D_MODEL = 1024
SSD_HEADS = 16
SSD_HEADDIM = 64
SSD_GROUPS = 2
SSD_STATE = 128
SSD_CONV = 4
SSD_CONV_DIM = 1536
HEADS_PER_GROUP = SSD_HEADS // SSD_GROUPS
S5_GROUPS = 64
S5_GROUP = 16
S5_STATE = 64
S5_FLAT = S5_GROUPS * S5_STATE
S5_SLABS = D_MODEL // 128
MEM_LEN = 256
MEM_HEADS = 4
MEM_HEAD_DIM = 256
MOE_GROUPS = 4
MOE_EXPERTS_PER_GROUP = 8
MOE_EXPERTS = MOE_GROUPS * MOE_EXPERTS_PER_GROUP
MOE_D_FF = 512
NORM_EPS = 1e-6

LANES = 128
VMEM_LIMIT_BYTES = 56 * 1024 * 1024
TOKEN_TILE = 512
SSD_CHUNK = 128
S5_TIME_TILE = 32
MOE_ROW_TILE = 256
GATHER_TILE = 512
NEG_BIG = -1e30


def _params(n_axes):
    return pltpu.CompilerParams(dimension_semantics=("arbitrary",) * n_axes,
                                vmem_limit_bytes=VMEM_LIMIT_BYTES)


def _const_spec(shape):
    nd = len(shape)
    return pl.BlockSpec(shape, lambda *_: (0,) * nd)


def _dot(a, b):
    return jnp.dot(a, b, preferred_element_type=F32)


def _dot_nt(a, b):
    return lax.dot_general(a, b, (((1,), (1,)), ((), ())), preferred_element_type=F32)


def _split3(v):
    h = v.astype(BF)
    r = v - h.astype(F32)
    m = r.astype(BF)
    l = (r - m.astype(F32)).astype(BF)
    return h, m, l


def _dot_exact(a_bf, v, nt=False):
    f = _dot_nt if nt else _dot
    h, m, l = _split3(v)
    return f(a_bf, h) + f(a_bf, m) + f(a_bf, l)


def _rms(x, g):
    return x * lax.rsqrt(jnp.mean(x * x, axis=-1, keepdims=True) + NORM_EPS) * g


def _sigmoid(x):
    return 1.0 / (1.0 + jnp.exp(-x))


def _silu(x):
    return x * _sigmoid(x)


def _proj_cols(xb, w_ref, o_ref, c0, width):
    for c in range(0, width, 512):
        ce = min(c + 512, width)
        o_ref[:, c:ce] = _dot(xb, w_ref[:, c0 + c:c0 + ce]).astype(o_ref.dtype)


def _inproj_kernel(x_ref, g_ref, w_ref, wdt_ref, dtb_ref, z_ref, xbc_ref, u5_ref, ga_ref, gb_ref, dt_ref):
    xb = _rms(x_ref[...], g_ref[...]).astype(BF)
    c0 = 0
    for o_ref in (z_ref, xbc_ref, u5_ref, ga_ref, gb_ref):
        width = o_ref.shape[1]
        _proj_cols(xb, w_ref, o_ref, c0, width)
        c0 += width
    raw = _dot(xb, wdt_ref[...]) + dtb_ref[...]
    dt_ref[...] = jnp.maximum(raw, 0.0) + jnp.log1p(jnp.exp(-jnp.abs(raw)))


def _inproj(x2d, g, w_main, w_dt, dt_bias):
    t = x2d.shape[0]
    tm = TOKEN_TILE
    widths = (D_MODEL, SSD_CONV_DIM, D_MODEL, D_MODEL, D_MODEL)
    row = lambda w: pl.BlockSpec((tm, w), lambda i: (i, 0))
    return pl.pallas_call(
        _inproj_kernel,
        grid=(t // tm,),
        in_specs=[row(D_MODEL), _const_spec(g.shape), _const_spec(w_main.shape), _const_spec(w_dt.shape),
                  _const_spec(dt_bias.shape)],
        out_specs=[row(w) for w in widths] + [row(LANES)],
        out_shape=[jax.ShapeDtypeStruct((t, w), BF) for w in widths] + [jax.ShapeDtypeStruct((t, LANES), F32)],
        compiler_params=_params(1),
        name="inproj",
    )(x2d, g, w_main, w_dt, dt_bias)


def _memkv_kernel(x_ref, g_ref, w_ref, k_ref, v_ref):
    xb = _rms(x_ref[...], g_ref[...]).astype(BF)
    _proj_cols(xb, w_ref, k_ref, 0, D_MODEL)
    _proj_cols(xb, w_ref, v_ref, D_MODEL, D_MODEL)


def _memkv(mem2d, g, w_kv):
    t = mem2d.shape[0]
    tm = TOKEN_TILE
    row = pl.BlockSpec((tm, D_MODEL), lambda i: (i, 0))
    return pl.pallas_call(
        _memkv_kernel,
        grid=(t // tm,),
        in_specs=[row, _const_spec(g.shape), _const_spec(w_kv.shape)],
        out_specs=[row, row],
        out_shape=[jax.ShapeDtypeStruct((t, D_MODEL), F32)] * 2,
        compiler_params=_params(1),
        name="memkv",
    )(mem2d, g, w_kv)


def _ssd_kernel(*refs, lq, has_h0):
    q = SSD_CHUNK
    nseq = q // lq
    if has_h0:
        (xbc_ref, dt_ref, cw_ref, cb_ref, alog_ref, dsk_ref, eexp_ref, eye_ref, conv0_ref, h0_ref,
         y_ref, convo_ref, ho_ref, cbuf, yacc, yint, xwt, tot_s) = refs
        hin_ref = h0_ref
    else:
        (xbc_ref, dt_ref, cw_ref, cb_ref, alog_ref, dsk_ref, eexp_ref, eye_ref,
         y_ref, convo_ref, ho_ref, cbuf, yacc, yint, xwt, tot_s) = refs
        hin_ref = ho_ref

        @pl.when(pl.program_id(1) == 0)
        def _():
            ho_ref[...] = jnp.zeros_like(ho_ref)
            cbuf[:, 5:8, :] = jnp.zeros((nseq, 3, SSD_CONV_DIM), F32)

    if has_h0:
        cbuf[:, 5:8, :] = conv0_ref[...]

    xraw = xbc_ref[...].astype(F32)
    for i in range(nseq):
        cbuf[i, 8:8 + lq, :] = xraw[i * lq:(i + 1) * lq]
    convs = []
    for i in range(nseq):
        acc = cb_ref[...]
        for k in range(SSD_CONV):
            acc = acc + cbuf[i, 5 + k:5 + k + lq, :] * cw_ref[k:k + 1, :]
        convs.append(acc)
        tail = cbuf[i, lq + 5:lq + 8, :]
        convo_ref[i] = tail
        cbuf[i, 5:8, :] = tail
    xc = _silu(convs[0] if nseq == 1 else jnp.concatenate(convs, axis=0))
    xs = xc[:, :D_MODEL]
    xs_bf = xs.astype(BF)
    bm_bf = xc[:, D_MODEL:D_MODEL + SSD_GROUPS * SSD_STATE].astype(BF)
    cm = xc[:, D_MODEL + SSD_GROUPS * SSD_STATE:]
    cm_bf = cm.astype(BF)

    dt = dt_ref[...]
    da = dt * (-jnp.exp(alog_ref[...]))
    ri = lax.broadcasted_iota(I32, (q, q), 0)
    ci = lax.broadcasted_iota(I32, (q, q), 1)
    if nseq == 1:
        causal = ci <= ri
    else:
        sh = int(math.log2(lq))
        same = lax.shift_right_logical(ri, sh) == lax.shift_right_logical(ci, sh)
        causal = same & (ci <= ri)
    lmat = jnp.where(causal, 1.0, 0.0).astype(BF)
    cs = _dot_exact(lmat, da)
    if nseq == 1:
        tot = jnp.broadcast_to(cs[q - 1:q, :], (q, LANES))
    else:
        tot = _dot_exact(jnp.where(same, 1.0, 0.0).astype(BF), da)
    tot_s[...] = tot
    eye = eye_ref[...]
    cs_t = _dot_exact(eye, cs, nt=True)
    dt_t = _dot_exact(eye, dt, nt=True)
    eexp = eexp_ref[...]
    ecs_h, ecs_m, _ = _split3(jnp.exp(cs))
    ecs_x = _dot(ecs_h, eexp) + _dot(ecs_m, eexp)
    wend_x = _dot((jnp.exp(tot - cs) * dt).astype(BF), eexp)

    lane = lax.broadcasted_iota(I32, (q, LANES), 1)
    for g in range(SSD_GROUPS):
        cbg = _dot_nt(cm_bf[:, g * SSD_STATE:(g + 1) * SSD_STATE], bm_bf[:, g * SSD_STATE:(g + 1) * SSD_STATE])
        for jp in range(HEADS_PER_GROUP // 2):
            j = g * (HEADS_PER_GROUP // 2) + jp
            ms = []
            for h in (2 * j, 2 * j + 1):
                col = jnp.broadcast_to(cs[:, h:h + 1], (q, q))
                row = jnp.broadcast_to(cs_t[h:h + 1, :], (q, q))
                dtr = jnp.broadcast_to(dt_t[h:h + 1, :], (q, q))
                ms.append((jnp.where(causal, jnp.exp(col - row), 0.0) * cbg * dtr).astype(BF))
            mp = jnp.concatenate(ms, axis=1)
            xp = xs_bf[:, j * LANES:(j + 1) * LANES]
            zero = jnp.zeros_like(xp)
            x2 = jnp.concatenate([jnp.where(lane < SSD_HEADDIM, xp, zero),
                                  jnp.where(lane >= SSD_HEADDIM, xp, zero)], axis=0)
            yacc[:, j * LANES:(j + 1) * LANES] = _dot(mp, x2)

    gw = HEADS_PER_GROUP * SSD_HEADDIM
    xw = xs * wend_x
    for g in range(SSD_GROUPS):
        xwt[g * gw:(g + 1) * gw, :] = xw[:, g * gw:(g + 1) * gw].T.astype(BF)

    if nseq == 1:
        for g in range(SSD_GROUPS):
            hin_g = hin_ref[0, g * HEADS_PER_GROUP:(g + 1) * HEADS_PER_GROUP].reshape(gw, SSD_STATE)
            yint[:, g * gw:(g + 1) * gw] = _dot_nt(cm_bf[:, g * SSD_STATE:(g + 1) * SSD_STATE], hin_g.astype(BF))
            s_new = _dot(xwt[g * gw:(g + 1) * gw, :], bm_bf[:, g * SSD_STATE:(g + 1) * SSD_STATE])
            for hl in range(HEADS_PER_GROUP):
                h = g * HEADS_PER_GROUP + hl
                dec = jnp.exp(jnp.broadcast_to(tot[0:1, h:h + 1], (SSD_HEADDIM, SSD_STATE)))
                ho_ref[0, h] = dec * hin_ref[0, h] + s_new[hl * SSD_HEADDIM:(hl + 1) * SSD_HEADDIM]
    else:
        yint[...] = jnp.zeros_like(yint)
        sh = int(math.log2(lq))
        rowseq = lax.shift_right_logical(lax.broadcasted_iota(I32, (q, SSD_STATE), 0), sh)
        colseq = lax.shift_right_logical(lax.broadcasted_iota(I32, (gw, q), 1), sh)

        def seq_body(i, carry):
            trow = tot_s[pl.ds(i * lq, 1), :]
            for g in range(SSD_GROUPS):
                hin_g = hin_ref[i, g * HEADS_PER_GROUP:(g + 1) * HEADS_PER_GROUP].reshape(gw, SSD_STATE)
                cmg = cm[:, g * SSD_STATE:(g + 1) * SSD_STATE]
                lhs = jnp.where(rowseq == i, cmg, 0.0).astype(BF)
                yint[:, g * gw:(g + 1) * gw] += _dot_nt(lhs, hin_g.astype(BF))
                xg = xwt[g * gw:(g + 1) * gw, :]
                xi = jnp.where(colseq == i, xg, jnp.zeros_like(xg))
                s_new = _dot(xi, bm_bf[:, g * SSD_STATE:(g + 1) * SSD_STATE])
                for hl in range(HEADS_PER_GROUP):
                    h = g * HEADS_PER_GROUP + hl
                    dec = jnp.exp(jnp.broadcast_to(trow[:, h:h + 1], (SSD_HEADDIM, SSD_STATE)))
                    ho_ref[i, h] = dec * hin_ref[i, h] + s_new[hl * SSD_HEADDIM:(hl + 1) * SSD_HEADDIM]
            return carry

        lax.fori_loop(0, nseq, seq_body, 0)

    y_ref[...] = (yacc[...] + yint[...] * ecs_x + dsk_ref[...] * xs).astype(y_ref.dtype)


def _ssd(xbc, dt, conv_w, conv_b, a_log, dskip, eexp, eye, conv0, h0, *, batch, seqlen):
    q = SSD_CHUNK
    t = xbc.shape[0]
    has_h0 = h0 is not None
    if has_h0:
        lq = seqlen
        nseq = q // lq
        grid = (t // q,)
        rmap = lambda i: (i, 0)
        smap3 = lambda i: (i, 0, 0)
        smap4 = lambda i: (i, 0, 0, 0)
    else:
        lq = q
        nseq = 1
        nc = seqlen // q
        grid = (batch, nc)
        rmap = lambda b, c: (b * nc + c, 0)
        smap3 = lambda b, c: (b, 0, 0)
        smap4 = lambda b, c: (b, 0, 0, 0)
    consts = [conv_w, conv_b, a_log, dskip, eexp, eye]
    in_specs = [pl.BlockSpec((q, SSD_CONV_DIM), rmap), pl.BlockSpec((q, LANES), rmap)]
    in_specs += [_const_spec(c.shape) for c in consts]
    args = [xbc, dt] + consts
    if has_h0:
        in_specs += [pl.BlockSpec((nseq, SSD_CONV - 1, SSD_CONV_DIM), smap3),
                     pl.BlockSpec((nseq, SSD_HEADS, SSD_HEADDIM, SSD_STATE), smap4)]
        args += [conv0, h0]
    return pl.pallas_call(
        functools.partial(_ssd_kernel, lq=lq, has_h0=has_h0),
        grid=grid,
        in_specs=in_specs,
        out_specs=[pl.BlockSpec((q, D_MODEL), rmap),
                   pl.BlockSpec((nseq, SSD_CONV - 1, SSD_CONV_DIM), smap3),
                   pl.BlockSpec((nseq, SSD_HEADS, SSD_HEADDIM, SSD_STATE), smap4)],
        out_shape=[jax.ShapeDtypeStruct((t, D_MODEL), BF),
                   jax.ShapeDtypeStruct((batch, SSD_CONV - 1, SSD_CONV_DIM), F32),
                   jax.ShapeDtypeStruct((batch, SSD_HEADS, SSD_HEADDIM, SSD_STATE), F32)],
        scratch_shapes=[pltpu.VMEM((nseq, lq + 8, SSD_CONV_DIM), F32),
                        pltpu.VMEM((q, D_MODEL), F32),
                        pltpu.VMEM((q, D_MODEL), F32),
                        pltpu.VMEM((D_MODEL, q), BF),
                        pltpu.VMEM((q, LANES), F32)],
        compiler_params=_params(len(grid)),
        name="ssd_sample" if has_h0 else "ssd_prompt",
    )(*args)


S5_SEQ_TILE = 8
S5_LANE_CHUNK = 1024


def _s5_kernel(*refs, tl, has_h0):
    bt = S5_SEQ_TILE
    m = bt * tl
    if has_h0:
        (u_ref, perm_ref, permt_ref, bblk_ref, cblk_ref, lre_ref, lim_ref, d_ref, h0re_ref, h0im_ref,
         y_ref, ore_ref, oim_ref, bre, bim, sre, sim) = refs
    else:
        (u_ref, perm_ref, permt_ref, bblk_ref, cblk_ref, lre_ref, lim_ref, d_ref,
         y_ref, ore_ref, oim_ref, bre, bim, sre, sim) = refs

    @pl.when(pl.program_id(1) == 0)
    def _():
        if has_h0:
            sre[...] = h0re_ref[...]
            sim[...] = h0im_ref[...]
        else:
            sre[...] = jnp.zeros_like(sre)
            sim[...] = jnp.zeros_like(sim)

    u = u_ref[...].reshape(m, D_MODEL)
    u_tm = _dot(perm_ref[...], u).astype(BF)
    half = S5_FLAT // S5_SLABS
    for j in range(S5_SLABS):
        r = _dot(u_tm[:, j * LANES:(j + 1) * LANES], bblk_ref[j])
        bre[:, j * half:(j + 1) * half] = r[:, :half]
        bim[:, j * half:(j + 1) * half] = r[:, half:]

    for jc in range(S5_FLAT // S5_LANE_CHUNK):
        sl = slice(jc * S5_LANE_CHUNK, (jc + 1) * S5_LANE_CHUNK)
        lr = lre_ref[:, sl]
        li = lim_ref[:, sl]

        def step(l, carry, sl=sl, lr=lr, li=li):
            sr, si = carry
            rows = pl.ds(pl.multiple_of(l * bt, bt), bt)
            nr = lr * sr - li * si + bre[rows, sl]
            ni = lr * si + li * sr + bim[rows, sl]
            bre[rows, sl] = nr
            bim[rows, sl] = ni
            return nr, ni

        sr, si = lax.fori_loop(0, tl, step, (sre[:, sl], sim[:, sl]))
        sre[:, sl] = sr
        sim[:, sl] = si

    ys = []
    for j in range(S5_SLABS):
        st = jnp.concatenate([bre[:, j * half:(j + 1) * half], bim[:, j * half:(j + 1) * half]], axis=1)
        ys.append(_dot(st.astype(BF), cblk_ref[j]))
    y_tm = jnp.concatenate(ys, axis=1).astype(BF)
    y_bm = _dot(permt_ref[...], y_tm) + d_ref[...] * u.astype(F32)
    y_ref[...] = y_bm.astype(y_ref.dtype).reshape(y_ref.shape)
    ore_ref[...] = sre[...]
    oim_ref[...] = sim[...]


def _s5(u, perm, permt, bblk, cblk, lre, lim, dvec, h0re, h0im, *, batch, seqlen, tl):
    bt = S5_SEQ_TILE
    m = bt * tl
    has_h0 = h0re is not None
    nb = batch // bt
    nt = seqlen // tl
    if u.ndim == 3:
        u_spec = pl.BlockSpec((bt, tl, D_MODEL), lambda b, t: (b, t, 0))
    else:
        u_spec = pl.BlockSpec((m, D_MODEL), lambda b, t: (b * nt + t, 0))
    st_spec = pl.BlockSpec((bt, S5_FLAT), lambda b, t: (b, 0))
    consts = [perm, permt, bblk, cblk, lre, lim, dvec]
    in_specs = [u_spec] + [_const_spec(c.shape) for c in consts]
    args = [u] + consts
    if has_h0:
        in_specs += [st_spec, st_spec]
        args += [h0re, h0im]
    return pl.pallas_call(
        functools.partial(_s5_kernel, tl=tl, has_h0=has_h0),
        grid=(nb, nt),
        in_specs=in_specs,
        out_specs=[u_spec, st_spec, st_spec],
        out_shape=[jax.ShapeDtypeStruct(u.shape, BF),
                   jax.ShapeDtypeStruct((batch, S5_FLAT), F32),
                   jax.ShapeDtypeStruct((batch, S5_FLAT), F32)],
        scratch_shapes=[pltpu.VMEM((m, S5_FLAT), F32), pltpu.VMEM((m, S5_FLAT), F32),
                        pltpu.VMEM((bt, S5_FLAT), F32), pltpu.VMEM((bt, S5_FLAT), F32)],
        compiler_params=_params(2),
        name="s5_sample" if has_h0 else "s5_prompt",
    )(*args)


def _mixout_kernel(y_ref, z_ref, y5_ref, ga_ref, gb_ref, x_ref, ng_ref, wa_ref, wglu_ref, bglu_ref, wmix_ref, o_ref):
    y = y_ref[...].astype(F32) * _silu(z_ref[...].astype(F32))
    yn = _rms(y, ng_ref[...]).astype(BF)
    branch_a = _dot(yn, wa_ref[...])
    glu = _dot(y5_ref[...], wglu_ref[...]) + bglu_ref[...]
    branch_b = glu[:, :D_MODEL] * _sigmoid(glu[:, D_MODEL:])
    merged = _sigmoid(ga_ref[...].astype(F32)) * branch_a + _sigmoid(gb_ref[...].astype(F32)) * branch_b
    o_ref[...] = x_ref[...] + _dot(merged.astype(BF), wmix_ref[...])


def _mixout(y, z, y5, ga, gb, x2d, ng, wa, wglu, bglu, wmix):
    t = x2d.shape[0]
    tm = TOKEN_TILE
    row = pl.BlockSpec((tm, D_MODEL), lambda i: (i, 0))
    consts = [ng, wa, wglu, bglu, wmix]
    return pl.pallas_call(
        _mixout_kernel,
        grid=(t // tm,),
        in_specs=[row] * 6 + [_const_spec(c.shape) for c in consts],
        out_specs=row,
        out_shape=jax.ShapeDtypeStruct((t, D_MODEL), F32),
        compiler_params=_params(1),
        name="mixout",
    )(y, z, y5, ga, gb, x2d, *consts)


def _attn_kernel(h_ref, g_ref, wq_ref, wo_ref, k_ref, v_ref, o_ref, obuf, *, nkv, rows_per_seq):
    h = h_ref[...]
    r = h.shape[0]
    qv = _dot(_rms(h, g_ref[...]).astype(BF), wq_ref[...]).astype(BF)
    kb = k_ref[...].reshape(nkv * MEM_LEN, D_MODEL).astype(BF)
    vb = v_ref[...].reshape(nkv * MEM_LEN, D_MODEL).astype(BF)
    if nkv > 1:
        rs = lax.shift_right_logical(lax.broadcasted_iota(I32, (r, nkv * MEM_LEN), 0), int(math.log2(rows_per_seq)))
        cq = lax.shift_right_logical(lax.broadcasted_iota(I32, (r, nkv * MEM_LEN), 1), int(math.log2(MEM_LEN)))
        visible = rs == cq
    scale = MEM_HEAD_DIM ** -0.5
    for hd in range(MEM_HEADS):
        sl = slice(hd * MEM_HEAD_DIM, (hd + 1) * MEM_HEAD_DIM)
        s = _dot_nt(qv[:, sl], kb[:, sl]) * scale
        if nkv > 1:
            s = jnp.where(visible, s, NEG_BIG)
        p = jnp.exp(s - jnp.max(s, axis=-1, keepdims=True))
        p = p / jnp.sum(p, axis=-1, keepdims=True)
        obuf[:, sl] = _dot(p.astype(BF), vb[:, sl])
    o_ref[...] = h + _dot(obuf[...].astype(BF), wo_ref[...])


def _attn(h2d, g, wq, wo, k3, v3, *, rows, nkv, rows_per_seq):
    t = h2d.shape[0]
    nsteps = t // rows
    steps_per_kv = nsteps // (k3.shape[0] // nkv)
    row = pl.BlockSpec((rows, D_MODEL), lambda i: (i, 0))
    kv = pl.BlockSpec((nkv, MEM_LEN, D_MODEL), lambda i: (i // steps_per_kv, 0, 0))
    consts = [g, wq, wo]
    return pl.pallas_call(
        functools.partial(_attn_kernel, nkv=nkv, rows_per_seq=rows_per_seq),
        grid=(nsteps,),
        in_specs=[row] + [_const_spec(c.shape) for c in consts] + [kv, kv],
        out_specs=row,
        out_shape=jax.ShapeDtypeStruct((t, D_MODEL), F32),
        scratch_shapes=[pltpu.VMEM((rows, D_MODEL), F32)],
        compiler_params=_params(1),
        name="attn",
    )(h2d, *consts, k3, v3)


ROUTER_LANE0 = MOE_GROUPS


def _router_kernel(hp_ref, hs_ref, g_ref, wh_ref, wm_ref, wl_ref, b_ref, xn_ref, idx_ref, wt_ref, *, np_steps):
    i = pl.program_id(0)
    h = jnp.where(i < np_steps, hp_ref[...], hs_ref[...])
    xn = _rms(h, g_ref[...])
    xn_ref[...] = xn
    xh, xm, xl = _split3(xn)
    wh, wm, wl = wh_ref[...], wm_ref[...], wl_ref[...]
    logits = (_dot(xh, wh) + _dot(xh, wm) + _dot(xm, wh) + _dot(xh, wl) + _dot(xm, wm) + _dot(xl, wh)) + b_ref[...]
    lane = lax.broadcasted_iota(I32, logits.shape, 1)
    is_g = lane < MOE_GROUPS
    lg = jnp.where(is_g, logits, NEG_BIG)
    mg = jnp.max(lg, axis=-1, keepdims=True)
    gidx = jnp.min(jnp.where(lg == mg, lane, LANES), axis=-1, keepdims=True)
    g_w = 1.0 / jnp.sum(jnp.where(is_g, jnp.exp(lg - mg), 0.0), axis=-1, keepdims=True)
    e_lane = lane - ROUTER_LANE0
    in_grp = ((e_lane >= 0) & (e_lane < MOE_EXPERTS)
              & (lax.shift_right_logical(jnp.maximum(e_lane, 0), 3) == gidx))
    le = jnp.where(in_grp, logits, NEG_BIG)
    m1 = jnp.max(le, axis=-1, keepdims=True)
    i1 = jnp.min(jnp.where(le == m1, lane, LANES), axis=-1, keepdims=True)
    le2 = jnp.where(lane == i1, NEG_BIG, le)
    m2 = jnp.max(le2, axis=-1, keepdims=True)
    i2 = jnp.min(jnp.where(le2 == m2, lane, LANES), axis=-1, keepdims=True)
    ratio = jnp.exp(m2 - m1)
    w1 = g_w / (1.0 + ratio)
    w2 = g_w * ratio / (1.0 + ratio)
    idx_ref[...] = jnp.where(lane == 0, i1 - ROUTER_LANE0, jnp.where(lane == 1, i2 - ROUTER_LANE0, 0))
    wt_ref[...] = jnp.where(lane == 0, w1, jnp.where(lane == 1, w2, 0.0))


def _router(hp, hs, g, wh, wm, wl, b):
    tm = TOKEN_TILE
    np_steps = hp.shape[0] // tm
    ns_steps = hs.shape[0] // tm
    t = hp.shape[0] + hs.shape[0]
    consts = [g, wh, wm, wl, b]
    row = lambda w: pl.BlockSpec((tm, w), lambda i: (i, 0))
    return pl.pallas_call(
        functools.partial(_router_kernel, np_steps=np_steps),
        grid=(np_steps + ns_steps,),
        in_specs=[pl.BlockSpec((tm, D_MODEL), lambda i: (jnp.minimum(i, np_steps - 1), 0)),
                  pl.BlockSpec((tm, D_MODEL), lambda i: (jnp.maximum(i - np_steps, 0), 0))]
                 + [_const_spec(c.shape) for c in consts],
        out_specs=[row(D_MODEL), row(LANES), row(LANES)],
        out_shape=[jax.ShapeDtypeStruct((t, D_MODEL), F32), jax.ShapeDtypeStruct((t, LANES), I32),
                   jax.ShapeDtypeStruct((t, LANES), F32)],
        compiler_params=_params(1),
        name="router",
    )(hp, hs, *consts)


def _gather_kernel(idx_ref, src_ref, o_ref, sem):
    tr = o_ref.shape[0]

    def row_copy(r, src_row):
        return pltpu.make_async_copy(src_ref.at[pl.ds(src_row, 1)], o_ref.at[pl.ds(r, 1)], sem)

    def start(r, c):
        row_copy(r, idx_ref[0, 0, r]).start()
        return c

    def wait(r, c):
        row_copy(r, 0).wait()
        return c

    lax.fori_loop(0, tr, start, 0)
    lax.fori_loop(0, tr, wait, 0)


def _gather_rows(src, idx):
    r = idx.shape[0]
    tr = GATHER_TILE
    c = src.shape[1]
    idx3 = idx.reshape(r // tr, 1, tr)
    return pl.pallas_call(
        _gather_kernel,
        grid=(r // tr,),
        in_specs=[pl.BlockSpec((1, 1, tr), lambda i: (i, 0, 0), memory_space=pltpu.SMEM),
                  pl.BlockSpec(memory_space=pl.ANY)],
        out_specs=pl.BlockSpec((tr, c), lambda i: (i, 0)),
        out_shape=jax.ShapeDtypeStruct((r, c), src.dtype),
        scratch_shapes=[pltpu.SemaphoreType.DMA(())],
        compiler_params=_params(1),
        name="gather_rows",
    )(idx3, src)


def _expert_kernel(te_ref, nu_ref, x_ref, wg_ref, wu_ref, wd_ref, y_ref):
    i = pl.program_id(0)

    @pl.when(i < nu_ref[0])
    def _():
        x = x_ref[...].astype(BF)
        gate = _dot(x, wg_ref[0].astype(BF))
        up = _dot(x, wu_ref[0].astype(BF))
        y_ref[...] = _dot((_silu(gate) * up).astype(BF), wd_ref[0].astype(BF))

    @pl.when(i >= nu_ref[0])
    def _():
        y_ref[...] = jnp.zeros_like(y_ref)


def _experts(x_sorted, tile_expert, n_used, wg, wu, wd):
    r = x_sorted.shape[0]
    tm = MOE_ROW_TILE
    grid_spec = pltpu.PrefetchScalarGridSpec(
        num_scalar_prefetch=2,
        grid=(r // tm,),
        in_specs=[pl.BlockSpec((tm, D_MODEL), lambda i, te, nu: (i, 0)),
                  pl.BlockSpec((1, D_MODEL, MOE_D_FF), lambda i, te, nu: (te[i], 0, 0)),
                  pl.BlockSpec((1, D_MODEL, MOE_D_FF), lambda i, te, nu: (te[i], 0, 0)),
                  pl.BlockSpec((1, MOE_D_FF, D_MODEL), lambda i, te, nu: (te[i], 0, 0))],
        out_specs=pl.BlockSpec((tm, D_MODEL), lambda i, te, nu: (i, 0)),
    )
    return pl.pallas_call(
        _expert_kernel,
        grid_spec=grid_spec,
        out_shape=jax.ShapeDtypeStruct((r, D_MODEL), F32),
        compiler_params=_params(1),
        name="experts",
    )(tile_expert, n_used, x_sorted, wg, wu, wd)


def _combine_kernel(h_ref, yy_ref, wt_ref, g_ref, o_ref):
    wt = wt_ref[...]
    yy = yy_ref[...]
    h = h_ref[...] + wt[:, 0:1] * yy[:, :D_MODEL] + wt[:, 1:2] * yy[:, D_MODEL:]
    o_ref[...] = _rms(h, g_ref[...])


def _combine(h2d, yy, wt, g, *, tile_offset):
    t = h2d.shape[0]
    tm = TOKEN_TILE
    return pl.pallas_call(
        _combine_kernel,
        grid=(t // tm,),
        in_specs=[pl.BlockSpec((tm, D_MODEL), lambda i: (i, 0)),
                  pl.BlockSpec((tm, 2 * D_MODEL), lambda i: (i + tile_offset, 0)),
                  pl.BlockSpec((tm, LANES), lambda i: (i + tile_offset, 0)),
                  _const_spec(g.shape)],
        out_specs=pl.BlockSpec((tm, D_MODEL), lambda i: (i, 0)),
        out_shape=jax.ShapeDtypeStruct((t, D_MODEL), F32),
        compiler_params=_params(1),
        name="combine",
    )(h2d, yy, wt, g)


def _moe_plan(idx_slab):
    t = idx_slab.shape[0]
    tm = MOE_ROW_TILE
    n_rows = 2 * t + MOE_EXPERTS * (tm - 1)
    n_rows = ((n_rows + GATHER_TILE - 1) // GATHER_TILE) * GATHER_TILE
    n_tiles = n_rows // tm
    e_flat = idx_slab[:, :2].reshape(-1)
    onehot = (e_flat[:, None] == jnp.arange(MOE_EXPERTS, dtype=I32)[None, :]).astype(I32)
    csum = jnp.cumsum(onehot, axis=0)
    rank = jnp.sum(onehot * csum, axis=1) - 1
    counts = csum[-1]
    padded = ((counts + tm - 1) // tm) * tm
    pend = jnp.cumsum(padded)
    pstart = pend - padded
    pos = pstart[e_flat] + rank
    tok_of_row = jnp.zeros((n_rows,), I32).at[pos].set(jnp.arange(2 * t, dtype=I32) // 2)
    tile_start = jnp.arange(n_tiles, dtype=I32) * tm
    tile_expert = jnp.minimum(jnp.searchsorted(pend, tile_start, side="right"), MOE_EXPERTS - 1).astype(I32)
    n_used = (pend[-1] // tm).astype(I32).reshape(1)
    return pos.astype(I32), tok_of_row, tile_expert, n_used


def _perm_matrix(bt, tl):
    m = bt * tl
    p = np.zeros((m, m), np.float32)
    for l in range(tl):
        for b in range(bt):
            p[l * bt + b, b * tl + l] = 1.0
    return p


def _s5_operators(lam_re, lam_im, log_step, b_re, b_im, c_re, c_im):
    lam = lax.complex(lam_re, lam_im)
    step = jnp.exp(log_step)[:, None]
    lam_bar = jnp.exp(lam * step)
    b_bar = ((lam_bar - 1.0) / lam)[..., None] * lax.complex(b_re, b_im)
    gl = LANES // S5_GROUP
    eye = jnp.eye(gl, dtype=F32)

    def in_block(bpart):
        bp = bpart.reshape(S5_SLABS, gl, S5_STATE, S5_GROUP)
        blk = jnp.einsum("sgnk,gh->sgkhn", bp, eye)
        return blk.reshape(S5_SLABS, LANES, gl * S5_STATE)

    def out_block(cpart):
        cp = cpart.reshape(S5_SLABS, gl, S5_GROUP, S5_STATE)
        blk = jnp.einsum("sgkn,gh->sgnhk", cp, eye)
        return blk.reshape(S5_SLABS, gl * S5_STATE, LANES)

    bblk = jnp.concatenate([in_block(jnp.real(b_bar)), in_block(jnp.imag(b_bar))], axis=2).astype(BF)
    cblk = jnp.concatenate([out_block(c_re), out_block(-c_im)], axis=1).astype(BF)
    lre = jnp.broadcast_to(jnp.real(lam_bar).reshape(1, S5_FLAT), (S5_SEQ_TILE, S5_FLAT))
    lim = jnp.broadcast_to(jnp.imag(lam_bar).reshape(1, S5_FLAT), (S5_SEQ_TILE, S5_FLAT))
    return bblk, cblk, lre, lim


def _pad_lanes(v, width=LANES):
    return jnp.pad(v, [(0, 0)] * (v.ndim - 1) + [(0, width - v.shape[-1])])


def kernel(x_prompt, x_sample, state_conv, state_ssd, state_s5_re, state_s5_im, cache_mem_k, cache_mem_v, mem_prompt, norm_mix_g, w_in, conv_w, conv_b, ssd_dt_bias, ssd_a_log, ssd_d, ssd_norm_g, w_ssd_branch, s5_lambda_re, s5_lambda_im, s5_log_step, s5_b_re, s5_b_im, s5_c_re, s5_c_im, s5_d, w_glu, b_glu, w_mix_out, norm_mem_q_g, norm_mem_kv_g, w_mem_q, w_mem_k, w_mem_v, w_mem_o, norm_ffn_g, w_router_group, b_router_group, w_router_expert, b_router_expert, w_exp_gate, w_exp_up, w_exp_down, norm_final_g):
    bp, lp, _ = x_prompt.shape
    bs, ls, _ = x_sample.shape
    tp, ts = bp * lp, bs * ls
    row1 = lambda v: v.reshape(1, -1).astype(F32)

    w_in0 = w_in[0]
    o_xbc = D_MODEL
    o_dt = o_xbc + SSD_CONV_DIM
    o_u5 = o_dt + SSD_HEADS
    w_main = jnp.concatenate([w_in0[:, :o_dt], w_in0[:, o_u5:]], axis=1).astype(BF)
    w_dt = _pad_lanes(w_in0[:, o_dt:o_u5]).astype(BF)
    dt_bias = _pad_lanes(row1(ssd_dt_bias[0]))
    a_log = _pad_lanes(row1(ssd_a_log[0]))
    dskip = jnp.repeat(ssd_d[0].astype(F32), SSD_HEADDIM).reshape(1, D_MODEL)
    eexp_np = np.zeros((LANES, D_MODEL), np.float32)
    for h in range(SSD_HEADS):
        eexp_np[h, h * SSD_HEADDIM:(h + 1) * SSD_HEADDIM] = 1.0
    eexp = jnp.asarray(eexp_np).astype(BF)
    eye16 = jnp.asarray(np.eye(SSD_HEADS, LANES, dtype=np.float32)).astype(BF)
    bblk, cblk, lre, lim = _s5_operators(s5_lambda_re[0], s5_lambda_im[0], s5_log_step[0], s5_b_re[0],
                                         s5_b_im[0], s5_c_re[0], s5_c_im[0])
    s5d = row1(s5_d[0])
    w_kv = jnp.concatenate([w_mem_k[0], w_mem_v[0]], axis=1).astype(BF)
    w_r = _pad_lanes(jnp.concatenate([w_router_group[0], w_router_expert[0]], axis=1).astype(F32))
    wr_h, wr_m, wr_l = _split3(w_r)
    b_r = _pad_lanes(row1(jnp.concatenate([b_router_group[0], b_router_expert[0]])))
    wg = w_exp_gate[0].reshape(MOE_EXPERTS, D_MODEL, MOE_D_FF)
    wu = w_exp_up[0].reshape(MOE_EXPERTS, D_MODEL, MOE_D_FF)
    wd = w_exp_down[0].reshape(MOE_EXPERTS, MOE_D_FF, D_MODEL)

    def mixer(x2d, batch, seqlen, conv0, h0, s5re0, s5im0):
        z, xbc, u5, ga, gb, dt = _inproj(x2d, row1(norm_mix_g[0]), w_main, w_dt, dt_bias)
        y_ssd, new_conv, new_ssd = _ssd(xbc, dt, conv_w[0].astype(F32), row1(conv_b[0]), a_log, dskip, eexp, eye16,
                                        conv0, h0, batch=batch, seqlen=seqlen)
        if h0 is None:
            tl = S5_TIME_TILE
            u_in = u5.reshape(batch, seqlen, D_MODEL)
        else:
            tl = seqlen
            u_in = u5
        perm = _perm_matrix(S5_SEQ_TILE, tl)
        y5, new_re, new_im = _s5(u_in, jnp.asarray(perm).astype(BF), jnp.asarray(perm.T).astype(BF), bblk, cblk,
                                 lre, lim, s5d, s5re0, s5im0, batch=batch, seqlen=seqlen, tl=tl)
        h1 = _mixout(y_ssd, z, y5.reshape(-1, D_MODEL), ga, gb, x2d, row1(ssd_norm_g[0]),
                     w_ssd_branch[0].astype(BF), w_glu[0].astype(BF), row1(b_glu[0]), w_mix_out[0].astype(BF))
        return h1, new_conv, new_ssd, new_re, new_im

    xp2 = x_prompt.reshape(tp, D_MODEL)
    xs2 = x_sample.reshape(ts, D_MODEL)
    h1p, conv_p, ssd_p, re_p, im_p = mixer(xp2, bp, lp, None, None, None, None)
    h1s, conv_s, ssd_s, re_s, im_s = mixer(xs2, bs, ls, state_conv[0], state_ssd[0],
                                           state_s5_re[0].reshape(bs, S5_FLAT), state_s5_im[0].reshape(bs, S5_FLAT))

    mk_p, mv_p = _memkv(mem_prompt.reshape(bp * MEM_LEN, D_MODEL), row1(norm_mem_kv_g[0]), w_kv)
    gq = row1(norm_mem_q_g[0])
    wq = w_mem_q[0].astype(BF)
    wo = w_mem_o[0].astype(BF)
    h2p = _attn(h1p, gq, wq, wo, mk_p.reshape(bp, MEM_LEN, D_MODEL), mv_p.reshape(bp, MEM_LEN, D_MODEL),
                rows=TOKEN_TILE, nkv=1, rows_per_seq=lp)
    sample_seqs = 4
    h2s = _attn(h1s, gq, wq, wo, cache_mem_k[0].reshape(bs, MEM_LEN, D_MODEL),
                cache_mem_v[0].reshape(bs, MEM_LEN, D_MODEL), rows=sample_seqs * ls, nkv=sample_seqs,
                rows_per_seq=ls)

    xn_all, r_idx, r_wt = _router(h2p, h2s, row1(norm_ffn_g[0]), wr_h, wr_m, wr_l, b_r)
    pos, tok_of_row, tile_expert, n_used = _moe_plan(r_idx)
    x_sorted = _gather_rows(xn_all, tok_of_row)
    y_sorted = _experts(x_sorted, tile_expert, n_used, wg, wu, wd)
    yy = _gather_rows(y_sorted, pos).reshape(tp + ts, 2 * D_MODEL)
    gf = row1(norm_final_g)
    y_prompt = _combine(h2p, yy, r_wt, gf, tile_offset=0)
    y_sample = _combine(h2s, yy, r_wt, gf, tile_offset=tp // TOKEN_TILE)

    return (y_prompt.reshape(bp, lp, D_MODEL), y_sample.reshape(bs, ls, D_MODEL),
            conv_p[None], ssd_p[None],
            re_p.reshape(1, bp, S5_GROUPS, S5_STATE), im_p.reshape(1, bp, S5_GROUPS, S5_STATE),
            mk_p.reshape(1, bp, MEM_LEN, MEM_HEADS, MEM_HEAD_DIM), mv_p.reshape(1, bp, MEM_LEN, MEM_HEADS, MEM_HEAD_DIM),
            conv_s[None], ssd_s[None],
            re_s.reshape(1, bs, S5_GROUPS, S5_STATE), im_s.reshape(1, bs, S5_GROUPS, S5_STATE))
```

```python
import functools
import math

import numpy as np
import jax
import jax.numpy as jnp
from jax import lax
from jax.experimental import pallas as pl
from jax.experimental.pallas import tpu as pltpu

F32 = jnp.float32
BF = jnp.bfloat16
I32 = jnp.int32

D_MODEL = 1024
SSD_HEADS = 16
SSD_HEADDIM = 64
SSD_GROUPS = 2
SSD_STATE = 128
SSD_CONV = 4
SSD_CONV_DIM = 1536
HEADS_PER_GROUP = SSD_HEADS // SSD_GROUPS
S5_GROUPS = 64
S5_GROUP = 16
S5_STATE = 64
S5_FLAT = S5_GROUPS * S5_STATE
S5_SLABS = D_MODEL // 128
MEM_LEN = 256
MEM_HEADS = 4
MEM_HEAD_DIM = 256
MOE_GROUPS = 4
MOE_EXPERTS_PER_GROUP = 8
MOE_EXPERTS = MOE_GROUPS * MOE_EXPERTS_PER_GROUP
MOE_D_FF = 512
NORM_EPS = 1e-6

LANES = 128
VMEM_LIMIT_BYTES = 56 * 1024 * 1024
TOKEN_TILE = 512
SSD_CHUNK = 128
S5_TIME_TILE = 32
MOE_ROW_TILE = 256
NEG_BIG = -1e30


def _params(n_axes):
    return pltpu.CompilerParams(dimension_semantics=("arbitrary",) * n_axes,
                                vmem_limit_bytes=VMEM_LIMIT_BYTES)


def _const_spec(shape):
    nd = len(shape)
    return pl.BlockSpec(shape, lambda *_: (0,) * nd)


def _dot(a, b):
    return jnp.dot(a, b, preferred_element_type=F32)


def _dot_nt(a, b):
    return lax.dot_general(a, b, (((1,), (1,)), ((), ())), preferred_element_type=F32)


def _split3(v):
    h = v.astype(BF)
    r = v - h.astype(F32)
    m = r.astype(BF)
    l = (r - m.astype(F32)).astype(BF)
    return h, m, l


def _dot_exact(a_bf, v, nt=False):
    f = _dot_nt if nt else _dot
    h, m, l = _split3(v)
    return f(a_bf, h) + f(a_bf, m) + f(a_bf, l)


def _rms(x, g):
    return x * lax.rsqrt(jnp.mean(x * x, axis=-1, keepdims=True) + NORM_EPS) * g


def _sigmoid(x):
    return 1.0 / (1.0 + jnp.exp(-x))


def _silu(x):
    return x * _sigmoid(x)


def _proj_cols(xb, w_ref, o_ref, c0, width):
    for c in range(0, width, 512):
        ce = min(c + 512, width)
        o_ref[:, c:ce] = _dot(xb, w_ref[:, c0 + c:c0 + ce]).astype(o_ref.dtype)


def _inproj_kernel(x_ref, g_ref, w_ref, wdt_ref, dtb_ref, z_ref, xbc_ref, u5_ref, ga_ref, gb_ref, dt_ref):
    xb = _rms(x_ref[...], g_ref[...]).astype(BF)
    c0 = 0
    for o_ref in (z_ref, xbc_ref, u5_ref, ga_ref, gb_ref):
        width = o_ref.shape[1]
        _proj_cols(xb, w_ref, o_ref, c0, width)
        c0 += width
    raw = _dot(xb, wdt_ref[...]) + dtb_ref[...]
    dt_ref[...] = jnp.maximum(raw, 0.0) + jnp.log1p(jnp.exp(-jnp.abs(raw)))


def _inproj(x2d, g, w_main, w_dt, dt_bias):
    t = x2d.shape[0]
    tm = TOKEN_TILE
    widths = (D_MODEL, SSD_CONV_DIM, D_MODEL, D_MODEL, D_MODEL)
    row = lambda w: pl.BlockSpec((tm, w), lambda i: (i, 0))
    return pl.pallas_call(
        _inproj_kernel,
        grid=(t // tm,),
        in_specs=[row(D_MODEL), _const_spec(g.shape), _const_spec(w_main.shape), _const_spec(w_dt.shape),
                  _const_spec(dt_bias.shape)],
        out_specs=[row(w) for w in widths] + [row(LANES)],
        out_shape=[jax.ShapeDtypeStruct((t, w), BF) for w in widths] + [jax.ShapeDtypeStruct((t, LANES), F32)],
        compiler_params=_params(1),
        name="inproj",
    )(x2d, g, w_main, w_dt, dt_bias)


def _memkv_kernel(x_ref, g_ref, w_ref, k_ref, v_ref):
    xb = _rms(x_ref[...], g_ref[...]).astype(BF)
    _proj_cols(xb, w_ref, k_ref, 0, D_MODEL)
    _proj_cols(xb, w_ref, v_ref, D_MODEL, D_MODEL)


def _memkv(mem2d, g, w_kv):
    t = mem2d.shape[0]
    tm = TOKEN_TILE
    row = pl.BlockSpec((tm, D_MODEL), lambda i: (i, 0))
    return pl.pallas_call(
        _memkv_kernel,
        grid=(t // tm,),
        in_specs=[row, _const_spec(g.shape), _const_spec(w_kv.shape)],
        out_specs=[row, row],
        out_shape=[jax.ShapeDtypeStruct((t, D_MODEL), F32)] * 2,
        compiler_params=_params(1),
        name="memkv",
    )(mem2d, g, w_kv)


def _ssd_kernel(*refs, lq, has_h0):
    q = SSD_CHUNK
    nseq = q // lq
    if has_h0:
        (xbc_ref, dt_ref, cw_ref, cb_ref, alog_ref, dsk_ref, eexp_ref, eye_ref, conv0_ref, h0_ref,
         y_ref, convo_ref, ho_ref, cbuf, yacc, yint, xwt, tot_s) = refs
        hin_ref = h0_ref
    else:
        (xbc_ref, dt_ref, cw_ref, cb_ref, alog_ref, dsk_ref, eexp_ref, eye_ref,
         y_ref, convo_ref, ho_ref, cbuf, yacc, yint, xwt, tot_s) = refs
        hin_ref = ho_ref

        @pl.when(pl.program_id(1) == 0)
        def _():
            ho_ref[...] = jnp.zeros_like(ho_ref)
            cbuf[:, 5:8, :] = jnp.zeros((nseq, 3, SSD_CONV_DIM), F32)

    if has_h0:
        cbuf[:, 5:8, :] = conv0_ref[...]

    xraw = xbc_ref[...].astype(F32)
    for i in range(nseq):
        cbuf[i, 8:8 + lq, :] = xraw[i * lq:(i + 1) * lq]
    convs = []
    for i in range(nseq):
        acc = cb_ref[...]
        for k in range(SSD_CONV):
            acc = acc + cbuf[i, 5 + k:5 + k + lq, :] * cw_ref[k:k + 1, :]
        convs.append(acc)
        tail = cbuf[i, lq + 5:lq + 8, :]
        convo_ref[i] = tail
        cbuf[i, 5:8, :] = tail
    xc = _silu(convs[0] if nseq == 1 else jnp.concatenate(convs, axis=0))
    xs = xc[:, :D_MODEL]
    xs_bf = xs.astype(BF)
    bm_bf = xc[:, D_MODEL:D_MODEL + SSD_GROUPS * SSD_STATE].astype(BF)
    cm = xc[:, D_MODEL + SSD_GROUPS * SSD_STATE:]
    cm_bf = cm.astype(BF)

    dt = dt_ref[...]
    da = dt * (-jnp.exp(alog_ref[...]))
    ri = lax.broadcasted_iota(I32, (q, q), 0)
    ci = lax.broadcasted_iota(I32, (q, q), 1)
    if nseq == 1:
        causal = ci <= ri
    else:
        sh = int(math.log2(lq))
        same = lax.shift_right_logical(ri, sh) == lax.shift_right_logical(ci, sh)
        causal = same & (ci <= ri)
    lmat = jnp.where(causal, 1.0, 0.0).astype(BF)
    cs = _dot_exact(lmat, da)
    if nseq == 1:
        tot = jnp.broadcast_to(cs[q - 1:q, :], (q, LANES))
    else:
        tot = _dot_exact(jnp.where(same, 1.0, 0.0).astype(BF), da)
    tot_s[...] = tot
    eye = eye_ref[...]
    cs_t = _dot_exact(eye, cs, nt=True)
    dt_t = _dot_exact(eye, dt, nt=True)
    eexp = eexp_ref[...]
    ecs_h, ecs_m, _ = _split3(jnp.exp(cs))
    ecs_x = _dot(ecs_h, eexp) + _dot(ecs_m, eexp)
    wend_x = _dot((jnp.exp(tot - cs) * dt).astype(BF), eexp)

    lane = lax.broadcasted_iota(I32, (q, LANES), 1)
    for g in range(SSD_GROUPS):
        cbg = _dot_nt(cm_bf[:, g * SSD_STATE:(g + 1) * SSD_STATE], bm_bf[:, g * SSD_STATE:(g + 1) * SSD_STATE])
        for jp in range(HEADS_PER_GROUP // 2):
            j = g * (HEADS_PER_GROUP // 2) + jp
            ms = []
            for h in (2 * j, 2 * j + 1):
                col = jnp.broadcast_to(cs[:, h:h + 1], (q, q))
                row = jnp.broadcast_to(cs_t[h:h + 1, :], (q, q))
                dtr = jnp.broadcast_to(dt_t[h:h + 1, :], (q, q))
                ms.append((jnp.where(causal, jnp.exp(col - row), 0.0) * cbg * dtr).astype(BF))
            mp = jnp.concatenate(ms, axis=1)
            xp = xs_bf[:, j * LANES:(j + 1) * LANES]
            zero = jnp.zeros_like(xp)
            x2 = jnp.concatenate([jnp.where(lane < SSD_HEADDIM, xp, zero),
                                  jnp.where(lane >= SSD_HEADDIM, xp, zero)], axis=0)
            yacc[:, j * LANES:(j + 1) * LANES] = _dot(mp, x2)

    gw = HEADS_PER_GROUP * SSD_HEADDIM
    xw = xs * wend_x
    for g in range(SSD_GROUPS):
        xwt[g * gw:(g + 1) * gw, :] = xw[:, g * gw:(g + 1) * gw].T.astype(BF)

    if nseq == 1:
        for g in range(SSD_GROUPS):
            hin_g = hin_ref[0, g * HEADS_PER_GROUP:(g + 1) * HEADS_PER_GROUP].reshape(gw, SSD_STATE)
            yint[:, g * gw:(g + 1) * gw] = _dot_nt(cm_bf[:, g * SSD_STATE:(g + 1) * SSD_STATE], hin_g.astype(BF))
            s_new = _dot(xwt[g * gw:(g + 1) * gw, :], bm_bf[:, g * SSD_STATE:(g + 1) * SSD_STATE])
            for hl in range(HEADS_PER_GROUP):
                h = g * HEADS_PER_GROUP + hl
                dec = jnp.exp(jnp.broadcast_to(tot[0:1, h:h + 1], (SSD_HEADDIM, SSD_STATE)))
                ho_ref[0, h] = dec * hin_ref[0, h] + s_new[hl * SSD_HEADDIM:(hl + 1) * SSD_HEADDIM]
    else:
        yint[...] = jnp.zeros_like(yint)
        sh = int(math.log2(lq))
        rowseq = lax.shift_right_logical(lax.broadcasted_iota(I32, (q, SSD_STATE), 0), sh)
        colseq = lax.shift_right_logical(lax.broadcasted_iota(I32, (gw, q), 1), sh)

        def seq_body(i, carry):
            trow = tot_s[pl.ds(i * lq, 1), :]
            for g in range(SSD_GROUPS):
                hin_g = hin_ref[i, g * HEADS_PER_GROUP:(g + 1) * HEADS_PER_GROUP].reshape(gw, SSD_STATE)
                cmg = cm[:, g * SSD_STATE:(g + 1) * SSD_STATE]
                lhs = jnp.where(rowseq == i, cmg, 0.0).astype(BF)
                yint[:, g * gw:(g + 1) * gw] += _dot_nt(lhs, hin_g.astype(BF))
                xg = xwt[g * gw:(g + 1) * gw, :]
                xi = jnp.where(colseq == i, xg, jnp.zeros_like(xg))
                s_new = _dot(xi, bm_bf[:, g * SSD_STATE:(g + 1) * SSD_STATE])
                for hl in range(HEADS_PER_GROUP):
                    h = g * HEADS_PER_GROUP + hl
                    dec = jnp.exp(jnp.broadcast_to(trow[:, h:h + 1], (SSD_HEADDIM, SSD_STATE)))
                    ho_ref[i, h] = dec * hin_ref[i, h] + s_new[hl * SSD_HEADDIM:(hl + 1) * SSD_HEADDIM]
            return carry

        lax.fori_loop(0, nseq, seq_body, 0)

    y_ref[...] = (yacc[...] + yint[...] * ecs_x + dsk_ref[...] * xs).astype(y_ref.dtype)


def _ssd(xbc, dt, conv_w, conv_b, a_log, dskip, eexp, eye, conv0, h0, *, batch, seqlen):
    q = SSD_CHUNK
    t = xbc.shape[0]
    has_h0 = h0 is not None
    if has_h0:
        lq = seqlen
        nseq = q // lq
        grid = (t // q,)
        rmap = lambda i: (i, 0)
        smap3 = lambda i: (i, 0, 0)
        smap4 = lambda i: (i, 0, 0, 0)
    else:
        lq = q
        nseq = 1
        nc = seqlen // q
        grid = (batch, nc)
        rmap = lambda b, c: (b * nc + c, 0)
        smap3 = lambda b, c: (b, 0, 0)
        smap4 = lambda b, c: (b, 0, 0, 0)
    consts = [conv_w, conv_b, a_log, dskip, eexp, eye]
    in_specs = [pl.BlockSpec((q, SSD_CONV_DIM), rmap), pl.BlockSpec((q, LANES), rmap)]
    in_specs += [_const_spec(c.shape) for c in consts]
    args = [xbc, dt] + consts
    if has_h0:
        in_specs += [pl.BlockSpec((nseq, SSD_CONV - 1, SSD_CONV_DIM), smap3),
                     pl.BlockSpec((nseq, SSD_HEADS, SSD_HEADDIM, SSD_STATE), smap4)]
        args += [conv0, h0]
    return pl.pallas_call(
        functools.partial(_ssd_kernel, lq=lq, has_h0=has_h0),
        grid=grid,
        in_specs=in_specs,
        out_specs=[pl.BlockSpec((q, D_MODEL), rmap),
                   pl.BlockSpec((nseq, SSD_CONV - 1, SSD_CONV_DIM), smap3),
                   pl.BlockSpec((nseq, SSD_HEADS, SSD_HEADDIM, SSD_STATE), smap4)],
        out_shape=[jax.ShapeDtypeStruct((t, D_MODEL), BF),
                   jax.ShapeDtypeStruct((batch, SSD_CONV - 1, SSD_CONV_DIM), F32),
                   jax.ShapeDtypeStruct((batch, SSD_HEADS, SSD_HEADDIM, SSD_STATE), F32)],
        scratch_shapes=[pltpu.VMEM((nseq, lq + 8, SSD_CONV_DIM), F32),
                        pltpu.VMEM((q, D_MODEL), F32),
                        pltpu.VMEM((q, D_MODEL), F32),
                        pltpu.VMEM((D_MODEL, q), BF),
                        pltpu.VMEM((q, LANES), F32)],
        compiler_params=_params(len(grid)),
        name="ssd_sample" if has_h0 else "ssd_prompt",
    )(*args)


S5_SEQ_TILE = 8
S5_LANE_CHUNK = 1024


def _s5_kernel(*refs, tl, has_h0):
    bt = S5_SEQ_TILE
    m = bt * tl
    if has_h0:
        (u_ref, perm_ref, permt_ref, bblk_ref, cblk_ref, lre_ref, lim_ref, d_ref, h0re_ref, h0im_ref,
         y_ref, ore_ref, oim_ref, bre, bim, sre, sim) = refs
    else:
        (u_ref, perm_ref, permt_ref, bblk_ref, cblk_ref, lre_ref, lim_ref, d_ref,
         y_ref, ore_ref, oim_ref, bre, bim, sre, sim) = refs

    @pl.when(pl.program_id(1) == 0)
    def _():
        if has_h0:
            sre[...] = h0re_ref[...]
            sim[...] = h0im_ref[...]
        else:
            sre[...] = jnp.zeros_like(sre)
            sim[...] = jnp.zeros_like(sim)

    u = u_ref[...].reshape(m, D_MODEL)
    u_tm = _dot(perm_ref[...], u).astype(BF)
    half = S5_FLAT // S5_SLABS
    for j in range(S5_SLABS):
        r = _dot(u_tm[:, j * LANES:(j + 1) * LANES], bblk_ref[j])
        bre[:, j * half:(j + 1) * half] = r[:, :half]
        bim[:, j * half:(j + 1) * half] = r[:, half:]

    for jc in range(S5_FLAT // S5_LANE_CHUNK):
        sl = slice(jc * S5_LANE_CHUNK, (jc + 1) * S5_LANE_CHUNK)
        lr = lre_ref[:, sl]
        li = lim_ref[:, sl]

        def step(l, carry, sl=sl, lr=lr, li=li):
            sr, si = carry
            rows = pl.ds(pl.multiple_of(l * bt, bt), bt)
            nr = lr * sr - li * si + bre[rows, sl]
            ni = lr * si + li * sr + bim[rows, sl]
            bre[rows, sl] = nr
            bim[rows, sl] = ni
            return nr, ni

        sr, si = lax.fori_loop(0, tl, step, (sre[:, sl], sim[:, sl]))
        sre[:, sl] = sr
        sim[:, sl] = si

    ys = []
    for j in range(S5_SLABS):
        st = jnp.concatenate([bre[:, j * half:(j + 1) * half], bim[:, j * half:(j + 1) * half]], axis=1)
        ys.append(_dot(st.astype(BF), cblk_ref[j]))
    y_tm = jnp.concatenate(ys, axis=1).astype(BF)
    y_bm = _dot(permt_ref[...], y_tm) + d_ref[...] * u.astype(F32)
    y_ref[...] = y_bm.astype(y_ref.dtype).reshape(y_ref.shape)
    ore_ref[...] = sre[...]
    oim_ref[...] = sim[...]


def _s5(u, perm, permt, bblk, cblk, lre, lim, dvec, h0re, h0im, *, batch, seqlen, tl):
    bt = S5_SEQ_TILE
    m = bt * tl
    has_h0 = h0re is not None
    nb = batch // bt
    nt = seqlen // tl
    if u.ndim == 3:
        u_spec = pl.BlockSpec((bt, tl, D_MODEL), lambda b, t: (b, t, 0))
    else:
        u_spec = pl.BlockSpec((m, D_MODEL), lambda b, t: (b * nt + t, 0))
    st_spec = pl.BlockSpec((bt, S5_FLAT), lambda b, t: (b, 0))
    consts = [perm, permt, bblk, cblk, lre, lim, dvec]
    in_specs = [u_spec] + [_const_spec(c.shape) for c in consts]
    args = [u] + consts
    if has_h0:
        in_specs += [st_spec, st_spec]
        args += [h0re, h0im]
    return pl.pallas_call(
        functools.partial(_s5_kernel, tl=tl, has_h0=has_h0),
        grid=(nb, nt),
        in_specs=in_specs,
        out_specs=[u_spec, st_spec, st_spec],
        out_shape=[jax.ShapeDtypeStruct(u.shape, BF),
                   jax.ShapeDtypeStruct((batch, S5_FLAT), F32),
                   jax.ShapeDtypeStruct((batch, S5_FLAT), F32)],
        scratch_shapes=[pltpu.VMEM((m, S5_FLAT), F32), pltpu.VMEM((m, S5_FLAT), F32),
                        pltpu.VMEM((bt, S5_FLAT), F32), pltpu.VMEM((bt, S5_FLAT), F32)],
        compiler_params=_params(2),
        name="s5_sample" if has_h0 else "s5_prompt",
    )(*args)


def _mixout_kernel(y_ref, z_ref, y5_ref, ga_ref, gb_ref, x_ref, ng_ref, wa_ref, wglu_ref, bglu_ref, wmix_ref, o_ref):
    y = y_ref[...].astype(F32) * _silu(z_ref[...].astype(F32))
    yn = _rms(y, ng_ref[...]).astype(BF)
    branch_a = _dot(yn, wa_ref[...])
    glu = _dot(y5_ref[...], wglu_ref[...]) + bglu_ref[...]
    branch_b = glu[:, :D_MODEL] * _sigmoid(glu[:, D_MODEL:])
    merged = _sigmoid(ga_ref[...].astype(F32)) * branch_a + _sigmoid(gb_ref[...].astype(F32)) * branch_b
    o_ref[...] = x_ref[...] + _dot(merged.astype(BF), wmix_ref[...])


def _mixout(y, z, y5, ga, gb, x2d, ng, wa, wglu, bglu, wmix):
    t = x2d.shape[0]
    tm = TOKEN_TILE
    row = pl.BlockSpec((tm, D_MODEL), lambda i: (i, 0))
    consts = [ng, wa, wglu, bglu, wmix]
    return pl.pallas_call(
        _mixout_kernel,
        grid=(t // tm,),
        in_specs=[row] * 6 + [_const_spec(c.shape) for c in consts],
        out_specs=row,
        out_shape=jax.ShapeDtypeStruct((t, D_MODEL), F32),
        compiler_params=_params(1),
        name="mixout",
    )(y, z, y5, ga, gb, x2d, *consts)


def _attn_kernel(h_ref, g_ref, wq_ref, wo_ref, k_ref, v_ref, o_ref, obuf, *, nkv, rows_per_seq):
    h = h_ref[...]
    r = h.shape[0]
    qv = _dot(_rms(h, g_ref[...]).astype(BF), wq_ref[...]).astype(BF)
    heads_split = len(k_ref.shape) == 4

    def head_of(ref, hd):
        if heads_split:
            parts = [ref[i, :, hd, :] for i in range(nkv)]
            return (parts[0] if nkv == 1 else jnp.concatenate(parts, axis=0)).astype(BF)
        return ref[0, :, hd * MEM_HEAD_DIM:(hd + 1) * MEM_HEAD_DIM].astype(BF)

    if nkv > 1:
        rs = lax.shift_right_logical(lax.broadcasted_iota(I32, (r, nkv * MEM_LEN), 0), int(math.log2(rows_per_seq)))
        cq = lax.shift_right_logical(lax.broadcasted_iota(I32, (r, nkv * MEM_LEN), 1), int(math.log2(MEM_LEN)))
        visible = rs == cq
    scale = MEM_HEAD_DIM ** -0.5
    for hd in range(MEM_HEADS):
        sl = slice(hd * MEM_HEAD_DIM, (hd + 1) * MEM_HEAD_DIM)
        s = _dot_nt(qv[:, sl], head_of(k_ref, hd)) * scale
        if nkv > 1:
            s = jnp.where(visible, s, NEG_BIG)
        p = jnp.exp(s - jnp.max(s, axis=-1, keepdims=True))
        p = p / jnp.sum(p, axis=-1, keepdims=True)
        obuf[:, sl] = _dot(p.astype(BF), head_of(v_ref, hd))
    o_ref[...] = h + _dot(obuf[...].astype(BF), wo_ref[...])


def _attn(h2d, g, wq, wo, k3, v3, *, rows, nkv, rows_per_seq):
    t = h2d.shape[0]
    nsteps = t // rows
    steps_per_kv = nsteps // (k3.shape[0] // nkv)
    row = pl.BlockSpec((rows, D_MODEL), lambda i: (i, 0))
    if k3.ndim == 4:
        kv = pl.BlockSpec((nkv, MEM_LEN, MEM_HEADS, MEM_HEAD_DIM), lambda i: (i // steps_per_kv, 0, 0, 0))
    else:
        kv = pl.BlockSpec((nkv, MEM_LEN, D_MODEL), lambda i: (i // steps_per_kv, 0, 0))
    consts = [g, wq, wo]
    return pl.pallas_call(
        functools.partial(_attn_kernel, nkv=nkv, rows_per_seq=rows_per_seq),
        grid=(nsteps,),
        in_specs=[row] + [_const_spec(c.shape) for c in consts] + [kv, kv],
        out_specs=row,
        out_shape=jax.ShapeDtypeStruct((t, D_MODEL), F32),
        scratch_shapes=[pltpu.VMEM((rows, D_MODEL), F32)],
        compiler_params=_params(1),
        name="attn",
    )(h2d, *consts, k3, v3)


ROUTER_LANE0 = MOE_GROUPS


def _router_kernel(hp_ref, hs_ref, g_ref, wh_ref, wm_ref, wl_ref, b_ref, tri_ref, idx_ref, wt_ref, cnt_ref, carry,
                   *, np_steps):
    i = pl.program_id(0)

    @pl.when(i == 0)
    def _():
        carry[...] = jnp.zeros_like(carry)

    h = jnp.where(i < np_steps, hp_ref[...], hs_ref[...])
    xn = _rms(h, g_ref[...])
    xh, xm, xl = _split3(xn)
    wh, wm, wl = wh_ref[...], wm_ref[...], wl_ref[...]
    logits = (_dot(xh, wh) + _dot(xh, wm) + _dot(xm, wh) + _dot(xh, wl) + _dot(xm, wm) + _dot(xl, wh)) + b_ref[...]
    lane = lax.broadcasted_iota(I32, logits.shape, 1)
    is_g = lane < MOE_GROUPS
    lg = jnp.where(is_g, logits, NEG_BIG)
    mg = jnp.max(lg, axis=-1, keepdims=True)
    gidx = jnp.min(jnp.where(lg == mg, lane, LANES), axis=-1, keepdims=True)
    g_w = 1.0 / jnp.sum(jnp.where(is_g, jnp.exp(lg - mg), 0.0), axis=-1, keepdims=True)
    e_lane = lane - ROUTER_LANE0
    in_grp = ((e_lane >= 0) & (e_lane < MOE_EXPERTS)
              & (lax.shift_right_logical(jnp.maximum(e_lane, 0), 3) == gidx))
    le = jnp.where(in_grp, logits, NEG_BIG)
    m1 = jnp.max(le, axis=-1, keepdims=True)
    i1 = jnp.min(jnp.where(le == m1, lane, LANES), axis=-1, keepdims=True)
    le2 = jnp.where(lane == i1, NEG_BIG, le)
    m2 = jnp.max(le2, axis=-1, keepdims=True)
    i2 = jnp.min(jnp.where(le2 == m2, lane, LANES), axis=-1, keepdims=True)
    ratio = jnp.exp(m2 - m1)
    w1 = g_w / (1.0 + ratio)
    w2 = g_w * ratio / (1.0 + ratio)
    wt_ref[...] = jnp.where(lane == 0, w1, jnp.where(lane == 1, w2, 0.0))

    oh1 = jnp.where(lane == i1, 1.0, 0.0)
    oh2 = jnp.where(lane == i2, 1.0, 0.0)
    tri = tri_ref[...]
    c0 = carry[...]
    tot1 = jnp.sum(oh1, axis=0, keepdims=True)
    r1 = jnp.sum(oh1 * (_dot(tri, oh1.astype(BF)) + c0), axis=-1, keepdims=True)
    r2 = jnp.sum(oh2 * (_dot(tri, oh2.astype(BF)) + (c0 + tot1)), axis=-1, keepdims=True)
    c1 = c0 + tot1 + jnp.sum(oh2, axis=0, keepdims=True)
    carry[...] = c1
    idx_ref[...] = jnp.where(lane == 0, i1 - ROUTER_LANE0,
                             jnp.where(lane == 1, i2 - ROUTER_LANE0,
                                       jnp.where(lane == 2, r1.astype(I32), jnp.where(lane == 3, r2.astype(I32), 0))))
    cnt_ref[...] = jnp.broadcast_to(c1, cnt_ref.shape).astype(I32)


def _router(hp, hs, g, wh, wm, wl, b):
    tm = TOKEN_TILE
    np_steps = hp.shape[0] // tm
    ns_steps = hs.shape[0] // tm
    t = hp.shape[0] + hs.shape[0]
    tri = jnp.asarray(np.tril(np.ones((tm, tm), np.float32), -1)).astype(BF)
    consts = [g, wh, wm, wl, b, tri]
    row = lambda w: pl.BlockSpec((tm, w), lambda i: (i, 0))
    return pl.pallas_call(
        functools.partial(_router_kernel, np_steps=np_steps),
        grid=(np_steps + ns_steps,),
        in_specs=[pl.BlockSpec((tm, D_MODEL), lambda i: (jnp.minimum(i, np_steps - 1), 0)),
                  pl.BlockSpec((tm, D_MODEL), lambda i: (jnp.maximum(i - np_steps, 0), 0))]
                 + [_const_spec(c.shape) for c in consts],
        out_specs=[row(LANES), row(LANES), _const_spec((8, LANES))],
        out_shape=[jax.ShapeDtypeStruct((t, LANES), I32), jax.ShapeDtypeStruct((t, LANES), F32),
                   jax.ShapeDtypeStruct((8, LANES), I32)],
        scratch_shapes=[pltpu.VMEM((1, LANES), F32)],
        compiler_params=_params(1),
        name="router",
    )(hp, hs, *consts)


SLABS = D_MODEL // LANES


def _token_rows(t):
    return pl.ds(pl.multiple_of(t * SLABS, SLABS), SLABS)


def _to_token_major(ref, x, base=0):
    tm = x.shape[0]
    for s in range(SLABS):
        ref[pl.ds(base + s, tm, stride=SLABS), :] = x[:, s * LANES:(s + 1) * LANES]


def _from_token_major(ref, tm, base=0):
    return jnp.concatenate([ref[pl.ds(base + s, tm, stride=SLABS), :] for s in range(SLABS)], axis=1)


def _scatter_kernel(ve_ref, pe_ref, hp_ref, hs_ref, g_ref, pos_ref, xs_ref, xbuf, zbuf, sem, zsem, *, np_steps):
    i = pl.program_id(0)
    tm = TOKEN_TILE

    tile_rows = MOE_ROW_TILE * SLABS
    n_tiles = xs_ref.shape[0] // tile_rows

    def pad_copy(r):
        return pltpu.make_async_copy(zbuf.at[pl.ds(0, SLABS)], xs_ref.at[_token_rows(r)], zsem)

    def tile_copy(j):
        return pltpu.make_async_copy(zbuf, xs_ref.at[pl.ds(pl.multiple_of(j * tile_rows, tile_rows), tile_rows)], zsem)

    @pl.when(i == 0)
    def _():
        zbuf[...] = jnp.zeros_like(zbuf)
        first_unused = pe_ref[MOE_EXPERTS - 1] // MOE_ROW_TILE

        def tile_start(j, c):
            tile_copy(j).start()
            return c

        def tile_wait(j, c):
            tile_copy(0).wait()
            return c

        lax.fori_loop(first_unused, n_tiles, tile_start, 0)
        lax.fori_loop(first_unused, n_tiles, tile_wait, 0)

        def per_expert(e, n):
            lo = ve_ref[e]
            hi = pe_ref[e]

            def body(r, c):
                pad_copy(r).start()
                return c

            lax.fori_loop(lo, hi, body, 0)
            return n + (hi - lo)

        n_pad = lax.fori_loop(0, MOE_EXPERTS, per_expert, 0)

        def pad_wait(r, c):
            pad_copy(0).wait()
            return c

        lax.fori_loop(0, n_pad, pad_wait, 0)

    h = jnp.where(i < np_steps, hp_ref[...], hs_ref[...])
    _to_token_major(xbuf, _rms(h, g_ref[...]))

    def row_copy(t, dst_row):
        return pltpu.make_async_copy(xbuf.at[_token_rows(t)], xs_ref.at[_token_rows(dst_row)], sem)

    def issue(t, c):
        row_copy(t, pos_ref[0, 0, t]).start()
        row_copy(t, pos_ref[0, 0, tm + t]).start()
        return c

    def wait(t, c):
        row_copy(0, 0).wait()
        return c

    lax.fori_loop(0, tm, issue, 0, unroll=8)
    lax.fori_loop(0, 2 * tm, wait, 0, unroll=16)


def _scatter(hp, hs, g, pos_tiles, valid_end, pend, n_rows):
    tm = TOKEN_TILE
    np_steps = hp.shape[0] // tm
    ns_steps = hs.shape[0] // tm
    grid_spec = pltpu.PrefetchScalarGridSpec(
        num_scalar_prefetch=2,
        grid=(np_steps + ns_steps,),
        in_specs=[pl.BlockSpec((tm, D_MODEL), lambda i, ve, pe: (jnp.minimum(i, np_steps - 1), 0)),
                  pl.BlockSpec((tm, D_MODEL), lambda i, ve, pe: (jnp.maximum(i - np_steps, 0), 0)),
                  pl.BlockSpec(g.shape, lambda i, ve, pe: (0, 0)),
                  pl.BlockSpec((1, 1, 2 * tm), lambda i, ve, pe: (i, 0, 0), memory_space=pltpu.SMEM)],
        out_specs=pl.BlockSpec(memory_space=pl.ANY),
        scratch_shapes=[pltpu.VMEM((tm * SLABS, LANES), F32), pltpu.VMEM((MOE_ROW_TILE * SLABS, LANES), F32),
                        pltpu.SemaphoreType.DMA(()), pltpu.SemaphoreType.DMA(())],
    )
    return pl.pallas_call(
        functools.partial(_scatter_kernel, np_steps=np_steps),
        grid_spec=grid_spec,
        out_shape=jax.ShapeDtypeStruct((n_rows * SLABS, LANES), F32),
        compiler_params=_params(1),
        name="scatter",
    )(valid_end, pend, hp, hs, g, pos_tiles)


def _expert_kernel(te_ref, nu_ref, x_ref, wg_ref, wu_ref, wd_ref, y_ref):
    i = pl.program_id(0)
    tm = MOE_ROW_TILE

    @pl.when(i < nu_ref[0])
    def _():
        x = _from_token_major(x_ref, tm).astype(BF)
        gate = _dot(x, wg_ref[0].astype(BF))
        up = _dot(x, wu_ref[0].astype(BF))
        _to_token_major(y_ref, _dot((_silu(gate) * up).astype(BF), wd_ref[0].astype(BF)))

    @pl.when(i >= nu_ref[0])
    def _():
        y_ref[...] = jnp.zeros_like(y_ref)


def _experts(x_sorted, tile_expert, n_used, wg, wu, wd):
    tm = MOE_ROW_TILE
    n_tiles = x_sorted.shape[0] // (tm * SLABS)
    used = lambda i, nu: jnp.minimum(i, nu[0] - 1)
    grid_spec = pltpu.PrefetchScalarGridSpec(
        num_scalar_prefetch=2,
        grid=(n_tiles,),
        in_specs=[pl.BlockSpec((tm * SLABS, LANES), lambda i, te, nu: (used(i, nu), 0)),
                  pl.BlockSpec((1, D_MODEL, MOE_D_FF), lambda i, te, nu: (te[used(i, nu)], 0, 0)),
                  pl.BlockSpec((1, D_MODEL, MOE_D_FF), lambda i, te, nu: (te[used(i, nu)], 0, 0)),
                  pl.BlockSpec((1, MOE_D_FF, D_MODEL), lambda i, te, nu: (te[used(i, nu)], 0, 0))],
        out_specs=pl.BlockSpec((tm * SLABS, LANES), lambda i, te, nu: (i, 0)),
    )
    return pl.pallas_call(
        _expert_kernel,
        grid_spec=grid_spec,
        out_shape=jax.ShapeDtypeStruct(x_sorted.shape, F32),
        compiler_params=_params(1),
        name="experts",
    )(tile_expert, n_used, x_sorted, wg, wu, wd)


def _combine_kernel(h_ref, wt_ref, g_ref, pcur_ref, pnext_ref, ys_ref, o_ref, ybuf, sem, *, n):
    i = pl.program_id(0)
    tm = TOKEN_TILE

    def row_copy(pos_ref, j, slot):
        return pltpu.make_async_copy(ys_ref.at[_token_rows(pos_ref[0, 0, j])], ybuf.at[slot, _token_rows(j)],
                                     sem.at[slot])

    def issue(pos_ref, slot):
        def body(j, c):
            row_copy(pos_ref, j, slot).start()
            return c

        lax.fori_loop(0, 2 * tm, body, 0, unroll=8)

    def finish(slot):
        def wait(j, c):
            row_copy(pcur_ref, 0, slot).wait()
            return c

        lax.fori_loop(0, 2 * tm, wait, 0, unroll=16)
        wt = wt_ref[...]
        y1 = _from_token_major(ybuf.at[slot], tm)
        y2 = _from_token_major(ybuf.at[slot], tm, base=tm * SLABS)
        o_ref[...] = _rms(h_ref[...] + wt[:, 0:1] * y1 + wt[:, 1:2] * y2, g_ref[...])

    @pl.when(i == 0)
    def _():
        issue(pcur_ref, 0)

    for slot in (0, 1):
        @pl.when(lax.rem(i, 2) == slot)
        def _(slot=slot):
            @pl.when(i + 1 < n)
            def _():
                issue(pnext_ref, 1 - slot)

            finish(slot)


def _combine(h2d, wt, g, pos_tiles, y_sorted, *, tile_offset):
    t = h2d.shape[0]
    tm = TOKEN_TILE
    nsteps = t // tm
    pos_spec = lambda f: pl.BlockSpec((1, 1, 2 * tm), f, memory_space=pltpu.SMEM)
    return pl.pallas_call(
        functools.partial(_combine_kernel, n=nsteps),
        grid=(nsteps,),
        in_specs=[pl.BlockSpec((tm, D_MODEL), lambda i: (i, 0)),
                  pl.BlockSpec((tm, LANES), lambda i: (i + tile_offset, 0)),
                  _const_spec(g.shape),
                  pos_spec(lambda i: (i + tile_offset, 0, 0)),
                  pos_spec(lambda i: (jnp.minimum(i + 1, nsteps - 1) + tile_offset, 0, 0)),
                  pl.BlockSpec(memory_space=pl.ANY)],
        out_specs=pl.BlockSpec((tm, D_MODEL), lambda i: (i, 0)),
        out_shape=jax.ShapeDtypeStruct((t, D_MODEL), F32),
        scratch_shapes=[pltpu.VMEM((2, 2 * tm * SLABS, LANES), F32), pltpu.SemaphoreType.DMA((2,))],
        compiler_params=_params(1),
        name="combine",
    )(h2d, wt, g, pos_tiles, pos_tiles, y_sorted)


def _moe_plan(idx_slab, cnt):
    t = idx_slab.shape[0]
    tm = TOKEN_TILE
    rt = MOE_ROW_TILE
    n_tiles = (2 * t + MOE_EXPERTS * (rt - 1) + rt - 1) // rt
    counts = cnt[0, ROUTER_LANE0:ROUTER_LANE0 + MOE_EXPERTS]
    padded = ((counts + rt - 1) // rt) * rt
    pend = jnp.cumsum(padded).astype(I32)
    pstart = pend - padded
    pos1 = pstart[idx_slab[:, 0]] + idx_slab[:, 2]
    pos2 = pstart[idx_slab[:, 1]] + idx_slab[:, 3]
    pos_tiles = jnp.concatenate([pos1.reshape(t // tm, 1, tm), pos2.reshape(t // tm, 1, tm)], axis=2).astype(I32)
    valid_end = (pstart + counts).astype(I32)
    tile_start = jnp.arange(n_tiles, dtype=I32) * rt
    tile_expert = jnp.minimum(jnp.searchsorted(pend, tile_start, side="right"), MOE_EXPERTS - 1).astype(I32)
    n_used = (pend[-1] // rt).astype(I32).reshape(1)
    return pos_tiles, valid_end, pend, tile_expert, n_used, n_tiles * rt


def _perm_matrix(bt, tl):
    m = bt * tl
    p = np.zeros((m, m), np.float32)
    for l in range(tl):
        for b in range(bt):
            p[l * bt + b, b * tl + l] = 1.0
    return p


def _s5_operators(lam_re, lam_im, log_step, b_re, b_im, c_re, c_im):
    lam = lax.complex(lam_re, lam_im)
    step = jnp.exp(log_step)[:, None]
    lam_bar = jnp.exp(lam * step)
    b_bar = ((lam_bar - 1.0) / lam)[..., None] * lax.complex(b_re, b_im)
    gl = LANES // S5_GROUP
    eye = jnp.eye(gl, dtype=F32)

    def in_block(bpart):
        bp = bpart.reshape(S5_SLABS, gl, S5_STATE, S5_GROUP)
        blk = jnp.einsum("sgnk,gh->sgkhn", bp, eye)
        return blk.reshape(S5_SLABS, LANES, gl * S5_STATE)

    def out_block(cpart):
        cp = cpart.reshape(S5_SLABS, gl, S5_GROUP, S5_STATE)
        blk = jnp.einsum("sgkn,gh->sgnhk", cp, eye)
        return blk.reshape(S5_SLABS, gl * S5_STATE, LANES)

    bblk = jnp.concatenate([in_block(jnp.real(b_bar)), in_block(jnp.imag(b_bar))], axis=2).astype(BF)
    cblk = jnp.concatenate([out_block(c_re), out_block(-c_im)], axis=1).astype(BF)
    lre = jnp.broadcast_to(jnp.real(lam_bar).reshape(1, S5_FLAT), (S5_SEQ_TILE, S5_FLAT))
    lim = jnp.broadcast_to(jnp.imag(lam_bar).reshape(1, S5_FLAT), (S5_SEQ_TILE, S5_FLAT))
    return bblk, cblk, lre, lim


def _pad_lanes(v, width=LANES):
    return jnp.pad(v, [(0, 0)] * (v.ndim - 1) + [(0, width - v.shape[-1])])


def kernel(x_prompt, x_sample, state_conv, state_ssd, state_s5_re, state_s5_im, cache_mem_k, cache_mem_v, mem_prompt, norm_mix_g, w_in, conv_w, conv_b, ssd_dt_bias, ssd_a_log, ssd_d, ssd_norm_g, w_ssd_branch, s5_lambda_re, s5_lambda_im, s5_log_step, s5_b_re, s5_b_im, s5_c_re, s5_c_im, s5_d, w_glu, b_glu, w_mix_out, norm_mem_q_g, norm_mem_kv_g, w_mem_q, w_mem_k, w_mem_v, w_mem_o, norm_ffn_g, w_router_group, b_router_group, w_router_expert, b_router_expert, w_exp_gate, w_exp_up, w_exp_down, norm_final_g):
    bp, lp, _ = x_prompt.shape
    bs, ls, _ = x_sample.shape
    tp, ts = bp * lp, bs * ls
    row1 = lambda v: v.reshape(1, -1).astype(F32)

    w_in0 = w_in[0]
    o_xbc = D_MODEL
    o_dt = o_xbc + SSD_CONV_DIM
    o_u5 = o_dt + SSD_HEADS
    w_main = jnp.concatenate([w_in0[:, :o_dt], w_in0[:, o_u5:]], axis=1).astype(BF)
    w_dt = _pad_lanes(w_in0[:, o_dt:o_u5]).astype(BF)
    dt_bias = _pad_lanes(row1(ssd_dt_bias[0]))
    a_log = _pad_lanes(row1(ssd_a_log[0]))
    dskip = jnp.repeat(ssd_d[0].astype(F32), SSD_HEADDIM).reshape(1, D_MODEL)
    eexp_np = np.zeros((LANES, D_MODEL), np.float32)
    for h in range(SSD_HEADS):
        eexp_np[h, h * SSD_HEADDIM:(h + 1) * SSD_HEADDIM] = 1.0
    eexp = jnp.asarray(eexp_np).astype(BF)
    eye16 = jnp.asarray(np.eye(SSD_HEADS, LANES, dtype=np.float32)).astype(BF)
    bblk, cblk, lre, lim = _s5_operators(s5_lambda_re[0], s5_lambda_im[0], s5_log_step[0], s5_b_re[0],
                                         s5_b_im[0], s5_c_re[0], s5_c_im[0])
    s5d = row1(s5_d[0])
    w_kv = jnp.concatenate([w_mem_k[0], w_mem_v[0]], axis=1).astype(BF)
    w_r = _pad_lanes(jnp.concatenate([w_router_group[0], w_router_expert[0]], axis=1).astype(F32))
    wr_h, wr_m, wr_l = _split3(w_r)
    b_r = _pad_lanes(row1(jnp.concatenate([b_router_group[0], b_router_expert[0]])))
    wg = w_exp_gate[0].reshape(MOE_EXPERTS, D_MODEL, MOE_D_FF)
    wu = w_exp_up[0].reshape(MOE_EXPERTS, D_MODEL, MOE_D_FF)
    wd = w_exp_down[0].reshape(MOE_EXPERTS, MOE_D_FF, D_MODEL)

    def mixer(x2d, batch, seqlen, conv0, h0, s5re0, s5im0):
        z, xbc, u5, ga, gb, dt = _inproj(x2d, row1(norm_mix_g[0]), w_main, w_dt, dt_bias)
        y_ssd, new_conv, new_ssd = _ssd(xbc, dt, conv_w[0].astype(F32), row1(conv_b[0]), a_log, dskip, eexp, eye16,
                                        conv0, h0, batch=batch, seqlen=seqlen)
        if h0 is None:
            tl = S5_TIME_TILE
            u_in = u5.reshape(batch, seqlen, D_MODEL)
        else:
            tl = seqlen
            u_in = u5
        perm = _perm_matrix(S5_SEQ_TILE, tl)
        y5, new_re, new_im = _s5(u_in, jnp.asarray(perm).astype(BF), jnp.asarray(perm.T).astype(BF), bblk, cblk,
                                 lre, lim, s5d, s5re0, s5im0, batch=batch, seqlen=seqlen, tl=tl)
        h1 = _mixout(y_ssd, z, y5.reshape(-1, D_MODEL), ga, gb, x2d, row1(ssd_norm_g[0]),
                     w_ssd_branch[0].astype(BF), w_glu[0].astype(BF), row1(b_glu[0]), w_mix_out[0].astype(BF))
        return h1, new_conv, new_ssd, new_re, new_im

    xp2 = x_prompt.reshape(tp, D_MODEL)
    xs2 = x_sample.reshape(ts, D_MODEL)
    h1p, conv_p, ssd_p, re_p, im_p = mixer(xp2, bp, lp, None, None, None, None)
    h1s, conv_s, ssd_s, re_s, im_s = mixer(xs2, bs, ls, state_conv[0], state_ssd[0],
                                           state_s5_re[0].reshape(bs, S5_FLAT), state_s5_im[0].reshape(bs, S5_FLAT))

    mk_p, mv_p = _memkv(mem_prompt.reshape(bp * MEM_LEN, D_MODEL), row1(norm_mem_kv_g[0]), w_kv)
    gq = row1(norm_mem_q_g[0])
    wq = w_mem_q[0].astype(BF)
    wo = w_mem_o[0].astype(BF)
    h2p = _attn(h1p, gq, wq, wo, mk_p.reshape(bp, MEM_LEN, D_MODEL), mv_p.reshape(bp, MEM_LEN, D_MODEL),
                rows=TOKEN_TILE, nkv=1, rows_per_seq=lp)
    sample_seqs = 4
    h2s = _attn(h1s, gq, wq, wo, cache_mem_k[0], cache_mem_v[0], rows=sample_seqs * ls, nkv=sample_seqs,
                rows_per_seq=ls)

    g_ffn = row1(norm_ffn_g[0])
    r_idx, r_wt, r_cnt = _router(h2p, h2s, g_ffn, wr_h, wr_m, wr_l, b_r)
    pos_tiles, valid_end, pend, tile_expert, n_used, n_rows = _moe_plan(r_idx, r_cnt)
    x_sorted = _scatter(h2p, h2s, g_ffn, pos_tiles, valid_end, pend, n_rows)
    y_sorted = _experts(x_sorted, tile_expert, n_used, wg, wu, wd)
    gf = row1(norm_final_g)
    y_prompt = _combine(h2p, r_wt, gf, pos_tiles, y_sorted, tile_offset=0)
    y_sample = _combine(h2s, r_wt, gf, pos_tiles, y_sorted, tile_offset=tp // TOKEN_TILE)

    return (y_prompt.reshape(bp, lp, D_MODEL), y_sample.reshape(bs, ls, D_MODEL),
            conv_p[None], ssd_p[None],
            re_p.reshape(1, bp, S5_GROUPS, S5_STATE), im_p.reshape(1, bp, S5_GROUPS, S5_STATE),
            mk_p.reshape(1, bp, MEM_LEN, MEM_HEADS, MEM_HEAD_DIM), mv_p.reshape(1, bp, MEM_LEN, MEM_HEADS, MEM_HEAD_DIM),
            conv_s[None], ssd_s[None],
            re_s.reshape(1, bs, S5_GROUPS, S5_STATE), im_s.reshape(1, bs, S5_GROUPS, S5_STATE))
```

```python
import functools
import math

import numpy as np
import jax
import jax.numpy as jnp
from jax import lax
from jax.experimental import pallas as pl
from jax.experimental.pallas import tpu as pltpu

F32 = jnp.float32
BF = jnp.bfloat16
I32 = jnp.int32

D_MODEL = 1024
SSD_HEADS = 16
SSD_HEADDIM = 64
SSD_GROUPS = 2
SSD_STATE = 128
SSD_CONV = 4
SSD_CONV_DIM = 1536
HEADS_PER_GROUP = SSD_HEADS // SSD_GROUPS
S5_GROUPS = 64
S5_GROUP = 16
S5_STATE = 64
S5_FLAT = S5_GROUPS * S5_STATE
S5_SLABS = D_MODEL // 128
MEM_LEN = 256
MEM_HEADS = 4
MEM_HEAD_DIM = 256
MOE_GROUPS = 4
MOE_EXPERTS_PER_GROUP = 8
MOE_EXPERTS = MOE_GROUPS * MOE_EXPERTS_PER_GROUP
MOE_D_FF = 512
NORM_EPS = 1e-6

LANES = 128
VMEM_LIMIT_BYTES = 56 * 1024 * 1024
TOKEN_TILE = 512
SSD_CHUNK = 128
S5_TIME_TILE = 32
MOE_ROW_TILE = 256
NEG_BIG = -1e30


def _params(n_axes):
    return pltpu.CompilerParams(dimension_semantics=("arbitrary",) * n_axes,
                                vmem_limit_bytes=VMEM_LIMIT_BYTES)


def _const_spec(shape):
    nd = len(shape)
    return pl.BlockSpec(shape, lambda *_: (0,) * nd)


def _dot(a, b):
    return jnp.dot(a, b, preferred_element_type=F32)


def _dot_nt(a, b):
    return lax.dot_general(a, b, (((1,), (1,)), ((), ())), preferred_element_type=F32)


def _split3(v):
    h = v.astype(BF)
    r = v - h.astype(F32)
    m = r.astype(BF)
    l = (r - m.astype(F32)).astype(BF)
    return h, m, l


def _dot_exact(a_bf, v, nt=False):
    f = _dot_nt if nt else _dot
    h, m, l = _split3(v)
    return f(a_bf, h) + f(a_bf, m) + f(a_bf, l)


def _rms(x, g):
    return x * lax.rsqrt(jnp.mean(x * x, axis=-1, keepdims=True) + NORM_EPS) * g


def _sigmoid(x):
    return 1.0 / (1.0 + jnp.exp(-x))


def _silu(x):
    return x * _sigmoid(x)


def _proj_cols(xb, w_ref, o_ref, c0, width):
    for c in range(0, width, 512):
        ce = min(c + 512, width)
        o_ref[:, c:ce] = _dot(xb, w_ref[:, c0 + c:c0 + ce]).astype(o_ref.dtype)


def _inproj_kernel(x_ref, g_ref, w_ref, wdt_ref, dtb_ref, z_ref, xbc_ref, u5_ref, ga_ref, gb_ref, dt_ref):
    xb = _rms(x_ref[...], g_ref[...]).astype(BF)
    c0 = 0
    for o_ref in (z_ref, xbc_ref, u5_ref, ga_ref, gb_ref):
        width = o_ref.shape[1]
        _proj_cols(xb, w_ref, o_ref, c0, width)
        c0 += width
    raw = _dot(xb, wdt_ref[...]) + dtb_ref[...]
    dt_ref[...] = jnp.maximum(raw, 0.0) + jnp.log1p(jnp.exp(-jnp.abs(raw)))


def _inproj(x2d, g, w_main, w_dt, dt_bias):
    t = x2d.shape[0]
    tm = TOKEN_TILE
    widths = (D_MODEL, SSD_CONV_DIM, D_MODEL, D_MODEL, D_MODEL)
    row = lambda w: pl.BlockSpec((tm, w), lambda i: (i, 0))
    return pl.pallas_call(
        _inproj_kernel,
        grid=(t // tm,),
        in_specs=[row(D_MODEL), _const_spec(g.shape), _const_spec(w_main.shape), _const_spec(w_dt.shape),
                  _const_spec(dt_bias.shape)],
        out_specs=[row(w) for w in widths] + [row(LANES)],
        out_shape=[jax.ShapeDtypeStruct((t, w), BF) for w in widths] + [jax.ShapeDtypeStruct((t, LANES), F32)],
        compiler_params=_params(1),
        name="inproj",
    )(x2d, g, w_main, w_dt, dt_bias)


def _memkv_kernel(x_ref, g_ref, w_ref, k_ref, v_ref):
    xb = _rms(x_ref[...], g_ref[...]).astype(BF)
    _proj_cols(xb, w_ref, k_ref, 0, D_MODEL)
    _proj_cols(xb, w_ref, v_ref, D_MODEL, D_MODEL)


def _memkv(mem2d, g, w_kv):
    t = mem2d.shape[0]
    tm = TOKEN_TILE
    row = pl.BlockSpec((tm, D_MODEL), lambda i: (i, 0))
    return pl.pallas_call(
        _memkv_kernel,
        grid=(t // tm,),
        in_specs=[row, _const_spec(g.shape), _const_spec(w_kv.shape)],
        out_specs=[row, row],
        out_shape=[jax.ShapeDtypeStruct((t, D_MODEL), F32)] * 2,
        compiler_params=_params(1),
        name="memkv",
    )(mem2d, g, w_kv)


def _ssd_kernel(*refs, lq, has_h0):
    q = SSD_CHUNK
    nseq = q // lq
    if has_h0:
        (xbc_ref, dt_ref, cw_ref, cb_ref, alog_ref, dsk_ref, eexp_ref, eye_ref, shift_ref, conv0_ref, h0_ref,
         y_ref, convo_ref, ho_ref, hbuf, cacc, yacc, yint, xwt, tot_s) = refs
        hin_ref = h0_ref
        hbuf[...] = jnp.zeros_like(hbuf)
        hbuf[:, 0:SSD_CONV - 1, :] = conv0_ref[...]
    else:
        (xbc_ref, dt_ref, cw_ref, cb_ref, alog_ref, dsk_ref, eexp_ref, eye_ref, shift_ref,
         y_ref, convo_ref, ho_ref, hbuf, cacc, yacc, yint, xwt, tot_s) = refs
        hin_ref = ho_ref

        @pl.when(pl.program_id(1) == 0)
        def _():
            ho_ref[...] = jnp.zeros_like(ho_ref)
            hbuf[...] = jnp.zeros_like(hbuf)

    x16 = xbc_ref[...]
    xraw = x16.astype(F32)
    acc = cb_ref[...] + xraw * cw_ref[SSD_CONV - 1:SSD_CONV, :]
    for k in range(SSD_CONV - 1):
        acc = acc + _dot(shift_ref[k], x16) * cw_ref[k:k + 1, :]
    cacc[...] = acc
    for i in range(nseq):
        tail = xraw[(i + 1) * lq - (SSD_CONV - 1):(i + 1) * lq]
        convo_ref[i] = tail
        corr = hbuf[i, 0:8, :] * cw_ref[0:1, :]
        for k in range(1, SSD_CONV - 1):
            corr = corr + hbuf[i, k:k + 8, :] * cw_ref[k:k + 1, :]
        cacc[i * lq:i * lq + 8, :] += corr
        if not has_h0:
            hbuf[i, 0:SSD_CONV - 1, :] = tail
    xc = _silu(cacc[...])
    xs = xc[:, :D_MODEL]
    xs_bf = xs.astype(BF)
    bm_bf = xc[:, D_MODEL:D_MODEL + SSD_GROUPS * SSD_STATE].astype(BF)
    cm = xc[:, D_MODEL + SSD_GROUPS * SSD_STATE:]
    cm_bf = cm.astype(BF)

    dt = dt_ref[...]
    da = dt * (-jnp.exp(alog_ref[...]))
    ri = lax.broadcasted_iota(I32, (q, q), 0)
    ci = lax.broadcasted_iota(I32, (q, q), 1)
    if nseq == 1:
        causal = ci <= ri
    else:
        sh = int(math.log2(lq))
        same = lax.shift_right_logical(ri, sh) == lax.shift_right_logical(ci, sh)
        causal = same & (ci <= ri)
    lmat = jnp.where(causal, 1.0, 0.0).astype(BF)
    cs = _dot_exact(lmat, da)
    if nseq == 1:
        tot = jnp.broadcast_to(cs[q - 1:q, :], (q, LANES))
    else:
        tot = _dot_exact(jnp.where(same, 1.0, 0.0).astype(BF), da)
    tot_s[...] = tot
    eye = eye_ref[...]
    cs_t = _dot_exact(eye, cs, nt=True)
    dt_t = _dot_exact(eye, dt, nt=True)
    eexp = eexp_ref[...]
    ecs_h, ecs_m, _ = _split3(jnp.exp(cs))
    ecs_x = _dot(ecs_h, eexp) + _dot(ecs_m, eexp)
    wend_x = _dot((jnp.exp(tot - cs) * dt).astype(BF), eexp)

    lane = lax.broadcasted_iota(I32, (q, LANES), 1)
    for g in range(SSD_GROUPS):
        cbg = _dot_nt(cm_bf[:, g * SSD_STATE:(g + 1) * SSD_STATE], bm_bf[:, g * SSD_STATE:(g + 1) * SSD_STATE])
        for jp in range(HEADS_PER_GROUP // 2):
            j = g * (HEADS_PER_GROUP // 2) + jp
            ms = []
            for h in (2 * j, 2 * j + 1):
                col = jnp.broadcast_to(cs[:, h:h + 1], (q, q))
                row = jnp.broadcast_to(cs_t[h:h + 1, :], (q, q))
                dtr = jnp.broadcast_to(dt_t[h:h + 1, :], (q, q))
                ms.append((jnp.where(causal, jnp.exp(col - row), 0.0) * cbg * dtr).astype(BF))
            mp = jnp.concatenate(ms, axis=1)
            xp = xs_bf[:, j * LANES:(j + 1) * LANES]
            zero = jnp.zeros_like(xp)
            x2 = jnp.concatenate([jnp.where(lane < SSD_HEADDIM, xp, zero),
                                  jnp.where(lane >= SSD_HEADDIM, xp, zero)], axis=0)
            yacc[:, j * LANES:(j + 1) * LANES] = _dot(mp, x2)

    gw = HEADS_PER_GROUP * SSD_HEADDIM
    xw = xs * wend_x
    for g in range(SSD_GROUPS):
        xwt[g * gw:(g + 1) * gw, :] = xw[:, g * gw:(g + 1) * gw].T.astype(BF)

    if nseq == 1:
        for g in range(SSD_GROUPS):
            hin_g = hin_ref[0, g * HEADS_PER_GROUP:(g + 1) * HEADS_PER_GROUP].reshape(gw, SSD_STATE)
            yint[:, g * gw:(g + 1) * gw] = _dot_nt(cm_bf[:, g * SSD_STATE:(g + 1) * SSD_STATE], hin_g.astype(BF))
            s_new = _dot(xwt[g * gw:(g + 1) * gw, :], bm_bf[:, g * SSD_STATE:(g + 1) * SSD_STATE])
            for hl in range(HEADS_PER_GROUP):
                h = g * HEADS_PER_GROUP + hl
                dec = jnp.exp(jnp.broadcast_to(tot[0:1, h:h + 1], (SSD_HEADDIM, SSD_STATE)))
                ho_ref[0, h] = dec * hin_ref[0, h] + s_new[hl * SSD_HEADDIM:(hl + 1) * SSD_HEADDIM]
    else:
        yint[...] = jnp.zeros_like(yint)
        sh = int(math.log2(lq))
        rowseq = lax.shift_right_logical(lax.broadcasted_iota(I32, (q, SSD_STATE), 0), sh)
        colseq = lax.shift_right_logical(lax.broadcasted_iota(I32, (gw, q), 1), sh)

        def seq_body(i, carry):
            trow = tot_s[pl.ds(i * lq, 1), :]
            for g in range(SSD_GROUPS):
                hin_g = hin_ref[i, g * HEADS_PER_GROUP:(g + 1) * HEADS_PER_GROUP].reshape(gw, SSD_STATE)
                cmg = cm[:, g * SSD_STATE:(g + 1) * SSD_STATE]
                lhs = jnp.where(rowseq == i, cmg, 0.0).astype(BF)
                yint[:, g * gw:(g + 1) * gw] += _dot_nt(lhs, hin_g.astype(BF))
                xg = xwt[g * gw:(g + 1) * gw, :]
                xi = jnp.where(colseq == i, xg, jnp.zeros_like(xg))
                s_new = _dot(xi, bm_bf[:, g * SSD_STATE:(g + 1) * SSD_STATE])
                for hl in range(HEADS_PER_GROUP):
                    h = g * HEADS_PER_GROUP + hl
                    dec = jnp.exp(jnp.broadcast_to(trow[:, h:h + 1], (SSD_HEADDIM, SSD_STATE)))
                    ho_ref[i, h] = dec * hin_ref[i, h] + s_new[hl * SSD_HEADDIM:(hl + 1) * SSD_HEADDIM]
            return carry

        lax.fori_loop(0, nseq, seq_body, 0)

    y_ref[...] = (yacc[...] + yint[...] * ecs_x + dsk_ref[...] * xs).astype(y_ref.dtype)


def _ssd(xbc, dt, conv_w, conv_b, a_log, dskip, eexp, eye, conv0, h0, *, batch, seqlen):
    q = SSD_CHUNK
    t = xbc.shape[0]
    has_h0 = h0 is not None
    if has_h0:
        lq = seqlen
        nseq = q // lq
        grid = (t // q,)
        rmap = lambda i: (i, 0)
        smap3 = lambda i: (i, 0, 0)
        smap4 = lambda i: (i, 0, 0, 0)
    else:
        lq = q
        nseq = 1
        nc = seqlen // q
        grid = (batch, nc)
        rmap = lambda b, c: (b * nc + c, 0)
        smap3 = lambda b, c: (b, 0, 0)
        smap4 = lambda b, c: (b, 0, 0, 0)
    shifts = np.zeros((SSD_CONV - 1, q, q), np.float32)
    for k in range(SSD_CONV - 1):
        for r in range(q):
            if r % lq + k - (SSD_CONV - 1) >= 0:
                shifts[k, r, r + k - (SSD_CONV - 1)] = 1.0
    consts = [conv_w, conv_b, a_log, dskip, eexp, eye, jnp.asarray(shifts).astype(BF)]
    in_specs = [pl.BlockSpec((q, SSD_CONV_DIM), rmap), pl.BlockSpec((q, LANES), rmap)]
    in_specs += [_const_spec(c.shape) for c in consts]
    args = [xbc, dt] + consts
    if has_h0:
        in_specs += [pl.BlockSpec((nseq, SSD_CONV - 1, SSD_CONV_DIM), smap3),
                     pl.BlockSpec((nseq, SSD_HEADS, SSD_HEADDIM, SSD_STATE), smap4)]
        args += [conv0, h0]
    return pl.pallas_call(
        functools.partial(_ssd_kernel, lq=lq, has_h0=has_h0),
        grid=grid,
        in_specs=in_specs,
        out_specs=[pl.BlockSpec((q, D_MODEL), rmap),
                   pl.BlockSpec((nseq, SSD_CONV - 1, SSD_CONV_DIM), smap3),
                   pl.BlockSpec((nseq, SSD_HEADS, SSD_HEADDIM, SSD_STATE), smap4)],
        out_shape=[jax.ShapeDtypeStruct((t, D_MODEL), BF),
                   jax.ShapeDtypeStruct((batch, SSD_CONV - 1, SSD_CONV_DIM), F32),
                   jax.ShapeDtypeStruct((batch, SSD_HEADS, SSD_HEADDIM, SSD_STATE), F32)],
        scratch_shapes=[pltpu.VMEM((nseq, 16, SSD_CONV_DIM), F32),
                        pltpu.VMEM((q, SSD_CONV_DIM), F32),
                        pltpu.VMEM((q, D_MODEL), F32),
                        pltpu.VMEM((q, D_MODEL), F32),
                        pltpu.VMEM((D_MODEL, q), BF),
                        pltpu.VMEM((q, LANES), F32)],
        compiler_params=_params(len(grid)),
        name="ssd_sample" if has_h0 else "ssd_prompt",
    )(*args)


S5_SEQ_TILE = 8
S5_LANE_CHUNK = 1024


def _s5_kernel(*refs, tl, has_h0):
    bt = S5_SEQ_TILE
    m = bt * tl
    if has_h0:
        (u_ref, perm_ref, permt_ref, bblk_ref, cblk_ref, lre_ref, lim_ref, d_ref, h0re_ref, h0im_ref,
         y_ref, ore_ref, oim_ref, bre, bim, sre, sim) = refs
    else:
        (u_ref, perm_ref, permt_ref, bblk_ref, cblk_ref, lre_ref, lim_ref, d_ref,
         y_ref, ore_ref, oim_ref, bre, bim, sre, sim) = refs

    @pl.when(pl.program_id(1) == 0)
    def _():
        if has_h0:
            sre[...] = h0re_ref[...]
            sim[...] = h0im_ref[...]
        else:
            sre[...] = jnp.zeros_like(sre)
            sim[...] = jnp.zeros_like(sim)

    u = u_ref[...].reshape(m, D_MODEL)
    u_tm = _dot(perm_ref[...], u).astype(BF)
    half = S5_FLAT // S5_SLABS
    for j in range(S5_SLABS):
        r = _dot(u_tm[:, j * LANES:(j + 1) * LANES], bblk_ref[j])
        bre[:, j * half:(j + 1) * half] = r[:, :half]
        bim[:, j * half:(j + 1) * half] = r[:, half:]

    for jc in range(S5_FLAT // S5_LANE_CHUNK):
        sl = slice(jc * S5_LANE_CHUNK, (jc + 1) * S5_LANE_CHUNK)
        lr = lre_ref[:, sl]
        li = lim_ref[:, sl]

        def step(l, carry, sl=sl, lr=lr, li=li):
            sr, si = carry
            rows = pl.ds(pl.multiple_of(l * bt, bt), bt)
            nr = lr * sr - li * si + bre[rows, sl]
            ni = lr * si + li * sr + bim[rows, sl]
            bre[rows, sl] = nr
            bim[rows, sl] = ni
            return nr, ni

        sr, si = lax.fori_loop(0, tl, step, (sre[:, sl], sim[:, sl]))
        sre[:, sl] = sr
        sim[:, sl] = si

    ys = []
    for j in range(S5_SLABS):
        st = jnp.concatenate([bre[:, j * half:(j + 1) * half], bim[:, j * half:(j + 1) * half]], axis=1)
        ys.append(_dot(st.astype(BF), cblk_ref[j]))
    y_tm = jnp.concatenate(ys, axis=1).astype(BF)
    y_bm = _dot(permt_ref[...], y_tm) + d_ref[...] * u.astype(F32)
    y_ref[...] = y_bm.astype(y_ref.dtype).reshape(y_ref.shape)
    ore_ref[...] = sre[...]
    oim_ref[...] = sim[...]


def _s5(u, perm, permt, bblk, cblk, lre, lim, dvec, h0re, h0im, *, batch, seqlen, tl):
    bt = S5_SEQ_TILE
    m = bt * tl
    has_h0 = h0re is not None
    nb = batch // bt
    nt = seqlen // tl
    if u.ndim == 3:
        u_spec = pl.BlockSpec((bt, tl, D_MODEL), lambda b, t: (b, t, 0))
    else:
        u_spec = pl.BlockSpec((m, D_MODEL), lambda b, t: (b * nt + t, 0))
    st_spec = pl.BlockSpec((bt, S5_FLAT), lambda b, t: (b, 0))
    consts = [perm, permt, bblk, cblk, lre, lim, dvec]
    in_specs = [u_spec] + [_const_spec(c.shape) for c in consts]
    args = [u] + consts
    if has_h0:
        in_specs += [st_spec, st_spec]
        args += [h0re, h0im]
    return pl.pallas_call(
        functools.partial(_s5_kernel, tl=tl, has_h0=has_h0),
        grid=(nb, nt),
        in_specs=in_specs,
        out_specs=[u_spec, st_spec, st_spec],
        out_shape=[jax.ShapeDtypeStruct(u.shape, BF),
                   jax.ShapeDtypeStruct((batch, S5_FLAT), F32),
                   jax.ShapeDtypeStruct((batch, S5_FLAT), F32)],
        scratch_shapes=[pltpu.VMEM((m, S5_FLAT), F32), pltpu.VMEM((m, S5_FLAT), F32),
                        pltpu.VMEM((bt, S5_FLAT), F32), pltpu.VMEM((bt, S5_FLAT), F32)],
        compiler_params=_params(2),
        name="s5_sample" if has_h0 else "s5_prompt",
    )(*args)


def _mixout_kernel(y_ref, z_ref, y5_ref, ga_ref, gb_ref, x_ref, ng_ref, wa_ref, wglu_ref, bglu_ref, wmix_ref, o_ref):
    y = y_ref[...].astype(F32) * _silu(z_ref[...].astype(F32))
    yn = _rms(y, ng_ref[...]).astype(BF)
    branch_a = _dot(yn, wa_ref[...])
    glu = _dot(y5_ref[...], wglu_ref[...]) + bglu_ref[...]
    branch_b = glu[:, :D_MODEL] * _sigmoid(glu[:, D_MODEL:])
    merged = _sigmoid(ga_ref[...].astype(F32)) * branch_a + _sigmoid(gb_ref[...].astype(F32)) * branch_b
    o_ref[...] = x_ref[...] + _dot(merged.astype(BF), wmix_ref[...])


def _mixout(y, z, y5, ga, gb, x2d, ng, wa, wglu, bglu, wmix):
    t = x2d.shape[0]
    tm = TOKEN_TILE
    row = pl.BlockSpec((tm, D_MODEL), lambda i: (i, 0))
    consts = [ng, wa, wglu, bglu, wmix]
    return pl.pallas_call(
        _mixout_kernel,
        grid=(t // tm,),
        in_specs=[row] * 6 + [_const_spec(c.shape) for c in consts],
        out_specs=row,
        out_shape=jax.ShapeDtypeStruct((t, D_MODEL), F32),
        compiler_params=_params(1),
        name="mixout",
    )(y, z, y5, ga, gb, x2d, *consts)


def _attn_kernel(h_ref, g_ref, wq_ref, wo_ref, k_ref, v_ref, o_ref, obuf, *, nkv, rows_per_seq):
    h = h_ref[...]
    r = h.shape[0]
    qv = _dot(_rms(h, g_ref[...]).astype(BF), wq_ref[...]).astype(BF)
    heads_split = len(k_ref.shape) == 4

    def head_of(ref, hd):
        if heads_split:
            parts = [ref[i, :, hd, :] for i in range(nkv)]
            return (parts[0] if nkv == 1 else jnp.concatenate(parts, axis=0)).astype(BF)
        return ref[0, :, hd * MEM_HEAD_DIM:(hd + 1) * MEM_HEAD_DIM].astype(BF)

    if nkv > 1:
        rs = lax.shift_right_logical(lax.broadcasted_iota(I32, (r, nkv * MEM_LEN), 0), int(math.log2(rows_per_seq)))
        cq = lax.shift_right_logical(lax.broadcasted_iota(I32, (r, nkv * MEM_LEN), 1), int(math.log2(MEM_LEN)))
        visible = rs == cq
    scale = MEM_HEAD_DIM ** -0.5
    for hd in range(MEM_HEADS):
        sl = slice(hd * MEM_HEAD_DIM, (hd + 1) * MEM_HEAD_DIM)
        s = _dot_nt(qv[:, sl], head_of(k_ref, hd)) * scale
        if nkv > 1:
            s = jnp.where(visible, s, NEG_BIG)
        p = jnp.exp(s - jnp.max(s, axis=-1, keepdims=True))
        p = p / jnp.sum(p, axis=-1, keepdims=True)
        obuf[:, sl] = _dot(p.astype(BF), head_of(v_ref, hd))
    o_ref[...] = h + _dot(obuf[...].astype(BF), wo_ref[...])


def _attn(h2d, g, wq, wo, k3, v3, *, rows, nkv, rows_per_seq):
    t = h2d.shape[0]
    nsteps = t // rows
    steps_per_kv = nsteps // (k3.shape[0] // nkv)
    row = pl.BlockSpec((rows, D_MODEL), lambda i: (i, 0))
    if k3.ndim == 4:
        kv = pl.BlockSpec((nkv, MEM_LEN, MEM_HEADS, MEM_HEAD_DIM), lambda i: (i // steps_per_kv, 0, 0, 0))
    else:
        kv = pl.BlockSpec((nkv, MEM_LEN, D_MODEL), lambda i: (i // steps_per_kv, 0, 0))
    consts = [g, wq, wo]
    return pl.pallas_call(
        functools.partial(_attn_kernel, nkv=nkv, rows_per_seq=rows_per_seq),
        grid=(nsteps,),
        in_specs=[row] + [_const_spec(c.shape) for c in consts] + [kv, kv],
        out_specs=row,
        out_shape=jax.ShapeDtypeStruct((t, D_MODEL), F32),
        scratch_shapes=[pltpu.VMEM((rows, D_MODEL), F32)],
        compiler_params=_params(1),
        name="attn",
    )(h2d, *consts, k3, v3)


ROUTER_LANE0 = MOE_GROUPS
ROUTE_ROWS = 8


def _router_kernel(hp_ref, hs_ref, g_ref, wh_ref, wm_ref, wl_ref, b_ref, tri_ref, eye_ref, rows_ref, wt_ref, cnt_ref,
                   carry, *, np_steps):
    i = pl.program_id(0)

    @pl.when(i == 0)
    def _():
        carry[...] = jnp.zeros_like(carry)

    h = jnp.where(i < np_steps, hp_ref[...], hs_ref[...])
    xn = _rms(h, g_ref[...])
    xh, xm, xl = _split3(xn)
    wh, wm, wl = wh_ref[...], wm_ref[...], wl_ref[...]
    logits = (_dot(xh, wh) + _dot(xh, wm) + _dot(xm, wh) + _dot(xh, wl) + _dot(xm, wm) + _dot(xl, wh)) + b_ref[...]
    lane = lax.broadcasted_iota(I32, logits.shape, 1)
    is_g = lane < MOE_GROUPS
    lg = jnp.where(is_g, logits, NEG_BIG)
    mg = jnp.max(lg, axis=-1, keepdims=True)
    gidx = jnp.min(jnp.where(lg == mg, lane, LANES), axis=-1, keepdims=True)
    g_w = 1.0 / jnp.sum(jnp.where(is_g, jnp.exp(lg - mg), 0.0), axis=-1, keepdims=True)
    e_lane = lane - ROUTER_LANE0
    in_grp = ((e_lane >= 0) & (e_lane < MOE_EXPERTS)
              & (lax.shift_right_logical(jnp.maximum(e_lane, 0), 3) == gidx))
    le = jnp.where(in_grp, logits, NEG_BIG)
    m1 = jnp.max(le, axis=-1, keepdims=True)
    i1 = jnp.min(jnp.where(le == m1, lane, LANES), axis=-1, keepdims=True)
    le2 = jnp.where(lane == i1, NEG_BIG, le)
    m2 = jnp.max(le2, axis=-1, keepdims=True)
    i2 = jnp.min(jnp.where(le2 == m2, lane, LANES), axis=-1, keepdims=True)
    ratio = jnp.exp(m2 - m1)
    w1 = g_w / (1.0 + ratio)
    w2 = g_w * ratio / (1.0 + ratio)
    wt_ref[...] = jnp.where(lane == 0, w1, jnp.where(lane == 1, w2, 0.0))

    oh1 = jnp.where(lane == i1, 1.0, 0.0)
    oh2 = jnp.where(lane == i2, 1.0, 0.0)
    tri = tri_ref[...]
    c0 = carry[...]
    tot1 = jnp.sum(oh1, axis=0, keepdims=True)
    r1 = jnp.sum(oh1 * (_dot(tri, oh1.astype(BF)) + c0), axis=-1, keepdims=True)
    r2 = jnp.sum(oh2 * (_dot(tri, oh2.astype(BF)) + (c0 + tot1)), axis=-1, keepdims=True)
    c1 = c0 + tot1 + jnp.sum(oh2, axis=0, keepdims=True)
    carry[...] = c1
    cols = jnp.where(lane == 0, (i1 - ROUTER_LANE0).astype(F32),
                     jnp.where(lane == 1, (i2 - ROUTER_LANE0).astype(F32),
                               jnp.where(lane == 2, r1, jnp.where(lane == 3, r2, 0.0))))
    rows_ref[0] = _dot_exact(eye_ref[...], cols, nt=True).astype(I32)
    cnt_ref[...] = jnp.broadcast_to(c1, cnt_ref.shape).astype(I32)


def _router(hp, hs, g, wh, wm, wl, b):
    tm = TOKEN_TILE
    np_steps = hp.shape[0] // tm
    ns_steps = hs.shape[0] // tm
    t = hp.shape[0] + hs.shape[0]
    tri = jnp.asarray(np.tril(np.ones((tm, tm), np.float32), -1)).astype(BF)
    eye8 = jnp.asarray(np.eye(ROUTE_ROWS, LANES, dtype=np.float32)).astype(BF)
    consts = [g, wh, wm, wl, b, tri, eye8]
    return pl.pallas_call(
        functools.partial(_router_kernel, np_steps=np_steps),
        grid=(np_steps + ns_steps,),
        in_specs=[pl.BlockSpec((tm, D_MODEL), lambda i: (jnp.minimum(i, np_steps - 1), 0)),
                  pl.BlockSpec((tm, D_MODEL), lambda i: (jnp.maximum(i - np_steps, 0), 0))]
                 + [_const_spec(c.shape) for c in consts],
        out_specs=[pl.BlockSpec((1, ROUTE_ROWS, tm), lambda i: (i, 0, 0)),
                   pl.BlockSpec((tm, LANES), lambda i: (i, 0)), _const_spec((8, LANES))],
        out_shape=[jax.ShapeDtypeStruct((t // tm, ROUTE_ROWS, tm), I32), jax.ShapeDtypeStruct((t, LANES), F32),
                   jax.ShapeDtypeStruct((8, LANES), I32)],
        scratch_shapes=[pltpu.VMEM((1, LANES), F32)],
        compiler_params=_params(1),
        name="router",
    )(hp, hs, *consts)


SLABS = D_MODEL // LANES


def _token_rows(t):
    return pl.ds(pl.multiple_of(t * SLABS, SLABS), SLABS)


def _to_token_major(ref, x, base=0):
    tm = x.shape[0]
    for s in range(SLABS):
        ref[pl.ds(base + s, tm, stride=SLABS), :] = x[:, s * LANES:(s + 1) * LANES]


def _from_token_major(ref, tm, base=0):
    return jnp.concatenate([ref[pl.ds(base + s, tm, stride=SLABS), :] for s in range(SLABS)], axis=1)


def _sorted_row(ps_ref, rt_ref, k, t):
    return ps_ref[rt_ref[0, k, t]] + rt_ref[0, 2 + k, t]


def _scatter_kernel(ps_ref, ve_ref, pe_ref, hp_ref, hs_ref, g_ref, rt_ref, xs_ref, xbuf, zbuf, sem, zsem, *, np_steps):
    i = pl.program_id(0)
    tm = TOKEN_TILE

    tile_rows = MOE_ROW_TILE * SLABS
    n_tiles = xs_ref.shape[0] // tile_rows

    def pad_copy(r):
        return pltpu.make_async_copy(zbuf.at[pl.ds(0, SLABS)], xs_ref.at[_token_rows(r)], zsem)

    def tile_copy(j):
        return pltpu.make_async_copy(zbuf, xs_ref.at[pl.ds(pl.multiple_of(j * tile_rows, tile_rows), tile_rows)], zsem)

    @pl.when(i == 0)
    def _():
        zbuf[...] = jnp.zeros_like(zbuf)
        first_unused = pe_ref[MOE_EXPERTS - 1] // MOE_ROW_TILE

        def tile_start(j, c):
            tile_copy(j).start()
            return c

        def tile_wait(j, c):
            tile_copy(0).wait()
            return c

        lax.fori_loop(first_unused, n_tiles, tile_start, 0)
        lax.fori_loop(first_unused, n_tiles, tile_wait, 0)

        def per_expert(e, n):
            lo = ve_ref[e]
            hi = pe_ref[e]

            def body(r, c):
                pad_copy(r).start()
                return c

            lax.fori_loop(lo, hi, body, 0)
            return n + (hi - lo)

        n_pad = lax.fori_loop(0, MOE_EXPERTS, per_expert, 0)

        def pad_wait(r, c):
            pad_copy(0).wait()
            return c

        lax.fori_loop(0, n_pad, pad_wait, 0)

    h = jnp.where(i < np_steps, hp_ref[...], hs_ref[...])
    _to_token_major(xbuf, _rms(h, g_ref[...]))

    def row_copy(t, dst_row):
        return pltpu.make_async_copy(xbuf.at[_token_rows(t)], xs_ref.at[_token_rows(dst_row)], sem)

    def issue(t, c):
        row_copy(t, _sorted_row(ps_ref, rt_ref, 0, t)).start()
        row_copy(t, _sorted_row(ps_ref, rt_ref, 1, t)).start(priority=1)
        return c

    def wait(t, c):
        row_copy(0, 0).wait()
        return c

    lax.fori_loop(0, tm, issue, 0, unroll=8)
    lax.fori_loop(0, 2 * tm, wait, 0, unroll=16)


def _scatter(hp, hs, g, route, pstart, valid_end, pend, n_rows):
    tm = TOKEN_TILE
    np_steps = hp.shape[0] // tm
    ns_steps = hs.shape[0] // tm
    grid_spec = pltpu.PrefetchScalarGridSpec(
        num_scalar_prefetch=3,
        grid=(np_steps + ns_steps,),
        in_specs=[pl.BlockSpec((tm, D_MODEL), lambda i, *_: (jnp.minimum(i, np_steps - 1), 0)),
                  pl.BlockSpec((tm, D_MODEL), lambda i, *_: (jnp.maximum(i - np_steps, 0), 0)),
                  pl.BlockSpec(g.shape, lambda i, *_: (0, 0)),
                  pl.BlockSpec((1, ROUTE_ROWS, tm), lambda i, *_: (i, 0, 0), memory_space=pltpu.SMEM)],
        out_specs=pl.BlockSpec(memory_space=pl.ANY),
        scratch_shapes=[pltpu.VMEM((tm * SLABS, LANES), F32), pltpu.VMEM((MOE_ROW_TILE * SLABS, LANES), F32),
                        pltpu.SemaphoreType.DMA(()), pltpu.SemaphoreType.DMA(())],
    )
    return pl.pallas_call(
        functools.partial(_scatter_kernel, np_steps=np_steps),
        grid_spec=grid_spec,
        out_shape=jax.ShapeDtypeStruct((n_rows * SLABS, LANES), F32),
        compiler_params=_params(1),
        name="scatter",
    )(pstart, valid_end, pend, hp, hs, g, route)


def _expert_kernel(te_ref, nu_ref, x_ref, wg_ref, wu_ref, wd_ref, y_ref, wg16, wu16, wd16):
    i = pl.program_id(0)
    tm = MOE_ROW_TILE

    @pl.when(i < nu_ref[0])
    def _():
        @pl.when((i == 0) | (te_ref[i] != te_ref[jnp.maximum(i - 1, 0)]))
        def _():
            wg16[...] = wg_ref[0].astype(BF)
            wu16[...] = wu_ref[0].astype(BF)
            wd16[...] = wd_ref[0].astype(BF)

        x = _from_token_major(x_ref, tm).astype(BF)
        gate = _dot(x, wg16[...])
        up = _dot(x, wu16[...])
        _to_token_major(y_ref, _dot((_silu(gate) * up).astype(BF), wd16[...]))

    @pl.when(i >= nu_ref[0])
    def _():
        y_ref[...] = jnp.zeros_like(y_ref)


def _experts(x_sorted, tile_expert, n_used, wg, wu, wd):
    tm = MOE_ROW_TILE
    n_tiles = x_sorted.shape[0] // (tm * SLABS)
    used = lambda i, nu: jnp.minimum(i, nu[0] - 1)
    grid_spec = pltpu.PrefetchScalarGridSpec(
        num_scalar_prefetch=2,
        grid=(n_tiles,),
        in_specs=[pl.BlockSpec((tm * SLABS, LANES), lambda i, te, nu: (used(i, nu), 0)),
                  pl.BlockSpec((1, D_MODEL, MOE_D_FF), lambda i, te, nu: (te[used(i, nu)], 0, 0)),
                  pl.BlockSpec((1, D_MODEL, MOE_D_FF), lambda i, te, nu: (te[used(i, nu)], 0, 0)),
                  pl.BlockSpec((1, MOE_D_FF, D_MODEL), lambda i, te, nu: (te[used(i, nu)], 0, 0))],
        out_specs=pl.BlockSpec((tm * SLABS, LANES), lambda i, te, nu: (i, 0)),
        scratch_shapes=[pltpu.VMEM((D_MODEL, MOE_D_FF), BF), pltpu.VMEM((D_MODEL, MOE_D_FF), BF),
                        pltpu.VMEM((MOE_D_FF, D_MODEL), BF)],
    )
    return pl.pallas_call(
        _expert_kernel,
        grid_spec=grid_spec,
        out_shape=jax.ShapeDtypeStruct(x_sorted.shape, F32),
        compiler_params=_params(1),
        name="experts",
    )(tile_expert, n_used, x_sorted, wg, wu, wd)


def _combine_kernel(ps_ref, h_ref, wt_ref, g_ref, pcur_ref, pnext_ref, ys_ref, o_ref, ybuf, sem, *, n):
    i = pl.program_id(0)
    tm = TOKEN_TILE

    def row_copy(src_row, j, slot):
        return pltpu.make_async_copy(ys_ref.at[_token_rows(src_row)], ybuf.at[slot, _token_rows(j)], sem.at[slot])

    def issue(rt_ref, slot):
        def body(t, c):
            row_copy(_sorted_row(ps_ref, rt_ref, 0, t), t, slot).start()
            row_copy(_sorted_row(ps_ref, rt_ref, 1, t), tm + t, slot).start(priority=1)
            return c

        lax.fori_loop(0, tm, body, 0, unroll=8)

    def finish(slot):
        def wait(j, c):
            row_copy(0, 0, slot).wait()
            return c

        lax.fori_loop(0, 2 * tm, wait, 0, unroll=16)
        wt = wt_ref[...]
        y1 = _from_token_major(ybuf.at[slot], tm)
        y2 = _from_token_major(ybuf.at[slot], tm, base=tm * SLABS)
        o_ref[...] = _rms(h_ref[...] + wt[:, 0:1] * y1 + wt[:, 1:2] * y2, g_ref[...])

    @pl.when(i == 0)
    def _():
        issue(pcur_ref, 0)

    for slot in (0, 1):
        @pl.when(lax.rem(i, 2) == slot)
        def _(slot=slot):
            @pl.when(i + 1 < n)
            def _():
                issue(pnext_ref, 1 - slot)

            finish(slot)


def _combine(h2d, wt, g, route, pstart, y_sorted, *, tile_offset):
    t = h2d.shape[0]
    tm = TOKEN_TILE
    nsteps = t // tm
    route_spec = lambda f: pl.BlockSpec((1, ROUTE_ROWS, tm), f, memory_space=pltpu.SMEM)
    grid_spec = pltpu.PrefetchScalarGridSpec(
        num_scalar_prefetch=1,
        grid=(nsteps,),
        in_specs=[pl.BlockSpec((tm, D_MODEL), lambda i, ps: (i, 0)),
                  pl.BlockSpec((tm, LANES), lambda i, ps: (i + tile_offset, 0)),
                  pl.BlockSpec(g.shape, lambda i, ps: (0, 0)),
                  route_spec(lambda i, ps: (i + tile_offset, 0, 0)),
                  route_spec(lambda i, ps: (jnp.minimum(i + 1, nsteps - 1) + tile_offset, 0, 0)),
                  pl.BlockSpec(memory_space=pl.ANY)],
        out_specs=pl.BlockSpec((tm, D_MODEL), lambda i, ps: (i, 0)),
        scratch_shapes=[pltpu.VMEM((2, 2 * tm * SLABS, LANES), F32), pltpu.SemaphoreType.DMA((2,))],
    )
    return pl.pallas_call(
        functools.partial(_combine_kernel, n=nsteps),
        grid_spec=grid_spec,
        out_shape=jax.ShapeDtypeStruct((t, D_MODEL), F32),
        compiler_params=_params(1),
        name="combine",
    )(pstart, h2d, wt, g, route, route, y_sorted)


def _moe_plan(cnt, t):
    rt = MOE_ROW_TILE
    n_tiles = (2 * t + MOE_EXPERTS * (rt - 1) + rt - 1) // rt
    counts = cnt[0, ROUTER_LANE0:ROUTER_LANE0 + MOE_EXPERTS]
    padded = ((counts + rt - 1) // rt) * rt
    pend = jnp.cumsum(padded).astype(I32)
    pstart = pend - padded
    valid_end = (pstart + counts).astype(I32)
    tile_start = jnp.arange(n_tiles, dtype=I32) * rt
    tile_expert = jnp.sum((tile_start[:, None] >= pend[None, :]).astype(I32), axis=1)
    tile_expert = jnp.minimum(tile_expert, MOE_EXPERTS - 1).astype(I32)
    n_used = (pend[-1] // rt).astype(I32).reshape(1)
    return pstart, valid_end, pend, tile_expert, n_used, n_tiles * rt


def _perm_matrix(bt, tl):
    m = bt * tl
    p = np.zeros((m, m), np.float32)
    for l in range(tl):
        for b in range(bt):
            p[l * bt + b, b * tl + l] = 1.0
    return p


def _s5_operators(lam_re, lam_im, log_step, b_re, b_im, c_re, c_im):
    lam = lax.complex(lam_re, lam_im)
    step = jnp.exp(log_step)[:, None]
    lam_bar = jnp.exp(lam * step)
    b_bar = ((lam_bar - 1.0) / lam)[..., None] * lax.complex(b_re, b_im)
    gl = LANES // S5_GROUP
    eye = jnp.eye(gl, dtype=F32)

    def in_block(bpart):
        bp = bpart.reshape(S5_SLABS, gl, S5_STATE, S5_GROUP)
        blk = jnp.einsum("sgnk,gh->sgkhn", bp, eye)
        return blk.reshape(S5_SLABS, LANES, gl * S5_STATE)

    def out_block(cpart):
        cp = cpart.reshape(S5_SLABS, gl, S5_GROUP, S5_STATE)
        blk = jnp.einsum("sgkn,gh->sgnhk", cp, eye)
        return blk.reshape(S5_SLABS, gl * S5_STATE, LANES)

    bblk = jnp.concatenate([in_block(jnp.real(b_bar)), in_block(jnp.imag(b_bar))], axis=2).astype(BF)
    cblk = jnp.concatenate([out_block(c_re), out_block(-c_im)], axis=1).astype(BF)
    lre = jnp.broadcast_to(jnp.real(lam_bar).reshape(1, S5_FLAT), (S5_SEQ_TILE, S5_FLAT))
    lim = jnp.broadcast_to(jnp.imag(lam_bar).reshape(1, S5_FLAT), (S5_SEQ_TILE, S5_FLAT))
    return bblk, cblk, lre, lim


def _pad_lanes(v, width=LANES):
    return jnp.pad(v, [(0, 0)] * (v.ndim - 1) + [(0, width - v.shape[-1])])


def kernel(x_prompt, x_sample, state_conv, state_ssd, state_s5_re, state_s5_im, cache_mem_k, cache_mem_v, mem_prompt, norm_mix_g, w_in, conv_w, conv_b, ssd_dt_bias, ssd_a_log, ssd_d, ssd_norm_g, w_ssd_branch, s5_lambda_re, s5_lambda_im, s5_log_step, s5_b_re, s5_b_im, s5_c_re, s5_c_im, s5_d, w_glu, b_glu, w_mix_out, norm_mem_q_g, norm_mem_kv_g, w_mem_q, w_mem_k, w_mem_v, w_mem_o, norm_ffn_g, w_router_group, b_router_group, w_router_expert, b_router_expert, w_exp_gate, w_exp_up, w_exp_down, norm_final_g):
    bp, lp, _ = x_prompt.shape
    bs, ls, _ = x_sample.shape
    tp, ts = bp * lp, bs * ls
    row1 = lambda v: v.reshape(1, -1).astype(F32)

    w_in0 = w_in[0]
    o_xbc = D_MODEL
    o_dt = o_xbc + SSD_CONV_DIM
    o_u5 = o_dt + SSD_HEADS
    w_main = jnp.concatenate([w_in0[:, :o_dt], w_in0[:, o_u5:]], axis=1).astype(BF)
    w_dt = _pad_lanes(w_in0[:, o_dt:o_u5]).astype(BF)
    dt_bias = _pad_lanes(row1(ssd_dt_bias[0]))
    a_log = _pad_lanes(row1(ssd_a_log[0]))
    dskip = jnp.repeat(ssd_d[0].astype(F32), SSD_HEADDIM).reshape(1, D_MODEL)
    eexp_np = np.zeros((LANES, D_MODEL), np.float32)
    for h in range(SSD_HEADS):
        eexp_np[h, h * SSD_HEADDIM:(h + 1) * SSD_HEADDIM] = 1.0
    eexp = jnp.asarray(eexp_np).astype(BF)
    eye16 = jnp.asarray(np.eye(SSD_HEADS, LANES, dtype=np.float32)).astype(BF)
    bblk, cblk, lre, lim = _s5_operators(s5_lambda_re[0], s5_lambda_im[0], s5_log_step[0], s5_b_re[0],
                                         s5_b_im[0], s5_c_re[0], s5_c_im[0])
    s5d = row1(s5_d[0])
    w_kv = jnp.concatenate([w_mem_k[0], w_mem_v[0]], axis=1).astype(BF)
    w_r = _pad_lanes(jnp.concatenate([w_router_group[0], w_router_expert[0]], axis=1).astype(F32))
    wr_h, wr_m, wr_l = _split3(w_r)
    b_r = _pad_lanes(row1(jnp.concatenate([b_router_group[0], b_router_expert[0]])))
    wg = w_exp_gate[0].reshape(MOE_EXPERTS, D_MODEL, MOE_D_FF)
    wu = w_exp_up[0].reshape(MOE_EXPERTS, D_MODEL, MOE_D_FF)
    wd = w_exp_down[0].reshape(MOE_EXPERTS, MOE_D_FF, D_MODEL)

    def mixer(x2d, batch, seqlen, conv0, h0, s5re0, s5im0):
        z, xbc, u5, ga, gb, dt = _inproj(x2d, row1(norm_mix_g[0]), w_main, w_dt, dt_bias)
        y_ssd, new_conv, new_ssd = _ssd(xbc, dt, conv_w[0].astype(F32), row1(conv_b[0]), a_log, dskip, eexp, eye16,
                                        conv0, h0, batch=batch, seqlen=seqlen)
        if h0 is None:
            tl = S5_TIME_TILE
            u_in = u5.reshape(batch, seqlen, D_MODEL)
        else:
            tl = seqlen
            u_in = u5
        perm = _perm_matrix(S5_SEQ_TILE, tl)
        y5, new_re, new_im = _s5(u_in, jnp.asarray(perm).astype(BF), jnp.asarray(perm.T).astype(BF), bblk, cblk,
                                 lre, lim, s5d, s5re0, s5im0, batch=batch, seqlen=seqlen, tl=tl)
        h1 = _mixout(y_ssd, z, y5.reshape(-1, D_MODEL), ga, gb, x2d, row1(ssd_norm_g[0]),
                     w_ssd_branch[0].astype(BF), w_glu[0].astype(BF), row1(b_glu[0]), w_mix_out[0].astype(BF))
        return h1, new_conv, new_ssd, new_re, new_im

    xp2 = x_prompt.reshape(tp, D_MODEL)
    xs2 = x_sample.reshape(ts, D_MODEL)
    h1p, conv_p, ssd_p, re_p, im_p = mixer(xp2, bp, lp, None, None, None, None)
    h1s, conv_s, ssd_s, re_s, im_s = mixer(xs2, bs, ls, state_conv[0], state_ssd[0],
                                           state_s5_re[0].reshape(bs, S5_FLAT), state_s5_im[0].reshape(bs, S5_FLAT))

    mk_p, mv_p = _memkv(mem_prompt.reshape(bp * MEM_LEN, D_MODEL), row1(norm_mem_kv_g[0]), w_kv)
    gq = row1(norm_mem_q_g[0])
    wq = w_mem_q[0].astype(BF)
    wo = w_mem_o[0].astype(BF)
    h2p = _attn(h1p, gq, wq, wo, mk_p.reshape(bp, MEM_LEN, D_MODEL), mv_p.reshape(bp, MEM_LEN, D_MODEL),
                rows=TOKEN_TILE, nkv=1, rows_per_seq=lp)
    sample_seqs = 4
    h2s = _attn(h1s, gq, wq, wo, cache_mem_k[0], cache_mem_v[0], rows=sample_seqs * ls, nkv=sample_seqs,
                rows_per_seq=ls)

    g_ffn = row1(norm_ffn_g[0])
    route, r_wt, r_cnt = _router(h2p, h2s, g_ffn, wr_h, wr_m, wr_l, b_r)
    pstart, valid_end, pend, tile_expert, n_used, n_rows = _moe_plan(r_cnt, tp + ts)
    x_sorted = _scatter(h2p, h2s, g_ffn, route, pstart, valid_end, pend, n_rows)
    y_sorted = _experts(x_sorted, tile_expert, n_used, wg, wu, wd)
    gf = row1(norm_final_g)
    y_prompt = _combine(h2p, r_wt, gf, route, pstart, y_sorted, tile_offset=0)
    y_sample = _combine(h2s, r_wt, gf, route, pstart, y_sorted, tile_offset=tp // TOKEN_TILE)

    return (y_prompt.reshape(bp, lp, D_MODEL), y_sample.reshape(bs, ls, D_MODEL),
            conv_p[None], ssd_p[None],
            re_p.reshape(1, bp, S5_GROUPS, S5_STATE), im_p.reshape(1, bp, S5_GROUPS, S5_STATE),
            mk_p.reshape(1, bp, MEM_LEN, MEM_HEADS, MEM_HEAD_DIM), mv_p.reshape(1, bp, MEM_LEN, MEM_HEADS, MEM_HEAD_DIM),
            conv_s[None], ssd_s[None],
            re_s.reshape(1, bs, S5_GROUPS, S5_STATE), im_s.reshape(1, bs, S5_GROUPS, S5_STATE))
```

```python
import functools
import math

import numpy as np
import jax
import jax.numpy as jnp
from jax import lax
from jax.experimental import pallas as pl
from jax.experimental.pallas import tpu as pltpu

F32 = jnp.float32
BF = jnp.bfloat16
I32 = jnp.int32

D_MODEL = 1024
SSD_HEADS = 16
SSD_HEADDIM = 64
SSD_GROUPS = 2
SSD_STATE = 128
SSD_CONV = 4
SSD_CONV_DIM = 1536
HEADS_PER_GROUP = SSD_HEADS // SSD_GROUPS
S5_GROUPS = 64
S5_GROUP = 16
S5_STATE = 64
S5_FLAT = S5_GROUPS * S5_STATE
S5_SLABS = D_MODEL // 128
MEM_LEN = 256
MEM_HEADS = 4
MEM_HEAD_DIM = 256
MOE_GROUPS = 4
MOE_EXPERTS_PER_GROUP = 8
MOE_EXPERTS = MOE_GROUPS * MOE_EXPERTS_PER_GROUP
MOE_D_FF = 512
NORM_EPS = 1e-6

LANES = 128
VMEM_LIMIT_BYTES = 56 * 1024 * 1024
TOKEN_TILE = 512
SSD_CHUNK = 128
S5_TIME_TILE = 32
MOE_ROW_TILE = 256
NEG_BIG = -1e30


def _params(n_axes):
    return pltpu.CompilerParams(dimension_semantics=("arbitrary",) * n_axes,
                                vmem_limit_bytes=VMEM_LIMIT_BYTES)


def _const_spec(shape):
    nd = len(shape)
    return pl.BlockSpec(shape, lambda *_: (0,) * nd)


def _dot(a, b):
    return jnp.dot(a, b, preferred_element_type=F32)


def _dot_nt(a, b):
    return lax.dot_general(a, b, (((1,), (1,)), ((), ())), preferred_element_type=F32)


def _split3(v):
    h = v.astype(BF)
    r = v - h.astype(F32)
    m = r.astype(BF)
    l = (r - m.astype(F32)).astype(BF)
    return h, m, l


def _dot_exact(a_bf, v, nt=False):
    f = _dot_nt if nt else _dot
    h, m, l = _split3(v)
    return f(a_bf, h) + f(a_bf, m) + f(a_bf, l)


def _rms(x, g):
    return x * lax.rsqrt(jnp.mean(x * x, axis=-1, keepdims=True) + NORM_EPS) * g


def _sigmoid(x):
    return 1.0 / (1.0 + jnp.exp(-x))


def _silu(x):
    return x * _sigmoid(x)


def _proj_cols(xb, w_ref, o_ref, c0, width):
    for c in range(0, width, 512):
        ce = min(c + 512, width)
        o_ref[:, c:ce] = _dot(xb, w_ref[:, c0 + c:c0 + ce]).astype(o_ref.dtype)


def _inproj_kernel(x_ref, g_ref, w_ref, wdt_ref, dtb_ref, z_ref, xbc_ref, u5_ref, ga_ref, gb_ref, dt_ref):
    xb = _rms(x_ref[...], g_ref[...]).astype(BF)
    c0 = 0
    for o_ref in (z_ref, xbc_ref, u5_ref, ga_ref, gb_ref):
        width = o_ref.shape[1]
        _proj_cols(xb, w_ref, o_ref, c0, width)
        c0 += width
    raw = _dot(xb, wdt_ref[...]) + dtb_ref[...]
    dt_ref[...] = jnp.maximum(raw, 0.0) + jnp.log1p(jnp.exp(-jnp.abs(raw)))


def _inproj(x2d, g, w_main, w_dt, dt_bias):
    t = x2d.shape[0]
    tm = TOKEN_TILE
    widths = (D_MODEL, SSD_CONV_DIM, D_MODEL, D_MODEL, D_MODEL)
    row = lambda w: pl.BlockSpec((tm, w), lambda i: (i, 0))
    return pl.pallas_call(
        _inproj_kernel,
        grid=(t // tm,),
        in_specs=[row(D_MODEL), _const_spec(g.shape), _const_spec(w_main.shape), _const_spec(w_dt.shape),
                  _const_spec(dt_bias.shape)],
        out_specs=[row(w) for w in widths] + [row(LANES)],
        out_shape=[jax.ShapeDtypeStruct((t, w), BF) for w in widths] + [jax.ShapeDtypeStruct((t, LANES), F32)],
        compiler_params=_params(1),
        name="inproj",
    )(x2d, g, w_main, w_dt, dt_bias)


def _memkv_kernel(x_ref, g_ref, w_ref, k_ref, v_ref):
    xb = _rms(x_ref[...], g_ref[...]).astype(BF)
    _proj_cols(xb, w_ref, k_ref, 0, D_MODEL)
    _proj_cols(xb, w_ref, v_ref, D_MODEL, D_MODEL)


def _memkv(mem2d, g, w_kv):
    t = mem2d.shape[0]
    tm = TOKEN_TILE
    row = pl.BlockSpec((tm, D_MODEL), lambda i: (i, 0))
    return pl.pallas_call(
        _memkv_kernel,
        grid=(t // tm,),
        in_specs=[row, _const_spec(g.shape), _const_spec(w_kv.shape)],
        out_specs=[row, row],
        out_shape=[jax.ShapeDtypeStruct((t, D_MODEL), F32)] * 2,
        compiler_params=_params(1),
        name="memkv",
    )(mem2d, g, w_kv)


def _ssd_kernel(*refs, lq, has_h0):
    q = SSD_CHUNK
    nseq = q // lq
    if has_h0:
        (xbc_ref, dt_ref, cw_ref, cb_ref, alog_ref, dsk_ref, eexp_ref, eye_ref, shift_ref, conv0_ref, h0_ref,
         y_ref, convo_ref, ho_ref, hbuf, cacc, yacc, yint, xwt, tot_s) = refs
        hin_ref = h0_ref
        hbuf[...] = jnp.zeros_like(hbuf)
        hbuf[:, 0:SSD_CONV - 1, :] = conv0_ref[...]
    else:
        (xbc_ref, dt_ref, cw_ref, cb_ref, alog_ref, dsk_ref, eexp_ref, eye_ref, shift_ref,
         y_ref, convo_ref, ho_ref, hbuf, cacc, yacc, yint, xwt, tot_s) = refs
        hin_ref = ho_ref

        @pl.when(pl.program_id(1) == 0)
        def _():
            ho_ref[...] = jnp.zeros_like(ho_ref)
            hbuf[...] = jnp.zeros_like(hbuf)

    x16 = xbc_ref[...]
    xraw = x16.astype(F32)
    acc = cb_ref[...] + xraw * cw_ref[SSD_CONV - 1:SSD_CONV, :]
    for k in range(SSD_CONV - 1):
        acc = acc + _dot(shift_ref[k], x16) * cw_ref[k:k + 1, :]
    cacc[...] = acc
    for i in range(nseq):
        tail = xraw[(i + 1) * lq - (SSD_CONV - 1):(i + 1) * lq]
        convo_ref[i] = tail
        corr = hbuf[i, 0:8, :] * cw_ref[0:1, :]
        for k in range(1, SSD_CONV - 1):
            corr = corr + hbuf[i, k:k + 8, :] * cw_ref[k:k + 1, :]
        cacc[i * lq:i * lq + 8, :] += corr
        if not has_h0:
            hbuf[i, 0:SSD_CONV - 1, :] = tail
    xc = _silu(cacc[...])
    xs = xc[:, :D_MODEL]
    xs_bf = xs.astype(BF)
    bm_bf = xc[:, D_MODEL:D_MODEL + SSD_GROUPS * SSD_STATE].astype(BF)
    cm = xc[:, D_MODEL + SSD_GROUPS * SSD_STATE:]
    cm_bf = cm.astype(BF)

    dt = dt_ref[...]
    da = dt * (-jnp.exp(alog_ref[...]))
    ri = lax.broadcasted_iota(I32, (q, q), 0)
    ci = lax.broadcasted_iota(I32, (q, q), 1)
    if nseq == 1:
        causal = ci <= ri
    else:
        sh = int(math.log2(lq))
        same = lax.shift_right_logical(ri, sh) == lax.shift_right_logical(ci, sh)
        causal = same & (ci <= ri)
    lmat = jnp.where(causal, 1.0, 0.0).astype(BF)
    cs = _dot_exact(lmat, da)
    if nseq == 1:
        tot = jnp.broadcast_to(cs[q - 1:q, :], (q, LANES))
    else:
        tot = _dot_exact(jnp.where(same, 1.0, 0.0).astype(BF), da)
    tot_s[...] = tot
    eye = eye_ref[...]
    cs_t = _dot_exact(eye, cs, nt=True)
    dt_t = _dot_exact(eye, dt, nt=True)
    eexp = eexp_ref[...]
    ecs_h, ecs_m, _ = _split3(jnp.exp(cs))
    ecs_x = _dot(ecs_h, eexp) + _dot(ecs_m, eexp)
    wend_x = _dot((jnp.exp(tot - cs) * dt).astype(BF), eexp)

    lane = lax.broadcasted_iota(I32, (q, LANES), 1)
    for g in range(SSD_GROUPS):
        cbg = _dot_nt(cm_bf[:, g * SSD_STATE:(g + 1) * SSD_STATE], bm_bf[:, g * SSD_STATE:(g + 1) * SSD_STATE])
        for jp in range(HEADS_PER_GROUP // 2):
            j = g * (HEADS_PER_GROUP // 2) + jp
            ms = []
            for h in (2 * j, 2 * j + 1):
                col = jnp.broadcast_to(cs[:, h:h + 1], (q, q))
                row = jnp.broadcast_to(cs_t[h:h + 1, :], (q, q))
                dtr = jnp.broadcast_to(dt_t[h:h + 1, :], (q, q))
                ms.append((jnp.where(causal, jnp.exp(col - row), 0.0) * cbg * dtr).astype(BF))
            mp = jnp.concatenate(ms, axis=1)
            xp = xs_bf[:, j * LANES:(j + 1) * LANES]
            zero = jnp.zeros_like(xp)
            x2 = jnp.concatenate([jnp.where(lane < SSD_HEADDIM, xp, zero),
                                  jnp.where(lane >= SSD_HEADDIM, xp, zero)], axis=0)
            yacc[:, j * LANES:(j + 1) * LANES] = _dot(mp, x2)

    gw = HEADS_PER_GROUP * SSD_HEADDIM
    xw = xs * wend_x
    for g in range(SSD_GROUPS):
        xwt[g * gw:(g + 1) * gw, :] = xw[:, g * gw:(g + 1) * gw].T.astype(BF)

    if nseq == 1:
        for g in range(SSD_GROUPS):
            hin_g = hin_ref[0, g * HEADS_PER_GROUP:(g + 1) * HEADS_PER_GROUP].reshape(gw, SSD_STATE)
            yint[:, g * gw:(g + 1) * gw] = _dot_nt(cm_bf[:, g * SSD_STATE:(g + 1) * SSD_STATE], hin_g.astype(BF))
            s_new = _dot(xwt[g * gw:(g + 1) * gw, :], bm_bf[:, g * SSD_STATE:(g + 1) * SSD_STATE])
            for hl in range(HEADS_PER_GROUP):
                h = g * HEADS_PER_GROUP + hl
                dec = jnp.exp(jnp.broadcast_to(tot[0:1, h:h + 1], (SSD_HEADDIM, SSD_STATE)))
                ho_ref[0, h] = dec * hin_ref[0, h] + s_new[hl * SSD_HEADDIM:(hl + 1) * SSD_HEADDIM]
    else:
        yint[...] = jnp.zeros_like(yint)
        sh = int(math.log2(lq))
        rowseq = lax.shift_right_logical(lax.broadcasted_iota(I32, (q, SSD_STATE), 0), sh)
        colseq = lax.shift_right_logical(lax.broadcasted_iota(I32, (gw, q), 1), sh)

        def seq_body(i, carry):
            trow = tot_s[pl.ds(i * lq, 1), :]
            for g in range(SSD_GROUPS):
                hin_g = hin_ref[i, g * HEADS_PER_GROUP:(g + 1) * HEADS_PER_GROUP].reshape(gw, SSD_STATE)
                cmg = cm[:, g * SSD_STATE:(g + 1) * SSD_STATE]
                lhs = jnp.where(rowseq == i, cmg, 0.0).astype(BF)
                yint[:, g * gw:(g + 1) * gw] += _dot_nt(lhs, hin_g.astype(BF))
                xg = xwt[g * gw:(g + 1) * gw, :]
                xi = jnp.where(colseq == i, xg, jnp.zeros_like(xg))
                s_new = _dot(xi, bm_bf[:, g * SSD_STATE:(g + 1) * SSD_STATE])
                for hl in range(HEADS_PER_GROUP):
                    h = g * HEADS_PER_GROUP + hl
                    dec = jnp.exp(jnp.broadcast_to(trow[:, h:h + 1], (SSD_HEADDIM, SSD_STATE)))
                    ho_ref[i, h] = dec * hin_ref[i, h] + s_new[hl * SSD_HEADDIM:(hl + 1) * SSD_HEADDIM]
            return carry

        lax.fori_loop(0, nseq, seq_body, 0)

    y_ref[...] = (yacc[...] + yint[...] * ecs_x + dsk_ref[...] * xs).astype(y_ref.dtype)


def _ssd(xbc, dt, conv_w, conv_b, a_log, dskip, eexp, eye, conv0, h0, *, batch, seqlen):
    q = SSD_CHUNK
    t = xbc.shape[0]
    has_h0 = h0 is not None
    if has_h0:
        lq = seqlen
        nseq = q // lq
        grid = (t // q,)
        rmap = lambda i: (i, 0)
        smap3 = lambda i: (i, 0, 0)
        smap4 = lambda i: (i, 0, 0, 0)
    else:
        lq = q
        nseq = 1
        nc = seqlen // q
        grid = (batch, nc)
        rmap = lambda b, c: (b * nc + c, 0)
        smap3 = lambda b, c: (b, 0, 0)
        smap4 = lambda b, c: (b, 0, 0, 0)
    shifts = np.zeros((SSD_CONV - 1, q, q), np.float32)
    for k in range(SSD_CONV - 1):
        for r in range(q):
            if r % lq + k - (SSD_CONV - 1) >= 0:
                shifts[k, r, r + k - (SSD_CONV - 1)] = 1.0
    consts = [conv_w, conv_b, a_log, dskip, eexp, eye, jnp.asarray(shifts).astype(BF)]
    in_specs = [pl.BlockSpec((q, SSD_CONV_DIM), rmap), pl.BlockSpec((q, LANES), rmap)]
    in_specs += [_const_spec(c.shape) for c in consts]
    args = [xbc, dt] + consts
    if has_h0:
        in_specs += [pl.BlockSpec((nseq, SSD_CONV - 1, SSD_CONV_DIM), smap3),
                     pl.BlockSpec((nseq, SSD_HEADS, SSD_HEADDIM, SSD_STATE), smap4)]
        args += [conv0, h0]
    return pl.pallas_call(
        functools.partial(_ssd_kernel, lq=lq, has_h0=has_h0),
        grid=grid,
        in_specs=in_specs,
        out_specs=[pl.BlockSpec((q, D_MODEL), rmap),
                   pl.BlockSpec((nseq, SSD_CONV - 1, SSD_CONV_DIM), smap3),
                   pl.BlockSpec((nseq, SSD_HEADS, SSD_HEADDIM, SSD_STATE), smap4)],
        out_shape=[jax.ShapeDtypeStruct((t, D_MODEL), BF),
                   jax.ShapeDtypeStruct((batch, SSD_CONV - 1, SSD_CONV_DIM), F32),
                   jax.ShapeDtypeStruct((batch, SSD_HEADS, SSD_HEADDIM, SSD_STATE), F32)],
        scratch_shapes=[pltpu.VMEM((nseq, 16, SSD_CONV_DIM), F32),
                        pltpu.VMEM((q, SSD_CONV_DIM), F32),
                        pltpu.VMEM((q, D_MODEL), F32),
                        pltpu.VMEM((q, D_MODEL), F32),
                        pltpu.VMEM((D_MODEL, q), BF),
                        pltpu.VMEM((q, LANES), F32)],
        compiler_params=_params(len(grid)),
        name="ssd_sample" if has_h0 else "ssd_prompt",
    )(*args)


S5_SEQ_TILE = 8
S5_LANE_CHUNK = 1024


def _s5_kernel(*refs, tl, has_h0):
    bt = S5_SEQ_TILE
    m = bt * tl
    if has_h0:
        (u_ref, perm_ref, permt_ref, bblk_ref, cblk_ref, lre_ref, lim_ref, d_ref, h0re_ref, h0im_ref,
         y_ref, ore_ref, oim_ref, bre, bim, sre, sim) = refs
    else:
        (u_ref, perm_ref, permt_ref, bblk_ref, cblk_ref, lre_ref, lim_ref, d_ref,
         y_ref, ore_ref, oim_ref, bre, bim, sre, sim) = refs

    @pl.when(pl.program_id(1) == 0)
    def _():
        if has_h0:
            sre[...] = h0re_ref[...]
            sim[...] = h0im_ref[...]
        else:
            sre[...] = jnp.zeros_like(sre)
            sim[...] = jnp.zeros_like(sim)

    u = u_ref[...].reshape(m, D_MODEL)
    u_tm = _dot(perm_ref[...], u).astype(BF)
    half = S5_FLAT // S5_SLABS
    for j in range(S5_SLABS):
        r = _dot(u_tm[:, j * LANES:(j + 1) * LANES], bblk_ref[j])
        bre[:, j * half:(j + 1) * half] = r[:, :half]
        bim[:, j * half:(j + 1) * half] = r[:, half:]

    for jc in range(S5_FLAT // S5_LANE_CHUNK):
        sl = slice(jc * S5_LANE_CHUNK, (jc + 1) * S5_LANE_CHUNK)
        lr = lre_ref[:, sl]
        li = lim_ref[:, sl]

        def step(l, carry, sl=sl, lr=lr, li=li):
            sr, si = carry
            rows = pl.ds(pl.multiple_of(l * bt, bt), bt)
            nr = lr * sr - li * si + bre[rows, sl]
            ni = lr * si + li * sr + bim[rows, sl]
            bre[rows, sl] = nr
            bim[rows, sl] = ni
            return nr, ni

        sr, si = lax.fori_loop(0, tl, step, (sre[:, sl], sim[:, sl]))
        sre[:, sl] = sr
        sim[:, sl] = si

    ys = []
    for j in range(S5_SLABS):
        st = jnp.concatenate([bre[:, j * half:(j + 1) * half], bim[:, j * half:(j + 1) * half]], axis=1)
        ys.append(_dot(st.astype(BF), cblk_ref[j]))
    y_tm = jnp.concatenate(ys, axis=1).astype(BF)
    y_bm = _dot(permt_ref[...], y_tm) + d_ref[...] * u.astype(F32)
    y_ref[...] = y_bm.astype(y_ref.dtype).reshape(y_ref.shape)
    ore_ref[...] = sre[...]
    oim_ref[...] = sim[...]


def _s5(u, perm, permt, bblk, cblk, lre, lim, dvec, h0re, h0im, *, batch, seqlen, tl):
    bt = S5_SEQ_TILE
    m = bt * tl
    has_h0 = h0re is not None
    nb = batch // bt
    nt = seqlen // tl
    if u.ndim == 3:
        u_spec = pl.BlockSpec((bt, tl, D_MODEL), lambda b, t: (b, t, 0))
    else:
        u_spec = pl.BlockSpec((m, D_MODEL), lambda b, t: (b * nt + t, 0))
    st_spec = pl.BlockSpec((bt, S5_FLAT), lambda b, t: (b, 0))
    consts = [perm, permt, bblk, cblk, lre, lim, dvec]
    in_specs = [u_spec] + [_const_spec(c.shape) for c in consts]
    args = [u] + consts
    if has_h0:
        in_specs += [st_spec, st_spec]
        args += [h0re, h0im]
    return pl.pallas_call(
        functools.partial(_s5_kernel, tl=tl, has_h0=has_h0),
        grid=(nb, nt),
        in_specs=in_specs,
        out_specs=[u_spec, st_spec, st_spec],
        out_shape=[jax.ShapeDtypeStruct(u.shape, BF),
                   jax.ShapeDtypeStruct((batch, S5_FLAT), F32),
                   jax.ShapeDtypeStruct((batch, S5_FLAT), F32)],
        scratch_shapes=[pltpu.VMEM((m, S5_FLAT), F32), pltpu.VMEM((m, S5_FLAT), F32),
                        pltpu.VMEM((bt, S5_FLAT), F32), pltpu.VMEM((bt, S5_FLAT), F32)],
        compiler_params=_params(2),
        name="s5_sample" if has_h0 else "s5_prompt",
    )(*args)


def _mixout_kernel(y_ref, z_ref, y5_ref, ga_ref, gb_ref, x_ref, ng_ref, wa_ref, wglu_ref, bglu_ref, wmix_ref, o_ref):
    y = y_ref[...].astype(F32) * _silu(z_ref[...].astype(F32))
    yn = _rms(y, ng_ref[...]).astype(BF)
    branch_a = _dot(yn, wa_ref[...])
    glu = _dot(y5_ref[...], wglu_ref[...]) + bglu_ref[...]
    branch_b = glu[:, :D_MODEL] * _sigmoid(glu[:, D_MODEL:])
    merged = _sigmoid(ga_ref[...].astype(F32)) * branch_a + _sigmoid(gb_ref[...].astype(F32)) * branch_b
    o_ref[...] = x_ref[...] + _dot(merged.astype(BF), wmix_ref[...])


def _mixout(y, z, y5, ga, gb, x2d, ng, wa, wglu, bglu, wmix):
    t = x2d.shape[0]
    tm = TOKEN_TILE
    row = pl.BlockSpec((tm, D_MODEL), lambda i: (i, 0))
    consts = [ng, wa, wglu, bglu, wmix]
    return pl.pallas_call(
        _mixout_kernel,
        grid=(t // tm,),
        in_specs=[row] * 6 + [_const_spec(c.shape) for c in consts],
        out_specs=row,
        out_shape=jax.ShapeDtypeStruct((t, D_MODEL), F32),
        compiler_params=_params(1),
        name="mixout",
    )(y, z, y5, ga, gb, x2d, *consts)


def _attn_kernel(h_ref, g_ref, wq_ref, wo_ref, k_ref, v_ref, o_ref, obuf, *, nkv, rows_per_seq):
    h = h_ref[...]
    r = h.shape[0]
    qv = _dot(_rms(h, g_ref[...]).astype(BF), wq_ref[...]).astype(BF)
    heads_split = len(k_ref.shape) == 4

    def head_of(ref, hd):
        if heads_split:
            parts = [ref[i, :, hd, :] for i in range(nkv)]
            return (parts[0] if nkv == 1 else jnp.concatenate(parts, axis=0)).astype(BF)
        return ref[0, :, hd * MEM_HEAD_DIM:(hd + 1) * MEM_HEAD_DIM].astype(BF)

    if nkv > 1:
        rs = lax.shift_right_logical(lax.broadcasted_iota(I32, (r, nkv * MEM_LEN), 0), int(math.log2(rows_per_seq)))
        cq = lax.shift_right_logical(lax.broadcasted_iota(I32, (r, nkv * MEM_LEN), 1), int(math.log2(MEM_LEN)))
        visible = rs == cq
    scale = MEM_HEAD_DIM ** -0.5
    for hd in range(MEM_HEADS):
        sl = slice(hd * MEM_HEAD_DIM, (hd + 1) * MEM_HEAD_DIM)
        s = _dot_nt(qv[:, sl], head_of(k_ref, hd)) * scale
        if nkv > 1:
            s = jnp.where(visible, s, NEG_BIG)
        p = jnp.exp(s - jnp.max(s, axis=-1, keepdims=True))
        p = p / jnp.sum(p, axis=-1, keepdims=True)
        obuf[:, sl] = _dot(p.astype(BF), head_of(v_ref, hd))
    o_ref[...] = h + _dot(obuf[...].astype(BF), wo_ref[...])


def _attn(h2d, g, wq, wo, k3, v3, *, rows, nkv, rows_per_seq):
    t = h2d.shape[0]
    nsteps = t // rows
    steps_per_kv = nsteps // (k3.shape[0] // nkv)
    row = pl.BlockSpec((rows, D_MODEL), lambda i: (i, 0))
    if k3.ndim == 4:
        kv = pl.BlockSpec((nkv, MEM_LEN, MEM_HEADS, MEM_HEAD_DIM), lambda i: (i // steps_per_kv, 0, 0, 0))
    else:
        kv = pl.BlockSpec((nkv, MEM_LEN, D_MODEL), lambda i: (i // steps_per_kv, 0, 0))
    consts = [g, wq, wo]
    return pl.pallas_call(
        functools.partial(_attn_kernel, nkv=nkv, rows_per_seq=rows_per_seq),
        grid=(nsteps,),
        in_specs=[row] + [_const_spec(c.shape) for c in consts] + [kv, kv],
        out_specs=row,
        out_shape=jax.ShapeDtypeStruct((t, D_MODEL), F32),
        scratch_shapes=[pltpu.VMEM((rows, D_MODEL), F32)],
        compiler_params=_params(1),
        name="attn",
    )(h2d, *consts, k3, v3)


ROUTER_LANE0 = MOE_GROUPS
ROUTE_ROWS = 8


def _router_kernel(hp_ref, hs_ref, g_ref, wh_ref, wm_ref, wl_ref, b_ref, tri_ref, eye_ref, rows_ref, wt_ref, cnt_ref,
                   carry, *, np_steps):
    i = pl.program_id(0)

    @pl.when(i == 0)
    def _():
        carry[...] = jnp.zeros_like(carry)

    h = jnp.where(i < np_steps, hp_ref[...], hs_ref[...])
    xn = _rms(h, g_ref[...])
    xh, xm, xl = _split3(xn)
    wh, wm, wl = wh_ref[...], wm_ref[...], wl_ref[...]
    logits = (_dot(xh, wh) + _dot(xh, wm) + _dot(xm, wh) + _dot(xh, wl) + _dot(xm, wm) + _dot(xl, wh)) + b_ref[...]
    lane = lax.broadcasted_iota(I32, logits.shape, 1)
    is_g = lane < MOE_GROUPS
    lg = jnp.where(is_g, logits, NEG_BIG)
    mg = jnp.max(lg, axis=-1, keepdims=True)
    gidx = jnp.min(jnp.where(lg == mg, lane, LANES), axis=-1, keepdims=True)
    g_w = 1.0 / jnp.sum(jnp.where(is_g, jnp.exp(lg - mg), 0.0), axis=-1, keepdims=True)
    e_lane = lane - ROUTER_LANE0
    in_grp = ((e_lane >= 0) & (e_lane < MOE_EXPERTS)
              & (lax.shift_right_logical(jnp.maximum(e_lane, 0), 3) == gidx))
    le = jnp.where(in_grp, logits, NEG_BIG)
    m1 = jnp.max(le, axis=-1, keepdims=True)
    i1 = jnp.min(jnp.where(le == m1, lane, LANES), axis=-1, keepdims=True)
    le2 = jnp.where(lane == i1, NEG_BIG, le)
    m2 = jnp.max(le2, axis=-1, keepdims=True)
    i2 = jnp.min(jnp.where(le2 == m2, lane, LANES), axis=-1, keepdims=True)
    ratio = jnp.exp(m2 - m1)
    w1 = g_w / (1.0 + ratio)
    w2 = g_w * ratio / (1.0 + ratio)
    wt_ref[...] = jnp.where(lane == 0, w1, jnp.where(lane == 1, w2, 0.0))

    oh1 = jnp.where(lane == i1, 1.0, 0.0)
    oh2 = jnp.where(lane == i2, 1.0, 0.0)
    tri = tri_ref[...]
    c0 = carry[...]
    tot1 = jnp.sum(oh1, axis=0, keepdims=True)
    r1 = jnp.sum(oh1 * (_dot(tri, oh1.astype(BF)) + c0), axis=-1, keepdims=True)
    r2 = jnp.sum(oh2 * (_dot(tri, oh2.astype(BF)) + (c0 + tot1)), axis=-1, keepdims=True)
    c1 = c0 + tot1 + jnp.sum(oh2, axis=0, keepdims=True)
    carry[...] = c1
    cols = jnp.where(lane == 0, (i1 - ROUTER_LANE0).astype(F32),
                     jnp.where(lane == 1, (i2 - ROUTER_LANE0).astype(F32),
                               jnp.where(lane == 2, r1, jnp.where(lane == 3, r2, 0.0))))
    rows_ref[0] = _dot_exact(eye_ref[...], cols, nt=True).astype(I32)
    cnt_ref[...] = jnp.broadcast_to(c1, cnt_ref.shape).astype(I32)


def _router(hp, hs, g, wh, wm, wl, b):
    tm = TOKEN_TILE
    np_steps = hp.shape[0] // tm
    ns_steps = hs.shape[0] // tm
    t = hp.shape[0] + hs.shape[0]
    tri = jnp.asarray(np.tril(np.ones((tm, tm), np.float32), -1)).astype(BF)
    eye8 = jnp.asarray(np.eye(ROUTE_ROWS, LANES, dtype=np.float32)).astype(BF)
    consts = [g, wh, wm, wl, b, tri, eye8]
    return pl.pallas_call(
        functools.partial(_router_kernel, np_steps=np_steps),
        grid=(np_steps + ns_steps,),
        in_specs=[pl.BlockSpec((tm, D_MODEL), lambda i: (jnp.minimum(i, np_steps - 1), 0)),
                  pl.BlockSpec((tm, D_MODEL), lambda i: (jnp.maximum(i - np_steps, 0), 0))]
                 + [_const_spec(c.shape) for c in consts],
        out_specs=[pl.BlockSpec((1, ROUTE_ROWS, tm), lambda i: (i, 0, 0)),
                   pl.BlockSpec((tm, LANES), lambda i: (i, 0)), _const_spec((8, LANES))],
        out_shape=[jax.ShapeDtypeStruct((t // tm, ROUTE_ROWS, tm), I32), jax.ShapeDtypeStruct((t, LANES), F32),
                   jax.ShapeDtypeStruct((8, LANES), I32)],
        scratch_shapes=[pltpu.VMEM((1, LANES), F32)],
        compiler_params=_params(1),
        name="router",
    )(hp, hs, *consts)


SLABS = D_MODEL // LANES


def _token_rows(t):
    return pl.ds(pl.multiple_of(t * SLABS, SLABS), SLABS)


def _to_token_major(ref, x, base=0):
    tm = x.shape[0]
    for s in range(SLABS):
        ref[pl.ds(base + s, tm, stride=SLABS), :] = x[:, s * LANES:(s + 1) * LANES]


def _from_token_major(ref, tm, base=0):
    return jnp.concatenate([ref[pl.ds(base + s, tm, stride=SLABS), :] for s in range(SLABS)], axis=1)


def _sorted_row(ps_ref, rt_ref, k, t):
    return ps_ref[rt_ref[0, k, t]] + rt_ref[0, 2 + k, t]


def _scatter_kernel(ps_ref, ve_ref, pe_ref, hp_ref, hs_ref, g_ref, rt_ref, xs_ref, xbuf, zbuf, sem, zsem, *, np_steps):
    i = pl.program_id(0)
    tm = TOKEN_TILE

    tile_rows = MOE_ROW_TILE * SLABS
    n_tiles = xs_ref.shape[0] // tile_rows

    def pad_copy(r):
        return pltpu.make_async_copy(zbuf.at[pl.ds(0, SLABS)], xs_ref.at[_token_rows(r)], zsem)

    def tile_copy(j):
        return pltpu.make_async_copy(zbuf, xs_ref.at[pl.ds(pl.multiple_of(j * tile_rows, tile_rows), tile_rows)], zsem)

    @pl.when(i == 0)
    def _():
        zbuf[...] = jnp.zeros_like(zbuf)
        first_unused = pe_ref[MOE_EXPERTS - 1] // MOE_ROW_TILE

        def tile_start(j, c):
            tile_copy(j).start()
            return c

        def tile_wait(j, c):
            tile_copy(0).wait()
            return c

        lax.fori_loop(first_unused, n_tiles, tile_start, 0)
        lax.fori_loop(first_unused, n_tiles, tile_wait, 0)

        def per_expert(e, n):
            lo = ve_ref[e]
            hi = pe_ref[e]

            def body(r, c):
                pad_copy(r).start()
                return c

            lax.fori_loop(lo, hi, body, 0)
            return n + (hi - lo)

        n_pad = lax.fori_loop(0, MOE_EXPERTS, per_expert, 0)

        def pad_wait(r, c):
            pad_copy(0).wait()
            return c

        lax.fori_loop(0, n_pad, pad_wait, 0)

    h = jnp.where(i < np_steps, hp_ref[...], hs_ref[...])
    _to_token_major(xbuf, _rms(h, g_ref[...]))

    def row_copy(t, dst_row):
        return pltpu.make_async_copy(xbuf.at[_token_rows(t)], xs_ref.at[_token_rows(dst_row)], sem)

    def issue(t, c):
        row_copy(t, _sorted_row(ps_ref, rt_ref, 0, t)).start()
        row_copy(t, _sorted_row(ps_ref, rt_ref, 1, t)).start(priority=1)
        return c

    def wait(t, c):
        row_copy(0, 0).wait()
        return c

    lax.fori_loop(0, tm, issue, 0, unroll=8)
    lax.fori_loop(0, 2 * tm, wait, 0, unroll=16)


def _scatter(hp, hs, g, route, pstart, valid_end, pend, n_rows):
    tm = TOKEN_TILE
    np_steps = hp.shape[0] // tm
    ns_steps = hs.shape[0] // tm
    grid_spec = pltpu.PrefetchScalarGridSpec(
        num_scalar_prefetch=3,
        grid=(np_steps + ns_steps,),
        in_specs=[pl.BlockSpec((tm, D_MODEL), lambda i, *_: (jnp.minimum(i, np_steps - 1), 0)),
                  pl.BlockSpec((tm, D_MODEL), lambda i, *_: (jnp.maximum(i - np_steps, 0), 0)),
                  pl.BlockSpec(g.shape, lambda i, *_: (0, 0)),
                  pl.BlockSpec((1, ROUTE_ROWS, tm), lambda i, *_: (i, 0, 0), memory_space=pltpu.SMEM)],
        out_specs=pl.BlockSpec(memory_space=pl.ANY),
        scratch_shapes=[pltpu.VMEM((tm * SLABS, LANES), F32), pltpu.VMEM((MOE_ROW_TILE * SLABS, LANES), F32),
                        pltpu.SemaphoreType.DMA(()), pltpu.SemaphoreType.DMA(())],
    )
    return pl.pallas_call(
        functools.partial(_scatter_kernel, np_steps=np_steps),
        grid_spec=grid_spec,
        out_shape=jax.ShapeDtypeStruct((n_rows * SLABS, LANES), F32),
        compiler_params=_params(1),
        name="scatter",
    )(pstart, valid_end, pend, hp, hs, g, route)


def _expert_kernel(te_ref, nu_ref, x_ref, wg_ref, wu_ref, wd_ref, y_ref, wg16, wu16, wd16):
    i = pl.program_id(0)
    tm = MOE_ROW_TILE

    @pl.when(i < nu_ref[0])
    def _():
        @pl.when((i == 0) | (te_ref[i] != te_ref[jnp.maximum(i - 1, 0)]))
        def _():
            wg16[...] = wg_ref[0].astype(BF)
            wu16[...] = wu_ref[0].astype(BF)
            wd16[...] = wd_ref[0].astype(BF)

        x = _from_token_major(x_ref, tm).astype(BF)
        gate = _dot(x, wg16[...])
        up = _dot(x, wu16[...])
        _to_token_major(y_ref, _dot((_silu(gate) * up).astype(BF), wd16[...]))

    @pl.when(i >= nu_ref[0])
    def _():
        y_ref[...] = jnp.zeros_like(y_ref)


def _experts(x_sorted, tile_expert, n_used, wg, wu, wd):
    tm = MOE_ROW_TILE
    n_tiles = x_sorted.shape[0] // (tm * SLABS)
    used = lambda i, nu: jnp.minimum(i, nu[0] - 1)
    grid_spec = pltpu.PrefetchScalarGridSpec(
        num_scalar_prefetch=2,
        grid=(n_tiles,),
        in_specs=[pl.BlockSpec((tm * SLABS, LANES), lambda i, te, nu: (used(i, nu), 0)),
                  pl.BlockSpec((1, D_MODEL, MOE_D_FF), lambda i, te, nu: (te[used(i, nu)], 0, 0)),
                  pl.BlockSpec((1, D_MODEL, MOE_D_FF), lambda i, te, nu: (te[used(i, nu)], 0, 0)),
                  pl.BlockSpec((1, MOE_D_FF, D_MODEL), lambda i, te, nu: (te[used(i, nu)], 0, 0))],
        out_specs=pl.BlockSpec((tm * SLABS, LANES), lambda i, te, nu: (i, 0)),
        scratch_shapes=[pltpu.VMEM((D_MODEL, MOE_D_FF), BF), pltpu.VMEM((D_MODEL, MOE_D_FF), BF),
                        pltpu.VMEM((MOE_D_FF, D_MODEL), BF)],
    )
    return pl.pallas_call(
        _expert_kernel,
        grid_spec=grid_spec,
        out_shape=jax.ShapeDtypeStruct(x_sorted.shape, F32),
        compiler_params=_params(1),
        name="experts",
    )(tile_expert, n_used, x_sorted, wg, wu, wd)


def _combine_kernel(ps_ref, h_ref, wt_ref, g_ref, pcur_ref, pnext_ref, ys_ref, o_ref, ybuf, sem, *, n):
    i = pl.program_id(0)
    tm = TOKEN_TILE

    def row_copy(src_row, j, slot):
        return pltpu.make_async_copy(ys_ref.at[_token_rows(src_row)], ybuf.at[slot, _token_rows(j)], sem.at[slot])

    def issue(rt_ref, slot):
        def body(t, c):
            row_copy(_sorted_row(ps_ref, rt_ref, 0, t), t, slot).start()
            row_copy(_sorted_row(ps_ref, rt_ref, 1, t), tm + t, slot).start(priority=1)
            return c

        lax.fori_loop(0, tm, body, 0, unroll=8)

    def finish(slot):
        def wait(j, c):
            row_copy(0, 0, slot).wait()
            return c

        lax.fori_loop(0, 2 * tm, wait, 0, unroll=16)
        wt = wt_ref[...]
        y1 = _from_token_major(ybuf.at[slot], tm)
        y2 = _from_token_major(ybuf.at[slot], tm, base=tm * SLABS)
        o_ref[...] = _rms(h_ref[...] + wt[:, 0:1] * y1 + wt[:, 1:2] * y2, g_ref[...])

    @pl.when(i == 0)
    def _():
        issue(pcur_ref, 0)

    for slot in (0, 1):
        @pl.when(lax.rem(i, 2) == slot)
        def _(slot=slot):
            @pl.when(i + 1 < n)
            def _():
                issue(pnext_ref, 1 - slot)

            finish(slot)


def _combine(h2d, wt, g, route, pstart, y_sorted, *, tile_offset):
    t = h2d.shape[0]
    tm = TOKEN_TILE
    nsteps = t // tm
    route_spec = lambda f: pl.BlockSpec((1, ROUTE_ROWS, tm), f, memory_space=pltpu.SMEM)
    grid_spec = pltpu.PrefetchScalarGridSpec(
        num_scalar_prefetch=1,
        grid=(nsteps,),
        in_specs=[pl.BlockSpec((tm, D_MODEL), lambda i, ps: (i, 0)),
                  pl.BlockSpec((tm, LANES), lambda i, ps: (i + tile_offset, 0)),
                  pl.BlockSpec(g.shape, lambda i, ps: (0, 0)),
                  route_spec(lambda i, ps: (i + tile_offset, 0, 0)),
                  route_spec(lambda i, ps: (jnp.minimum(i + 1, nsteps - 1) + tile_offset, 0, 0)),
                  pl.BlockSpec(memory_space=pl.ANY)],
        out_specs=pl.BlockSpec((tm, D_MODEL), lambda i, ps: (i, 0)),
        scratch_shapes=[pltpu.VMEM((2, 2 * tm * SLABS, LANES), F32), pltpu.SemaphoreType.DMA((2,))],
    )
    return pl.pallas_call(
        functools.partial(_combine_kernel, n=nsteps),
        grid_spec=grid_spec,
        out_shape=jax.ShapeDtypeStruct((t, D_MODEL), F32),
        compiler_params=_params(1),
        name="combine",
    )(pstart, h2d, wt, g, route, route, y_sorted)


def _moe_plan(cnt, t):
    rt = MOE_ROW_TILE
    n_tiles = (2 * t + MOE_EXPERTS * (rt - 1) + rt - 1) // rt
    counts = cnt[0, ROUTER_LANE0:ROUTER_LANE0 + MOE_EXPERTS]
    padded = ((counts + rt - 1) // rt) * rt
    pend = jnp.cumsum(padded).astype(I32)
    pstart = pend - padded
    valid_end = (pstart + counts).astype(I32)
    tile_start = jnp.arange(n_tiles, dtype=I32) * rt
    tile_expert = jnp.sum((tile_start[:, None] >= pend[None, :]).astype(I32), axis=1)
    tile_expert = jnp.minimum(tile_expert, MOE_EXPERTS - 1).astype(I32)
    n_used = (pend[-1] // rt).astype(I32).reshape(1)
    return pstart, valid_end, pend, tile_expert, n_used, n_tiles * rt


GRANULE = 8
LOCAL_ROWS = 1280
TILE_GRANULES = MOE_ROW_TILE // GRANULE


def _route_kernel(hp_ref, hs_ref, g_ref, wcat_ref, wh_ref, b_ref, tri_ref, eye_ref, upper_ref,
                  xl_ref, idx_ref, wt_ref, seg_ref, *, np_steps):
    i = pl.program_id(0)
    h = jnp.where(i < np_steps, hp_ref[...], hs_ref[...])
    xn = _rms(h, g_ref[...])
    tm = xn.shape[0]
    xh = xn.astype(BF)
    xm = (xn - xh.astype(F32)).astype(BF)
    both = _dot(xh, wcat_ref[...])
    logits = both[:, :LANES] + both[:, LANES:] + _dot(xm, wh_ref[...]) + b_ref[...]
    lane = lax.broadcasted_iota(I32, logits.shape, 1)
    is_g = lane < MOE_GROUPS
    lg = jnp.where(is_g, logits, NEG_BIG)
    mg = jnp.max(lg, axis=-1, keepdims=True)
    gidx = jnp.min(jnp.where(lg == mg, lane, LANES), axis=-1, keepdims=True)
    g_w = 1.0 / jnp.sum(jnp.where(is_g, jnp.exp(lg - mg), 0.0), axis=-1, keepdims=True)
    e_lane = lane - ROUTER_LANE0
    in_grp = ((e_lane >= 0) & (e_lane < MOE_EXPERTS)
              & (lax.shift_right_logical(jnp.maximum(e_lane, 0), 3) == gidx))
    le = jnp.where(in_grp, logits, NEG_BIG)
    m1 = jnp.max(le, axis=-1, keepdims=True)
    i1 = jnp.min(jnp.where(le == m1, lane, LANES), axis=-1, keepdims=True)
    le2 = jnp.where(lane == i1, NEG_BIG, le)
    m2 = jnp.max(le2, axis=-1, keepdims=True)
    i2 = jnp.min(jnp.where(le2 == m2, lane, LANES), axis=-1, keepdims=True)
    ratio = jnp.exp(m2 - m1)
    w1 = g_w / (1.0 + ratio)
    w2 = g_w * ratio / (1.0 + ratio)
    wt_ref[...] = jnp.where(lane == 0, w1, jnp.where(lane == 1, w2, 0.0))

    oh1 = jnp.where(lane == i1, 1.0, 0.0)
    oh2 = jnp.where(lane == i2, 1.0, 0.0)
    tot1 = jnp.sum(oh1, axis=0, keepdims=True)
    cnt = (tot1 + jnp.sum(oh2, axis=0, keepdims=True)).astype(I32)
    padded = lax.shift_left(lax.shift_right_logical(cnt + (GRANULE - 1), 3), 3)
    pad8 = jnp.broadcast_to(padded.astype(F32), (8, LANES)).astype(BF)
    lstart = _dot(pad8, upper_ref[...])[0:1, :]
    tri = tri_ref[...]
    l1 = jnp.sum(oh1 * (_dot(tri, oh1.astype(BF)) + lstart), axis=-1, keepdims=True)
    l2 = jnp.sum(oh2 * (_dot(tri, oh2.astype(BF)) + (lstart + tot1)), axis=-1, keepdims=True)
    idx_ref[...] = jnp.where(lane == 0, l1.astype(I32), jnp.where(lane == 1, l2.astype(I32), 0))
    sub = lax.broadcasted_iota(I32, (8, LANES), 0)
    seg_ref[0] = jnp.where(sub == 0, jnp.broadcast_to(cnt, (8, LANES)),
                           jnp.where(sub == 1, jnp.broadcast_to(lstart.astype(I32), (8, LANES)), 0))

    cols = jnp.where(lane == 0, l1, jnp.where(lane == 1, l2, 0.0))
    rows = _dot_exact(eye_ref[...], cols, nt=True)
    r_id = lax.broadcasted_iota(I32, (LOCAL_ROWS, tm), 0).astype(F32)
    perm = jnp.where(r_id == rows[0:1, :], 1.0, 0.0) + jnp.where(r_id == rows[1:2, :], 1.0, 0.0)
    xl_ref[...] = _dot(perm.astype(BF), xh)


def _route(hp, hs, g, wcat, wh, b):
    tm = TOKEN_TILE
    np_steps = hp.shape[0] // tm
    ns_steps = hs.shape[0] // tm
    nt = np_steps + ns_steps
    tri = jnp.asarray(np.tril(np.ones((tm, tm), np.float32), -1)).astype(BF)
    eye8 = jnp.asarray(np.eye(8, LANES, dtype=np.float32)).astype(BF)
    upper = jnp.asarray(np.triu(np.ones((LANES, LANES), np.float32), 1)).astype(BF)
    consts = [g, wcat, wh, b, tri, eye8, upper]
    return pl.pallas_call(
        functools.partial(_route_kernel, np_steps=np_steps),
        grid=(nt,),
        in_specs=[pl.BlockSpec((tm, D_MODEL), lambda i: (jnp.minimum(i, np_steps - 1), 0)),
                  pl.BlockSpec((tm, D_MODEL), lambda i: (jnp.maximum(i - np_steps, 0), 0))]
                 + [_const_spec(c.shape) for c in consts],
        out_specs=[pl.BlockSpec((LOCAL_ROWS, D_MODEL), lambda i: (i, 0)),
                   pl.BlockSpec((tm, LANES), lambda i: (i, 0)),
                   pl.BlockSpec((tm, LANES), lambda i: (i, 0)),
                   pl.BlockSpec((1, 8, LANES), lambda i: (i, 0, 0))],
        out_shape=[jax.ShapeDtypeStruct((nt * LOCAL_ROWS, D_MODEL), F32),
                   jax.ShapeDtypeStruct((nt * tm, LANES), I32),
                   jax.ShapeDtypeStruct((nt * tm, LANES), F32),
                   jax.ShapeDtypeStruct((nt, 8, LANES), I32)],
        compiler_params=_params(1),
        name="route",
    )(hp, hs, *consts)


def _granule_plan(seg, n_tiles):
    cnt = seg[:, 0, ROUTER_LANE0:ROUTER_LANE0 + MOE_EXPERTS]
    lstart = seg[:, 1, ROUTER_LANE0:ROUTER_LANE0 + MOE_EXPERTS]
    ntt = cnt.shape[0]
    ng = (cnt + GRANULE - 1) // GRANULE
    cum = jnp.cumsum(ng, axis=0)
    first = (jnp.arange(ntt, dtype=I32)[:, None] * LOCAL_ROWS + lstart) // GRANULE - (cum - ng)
    total = cum[-1]
    tiles_e = (total + TILE_GRANULES - 1) // TILE_GRANULES
    tend = jnp.cumsum(tiles_e)
    tstart = tend - tiles_e
    tile = jnp.arange(n_tiles, dtype=I32)
    te = jnp.minimum(jnp.sum((tile[:, None] >= tend[None, :]).astype(I32), axis=1), MOE_EXPERTS - 1)
    n_used = tend[-1].astype(I32).reshape(1)
    sel = (te[:, None] == jnp.arange(MOE_EXPERTS, dtype=I32)[None, :]).astype(I32)
    pick = lambda table: jnp.sum(sel[:, None, :] * table[None, :, :], axis=2)
    cum_t, first_t = pick(cum), pick(first)
    off = (tile - jnp.sum(sel * tstart[None, :], axis=1))[:, None] * TILE_GRANULES + jnp.arange(TILE_GRANULES, dtype=I32)[None, :]
    valid = (off < jnp.sum(sel * total[None, :], axis=1)[:, None]) & (tile < n_used[0])[:, None]
    j = jnp.sum((cum_t[:, None, :] <= off[:, :, None]).astype(I32), axis=2)
    jsel = (j[:, :, None] == jnp.arange(ntt, dtype=I32)[None, None, :]).astype(I32)
    src = jnp.sum(jsel * first_t[:, None, :], axis=2) + off
    gsrc = jnp.where(valid, src, 0).astype(I32).reshape(-1)
    n_valid = jnp.sum(valid.astype(I32), axis=1).astype(I32)
    return te.astype(I32), n_used, gsrc, n_valid


def _grouped_kernel(te_ref, nu_ref, gsrc_ref, nv_ref, xin_ref, wg_ref, wu_ref, wd_ref, xout_ref,
                    xbuf, ybuf, wg16, wu16, wd16, sem_in, sem_out):
    i = pl.program_id(0)
    nu = nu_ref[0]

    def hbm_rows(tile, g):
        return pl.ds(pl.multiple_of(gsrc_ref[tile * TILE_GRANULES + g] * GRANULE, GRANULE), GRANULE)

    def buf_rows(g):
        return pl.ds(pl.multiple_of(g * GRANULE, GRANULE), GRANULE)

    def fetch(tile, slot, g):
        return pltpu.make_async_copy(xin_ref.at[hbm_rows(tile, g)], xbuf.at[slot, buf_rows(g)], sem_in.at[slot])

    def store(tile, slot, g):
        return pltpu.make_async_copy(ybuf.at[slot, buf_rows(g)], xout_ref.at[hbm_rows(tile, g)], sem_out.at[slot])

    def over_granules(tile, fn):
        def body(g, c):
            fn(g)
            return c

        lax.fori_loop(0, nv_ref[tile], body, 0)

    def fetch_start(tile, slot):
        over_granules(tile, lambda g: fetch(tile, slot, g).start())

    def fetch_wait(tile, slot):
        over_granules(tile, lambda g: fetch(tile, slot, 0).wait())

    def store_start(tile, slot):
        over_granules(tile, lambda g: store(tile, slot, g).start())

    def store_wait(tile, slot):
        over_granules(tile, lambda g: store(tile, slot, 0).wait())

    def step(slot):
        @pl.when(i == 0)
        def _():
            xbuf[...] = jnp.zeros_like(xbuf)
            fetch_start(0, 0)

        @pl.when(i + 1 < nu)
        def _():
            fetch_start(i + 1, 1 - slot)

        @pl.when(i >= 2)
        def _():
            store_wait(i - 2, slot)

        @pl.when((i == 0) | (te_ref[i] != te_ref[jnp.maximum(i - 1, 0)]))
        def _():
            wg16[...] = wg_ref[0].astype(BF)
            wu16[...] = wu_ref[0].astype(BF)
            wd16[...] = wd_ref[0].astype(BF)

        fetch_wait(i, slot)
        x = xbuf[slot].astype(BF)
        gate = _dot(x, wg16[...])
        up = _dot(x, wu16[...])
        ybuf[slot] = _dot((_silu(gate) * up).astype(BF), wd16[...])
        store_start(i, slot)

        @pl.when(i == nu - 1)
        def _():
            store_wait(i, slot)

            @pl.when(i >= 1)
            def _():
                store_wait(i - 1, 1 - slot)

    for slot in (0, 1):
        @pl.when((i < nu) & (lax.rem(i, 2) == slot))
        def _(slot=slot):
            step(slot)


def _grouped(x_local, te, n_used, gsrc, n_valid, wg, wu, wd):
    tm = MOE_ROW_TILE
    n_tiles = te.shape[0]
    used = lambda i, nu: jnp.minimum(i, jnp.maximum(nu[0] - 1, 0))
    wspec = lambda shape: pl.BlockSpec((1,) + shape, lambda i, te, nu, gs, gd: (te[used(i, nu)], 0, 0))
    grid_spec = pltpu.PrefetchScalarGridSpec(
        num_scalar_prefetch=4,
        grid=(n_tiles,),
        in_specs=[pl.BlockSpec(memory_space=pl.ANY), wspec((D_MODEL, MOE_D_FF)), wspec((D_MODEL, MOE_D_FF)),
                  wspec((MOE_D_FF, D_MODEL))],
        out_specs=pl.BlockSpec(memory_space=pl.ANY),
        scratch_shapes=[pltpu.VMEM((2, tm, D_MODEL), F32), pltpu.VMEM((2, tm, D_MODEL), F32),
                        pltpu.VMEM((D_MODEL, MOE_D_FF), BF), pltpu.VMEM((D_MODEL, MOE_D_FF), BF),
                        pltpu.VMEM((MOE_D_FF, D_MODEL), BF),
                        pltpu.SemaphoreType.DMA((2,)), pltpu.SemaphoreType.DMA((2,))],
    )
    return pl.pallas_call(
        _grouped_kernel,
        grid_spec=grid_spec,
        out_shape=jax.ShapeDtypeStruct(x_local.shape, F32),
        input_output_aliases={4: 0},
        compiler_params=_params(1),
        name="grouped",
    )(te, n_used, gsrc, n_valid, x_local, wg, wu, wd)


def _merge_kernel(h_ref, idx_ref, wt_ref, g_ref, yl_ref, o_ref):
    tm = h_ref.shape[0]
    idx = idx_ref[...]
    wt = wt_ref[...]
    r_id = lax.broadcasted_iota(I32, (tm, LOCAL_ROWS), 1)
    sel = jnp.where(r_id == idx[:, 0:1], wt[:, 0:1], 0.0) + jnp.where(r_id == idx[:, 1:2], wt[:, 1:2], 0.0)
    moe = _dot(sel.astype(BF), yl_ref[...].astype(BF))
    o_ref[...] = _rms(h_ref[...] + moe, g_ref[...])


def _merge(h2d, idx, wt, g, y_local, *, tile_offset):
    t = h2d.shape[0]
    tm = TOKEN_TILE
    return pl.pallas_call(
        _merge_kernel,
        grid=(t // tm,),
        in_specs=[pl.BlockSpec((tm, D_MODEL), lambda i: (i, 0)),
                  pl.BlockSpec((tm, LANES), lambda i: (i + tile_offset, 0)),
                  pl.BlockSpec((tm, LANES), lambda i: (i + tile_offset, 0)),
                  _const_spec(g.shape),
                  pl.BlockSpec((LOCAL_ROWS, D_MODEL), lambda i: (i + tile_offset, 0))],
        out_specs=pl.BlockSpec((tm, D_MODEL), lambda i: (i, 0)),
        out_shape=jax.ShapeDtypeStruct((t, D_MODEL), F32),
        compiler_params=_params(1),
        name="merge",
    )(h2d, idx, wt, g, y_local)


def _perm_matrix(bt, tl):
    m = bt * tl
    p = np.zeros((m, m), np.float32)
    for l in range(tl):
        for b in range(bt):
            p[l * bt + b, b * tl + l] = 1.0
    return p


def _s5_operators(lam_re, lam_im, log_step, b_re, b_im, c_re, c_im):
    lam = lax.complex(lam_re, lam_im)
    step = jnp.exp(log_step)[:, None]
    lam_bar = jnp.exp(lam * step)
    b_bar = ((lam_bar - 1.0) / lam)[..., None] * lax.complex(b_re, b_im)
    gl = LANES // S5_GROUP
    eye = jnp.eye(gl, dtype=F32)

    def in_block(bpart):
        bp = bpart.reshape(S5_SLABS, gl, S5_STATE, S5_GROUP)
        blk = jnp.einsum("sgnk,gh->sgkhn", bp, eye)
        return blk.reshape(S5_SLABS, LANES, gl * S5_STATE)

    def out_block(cpart):
        cp = cpart.reshape(S5_SLABS, gl, S5_GROUP, S5_STATE)
        blk = jnp.einsum("sgkn,gh->sgnhk", cp, eye)
        return blk.reshape(S5_SLABS, gl * S5_STATE, LANES)

    bblk = jnp.concatenate([in_block(jnp.real(b_bar)), in_block(jnp.imag(b_bar))], axis=2).astype(BF)
    cblk = jnp.concatenate([out_block(c_re), out_block(-c_im)], axis=1).astype(BF)
    lre = jnp.broadcast_to(jnp.real(lam_bar).reshape(1, S5_FLAT), (S5_SEQ_TILE, S5_FLAT))
    lim = jnp.broadcast_to(jnp.imag(lam_bar).reshape(1, S5_FLAT), (S5_SEQ_TILE, S5_FLAT))
    return bblk, cblk, lre, lim


def _pad_lanes(v, width=LANES):
    return jnp.pad(v, [(0, 0)] * (v.ndim - 1) + [(0, width - v.shape[-1])])


def kernel(x_prompt, x_sample, state_conv, state_ssd, state_s5_re, state_s5_im, cache_mem_k, cache_mem_v, mem_prompt, norm_mix_g, w_in, conv_w, conv_b, ssd_dt_bias, ssd_a_log, ssd_d, ssd_norm_g, w_ssd_branch, s5_lambda_re, s5_lambda_im, s5_log_step, s5_b_re, s5_b_im, s5_c_re, s5_c_im, s5_d, w_glu, b_glu, w_mix_out, norm_mem_q_g, norm_mem_kv_g, w_mem_q, w_mem_k, w_mem_v, w_mem_o, norm_ffn_g, w_router_group, b_router_group, w_router_expert, b_router_expert, w_exp_gate, w_exp_up, w_exp_down, norm_final_g):
    bp, lp, _ = x_prompt.shape
    bs, ls, _ = x_sample.shape
    tp, ts = bp * lp, bs * ls
    row1 = lambda v: v.reshape(1, -1).astype(F32)

    w_in0 = w_in[0]
    o_xbc = D_MODEL
    o_dt = o_xbc + SSD_CONV_DIM
    o_u5 = o_dt + SSD_HEADS
    w_main = jnp.concatenate([w_in0[:, :o_dt], w_in0[:, o_u5:]], axis=1).astype(BF)
    w_dt = _pad_lanes(w_in0[:, o_dt:o_u5]).astype(BF)
    dt_bias = _pad_lanes(row1(ssd_dt_bias[0]))
    a_log = _pad_lanes(row1(ssd_a_log[0]))
    dskip = jnp.repeat(ssd_d[0].astype(F32), SSD_HEADDIM).reshape(1, D_MODEL)
    eexp_np = np.zeros((LANES, D_MODEL), np.float32)
    for h in range(SSD_HEADS):
        eexp_np[h, h * SSD_HEADDIM:(h + 1) * SSD_HEADDIM] = 1.0
    eexp = jnp.asarray(eexp_np).astype(BF)
    eye16 = jnp.asarray(np.eye(SSD_HEADS, LANES, dtype=np.float32)).astype(BF)
    bblk, cblk, lre, lim = _s5_operators(s5_lambda_re[0], s5_lambda_im[0], s5_log_step[0], s5_b_re[0],
                                         s5_b_im[0], s5_c_re[0], s5_c_im[0])
    s5d = row1(s5_d[0])
    w_kv = jnp.concatenate([w_mem_k[0], w_mem_v[0]], axis=1).astype(BF)
    w_r = _pad_lanes(jnp.concatenate([w_router_group[0], w_router_expert[0]], axis=1).astype(F32))
    wr_h, wr_m, wr_l = _split3(w_r)
    b_r = _pad_lanes(row1(jnp.concatenate([b_router_group[0], b_router_expert[0]])))
    wg = w_exp_gate[0].reshape(MOE_EXPERTS, D_MODEL, MOE_D_FF)
    wu = w_exp_up[0].reshape(MOE_EXPERTS, D_MODEL, MOE_D_FF)
    wd = w_exp_down[0].reshape(MOE_EXPERTS, MOE_D_FF, D_MODEL)

    def mixer(x2d, batch, seqlen, conv0, h0, s5re0, s5im0):
        z, xbc, u5, ga, gb, dt = _inproj(x2d, row1(norm_mix_g[0]), w_main, w_dt, dt_bias)
        y_ssd, new_conv, new_ssd = _ssd(xbc, dt, conv_w[0].astype(F32), row1(conv_b[0]), a_log, dskip, eexp, eye16,
                                        conv0, h0, batch=batch, seqlen=seqlen)
        if h0 is None:
            tl = S5_TIME_TILE
            u_in = u5.reshape(batch, seqlen, D_MODEL)
        else:
            tl = seqlen
            u_in = u5
        perm = _perm_matrix(S5_SEQ_TILE, tl)
        y5, new_re, new_im = _s5(u_in, jnp.asarray(perm).astype(BF), jnp.asarray(perm.T).astype(BF), bblk, cblk,
                                 lre, lim, s5d, s5re0, s5im0, batch=batch, seqlen=seqlen, tl=tl)
        h1 = _mixout(y_ssd, z, y5.reshape(-1, D_MODEL), ga, gb, x2d, row1(ssd_norm_g[0]),
                     w_ssd_branch[0].astype(BF), w_glu[0].astype(BF), row1(b_glu[0]), w_mix_out[0].astype(BF))
        return h1, new_conv, new_ssd, new_re, new_im

    xp2 = x_prompt.reshape(tp, D_MODEL)
    xs2 = x_sample.reshape(ts, D_MODEL)
    h1p, conv_p, ssd_p, re_p, im_p = mixer(xp2, bp, lp, None, None, None, None)
    h1s, conv_s, ssd_s, re_s, im_s = mixer(xs2, bs, ls, state_conv[0], state_ssd[0],
                                           state_s5_re[0].reshape(bs, S5_FLAT), state_s5_im[0].reshape(bs, S5_FLAT))

    mk_p, mv_p = _memkv(mem_prompt.reshape(bp * MEM_LEN, D_MODEL), row1(norm_mem_kv_g[0]), w_kv)
    gq = row1(norm_mem_q_g[0])
    wq = w_mem_q[0].astype(BF)
    wo = w_mem_o[0].astype(BF)
    h2p = _attn(h1p, gq, wq, wo, mk_p.reshape(bp, MEM_LEN, D_MODEL), mv_p.reshape(bp, MEM_LEN, D_MODEL),
                rows=TOKEN_TILE, nkv=1, rows_per_seq=lp)
    sample_seqs = 4
    h2s = _attn(h1s, gq, wq, wo, cache_mem_k[0], cache_mem_v[0], rows=sample_seqs * ls, nkv=sample_seqs,
                rows_per_seq=ls)

    g_ffn = row1(norm_ffn_g[0])
    x_local, r_idx, r_wt, seg = _route(h2p, h2s, g_ffn, jnp.concatenate([wr_h, wr_m], axis=1), wr_h, b_r)
    token_tiles = (tp + ts) // TOKEN_TILE
    max_granules = (2 * (tp + ts) + token_tiles * MOE_EXPERTS * (GRANULE - 1)) // GRANULE
    n_tiles = max_granules // TILE_GRANULES + MOE_EXPERTS
    te, n_used, gsrc, n_valid = _granule_plan(seg, n_tiles)
    y_local = _grouped(x_local, te, n_used, gsrc, n_valid, wg, wu, wd)
    gf = row1(norm_final_g)
    y_prompt = _merge(h2p, r_idx, r_wt, gf, y_local, tile_offset=0)
    y_sample = _merge(h2s, r_idx, r_wt, gf, y_local, tile_offset=tp // TOKEN_TILE)

    return (y_prompt.reshape(bp, lp, D_MODEL), y_sample.reshape(bs, ls, D_MODEL),
            conv_p[None], ssd_p[None],
            re_p.reshape(1, bp, S5_GROUPS, S5_STATE), im_p.reshape(1, bp, S5_GROUPS, S5_STATE),
            mk_p.reshape(1, bp, MEM_LEN, MEM_HEADS, MEM_HEAD_DIM), mv_p.reshape(1, bp, MEM_LEN, MEM_HEADS, MEM_HEAD_DIM),
            conv_s[None], ssd_s[None],
            re_s.reshape(1, bs, S5_GROUPS, S5_STATE), im_s.reshape(1, bs, S5_GROUPS, S5_STATE))
```

```python
import functools
import math

import numpy as np
import jax
import jax.numpy as jnp
from jax import lax
from jax.experimental import pallas as pl
from jax.experimental.pallas import tpu as pltpu

F32 = jnp.float32
BF = jnp.bfloat16
I32 = jnp.int32

D_MODEL = 1024
SSD_HEADS = 16
SSD_HEADDIM = 64
SSD_GROUPS = 2
SSD_STATE = 128
SSD_CONV = 4
SSD_CONV_DIM = 1536
HEADS_PER_GROUP = SSD_HEADS // SSD_GROUPS
S5_GROUPS = 64
S5_GROUP = 16
S5_STATE = 64
S5_FLAT = S5_GROUPS * S5_STATE
S5_SLABS = D_MODEL // 128
MEM_LEN = 256
MEM_HEADS = 4
MEM_HEAD_DIM = 256
MOE_GROUPS = 4
MOE_EXPERTS_PER_GROUP = 8
MOE_EXPERTS = MOE_GROUPS * MOE_EXPERTS_PER_GROUP
MOE_D_FF = 512
NORM_EPS = 1e-6

LANES = 128
VMEM_LIMIT_BYTES = 56 * 1024 * 1024
TOKEN_TILE = 512
SSD_CHUNK = 128
S5_TIME_TILE = 32
MOE_ROW_TILE = 256
NEG_BIG = -1e30


def _params(n_axes):
    return pltpu.CompilerParams(dimension_semantics=("arbitrary",) * n_axes,
                                vmem_limit_bytes=VMEM_LIMIT_BYTES)


def _const_spec(shape):
    nd = len(shape)
    return pl.BlockSpec(shape, lambda *_: (0,) * nd)


def _dot(a, b):
    return jnp.dot(a, b, preferred_element_type=F32)


def _dot_nt(a, b):
    return lax.dot_general(a, b, (((1,), (1,)), ((), ())), preferred_element_type=F32)


def _split3(v):
    h = v.astype(BF)
    r = v - h.astype(F32)
    m = r.astype(BF)
    l = (r - m.astype(F32)).astype(BF)
    return h, m, l


def _dot_exact(a_bf, v, nt=False):
    f = _dot_nt if nt else _dot
    h, m, l = _split3(v)
    return f(a_bf, h) + f(a_bf, m) + f(a_bf, l)


def _rms(x, g):
    return x * lax.rsqrt(jnp.mean(x * x, axis=-1, keepdims=True) + NORM_EPS) * g


def _sigmoid(x):
    return 1.0 / (1.0 + jnp.exp(-x))


def _silu(x):
    return x * _sigmoid(x)


def _proj_cols(xb, w_ref, o_ref, c0, width):
    for c in range(0, width, 512):
        ce = min(c + 512, width)
        o_ref[:, c:ce] = _dot(xb, w_ref[:, c0 + c:c0 + ce]).astype(o_ref.dtype)


def _inproj_kernel(x_ref, g_ref, w_ref, wdt_ref, dtb_ref, z_ref, xbc_ref, u5_ref, ga_ref, gb_ref, dt_ref):
    xb = _rms(x_ref[...], g_ref[...]).astype(BF)
    c0 = 0
    for o_ref in (z_ref, xbc_ref, u5_ref, ga_ref, gb_ref):
        width = o_ref.shape[1]
        _proj_cols(xb, w_ref, o_ref, c0, width)
        c0 += width
    raw = _dot(xb, wdt_ref[...]) + dtb_ref[...]
    dt_ref[...] = jnp.maximum(raw, 0.0) + jnp.log1p(jnp.exp(-jnp.abs(raw)))


def _inproj(x2d, g, w_main, w_dt, dt_bias):
    t = x2d.shape[0]
    tm = TOKEN_TILE
    widths = (D_MODEL, SSD_CONV_DIM, D_MODEL, D_MODEL, D_MODEL)
    row = lambda w: pl.BlockSpec((tm, w), lambda i: (i, 0))
    return pl.pallas_call(
        _inproj_kernel,
        grid=(t // tm,),
        in_specs=[row(D_MODEL), _const_spec(g.shape), _const_spec(w_main.shape), _const_spec(w_dt.shape),
                  _const_spec(dt_bias.shape)],
        out_specs=[row(w) for w in widths] + [row(LANES)],
        out_shape=[jax.ShapeDtypeStruct((t, w), BF) for w in widths] + [jax.ShapeDtypeStruct((t, LANES), F32)],
        compiler_params=_params(1),
        name="inproj",
    )(x2d, g, w_main, w_dt, dt_bias)


def _memkv_kernel(x_ref, g_ref, w_ref, k_ref, v_ref):
    xb = _rms(x_ref[...], g_ref[...]).astype(BF)
    _proj_cols(xb, w_ref, k_ref, 0, D_MODEL)
    _proj_cols(xb, w_ref, v_ref, D_MODEL, D_MODEL)


def _memkv(mem2d, g, w_kv):
    t = mem2d.shape[0]
    tm = TOKEN_TILE
    row = pl.BlockSpec((tm, D_MODEL), lambda i: (i, 0))
    return pl.pallas_call(
        _memkv_kernel,
        grid=(t // tm,),
        in_specs=[row, _const_spec(g.shape), _const_spec(w_kv.shape)],
        out_specs=[row, row],
        out_shape=[jax.ShapeDtypeStruct((t, D_MODEL), F32)] * 2,
        compiler_params=_params(1),
        name="memkv",
    )(mem2d, g, w_kv)


def _ssd_kernel(*refs, lq, has_h0):
    q = SSD_CHUNK
    nseq = q // lq
    if has_h0:
        (xbc_ref, dt_ref, cw_ref, cb_ref, alog_ref, dsk_ref, eexp_ref, eye_ref, shift_ref, conv0_ref, h0_ref,
         y_ref, convo_ref, ho_ref, hbuf, cacc, yacc, yint, xwt, tot_s) = refs
        hin_ref = h0_ref
        hbuf[...] = jnp.zeros_like(hbuf)
        hbuf[:, 0:SSD_CONV - 1, :] = conv0_ref[...]
    else:
        (xbc_ref, dt_ref, cw_ref, cb_ref, alog_ref, dsk_ref, eexp_ref, eye_ref, shift_ref,
         y_ref, convo_ref, ho_ref, hbuf, cacc, yacc, yint, xwt, tot_s) = refs
        hin_ref = ho_ref

        @pl.when(pl.program_id(1) == 0)
        def _():
            ho_ref[...] = jnp.zeros_like(ho_ref)
            hbuf[...] = jnp.zeros_like(hbuf)

    x16 = xbc_ref[...]
    xraw = x16.astype(F32)
    acc = cb_ref[...] + xraw * cw_ref[SSD_CONV - 1:SSD_CONV, :]
    for k in range(SSD_CONV - 1):
        acc = acc + _dot(shift_ref[k], x16) * cw_ref[k:k + 1, :]
    cacc[...] = acc
    for i in range(nseq):
        tail = xraw[(i + 1) * lq - (SSD_CONV - 1):(i + 1) * lq]
        convo_ref[i] = tail
        corr = hbuf[i, 0:8, :] * cw_ref[0:1, :]
        for k in range(1, SSD_CONV - 1):
            corr = corr + hbuf[i, k:k + 8, :] * cw_ref[k:k + 1, :]
        cacc[i * lq:i * lq + 8, :] += corr
        if not has_h0:
            hbuf[i, 0:SSD_CONV - 1, :] = tail
    xc = _silu(cacc[...])
    xs = xc[:, :D_MODEL]
    xs_bf = xs.astype(BF)
    bm_bf = xc[:, D_MODEL:D_MODEL + SSD_GROUPS * SSD_STATE].astype(BF)
    cm = xc[:, D_MODEL + SSD_GROUPS * SSD_STATE:]
    cm_bf = cm.astype(BF)

    dt = dt_ref[...]
    da = dt * (-jnp.exp(alog_ref[...]))
    ri = lax.broadcasted_iota(I32, (q, q), 0)
    ci = lax.broadcasted_iota(I32, (q, q), 1)
    if nseq == 1:
        causal = ci <= ri
    else:
        sh = int(math.log2(lq))
        same = lax.shift_right_logical(ri, sh) == lax.shift_right_logical(ci, sh)
        causal = same & (ci <= ri)
    lmat = jnp.where(causal, 1.0, 0.0).astype(BF)
    cs = _dot_exact(lmat, da)
    if nseq == 1:
        tot = jnp.broadcast_to(cs[q - 1:q, :], (q, LANES))
    else:
        tot = _dot_exact(jnp.where(same, 1.0, 0.0).astype(BF), da)
    tot_s[...] = tot
    eye = eye_ref[...]
    cs_t = _dot_exact(eye, cs, nt=True)
    dt_t = _dot_exact(eye, dt, nt=True)
    eexp = eexp_ref[...]
    ecs_h, ecs_m, _ = _split3(jnp.exp(cs))
    ecs_x = _dot(ecs_h, eexp) + _dot(ecs_m, eexp)
    wend_x = _dot((jnp.exp(tot - cs) * dt).astype(BF), eexp)

    lane = lax.broadcasted_iota(I32, (q, LANES), 1)
    for g in range(SSD_GROUPS):
        cbg = _dot_nt(cm_bf[:, g * SSD_STATE:(g + 1) * SSD_STATE], bm_bf[:, g * SSD_STATE:(g + 1) * SSD_STATE])
        for jp in range(HEADS_PER_GROUP // 2):
            j = g * (HEADS_PER_GROUP // 2) + jp
            ms = []
            for h in (2 * j, 2 * j + 1):
                col = jnp.broadcast_to(cs[:, h:h + 1], (q, q))
                row = jnp.broadcast_to(cs_t[h:h + 1, :], (q, q))
                dtr = jnp.broadcast_to(dt_t[h:h + 1, :], (q, q))
                ms.append((jnp.where(causal, jnp.exp(col - row), 0.0) * cbg * dtr).astype(BF))
            mp = jnp.concatenate(ms, axis=1)
            xp = xs_bf[:, j * LANES:(j + 1) * LANES]
            zero = jnp.zeros_like(xp)
            x2 = jnp.concatenate([jnp.where(lane < SSD_HEADDIM, xp, zero),
                                  jnp.where(lane >= SSD_HEADDIM, xp, zero)], axis=0)
            yacc[:, j * LANES:(j + 1) * LANES] = _dot(mp, x2)

    gw = HEADS_PER_GROUP * SSD_HEADDIM
    xw = xs * wend_x
    for g in range(SSD_GROUPS):
        xwt[g * gw:(g + 1) * gw, :] = xw[:, g * gw:(g + 1) * gw].T.astype(BF)

    if nseq == 1:
        for g in range(SSD_GROUPS):
            hin_g = hin_ref[0, g * HEADS_PER_GROUP:(g + 1) * HEADS_PER_GROUP].reshape(gw, SSD_STATE)
            yint[:, g * gw:(g + 1) * gw] = _dot_nt(cm_bf[:, g * SSD_STATE:(g + 1) * SSD_STATE], hin_g.astype(BF))
            s_new = _dot(xwt[g * gw:(g + 1) * gw, :], bm_bf[:, g * SSD_STATE:(g + 1) * SSD_STATE])
            for hl in range(HEADS_PER_GROUP):
                h = g * HEADS_PER_GROUP + hl
                dec = jnp.exp(jnp.broadcast_to(tot[0:1, h:h + 1], (SSD_HEADDIM, SSD_STATE)))
                ho_ref[0, h] = dec * hin_ref[0, h] + s_new[hl * SSD_HEADDIM:(hl + 1) * SSD_HEADDIM]
    else:
        yint[...] = jnp.zeros_like(yint)
        sh = int(math.log2(lq))
        rowseq = lax.shift_right_logical(lax.broadcasted_iota(I32, (q, SSD_STATE), 0), sh)
        colseq = lax.shift_right_logical(lax.broadcasted_iota(I32, (gw, q), 1), sh)

        def seq_body(i, carry):
            trow = tot_s[pl.ds(i * lq, 1), :]
            for g in range(SSD_GROUPS):
                hin_g = hin_ref[i, g * HEADS_PER_GROUP:(g + 1) * HEADS_PER_GROUP].reshape(gw, SSD_STATE)
                cmg = cm[:, g * SSD_STATE:(g + 1) * SSD_STATE]
                lhs = jnp.where(rowseq == i, cmg, 0.0).astype(BF)
                yint[:, g * gw:(g + 1) * gw] += _dot_nt(lhs, hin_g.astype(BF))
                xg = xwt[g * gw:(g + 1) * gw, :]
                xi = jnp.where(colseq == i, xg, jnp.zeros_like(xg))
                s_new = _dot(xi, bm_bf[:, g * SSD_STATE:(g + 1) * SSD_STATE])
                for hl in range(HEADS_PER_GROUP):
                    h = g * HEADS_PER_GROUP + hl
                    dec = jnp.exp(jnp.broadcast_to(trow[:, h:h + 1], (SSD_HEADDIM, SSD_STATE)))
                    ho_ref[i, h] = dec * hin_ref[i, h] + s_new[hl * SSD_HEADDIM:(hl + 1) * SSD_HEADDIM]
            return carry

        lax.fori_loop(0, nseq, seq_body, 0)

    y_ref[...] = (yacc[...] + yint[...] * ecs_x + dsk_ref[...] * xs).astype(y_ref.dtype)


def _ssd(xbc, dt, conv_w, conv_b, a_log, dskip, eexp, eye, conv0, h0, *, batch, seqlen):
    q = SSD_CHUNK
    t = xbc.shape[0]
    has_h0 = h0 is not None
    if has_h0:
        lq = seqlen
        nseq = q // lq
        grid = (t // q,)
        rmap = lambda i: (i, 0)
        smap3 = lambda i: (i, 0, 0)
        smap4 = lambda i: (i, 0, 0, 0)
    else:
        lq = q
        nseq = 1
        nc = seqlen // q
        grid = (batch, nc)
        rmap = lambda b, c: (b * nc + c, 0)
        smap3 = lambda b, c: (b, 0, 0)
        smap4 = lambda b, c: (b, 0, 0, 0)
    shifts = np.zeros((SSD_CONV - 1, q, q), np.float32)
    for k in range(SSD_CONV - 1):
        for r in range(q):
            if r % lq + k - (SSD_CONV - 1) >= 0:
                shifts[k, r, r + k - (SSD_CONV - 1)] = 1.0
    consts = [conv_w, conv_b, a_log, dskip, eexp, eye, jnp.asarray(shifts).astype(BF)]
    in_specs = [pl.BlockSpec((q, SSD_CONV_DIM), rmap), pl.BlockSpec((q, LANES), rmap)]
    in_specs += [_const_spec(c.shape) for c in consts]
    args = [xbc, dt] + consts
    if has_h0:
        in_specs += [pl.BlockSpec((nseq, SSD_CONV - 1, SSD_CONV_DIM), smap3),
                     pl.BlockSpec((nseq, SSD_HEADS, SSD_HEADDIM, SSD_STATE), smap4)]
        args += [conv0, h0]
    return pl.pallas_call(
        functools.partial(_ssd_kernel, lq=lq, has_h0=has_h0),
        grid=grid,
        in_specs=in_specs,
        out_specs=[pl.BlockSpec((q, D_MODEL), rmap),
                   pl.BlockSpec((nseq, SSD_CONV - 1, SSD_CONV_DIM), smap3),
                   pl.BlockSpec((nseq, SSD_HEADS, SSD_HEADDIM, SSD_STATE), smap4)],
        out_shape=[jax.ShapeDtypeStruct((t, D_MODEL), BF),
                   jax.ShapeDtypeStruct((batch, SSD_CONV - 1, SSD_CONV_DIM), F32),
                   jax.ShapeDtypeStruct((batch, SSD_HEADS, SSD_HEADDIM, SSD_STATE), F32)],
        scratch_shapes=[pltpu.VMEM((nseq, 16, SSD_CONV_DIM), F32),
                        pltpu.VMEM((q, SSD_CONV_DIM), F32),
                        pltpu.VMEM((q, D_MODEL), F32),
                        pltpu.VMEM((q, D_MODEL), F32),
                        pltpu.VMEM((D_MODEL, q), BF),
                        pltpu.VMEM((q, LANES), F32)],
        compiler_params=_params(len(grid)),
        name="ssd_sample" if has_h0 else "ssd_prompt",
    )(*args)


S5_SEQ_TILE = 8
S5_LANE_CHUNK = 1024


def _s5_kernel(*refs, tl, has_h0):
    bt = S5_SEQ_TILE
    m = bt * tl
    if has_h0:
        (u_ref, perm_ref, permt_ref, bblk_ref, cblk_ref, lre_ref, lim_ref, d_ref, h0re_ref, h0im_ref,
         y_ref, ore_ref, oim_ref, bre, bim, sre, sim) = refs
    else:
        (u_ref, perm_ref, permt_ref, bblk_ref, cblk_ref, lre_ref, lim_ref, d_ref,
         y_ref, ore_ref, oim_ref, bre, bim, sre, sim) = refs

    @pl.when(pl.program_id(1) == 0)
    def _():
        if has_h0:
            sre[...] = h0re_ref[...]
            sim[...] = h0im_ref[...]
        else:
            sre[...] = jnp.zeros_like(sre)
            sim[...] = jnp.zeros_like(sim)

    u = u_ref[...].reshape(m, D_MODEL)
    u_tm = _dot(perm_ref[...], u).astype(BF)
    half = S5_FLAT // S5_SLABS
    for j in range(S5_SLABS):
        r = _dot(u_tm[:, j * LANES:(j + 1) * LANES], bblk_ref[j])
        bre[:, j * half:(j + 1) * half] = r[:, :half]
        bim[:, j * half:(j + 1) * half] = r[:, half:]

    for jc in range(S5_FLAT // S5_LANE_CHUNK):
        sl = slice(jc * S5_LANE_CHUNK, (jc + 1) * S5_LANE_CHUNK)
        lr = lre_ref[:, sl]
        li = lim_ref[:, sl]

        def step(l, carry, sl=sl, lr=lr, li=li):
            sr, si = carry
            rows = pl.ds(pl.multiple_of(l * bt, bt), bt)
            nr = lr * sr - li * si + bre[rows, sl]
            ni = lr * si + li * sr + bim[rows, sl]
            bre[rows, sl] = nr
            bim[rows, sl] = ni
            return nr, ni

        sr, si = lax.fori_loop(0, tl, step, (sre[:, sl], sim[:, sl]))
        sre[:, sl] = sr
        sim[:, sl] = si

    ys = []
    for j in range(S5_SLABS):
        st = jnp.concatenate([bre[:, j * half:(j + 1) * half], bim[:, j * half:(j + 1) * half]], axis=1)
        ys.append(_dot(st.astype(BF), cblk_ref[j]))
    y_tm = jnp.concatenate(ys, axis=1).astype(BF)
    y_bm = _dot(permt_ref[...], y_tm) + d_ref[...] * u.astype(F32)
    y_ref[...] = y_bm.astype(y_ref.dtype).reshape(y_ref.shape)
    ore_ref[...] = sre[...]
    oim_ref[...] = sim[...]


def _s5(u, perm, permt, bblk, cblk, lre, lim, dvec, h0re, h0im, *, batch, seqlen, tl):
    bt = S5_SEQ_TILE
    m = bt * tl
    has_h0 = h0re is not None
    nb = batch // bt
    nt = seqlen // tl
    if u.ndim == 3:
        u_spec = pl.BlockSpec((bt, tl, D_MODEL), lambda b, t: (b, t, 0))
    else:
        u_spec = pl.BlockSpec((m, D_MODEL), lambda b, t: (b * nt + t, 0))
    st_spec = pl.BlockSpec((bt, S5_FLAT), lambda b, t: (b, 0))
    consts = [perm, permt, bblk, cblk, lre, lim, dvec]
    in_specs = [u_spec] + [_const_spec(c.shape) for c in consts]
    args = [u] + consts
    if has_h0:
        in_specs += [st_spec, st_spec]
        args += [h0re, h0im]
    return pl.pallas_call(
        functools.partial(_s5_kernel, tl=tl, has_h0=has_h0),
        grid=(nb, nt),
        in_specs=in_specs,
        out_specs=[u_spec, st_spec, st_spec],
        out_shape=[jax.ShapeDtypeStruct(u.shape, BF),
                   jax.ShapeDtypeStruct((batch, S5_FLAT), F32),
                   jax.ShapeDtypeStruct((batch, S5_FLAT), F32)],
        scratch_shapes=[pltpu.VMEM((m, S5_FLAT), F32), pltpu.VMEM((m, S5_FLAT), F32),
                        pltpu.VMEM((bt, S5_FLAT), F32), pltpu.VMEM((bt, S5_FLAT), F32)],
        compiler_params=_params(2),
        name="s5_sample" if has_h0 else "s5_prompt",
    )(*args)


def _mixout_kernel(y_ref, z_ref, y5_ref, ga_ref, gb_ref, x_ref, ng_ref, wa_ref, wglu_ref, bglu_ref, wmix_ref, o_ref):
    y = y_ref[...].astype(F32) * _silu(z_ref[...].astype(F32))
    yn = _rms(y, ng_ref[...]).astype(BF)
    branch_a = _dot(yn, wa_ref[...])
    glu = _dot(y5_ref[...], wglu_ref[...]) + bglu_ref[...]
    branch_b = glu[:, :D_MODEL] * _sigmoid(glu[:, D_MODEL:])
    merged = _sigmoid(ga_ref[...].astype(F32)) * branch_a + _sigmoid(gb_ref[...].astype(F32)) * branch_b
    o_ref[...] = x_ref[...] + _dot(merged.astype(BF), wmix_ref[...])


def _mixout(y, z, y5, ga, gb, x2d, ng, wa, wglu, bglu, wmix):
    t = x2d.shape[0]
    tm = TOKEN_TILE
    row = pl.BlockSpec((tm, D_MODEL), lambda i: (i, 0))
    consts = [ng, wa, wglu, bglu, wmix]
    return pl.pallas_call(
        _mixout_kernel,
        grid=(t // tm,),
        in_specs=[row] * 6 + [_const_spec(c.shape) for c in consts],
        out_specs=row,
        out_shape=jax.ShapeDtypeStruct((t, D_MODEL), F32),
        compiler_params=_params(1),
        name="mixout",
    )(y, z, y5, ga, gb, x2d, *consts)


def _attn_kernel(h_ref, g_ref, wq_ref, wo_ref, k_ref, v_ref, o_ref, obuf, *, nkv, rows_per_seq):
    h = h_ref[...]
    r = h.shape[0]
    qv = _dot(_rms(h, g_ref[...]).astype(BF), wq_ref[...]).astype(BF)
    heads_split = len(k_ref.shape) == 4

    def head_of(ref, hd):
        if heads_split:
            parts = [ref[i, :, hd, :] for i in range(nkv)]
            return (parts[0] if nkv == 1 else jnp.concatenate(parts, axis=0)).astype(BF)
        return ref[0, :, hd * MEM_HEAD_DIM:(hd + 1) * MEM_HEAD_DIM].astype(BF)

    if nkv > 1:
        rs = lax.shift_right_logical(lax.broadcasted_iota(I32, (r, nkv * MEM_LEN), 0), int(math.log2(rows_per_seq)))
        cq = lax.shift_right_logical(lax.broadcasted_iota(I32, (r, nkv * MEM_LEN), 1), int(math.log2(MEM_LEN)))
        visible = rs == cq
    scale = MEM_HEAD_DIM ** -0.5
    for hd in range(MEM_HEADS):
        sl = slice(hd * MEM_HEAD_DIM, (hd + 1) * MEM_HEAD_DIM)
        s = _dot_nt(qv[:, sl], head_of(k_ref, hd)) * scale
        if nkv > 1:
            s = jnp.where(visible, s, NEG_BIG)
        p = jnp.exp(s - jnp.max(s, axis=-1, keepdims=True))
        p = p / jnp.sum(p, axis=-1, keepdims=True)
        obuf[:, sl] = _dot(p.astype(BF), head_of(v_ref, hd))
    o_ref[...] = h + _dot(obuf[...].astype(BF), wo_ref[...])


def _attn(h2d, g, wq, wo, k3, v3, *, rows, nkv, rows_per_seq):
    t = h2d.shape[0]
    nsteps = t // rows
    steps_per_kv = nsteps // (k3.shape[0] // nkv)
    row = pl.BlockSpec((rows, D_MODEL), lambda i: (i, 0))
    if k3.ndim == 4:
        kv = pl.BlockSpec((nkv, MEM_LEN, MEM_HEADS, MEM_HEAD_DIM), lambda i: (i // steps_per_kv, 0, 0, 0))
    else:
        kv = pl.BlockSpec((nkv, MEM_LEN, D_MODEL), lambda i: (i // steps_per_kv, 0, 0))
    consts = [g, wq, wo]
    return pl.pallas_call(
        functools.partial(_attn_kernel, nkv=nkv, rows_per_seq=rows_per_seq),
        grid=(nsteps,),
        in_specs=[row] + [_const_spec(c.shape) for c in consts] + [kv, kv],
        out_specs=row,
        out_shape=jax.ShapeDtypeStruct((t, D_MODEL), F32),
        scratch_shapes=[pltpu.VMEM((rows, D_MODEL), F32)],
        compiler_params=_params(1),
        name="attn",
    )(h2d, *consts, k3, v3)


ROUTER_LANE0 = MOE_GROUPS
ROUTE_ROWS = 8


def _router_kernel(hp_ref, hs_ref, g_ref, wh_ref, wm_ref, wl_ref, b_ref, tri_ref, eye_ref, rows_ref, wt_ref, cnt_ref,
                   carry, *, np_steps):
    i = pl.program_id(0)

    @pl.when(i == 0)
    def _():
        carry[...] = jnp.zeros_like(carry)

    h = jnp.where(i < np_steps, hp_ref[...], hs_ref[...])
    xn = _rms(h, g_ref[...])
    xh, xm, xl = _split3(xn)
    wh, wm, wl = wh_ref[...], wm_ref[...], wl_ref[...]
    logits = (_dot(xh, wh) + _dot(xh, wm) + _dot(xm, wh) + _dot(xh, wl) + _dot(xm, wm) + _dot(xl, wh)) + b_ref[...]
    lane = lax.broadcasted_iota(I32, logits.shape, 1)
    is_g = lane < MOE_GROUPS
    lg = jnp.where(is_g, logits, NEG_BIG)
    mg = jnp.max(lg, axis=-1, keepdims=True)
    gidx = jnp.min(jnp.where(lg == mg, lane, LANES), axis=-1, keepdims=True)
    g_w = 1.0 / jnp.sum(jnp.where(is_g, jnp.exp(lg - mg), 0.0), axis=-1, keepdims=True)
    e_lane = lane - ROUTER_LANE0
    in_grp = ((e_lane >= 0) & (e_lane < MOE_EXPERTS)
              & (lax.shift_right_logical(jnp.maximum(e_lane, 0), 3) == gidx))
    le = jnp.where(in_grp, logits, NEG_BIG)
    m1 = jnp.max(le, axis=-1, keepdims=True)
    i1 = jnp.min(jnp.where(le == m1, lane, LANES), axis=-1, keepdims=True)
    le2 = jnp.where(lane == i1, NEG_BIG, le)
    m2 = jnp.max(le2, axis=-1, keepdims=True)
    i2 = jnp.min(jnp.where(le2 == m2, lane, LANES), axis=-1, keepdims=True)
    ratio = jnp.exp(m2 - m1)
    w1 = g_w / (1.0 + ratio)
    w2 = g_w * ratio / (1.0 + ratio)
    wt_ref[...] = jnp.where(lane == 0, w1, jnp.where(lane == 1, w2, 0.0))

    oh1 = jnp.where(lane == i1, 1.0, 0.0)
    oh2 = jnp.where(lane == i2, 1.0, 0.0)
    tri = tri_ref[...]
    c0 = carry[...]
    tot1 = jnp.sum(oh1, axis=0, keepdims=True)
    r1 = jnp.sum(oh1 * (_dot(tri, oh1.astype(BF)) + c0), axis=-1, keepdims=True)
    r2 = jnp.sum(oh2 * (_dot(tri, oh2.astype(BF)) + (c0 + tot1)), axis=-1, keepdims=True)
    c1 = c0 + tot1 + jnp.sum(oh2, axis=0, keepdims=True)
    carry[...] = c1
    cols = jnp.where(lane == 0, (i1 - ROUTER_LANE0).astype(F32),
                     jnp.where(lane == 1, (i2 - ROUTER_LANE0).astype(F32),
                               jnp.where(lane == 2, r1, jnp.where(lane == 3, r2, 0.0))))
    rows_ref[0] = _dot_exact(eye_ref[...], cols, nt=True).astype(I32)
    cnt_ref[...] = jnp.broadcast_to(c1, cnt_ref.shape).astype(I32)


def _router(hp, hs, g, wh, wm, wl, b):
    tm = TOKEN_TILE
    np_steps = hp.shape[0] // tm
    ns_steps = hs.shape[0] // tm
    t = hp.shape[0] + hs.shape[0]
    tri = jnp.asarray(np.tril(np.ones((tm, tm), np.float32), -1)).astype(BF)
    eye8 = jnp.asarray(np.eye(ROUTE_ROWS, LANES, dtype=np.float32)).astype(BF)
    consts = [g, wh, wm, wl, b, tri, eye8]
    return pl.pallas_call(
        functools.partial(_router_kernel, np_steps=np_steps),
        grid=(np_steps + ns_steps,),
        in_specs=[pl.BlockSpec((tm, D_MODEL), lambda i: (jnp.minimum(i, np_steps - 1), 0)),
                  pl.BlockSpec((tm, D_MODEL), lambda i: (jnp.maximum(i - np_steps, 0), 0))]
                 + [_const_spec(c.shape) for c in consts],
        out_specs=[pl.BlockSpec((1, ROUTE_ROWS, tm), lambda i: (i, 0, 0)),
                   pl.BlockSpec((tm, LANES), lambda i: (i, 0)), _const_spec((8, LANES))],
        out_shape=[jax.ShapeDtypeStruct((t // tm, ROUTE_ROWS, tm), I32), jax.ShapeDtypeStruct((t, LANES), F32),
                   jax.ShapeDtypeStruct((8, LANES), I32)],
        scratch_shapes=[pltpu.VMEM((1, LANES), F32)],
        compiler_params=_params(1),
        name="router",
    )(hp, hs, *consts)


SLABS = D_MODEL // LANES


def _token_rows(t):
    return pl.ds(pl.multiple_of(t * SLABS, SLABS), SLABS)


def _to_token_major(ref, x, base=0):
    tm = x.shape[0]
    for s in range(SLABS):
        ref[pl.ds(base + s, tm, stride=SLABS), :] = x[:, s * LANES:(s + 1) * LANES]


def _from_token_major(ref, tm, base=0):
    return jnp.concatenate([ref[pl.ds(base + s, tm, stride=SLABS), :] for s in range(SLABS)], axis=1)


def _sorted_row(ps_ref, rt_ref, k, t):
    return ps_ref[rt_ref[0, k, t]] + rt_ref[0, 2 + k, t]


def _scatter_kernel(ps_ref, ve_ref, pe_ref, hp_ref, hs_ref, g_ref, rt_ref, xs_ref, xbuf, zbuf, sem, zsem, *, np_steps):
    i = pl.program_id(0)
    tm = TOKEN_TILE

    tile_rows = MOE_ROW_TILE * SLABS
    n_tiles = xs_ref.shape[0] // tile_rows

    def pad_copy(r):
        return pltpu.make_async_copy(zbuf.at[pl.ds(0, SLABS)], xs_ref.at[_token_rows(r)], zsem)

    def tile_copy(j):
        return pltpu.make_async_copy(zbuf, xs_ref.at[pl.ds(pl.multiple_of(j * tile_rows, tile_rows), tile_rows)], zsem)

    @pl.when(i == 0)
    def _():
        zbuf[...] = jnp.zeros_like(zbuf)
        first_unused = pe_ref[MOE_EXPERTS - 1] // MOE_ROW_TILE

        def tile_start(j, c):
            tile_copy(j).start()
            return c

        def tile_wait(j, c):
            tile_copy(0).wait()
            return c

        lax.fori_loop(first_unused, n_tiles, tile_start, 0)
        lax.fori_loop(first_unused, n_tiles, tile_wait, 0)

        def per_expert(e, n):
            lo = ve_ref[e]
            hi = pe_ref[e]

            def body(r, c):
                pad_copy(r).start()
                return c

            lax.fori_loop(lo, hi, body, 0)
            return n + (hi - lo)

        n_pad = lax.fori_loop(0, MOE_EXPERTS, per_expert, 0)

        def pad_wait(r, c):
            pad_copy(0).wait()
            return c

        lax.fori_loop(0, n_pad, pad_wait, 0)

    h = jnp.where(i < np_steps, hp_ref[...], hs_ref[...])
    _to_token_major(xbuf, _rms(h, g_ref[...]))

    def row_copy(t, dst_row):
        return pltpu.make_async_copy(xbuf.at[_token_rows(t)], xs_ref.at[_token_rows(dst_row)], sem)

    def issue(t, c):
        row_copy(t, _sorted_row(ps_ref, rt_ref, 0, t)).start()
        row_copy(t, _sorted_row(ps_ref, rt_ref, 1, t)).start(priority=1)
        return c

    def wait(t, c):
        row_copy(0, 0).wait()
        return c

    lax.fori_loop(0, tm, issue, 0, unroll=8)
    lax.fori_loop(0, 2 * tm, wait, 0, unroll=16)


def _scatter(hp, hs, g, route, pstart, valid_end, pend, n_rows):
    tm = TOKEN_TILE
    np_steps = hp.shape[0] // tm
    ns_steps = hs.shape[0] // tm
    grid_spec = pltpu.PrefetchScalarGridSpec(
        num_scalar_prefetch=3,
        grid=(np_steps + ns_steps,),
        in_specs=[pl.BlockSpec((tm, D_MODEL), lambda i, *_: (jnp.minimum(i, np_steps - 1), 0)),
                  pl.BlockSpec((tm, D_MODEL), lambda i, *_: (jnp.maximum(i - np_steps, 0), 0)),
                  pl.BlockSpec(g.shape, lambda i, *_: (0, 0)),
                  pl.BlockSpec((1, ROUTE_ROWS, tm), lambda i, *_: (i, 0, 0), memory_space=pltpu.SMEM)],
        out_specs=pl.BlockSpec(memory_space=pl.ANY),
        scratch_shapes=[pltpu.VMEM((tm * SLABS, LANES), F32), pltpu.VMEM((MOE_ROW_TILE * SLABS, LANES), F32),
                        pltpu.SemaphoreType.DMA(()), pltpu.SemaphoreType.DMA(())],
    )
    return pl.pallas_call(
        functools.partial(_scatter_kernel, np_steps=np_steps),
        grid_spec=grid_spec,
        out_shape=jax.ShapeDtypeStruct((n_rows * SLABS, LANES), F32),
        compiler_params=_params(1),
        name="scatter",
    )(pstart, valid_end, pend, hp, hs, g, route)


def _expert_kernel(te_ref, nu_ref, x_ref, wg_ref, wu_ref, wd_ref, y_ref, wg16, wu16, wd16):
    i = pl.program_id(0)
    tm = MOE_ROW_TILE

    @pl.when(i < nu_ref[0])
    def _():
        @pl.when((i == 0) | (te_ref[i] != te_ref[jnp.maximum(i - 1, 0)]))
        def _():
            wg16[...] = wg_ref[0].astype(BF)
            wu16[...] = wu_ref[0].astype(BF)
            wd16[...] = wd_ref[0].astype(BF)

        x = _from_token_major(x_ref, tm).astype(BF)
        gate = _dot(x, wg16[...])
        up = _dot(x, wu16[...])
        _to_token_major(y_ref, _dot((_silu(gate) * up).astype(BF), wd16[...]))

    @pl.when(i >= nu_ref[0])
    def _():
        y_ref[...] = jnp.zeros_like(y_ref)


def _experts(x_sorted, tile_expert, n_used, wg, wu, wd):
    tm = MOE_ROW_TILE
    n_tiles = x_sorted.shape[0] // (tm * SLABS)
    used = lambda i, nu: jnp.minimum(i, nu[0] - 1)
    grid_spec = pltpu.PrefetchScalarGridSpec(
        num_scalar_prefetch=2,
        grid=(n_tiles,),
        in_specs=[pl.BlockSpec((tm * SLABS, LANES), lambda i, te, nu: (used(i, nu), 0)),
                  pl.BlockSpec((1, D_MODEL, MOE_D_FF), lambda i, te, nu: (te[used(i, nu)], 0, 0)),
                  pl.BlockSpec((1, D_MODEL, MOE_D_FF), lambda i, te, nu: (te[used(i, nu)], 0, 0)),
                  pl.BlockSpec((1, MOE_D_FF, D_MODEL), lambda i, te, nu: (te[used(i, nu)], 0, 0))],
        out_specs=pl.BlockSpec((tm * SLABS, LANES), lambda i, te, nu: (i, 0)),
        scratch_shapes=[pltpu.VMEM((D_MODEL, MOE_D_FF), BF), pltpu.VMEM((D_MODEL, MOE_D_FF), BF),
                        pltpu.VMEM((MOE_D_FF, D_MODEL), BF)],
    )
    return pl.pallas_call(
        _expert_kernel,
        grid_spec=grid_spec,
        out_shape=jax.ShapeDtypeStruct(x_sorted.shape, F32),
        compiler_params=_params(1),
        name="experts",
    )(tile_expert, n_used, x_sorted, wg, wu, wd)


def _combine_kernel(ps_ref, h_ref, wt_ref, g_ref, pcur_ref, pnext_ref, ys_ref, o_ref, ybuf, sem, *, n):
    i = pl.program_id(0)
    tm = TOKEN_TILE

    def row_copy(src_row, j, slot):
        return pltpu.make_async_copy(ys_ref.at[_token_rows(src_row)], ybuf.at[slot, _token_rows(j)], sem.at[slot])

    def issue(rt_ref, slot):
        def body(t, c):
            row_copy(_sorted_row(ps_ref, rt_ref, 0, t), t, slot).start()
            row_copy(_sorted_row(ps_ref, rt_ref, 1, t), tm + t, slot).start(priority=1)
            return c

        lax.fori_loop(0, tm, body, 0, unroll=8)

    def finish(slot):
        def wait(j, c):
            row_copy(0, 0, slot).wait()
            return c

        lax.fori_loop(0, 2 * tm, wait, 0, unroll=16)
        wt = wt_ref[...]
        y1 = _from_token_major(ybuf.at[slot], tm)
        y2 = _from_token_major(ybuf.at[slot], tm, base=tm * SLABS)
        o_ref[...] = _rms(h_ref[...] + wt[:, 0:1] * y1 + wt[:, 1:2] * y2, g_ref[...])

    @pl.when(i == 0)
    def _():
        issue(pcur_ref, 0)

    for slot in (0, 1):
        @pl.when(lax.rem(i, 2) == slot)
        def _(slot=slot):
            @pl.when(i + 1 < n)
            def _():
                issue(pnext_ref, 1 - slot)

            finish(slot)


def _combine(h2d, wt, g, route, pstart, y_sorted, *, tile_offset):
    t = h2d.shape[0]
    tm = TOKEN_TILE
    nsteps = t // tm
    route_spec = lambda f: pl.BlockSpec((1, ROUTE_ROWS, tm), f, memory_space=pltpu.SMEM)
    grid_spec = pltpu.PrefetchScalarGridSpec(
        num_scalar_prefetch=1,
        grid=(nsteps,),
        in_specs=[pl.BlockSpec((tm, D_MODEL), lambda i, ps: (i, 0)),
                  pl.BlockSpec((tm, LANES), lambda i, ps: (i + tile_offset, 0)),
                  pl.BlockSpec(g.shape, lambda i, ps: (0, 0)),
                  route_spec(lambda i, ps: (i + tile_offset, 0, 0)),
                  route_spec(lambda i, ps: (jnp.minimum(i + 1, nsteps - 1) + tile_offset, 0, 0)),
                  pl.BlockSpec(memory_space=pl.ANY)],
        out_specs=pl.BlockSpec((tm, D_MODEL), lambda i, ps: (i, 0)),
        scratch_shapes=[pltpu.VMEM((2, 2 * tm * SLABS, LANES), F32), pltpu.SemaphoreType.DMA((2,))],
    )
    return pl.pallas_call(
        functools.partial(_combine_kernel, n=nsteps),
        grid_spec=grid_spec,
        out_shape=jax.ShapeDtypeStruct((t, D_MODEL), F32),
        compiler_params=_params(1),
        name="combine",
    )(pstart, h2d, wt, g, route, route, y_sorted)


def _moe_plan(cnt, t):
    rt = MOE_ROW_TILE
    n_tiles = (2 * t + MOE_EXPERTS * (rt - 1) + rt - 1) // rt
    counts = cnt[0, ROUTER_LANE0:ROUTER_LANE0 + MOE_EXPERTS]
    padded = ((counts + rt - 1) // rt) * rt
    pend = jnp.cumsum(padded).astype(I32)
    pstart = pend - padded
    valid_end = (pstart + counts).astype(I32)
    tile_start = jnp.arange(n_tiles, dtype=I32) * rt
    tile_expert = jnp.sum((tile_start[:, None] >= pend[None, :]).astype(I32), axis=1)
    tile_expert = jnp.minimum(tile_expert, MOE_EXPERTS - 1).astype(I32)
    n_used = (pend[-1] // rt).astype(I32).reshape(1)
    return pstart, valid_end, pend, tile_expert, n_used, n_tiles * rt


GRANULE = 8
LOCAL_USED = 2 * TOKEN_TILE + MOE_EXPERTS * (GRANULE - 1)
LOCAL_ROWS = 1280
SPARE_GRANULES = (LOCAL_ROWS - LOCAL_USED) // GRANULE
TILE_GRANULES = MOE_ROW_TILE // GRANULE


def _route_kernel(hp_ref, hs_ref, g_ref, wcat_ref, wh_ref, b_ref, tri_ref, eye_ref, upper_ref,
                  xl_ref, idx_ref, wt_ref, seg_ref, *, np_steps):
    i = pl.program_id(0)
    h = jnp.where(i < np_steps, hp_ref[...], hs_ref[...])
    xn = _rms(h, g_ref[...])
    tm = xn.shape[0]
    xh = xn.astype(BF)
    xm = (xn - xh.astype(F32)).astype(BF)
    both = _dot(xh, wcat_ref[...])
    logits = both[:, :LANES] + both[:, LANES:] + _dot(xm, wh_ref[...]) + b_ref[...]
    lane = lax.broadcasted_iota(I32, logits.shape, 1)
    is_g = lane < MOE_GROUPS
    lg = jnp.where(is_g, logits, NEG_BIG)
    mg = jnp.max(lg, axis=-1, keepdims=True)
    gidx = jnp.min(jnp.where(lg == mg, lane, LANES), axis=-1, keepdims=True)
    g_w = 1.0 / jnp.sum(jnp.where(is_g, jnp.exp(lg - mg), 0.0), axis=-1, keepdims=True)
    e_lane = lane - ROUTER_LANE0
    in_grp = ((e_lane >= 0) & (e_lane < MOE_EXPERTS)
              & (lax.shift_right_logical(jnp.maximum(e_lane, 0), 3) == gidx))
    le = jnp.where(in_grp, logits, NEG_BIG)
    m1 = jnp.max(le, axis=-1, keepdims=True)
    i1 = jnp.min(jnp.where(le == m1, lane, LANES), axis=-1, keepdims=True)
    le2 = jnp.where(lane == i1, NEG_BIG, le)
    m2 = jnp.max(le2, axis=-1, keepdims=True)
    i2 = jnp.min(jnp.where(le2 == m2, lane, LANES), axis=-1, keepdims=True)
    ratio = jnp.exp(m2 - m1)
    w1 = g_w / (1.0 + ratio)
    w2 = g_w * ratio / (1.0 + ratio)
    wt_ref[...] = jnp.where(lane == 0, w1, jnp.where(lane == 1, w2, 0.0))

    oh1 = jnp.where(lane == i1, 1.0, 0.0)
    oh2 = jnp.where(lane == i2, 1.0, 0.0)
    tot1 = jnp.sum(oh1, axis=0, keepdims=True)
    cnt = (tot1 + jnp.sum(oh2, axis=0, keepdims=True)).astype(I32)
    padded = lax.shift_left(lax.shift_right_logical(cnt + (GRANULE - 1), 3), 3)
    pad8 = jnp.broadcast_to(padded.astype(F32), (8, LANES)).astype(BF)
    lstart = _dot(pad8, upper_ref[...])[0:1, :]
    tri = tri_ref[...]
    l1 = jnp.sum(oh1 * (_dot(tri, oh1.astype(BF)) + lstart), axis=-1, keepdims=True)
    l2 = jnp.sum(oh2 * (_dot(tri, oh2.astype(BF)) + (lstart + tot1)), axis=-1, keepdims=True)
    idx_ref[...] = jnp.where(lane == 0, l1.astype(I32), jnp.where(lane == 1, l2.astype(I32), 0))
    sub = lax.broadcasted_iota(I32, (8, LANES), 0)
    seg_ref[0] = jnp.where(sub == 0, jnp.broadcast_to(cnt, (8, LANES)),
                           jnp.where(sub == 1, jnp.broadcast_to(lstart.astype(I32), (8, LANES)), 0))

    cols = jnp.where(lane == 0, l1, jnp.where(lane == 1, l2, 0.0))
    rows = _dot_exact(eye_ref[...], cols, nt=True)
    r_id = lax.broadcasted_iota(I32, (LOCAL_ROWS, tm), 0).astype(F32)
    perm = jnp.where(r_id == rows[0:1, :], 1.0, 0.0) + jnp.where(r_id == rows[1:2, :], 1.0, 0.0)
    xl_ref[...] = _dot(perm.astype(BF), xh)


def _route(hp, hs, g, wcat, wh, b):
    tm = TOKEN_TILE
    np_steps = hp.shape[0] // tm
    ns_steps = hs.shape[0] // tm
    nt = np_steps + ns_steps
    tri = jnp.asarray(np.tril(np.ones((tm, tm), np.float32), -1)).astype(BF)
    eye8 = jnp.asarray(np.eye(8, LANES, dtype=np.float32)).astype(BF)
    upper = jnp.asarray(np.triu(np.ones((LANES, LANES), np.float32), 1)).astype(BF)
    consts = [g, wcat, wh, b, tri, eye8, upper]
    return pl.pallas_call(
        functools.partial(_route_kernel, np_steps=np_steps),
        grid=(nt,),
        in_specs=[pl.BlockSpec((tm, D_MODEL), lambda i: (jnp.minimum(i, np_steps - 1), 0)),
                  pl.BlockSpec((tm, D_MODEL), lambda i: (jnp.maximum(i - np_steps, 0), 0))]
                 + [_const_spec(c.shape) for c in consts],
        out_specs=[pl.BlockSpec((LOCAL_ROWS, D_MODEL), lambda i: (i, 0)),
                   pl.BlockSpec((tm, LANES), lambda i: (i, 0)),
                   pl.BlockSpec((tm, LANES), lambda i: (i, 0)),
                   pl.BlockSpec((1, 8, LANES), lambda i: (i, 0, 0))],
        out_shape=[jax.ShapeDtypeStruct((nt * LOCAL_ROWS, D_MODEL), F32),
                   jax.ShapeDtypeStruct((nt * tm, LANES), I32),
                   jax.ShapeDtypeStruct((nt * tm, LANES), F32),
                   jax.ShapeDtypeStruct((nt, 8, LANES), I32)],
        compiler_params=_params(1),
        name="route",
    )(hp, hs, *consts)


def _granule_plan(seg, n_tiles):
    cnt = seg[:, 0, ROUTER_LANE0:ROUTER_LANE0 + MOE_EXPERTS]
    lstart = seg[:, 1, ROUTER_LANE0:ROUTER_LANE0 + MOE_EXPERTS]
    ntt = cnt.shape[0]
    ng = (cnt + GRANULE - 1) // GRANULE
    cum = jnp.cumsum(ng, axis=0)
    first = (jnp.arange(ntt, dtype=I32)[:, None] * LOCAL_ROWS + lstart) // GRANULE - (cum - ng)
    total = cum[-1]
    tiles_e = (total + TILE_GRANULES - 1) // TILE_GRANULES
    tend = jnp.cumsum(tiles_e)
    tstart = tend - tiles_e
    tile = jnp.arange(n_tiles, dtype=I32)
    te = jnp.minimum(jnp.sum((tile[:, None] >= tend[None, :]).astype(I32), axis=1), MOE_EXPERTS - 1)
    n_used = tend[-1].astype(I32).reshape(1)
    sel = (te[:, None] == jnp.arange(MOE_EXPERTS, dtype=I32)[None, :]).astype(I32)
    pick = lambda table: jnp.sum(sel[:, None, :] * table[None, :, :], axis=2)
    cum_t, first_t = pick(cum), pick(first)
    off = (tile - jnp.sum(sel * tstart[None, :], axis=1))[:, None] * TILE_GRANULES + jnp.arange(TILE_GRANULES, dtype=I32)[None, :]
    valid = (off < jnp.sum(sel * total[None, :], axis=1)[:, None]) & (tile < n_used[0])[:, None]
    j = jnp.sum((cum_t[:, None, :] <= off[:, :, None]).astype(I32), axis=2)
    jsel = (j[:, :, None] == jnp.arange(ntt, dtype=I32)[None, None, :]).astype(I32)
    src = jnp.sum(jsel * first_t[:, None, :], axis=2) + off
    spare = lambda k: ((k // SPARE_GRANULES) * LOCAL_ROWS + LOCAL_USED) // GRANULE + k % SPARE_GRANULES
    assert (ntt - 1) * SPARE_GRANULES >= 2 * TILE_GRANULES
    zero_granule = spare((ntt - 1) * SPARE_GRANULES)
    trash = spare((tile % 2)[:, None] * TILE_GRANULES + jnp.arange(TILE_GRANULES, dtype=I32)[None, :])
    gsrc = jnp.where(valid, src, zero_granule).astype(I32).reshape(-1)
    gdst = jnp.where(valid, src, trash).astype(I32).reshape(-1)
    return te.astype(I32), n_used, gsrc, gdst


def _grouped_kernel(te_ref, nu_ref, gsrc_ref, gdst_ref, xin_ref, wg_ref, wu_ref, wd_ref, xout_ref,
                    xbuf, ybuf, wg16, wu16, wd16, sem_in, sem_out):
    i = pl.program_id(0)
    nu = nu_ref[0]
    tm = MOE_ROW_TILE

    def hbm_rows(table_ref, tile, g):
        return pl.ds(pl.multiple_of(table_ref[tile * TILE_GRANULES + g] * GRANULE, GRANULE), GRANULE)

    def fetch_start(tile, slot):
        for g in range(TILE_GRANULES):
            pltpu.make_async_copy(xin_ref.at[hbm_rows(gsrc_ref, tile, g)],
                                  xbuf.at[slot, pl.ds(g * GRANULE, GRANULE)], sem_in.at[slot]).start()

    def store_start(tile, slot):
        for g in range(TILE_GRANULES):
            pltpu.make_async_copy(ybuf.at[slot, pl.ds(g * GRANULE, GRANULE)],
                                  xout_ref.at[hbm_rows(gdst_ref, tile, g)], sem_out.at[slot]).start()

    def fetch_wait(slot):
        pltpu.make_async_copy(xin_ref.at[pl.ds(0, tm)], xbuf.at[slot], sem_in.at[slot]).wait()

    def store_wait(slot):
        pltpu.make_async_copy(ybuf.at[slot], xout_ref.at[pl.ds(0, tm)], sem_out.at[slot]).wait()

    def step(slot):
        @pl.when(i == 0)
        def _():
            fetch_start(0, 0)

        @pl.when(i + 1 < nu)
        def _():
            fetch_start(i + 1, 1 - slot)

        @pl.when(i >= 2)
        def _():
            store_wait(slot)

        @pl.when((i == 0) | (te_ref[i] != te_ref[jnp.maximum(i - 1, 0)]))
        def _():
            wg16[...] = wg_ref[0].astype(BF)
            wu16[...] = wu_ref[0].astype(BF)
            wd16[...] = wd_ref[0].astype(BF)

        fetch_wait(slot)
        x = xbuf[slot].astype(BF)
        gate = _dot(x, wg16[...])
        up = _dot(x, wu16[...])
        ybuf[slot] = _dot((_silu(gate) * up).astype(BF), wd16[...])
        store_start(i, slot)

        @pl.when(i == nu - 1)
        def _():
            store_wait(slot)

            @pl.when(i >= 1)
            def _():
                store_wait(1 - slot)

    for slot in (0, 1):
        @pl.when((i < nu) & (lax.rem(i, 2) == slot))
        def _(slot=slot):
            step(slot)


def _grouped(x_local, te, n_used, gsrc, gdst, wg, wu, wd):
    tm = MOE_ROW_TILE
    n_tiles = te.shape[0]
    used = lambda i, nu: jnp.minimum(i, jnp.maximum(nu[0] - 1, 0))
    wspec = lambda shape: pl.BlockSpec((1,) + shape, lambda i, te, nu, gs, gd: (te[used(i, nu)], 0, 0))
    grid_spec = pltpu.PrefetchScalarGridSpec(
        num_scalar_prefetch=4,
        grid=(n_tiles,),
        in_specs=[pl.BlockSpec(memory_space=pl.ANY), wspec((D_MODEL, MOE_D_FF)), wspec((D_MODEL, MOE_D_FF)),
                  wspec((MOE_D_FF, D_MODEL))],
        out_specs=pl.BlockSpec(memory_space=pl.ANY),
        scratch_shapes=[pltpu.VMEM((2, tm, D_MODEL), F32), pltpu.VMEM((2, tm, D_MODEL), F32),
                        pltpu.VMEM((D_MODEL, MOE_D_FF), BF), pltpu.VMEM((D_MODEL, MOE_D_FF), BF),
                        pltpu.VMEM((MOE_D_FF, D_MODEL), BF),
                        pltpu.SemaphoreType.DMA((2,)), pltpu.SemaphoreType.DMA((2,))],
    )
    return pl.pallas_call(
        _grouped_kernel,
        grid_spec=grid_spec,
        out_shape=jax.ShapeDtypeStruct(x_local.shape, F32),
        input_output_aliases={4: 0},
        compiler_params=_params(1),
        name="grouped",
    )(te, n_used, gsrc, gdst, x_local, wg, wu, wd)


def _merge_kernel(h_ref, idx_ref, wt_ref, g_ref, yl_ref, o_ref):
    tm = h_ref.shape[0]
    idx = idx_ref[...]
    wt = wt_ref[...]
    r_id = lax.broadcasted_iota(I32, (tm, LOCAL_ROWS), 1)
    sel = jnp.where(r_id == idx[:, 0:1], wt[:, 0:1], 0.0) + jnp.where(r_id == idx[:, 1:2], wt[:, 1:2], 0.0)
    moe = _dot(sel.astype(BF), yl_ref[...].astype(BF))
    o_ref[...] = _rms(h_ref[...] + moe, g_ref[...])


def _merge(h2d, idx, wt, g, y_local, *, tile_offset):
    t = h2d.shape[0]
    tm = TOKEN_TILE
    return pl.pallas_call(
        _merge_kernel,
        grid=(t // tm,),
        in_specs=[pl.BlockSpec((tm, D_MODEL), lambda i: (i, 0)),
                  pl.BlockSpec((tm, LANES), lambda i: (i + tile_offset, 0)),
                  pl.BlockSpec((tm, LANES), lambda i: (i + tile_offset, 0)),
                  _const_spec(g.shape),
                  pl.BlockSpec((LOCAL_ROWS, D_MODEL), lambda i: (i + tile_offset, 0))],
        out_specs=pl.BlockSpec((tm, D_MODEL), lambda i: (i, 0)),
        out_shape=jax.ShapeDtypeStruct((t, D_MODEL), F32),
        compiler_params=_params(1),
        name="merge",
    )(h2d, idx, wt, g, y_local)


def _perm_matrix(bt, tl):
    m = bt * tl
    p = np.zeros((m, m), np.float32)
    for l in range(tl):
        for b in range(bt):
            p[l * bt + b, b * tl + l] = 1.0
    return p


def _s5_operators(lam_re, lam_im, log_step, b_re, b_im, c_re, c_im):
    lam = lax.complex(lam_re, lam_im)
    step = jnp.exp(log_step)[:, None]
    lam_bar = jnp.exp(lam * step)
    b_bar = ((lam_bar - 1.0) / lam)[..., None] * lax.complex(b_re, b_im)
    gl = LANES // S5_GROUP
    eye = jnp.eye(gl, dtype=F32)

    def in_block(bpart):
        bp = bpart.reshape(S5_SLABS, gl, S5_STATE, S5_GROUP)
        blk = jnp.einsum("sgnk,gh->sgkhn", bp, eye)
        return blk.reshape(S5_SLABS, LANES, gl * S5_STATE)

    def out_block(cpart):
        cp = cpart.reshape(S5_SLABS, gl, S5_GROUP, S5_STATE)
        blk = jnp.einsum("sgkn,gh->sgnhk", cp, eye)
        return blk.reshape(S5_SLABS, gl * S5_STATE, LANES)

    bblk = jnp.concatenate([in_block(jnp.real(b_bar)), in_block(jnp.imag(b_bar))], axis=2).astype(BF)
    cblk = jnp.concatenate([out_block(c_re), out_block(-c_im)], axis=1).astype(BF)
    lre = jnp.broadcast_to(jnp.real(lam_bar).reshape(1, S5_FLAT), (S5_SEQ_TILE, S5_FLAT))
    lim = jnp.broadcast_to(jnp.imag(lam_bar).reshape(1, S5_FLAT), (S5_SEQ_TILE, S5_FLAT))
    return bblk, cblk, lre, lim


def _pad_lanes(v, width=LANES):
    return jnp.pad(v, [(0, 0)] * (v.ndim - 1) + [(0, width - v.shape[-1])])


def kernel(x_prompt, x_sample, state_conv, state_ssd, state_s5_re, state_s5_im, cache_mem_k, cache_mem_v, mem_prompt, norm_mix_g, w_in, conv_w, conv_b, ssd_dt_bias, ssd_a_log, ssd_d, ssd_norm_g, w_ssd_branch, s5_lambda_re, s5_lambda_im, s5_log_step, s5_b_re, s5_b_im, s5_c_re, s5_c_im, s5_d, w_glu, b_glu, w_mix_out, norm_mem_q_g, norm_mem_kv_g, w_mem_q, w_mem_k, w_mem_v, w_mem_o, norm_ffn_g, w_router_group, b_router_group, w_router_expert, b_router_expert, w_exp_gate, w_exp_up, w_exp_down, norm_final_g):
    bp, lp, _ = x_prompt.shape
    bs, ls, _ = x_sample.shape
    tp, ts = bp * lp, bs * ls
    row1 = lambda v: v.reshape(1, -1).astype(F32)

    w_in0 = w_in[0]
    o_xbc = D_MODEL
    o_dt = o_xbc + SSD_CONV_DIM
    o_u5 = o_dt + SSD_HEADS
    w_main = jnp.concatenate([w_in0[:, :o_dt], w_in0[:, o_u5:]], axis=1).astype(BF)
    w_dt = _pad_lanes(w_in0[:, o_dt:o_u5]).astype(BF)
    dt_bias = _pad_lanes(row1(ssd_dt_bias[0]))
    a_log = _pad_lanes(row1(ssd_a_log[0]))
    dskip = jnp.repeat(ssd_d[0].astype(F32), SSD_HEADDIM).reshape(1, D_MODEL)
    eexp_np = np.zeros((LANES, D_MODEL), np.float32)
    for h in range(SSD_HEADS):
        eexp_np[h, h * SSD_HEADDIM:(h + 1) * SSD_HEADDIM] = 1.0
    eexp = jnp.asarray(eexp_np).astype(BF)
    eye16 = jnp.asarray(np.eye(SSD_HEADS, LANES, dtype=np.float32)).astype(BF)
    bblk, cblk, lre, lim = _s5_operators(s5_lambda_re[0], s5_lambda_im[0], s5_log_step[0], s5_b_re[0],
                                         s5_b_im[0], s5_c_re[0], s5_c_im[0])
    s5d = row1(s5_d[0])
    w_kv = jnp.concatenate([w_mem_k[0], w_mem_v[0]], axis=1).astype(BF)
    w_r = _pad_lanes(jnp.concatenate([w_router_group[0], w_router_expert[0]], axis=1).astype(F32))
    wr_h, wr_m, wr_l = _split3(w_r)
    b_r = _pad_lanes(row1(jnp.concatenate([b_router_group[0], b_router_expert[0]])))
    wg = w_exp_gate[0].reshape(MOE_EXPERTS, D_MODEL, MOE_D_FF)
    wu = w_exp_up[0].reshape(MOE_EXPERTS, D_MODEL, MOE_D_FF)
    wd = w_exp_down[0].reshape(MOE_EXPERTS, MOE_D_FF, D_MODEL)

    def mixer(x2d, batch, seqlen, conv0, h0, s5re0, s5im0):
        z, xbc, u5, ga, gb, dt = _inproj(x2d, row1(norm_mix_g[0]), w_main, w_dt, dt_bias)
        y_ssd, new_conv, new_ssd = _ssd(xbc, dt, conv_w[0].astype(F32), row1(conv_b[0]), a_log, dskip, eexp, eye16,
                                        conv0, h0, batch=batch, seqlen=seqlen)
        if h0 is None:
            tl = S5_TIME_TILE
            u_in = u5.reshape(batch, seqlen, D_MODEL)
        else:
            tl = seqlen
            u_in = u5
        perm = _perm_matrix(S5_SEQ_TILE, tl)
        y5, new_re, new_im = _s5(u_in, jnp.asarray(perm).astype(BF), jnp.asarray(perm.T).astype(BF), bblk, cblk,
                                 lre, lim, s5d, s5re0, s5im0, batch=batch, seqlen=seqlen, tl=tl)
        h1 = _mixout(y_ssd, z, y5.reshape(-1, D_MODEL), ga, gb, x2d, row1(ssd_norm_g[0]),
                     w_ssd_branch[0].astype(BF), w_glu[0].astype(BF), row1(b_glu[0]), w_mix_out[0].astype(BF))
        return h1, new_conv, new_ssd, new_re, new_im

    xp2 = x_prompt.reshape(tp, D_MODEL)
    xs2 = x_sample.reshape(ts, D_MODEL)
    h1p, conv_p, ssd_p, re_p, im_p = mixer(xp2, bp, lp, None, None, None, None)
    h1s, conv_s, ssd_s, re_s, im_s = mixer(xs2, bs, ls, state_conv[0], state_ssd[0],
                                           state_s5_re[0].reshape(bs, S5_FLAT), state_s5_im[0].reshape(bs, S5_FLAT))

    mk_p, mv_p = _memkv(mem_prompt.reshape(bp * MEM_LEN, D_MODEL), row1(norm_mem_kv_g[0]), w_kv)
    gq = row1(norm_mem_q_g[0])
    wq = w_mem_q[0].astype(BF)
    wo = w_mem_o[0].astype(BF)
    h2p = _attn(h1p, gq, wq, wo, mk_p.reshape(bp, MEM_LEN, D_MODEL), mv_p.reshape(bp, MEM_LEN, D_MODEL),
                rows=TOKEN_TILE, nkv=1, rows_per_seq=lp)
    sample_seqs = 4
    h2s = _attn(h1s, gq, wq, wo, cache_mem_k[0], cache_mem_v[0], rows=sample_seqs * ls, nkv=sample_seqs,
                rows_per_seq=ls)

    g_ffn = row1(norm_ffn_g[0])
    x_local, r_idx, r_wt, seg = _route(h2p, h2s, g_ffn, jnp.concatenate([wr_h, wr_m], axis=1), wr_h, b_r)
    token_tiles = (tp + ts) // TOKEN_TILE
    max_granules = (2 * (tp + ts) + token_tiles * MOE_EXPERTS * (GRANULE - 1)) // GRANULE
    n_tiles = max_granules // TILE_GRANULES + MOE_EXPERTS
    te, n_used, gsrc, gdst = _granule_plan(seg, n_tiles)
    y_local = _grouped(x_local, te, n_used, gsrc, gdst, wg, wu, wd)
    gf = row1(norm_final_g)
    y_prompt = _merge(h2p, r_idx, r_wt, gf, y_local, tile_offset=0)
    y_sample = _merge(h2s, r_idx, r_wt, gf, y_local, tile_offset=tp // TOKEN_TILE)

    return (y_prompt.reshape(bp, lp, D_MODEL), y_sample.reshape(bs, ls, D_MODEL),
            conv_p[None], ssd_p[None],
            re_p.reshape(1, bp, S5_GROUPS, S5_STATE), im_p.reshape(1, bp, S5_GROUPS, S5_STATE),
            mk_p.reshape(1, bp, MEM_LEN, MEM_HEADS, MEM_HEAD_DIM), mv_p.reshape(1, bp, MEM_LEN, MEM_HEADS, MEM_HEAD_DIM),
            conv_s[None], ssd_s[None],
            re_s.reshape(1, bs, S5_GROUPS, S5_STATE), im_s.reshape(1, bs, S5_GROUPS, S5_STATE))
```

```python
import functools
import math

import numpy as np
import jax
import jax.numpy as jnp
from jax import lax
from jax.experimental import pallas as pl
from jax.experimental.pallas import tpu as pltpu

F32 = jnp.float32
BF = jnp.bfloat16
I32 = jnp.int32

D_MODEL = 1024
SSD_HEADS = 16
SSD_HEADDIM = 64
SSD_GROUPS = 2
SSD_STATE = 128
SSD_CONV = 4
SSD_CONV_DIM = 1536
HEADS_PER_GROUP = SSD_HEADS // SSD_GROUPS
S5_GROUPS = 64
S5_GROUP = 16
S5_STATE = 64
S5_FLAT = S5_GROUPS * S5_STATE
S5_SLABS = D_MODEL // 128
MEM_LEN = 256
MEM_HEADS = 4
MEM_HEAD_DIM = 256
MOE_GROUPS = 4
MOE_EXPERTS_PER_GROUP = 8
MOE_EXPERTS = MOE_GROUPS * MOE_EXPERTS_PER_GROUP
MOE_D_FF = 512
NORM_EPS = 1e-6

LANES = 128
VMEM_LIMIT_BYTES = 56 * 1024 * 1024
TOKEN_TILE = 512
SSD_CHUNK = 128
S5_TIME_TILE = 64
MOE_ROW_TILE = 512
NEG_BIG = -1e30


def _params(n_axes):
    return pltpu.CompilerParams(dimension_semantics=("arbitrary",) * n_axes,
                                vmem_limit_bytes=VMEM_LIMIT_BYTES)


def _const_spec(shape):
    nd = len(shape)
    return pl.BlockSpec(shape, lambda *_: (0,) * nd)


def _dot(a, b):
    return jnp.dot(a, b, preferred_element_type=F32)


def _dot_nt(a, b):
    return lax.dot_general(a, b, (((1,), (1,)), ((), ())), preferred_element_type=F32)


def _split3(v):
    h = v.astype(BF)
    r = v - h.astype(F32)
    m = r.astype(BF)
    l = (r - m.astype(F32)).astype(BF)
    return h, m, l


def _dot_exact(a_bf, v, nt=False):
    f = _dot_nt if nt else _dot
    h, m, l = _split3(v)
    return f(a_bf, h) + f(a_bf, m) + f(a_bf, l)


def _rms(x, g):
    return x * lax.rsqrt(jnp.mean(x * x, axis=-1, keepdims=True) + NORM_EPS) * g


def _sigmoid(x):
    return 1.0 / (1.0 + jnp.exp(-x))


def _silu(x):
    return x * _sigmoid(x)


def _proj_cols(xb, w_ref, o_ref, c0, width):
    for c in range(0, width, 512):
        ce = min(c + 512, width)
        o_ref[:, c:ce] = _dot(xb, w_ref[:, c0 + c:c0 + ce]).astype(o_ref.dtype)


def _inproj_kernel(x_ref, g_ref, w_ref, wdt_ref, dtb_ref, z_ref, xbc_ref, u5_ref, ga_ref, gb_ref, dt_ref):
    xb = _rms(x_ref[...], g_ref[...]).astype(BF)
    c0 = 0
    for o_ref in (z_ref, xbc_ref, u5_ref, ga_ref, gb_ref):
        width = o_ref.shape[1]
        _proj_cols(xb, w_ref, o_ref, c0, width)
        c0 += width
    raw = _dot(xb, wdt_ref[...]) + dtb_ref[...]
    dt_ref[...] = jnp.maximum(raw, 0.0) + jnp.log1p(jnp.exp(-jnp.abs(raw)))


def _inproj(x2d, g, w_main, w_dt, dt_bias):
    t = x2d.shape[0]
    tm = TOKEN_TILE
    widths = (D_MODEL, SSD_CONV_DIM, D_MODEL, D_MODEL, D_MODEL)
    row = lambda w: pl.BlockSpec((tm, w), lambda i: (i, 0))
    return pl.pallas_call(
        _inproj_kernel,
        grid=(t // tm,),
        in_specs=[row(D_MODEL), _const_spec(g.shape), _const_spec(w_main.shape), _const_spec(w_dt.shape),
                  _const_spec(dt_bias.shape)],
        out_specs=[row(w) for w in widths] + [row(LANES)],
        out_shape=[jax.ShapeDtypeStruct((t, w), BF) for w in widths] + [jax.ShapeDtypeStruct((t, LANES), F32)],
        compiler_params=_params(1),
        name="inproj",
    )(x2d, g, w_main, w_dt, dt_bias)


def _memkv_kernel(x_ref, g_ref, w_ref, k_ref, v_ref):
    xb = _rms(x_ref[...], g_ref[...]).astype(BF)
    _proj_cols(xb, w_ref, k_ref, 0, D_MODEL)
    _proj_cols(xb, w_ref, v_ref, D_MODEL, D_MODEL)


def _memkv(mem2d, g, w_kv):
    t = mem2d.shape[0]
    tm = TOKEN_TILE
    row = pl.BlockSpec((tm, D_MODEL), lambda i: (i, 0))
    return pl.pallas_call(
        _memkv_kernel,
        grid=(t // tm,),
        in_specs=[row, _const_spec(g.shape), _const_spec(w_kv.shape)],
        out_specs=[row, row],
        out_shape=[jax.ShapeDtypeStruct((t, D_MODEL), F32)] * 2,
        compiler_params=_params(1),
        name="memkv",
    )(mem2d, g, w_kv)


def _ssd_kernel(*refs, lq, has_h0):
    q = SSD_CHUNK
    nseq = q // lq
    if has_h0:
        (xbc_ref, dt_ref, cw_ref, cb_ref, alog_ref, dsk_ref, eexp_ref, eye_ref, shift_ref, conv0_ref, h0_ref,
         y_ref, convo_ref, ho_ref, hbuf, cacc, yacc, yint, xwt, tot_s) = refs
        hin_ref = h0_ref
        hbuf[...] = jnp.zeros_like(hbuf)
        hbuf[:, 0:SSD_CONV - 1, :] = conv0_ref[...]
    else:
        (xbc_ref, dt_ref, cw_ref, cb_ref, alog_ref, dsk_ref, eexp_ref, eye_ref, shift_ref,
         y_ref, convo_ref, ho_ref, hbuf, cacc, yacc, yint, xwt, tot_s) = refs
        hin_ref = ho_ref

        @pl.when(pl.program_id(1) == 0)
        def _():
            ho_ref[...] = jnp.zeros_like(ho_ref)
            hbuf[...] = jnp.zeros_like(hbuf)

    x16 = xbc_ref[...]
    xraw = x16.astype(F32)
    acc = cb_ref[...] + xraw * cw_ref[SSD_CONV - 1:SSD_CONV, :]
    for k in range(SSD_CONV - 1):
        acc = acc + _dot(shift_ref[k], x16) * cw_ref[k:k + 1, :]
    cacc[...] = acc
    for i in range(nseq):
        tail = xraw[(i + 1) * lq - (SSD_CONV - 1):(i + 1) * lq]
        convo_ref[i] = tail
        corr = hbuf[i, 0:8, :] * cw_ref[0:1, :]
        for k in range(1, SSD_CONV - 1):
            corr = corr + hbuf[i, k:k + 8, :] * cw_ref[k:k + 1, :]
        cacc[i * lq:i * lq + 8, :] += corr
        if not has_h0:
            hbuf[i, 0:SSD_CONV - 1, :] = tail
    xc = _silu(cacc[...])
    xs = xc[:, :D_MODEL]
    xs_bf = xs.astype(BF)
    bm_bf = xc[:, D_MODEL:D_MODEL + SSD_GROUPS * SSD_STATE].astype(BF)
    cm = xc[:, D_MODEL + SSD_GROUPS * SSD_STATE:]
    cm_bf = cm.astype(BF)

    dt = dt_ref[...]
    da = dt * (-jnp.exp(alog_ref[...]))
    ri = lax.broadcasted_iota(I32, (q, q), 0)
    ci = lax.broadcasted_iota(I32, (q, q), 1)
    if nseq == 1:
        causal = ci <= ri
    else:
        sh = int(math.log2(lq))
        same = lax.shift_right_logical(ri, sh) == lax.shift_right_logical(ci, sh)
        causal = same & (ci <= ri)
    lmat = jnp.where(causal, 1.0, 0.0).astype(BF)
    cs = _dot_exact(lmat, da)
    if nseq == 1:
        tot = jnp.broadcast_to(cs[q - 1:q, :], (q, LANES))
    else:
        tot = _dot_exact(jnp.where(same, 1.0, 0.0).astype(BF), da)
    tot_s[...] = tot
    eye = eye_ref[...]
    cs_t = _dot_exact(eye, cs, nt=True)
    dt_t = _dot_exact(eye, dt, nt=True)
    eexp = eexp_ref[...]
    ecs_h, ecs_m, _ = _split3(jnp.exp(cs))
    ecs_x = _dot(ecs_h, eexp) + _dot(ecs_m, eexp)
    wend_x = _dot((jnp.exp(tot - cs) * dt).astype(BF), eexp)

    lane = lax.broadcasted_iota(I32, (q, LANES), 1)
    for g in range(SSD_GROUPS):
        cbg = _dot_nt(cm_bf[:, g * SSD_STATE:(g + 1) * SSD_STATE], bm_bf[:, g * SSD_STATE:(g + 1) * SSD_STATE])
        for jp in range(HEADS_PER_GROUP // 2):
            j = g * (HEADS_PER_GROUP // 2) + jp
            ms = []
            for h in (2 * j, 2 * j + 1):
                col = jnp.broadcast_to(cs[:, h:h + 1], (q, q))
                row = jnp.broadcast_to(cs_t[h:h + 1, :], (q, q))
                dtr = jnp.broadcast_to(dt_t[h:h + 1, :], (q, q))
                ms.append((jnp.where(causal, jnp.exp(col - row), 0.0) * cbg * dtr).astype(BF))
            mp = jnp.concatenate(ms, axis=1)
            xp = xs_bf[:, j * LANES:(j + 1) * LANES]
            zero = jnp.zeros_like(xp)
            x2 = jnp.concatenate([jnp.where(lane < SSD_HEADDIM, xp, zero),
                                  jnp.where(lane >= SSD_HEADDIM, xp, zero)], axis=0)
            yacc[:, j * LANES:(j + 1) * LANES] = _dot(mp, x2)

    gw = HEADS_PER_GROUP * SSD_HEADDIM
    xw = xs * wend_x
    for g in range(SSD_GROUPS):
        xwt[g * gw:(g + 1) * gw, :] = xw[:, g * gw:(g + 1) * gw].T.astype(BF)

    if nseq == 1:
        for g in range(SSD_GROUPS):
            hin_g = hin_ref[0, g * HEADS_PER_GROUP:(g + 1) * HEADS_PER_GROUP].reshape(gw, SSD_STATE)
            yint[:, g * gw:(g + 1) * gw] = _dot_nt(cm_bf[:, g * SSD_STATE:(g + 1) * SSD_STATE], hin_g.astype(BF))
            s_new = _dot(xwt[g * gw:(g + 1) * gw, :], bm_bf[:, g * SSD_STATE:(g + 1) * SSD_STATE])
            for hl in range(HEADS_PER_GROUP):
                h = g * HEADS_PER_GROUP + hl
                dec = jnp.exp(jnp.broadcast_to(tot[0:1, h:h + 1], (SSD_HEADDIM, SSD_STATE)))
                ho_ref[0, h] = dec * hin_ref[0, h] + s_new[hl * SSD_HEADDIM:(hl + 1) * SSD_HEADDIM]
    else:
        yint[...] = jnp.zeros_like(yint)
        sh = int(math.log2(lq))
        rowseq = lax.shift_right_logical(lax.broadcasted_iota(I32, (q, SSD_STATE), 0), sh)
        colseq = lax.shift_right_logical(lax.broadcasted_iota(I32, (gw, q), 1), sh)

        def seq_body(i, carry):
            trow = tot_s[pl.ds(i * lq, 1), :]
            for g in range(SSD_GROUPS):
                hin_g = hin_ref[i, g * HEADS_PER_GROUP:(g + 1) * HEADS_PER_GROUP].reshape(gw, SSD_STATE)
                cmg = cm[:, g * SSD_STATE:(g + 1) * SSD_STATE]
                lhs = jnp.where(rowseq == i, cmg, 0.0).astype(BF)
                yint[:, g * gw:(g + 1) * gw] += _dot_nt(lhs, hin_g.astype(BF))
                xg = xwt[g * gw:(g + 1) * gw, :]
                xi = jnp.where(colseq == i, xg, jnp.zeros_like(xg))
                s_new = _dot(xi, bm_bf[:, g * SSD_STATE:(g + 1) * SSD_STATE])
                for hl in range(HEADS_PER_GROUP):
                    h = g * HEADS_PER_GROUP + hl
                    dec = jnp.exp(jnp.broadcast_to(trow[:, h:h + 1], (SSD_HEADDIM, SSD_STATE)))
                    ho_ref[i, h] = dec * hin_ref[i, h] + s_new[hl * SSD_HEADDIM:(hl + 1) * SSD_HEADDIM]
            return carry

        lax.fori_loop(0, nseq, seq_body, 0)

    y_ref[...] = (yacc[...] + yint[...] * ecs_x + dsk_ref[...] * xs).astype(y_ref.dtype)


def _ssd(xbc, dt, conv_w, conv_b, a_log, dskip, eexp, eye, conv0, h0, *, batch, seqlen):
    q = SSD_CHUNK
    t = xbc.shape[0]
    has_h0 = h0 is not None
    if has_h0:
        lq = seqlen
        nseq = q // lq
        grid = (t // q,)
        rmap = lambda i: (i, 0)
        smap3 = lambda i: (i, 0, 0)
        smap4 = lambda i: (i, 0, 0, 0)
    else:
        lq = q
        nseq = 1
        nc = seqlen // q
        grid = (batch, nc)
        rmap = lambda b, c: (b * nc + c, 0)
        smap3 = lambda b, c: (b, 0, 0)
        smap4 = lambda b, c: (b, 0, 0, 0)
    shifts = np.zeros((SSD_CONV - 1, q, q), np.float32)
    for k in range(SSD_CONV - 1):
        for r in range(q):
            if r % lq + k - (SSD_CONV - 1) >= 0:
                shifts[k, r, r + k - (SSD_CONV - 1)] = 1.0
    consts = [conv_w, conv_b, a_log, dskip, eexp, eye, jnp.asarray(shifts).astype(BF)]
    in_specs = [pl.BlockSpec((q, SSD_CONV_DIM), rmap), pl.BlockSpec((q, LANES), rmap)]
    in_specs += [_const_spec(c.shape) for c in consts]
    args = [xbc, dt] + consts
    if has_h0:
        in_specs += [pl.BlockSpec((nseq, SSD_CONV - 1, SSD_CONV_DIM), smap3),
                     pl.BlockSpec((nseq, SSD_HEADS, SSD_HEADDIM, SSD_STATE), smap4)]
        args += [conv0, h0]
    return pl.pallas_call(
        functools.partial(_ssd_kernel, lq=lq, has_h0=has_h0),
        grid=grid,
        in_specs=in_specs,
        out_specs=[pl.BlockSpec((q, D_MODEL), rmap),
                   pl.BlockSpec((nseq, SSD_CONV - 1, SSD_CONV_DIM), smap3),
                   pl.BlockSpec((nseq, SSD_HEADS, SSD_HEADDIM, SSD_STATE), smap4)],
        out_shape=[jax.ShapeDtypeStruct((t, D_MODEL), BF),
                   jax.ShapeDtypeStruct((batch, SSD_CONV - 1, SSD_CONV_DIM), F32),
                   jax.ShapeDtypeStruct((batch, SSD_HEADS, SSD_HEADDIM, SSD_STATE), F32)],
        scratch_shapes=[pltpu.VMEM((nseq, 16, SSD_CONV_DIM), F32),
                        pltpu.VMEM((q, SSD_CONV_DIM), F32),
                        pltpu.VMEM((q, D_MODEL), F32),
                        pltpu.VMEM((q, D_MODEL), F32),
                        pltpu.VMEM((D_MODEL, q), BF),
                        pltpu.VMEM((q, LANES), F32)],
        compiler_params=_params(len(grid)),
        name="ssd_sample" if has_h0 else "ssd_prompt",
    )(*args)


S5_SEQ_TILE = 8
S5_LANE_CHUNK = 1024


def _s5_kernel(*refs, tl, has_h0):
    bt = S5_SEQ_TILE
    m = bt * tl
    if has_h0:
        (u_ref, perm_ref, permt_ref, bblk_ref, cblk_ref, lre_ref, lim_ref, d_ref, h0re_ref, h0im_ref,
         y_ref, ore_ref, oim_ref, bre, bim, sre, sim) = refs
    else:
        (u_ref, perm_ref, permt_ref, bblk_ref, cblk_ref, lre_ref, lim_ref, d_ref,
         y_ref, ore_ref, oim_ref, bre, bim, sre, sim) = refs

    @pl.when(pl.program_id(1) == 0)
    def _():
        if has_h0:
            sre[...] = h0re_ref[...]
            sim[...] = h0im_ref[...]
        else:
            sre[...] = jnp.zeros_like(sre)
            sim[...] = jnp.zeros_like(sim)

    u = u_ref[...].reshape(m, D_MODEL)
    u_tm = _dot(perm_ref[...], u).astype(BF)
    half = S5_FLAT // S5_SLABS
    for j in range(S5_SLABS):
        r = _dot(u_tm[:, j * LANES:(j + 1) * LANES], bblk_ref[j])
        bre[:, j * half:(j + 1) * half] = r[:, :half]
        bim[:, j * half:(j + 1) * half] = r[:, half:]

    for jc in range(S5_FLAT // S5_LANE_CHUNK):
        sl = slice(jc * S5_LANE_CHUNK, (jc + 1) * S5_LANE_CHUNK)
        lr = lre_ref[:, sl]
        li = lim_ref[:, sl]

        def step(l, carry, sl=sl, lr=lr, li=li):
            sr, si = carry
            rows = pl.ds(pl.multiple_of(l * bt, bt), bt)
            nr = lr * sr - li * si + bre[rows, sl]
            ni = lr * si + li * sr + bim[rows, sl]
            bre[rows, sl] = nr
            bim[rows, sl] = ni
            return nr, ni

        sr, si = lax.fori_loop(0, tl, step, (sre[:, sl], sim[:, sl]), unroll=2)
        sre[:, sl] = sr
        sim[:, sl] = si

    ys = []
    for j in range(S5_SLABS):
        st = jnp.concatenate([bre[:, j * half:(j + 1) * half], bim[:, j * half:(j + 1) * half]], axis=1)
        ys.append(_dot(st.astype(BF), cblk_ref[j]))
    y_tm = jnp.concatenate(ys, axis=1).astype(BF)
    y_bm = _dot(permt_ref[...], y_tm) + d_ref[...] * u.astype(F32)
    y_ref[...] = y_bm.astype(y_ref.dtype).reshape(y_ref.shape)
    ore_ref[...] = sre[...]
    oim_ref[...] = sim[...]


def _s5(u, perm, permt, bblk, cblk, lre, lim, dvec, h0re, h0im, *, batch, seqlen, tl):
    bt = S5_SEQ_TILE
    m = bt * tl
    has_h0 = h0re is not None
    nb = batch // bt
    nt = seqlen // tl
    if u.ndim == 3:
        u_spec = pl.BlockSpec((bt, tl, D_MODEL), lambda b, t: (b, t, 0))
    else:
        u_spec = pl.BlockSpec((m, D_MODEL), lambda b, t: (b * nt + t, 0))
    st_spec = pl.BlockSpec((bt, S5_FLAT), lambda b, t: (b, 0))
    consts = [perm, permt, bblk, cblk, lre, lim, dvec]
    in_specs = [u_spec] + [_const_spec(c.shape) for c in consts]
    args = [u] + consts
    if has_h0:
        in_specs += [st_spec, st_spec]
        args += [h0re, h0im]
    return pl.pallas_call(
        functools.partial(_s5_kernel, tl=tl, has_h0=has_h0),
        grid=(nb, nt),
        in_specs=in_specs,
        out_specs=[u_spec, st_spec, st_spec],
        out_shape=[jax.ShapeDtypeStruct(u.shape, BF),
                   jax.ShapeDtypeStruct((batch, S5_FLAT), F32),
                   jax.ShapeDtypeStruct((batch, S5_FLAT), F32)],
        scratch_shapes=[pltpu.VMEM((m, S5_FLAT), F32), pltpu.VMEM((m, S5_FLAT), F32),
                        pltpu.VMEM((bt, S5_FLAT), F32), pltpu.VMEM((bt, S5_FLAT), F32)],
        compiler_params=_params(2),
        name="s5_sample" if has_h0 else "s5_prompt",
    )(*args)


def _mixout_kernel(y_ref, z_ref, y5_ref, ga_ref, gb_ref, x_ref, ng_ref, wa_ref, wglu_ref, bglu_ref, wmix_ref, o_ref):
    y = y_ref[...].astype(F32) * _silu(z_ref[...].astype(F32))
    yn = _rms(y, ng_ref[...]).astype(BF)
    branch_a = _dot(yn, wa_ref[...])
    glu = _dot(y5_ref[...], wglu_ref[...]) + bglu_ref[...]
    branch_b = glu[:, :D_MODEL] * _sigmoid(glu[:, D_MODEL:])
    merged = _sigmoid(ga_ref[...].astype(F32)) * branch_a + _sigmoid(gb_ref[...].astype(F32)) * branch_b
    o_ref[...] = x_ref[...] + _dot(merged.astype(BF), wmix_ref[...])


def _mixout(y, z, y5, ga, gb, x2d, ng, wa, wglu, bglu, wmix):
    t = x2d.shape[0]
    tm = TOKEN_TILE
    row = pl.BlockSpec((tm, D_MODEL), lambda i: (i, 0))
    consts = [ng, wa, wglu, bglu, wmix]
    return pl.pallas_call(
        _mixout_kernel,
        grid=(t // tm,),
        in_specs=[row] * 6 + [_const_spec(c.shape) for c in consts],
        out_specs=row,
        out_shape=jax.ShapeDtypeStruct((t, D_MODEL), F32),
        compiler_params=_params(1),
        name="mixout",
    )(y, z, y5, ga, gb, x2d, *consts)


def _attn_kernel(h_ref, g_ref, wq_ref, wo_ref, k_ref, v_ref, o_ref, obuf, *, nkv, rows_per_seq):
    h = h_ref[...]
    r = h.shape[0]
    qv = _dot(_rms(h, g_ref[...]).astype(BF), wq_ref[...]).astype(BF)
    heads_split = len(k_ref.shape) == 4

    def head_of(ref, hd):
        if heads_split:
            parts = [ref[i, :, hd, :] for i in range(nkv)]
            return (parts[0] if nkv == 1 else jnp.concatenate(parts, axis=0)).astype(BF)
        return ref[0, :, hd * MEM_HEAD_DIM:(hd + 1) * MEM_HEAD_DIM].astype(BF)

    if nkv > 1:
        rs = lax.shift_right_logical(lax.broadcasted_iota(I32, (r, nkv * MEM_LEN), 0), int(math.log2(rows_per_seq)))
        cq = lax.shift_right_logical(lax.broadcasted_iota(I32, (r, nkv * MEM_LEN), 1), int(math.log2(MEM_LEN)))
        visible = rs == cq
    scale = MEM_HEAD_DIM ** -0.5
    for hd in range(MEM_HEADS):
        sl = slice(hd * MEM_HEAD_DIM, (hd + 1) * MEM_HEAD_DIM)
        s = _dot_nt(qv[:, sl], head_of(k_ref, hd)) * scale
        if nkv > 1:
            s = jnp.where(visible, s, NEG_BIG)
        p = jnp.exp(s - jnp.max(s, axis=-1, keepdims=True))
        p = p / jnp.sum(p, axis=-1, keepdims=True)
        obuf[:, sl] = _dot(p.astype(BF), head_of(v_ref, hd))
    o_ref[...] = h + _dot(obuf[...].astype(BF), wo_ref[...])


def _attn(h2d, g, wq, wo, k3, v3, *, rows, nkv, rows_per_seq):
    t = h2d.shape[0]
    nsteps = t // rows
    steps_per_kv = nsteps // (k3.shape[0] // nkv)
    row = pl.BlockSpec((rows, D_MODEL), lambda i: (i, 0))
    if k3.ndim == 4:
        kv = pl.BlockSpec((nkv, MEM_LEN, MEM_HEADS, MEM_HEAD_DIM), lambda i: (i // steps_per_kv, 0, 0, 0))
    else:
        kv = pl.BlockSpec((nkv, MEM_LEN, D_MODEL), lambda i: (i // steps_per_kv, 0, 0))
    consts = [g, wq, wo]
    return pl.pallas_call(
        functools.partial(_attn_kernel, nkv=nkv, rows_per_seq=rows_per_seq),
        grid=(nsteps,),
        in_specs=[row] + [_const_spec(c.shape) for c in consts] + [kv, kv],
        out_specs=row,
        out_shape=jax.ShapeDtypeStruct((t, D_MODEL), F32),
        scratch_shapes=[pltpu.VMEM((rows, D_MODEL), F32)],
        compiler_params=_params(1),
        name="attn",
    )(h2d, *consts, k3, v3)


ROUTER_LANE0 = MOE_GROUPS
ROUTE_ROWS = 8


def _router_kernel(hp_ref, hs_ref, g_ref, wh_ref, wm_ref, wl_ref, b_ref, tri_ref, eye_ref, rows_ref, wt_ref, cnt_ref,
                   carry, *, np_steps):
    i = pl.program_id(0)

    @pl.when(i == 0)
    def _():
        carry[...] = jnp.zeros_like(carry)

    h = jnp.where(i < np_steps, hp_ref[...], hs_ref[...])
    xn = _rms(h, g_ref[...])
    xh, xm, xl = _split3(xn)
    wh, wm, wl = wh_ref[...], wm_ref[...], wl_ref[...]
    logits = (_dot(xh, wh) + _dot(xh, wm) + _dot(xm, wh) + _dot(xh, wl) + _dot(xm, wm) + _dot(xl, wh)) + b_ref[...]
    lane = lax.broadcasted_iota(I32, logits.shape, 1)
    is_g = lane < MOE_GROUPS
    lg = jnp.where(is_g, logits, NEG_BIG)
    mg = jnp.max(lg, axis=-1, keepdims=True)
    gidx = jnp.min(jnp.where(lg == mg, lane, LANES), axis=-1, keepdims=True)
    g_w = 1.0 / jnp.sum(jnp.where(is_g, jnp.exp(lg - mg), 0.0), axis=-1, keepdims=True)
    e_lane = lane - ROUTER_LANE0
    in_grp = ((e_lane >= 0) & (e_lane < MOE_EXPERTS)
              & (lax.shift_right_logical(jnp.maximum(e_lane, 0), 3) == gidx))
    le = jnp.where(in_grp, logits, NEG_BIG)
    m1 = jnp.max(le, axis=-1, keepdims=True)
    i1 = jnp.min(jnp.where(le == m1, lane, LANES), axis=-1, keepdims=True)
    le2 = jnp.where(lane == i1, NEG_BIG, le)
    m2 = jnp.max(le2, axis=-1, keepdims=True)
    i2 = jnp.min(jnp.where(le2 == m2, lane, LANES), axis=-1, keepdims=True)
    ratio = jnp.exp(m2 - m1)
    w1 = g_w / (1.0 + ratio)
    w2 = g_w * ratio / (1.0 + ratio)
    wt_ref[...] = jnp.where(lane == 0, w1, jnp.where(lane == 1, w2, 0.0))

    oh1 = jnp.where(lane == i1, 1.0, 0.0)
    oh2 = jnp.where(lane == i2, 1.0, 0.0)
    tri = tri_ref[...]
    c0 = carry[...]
    tot1 = jnp.sum(oh1, axis=0, keepdims=True)
    r1 = jnp.sum(oh1 * (_dot(tri, oh1.astype(BF)) + c0), axis=-1, keepdims=True)
    r2 = jnp.sum(oh2 * (_dot(tri, oh2.astype(BF)) + (c0 + tot1)), axis=-1, keepdims=True)
    c1 = c0 + tot1 + jnp.sum(oh2, axis=0, keepdims=True)
    carry[...] = c1
    cols = jnp.where(lane == 0, (i1 - ROUTER_LANE0).astype(F32),
                     jnp.where(lane == 1, (i2 - ROUTER_LANE0).astype(F32),
                               jnp.where(lane == 2, r1, jnp.where(lane == 3, r2, 0.0))))
    rows_ref[0] = _dot_exact(eye_ref[...], cols, nt=True).astype(I32)
    cnt_ref[...] = jnp.broadcast_to(c1, cnt_ref.shape).astype(I32)


def _router(hp, hs, g, wh, wm, wl, b):
    tm = TOKEN_TILE
    np_steps = hp.shape[0] // tm
    ns_steps = hs.shape[0] // tm
    t = hp.shape[0] + hs.shape[0]
    tri = jnp.asarray(np.tril(np.ones((tm, tm), np.float32), -1)).astype(BF)
    eye8 = jnp.asarray(np.eye(ROUTE_ROWS, LANES, dtype=np.float32)).astype(BF)
    consts = [g, wh, wm, wl, b, tri, eye8]
    return pl.pallas_call(
        functools.partial(_router_kernel, np_steps=np_steps),
        grid=(np_steps + ns_steps,),
        in_specs=[pl.BlockSpec((tm, D_MODEL), lambda i: (jnp.minimum(i, np_steps - 1), 0)),
                  pl.BlockSpec((tm, D_MODEL), lambda i: (jnp.maximum(i - np_steps, 0), 0))]
                 + [_const_spec(c.shape) for c in consts],
        out_specs=[pl.BlockSpec((1, ROUTE_ROWS, tm), lambda i: (i, 0, 0)),
                   pl.BlockSpec((tm, LANES), lambda i: (i, 0)), _const_spec((8, LANES))],
        out_shape=[jax.ShapeDtypeStruct((t // tm, ROUTE_ROWS, tm), I32), jax.ShapeDtypeStruct((t, LANES), F32),
                   jax.ShapeDtypeStruct((8, LANES), I32)],
        scratch_shapes=[pltpu.VMEM((1, LANES), F32)],
        compiler_params=_params(1),
        name="router",
    )(hp, hs, *consts)


SLABS = D_MODEL // LANES


def _token_rows(t):
    return pl.ds(pl.multiple_of(t * SLABS, SLABS), SLABS)


def _to_token_major(ref, x, base=0):
    tm = x.shape[0]
    for s in range(SLABS):
        ref[pl.ds(base + s, tm, stride=SLABS), :] = x[:, s * LANES:(s + 1) * LANES]


def _from_token_major(ref, tm, base=0):
    return jnp.concatenate([ref[pl.ds(base + s, tm, stride=SLABS), :] for s in range(SLABS)], axis=1)


def _sorted_row(ps_ref, rt_ref, k, t):
    return ps_ref[rt_ref[0, k, t]] + rt_ref[0, 2 + k, t]


def _scatter_kernel(ps_ref, ve_ref, pe_ref, hp_ref, hs_ref, g_ref, rt_ref, xs_ref, xbuf, zbuf, sem, zsem, *, np_steps):
    i = pl.program_id(0)
    tm = TOKEN_TILE

    tile_rows = MOE_ROW_TILE * SLABS
    n_tiles = xs_ref.shape[0] // tile_rows

    def pad_copy(r):
        return pltpu.make_async_copy(zbuf.at[pl.ds(0, SLABS)], xs_ref.at[_token_rows(r)], zsem)

    def tile_copy(j):
        return pltpu.make_async_copy(zbuf, xs_ref.at[pl.ds(pl.multiple_of(j * tile_rows, tile_rows), tile_rows)], zsem)

    @pl.when(i == 0)
    def _():
        zbuf[...] = jnp.zeros_like(zbuf)
        first_unused = pe_ref[MOE_EXPERTS - 1] // MOE_ROW_TILE

        def tile_start(j, c):
            tile_copy(j).start()
            return c

        def tile_wait(j, c):
            tile_copy(0).wait()
            return c

        lax.fori_loop(first_unused, n_tiles, tile_start, 0)
        lax.fori_loop(first_unused, n_tiles, tile_wait, 0)

        def per_expert(e, n):
            lo = ve_ref[e]
            hi = pe_ref[e]

            def body(r, c):
                pad_copy(r).start()
                return c

            lax.fori_loop(lo, hi, body, 0)
            return n + (hi - lo)

        n_pad = lax.fori_loop(0, MOE_EXPERTS, per_expert, 0)

        def pad_wait(r, c):
            pad_copy(0).wait()
            return c

        lax.fori_loop(0, n_pad, pad_wait, 0)

    h = jnp.where(i < np_steps, hp_ref[...], hs_ref[...])
    _to_token_major(xbuf, _rms(h, g_ref[...]))

    def row_copy(t, dst_row):
        return pltpu.make_async_copy(xbuf.at[_token_rows(t)], xs_ref.at[_token_rows(dst_row)], sem)

    def issue(t, c):
        row_copy(t, _sorted_row(ps_ref, rt_ref, 0, t)).start()
        row_copy(t, _sorted_row(ps_ref, rt_ref, 1, t)).start(priority=1)
        return c

    def wait(t, c):
        row_copy(0, 0).wait()
        return c

    lax.fori_loop(0, tm, issue, 0, unroll=8)
    lax.fori_loop(0, 2 * tm, wait, 0, unroll=16)


def _scatter(hp, hs, g, route, pstart, valid_end, pend, n_rows):
    tm = TOKEN_TILE
    np_steps = hp.shape[0] // tm
    ns_steps = hs.shape[0] // tm
    grid_spec = pltpu.PrefetchScalarGridSpec(
        num_scalar_prefetch=3,
        grid=(np_steps + ns_steps,),
        in_specs=[pl.BlockSpec((tm, D_MODEL), lambda i, *_: (jnp.minimum(i, np_steps - 1), 0)),
                  pl.BlockSpec((tm, D_MODEL), lambda i, *_: (jnp.maximum(i - np_steps, 0), 0)),
                  pl.BlockSpec(g.shape, lambda i, *_: (0, 0)),
                  pl.BlockSpec((1, ROUTE_ROWS, tm), lambda i, *_: (i, 0, 0), memory_space=pltpu.SMEM)],
        out_specs=pl.BlockSpec(memory_space=pl.ANY),
        scratch_shapes=[pltpu.VMEM((tm * SLABS, LANES), F32), pltpu.VMEM((MOE_ROW_TILE * SLABS, LANES), F32),
                        pltpu.SemaphoreType.DMA(()), pltpu.SemaphoreType.DMA(())],
    )
    return pl.pallas_call(
        functools.partial(_scatter_kernel, np_steps=np_steps),
        grid_spec=grid_spec,
        out_shape=jax.ShapeDtypeStruct((n_rows * SLABS, LANES), F32),
        compiler_params=_params(1),
        name="scatter",
    )(pstart, valid_end, pend, hp, hs, g, route)


def _expert_kernel(te_ref, nu_ref, x_ref, wg_ref, wu_ref, wd_ref, y_ref, wg16, wu16, wd16):
    i = pl.program_id(0)
    tm = MOE_ROW_TILE

    @pl.when(i < nu_ref[0])
    def _():
        @pl.when((i == 0) | (te_ref[i] != te_ref[jnp.maximum(i - 1, 0)]))
        def _():
            wg16[...] = wg_ref[0].astype(BF)
            wu16[...] = wu_ref[0].astype(BF)
            wd16[...] = wd_ref[0].astype(BF)

        x = _from_token_major(x_ref, tm).astype(BF)
        gate = _dot(x, wg16[...])
        up = _dot(x, wu16[...])
        _to_token_major(y_ref, _dot((_silu(gate) * up).astype(BF), wd16[...]))

    @pl.when(i >= nu_ref[0])
    def _():
        y_ref[...] = jnp.zeros_like(y_ref)


def _experts(x_sorted, tile_expert, n_used, wg, wu, wd):
    tm = MOE_ROW_TILE
    n_tiles = x_sorted.shape[0] // (tm * SLABS)
    used = lambda i, nu: jnp.minimum(i, nu[0] - 1)
    grid_spec = pltpu.PrefetchScalarGridSpec(
        num_scalar_prefetch=2,
        grid=(n_tiles,),
        in_specs=[pl.BlockSpec((tm * SLABS, LANES), lambda i, te, nu: (used(i, nu), 0)),
                  pl.BlockSpec((1, D_MODEL, MOE_D_FF), lambda i, te, nu: (te[used(i, nu)], 0, 0)),
                  pl.BlockSpec((1, D_MODEL, MOE_D_FF), lambda i, te, nu: (te[used(i, nu)], 0, 0)),
                  pl.BlockSpec((1, MOE_D_FF, D_MODEL), lambda i, te, nu: (te[used(i, nu)], 0, 0))],
        out_specs=pl.BlockSpec((tm * SLABS, LANES), lambda i, te, nu: (i, 0)),
        scratch_shapes=[pltpu.VMEM((D_MODEL, MOE_D_FF), BF), pltpu.VMEM((D_MODEL, MOE_D_FF), BF),
                        pltpu.VMEM((MOE_D_FF, D_MODEL), BF)],
    )
    return pl.pallas_call(
        _expert_kernel,
        grid_spec=grid_spec,
        out_shape=jax.ShapeDtypeStruct(x_sorted.shape, F32),
        compiler_params=_params(1),
        name="experts",
    )(tile_expert, n_used, x_sorted, wg, wu, wd)


def _combine_kernel(ps_ref, h_ref, wt_ref, g_ref, pcur_ref, pnext_ref, ys_ref, o_ref, ybuf, sem, *, n):
    i = pl.program_id(0)
    tm = TOKEN_TILE

    def row_copy(src_row, j, slot):
        return pltpu.make_async_copy(ys_ref.at[_token_rows(src_row)], ybuf.at[slot, _token_rows(j)], sem.at[slot])

    def issue(rt_ref, slot):
        def body(t, c):
            row_copy(_sorted_row(ps_ref, rt_ref, 0, t), t, slot).start()
            row_copy(_sorted_row(ps_ref, rt_ref, 1, t), tm + t, slot).start(priority=1)
            return c

        lax.fori_loop(0, tm, body, 0, unroll=8)

    def finish(slot):
        def wait(j, c):
            row_copy(0, 0, slot).wait()
            return c

        lax.fori_loop(0, 2 * tm, wait, 0, unroll=16)
        wt = wt_ref[...]
        y1 = _from_token_major(ybuf.at[slot], tm)
        y2 = _from_token_major(ybuf.at[slot], tm, base=tm * SLABS)
        o_ref[...] = _rms(h_ref[...] + wt[:, 0:1] * y1 + wt[:, 1:2] * y2, g_ref[...])

    @pl.when(i == 0)
    def _():
        issue(pcur_ref, 0)

    for slot in (0, 1):
        @pl.when(lax.rem(i, 2) == slot)
        def _(slot=slot):
            @pl.when(i + 1 < n)
            def _():
                issue(pnext_ref, 1 - slot)

            finish(slot)


def _combine(h2d, wt, g, route, pstart, y_sorted, *, tile_offset):
    t = h2d.shape[0]
    tm = TOKEN_TILE
    nsteps = t // tm
    route_spec = lambda f: pl.BlockSpec((1, ROUTE_ROWS, tm), f, memory_space=pltpu.SMEM)
    grid_spec = pltpu.PrefetchScalarGridSpec(
        num_scalar_prefetch=1,
        grid=(nsteps,),
        in_specs=[pl.BlockSpec((tm, D_MODEL), lambda i, ps: (i, 0)),
                  pl.BlockSpec((tm, LANES), lambda i, ps: (i + tile_offset, 0)),
                  pl.BlockSpec(g.shape, lambda i, ps: (0, 0)),
                  route_spec(lambda i, ps: (i + tile_offset, 0, 0)),
                  route_spec(lambda i, ps: (jnp.minimum(i + 1, nsteps - 1) + tile_offset, 0, 0)),
                  pl.BlockSpec(memory_space=pl.ANY)],
        out_specs=pl.BlockSpec((tm, D_MODEL), lambda i, ps: (i, 0)),
        scratch_shapes=[pltpu.VMEM((2, 2 * tm * SLABS, LANES), F32), pltpu.SemaphoreType.DMA((2,))],
    )
    return pl.pallas_call(
        functools.partial(_combine_kernel, n=nsteps),
        grid_spec=grid_spec,
        out_shape=jax.ShapeDtypeStruct((t, D_MODEL), F32),
        compiler_params=_params(1),
        name="combine",
    )(pstart, h2d, wt, g, route, route, y_sorted)


def _moe_plan(cnt, t):
    rt = MOE_ROW_TILE
    n_tiles = (2 * t + MOE_EXPERTS * (rt - 1) + rt - 1) // rt
    counts = cnt[0, ROUTER_LANE0:ROUTER_LANE0 + MOE_EXPERTS]
    padded = ((counts + rt - 1) // rt) * rt
    pend = jnp.cumsum(padded).astype(I32)
    pstart = pend - padded
    valid_end = (pstart + counts).astype(I32)
    tile_start = jnp.arange(n_tiles, dtype=I32) * rt
    tile_expert = jnp.sum((tile_start[:, None] >= pend[None, :]).astype(I32), axis=1)
    tile_expert = jnp.minimum(tile_expert, MOE_EXPERTS - 1).astype(I32)
    n_used = (pend[-1] // rt).astype(I32).reshape(1)
    return pstart, valid_end, pend, tile_expert, n_used, n_tiles * rt


GRANULE = 8
LOCAL_USED = 2 * TOKEN_TILE + MOE_EXPERTS * (GRANULE - 1)
LOCAL_ROWS = 1280
SPARE_GRANULES = (LOCAL_ROWS - LOCAL_USED) // GRANULE
TILE_GRANULES = MOE_ROW_TILE // GRANULE


def _route_kernel(hp_ref, hs_ref, g_ref, wcat_ref, wh_ref, b_ref, tri_ref, eye_ref, upper_ref,
                  xl_ref, idx_ref, wt_ref, seg_ref, *, np_steps):
    i = pl.program_id(0)
    h = jnp.where(i < np_steps, hp_ref[...], hs_ref[...])
    xn = _rms(h, g_ref[...])
    tm = xn.shape[0]
    xh = xn.astype(BF)
    xm = (xn - xh.astype(F32)).astype(BF)
    both = _dot(xh, wcat_ref[...])
    logits = both[:, :LANES] + both[:, LANES:] + _dot(xm, wh_ref[...]) + b_ref[...]
    lane = lax.broadcasted_iota(I32, logits.shape, 1)
    is_g = lane < MOE_GROUPS
    lg = jnp.where(is_g, logits, NEG_BIG)
    mg = jnp.max(lg, axis=-1, keepdims=True)
    gidx = jnp.min(jnp.where(lg == mg, lane, LANES), axis=-1, keepdims=True)
    g_w = 1.0 / jnp.sum(jnp.where(is_g, jnp.exp(lg - mg), 0.0), axis=-1, keepdims=True)
    e_lane = lane - ROUTER_LANE0
    in_grp = ((e_lane >= 0) & (e_lane < MOE_EXPERTS)
              & (lax.shift_right_logical(jnp.maximum(e_lane, 0), 3) == gidx))
    le = jnp.where(in_grp, logits, NEG_BIG)
    m1 = jnp.max(le, axis=-1, keepdims=True)
    i1 = jnp.min(jnp.where(le == m1, lane, LANES), axis=-1, keepdims=True)
    le2 = jnp.where(lane == i1, NEG_BIG, le)
    m2 = jnp.max(le2, axis=-1, keepdims=True)
    i2 = jnp.min(jnp.where(le2 == m2, lane, LANES), axis=-1, keepdims=True)
    ratio = jnp.exp(m2 - m1)
    w1 = g_w / (1.0 + ratio)
    w2 = g_w * ratio / (1.0 + ratio)
    wt_ref[...] = jnp.where(lane == 0, w1, jnp.where(lane == 1, w2, 0.0))

    oh1 = jnp.where(lane == i1, 1.0, 0.0)
    oh2 = jnp.where(lane == i2, 1.0, 0.0)
    tot1 = jnp.sum(oh1, axis=0, keepdims=True)
    cnt = (tot1 + jnp.sum(oh2, axis=0, keepdims=True)).astype(I32)
    padded = lax.shift_left(lax.shift_right_logical(cnt + (GRANULE - 1), 3), 3)
    pad8 = jnp.broadcast_to(padded.astype(F32), (8, LANES)).astype(BF)
    lstart = _dot(pad8, upper_ref[...])[0:1, :]
    tri = tri_ref[...]
    l1 = jnp.sum(oh1 * (_dot(tri, oh1.astype(BF)) + lstart), axis=-1, keepdims=True)
    l2 = jnp.sum(oh2 * (_dot(tri, oh2.astype(BF)) + (lstart + tot1)), axis=-1, keepdims=True)
    idx_ref[...] = jnp.where(lane == 0, l1.astype(I32), jnp.where(lane == 1, l2.astype(I32), 0))
    sub = lax.broadcasted_iota(I32, (8, LANES), 0)
    seg_ref[0] = jnp.where(sub == 0, jnp.broadcast_to(cnt, (8, LANES)),
                           jnp.where(sub == 1, jnp.broadcast_to(lstart.astype(I32), (8, LANES)), 0))

    cols = jnp.where(lane == 0, l1, jnp.where(lane == 1, l2, 0.0))
    rows = _dot_exact(eye_ref[...], cols, nt=True)
    r_id = lax.broadcasted_iota(I32, (LOCAL_ROWS, tm), 0).astype(F32)
    perm = jnp.where(r_id == rows[0:1, :], 1.0, 0.0) + jnp.where(r_id == rows[1:2, :], 1.0, 0.0)
    xl_ref[...] = _dot(perm.astype(BF), xh)


def _route(hp, hs, g, wcat, wh, b):
    tm = TOKEN_TILE
    np_steps = hp.shape[0] // tm
    ns_steps = hs.shape[0] // tm
    nt = np_steps + ns_steps
    tri = jnp.asarray(np.tril(np.ones((tm, tm), np.float32), -1)).astype(BF)
    eye8 = jnp.asarray(np.eye(8, LANES, dtype=np.float32)).astype(BF)
    upper = jnp.asarray(np.triu(np.ones((LANES, LANES), np.float32), 1)).astype(BF)
    consts = [g, wcat, wh, b, tri, eye8, upper]
    return pl.pallas_call(
        functools.partial(_route_kernel, np_steps=np_steps),
        grid=(nt,),
        in_specs=[pl.BlockSpec((tm, D_MODEL), lambda i: (jnp.minimum(i, np_steps - 1), 0)),
                  pl.BlockSpec((tm, D_MODEL), lambda i: (jnp.maximum(i - np_steps, 0), 0))]
                 + [_const_spec(c.shape) for c in consts],
        out_specs=[pl.BlockSpec((LOCAL_ROWS, D_MODEL), lambda i: (i, 0)),
                   pl.BlockSpec((tm, LANES), lambda i: (i, 0)),
                   pl.BlockSpec((tm, LANES), lambda i: (i, 0)),
                   pl.BlockSpec((1, 8, LANES), lambda i: (i, 0, 0))],
        out_shape=[jax.ShapeDtypeStruct((nt * LOCAL_ROWS, D_MODEL), F32),
                   jax.ShapeDtypeStruct((nt * tm, LANES), I32),
                   jax.ShapeDtypeStruct((nt * tm, LANES), F32),
                   jax.ShapeDtypeStruct((nt, 8, LANES), I32)],
        compiler_params=_params(1),
        name="route",
    )(hp, hs, *consts)


def _granule_plan(seg, n_tiles):
    cnt = seg[:, 0, ROUTER_LANE0:ROUTER_LANE0 + MOE_EXPERTS]
    lstart = seg[:, 1, ROUTER_LANE0:ROUTER_LANE0 + MOE_EXPERTS]
    ntt = cnt.shape[0]
    ng = (cnt + GRANULE - 1) // GRANULE
    cum = jnp.cumsum(ng, axis=0)
    first = (jnp.arange(ntt, dtype=I32)[:, None] * LOCAL_ROWS + lstart) // GRANULE - (cum - ng)
    total = cum[-1]
    tiles_e = (total + TILE_GRANULES - 1) // TILE_GRANULES
    tend = jnp.cumsum(tiles_e)
    tstart = tend - tiles_e
    tile = jnp.arange(n_tiles, dtype=I32)
    te = jnp.minimum(jnp.sum((tile[:, None] >= tend[None, :]).astype(I32), axis=1), MOE_EXPERTS - 1)
    n_used = tend[-1].astype(I32).reshape(1)
    sel = (te[:, None] == jnp.arange(MOE_EXPERTS, dtype=I32)[None, :]).astype(I32)
    pick = lambda table: jnp.sum(sel[:, None, :] * table[None, :, :], axis=2)
    cum_t, first_t = pick(cum), pick(first)
    off = (tile - jnp.sum(sel * tstart[None, :], axis=1))[:, None] * TILE_GRANULES + jnp.arange(TILE_GRANULES, dtype=I32)[None, :]
    valid = (off < jnp.sum(sel * total[None, :], axis=1)[:, None]) & (tile < n_used[0])[:, None]
    j = jnp.sum((cum_t[:, None, :] <= off[:, :, None]).astype(I32), axis=2)
    jsel = (j[:, :, None] == jnp.arange(ntt, dtype=I32)[None, None, :]).astype(I32)
    src = jnp.sum(jsel * first_t[:, None, :], axis=2) + off
    spare = lambda k: ((k // SPARE_GRANULES) * LOCAL_ROWS + LOCAL_USED) // GRANULE + k % SPARE_GRANULES
    assert (ntt - 1) * SPARE_GRANULES >= 2 * TILE_GRANULES
    zero_granule = spare((ntt - 1) * SPARE_GRANULES)
    trash = spare((tile % 2)[:, None] * TILE_GRANULES + jnp.arange(TILE_GRANULES, dtype=I32)[None, :])
    gsrc = jnp.where(valid, src, zero_granule).astype(I32).reshape(-1)
    gdst = jnp.where(valid, src, trash).astype(I32).reshape(-1)
    return te.astype(I32), n_used, gsrc, gdst


def _grouped_kernel(te_ref, nu_ref, gsrc_ref, gdst_ref, xin_ref, wg_ref, wu_ref, wd_ref, xout_ref,
                    xbuf, ybuf, wg16, wu16, wd16, sem_in, sem_out):
    i = pl.program_id(0)
    nu = nu_ref[0]
    tm = MOE_ROW_TILE

    def hbm_rows(table_ref, tile, g):
        return pl.ds(pl.multiple_of(table_ref[tile * TILE_GRANULES + g] * GRANULE, GRANULE), GRANULE)

    def fetch_start(tile, slot):
        for g in range(TILE_GRANULES):
            pltpu.make_async_copy(xin_ref.at[hbm_rows(gsrc_ref, tile, g)],
                                  xbuf.at[slot, pl.ds(g * GRANULE, GRANULE)], sem_in.at[slot]).start()

    def store_start(tile, slot):
        for g in range(TILE_GRANULES):
            pltpu.make_async_copy(ybuf.at[slot, pl.ds(g * GRANULE, GRANULE)],
                                  xout_ref.at[hbm_rows(gdst_ref, tile, g)], sem_out.at[slot]).start()

    def fetch_wait(slot):
        pltpu.make_async_copy(xin_ref.at[pl.ds(0, tm)], xbuf.at[slot], sem_in.at[slot]).wait()

    def store_wait(slot):
        pltpu.make_async_copy(ybuf.at[slot], xout_ref.at[pl.ds(0, tm)], sem_out.at[slot]).wait()

    def step(slot):
        @pl.when(i == 0)
        def _():
            fetch_start(0, 0)

        @pl.when(i + 1 < nu)
        def _():
            fetch_start(i + 1, 1 - slot)

        @pl.when(i >= 2)
        def _():
            store_wait(slot)

        @pl.when((i == 0) | (te_ref[i] != te_ref[jnp.maximum(i - 1, 0)]))
        def _():
            wg16[...] = wg_ref[0].astype(BF)
            wu16[...] = wu_ref[0].astype(BF)
            wd16[...] = wd_ref[0].astype(BF)

        fetch_wait(slot)
        x = xbuf[slot].astype(BF)
        gate = _dot(x, wg16[...])
        up = _dot(x, wu16[...])
        ybuf[slot] = _dot((_silu(gate) * up).astype(BF), wd16[...])
        store_start(i, slot)

        @pl.when(i == nu - 1)
        def _():
            store_wait(slot)

            @pl.when(i >= 1)
            def _():
                store_wait(1 - slot)

    for slot in (0, 1):
        @pl.when((i < nu) & (lax.rem(i, 2) == slot))
        def _(slot=slot):
            step(slot)


def _grouped(x_local, te, n_used, gsrc, gdst, wg, wu, wd):
    tm = MOE_ROW_TILE
    n_tiles = te.shape[0]
    used = lambda i, nu: jnp.minimum(i, jnp.maximum(nu[0] - 1, 0))
    wspec = lambda shape: pl.BlockSpec((1,) + shape, lambda i, te, nu, gs, gd: (te[used(i, nu)], 0, 0))
    grid_spec = pltpu.PrefetchScalarGridSpec(
        num_scalar_prefetch=4,
        grid=(n_tiles,),
        in_specs=[pl.BlockSpec(memory_space=pl.ANY), wspec((D_MODEL, MOE_D_FF)), wspec((D_MODEL, MOE_D_FF)),
                  wspec((MOE_D_FF, D_MODEL))],
        out_specs=pl.BlockSpec(memory_space=pl.ANY),
        scratch_shapes=[pltpu.VMEM((2, tm, D_MODEL), F32), pltpu.VMEM((2, tm, D_MODEL), F32),
                        pltpu.VMEM((D_MODEL, MOE_D_FF), BF), pltpu.VMEM((D_MODEL, MOE_D_FF), BF),
                        pltpu.VMEM((MOE_D_FF, D_MODEL), BF),
                        pltpu.SemaphoreType.DMA((2,)), pltpu.SemaphoreType.DMA((2,))],
    )
    return pl.pallas_call(
        _grouped_kernel,
        grid_spec=grid_spec,
        out_shape=jax.ShapeDtypeStruct(x_local.shape, F32),
        input_output_aliases={4: 0},
        compiler_params=_params(1),
        name="grouped",
    )(te, n_used, gsrc, gdst, x_local, wg, wu, wd)


def _merge_kernel(h_ref, idx_ref, wt_ref, g_ref, yl_ref, o_ref):
    tm = h_ref.shape[0]
    idx = idx_ref[...]
    wt = wt_ref[...]
    r_id = lax.broadcasted_iota(I32, (tm, LOCAL_ROWS), 1)
    sel = jnp.where(r_id == idx[:, 0:1], wt[:, 0:1], 0.0) + jnp.where(r_id == idx[:, 1:2], wt[:, 1:2], 0.0)
    moe = _dot(sel.astype(BF), yl_ref[...].astype(BF))
    o_ref[...] = _rms(h_ref[...] + moe, g_ref[...])


def _merge(h2d, idx, wt, g, y_local, *, tile_offset):
    t = h2d.shape[0]
    tm = TOKEN_TILE
    return pl.pallas_call(
        _merge_kernel,
        grid=(t // tm,),
        in_specs=[pl.BlockSpec((tm, D_MODEL), lambda i: (i, 0)),
                  pl.BlockSpec((tm, LANES), lambda i: (i + tile_offset, 0)),
                  pl.BlockSpec((tm, LANES), lambda i: (i + tile_offset, 0)),
                  _const_spec(g.shape),
                  pl.BlockSpec((LOCAL_ROWS, D_MODEL), lambda i: (i + tile_offset, 0))],
        out_specs=pl.BlockSpec((tm, D_MODEL), lambda i: (i, 0)),
        out_shape=jax.ShapeDtypeStruct((t, D_MODEL), F32),
        compiler_params=_params(1),
        name="merge",
    )(h2d, idx, wt, g, y_local)


def _perm_matrix(bt, tl):
    m = bt * tl
    p = np.zeros((m, m), np.float32)
    for l in range(tl):
        for b in range(bt):
            p[l * bt + b, b * tl + l] = 1.0
    return p


def _s5_operators(lam_re, lam_im, log_step, b_re, b_im, c_re, c_im):
    lam = lax.complex(lam_re, lam_im)
    step = jnp.exp(log_step)[:, None]
    lam_bar = jnp.exp(lam * step)
    b_bar = ((lam_bar - 1.0) / lam)[..., None] * lax.complex(b_re, b_im)
    gl = LANES // S5_GROUP
    eye = jnp.eye(gl, dtype=F32)

    def in_block(bpart):
        bp = bpart.reshape(S5_SLABS, gl, S5_STATE, S5_GROUP)
        blk = jnp.einsum("sgnk,gh->sgkhn", bp, eye)
        return blk.reshape(S5_SLABS, LANES, gl * S5_STATE)

    def out_block(cpart):
        cp = cpart.reshape(S5_SLABS, gl, S5_GROUP, S5_STATE)
        blk = jnp.einsum("sgkn,gh->sgnhk", cp, eye)
        return blk.reshape(S5_SLABS, gl * S5_STATE, LANES)

    bblk = jnp.concatenate([in_block(jnp.real(b_bar)), in_block(jnp.imag(b_bar))], axis=2).astype(BF)
    cblk = jnp.concatenate([out_block(c_re), out_block(-c_im)], axis=1).astype(BF)
    lre = jnp.broadcast_to(jnp.real(lam_bar).reshape(1, S5_FLAT), (S5_SEQ_TILE, S5_FLAT))
    lim = jnp.broadcast_to(jnp.imag(lam_bar).reshape(1, S5_FLAT), (S5_SEQ_TILE, S5_FLAT))
    return bblk, cblk, lre, lim


def _pad_lanes(v, width=LANES):
    return jnp.pad(v, [(0, 0)] * (v.ndim - 1) + [(0, width - v.shape[-1])])


def kernel(x_prompt, x_sample, state_conv, state_ssd, state_s5_re, state_s5_im, cache_mem_k, cache_mem_v, mem_prompt, norm_mix_g, w_in, conv_w, conv_b, ssd_dt_bias, ssd_a_log, ssd_d, ssd_norm_g, w_ssd_branch, s5_lambda_re, s5_lambda_im, s5_log_step, s5_b_re, s5_b_im, s5_c_re, s5_c_im, s5_d, w_glu, b_glu, w_mix_out, norm_mem_q_g, norm_mem_kv_g, w_mem_q, w_mem_k, w_mem_v, w_mem_o, norm_ffn_g, w_router_group, b_router_group, w_router_expert, b_router_expert, w_exp_gate, w_exp_up, w_exp_down, norm_final_g):
    bp, lp, _ = x_prompt.shape
    bs, ls, _ = x_sample.shape
    tp, ts = bp * lp, bs * ls
    row1 = lambda v: v.reshape(1, -1).astype(F32)

    w_in0 = w_in[0]
    o_xbc = D_MODEL
    o_dt = o_xbc + SSD_CONV_DIM
    o_u5 = o_dt + SSD_HEADS
    w_main = jnp.concatenate([w_in0[:, :o_dt], w_in0[:, o_u5:]], axis=1).astype(BF)
    w_dt = _pad_lanes(w_in0[:, o_dt:o_u5]).astype(BF)
    dt_bias = _pad_lanes(row1(ssd_dt_bias[0]))
    a_log = _pad_lanes(row1(ssd_a_log[0]))
    dskip = jnp.repeat(ssd_d[0].astype(F32), SSD_HEADDIM).reshape(1, D_MODEL)
    eexp_np = np.zeros((LANES, D_MODEL), np.float32)
    for h in range(SSD_HEADS):
        eexp_np[h, h * SSD_HEADDIM:(h + 1) * SSD_HEADDIM] = 1.0
    eexp = jnp.asarray(eexp_np).astype(BF)
    eye16 = jnp.asarray(np.eye(SSD_HEADS, LANES, dtype=np.float32)).astype(BF)
    bblk, cblk, lre, lim = _s5_operators(s5_lambda_re[0], s5_lambda_im[0], s5_log_step[0], s5_b_re[0],
                                         s5_b_im[0], s5_c_re[0], s5_c_im[0])
    s5d = row1(s5_d[0])
    w_kv = jnp.concatenate([w_mem_k[0], w_mem_v[0]], axis=1).astype(BF)
    w_r = _pad_lanes(jnp.concatenate([w_router_group[0], w_router_expert[0]], axis=1).astype(F32))
    wr_h, wr_m, wr_l = _split3(w_r)
    b_r = _pad_lanes(row1(jnp.concatenate([b_router_group[0], b_router_expert[0]])))
    wg = w_exp_gate[0].reshape(MOE_EXPERTS, D_MODEL, MOE_D_FF)
    wu = w_exp_up[0].reshape(MOE_EXPERTS, D_MODEL, MOE_D_FF)
    wd = w_exp_down[0].reshape(MOE_EXPERTS, MOE_D_FF, D_MODEL)

    def mixer(x2d, batch, seqlen, conv0, h0, s5re0, s5im0):
        z, xbc, u5, ga, gb, dt = _inproj(x2d, row1(norm_mix_g[0]), w_main, w_dt, dt_bias)
        y_ssd, new_conv, new_ssd = _ssd(xbc, dt, conv_w[0].astype(F32), row1(conv_b[0]), a_log, dskip, eexp, eye16,
                                        conv0, h0, batch=batch, seqlen=seqlen)
        if h0 is None:
            tl = S5_TIME_TILE
            u_in = u5.reshape(batch, seqlen, D_MODEL)
        else:
            tl = seqlen
            u_in = u5
        perm = _perm_matrix(S5_SEQ_TILE, tl)
        y5, new_re, new_im = _s5(u_in, jnp.asarray(perm).astype(BF), jnp.asarray(perm.T).astype(BF), bblk, cblk,
                                 lre, lim, s5d, s5re0, s5im0, batch=batch, seqlen=seqlen, tl=tl)
        h1 = _mixout(y_ssd, z, y5.reshape(-1, D_MODEL), ga, gb, x2d, row1(ssd_norm_g[0]),
                     w_ssd_branch[0].astype(BF), w_glu[0].astype(BF), row1(b_glu[0]), w_mix_out[0].astype(BF))
        return h1, new_conv, new_ssd, new_re, new_im

    xp2 = x_prompt.reshape(tp, D_MODEL)
    xs2 = x_sample.reshape(ts, D_MODEL)
    h1p, conv_p, ssd_p, re_p, im_p = mixer(xp2, bp, lp, None, None, None, None)
    h1s, conv_s, ssd_s, re_s, im_s = mixer(xs2, bs, ls, state_conv[0], state_ssd[0],
                                           state_s5_re[0].reshape(bs, S5_FLAT), state_s5_im[0].reshape(bs, S5_FLAT))

    mk_p, mv_p = _memkv(mem_prompt.reshape(bp * MEM_LEN, D_MODEL), row1(norm_mem_kv_g[0]), w_kv)
    gq = row1(norm_mem_q_g[0])
    wq = w_mem_q[0].astype(BF)
    wo = w_mem_o[0].astype(BF)
    h2p = _attn(h1p, gq, wq, wo, mk_p.reshape(bp, MEM_LEN, D_MODEL), mv_p.reshape(bp, MEM_LEN, D_MODEL),
                rows=TOKEN_TILE, nkv=1, rows_per_seq=lp)
    sample_seqs = 4
    h2s = _attn(h1s, gq, wq, wo, cache_mem_k[0], cache_mem_v[0], rows=sample_seqs * ls, nkv=sample_seqs,
                rows_per_seq=ls)

    g_ffn = row1(norm_ffn_g[0])
    x_local, r_idx, r_wt, seg = _route(h2p, h2s, g_ffn, jnp.concatenate([wr_h, wr_m], axis=1), wr_h, b_r)
    token_tiles = (tp + ts) // TOKEN_TILE
    max_granules = (2 * (tp + ts) + token_tiles * MOE_EXPERTS * (GRANULE - 1)) // GRANULE
    n_tiles = max_granules // TILE_GRANULES + MOE_EXPERTS
    te, n_used, gsrc, gdst = _granule_plan(seg, n_tiles)
    y_local = _grouped(x_local, te, n_used, gsrc, gdst, wg, wu, wd)
    gf = row1(norm_final_g)
    y_prompt = _merge(h2p, r_idx, r_wt, gf, y_local, tile_offset=0)
    y_sample = _merge(h2s, r_idx, r_wt, gf, y_local, tile_offset=tp // TOKEN_TILE)

    return (y_prompt.reshape(bp, lp, D_MODEL), y_sample.reshape(bs, ls, D_MODEL),
            conv_p[None], ssd_p[None],
            re_p.reshape(1, bp, S5_GROUPS, S5_STATE), im_p.reshape(1, bp, S5_GROUPS, S5_STATE),
            mk_p.reshape(1, bp, MEM_LEN, MEM_HEADS, MEM_HEAD_DIM), mv_p.reshape(1, bp, MEM_LEN, MEM_HEADS, MEM_HEAD_DIM),
            conv_s[None], ssd_s[None],
            re_s.reshape(1, bs, S5_GROUPS, S5_STATE), im_s.reshape(1, bs, S5_GROUPS, S5_STATE))
```

```python
import functools
import math

import numpy as np
import jax
import jax.numpy as jnp
from jax import lax
from jax.experimental import pallas as pl
from jax.experimental.pallas import tpu as pltpu

F32 = jnp.float32
BF = jnp.bfloat16
I32 = jnp.int32

D_MODEL = 1024
SSD_HEADS = 16
SSD_HEADDIM = 64
SSD_GROUPS = 2
SSD_STATE = 128
SSD_CONV = 4
SSD_CONV_DIM = 1536
HEADS_PER_GROUP = SSD_HEADS // SSD_GROUPS
S5_GROUPS = 64
S5_GROUP = 16
S5_STATE = 64
S5_FLAT = S5_GROUPS * S5_STATE
S5_SLABS = D_MODEL // 128
MEM_LEN = 256
MEM_HEADS = 4
MEM_HEAD_DIM = 256
MOE_GROUPS = 4
MOE_EXPERTS_PER_GROUP = 8
MOE_EXPERTS = MOE_GROUPS * MOE_EXPERTS_PER_GROUP
MOE_D_FF = 512
NORM_EPS = 1e-6

LANES = 128
VMEM_LIMIT_BYTES = 56 * 1024 * 1024
TOKEN_TILE = 512
SSD_CHUNK = 128
S5_TIME_TILE = 64
MOE_ROW_TILE = 512
NEG_BIG = -1e30


def _params(n_axes):
    return pltpu.CompilerParams(dimension_semantics=("arbitrary",) * n_axes,
                                vmem_limit_bytes=VMEM_LIMIT_BYTES)


def _const_spec(shape):
    nd = len(shape)
    return pl.BlockSpec(shape, lambda *_: (0,) * nd)


def _dot(a, b):
    return jnp.dot(a, b, preferred_element_type=F32)


def _dot_nt(a, b):
    return lax.dot_general(a, b, (((1,), (1,)), ((), ())), preferred_element_type=F32)


def _split3(v):
    h = v.astype(BF)
    r = v - h.astype(F32)
    m = r.astype(BF)
    l = (r - m.astype(F32)).astype(BF)
    return h, m, l


def _dot_exact(a_bf, v, nt=False):
    f = _dot_nt if nt else _dot
    h, m, l = _split3(v)
    return f(a_bf, h) + f(a_bf, m) + f(a_bf, l)


def _rms(x, g):
    return x * lax.rsqrt(jnp.mean(x * x, axis=-1, keepdims=True) + NORM_EPS) * g


def _sigmoid(x):
    return 0.5 * jnp.tanh(0.5 * x) + 0.5


def _silu(x):
    return x * _sigmoid(x)


def _proj_cols(xb, w_ref, o_ref, c0, width):
    for c in range(0, width, 512):
        ce = min(c + 512, width)
        o_ref[:, c:ce] = _dot(xb, w_ref[:, c0 + c:c0 + ce]).astype(o_ref.dtype)


def _inproj_kernel(x_ref, g_ref, w_ref, wdt_ref, dtb_ref, z_ref, xbc_ref, u5_ref, ga_ref, gb_ref, dt_ref):
    xb = _rms(x_ref[...], g_ref[...]).astype(BF)
    c0 = 0
    for o_ref in (z_ref, xbc_ref, u5_ref, ga_ref, gb_ref):
        width = o_ref.shape[1]
        _proj_cols(xb, w_ref, o_ref, c0, width)
        c0 += width
    raw = _dot(xb, wdt_ref[...]) + dtb_ref[...]
    dt_ref[...] = jnp.maximum(raw, 0.0) + jnp.log1p(jnp.exp(-jnp.abs(raw)))


def _inproj(x2d, g, w_main, w_dt, dt_bias):
    t = x2d.shape[0]
    tm = TOKEN_TILE
    widths = (D_MODEL, SSD_CONV_DIM, D_MODEL, D_MODEL, D_MODEL)
    row = lambda w: pl.BlockSpec((tm, w), lambda i: (i, 0))
    return pl.pallas_call(
        _inproj_kernel,
        grid=(t // tm,),
        in_specs=[row(D_MODEL), _const_spec(g.shape), _const_spec(w_main.shape), _const_spec(w_dt.shape),
                  _const_spec(dt_bias.shape)],
        out_specs=[row(w) for w in widths] + [row(LANES)],
        out_shape=[jax.ShapeDtypeStruct((t, w), BF) for w in widths] + [jax.ShapeDtypeStruct((t, LANES), F32)],
        compiler_params=_params(1),
        name="inproj",
    )(x2d, g, w_main, w_dt, dt_bias)


def _memkv_kernel(x_ref, g_ref, w_ref, k_ref, v_ref):
    xb = _rms(x_ref[...], g_ref[...]).astype(BF)
    _proj_cols(xb, w_ref, k_ref, 0, D_MODEL)
    _proj_cols(xb, w_ref, v_ref, D_MODEL, D_MODEL)


def _memkv(mem2d, g, w_kv):
    t = mem2d.shape[0]
    tm = TOKEN_TILE
    row = pl.BlockSpec((tm, D_MODEL), lambda i: (i, 0))
    return pl.pallas_call(
        _memkv_kernel,
        grid=(t // tm,),
        in_specs=[row, _const_spec(g.shape), _const_spec(w_kv.shape)],
        out_specs=[row, row],
        out_shape=[jax.ShapeDtypeStruct((t, D_MODEL), F32)] * 2,
        compiler_params=_params(1),
        name="memkv",
    )(mem2d, g, w_kv)


def _ssd_kernel(*refs, lq, has_h0):
    q = SSD_CHUNK
    nseq = q // lq
    if has_h0:
        (xbc_ref, dt_ref, cw_ref, cb_ref, alog_ref, dsk_ref, eexp_ref, eye_ref, shift_ref, conv0_ref, h0_ref,
         y_ref, convo_ref, ho_ref, hbuf, cacc, yacc, yint, xwt, tot_s) = refs
        hin_ref = h0_ref
        hbuf[...] = jnp.zeros_like(hbuf)
        hbuf[:, 0:SSD_CONV - 1, :] = conv0_ref[...]
    else:
        (xbc_ref, dt_ref, cw_ref, cb_ref, alog_ref, dsk_ref, eexp_ref, eye_ref, shift_ref,
         y_ref, convo_ref, ho_ref, hbuf, cacc, yacc, yint, xwt, tot_s) = refs
        hin_ref = ho_ref

        @pl.when(pl.program_id(1) == 0)
        def _():
            ho_ref[...] = jnp.zeros_like(ho_ref)
            hbuf[...] = jnp.zeros_like(hbuf)

    x16 = xbc_ref[...]
    xraw = x16.astype(F32)
    acc = cb_ref[...] + xraw * cw_ref[SSD_CONV - 1:SSD_CONV, :]
    for k in range(SSD_CONV - 1):
        acc = acc + _dot(shift_ref[k], x16) * cw_ref[k:k + 1, :]
    cacc[...] = acc
    for i in range(nseq):
        tail = xraw[(i + 1) * lq - (SSD_CONV - 1):(i + 1) * lq]
        convo_ref[i] = tail
        corr = hbuf[i, 0:8, :] * cw_ref[0:1, :]
        for k in range(1, SSD_CONV - 1):
            corr = corr + hbuf[i, k:k + 8, :] * cw_ref[k:k + 1, :]
        cacc[i * lq:i * lq + 8, :] += corr
        if not has_h0:
            hbuf[i, 0:SSD_CONV - 1, :] = tail
    xc = _silu(cacc[...])
    xs = xc[:, :D_MODEL]
    xs_bf = xs.astype(BF)
    bm_bf = xc[:, D_MODEL:D_MODEL + SSD_GROUPS * SSD_STATE].astype(BF)
    cm = xc[:, D_MODEL + SSD_GROUPS * SSD_STATE:]
    cm_bf = cm.astype(BF)

    dt = dt_ref[...]
    da = dt * (-jnp.exp(alog_ref[...]))
    ri = lax.broadcasted_iota(I32, (q, q), 0)
    ci = lax.broadcasted_iota(I32, (q, q), 1)
    if nseq == 1:
        causal = ci <= ri
    else:
        sh = int(math.log2(lq))
        same = lax.shift_right_logical(ri, sh) == lax.shift_right_logical(ci, sh)
        causal = same & (ci <= ri)
    lmat = jnp.where(causal, 1.0, 0.0).astype(BF)
    cs = _dot_exact(lmat, da)
    if nseq == 1:
        tot = jnp.broadcast_to(cs[q - 1:q, :], (q, LANES))
    else:
        tot = _dot_exact(jnp.where(same, 1.0, 0.0).astype(BF), da)
    tot_s[...] = tot
    eye = eye_ref[...]
    cs_t = _dot_exact(eye, cs, nt=True)
    dt_t = _dot_exact(eye, dt, nt=True)
    eexp = eexp_ref[...]
    ecs_h, ecs_m, _ = _split3(jnp.exp(cs))
    ecs_x = _dot(ecs_h, eexp) + _dot(ecs_m, eexp)
    wend_x = _dot((jnp.exp(tot - cs) * dt).astype(BF), eexp)

    lane = lax.broadcasted_iota(I32, (q, LANES), 1)
    for g in range(SSD_GROUPS):
        cbg = _dot_nt(cm_bf[:, g * SSD_STATE:(g + 1) * SSD_STATE], bm_bf[:, g * SSD_STATE:(g + 1) * SSD_STATE])
        for jp in range(HEADS_PER_GROUP // 2):
            j = g * (HEADS_PER_GROUP // 2) + jp
            ms = []
            for h in (2 * j, 2 * j + 1):
                col = jnp.broadcast_to(cs[:, h:h + 1], (q, q))
                row = jnp.broadcast_to(cs_t[h:h + 1, :], (q, q))
                dtr = jnp.broadcast_to(dt_t[h:h + 1, :], (q, q))
                ms.append((jnp.where(causal, jnp.exp(col - row), 0.0) * cbg * dtr).astype(BF))
            mp = jnp.concatenate(ms, axis=1)
            xp = xs_bf[:, j * LANES:(j + 1) * LANES]
            zero = jnp.zeros_like(xp)
            x2 = jnp.concatenate([jnp.where(lane < SSD_HEADDIM, xp, zero),
                                  jnp.where(lane >= SSD_HEADDIM, xp, zero)], axis=0)
            yacc[:, j * LANES:(j + 1) * LANES] = _dot(mp, x2)

    gw = HEADS_PER_GROUP * SSD_HEADDIM
    xw = xs * wend_x
    for g in range(SSD_GROUPS):
        xwt[g * gw:(g + 1) * gw, :] = xw[:, g * gw:(g + 1) * gw].T.astype(BF)

    if nseq == 1:
        for g in range(SSD_GROUPS):
            hin_g = hin_ref[0, g * HEADS_PER_GROUP:(g + 1) * HEADS_PER_GROUP].reshape(gw, SSD_STATE)
            yint[:, g * gw:(g + 1) * gw] = _dot_nt(cm_bf[:, g * SSD_STATE:(g + 1) * SSD_STATE], hin_g.astype(BF))
            s_new = _dot(xwt[g * gw:(g + 1) * gw, :], bm_bf[:, g * SSD_STATE:(g + 1) * SSD_STATE])
            for hl in range(HEADS_PER_GROUP):
                h = g * HEADS_PER_GROUP + hl
                dec = jnp.exp(jnp.broadcast_to(tot[0:1, h:h + 1], (SSD_HEADDIM, SSD_STATE)))
                ho_ref[0, h] = dec * hin_ref[0, h] + s_new[hl * SSD_HEADDIM:(hl + 1) * SSD_HEADDIM]
    else:
        yint[...] = jnp.zeros_like(yint)
        sh = int(math.log2(lq))
        rowseq = lax.shift_right_logical(lax.broadcasted_iota(I32, (q, SSD_STATE), 0), sh)
        colseq = lax.shift_right_logical(lax.broadcasted_iota(I32, (gw, q), 1), sh)

        def seq_body(i, carry):
            trow = tot_s[pl.ds(i * lq, 1), :]
            for g in range(SSD_GROUPS):
                hin_g = hin_ref[i, g * HEADS_PER_GROUP:(g + 1) * HEADS_PER_GROUP].reshape(gw, SSD_STATE)
                cmg = cm[:, g * SSD_STATE:(g + 1) * SSD_STATE]
                lhs = jnp.where(rowseq == i, cmg, 0.0).astype(BF)
                yint[:, g * gw:(g + 1) * gw] += _dot_nt(lhs, hin_g.astype(BF))
                xg = xwt[g * gw:(g + 1) * gw, :]
                xi = jnp.where(colseq == i, xg, jnp.zeros_like(xg))
                s_new = _dot(xi, bm_bf[:, g * SSD_STATE:(g + 1) * SSD_STATE])
                for hl in range(HEADS_PER_GROUP):
                    h = g * HEADS_PER_GROUP + hl
                    dec = jnp.exp(jnp.broadcast_to(trow[:, h:h + 1], (SSD_HEADDIM, SSD_STATE)))
                    ho_ref[i, h] = dec * hin_ref[i, h] + s_new[hl * SSD_HEADDIM:(hl + 1) * SSD_HEADDIM]
            return carry

        lax.fori_loop(0, nseq, seq_body, 0)

    y_ref[...] = (yacc[...] + yint[...] * ecs_x + dsk_ref[...] * xs).astype(y_ref.dtype)


def _ssd(xbc, dt, conv_w, conv_b, a_log, dskip, eexp, eye, conv0, h0, *, batch, seqlen):
    q = SSD_CHUNK
    t = xbc.shape[0]
    has_h0 = h0 is not None
    if has_h0:
        lq = seqlen
        nseq = q // lq
        grid = (t // q,)
        rmap = lambda i: (i, 0)
        smap3 = lambda i: (i, 0, 0)
        smap4 = lambda i: (i, 0, 0, 0)
    else:
        lq = q
        nseq = 1
        nc = seqlen // q
        grid = (batch, nc)
        rmap = lambda b, c: (b * nc + c, 0)
        smap3 = lambda b, c: (b, 0, 0)
        smap4 = lambda b, c: (b, 0, 0, 0)
    shifts = np.zeros((SSD_CONV - 1, q, q), np.float32)
    for k in range(SSD_CONV - 1):
        for r in range(q):
            if r % lq + k - (SSD_CONV - 1) >= 0:
                shifts[k, r, r + k - (SSD_CONV - 1)] = 1.0
    consts = [conv_w, conv_b, a_log, dskip, eexp, eye, jnp.asarray(shifts).astype(BF)]
    in_specs = [pl.BlockSpec((q, SSD_CONV_DIM), rmap), pl.BlockSpec((q, LANES), rmap)]
    in_specs += [_const_spec(c.shape) for c in consts]
    args = [xbc, dt] + consts
    if has_h0:
        in_specs += [pl.BlockSpec((nseq, SSD_CONV - 1, SSD_CONV_DIM), smap3),
                     pl.BlockSpec((nseq, SSD_HEADS, SSD_HEADDIM, SSD_STATE), smap4)]
        args += [conv0, h0]
    return pl.pallas_call(
        functools.partial(_ssd_kernel, lq=lq, has_h0=has_h0),
        grid=grid,
        in_specs=in_specs,
        out_specs=[pl.BlockSpec((q, D_MODEL), rmap),
                   pl.BlockSpec((nseq, SSD_CONV - 1, SSD_CONV_DIM), smap3),
                   pl.BlockSpec((nseq, SSD_HEADS, SSD_HEADDIM, SSD_STATE), smap4)],
        out_shape=[jax.ShapeDtypeStruct((t, D_MODEL), BF),
                   jax.ShapeDtypeStruct((batch, SSD_CONV - 1, SSD_CONV_DIM), F32),
                   jax.ShapeDtypeStruct((batch, SSD_HEADS, SSD_HEADDIM, SSD_STATE), F32)],
        scratch_shapes=[pltpu.VMEM((nseq, 16, SSD_CONV_DIM), F32),
                        pltpu.VMEM((q, SSD_CONV_DIM), F32),
                        pltpu.VMEM((q, D_MODEL), F32),
                        pltpu.VMEM((q, D_MODEL), F32),
                        pltpu.VMEM((D_MODEL, q), BF),
                        pltpu.VMEM((q, LANES), F32)],
        compiler_params=_params(len(grid)),
        name="ssd_sample" if has_h0 else "ssd_prompt",
    )(*args)


S5_SEQ_TILE = 8
S5_LANE_CHUNK = 1024


def _s5_kernel(*refs, tl, has_h0):
    bt = S5_SEQ_TILE
    m = bt * tl
    if has_h0:
        (u_ref, perm_ref, permt_ref, bblk_ref, cblk_ref, lre_ref, lim_ref, d_ref, h0re_ref, h0im_ref,
         y_ref, ore_ref, oim_ref, bre, bim, sre, sim) = refs
    else:
        (u_ref, perm_ref, permt_ref, bblk_ref, cblk_ref, lre_ref, lim_ref, d_ref,
         y_ref, ore_ref, oim_ref, bre, bim, sre, sim) = refs

    @pl.when(pl.program_id(1) == 0)
    def _():
        if has_h0:
            sre[...] = h0re_ref[...]
            sim[...] = h0im_ref[...]
        else:
            sre[...] = jnp.zeros_like(sre)
            sim[...] = jnp.zeros_like(sim)

    u = u_ref[...].reshape(m, D_MODEL)
    u_tm = _dot(perm_ref[...], u).astype(BF)
    half = S5_FLAT // S5_SLABS
    for j in range(S5_SLABS):
        r = _dot(u_tm[:, j * LANES:(j + 1) * LANES], bblk_ref[j])
        bre[:, j * half:(j + 1) * half] = r[:, :half]
        bim[:, j * half:(j + 1) * half] = r[:, half:]

    for jc in range(S5_FLAT // S5_LANE_CHUNK):
        sl = slice(jc * S5_LANE_CHUNK, (jc + 1) * S5_LANE_CHUNK)
        lr = lre_ref[:, sl]
        li = lim_ref[:, sl]

        def step(l, carry, sl=sl, lr=lr, li=li):
            sr, si = carry
            rows = pl.ds(pl.multiple_of(l * bt, bt), bt)
            nr = lr * sr - li * si + bre[rows, sl]
            ni = lr * si + li * sr + bim[rows, sl]
            bre[rows, sl] = nr
            bim[rows, sl] = ni
            return nr, ni

        sr, si = lax.fori_loop(0, tl, step, (sre[:, sl], sim[:, sl]), unroll=2)
        sre[:, sl] = sr
        sim[:, sl] = si

    ys = []
    for j in range(S5_SLABS):
        st = jnp.concatenate([bre[:, j * half:(j + 1) * half], bim[:, j * half:(j + 1) * half]], axis=1)
        ys.append(_dot(st.astype(BF), cblk_ref[j]))
    y_tm = jnp.concatenate(ys, axis=1).astype(BF)
    y_bm = _dot(permt_ref[...], y_tm) + d_ref[...] * u.astype(F32)
    y_ref[...] = y_bm.astype(y_ref.dtype).reshape(y_ref.shape)
    ore_ref[...] = sre[...]
    oim_ref[...] = sim[...]


def _s5(u, perm, permt, bblk, cblk, lre, lim, dvec, h0re, h0im, *, batch, seqlen, tl):
    bt = S5_SEQ_TILE
    m = bt * tl
    has_h0 = h0re is not None
    nb = batch // bt
    nt = seqlen // tl
    if u.ndim == 3:
        u_spec = pl.BlockSpec((bt, tl, D_MODEL), lambda b, t: (b, t, 0))
    else:
        u_spec = pl.BlockSpec((m, D_MODEL), lambda b, t: (b * nt + t, 0))
    st_spec = pl.BlockSpec((bt, S5_FLAT), lambda b, t: (b, 0))
    consts = [perm, permt, bblk, cblk, lre, lim, dvec]
    in_specs = [u_spec] + [_const_spec(c.shape) for c in consts]
    args = [u] + consts
    if has_h0:
        in_specs += [st_spec, st_spec]
        args += [h0re, h0im]
    return pl.pallas_call(
        functools.partial(_s5_kernel, tl=tl, has_h0=has_h0),
        grid=(nb, nt),
        in_specs=in_specs,
        out_specs=[u_spec, st_spec, st_spec],
        out_shape=[jax.ShapeDtypeStruct(u.shape, BF),
                   jax.ShapeDtypeStruct((batch, S5_FLAT), F32),
                   jax.ShapeDtypeStruct((batch, S5_FLAT), F32)],
        scratch_shapes=[pltpu.VMEM((m, S5_FLAT), F32), pltpu.VMEM((m, S5_FLAT), F32),
                        pltpu.VMEM((bt, S5_FLAT), F32), pltpu.VMEM((bt, S5_FLAT), F32)],
        compiler_params=_params(2),
        name="s5_sample" if has_h0 else "s5_prompt",
    )(*args)


def _mixout_kernel(y_ref, z_ref, y5_ref, ga_ref, gb_ref, x_ref, ng_ref, wa_ref, wglu_ref, bglu_ref, wmix_ref, o_ref):
    y = y_ref[...].astype(F32) * _silu(z_ref[...].astype(F32))
    yn = _rms(y, ng_ref[...]).astype(BF)
    branch_a = _dot(yn, wa_ref[...])
    glu = _dot(y5_ref[...], wglu_ref[...]) + bglu_ref[...]
    branch_b = glu[:, :D_MODEL] * _sigmoid(glu[:, D_MODEL:])
    merged = _sigmoid(ga_ref[...].astype(F32)) * branch_a + _sigmoid(gb_ref[...].astype(F32)) * branch_b
    o_ref[...] = x_ref[...] + _dot(merged.astype(BF), wmix_ref[...])


def _mixout(y, z, y5, ga, gb, x2d, ng, wa, wglu, bglu, wmix):
    t = x2d.shape[0]
    tm = TOKEN_TILE
    row = pl.BlockSpec((tm, D_MODEL), lambda i: (i, 0))
    consts = [ng, wa, wglu, bglu, wmix]
    return pl.pallas_call(
        _mixout_kernel,
        grid=(t // tm,),
        in_specs=[row] * 6 + [_const_spec(c.shape) for c in consts],
        out_specs=row,
        out_shape=jax.ShapeDtypeStruct((t, D_MODEL), F32),
        compiler_params=_params(1),
        name="mixout",
    )(y, z, y5, ga, gb, x2d, *consts)


def _attn_kernel(h_ref, g_ref, wq_ref, wo_ref, k_ref, v_ref, o_ref, obuf, *, nkv, rows_per_seq):
    h = h_ref[...]
    r = h.shape[0]
    qv = _dot(_rms(h, g_ref[...]).astype(BF), wq_ref[...]).astype(BF)
    heads_split = len(k_ref.shape) == 4

    def head_of(ref, hd):
        if heads_split:
            parts = [ref[i, :, hd, :] for i in range(nkv)]
            return (parts[0] if nkv == 1 else jnp.concatenate(parts, axis=0)).astype(BF)
        return ref[0, :, hd * MEM_HEAD_DIM:(hd + 1) * MEM_HEAD_DIM].astype(BF)

    if nkv > 1:
        rs = lax.shift_right_logical(lax.broadcasted_iota(I32, (r, nkv * MEM_LEN), 0), int(math.log2(rows_per_seq)))
        cq = lax.shift_right_logical(lax.broadcasted_iota(I32, (r, nkv * MEM_LEN), 1), int(math.log2(MEM_LEN)))
        visible = rs == cq
    scale = MEM_HEAD_DIM ** -0.5
    for hd in range(MEM_HEADS):
        sl = slice(hd * MEM_HEAD_DIM, (hd + 1) * MEM_HEAD_DIM)
        s = _dot_nt(qv[:, sl], head_of(k_ref, hd)) * scale
        if nkv > 1:
            s = jnp.where(visible, s, NEG_BIG)
        p = jnp.exp(s - jnp.max(s, axis=-1, keepdims=True))
        p = p / jnp.sum(p, axis=-1, keepdims=True)
        obuf[:, sl] = _dot(p.astype(BF), head_of(v_ref, hd))
    o_ref[...] = h + _dot(obuf[...].astype(BF), wo_ref[...])


def _attn(h2d, g, wq, wo, k3, v3, *, rows, nkv, rows_per_seq):
    t = h2d.shape[0]
    nsteps = t // rows
    steps_per_kv = nsteps // (k3.shape[0] // nkv)
    row = pl.BlockSpec((rows, D_MODEL), lambda i: (i, 0))
    if k3.ndim == 4:
        kv = pl.BlockSpec((nkv, MEM_LEN, MEM_HEADS, MEM_HEAD_DIM), lambda i: (i // steps_per_kv, 0, 0, 0))
    else:
        kv = pl.BlockSpec((nkv, MEM_LEN, D_MODEL), lambda i: (i // steps_per_kv, 0, 0))
    consts = [g, wq, wo]
    return pl.pallas_call(
        functools.partial(_attn_kernel, nkv=nkv, rows_per_seq=rows_per_seq),
        grid=(nsteps,),
        in_specs=[row] + [_const_spec(c.shape) for c in consts] + [kv, kv],
        out_specs=row,
        out_shape=jax.ShapeDtypeStruct((t, D_MODEL), F32),
        scratch_shapes=[pltpu.VMEM((rows, D_MODEL), F32)],
        compiler_params=_params(1),
        name="attn",
    )(h2d, *consts, k3, v3)


ROUTER_LANE0 = MOE_GROUPS
ROUTE_ROWS = 8


def _router_kernel(hp_ref, hs_ref, g_ref, wh_ref, wm_ref, wl_ref, b_ref, tri_ref, eye_ref, rows_ref, wt_ref, cnt_ref,
                   carry, *, np_steps):
    i = pl.program_id(0)

    @pl.when(i == 0)
    def _():
        carry[...] = jnp.zeros_like(carry)

    h = jnp.where(i < np_steps, hp_ref[...], hs_ref[...])
    xn = _rms(h, g_ref[...])
    xh, xm, xl = _split3(xn)
    wh, wm, wl = wh_ref[...], wm_ref[...], wl_ref[...]
    logits = (_dot(xh, wh) + _dot(xh, wm) + _dot(xm, wh) + _dot(xh, wl) + _dot(xm, wm) + _dot(xl, wh)) + b_ref[...]
    lane = lax.broadcasted_iota(I32, logits.shape, 1)
    is_g = lane < MOE_GROUPS
    lg = jnp.where(is_g, logits, NEG_BIG)
    mg = jnp.max(lg, axis=-1, keepdims=True)
    gidx = jnp.min(jnp.where(lg == mg, lane, LANES), axis=-1, keepdims=True)
    g_w = 1.0 / jnp.sum(jnp.where(is_g, jnp.exp(lg - mg), 0.0), axis=-1, keepdims=True)
    e_lane = lane - ROUTER_LANE0
    in_grp = ((e_lane >= 0) & (e_lane < MOE_EXPERTS)
              & (lax.shift_right_logical(jnp.maximum(e_lane, 0), 3) == gidx))
    le = jnp.where(in_grp, logits, NEG_BIG)
    m1 = jnp.max(le, axis=-1, keepdims=True)
    i1 = jnp.min(jnp.where(le == m1, lane, LANES), axis=-1, keepdims=True)
    le2 = jnp.where(lane == i1, NEG_BIG, le)
    m2 = jnp.max(le2, axis=-1, keepdims=True)
    i2 = jnp.min(jnp.where(le2 == m2, lane, LANES), axis=-1, keepdims=True)
    ratio = jnp.exp(m2 - m1)
    w1 = g_w / (1.0 + ratio)
    w2 = g_w * ratio / (1.0 + ratio)
    wt_ref[...] = jnp.where(lane == 0, w1, jnp.where(lane == 1, w2, 0.0))

    oh1 = jnp.where(lane == i1, 1.0, 0.0)
    oh2 = jnp.where(lane == i2, 1.0, 0.0)
    tri = tri_ref[...]
    c0 = carry[...]
    tot1 = jnp.sum(oh1, axis=0, keepdims=True)
    r1 = jnp.sum(oh1 * (_dot(tri, oh1.astype(BF)) + c0), axis=-1, keepdims=True)
    r2 = jnp.sum(oh2 * (_dot(tri, oh2.astype(BF)) + (c0 + tot1)), axis=-1, keepdims=True)
    c1 = c0 + tot1 + jnp.sum(oh2, axis=0, keepdims=True)
    carry[...] = c1
    cols = jnp.where(lane == 0, (i1 - ROUTER_LANE0).astype(F32),
                     jnp.where(lane == 1, (i2 - ROUTER_LANE0).astype(F32),
                               jnp.where(lane == 2, r1, jnp.where(lane == 3, r2, 0.0))))
    rows_ref[0] = _dot_exact(eye_ref[...], cols, nt=True).astype(I32)
    cnt_ref[...] = jnp.broadcast_to(c1, cnt_ref.shape).astype(I32)


def _router(hp, hs, g, wh, wm, wl, b):
    tm = TOKEN_TILE
    np_steps = hp.shape[0] // tm
    ns_steps = hs.shape[0] // tm
    t = hp.shape[0] + hs.shape[0]
    tri = jnp.asarray(np.tril(np.ones((tm, tm), np.float32), -1)).astype(BF)
    eye8 = jnp.asarray(np.eye(ROUTE_ROWS, LANES, dtype=np.float32)).astype(BF)
    consts = [g, wh, wm, wl, b, tri, eye8]
    return pl.pallas_call(
        functools.partial(_router_kernel, np_steps=np_steps),
        grid=(np_steps + ns_steps,),
        in_specs=[pl.BlockSpec((tm, D_MODEL), lambda i: (jnp.minimum(i, np_steps - 1), 0)),
                  pl.BlockSpec((tm, D_MODEL), lambda i: (jnp.maximum(i - np_steps, 0), 0))]
                 + [_const_spec(c.shape) for c in consts],
        out_specs=[pl.BlockSpec((1, ROUTE_ROWS, tm), lambda i: (i, 0, 0)),
                   pl.BlockSpec((tm, LANES), lambda i: (i, 0)), _const_spec((8, LANES))],
        out_shape=[jax.ShapeDtypeStruct((t // tm, ROUTE_ROWS, tm), I32), jax.ShapeDtypeStruct((t, LANES), F32),
                   jax.ShapeDtypeStruct((8, LANES), I32)],
        scratch_shapes=[pltpu.VMEM((1, LANES), F32)],
        compiler_params=_params(1),
        name="router",
    )(hp, hs, *consts)


SLABS = D_MODEL // LANES


def _token_rows(t):
    return pl.ds(pl.multiple_of(t * SLABS, SLABS), SLABS)


def _to_token_major(ref, x, base=0):
    tm = x.shape[0]
    for s in range(SLABS):
        ref[pl.ds(base + s, tm, stride=SLABS), :] = x[:, s * LANES:(s + 1) * LANES]


def _from_token_major(ref, tm, base=0):
    return jnp.concatenate([ref[pl.ds(base + s, tm, stride=SLABS), :] for s in range(SLABS)], axis=1)


def _sorted_row(ps_ref, rt_ref, k, t):
    return ps_ref[rt_ref[0, k, t]] + rt_ref[0, 2 + k, t]


def _scatter_kernel(ps_ref, ve_ref, pe_ref, hp_ref, hs_ref, g_ref, rt_ref, xs_ref, xbuf, zbuf, sem, zsem, *, np_steps):
    i = pl.program_id(0)
    tm = TOKEN_TILE

    tile_rows = MOE_ROW_TILE * SLABS
    n_tiles = xs_ref.shape[0] // tile_rows

    def pad_copy(r):
        return pltpu.make_async_copy(zbuf.at[pl.ds(0, SLABS)], xs_ref.at[_token_rows(r)], zsem)

    def tile_copy(j):
        return pltpu.make_async_copy(zbuf, xs_ref.at[pl.ds(pl.multiple_of(j * tile_rows, tile_rows), tile_rows)], zsem)

    @pl.when(i == 0)
    def _():
        zbuf[...] = jnp.zeros_like(zbuf)
        first_unused = pe_ref[MOE_EXPERTS - 1] // MOE_ROW_TILE

        def tile_start(j, c):
            tile_copy(j).start()
            return c

        def tile_wait(j, c):
            tile_copy(0).wait()
            return c

        lax.fori_loop(first_unused, n_tiles, tile_start, 0)
        lax.fori_loop(first_unused, n_tiles, tile_wait, 0)

        def per_expert(e, n):
            lo = ve_ref[e]
            hi = pe_ref[e]

            def body(r, c):
                pad_copy(r).start()
                return c

            lax.fori_loop(lo, hi, body, 0)
            return n + (hi - lo)

        n_pad = lax.fori_loop(0, MOE_EXPERTS, per_expert, 0)

        def pad_wait(r, c):
            pad_copy(0).wait()
            return c

        lax.fori_loop(0, n_pad, pad_wait, 0)

    h = jnp.where(i < np_steps, hp_ref[...], hs_ref[...])
    _to_token_major(xbuf, _rms(h, g_ref[...]))

    def row_copy(t, dst_row):
        return pltpu.make_async_copy(xbuf.at[_token_rows(t)], xs_ref.at[_token_rows(dst_row)], sem)

    def issue(t, c):
        row_copy(t, _sorted_row(ps_ref, rt_ref, 0, t)).start()
        row_copy(t, _sorted_row(ps_ref, rt_ref, 1, t)).start(priority=1)
        return c

    def wait(t, c):
        row_copy(0, 0).wait()
        return c

    lax.fori_loop(0, tm, issue, 0, unroll=8)
    lax.fori_loop(0, 2 * tm, wait, 0, unroll=16)


def _scatter(hp, hs, g, route, pstart, valid_end, pend, n_rows):
    tm = TOKEN_TILE
    np_steps = hp.shape[0] // tm
    ns_steps = hs.shape[0] // tm
    grid_spec = pltpu.PrefetchScalarGridSpec(
        num_scalar_prefetch=3,
        grid=(np_steps + ns_steps,),
        in_specs=[pl.BlockSpec((tm, D_MODEL), lambda i, *_: (jnp.minimum(i, np_steps - 1), 0)),
                  pl.BlockSpec((tm, D_MODEL), lambda i, *_: (jnp.maximum(i - np_steps, 0), 0)),
                  pl.BlockSpec(g.shape, lambda i, *_: (0, 0)),
                  pl.BlockSpec((1, ROUTE_ROWS, tm), lambda i, *_: (i, 0, 0), memory_space=pltpu.SMEM)],
        out_specs=pl.BlockSpec(memory_space=pl.ANY),
        scratch_shapes=[pltpu.VMEM((tm * SLABS, LANES), F32), pltpu.VMEM((MOE_ROW_TILE * SLABS, LANES), F32),
                        pltpu.SemaphoreType.DMA(()), pltpu.SemaphoreType.DMA(())],
    )
    return pl.pallas_call(
        functools.partial(_scatter_kernel, np_steps=np_steps),
        grid_spec=grid_spec,
        out_shape=jax.ShapeDtypeStruct((n_rows * SLABS, LANES), F32),
        compiler_params=_params(1),
        name="scatter",
    )(pstart, valid_end, pend, hp, hs, g, route)


def _expert_kernel(te_ref, nu_ref, x_ref, wg_ref, wu_ref, wd_ref, y_ref, wg16, wu16, wd16):
    i = pl.program_id(0)
    tm = MOE_ROW_TILE

    @pl.when(i < nu_ref[0])
    def _():
        @pl.when((i == 0) | (te_ref[i] != te_ref[jnp.maximum(i - 1, 0)]))
        def _():
            wg16[...] = wg_ref[0].astype(BF)
            wu16[...] = wu_ref[0].astype(BF)
            wd16[...] = wd_ref[0].astype(BF)

        x = _from_token_major(x_ref, tm).astype(BF)
        gate = _dot(x, wg16[...])
        up = _dot(x, wu16[...])
        _to_token_major(y_ref, _dot((_silu(gate) * up).astype(BF), wd16[...]))

    @pl.when(i >= nu_ref[0])
    def _():
        y_ref[...] = jnp.zeros_like(y_ref)


def _experts(x_sorted, tile_expert, n_used, wg, wu, wd):
    tm = MOE_ROW_TILE
    n_tiles = x_sorted.shape[0] // (tm * SLABS)
    used = lambda i, nu: jnp.minimum(i, nu[0] - 1)
    grid_spec = pltpu.PrefetchScalarGridSpec(
        num_scalar_prefetch=2,
        grid=(n_tiles,),
        in_specs=[pl.BlockSpec((tm * SLABS, LANES), lambda i, te, nu: (used(i, nu), 0)),
                  pl.BlockSpec((1, D_MODEL, MOE_D_FF), lambda i, te, nu: (te[used(i, nu)], 0, 0)),
                  pl.BlockSpec((1, D_MODEL, MOE_D_FF), lambda i, te, nu: (te[used(i, nu)], 0, 0)),
                  pl.BlockSpec((1, MOE_D_FF, D_MODEL), lambda i, te, nu: (te[used(i, nu)], 0, 0))],
        out_specs=pl.BlockSpec((tm * SLABS, LANES), lambda i, te, nu: (i, 0)),
        scratch_shapes=[pltpu.VMEM((D_MODEL, MOE_D_FF), BF), pltpu.VMEM((D_MODEL, MOE_D_FF), BF),
                        pltpu.VMEM((MOE_D_FF, D_MODEL), BF)],
    )
    return pl.pallas_call(
        _expert_kernel,
        grid_spec=grid_spec,
        out_shape=jax.ShapeDtypeStruct(x_sorted.shape, F32),
        compiler_params=_params(1),
        name="experts",
    )(tile_expert, n_used, x_sorted, wg, wu, wd)


def _combine_kernel(ps_ref, h_ref, wt_ref, g_ref, pcur_ref, pnext_ref, ys_ref, o_ref, ybuf, sem, *, n):
    i = pl.program_id(0)
    tm = TOKEN_TILE

    def row_copy(src_row, j, slot):
        return pltpu.make_async_copy(ys_ref.at[_token_rows(src_row)], ybuf.at[slot, _token_rows(j)], sem.at[slot])

    def issue(rt_ref, slot):
        def body(t, c):
            row_copy(_sorted_row(ps_ref, rt_ref, 0, t), t, slot).start()
            row_copy(_sorted_row(ps_ref, rt_ref, 1, t), tm + t, slot).start(priority=1)
            return c

        lax.fori_loop(0, tm, body, 0, unroll=8)

    def finish(slot):
        def wait(j, c):
            row_copy(0, 0, slot).wait()
            return c

        lax.fori_loop(0, 2 * tm, wait, 0, unroll=16)
        wt = wt_ref[...]
        y1 = _from_token_major(ybuf.at[slot], tm)
        y2 = _from_token_major(ybuf.at[slot], tm, base=tm * SLABS)
        o_ref[...] = _rms(h_ref[...] + wt[:, 0:1] * y1 + wt[:, 1:2] * y2, g_ref[...])

    @pl.when(i == 0)
    def _():
        issue(pcur_ref, 0)

    for slot in (0, 1):
        @pl.when(lax.rem(i, 2) == slot)
        def _(slot=slot):
            @pl.when(i + 1 < n)
            def _():
                issue(pnext_ref, 1 - slot)

            finish(slot)


def _combine(h2d, wt, g, route, pstart, y_sorted, *, tile_offset):
    t = h2d.shape[0]
    tm = TOKEN_TILE
    nsteps = t // tm
    route_spec = lambda f: pl.BlockSpec((1, ROUTE_ROWS, tm), f, memory_space=pltpu.SMEM)
    grid_spec = pltpu.PrefetchScalarGridSpec(
        num_scalar_prefetch=1,
        grid=(nsteps,),
        in_specs=[pl.BlockSpec((tm, D_MODEL), lambda i, ps: (i, 0)),
                  pl.BlockSpec((tm, LANES), lambda i, ps: (i + tile_offset, 0)),
                  pl.BlockSpec(g.shape, lambda i, ps: (0, 0)),
                  route_spec(lambda i, ps: (i + tile_offset, 0, 0)),
                  route_spec(lambda i, ps: (jnp.minimum(i + 1, nsteps - 1) + tile_offset, 0, 0)),
                  pl.BlockSpec(memory_space=pl.ANY)],
        out_specs=pl.BlockSpec((tm, D_MODEL), lambda i, ps: (i, 0)),
        scratch_shapes=[pltpu.VMEM((2, 2 * tm * SLABS, LANES), F32), pltpu.SemaphoreType.DMA((2,))],
    )
    return pl.pallas_call(
        functools.partial(_combine_kernel, n=nsteps),
        grid_spec=grid_spec,
        out_shape=jax.ShapeDtypeStruct((t, D_MODEL), F32),
        compiler_params=_params(1),
        name="combine",
    )(pstart, h2d, wt, g, route, route, y_sorted)


def _moe_plan(cnt, t):
    rt = MOE_ROW_TILE
    n_tiles = (2 * t + MOE_EXPERTS * (rt - 1) + rt - 1) // rt
    counts = cnt[0, ROUTER_LANE0:ROUTER_LANE0 + MOE_EXPERTS]
    padded = ((counts + rt - 1) // rt) * rt
    pend = jnp.cumsum(padded).astype(I32)
    pstart = pend - padded
    valid_end = (pstart + counts).astype(I32)
    tile_start = jnp.arange(n_tiles, dtype=I32) * rt
    tile_expert = jnp.sum((tile_start[:, None] >= pend[None, :]).astype(I32), axis=1)
    tile_expert = jnp.minimum(tile_expert, MOE_EXPERTS - 1).astype(I32)
    n_used = (pend[-1] // rt).astype(I32).reshape(1)
    return pstart, valid_end, pend, tile_expert, n_used, n_tiles * rt


GRANULE = 8
LOCAL_USED = 2 * TOKEN_TILE + MOE_EXPERTS * (GRANULE - 1)
LOCAL_ROWS = 1280
SPARE_GRANULES = (LOCAL_ROWS - LOCAL_USED) // GRANULE
TILE_GRANULES = MOE_ROW_TILE // GRANULE


def _route_kernel(hp_ref, hs_ref, g_ref, wcat_ref, wh_ref, b_ref, tri_ref, eye_ref, upper_ref,
                  xl_ref, idx_ref, wt_ref, seg_ref, *, np_steps):
    i = pl.program_id(0)
    h = jnp.where(i < np_steps, hp_ref[...], hs_ref[...])
    xn = _rms(h, g_ref[...])
    tm = xn.shape[0]
    xh = xn.astype(BF)
    xm = (xn - xh.astype(F32)).astype(BF)
    both = _dot(xh, wcat_ref[...])
    logits = both[:, :LANES] + both[:, LANES:] + _dot(xm, wh_ref[...]) + b_ref[...]
    lane = lax.broadcasted_iota(I32, logits.shape, 1)
    is_g = lane < MOE_GROUPS
    lg = jnp.where(is_g, logits, NEG_BIG)
    mg = jnp.max(lg, axis=-1, keepdims=True)
    gidx = jnp.min(jnp.where(lg == mg, lane, LANES), axis=-1, keepdims=True)
    g_w = 1.0 / jnp.sum(jnp.where(is_g, jnp.exp(lg - mg), 0.0), axis=-1, keepdims=True)
    e_lane = lane - ROUTER_LANE0
    in_grp = ((e_lane >= 0) & (e_lane < MOE_EXPERTS)
              & (lax.shift_right_logical(jnp.maximum(e_lane, 0), 3) == gidx))
    le = jnp.where(in_grp, logits, NEG_BIG)
    m1 = jnp.max(le, axis=-1, keepdims=True)
    i1 = jnp.min(jnp.where(le == m1, lane, LANES), axis=-1, keepdims=True)
    le2 = jnp.where(lane == i1, NEG_BIG, le)
    m2 = jnp.max(le2, axis=-1, keepdims=True)
    i2 = jnp.min(jnp.where(le2 == m2, lane, LANES), axis=-1, keepdims=True)
    ratio = jnp.exp(m2 - m1)
    w1 = g_w / (1.0 + ratio)
    w2 = g_w * ratio / (1.0 + ratio)
    wt_ref[...] = jnp.where(lane == 0, w1, jnp.where(lane == 1, w2, 0.0))

    oh1 = jnp.where(lane == i1, 1.0, 0.0)
    oh2 = jnp.where(lane == i2, 1.0, 0.0)
    tot1 = jnp.sum(oh1, axis=0, keepdims=True)
    cnt = (tot1 + jnp.sum(oh2, axis=0, keepdims=True)).astype(I32)
    padded = lax.shift_left(lax.shift_right_logical(cnt + (GRANULE - 1), 3), 3)
    pad8 = jnp.broadcast_to(padded.astype(F32), (8, LANES)).astype(BF)
    lstart = _dot(pad8, upper_ref[...])[0:1, :]
    tri = tri_ref[...]
    l1 = jnp.sum(oh1 * (_dot(tri, oh1.astype(BF)) + lstart), axis=-1, keepdims=True)
    l2 = jnp.sum(oh2 * (_dot(tri, oh2.astype(BF)) + (lstart + tot1)), axis=-1, keepdims=True)
    idx_ref[...] = jnp.where(lane == 0, l1.astype(I32), jnp.where(lane == 1, l2.astype(I32), 0))
    sub = lax.broadcasted_iota(I32, (8, LANES), 0)
    seg_ref[0] = jnp.where(sub == 0, jnp.broadcast_to(cnt, (8, LANES)),
                           jnp.where(sub == 1, jnp.broadcast_to(lstart.astype(I32), (8, LANES)), 0))

    cols = jnp.where(lane == 0, l1, jnp.where(lane == 1, l2, 0.0))
    rows = _dot_exact(eye_ref[...], cols, nt=True)
    r_id = lax.broadcasted_iota(I32, (LOCAL_ROWS, tm), 0).astype(F32)
    perm = jnp.where(r_id == rows[0:1, :], 1.0, 0.0) + jnp.where(r_id == rows[1:2, :], 1.0, 0.0)
    xl_ref[...] = _dot(perm.astype(BF), xh)


def _route(hp, hs, g, wcat, wh, b):
    tm = TOKEN_TILE
    np_steps = hp.shape[0] // tm
    ns_steps = hs.shape[0] // tm
    nt = np_steps + ns_steps
    tri = jnp.asarray(np.tril(np.ones((tm, tm), np.float32), -1)).astype(BF)
    eye8 = jnp.asarray(np.eye(8, LANES, dtype=np.float32)).astype(BF)
    upper = jnp.asarray(np.triu(np.ones((LANES, LANES), np.float32), 1)).astype(BF)
    consts = [g, wcat, wh, b, tri, eye8, upper]
    return pl.pallas_call(
        functools.partial(_route_kernel, np_steps=np_steps),
        grid=(nt,),
        in_specs=[pl.BlockSpec((tm, D_MODEL), lambda i: (jnp.minimum(i, np_steps - 1), 0)),
                  pl.BlockSpec((tm, D_MODEL), lambda i: (jnp.maximum(i - np_steps, 0), 0))]
                 + [_const_spec(c.shape) for c in consts],
        out_specs=[pl.BlockSpec((LOCAL_ROWS, D_MODEL), lambda i: (i, 0)),
                   pl.BlockSpec((tm, LANES), lambda i: (i, 0)),
                   pl.BlockSpec((tm, LANES), lambda i: (i, 0)),
                   pl.BlockSpec((1, 8, LANES), lambda i: (i, 0, 0))],
        out_shape=[jax.ShapeDtypeStruct((nt * LOCAL_ROWS, D_MODEL), F32),
                   jax.ShapeDtypeStruct((nt * tm, LANES), I32),
                   jax.ShapeDtypeStruct((nt * tm, LANES), F32),
                   jax.ShapeDtypeStruct((nt, 8, LANES), I32)],
        compiler_params=_params(1),
        name="route",
    )(hp, hs, *consts)


def _granule_plan(seg, n_tiles):
    cnt = seg[:, 0, ROUTER_LANE0:ROUTER_LANE0 + MOE_EXPERTS]
    lstart = seg[:, 1, ROUTER_LANE0:ROUTER_LANE0 + MOE_EXPERTS]
    ntt = cnt.shape[0]
    ng = (cnt + GRANULE - 1) // GRANULE
    cum = jnp.cumsum(ng, axis=0)
    first = (jnp.arange(ntt, dtype=I32)[:, None] * LOCAL_ROWS + lstart) // GRANULE - (cum - ng)
    total = cum[-1]
    tiles_e = (total + TILE_GRANULES - 1) // TILE_GRANULES
    tend = jnp.cumsum(tiles_e)
    tstart = tend - tiles_e
    tile = jnp.arange(n_tiles, dtype=I32)
    te = jnp.minimum(jnp.sum((tile[:, None] >= tend[None, :]).astype(I32), axis=1), MOE_EXPERTS - 1)
    n_used = tend[-1].astype(I32).reshape(1)
    sel = (te[:, None] == jnp.arange(MOE_EXPERTS, dtype=I32)[None, :]).astype(I32)
    pick = lambda table: jnp.sum(sel[:, None, :] * table[None, :, :], axis=2)
    cum_t, first_t = pick(cum), pick(first)
    off = (tile - jnp.sum(sel * tstart[None, :], axis=1))[:, None] * TILE_GRANULES + jnp.arange(TILE_GRANULES, dtype=I32)[None, :]
    valid = (off < jnp.sum(sel * total[None, :], axis=1)[:, None]) & (tile < n_used[0])[:, None]
    j = jnp.sum((cum_t[:, None, :] <= off[:, :, None]).astype(I32), axis=2)
    jsel = (j[:, :, None] == jnp.arange(ntt, dtype=I32)[None, None, :]).astype(I32)
    src = jnp.sum(jsel * first_t[:, None, :], axis=2) + off
    spare = lambda k: ((k // SPARE_GRANULES) * LOCAL_ROWS + LOCAL_USED) // GRANULE + k % SPARE_GRANULES
    assert (ntt - 1) * SPARE_GRANULES >= 2 * TILE_GRANULES
    zero_granule = spare((ntt - 1) * SPARE_GRANULES)
    trash = spare((tile % 2)[:, None] * TILE_GRANULES + jnp.arange(TILE_GRANULES, dtype=I32)[None, :])
    gsrc = jnp.where(valid, src, zero_granule).astype(I32).reshape(-1)
    gdst = jnp.where(valid, src, trash).astype(I32).reshape(-1)
    n_valid = jnp.sum(valid.astype(I32), axis=1).astype(I32)
    return te.astype(I32), n_used, gsrc, gdst, n_valid


def _grouped_kernel(te_ref, nu_ref, gsrc_ref, gdst_ref, nv_ref, xin_ref, wg_ref, wu_ref, wd_ref, xout_ref,
                    xbuf, ybuf, wg16, wu16, wd16, sem_in, sem_out):
    i = pl.program_id(0)
    nu = nu_ref[0]
    tm = MOE_ROW_TILE

    def hbm_rows(table_ref, tile, g):
        return pl.ds(pl.multiple_of(table_ref[tile * TILE_GRANULES + g] * GRANULE, GRANULE), GRANULE)

    def fetch_start(tile, slot):
        for g in range(TILE_GRANULES):
            pltpu.make_async_copy(xin_ref.at[hbm_rows(gsrc_ref, tile, g)],
                                  xbuf.at[slot, pl.ds(g * GRANULE, GRANULE)], sem_in.at[slot]).start()

    def store_start(tile, slot):
        for g in range(TILE_GRANULES):
            pltpu.make_async_copy(ybuf.at[slot, pl.ds(g * GRANULE, GRANULE)],
                                  xout_ref.at[hbm_rows(gdst_ref, tile, g)], sem_out.at[slot]).start()

    def fetch_wait(slot):
        pltpu.make_async_copy(xin_ref.at[pl.ds(0, tm)], xbuf.at[slot], sem_in.at[slot]).wait()

    def store_wait(slot):
        pltpu.make_async_copy(ybuf.at[slot], xout_ref.at[pl.ds(0, tm)], sem_out.at[slot]).wait()

    def step(slot):
        @pl.when(i == 0)
        def _():
            ybuf[...] = jnp.zeros_like(ybuf)
            fetch_start(0, 0)

        @pl.when(i + 1 < nu)
        def _():
            fetch_start(i + 1, 1 - slot)

        @pl.when(i >= 2)
        def _():
            store_wait(slot)

        @pl.when((i == 0) | (te_ref[i] != te_ref[jnp.maximum(i - 1, 0)]))
        def _():
            wg16[...] = wg_ref[0].astype(BF)
            wu16[...] = wu_ref[0].astype(BF)
            wd16[...] = wd_ref[0].astype(BF)

        fetch_wait(slot)
        n_valid = nv_ref[i]
        quarter = TILE_GRANULES // 4
        for part in range(1, 5):
            rows = part * quarter * GRANULE

            @pl.when((n_valid > (part - 1) * quarter) & (n_valid <= part * quarter))
            def _(rows=rows):
                x = xbuf[slot, 0:rows, :].astype(BF)
                gate = _dot(x, wg16[...])
                up = _dot(x, wu16[...])
                ybuf[slot, 0:rows, :] = _dot((_silu(gate) * up).astype(BF), wd16[...])

        store_start(i, slot)

        @pl.when(i == nu - 1)
        def _():
            store_wait(slot)

            @pl.when(i >= 1)
            def _():
                store_wait(1 - slot)

    for slot in (0, 1):
        @pl.when((i < nu) & (lax.rem(i, 2) == slot))
        def _(slot=slot):
            step(slot)


def _grouped(x_local, te, n_used, gsrc, gdst, n_valid, wg, wu, wd):
    tm = MOE_ROW_TILE
    n_tiles = te.shape[0]
    used = lambda i, nu: jnp.minimum(i, jnp.maximum(nu[0] - 1, 0))
    wspec = lambda shape: pl.BlockSpec((1,) + shape, lambda i, te, nu, *_: (te[used(i, nu)], 0, 0))
    grid_spec = pltpu.PrefetchScalarGridSpec(
        num_scalar_prefetch=5,
        grid=(n_tiles,),
        in_specs=[pl.BlockSpec(memory_space=pl.ANY), wspec((D_MODEL, MOE_D_FF)), wspec((D_MODEL, MOE_D_FF)),
                  wspec((MOE_D_FF, D_MODEL))],
        out_specs=pl.BlockSpec(memory_space=pl.ANY),
        scratch_shapes=[pltpu.VMEM((2, tm, D_MODEL), F32), pltpu.VMEM((2, tm, D_MODEL), F32),
                        pltpu.VMEM((D_MODEL, MOE_D_FF), BF), pltpu.VMEM((D_MODEL, MOE_D_FF), BF),
                        pltpu.VMEM((MOE_D_FF, D_MODEL), BF),
                        pltpu.SemaphoreType.DMA((2,)), pltpu.SemaphoreType.DMA((2,))],
    )
    return pl.pallas_call(
        _grouped_kernel,
        grid_spec=grid_spec,
        out_shape=jax.ShapeDtypeStruct(x_local.shape, F32),
        input_output_aliases={5: 0},
        compiler_params=_params(1),
        name="grouped",
    )(te, n_used, gsrc, gdst, n_valid, x_local, wg, wu, wd)


def _merge_kernel(h_ref, idx_ref, wt_ref, g_ref, yl_ref, o_ref):
    tm = h_ref.shape[0]
    idx = idx_ref[...]
    wt = wt_ref[...]
    r_id = lax.broadcasted_iota(I32, (tm, LOCAL_ROWS), 1)
    sel = jnp.where(r_id == idx[:, 0:1], wt[:, 0:1], 0.0) + jnp.where(r_id == idx[:, 1:2], wt[:, 1:2], 0.0)
    moe = _dot(sel.astype(BF), yl_ref[...].astype(BF))
    o_ref[...] = _rms(h_ref[...] + moe, g_ref[...])


def _merge(h2d, idx, wt, g, y_local, *, tile_offset):
    t = h2d.shape[0]
    tm = TOKEN_TILE
    return pl.pallas_call(
        _merge_kernel,
        grid=(t // tm,),
        in_specs=[pl.BlockSpec((tm, D_MODEL), lambda i: (i, 0)),
                  pl.BlockSpec((tm, LANES), lambda i: (i + tile_offset, 0)),
                  pl.BlockSpec((tm, LANES), lambda i: (i + tile_offset, 0)),
                  _const_spec(g.shape),
                  pl.BlockSpec((LOCAL_ROWS, D_MODEL), lambda i: (i + tile_offset, 0))],
        out_specs=pl.BlockSpec((tm, D_MODEL), lambda i: (i, 0)),
        out_shape=jax.ShapeDtypeStruct((t, D_MODEL), F32),
        compiler_params=_params(1),
        name="merge",
    )(h2d, idx, wt, g, y_local)


def _perm_matrix(bt, tl):
    m = bt * tl
    p = np.zeros((m, m), np.float32)
    for l in range(tl):
        for b in range(bt):
            p[l * bt + b, b * tl + l] = 1.0
    return p


def _s5_operators(lam_re, lam_im, log_step, b_re, b_im, c_re, c_im):
    lam = lax.complex(lam_re, lam_im)
    step = jnp.exp(log_step)[:, None]
    lam_bar = jnp.exp(lam * step)
    b_bar = ((lam_bar - 1.0) / lam)[..., None] * lax.complex(b_re, b_im)
    gl = LANES // S5_GROUP
    eye = jnp.eye(gl, dtype=F32)

    def in_block(bpart):
        bp = bpart.reshape(S5_SLABS, gl, S5_STATE, S5_GROUP)
        blk = jnp.einsum("sgnk,gh->sgkhn", bp, eye)
        return blk.reshape(S5_SLABS, LANES, gl * S5_STATE)

    def out_block(cpart):
        cp = cpart.reshape(S5_SLABS, gl, S5_GROUP, S5_STATE)
        blk = jnp.einsum("sgkn,gh->sgnhk", cp, eye)
        return blk.reshape(S5_SLABS, gl * S5_STATE, LANES)

    bblk = jnp.concatenate([in_block(jnp.real(b_bar)), in_block(jnp.imag(b_bar))], axis=2).astype(BF)
    cblk = jnp.concatenate([out_block(c_re), out_block(-c_im)], axis=1).astype(BF)
    lre = jnp.broadcast_to(jnp.real(lam_bar).reshape(1, S5_FLAT), (S5_SEQ_TILE, S5_FLAT))
    lim = jnp.broadcast_to(jnp.imag(lam_bar).reshape(1, S5_FLAT), (S5_SEQ_TILE, S5_FLAT))
    return bblk, cblk, lre, lim


def _pad_lanes(v, width=LANES):
    return jnp.pad(v, [(0, 0)] * (v.ndim - 1) + [(0, width - v.shape[-1])])


def kernel(x_prompt, x_sample, state_conv, state_ssd, state_s5_re, state_s5_im, cache_mem_k, cache_mem_v, mem_prompt, norm_mix_g, w_in, conv_w, conv_b, ssd_dt_bias, ssd_a_log, ssd_d, ssd_norm_g, w_ssd_branch, s5_lambda_re, s5_lambda_im, s5_log_step, s5_b_re, s5_b_im, s5_c_re, s5_c_im, s5_d, w_glu, b_glu, w_mix_out, norm_mem_q_g, norm_mem_kv_g, w_mem_q, w_mem_k, w_mem_v, w_mem_o, norm_ffn_g, w_router_group, b_router_group, w_router_expert, b_router_expert, w_exp_gate, w_exp_up, w_exp_down, norm_final_g):
    bp, lp, _ = x_prompt.shape
    bs, ls, _ = x_sample.shape
    tp, ts = bp * lp, bs * ls
    row1 = lambda v: v.reshape(1, -1).astype(F32)

    w_in0 = w_in[0]
    o_xbc = D_MODEL
    o_dt = o_xbc + SSD_CONV_DIM
    o_u5 = o_dt + SSD_HEADS
    w_main = jnp.concatenate([w_in0[:, :o_dt], w_in0[:, o_u5:]], axis=1).astype(BF)
    w_dt = _pad_lanes(w_in0[:, o_dt:o_u5]).astype(BF)
    dt_bias = _pad_lanes(row1(ssd_dt_bias[0]))
    a_log = _pad_lanes(row1(ssd_a_log[0]))
    dskip = jnp.repeat(ssd_d[0].astype(F32), SSD_HEADDIM).reshape(1, D_MODEL)
    eexp_np = np.zeros((LANES, D_MODEL), np.float32)
    for h in range(SSD_HEADS):
        eexp_np[h, h * SSD_HEADDIM:(h + 1) * SSD_HEADDIM] = 1.0
    eexp = jnp.asarray(eexp_np).astype(BF)
    eye16 = jnp.asarray(np.eye(SSD_HEADS, LANES, dtype=np.float32)).astype(BF)
    bblk, cblk, lre, lim = _s5_operators(s5_lambda_re[0], s5_lambda_im[0], s5_log_step[0], s5_b_re[0],
                                         s5_b_im[0], s5_c_re[0], s5_c_im[0])
    s5d = row1(s5_d[0])
    w_kv = jnp.concatenate([w_mem_k[0], w_mem_v[0]], axis=1).astype(BF)
    w_r = _pad_lanes(jnp.concatenate([w_router_group[0], w_router_expert[0]], axis=1).astype(F32))
    wr_h, wr_m, wr_l = _split3(w_r)
    b_r = _pad_lanes(row1(jnp.concatenate([b_router_group[0], b_router_expert[0]])))
    wg = w_exp_gate[0].reshape(MOE_EXPERTS, D_MODEL, MOE_D_FF)
    wu = w_exp_up[0].reshape(MOE_EXPERTS, D_MODEL, MOE_D_FF)
    wd = w_exp_down[0].reshape(MOE_EXPERTS, MOE_D_FF, D_MODEL)

    def mixer(x2d, batch, seqlen, conv0, h0, s5re0, s5im0):
        z, xbc, u5, ga, gb, dt = _inproj(x2d, row1(norm_mix_g[0]), w_main, w_dt, dt_bias)
        y_ssd, new_conv, new_ssd = _ssd(xbc, dt, conv_w[0].astype(F32), row1(conv_b[0]), a_log, dskip, eexp, eye16,
                                        conv0, h0, batch=batch, seqlen=seqlen)
        if h0 is None:
            tl = S5_TIME_TILE
            u_in = u5.reshape(batch, seqlen, D_MODEL)
        else:
            tl = seqlen
            u_in = u5
        perm = _perm_matrix(S5_SEQ_TILE, tl)
        y5, new_re, new_im = _s5(u_in, jnp.asarray(perm).astype(BF), jnp.asarray(perm.T).astype(BF), bblk, cblk,
                                 lre, lim, s5d, s5re0, s5im0, batch=batch, seqlen=seqlen, tl=tl)
        h1 = _mixout(y_ssd, z, y5.reshape(-1, D_MODEL), ga, gb, x2d, row1(ssd_norm_g[0]),
                     w_ssd_branch[0].astype(BF), w_glu[0].astype(BF), row1(b_glu[0]), w_mix_out[0].astype(BF))
        return h1, new_conv, new_ssd, new_re, new_im

    xp2 = x_prompt.reshape(tp, D_MODEL)
    xs2 = x_sample.reshape(ts, D_MODEL)
    h1p, conv_p, ssd_p, re_p, im_p = mixer(xp2, bp, lp, None, None, None, None)
    h1s, conv_s, ssd_s, re_s, im_s = mixer(xs2, bs, ls, state_conv[0], state_ssd[0],
                                           state_s5_re[0].reshape(bs, S5_FLAT), state_s5_im[0].reshape(bs, S5_FLAT))

    mk_p, mv_p = _memkv(mem_prompt.reshape(bp * MEM_LEN, D_MODEL), row1(norm_mem_kv_g[0]), w_kv)
    gq = row1(norm_mem_q_g[0])
    wq = w_mem_q[0].astype(BF)
    wo = w_mem_o[0].astype(BF)
    h2p = _attn(h1p, gq, wq, wo, mk_p.reshape(bp, MEM_LEN, D_MODEL), mv_p.reshape(bp, MEM_LEN, D_MODEL),
                rows=TOKEN_TILE, nkv=1, rows_per_seq=lp)
    sample_seqs = 4
    h2s = _attn(h1s, gq, wq, wo, cache_mem_k[0], cache_mem_v[0], rows=sample_seqs * ls, nkv=sample_seqs,
                rows_per_seq=ls)

    g_ffn = row1(norm_ffn_g[0])
    x_local, r_idx, r_wt, seg = _route(h2p, h2s, g_ffn, jnp.concatenate([wr_h, wr_m], axis=1), wr_h, b_r)
    token_tiles = (tp + ts) // TOKEN_TILE
    max_granules = (2 * (tp + ts) + token_tiles * MOE_EXPERTS * (GRANULE - 1)) // GRANULE
    n_tiles = max_granules // TILE_GRANULES + MOE_EXPERTS
    te, n_used, gsrc, gdst, n_valid = _granule_plan(seg, n_tiles)
    y_local = _grouped(x_local, te, n_used, gsrc, gdst, n_valid, wg, wu, wd)
    gf = row1(norm_final_g)
    y_prompt = _merge(h2p, r_idx, r_wt, gf, y_local, tile_offset=0)
    y_sample = _merge(h2s, r_idx, r_wt, gf, y_local, tile_offset=tp // TOKEN_TILE)

    return (y_prompt.reshape(bp, lp, D_MODEL), y_sample.reshape(bs, ls, D_MODEL),
            conv_p[None], ssd_p[None],
            re_p.reshape(1, bp, S5_GROUPS, S5_STATE), im_p.reshape(1, bp, S5_GROUPS, S5_STATE),
            mk_p.reshape(1, bp, MEM_LEN, MEM_HEADS, MEM_HEAD_DIM), mv_p.reshape(1, bp, MEM_LEN, MEM_HEADS, MEM_HEAD_DIM),
            conv_s[None], ssd_s[None],
            re_s.reshape(1, bs, S5_GROUPS, S5_STATE), im_s.reshape(1, bs, S5_GROUPS, S5_STATE))
```

```python
import functools
import math

import numpy as np
import jax
import jax.numpy as jnp
from jax import lax
from jax.experimental import pallas as pl
from jax.experimental.pallas import tpu as pltpu

F32 = jnp.float32
BF = jnp.bfloat16
I32 = jnp.int32

D_MODEL = 1024
SSD_HEADS = 16
SSD_HEADDIM = 64
SSD_GROUPS = 2
SSD_STATE = 128
SSD_CONV = 4
SSD_CONV_DIM = 1536
HEADS_PER_GROUP = SSD_HEADS // SSD_GROUPS
S5_GROUPS = 64
S5_GROUP = 16
S5_STATE = 64
S5_FLAT = S5_GROUPS * S5_STATE
S5_SLABS = D_MODEL // 128
MEM_LEN = 256
MEM_HEADS = 4
MEM_HEAD_DIM = 256
MOE_GROUPS = 4
MOE_EXPERTS_PER_GROUP = 8
MOE_EXPERTS = MOE_GROUPS * MOE_EXPERTS_PER_GROUP
MOE_D_FF = 512
NORM_EPS = 1e-6

LANES = 128
VMEM_LIMIT_BYTES = 56 * 1024 * 1024
TOKEN_TILE = 512
SSD_CHUNK = 128
S5_TIME_TILE = 64
MOE_ROW_TILE = 512
NEG_BIG = -1e30


def _params(n_axes):
    return pltpu.CompilerParams(dimension_semantics=("arbitrary",) * n_axes,
                                vmem_limit_bytes=VMEM_LIMIT_BYTES)


def _const_spec(shape):
    nd = len(shape)
    return pl.BlockSpec(shape, lambda *_: (0,) * nd)


def _dot(a, b):
    return jnp.dot(a, b, preferred_element_type=F32)


def _dot_nt(a, b):
    return lax.dot_general(a, b, (((1,), (1,)), ((), ())), preferred_element_type=F32)


def _split3(v):
    h = v.astype(BF)
    r = v - h.astype(F32)
    m = r.astype(BF)
    l = (r - m.astype(F32)).astype(BF)
    return h, m, l


def _dot_exact(a_bf, v, nt=False):
    f = _dot_nt if nt else _dot
    h, m, l = _split3(v)
    return f(a_bf, h) + f(a_bf, m) + f(a_bf, l)


def _rms(x, g):
    return x * lax.rsqrt(jnp.mean(x * x, axis=-1, keepdims=True) + NORM_EPS) * g


def _sigmoid(x):
    return 0.5 * jnp.tanh(0.5 * x) + 0.5


def _silu(x):
    return x * _sigmoid(x)


def _proj_cols(xb, w_ref, o_ref, c0, width):
    for c in range(0, width, 512):
        ce = min(c + 512, width)
        o_ref[:, c:ce] = _dot(xb, w_ref[:, c0 + c:c0 + ce]).astype(o_ref.dtype)


def _inproj_kernel(x_ref, g_ref, w_ref, wdt_ref, dtb_ref, z_ref, xbc_ref, u5_ref, ga_ref, gb_ref, dt_ref):
    xb = _rms(x_ref[...], g_ref[...]).astype(BF)
    c0 = 0
    for o_ref in (z_ref, xbc_ref, u5_ref, ga_ref, gb_ref):
        width = o_ref.shape[1]
        _proj_cols(xb, w_ref, o_ref, c0, width)
        c0 += width
    raw = _dot(xb, wdt_ref[...]) + dtb_ref[...]
    dt_ref[...] = jnp.maximum(raw, 0.0) + jnp.log1p(jnp.exp(-jnp.abs(raw)))


def _inproj(x2d, g, w_main, w_dt, dt_bias):
    t = x2d.shape[0]
    tm = TOKEN_TILE
    widths = (D_MODEL, SSD_CONV_DIM, D_MODEL, D_MODEL, D_MODEL)
    row = lambda w: pl.BlockSpec((tm, w), lambda i: (i, 0))
    return pl.pallas_call(
        _inproj_kernel,
        grid=(t // tm,),
        in_specs=[row(D_MODEL), _const_spec(g.shape), _const_spec(w_main.shape), _const_spec(w_dt.shape),
                  _const_spec(dt_bias.shape)],
        out_specs=[row(w) for w in widths] + [row(LANES)],
        out_shape=[jax.ShapeDtypeStruct((t, w), BF) for w in widths] + [jax.ShapeDtypeStruct((t, LANES), F32)],
        compiler_params=_params(1),
        name="inproj",
    )(x2d, g, w_main, w_dt, dt_bias)


def _memkv_kernel(x_ref, g_ref, w_ref, k_ref, v_ref):
    xb = _rms(x_ref[...], g_ref[...]).astype(BF)
    _proj_cols(xb, w_ref, k_ref, 0, D_MODEL)
    _proj_cols(xb, w_ref, v_ref, D_MODEL, D_MODEL)


def _memkv(mem2d, g, w_kv):
    t = mem2d.shape[0]
    tm = TOKEN_TILE
    row = pl.BlockSpec((tm, D_MODEL), lambda i: (i, 0))
    return pl.pallas_call(
        _memkv_kernel,
        grid=(t // tm,),
        in_specs=[row, _const_spec(g.shape), _const_spec(w_kv.shape)],
        out_specs=[row, row],
        out_shape=[jax.ShapeDtypeStruct((t, D_MODEL), F32)] * 2,
        compiler_params=_params(1),
        name="memkv",
    )(mem2d, g, w_kv)


def _ssd_kernel(*refs, lq, has_h0):
    q = SSD_CHUNK
    nseq = q // lq
    if has_h0:
        (xbc_ref, dt_ref, cw_ref, cb_ref, alog_ref, dsk_ref, eexp_ref, eye_ref, shift_ref, conv0_ref, h0_ref,
         y_ref, convo_ref, ho_ref, hbuf, cacc, yacc, yint, xwt, tot_s) = refs
        hin_ref = h0_ref
        hbuf[...] = jnp.zeros_like(hbuf)
        hbuf[:, 0:SSD_CONV - 1, :] = conv0_ref[...]
    else:
        (xbc_ref, dt_ref, cw_ref, cb_ref, alog_ref, dsk_ref, eexp_ref, eye_ref, shift_ref,
         y_ref, convo_ref, ho_ref, hbuf, cacc, yacc, yint, xwt, tot_s) = refs
        hin_ref = ho_ref

        @pl.when(pl.program_id(1) == 0)
        def _():
            ho_ref[...] = jnp.zeros_like(ho_ref)
            hbuf[...] = jnp.zeros_like(hbuf)

    x16 = xbc_ref[...]
    xraw = x16.astype(F32)
    acc = cb_ref[...] + xraw * cw_ref[SSD_CONV - 1:SSD_CONV, :]
    for k in range(SSD_CONV - 1):
        acc = acc + _dot(shift_ref[k], x16) * cw_ref[k:k + 1, :]
    cacc[...] = acc
    for i in range(nseq):
        tail = xraw[(i + 1) * lq - (SSD_CONV - 1):(i + 1) * lq]
        convo_ref[i] = tail
        corr = hbuf[i, 0:8, :] * cw_ref[0:1, :]
        for k in range(1, SSD_CONV - 1):
            corr = corr + hbuf[i, k:k + 8, :] * cw_ref[k:k + 1, :]
        cacc[i * lq:i * lq + 8, :] += corr
        if not has_h0:
            hbuf[i, 0:SSD_CONV - 1, :] = tail
    xc = _silu(cacc[...])
    xs = xc[:, :D_MODEL]
    xs_bf = xs.astype(BF)
    bm_bf = xc[:, D_MODEL:D_MODEL + SSD_GROUPS * SSD_STATE].astype(BF)
    cm = xc[:, D_MODEL + SSD_GROUPS * SSD_STATE:]
    cm_bf = cm.astype(BF)

    dt = dt_ref[...]
    da = dt * (-jnp.exp(alog_ref[...]))
    ri = lax.broadcasted_iota(I32, (q, q), 0)
    ci = lax.broadcasted_iota(I32, (q, q), 1)
    if nseq == 1:
        causal = ci <= ri
    else:
        sh = int(math.log2(lq))
        same = lax.shift_right_logical(ri, sh) == lax.shift_right_logical(ci, sh)
        causal = same & (ci <= ri)
    lmat = jnp.where(causal, 1.0, 0.0).astype(BF)
    cs = _dot_exact(lmat, da)
    if nseq == 1:
        tot = jnp.broadcast_to(cs[q - 1:q, :], (q, LANES))
    else:
        tot = _dot_exact(jnp.where(same, 1.0, 0.0).astype(BF), da)
    tot_s[...] = tot
    eye = eye_ref[...]
    cs_t = _dot_exact(eye, cs, nt=True)
    dt_t = _dot_exact(eye, dt, nt=True)
    eexp = eexp_ref[...]
    ecs_h, ecs_m, _ = _split3(jnp.exp(cs))
    ecs_x = _dot(ecs_h, eexp) + _dot(ecs_m, eexp)
    wend_x = _dot((jnp.exp(tot - cs) * dt).astype(BF), eexp)

    lane = lax.broadcasted_iota(I32, (q, LANES), 1)
    for g in range(SSD_GROUPS):
        cbg = _dot_nt(cm_bf[:, g * SSD_STATE:(g + 1) * SSD_STATE], bm_bf[:, g * SSD_STATE:(g + 1) * SSD_STATE])
        for jp in range(HEADS_PER_GROUP // 2):
            j = g * (HEADS_PER_GROUP // 2) + jp
            ms = []
            for h in (2 * j, 2 * j + 1):
                col = jnp.broadcast_to(cs[:, h:h + 1], (q, q))
                row = jnp.broadcast_to(cs_t[h:h + 1, :], (q, q))
                dtr = jnp.broadcast_to(dt_t[h:h + 1, :], (q, q))
                ms.append((jnp.where(causal, jnp.exp(col - row), 0.0) * cbg * dtr).astype(BF))
            mp = jnp.concatenate(ms, axis=1)
            xp = xs_bf[:, j * LANES:(j + 1) * LANES]
            zero = jnp.zeros_like(xp)
            x2 = jnp.concatenate([jnp.where(lane < SSD_HEADDIM, xp, zero),
                                  jnp.where(lane >= SSD_HEADDIM, xp, zero)], axis=0)
            yacc[:, j * LANES:(j + 1) * LANES] = _dot(mp, x2)

    gw = HEADS_PER_GROUP * SSD_HEADDIM
    xw = xs * wend_x
    for g in range(SSD_GROUPS):
        xwt[g * gw:(g + 1) * gw, :] = xw[:, g * gw:(g + 1) * gw].T.astype(BF)

    if nseq == 1:
        for g in range(SSD_GROUPS):
            hin_g = hin_ref[0, g * HEADS_PER_GROUP:(g + 1) * HEADS_PER_GROUP].reshape(gw, SSD_STATE)
            yint[:, g * gw:(g + 1) * gw] = _dot_nt(cm_bf[:, g * SSD_STATE:(g + 1) * SSD_STATE], hin_g.astype(BF))
            s_new = _dot(xwt[g * gw:(g + 1) * gw, :], bm_bf[:, g * SSD_STATE:(g + 1) * SSD_STATE])
            for hl in range(HEADS_PER_GROUP):
                h = g * HEADS_PER_GROUP + hl
                dec = jnp.exp(jnp.broadcast_to(tot[0:1, h:h + 1], (SSD_HEADDIM, SSD_STATE)))
                ho_ref[0, h] = dec * hin_ref[0, h] + s_new[hl * SSD_HEADDIM:(hl + 1) * SSD_HEADDIM]
    else:
        yint[...] = jnp.zeros_like(yint)
        sh = int(math.log2(lq))
        rowseq = lax.shift_right_logical(lax.broadcasted_iota(I32, (q, SSD_STATE), 0), sh)
        colseq = lax.shift_right_logical(lax.broadcasted_iota(I32, (gw, q), 1), sh)

        def seq_body(i, carry):
            trow = tot_s[pl.ds(i * lq, 1), :]
            for g in range(SSD_GROUPS):
                hin_g = hin_ref[i, g * HEADS_PER_GROUP:(g + 1) * HEADS_PER_GROUP].reshape(gw, SSD_STATE)
                cmg = cm[:, g * SSD_STATE:(g + 1) * SSD_STATE]
                lhs = jnp.where(rowseq == i, cmg, 0.0).astype(BF)
                yint[:, g * gw:(g + 1) * gw] += _dot_nt(lhs, hin_g.astype(BF))
                xg = xwt[g * gw:(g + 1) * gw, :]
                xi = jnp.where(colseq == i, xg, jnp.zeros_like(xg))
                s_new = _dot(xi, bm_bf[:, g * SSD_STATE:(g + 1) * SSD_STATE])
                for hl in range(HEADS_PER_GROUP):
                    h = g * HEADS_PER_GROUP + hl
                    dec = jnp.exp(jnp.broadcast_to(trow[:, h:h + 1], (SSD_HEADDIM, SSD_STATE)))
                    ho_ref[i, h] = dec * hin_ref[i, h] + s_new[hl * SSD_HEADDIM:(hl + 1) * SSD_HEADDIM]
            return carry

        lax.fori_loop(0, nseq, seq_body, 0)

    y_ref[...] = (yacc[...] + yint[...] * ecs_x + dsk_ref[...] * xs).astype(y_ref.dtype)


def _ssd(xbc, dt, conv_w, conv_b, a_log, dskip, eexp, eye, conv0, h0, *, batch, seqlen):
    q = SSD_CHUNK
    t = xbc.shape[0]
    has_h0 = h0 is not None
    if has_h0:
        lq = seqlen
        nseq = q // lq
        grid = (t // q,)
        rmap = lambda i: (i, 0)
        smap3 = lambda i: (i, 0, 0)
        smap4 = lambda i: (i, 0, 0, 0)
    else:
        lq = q
        nseq = 1
        nc = seqlen // q
        grid = (batch, nc)
        rmap = lambda b, c: (b * nc + c, 0)
        smap3 = lambda b, c: (b, 0, 0)
        smap4 = lambda b, c: (b, 0, 0, 0)
    shifts = np.zeros((SSD_CONV - 1, q, q), np.float32)
    for k in range(SSD_CONV - 1):
        for r in range(q):
            if r % lq + k - (SSD_CONV - 1) >= 0:
                shifts[k, r, r + k - (SSD_CONV - 1)] = 1.0
    consts = [conv_w, conv_b, a_log, dskip, eexp, eye, jnp.asarray(shifts).astype(BF)]
    in_specs = [pl.BlockSpec((q, SSD_CONV_DIM), rmap), pl.BlockSpec((q, LANES), rmap)]
    in_specs += [_const_spec(c.shape) for c in consts]
    args = [xbc, dt] + consts
    if has_h0:
        in_specs += [pl.BlockSpec((nseq, SSD_CONV - 1, SSD_CONV_DIM), smap3),
                     pl.BlockSpec((nseq, SSD_HEADS, SSD_HEADDIM, SSD_STATE), smap4)]
        args += [conv0, h0]
    return pl.pallas_call(
        functools.partial(_ssd_kernel, lq=lq, has_h0=has_h0),
        grid=grid,
        in_specs=in_specs,
        out_specs=[pl.BlockSpec((q, D_MODEL), rmap),
                   pl.BlockSpec((nseq, SSD_CONV - 1, SSD_CONV_DIM), smap3),
                   pl.BlockSpec((nseq, SSD_HEADS, SSD_HEADDIM, SSD_STATE), smap4)],
        out_shape=[jax.ShapeDtypeStruct((t, D_MODEL), BF),
                   jax.ShapeDtypeStruct((batch, SSD_CONV - 1, SSD_CONV_DIM), F32),
                   jax.ShapeDtypeStruct((batch, SSD_HEADS, SSD_HEADDIM, SSD_STATE), F32)],
        scratch_shapes=[pltpu.VMEM((nseq, 16, SSD_CONV_DIM), F32),
                        pltpu.VMEM((q, SSD_CONV_DIM), F32),
                        pltpu.VMEM((q, D_MODEL), F32),
                        pltpu.VMEM((q, D_MODEL), F32),
                        pltpu.VMEM((D_MODEL, q), BF),
                        pltpu.VMEM((q, LANES), F32)],
        compiler_params=_params(len(grid)),
        name="ssd_sample" if has_h0 else "ssd_prompt",
    )(*args)


S5_SEQ_TILE = 8
S5_LANE_CHUNK = 1024


def _s5_kernel(*refs, tl, has_h0):
    bt = S5_SEQ_TILE
    m = bt * tl
    if has_h0:
        (u_ref, perm_ref, permt_ref, bblk_ref, cblk_ref, lre_ref, lim_ref, d_ref, h0re_ref, h0im_ref,
         y_ref, ore_ref, oim_ref, bre, bim, sre, sim) = refs
    else:
        (u_ref, perm_ref, permt_ref, bblk_ref, cblk_ref, lre_ref, lim_ref, d_ref,
         y_ref, ore_ref, oim_ref, bre, bim, sre, sim) = refs

    @pl.when(pl.program_id(1) == 0)
    def _():
        if has_h0:
            sre[...] = h0re_ref[...]
            sim[...] = h0im_ref[...]
        else:
            sre[...] = jnp.zeros_like(sre)
            sim[...] = jnp.zeros_like(sim)

    u = u_ref[...].reshape(m, D_MODEL)
    u_tm = _dot(perm_ref[...], u).astype(BF)
    half = S5_FLAT // S5_SLABS
    for j in range(S5_SLABS):
        r = _dot(u_tm[:, j * LANES:(j + 1) * LANES], bblk_ref[j])
        bre[:, j * half:(j + 1) * half] = r[:, :half]
        bim[:, j * half:(j + 1) * half] = r[:, half:]

    for jc in range(S5_FLAT // S5_LANE_CHUNK):
        sl = slice(jc * S5_LANE_CHUNK, (jc + 1) * S5_LANE_CHUNK)
        lr = lre_ref[:, sl]
        li = lim_ref[:, sl]

        def step(l, carry, sl=sl, lr=lr, li=li):
            sr, si = carry
            rows = pl.ds(pl.multiple_of(l * bt, bt), bt)
            nr = lr * sr - li * si + bre[rows, sl]
            ni = lr * si + li * sr + bim[rows, sl]
            bre[rows, sl] = nr
            bim[rows, sl] = ni
            return nr, ni

        sr, si = lax.fori_loop(0, tl, step, (sre[:, sl], sim[:, sl]), unroll=2)
        sre[:, sl] = sr
        sim[:, sl] = si

    ys = []
    for j in range(S5_SLABS):
        st = jnp.concatenate([bre[:, j * half:(j + 1) * half], bim[:, j * half:(j + 1) * half]], axis=1)
        ys.append(_dot(st.astype(BF), cblk_ref[j]))
    y_tm = jnp.concatenate(ys, axis=1).astype(BF)
    y_bm = _dot(permt_ref[...], y_tm) + d_ref[...] * u.astype(F32)
    y_ref[...] = y_bm.astype(y_ref.dtype).reshape(y_ref.shape)
    ore_ref[...] = sre[...]
    oim_ref[...] = sim[...]


def _s5(u, perm, permt, bblk, cblk, lre, lim, dvec, h0re, h0im, *, batch, seqlen, tl):
    bt = S5_SEQ_TILE
    m = bt * tl
    has_h0 = h0re is not None
    nb = batch // bt
    nt = seqlen // tl
    if u.ndim == 3:
        u_spec = pl.BlockSpec((bt, tl, D_MODEL), lambda b, t: (b, t, 0))
    else:
        u_spec = pl.BlockSpec((m, D_MODEL), lambda b, t: (b * nt + t, 0))
    st_spec = pl.BlockSpec((bt, S5_FLAT), lambda b, t: (b, 0))
    consts = [perm, permt, bblk, cblk, lre, lim, dvec]
    in_specs = [u_spec] + [_const_spec(c.shape) for c in consts]
    args = [u] + consts
    if has_h0:
        in_specs += [st_spec, st_spec]
        args += [h0re, h0im]
    return pl.pallas_call(
        functools.partial(_s5_kernel, tl=tl, has_h0=has_h0),
        grid=(nb, nt),
        in_specs=in_specs,
        out_specs=[u_spec, st_spec, st_spec],
        out_shape=[jax.ShapeDtypeStruct(u.shape, BF),
                   jax.ShapeDtypeStruct((batch, S5_FLAT), F32),
                   jax.ShapeDtypeStruct((batch, S5_FLAT), F32)],
        scratch_shapes=[pltpu.VMEM((m, S5_FLAT), F32), pltpu.VMEM((m, S5_FLAT), F32),
                        pltpu.VMEM((bt, S5_FLAT), F32), pltpu.VMEM((bt, S5_FLAT), F32)],
        compiler_params=_params(2),
        name="s5_sample" if has_h0 else "s5_prompt",
    )(*args)


def _mixout_kernel(y_ref, z_ref, y5_ref, ga_ref, gb_ref, x_ref, ng_ref, wa_ref, wglu_ref, bglu_ref, wmix_ref, o_ref):
    y = y_ref[...].astype(F32) * _silu(z_ref[...].astype(F32))
    yn = _rms(y, ng_ref[...]).astype(BF)
    branch_a = _dot(yn, wa_ref[...])
    glu = _dot(y5_ref[...], wglu_ref[...]) + bglu_ref[...]
    branch_b = glu[:, :D_MODEL] * _sigmoid(glu[:, D_MODEL:])
    merged = _sigmoid(ga_ref[...].astype(F32)) * branch_a + _sigmoid(gb_ref[...].astype(F32)) * branch_b
    o_ref[...] = x_ref[...] + _dot(merged.astype(BF), wmix_ref[...])


def _mixout(y, z, y5, ga, gb, x2d, ng, wa, wglu, bglu, wmix):
    t = x2d.shape[0]
    tm = TOKEN_TILE
    row = pl.BlockSpec((tm, D_MODEL), lambda i: (i, 0))
    consts = [ng, wa, wglu, bglu, wmix]
    return pl.pallas_call(
        _mixout_kernel,
        grid=(t // tm,),
        in_specs=[row] * 6 + [_const_spec(c.shape) for c in consts],
        out_specs=row,
        out_shape=jax.ShapeDtypeStruct((t, D_MODEL), F32),
        compiler_params=_params(1),
        name="mixout",
    )(y, z, y5, ga, gb, x2d, *consts)


def _attn_kernel(h_ref, g_ref, wq_ref, wo_ref, k_ref, v_ref, o_ref, obuf, *, nkv, rows_per_seq):
    h = h_ref[...]
    r = h.shape[0]
    q32 = _dot(_rms(h, g_ref[...]).astype(BF), wq_ref[...])
    scale = MEM_HEAD_DIM ** -0.5
    if len(k_ref.shape) == 4:
        rq = rows_per_seq
        nrow = MEM_HEADS * rq
        row_head = lax.shift_right_logical(lax.broadcasted_iota(I32, (nrow, MEM_LEN * MEM_HEADS), 0), int(math.log2(rq)))
        col_head = lax.broadcasted_iota(I32, (nrow, MEM_LEN * MEM_HEADS), 1) & (MEM_HEADS - 1)
        visible = row_head == col_head
        for i in range(nkv):
            qs = jnp.concatenate([q32[i * rq:(i + 1) * rq, hd * MEM_HEAD_DIM:(hd + 1) * MEM_HEAD_DIM]
                                  for hd in range(MEM_HEADS)], axis=0).astype(BF)
            kf = k_ref[i].reshape(MEM_LEN * MEM_HEADS, MEM_HEAD_DIM).astype(BF)
            vf = v_ref[i].reshape(MEM_LEN * MEM_HEADS, MEM_HEAD_DIM).astype(BF)
            s = jnp.where(visible, _dot_nt(qs, kf) * scale, NEG_BIG)
            p = jnp.exp(s - jnp.max(s, axis=-1, keepdims=True))
            p = p / jnp.sum(p, axis=-1, keepdims=True)
            o = _dot(p.astype(BF), vf)
            for hd in range(MEM_HEADS):
                obuf[i * rq:(i + 1) * rq, hd * MEM_HEAD_DIM:(hd + 1) * MEM_HEAD_DIM] = o[hd * rq:(hd + 1) * rq]
    else:
        qv = q32.astype(BF)
        for hd in range(MEM_HEADS):
            sl = slice(hd * MEM_HEAD_DIM, (hd + 1) * MEM_HEAD_DIM)
            s = _dot_nt(qv[:, sl], k_ref[0, :, sl].astype(BF)) * scale
            p = jnp.exp(s - jnp.max(s, axis=-1, keepdims=True))
            p = p / jnp.sum(p, axis=-1, keepdims=True)
            obuf[:, sl] = _dot(p.astype(BF), v_ref[0, :, sl].astype(BF))
    o_ref[...] = h + _dot(obuf[...].astype(BF), wo_ref[...])


def _attn(h2d, g, wq, wo, k3, v3, *, rows, nkv, rows_per_seq):
    t = h2d.shape[0]
    nsteps = t // rows
    steps_per_kv = nsteps // (k3.shape[0] // nkv)
    row = pl.BlockSpec((rows, D_MODEL), lambda i: (i, 0))
    if k3.ndim == 4:
        kv = pl.BlockSpec((nkv, MEM_LEN, MEM_HEADS, MEM_HEAD_DIM), lambda i: (i // steps_per_kv, 0, 0, 0))
    else:
        kv = pl.BlockSpec((nkv, MEM_LEN, D_MODEL), lambda i: (i // steps_per_kv, 0, 0))
    consts = [g, wq, wo]
    return pl.pallas_call(
        functools.partial(_attn_kernel, nkv=nkv, rows_per_seq=rows_per_seq),
        grid=(nsteps,),
        in_specs=[row] + [_const_spec(c.shape) for c in consts] + [kv, kv],
        out_specs=row,
        out_shape=jax.ShapeDtypeStruct((t, D_MODEL), F32),
        scratch_shapes=[pltpu.VMEM((rows, D_MODEL), F32)],
        compiler_params=_params(1),
        name="attn",
    )(h2d, *consts, k3, v3)


ROUTER_LANE0 = MOE_GROUPS
ROUTE_ROWS = 8


def _router_kernel(hp_ref, hs_ref, g_ref, wh_ref, wm_ref, wl_ref, b_ref, tri_ref, eye_ref, rows_ref, wt_ref, cnt_ref,
                   carry, *, np_steps):
    i = pl.program_id(0)

    @pl.when(i == 0)
    def _():
        carry[...] = jnp.zeros_like(carry)

    h = jnp.where(i < np_steps, hp_ref[...], hs_ref[...])
    xn = _rms(h, g_ref[...])
    xh, xm, xl = _split3(xn)
    wh, wm, wl = wh_ref[...], wm_ref[...], wl_ref[...]
    logits = (_dot(xh, wh) + _dot(xh, wm) + _dot(xm, wh) + _dot(xh, wl) + _dot(xm, wm) + _dot(xl, wh)) + b_ref[...]
    lane = lax.broadcasted_iota(I32, logits.shape, 1)
    is_g = lane < MOE_GROUPS
    lg = jnp.where(is_g, logits, NEG_BIG)
    mg = jnp.max(lg, axis=-1, keepdims=True)
    gidx = jnp.min(jnp.where(lg == mg, lane, LANES), axis=-1, keepdims=True)
    g_w = 1.0 / jnp.sum(jnp.where(is_g, jnp.exp(lg - mg), 0.0), axis=-1, keepdims=True)
    e_lane = lane - ROUTER_LANE0
    in_grp = ((e_lane >= 0) & (e_lane < MOE_EXPERTS)
              & (lax.shift_right_logical(jnp.maximum(e_lane, 0), 3) == gidx))
    le = jnp.where(in_grp, logits, NEG_BIG)
    m1 = jnp.max(le, axis=-1, keepdims=True)
    i1 = jnp.min(jnp.where(le == m1, lane, LANES), axis=-1, keepdims=True)
    le2 = jnp.where(lane == i1, NEG_BIG, le)
    m2 = jnp.max(le2, axis=-1, keepdims=True)
    i2 = jnp.min(jnp.where(le2 == m2, lane, LANES), axis=-1, keepdims=True)
    ratio = jnp.exp(m2 - m1)
    w1 = g_w / (1.0 + ratio)
    w2 = g_w * ratio / (1.0 + ratio)
    wt_ref[...] = jnp.where(lane == 0, w1, jnp.where(lane == 1, w2, 0.0))

    oh1 = jnp.where(lane == i1, 1.0, 0.0)
    oh2 = jnp.where(lane == i2, 1.0, 0.0)
    tri = tri_ref[...]
    c0 = carry[...]
    tot1 = jnp.sum(oh1, axis=0, keepdims=True)
    r1 = jnp.sum(oh1 * (_dot(tri, oh1.astype(BF)) + c0), axis=-1, keepdims=True)
    r2 = jnp.sum(oh2 * (_dot(tri, oh2.astype(BF)) + (c0 + tot1)), axis=-1, keepdims=True)
    c1 = c0 + tot1 + jnp.sum(oh2, axis=0, keepdims=True)
    carry[...] = c1
    cols = jnp.where(lane == 0, (i1 - ROUTER_LANE0).astype(F32),
                     jnp.where(lane == 1, (i2 - ROUTER_LANE0).astype(F32),
                               jnp.where(lane == 2, r1, jnp.where(lane == 3, r2, 0.0))))
    rows_ref[0] = _dot_exact(eye_ref[...], cols, nt=True).astype(I32)
    cnt_ref[...] = jnp.broadcast_to(c1, cnt_ref.shape).astype(I32)


def _router(hp, hs, g, wh, wm, wl, b):
    tm = TOKEN_TILE
    np_steps = hp.shape[0] // tm
    ns_steps = hs.shape[0] // tm
    t = hp.shape[0] + hs.shape[0]
    tri = jnp.asarray(np.tril(np.ones((tm, tm), np.float32), -1)).astype(BF)
    eye8 = jnp.asarray(np.eye(ROUTE_ROWS, LANES, dtype=np.float32)).astype(BF)
    consts = [g, wh, wm, wl, b, tri, eye8]
    return pl.pallas_call(
        functools.partial(_router_kernel, np_steps=np_steps),
        grid=(np_steps + ns_steps,),
        in_specs=[pl.BlockSpec((tm, D_MODEL), lambda i: (jnp.minimum(i, np_steps - 1), 0)),
                  pl.BlockSpec((tm, D_MODEL), lambda i: (jnp.maximum(i - np_steps, 0), 0))]
                 + [_const_spec(c.shape) for c in consts],
        out_specs=[pl.BlockSpec((1, ROUTE_ROWS, tm), lambda i: (i, 0, 0)),
                   pl.BlockSpec((tm, LANES), lambda i: (i, 0)), _const_spec((8, LANES))],
        out_shape=[jax.ShapeDtypeStruct((t // tm, ROUTE_ROWS, tm), I32), jax.ShapeDtypeStruct((t, LANES), F32),
                   jax.ShapeDtypeStruct((8, LANES), I32)],
        scratch_shapes=[pltpu.VMEM((1, LANES), F32)],
        compiler_params=_params(1),
        name="router",
    )(hp, hs, *consts)


SLABS = D_MODEL // LANES


def _token_rows(t):
    return pl.ds(pl.multiple_of(t * SLABS, SLABS), SLABS)


def _to_token_major(ref, x, base=0):
    tm = x.shape[0]
    for s in range(SLABS):
        ref[pl.ds(base + s, tm, stride=SLABS), :] = x[:, s * LANES:(s + 1) * LANES]


def _from_token_major(ref, tm, base=0):
    return jnp.concatenate([ref[pl.ds(base + s, tm, stride=SLABS), :] for s in range(SLABS)], axis=1)


def _sorted_row(ps_ref, rt_ref, k, t):
    return ps_ref[rt_ref[0, k, t]] + rt_ref[0, 2 + k, t]


def _scatter_kernel(ps_ref, ve_ref, pe_ref, hp_ref, hs_ref, g_ref, rt_ref, xs_ref, xbuf, zbuf, sem, zsem, *, np_steps):
    i = pl.program_id(0)
    tm = TOKEN_TILE

    tile_rows = MOE_ROW_TILE * SLABS
    n_tiles = xs_ref.shape[0] // tile_rows

    def pad_copy(r):
        return pltpu.make_async_copy(zbuf.at[pl.ds(0, SLABS)], xs_ref.at[_token_rows(r)], zsem)

    def tile_copy(j):
        return pltpu.make_async_copy(zbuf, xs_ref.at[pl.ds(pl.multiple_of(j * tile_rows, tile_rows), tile_rows)], zsem)

    @pl.when(i == 0)
    def _():
        zbuf[...] = jnp.zeros_like(zbuf)
        first_unused = pe_ref[MOE_EXPERTS - 1] // MOE_ROW_TILE

        def tile_start(j, c):
            tile_copy(j).start()
            return c

        def tile_wait(j, c):
            tile_copy(0).wait()
            return c

        lax.fori_loop(first_unused, n_tiles, tile_start, 0)
        lax.fori_loop(first_unused, n_tiles, tile_wait, 0)

        def per_expert(e, n):
            lo = ve_ref[e]
            hi = pe_ref[e]

            def body(r, c):
                pad_copy(r).start()
                return c

            lax.fori_loop(lo, hi, body, 0)
            return n + (hi - lo)

        n_pad = lax.fori_loop(0, MOE_EXPERTS, per_expert, 0)

        def pad_wait(r, c):
            pad_copy(0).wait()
            return c

        lax.fori_loop(0, n_pad, pad_wait, 0)

    h = jnp.where(i < np_steps, hp_ref[...], hs_ref[...])
    _to_token_major(xbuf, _rms(h, g_ref[...]))

    def row_copy(t, dst_row):
        return pltpu.make_async_copy(xbuf.at[_token_rows(t)], xs_ref.at[_token_rows(dst_row)], sem)

    def issue(t, c):
        row_copy(t, _sorted_row(ps_ref, rt_ref, 0, t)).start()
        row_copy(t, _sorted_row(ps_ref, rt_ref, 1, t)).start(priority=1)
        return c

    def wait(t, c):
        row_copy(0, 0).wait()
        return c

    lax.fori_loop(0, tm, issue, 0, unroll=8)
    lax.fori_loop(0, 2 * tm, wait, 0, unroll=16)


def _scatter(hp, hs, g, route, pstart, valid_end, pend, n_rows):
    tm = TOKEN_TILE
    np_steps = hp.shape[0] // tm
    ns_steps = hs.shape[0] // tm
    grid_spec = pltpu.PrefetchScalarGridSpec(
        num_scalar_prefetch=3,
        grid=(np_steps + ns_steps,),
        in_specs=[pl.BlockSpec((tm, D_MODEL), lambda i, *_: (jnp.minimum(i, np_steps - 1), 0)),
                  pl.BlockSpec((tm, D_MODEL), lambda i, *_: (jnp.maximum(i - np_steps, 0), 0)),
                  pl.BlockSpec(g.shape, lambda i, *_: (0, 0)),
                  pl.BlockSpec((1, ROUTE_ROWS, tm), lambda i, *_: (i, 0, 0), memory_space=pltpu.SMEM)],
        out_specs=pl.BlockSpec(memory_space=pl.ANY),
        scratch_shapes=[pltpu.VMEM((tm * SLABS, LANES), F32), pltpu.VMEM((MOE_ROW_TILE * SLABS, LANES), F32),
                        pltpu.SemaphoreType.DMA(()), pltpu.SemaphoreType.DMA(())],
    )
    return pl.pallas_call(
        functools.partial(_scatter_kernel, np_steps=np_steps),
        grid_spec=grid_spec,
        out_shape=jax.ShapeDtypeStruct((n_rows * SLABS, LANES), F32),
        compiler_params=_params(1),
        name="scatter",
    )(pstart, valid_end, pend, hp, hs, g, route)


def _expert_kernel(te_ref, nu_ref, x_ref, wg_ref, wu_ref, wd_ref, y_ref, wg16, wu16, wd16):
    i = pl.program_id(0)
    tm = MOE_ROW_TILE

    @pl.when(i < nu_ref[0])
    def _():
        @pl.when((i == 0) | (te_ref[i] != te_ref[jnp.maximum(i - 1, 0)]))
        def _():
            wg16[...] = wg_ref[0].astype(BF)
            wu16[...] = wu_ref[0].astype(BF)
            wd16[...] = wd_ref[0].astype(BF)

        x = _from_token_major(x_ref, tm).astype(BF)
        gate = _dot(x, wg16[...])
        up = _dot(x, wu16[...])
        _to_token_major(y_ref, _dot((_silu(gate) * up).astype(BF), wd16[...]))

    @pl.when(i >= nu_ref[0])
    def _():
        y_ref[...] = jnp.zeros_like(y_ref)


def _experts(x_sorted, tile_expert, n_used, wg, wu, wd):
    tm = MOE_ROW_TILE
    n_tiles = x_sorted.shape[0] // (tm * SLABS)
    used = lambda i, nu: jnp.minimum(i, nu[0] - 1)
    grid_spec = pltpu.PrefetchScalarGridSpec(
        num_scalar_prefetch=2,
        grid=(n_tiles,),
        in_specs=[pl.BlockSpec((tm * SLABS, LANES), lambda i, te, nu: (used(i, nu), 0)),
                  pl.BlockSpec((1, D_MODEL, MOE_D_FF), lambda i, te, nu: (te[used(i, nu)], 0, 0)),
                  pl.BlockSpec((1, D_MODEL, MOE_D_FF), lambda i, te, nu: (te[used(i, nu)], 0, 0)),
                  pl.BlockSpec((1, MOE_D_FF, D_MODEL), lambda i, te, nu: (te[used(i, nu)], 0, 0))],
        out_specs=pl.BlockSpec((tm * SLABS, LANES), lambda i, te, nu: (i, 0)),
        scratch_shapes=[pltpu.VMEM((D_MODEL, MOE_D_FF), BF), pltpu.VMEM((D_MODEL, MOE_D_FF), BF),
                        pltpu.VMEM((MOE_D_FF, D_MODEL), BF)],
    )
    return pl.pallas_call(
        _expert_kernel,
        grid_spec=grid_spec,
        out_shape=jax.ShapeDtypeStruct(x_sorted.shape, F32),
        compiler_params=_params(1),
        name="experts",
    )(tile_expert, n_used, x_sorted, wg, wu, wd)


def _combine_kernel(ps_ref, h_ref, wt_ref, g_ref, pcur_ref, pnext_ref, ys_ref, o_ref, ybuf, sem, *, n):
    i = pl.program_id(0)
    tm = TOKEN_TILE

    def row_copy(src_row, j, slot):
        return pltpu.make_async_copy(ys_ref.at[_token_rows(src_row)], ybuf.at[slot, _token_rows(j)], sem.at[slot])

    def issue(rt_ref, slot):
        def body(t, c):
            row_copy(_sorted_row(ps_ref, rt_ref, 0, t), t, slot).start()
            row_copy(_sorted_row(ps_ref, rt_ref, 1, t), tm + t, slot).start(priority=1)
            return c

        lax.fori_loop(0, tm, body, 0, unroll=8)

    def finish(slot):
        def wait(j, c):
            row_copy(0, 0, slot).wait()
            return c

        lax.fori_loop(0, 2 * tm, wait, 0, unroll=16)
        wt = wt_ref[...]
        y1 = _from_token_major(ybuf.at[slot], tm)
        y2 = _from_token_major(ybuf.at[slot], tm, base=tm * SLABS)
        o_ref[...] = _rms(h_ref[...] + wt[:, 0:1] * y1 + wt[:, 1:2] * y2, g_ref[...])

    @pl.when(i == 0)
    def _():
        issue(pcur_ref, 0)

    for slot in (0, 1):
        @pl.when(lax.rem(i, 2) == slot)
        def _(slot=slot):
            @pl.when(i + 1 < n)
            def _():
                issue(pnext_ref, 1 - slot)

            finish(slot)


def _combine(h2d, wt, g, route, pstart, y_sorted, *, tile_offset):
    t = h2d.shape[0]
    tm = TOKEN_TILE
    nsteps = t // tm
    route_spec = lambda f: pl.BlockSpec((1, ROUTE_ROWS, tm), f, memory_space=pltpu.SMEM)
    grid_spec = pltpu.PrefetchScalarGridSpec(
        num_scalar_prefetch=1,
        grid=(nsteps,),
        in_specs=[pl.BlockSpec((tm, D_MODEL), lambda i, ps: (i, 0)),
                  pl.BlockSpec((tm, LANES), lambda i, ps: (i + tile_offset, 0)),
                  pl.BlockSpec(g.shape, lambda i, ps: (0, 0)),
                  route_spec(lambda i, ps: (i + tile_offset, 0, 0)),
                  route_spec(lambda i, ps: (jnp.minimum(i + 1, nsteps - 1) + tile_offset, 0, 0)),
                  pl.BlockSpec(memory_space=pl.ANY)],
        out_specs=pl.BlockSpec((tm, D_MODEL), lambda i, ps: (i, 0)),
        scratch_shapes=[pltpu.VMEM((2, 2 * tm * SLABS, LANES), F32), pltpu.SemaphoreType.DMA((2,))],
    )
    return pl.pallas_call(
        functools.partial(_combine_kernel, n=nsteps),
        grid_spec=grid_spec,
        out_shape=jax.ShapeDtypeStruct((t, D_MODEL), F32),
        compiler_params=_params(1),
        name="combine",
    )(pstart, h2d, wt, g, route, route, y_sorted)


def _moe_plan(cnt, t):
    rt = MOE_ROW_TILE
    n_tiles = (2 * t + MOE_EXPERTS * (rt - 1) + rt - 1) // rt
    counts = cnt[0, ROUTER_LANE0:ROUTER_LANE0 + MOE_EXPERTS]
    padded = ((counts + rt - 1) // rt) * rt
    pend = jnp.cumsum(padded).astype(I32)
    pstart = pend - padded
    valid_end = (pstart + counts).astype(I32)
    tile_start = jnp.arange(n_tiles, dtype=I32) * rt
    tile_expert = jnp.sum((tile_start[:, None] >= pend[None, :]).astype(I32), axis=1)
    tile_expert = jnp.minimum(tile_expert, MOE_EXPERTS - 1).astype(I32)
    n_used = (pend[-1] // rt).astype(I32).reshape(1)
    return pstart, valid_end, pend, tile_expert, n_used, n_tiles * rt


GRANULE = 8
LOCAL_USED = 2 * TOKEN_TILE + MOE_EXPERTS * (GRANULE - 1)
LOCAL_ROWS = 1280
SPARE_GRANULES = (LOCAL_ROWS - LOCAL_USED) // GRANULE
TILE_GRANULES = MOE_ROW_TILE // GRANULE


def _route_kernel(hp_ref, hs_ref, g_ref, wcat_ref, wh_ref, b_ref, tri_ref, eye_ref, upper_ref,
                  xl_ref, idx_ref, wt_ref, seg_ref, *, np_steps):
    i = pl.program_id(0)
    h = jnp.where(i < np_steps, hp_ref[...], hs_ref[...])
    xn = _rms(h, g_ref[...])
    tm = xn.shape[0]
    xh = xn.astype(BF)
    xm = (xn - xh.astype(F32)).astype(BF)
    both = _dot(xh, wcat_ref[...])
    logits = both[:, :LANES] + both[:, LANES:] + _dot(xm, wh_ref[...]) + b_ref[...]
    lane = lax.broadcasted_iota(I32, logits.shape, 1)
    lane_f = lane.astype(F32)
    no_lane = float(LANES)
    is_g = lane < MOE_GROUPS
    lg = jnp.where(is_g, logits, NEG_BIG)
    mg = jnp.max(lg, axis=-1, keepdims=True)
    gidx = jnp.min(jnp.where(lg == mg, lane_f, no_lane), axis=-1, keepdims=True).astype(I32)
    g_w = 1.0 / jnp.sum(jnp.where(is_g, jnp.exp(lg - mg), 0.0), axis=-1, keepdims=True)
    e_lane = lane - ROUTER_LANE0
    in_grp = ((e_lane >= 0) & (e_lane < MOE_EXPERTS)
              & (lax.shift_right_logical(jnp.maximum(e_lane, 0), 3) == gidx))
    le = jnp.where(in_grp, logits, NEG_BIG)
    m1 = jnp.max(le, axis=-1, keepdims=True)
    i1 = jnp.min(jnp.where(le == m1, lane_f, no_lane), axis=-1, keepdims=True)
    le2 = jnp.where(lane_f == i1, NEG_BIG, le)
    m2 = jnp.max(le2, axis=-1, keepdims=True)
    i2 = jnp.min(jnp.where(le2 == m2, lane_f, no_lane), axis=-1, keepdims=True)
    ratio = jnp.exp(m2 - m1)
    w1 = g_w / (1.0 + ratio)
    w2 = g_w * ratio / (1.0 + ratio)
    wt_ref[...] = jnp.where(lane == 0, w1, jnp.where(lane == 1, w2, 0.0))

    oh1 = jnp.where(lane_f == i1, 1.0, 0.0)
    oh2 = jnp.where(lane_f == i2, 1.0, 0.0)
    tot1 = jnp.sum(oh1, axis=0, keepdims=True)
    cnt = (tot1 + jnp.sum(oh2, axis=0, keepdims=True)).astype(I32)
    padded = lax.shift_left(lax.shift_right_logical(cnt + (GRANULE - 1), 3), 3)
    pad8 = jnp.broadcast_to(padded.astype(F32), (8, LANES)).astype(BF)
    lstart = _dot(pad8, upper_ref[...])[0:1, :]
    tri = tri_ref[...]
    l1 = jnp.sum(oh1 * (_dot(tri, oh1.astype(BF)) + lstart), axis=-1, keepdims=True)
    l2 = jnp.sum(oh2 * (_dot(tri, oh2.astype(BF)) + (lstart + tot1)), axis=-1, keepdims=True)
    idx_ref[...] = jnp.where(lane == 0, l1.astype(I32), jnp.where(lane == 1, l2.astype(I32), 0))
    sub = lax.broadcasted_iota(I32, (8, LANES), 0)
    seg_ref[0] = jnp.where(sub == 0, jnp.broadcast_to(cnt, (8, LANES)),
                           jnp.where(sub == 1, jnp.broadcast_to(lstart.astype(I32), (8, LANES)), 0))

    cols = jnp.where(lane == 0, l1, jnp.where(lane == 1, l2, 0.0))
    rows = _dot_exact(eye_ref[...], cols, nt=True)
    r_id = lax.broadcasted_iota(I32, (LOCAL_ROWS, tm), 0).astype(F32)
    perm = jnp.where(r_id == rows[0:1, :], 1.0, 0.0) + jnp.where(r_id == rows[1:2, :], 1.0, 0.0)
    xl_ref[...] = _dot(perm.astype(BF), xh)


def _route(hp, hs, g, wcat, wh, b):
    tm = TOKEN_TILE
    np_steps = hp.shape[0] // tm
    ns_steps = hs.shape[0] // tm
    nt = np_steps + ns_steps
    tri = jnp.asarray(np.tril(np.ones((tm, tm), np.float32), -1)).astype(BF)
    eye8 = jnp.asarray(np.eye(8, LANES, dtype=np.float32)).astype(BF)
    upper = jnp.asarray(np.triu(np.ones((LANES, LANES), np.float32), 1)).astype(BF)
    consts = [g, wcat, wh, b, tri, eye8, upper]
    return pl.pallas_call(
        functools.partial(_route_kernel, np_steps=np_steps),
        grid=(nt,),
        in_specs=[pl.BlockSpec((tm, D_MODEL), lambda i: (jnp.minimum(i, np_steps - 1), 0)),
                  pl.BlockSpec((tm, D_MODEL), lambda i: (jnp.maximum(i - np_steps, 0), 0))]
                 + [_const_spec(c.shape) for c in consts],
        out_specs=[pl.BlockSpec((LOCAL_ROWS, D_MODEL), lambda i: (i, 0)),
                   pl.BlockSpec((tm, LANES), lambda i: (i, 0)),
                   pl.BlockSpec((tm, LANES), lambda i: (i, 0)),
                   pl.BlockSpec((1, 8, LANES), lambda i: (i, 0, 0))],
        out_shape=[jax.ShapeDtypeStruct((nt * LOCAL_ROWS, D_MODEL), F32),
                   jax.ShapeDtypeStruct((nt * tm, LANES), I32),
                   jax.ShapeDtypeStruct((nt * tm, LANES), F32),
                   jax.ShapeDtypeStruct((nt, 8, LANES), I32)],
        compiler_params=_params(1),
        name="route",
    )(hp, hs, *consts)


def _granule_plan(seg, n_tiles):
    cnt = seg[:, 0, ROUTER_LANE0:ROUTER_LANE0 + MOE_EXPERTS]
    lstart = seg[:, 1, ROUTER_LANE0:ROUTER_LANE0 + MOE_EXPERTS]
    ntt = cnt.shape[0]
    ng = (cnt + GRANULE - 1) // GRANULE
    cum = jnp.cumsum(ng, axis=0)
    first = (jnp.arange(ntt, dtype=I32)[:, None] * LOCAL_ROWS + lstart) // GRANULE - (cum - ng)
    total = cum[-1]
    tiles_e = (total + TILE_GRANULES - 1) // TILE_GRANULES
    tend = jnp.cumsum(tiles_e)
    tstart = tend - tiles_e
    tile = jnp.arange(n_tiles, dtype=I32)
    te = jnp.minimum(jnp.sum((tile[:, None] >= tend[None, :]).astype(I32), axis=1), MOE_EXPERTS - 1)
    n_used = tend[-1].astype(I32).reshape(1)
    sel = (te[:, None] == jnp.arange(MOE_EXPERTS, dtype=I32)[None, :]).astype(I32)
    pick = lambda table: jnp.sum(sel[:, None, :] * table[None, :, :], axis=2)
    cum_t, first_t = pick(cum), pick(first)
    off = (tile - jnp.sum(sel * tstart[None, :], axis=1))[:, None] * TILE_GRANULES + jnp.arange(TILE_GRANULES, dtype=I32)[None, :]
    valid = (off < jnp.sum(sel * total[None, :], axis=1)[:, None]) & (tile < n_used[0])[:, None]
    j = jnp.sum((cum_t[:, None, :] <= off[:, :, None]).astype(I32), axis=2)
    jsel = (j[:, :, None] == jnp.arange(ntt, dtype=I32)[None, None, :]).astype(I32)
    src = jnp.sum(jsel * first_t[:, None, :], axis=2) + off
    spare = lambda k: ((k // SPARE_GRANULES) * LOCAL_ROWS + LOCAL_USED) // GRANULE + k % SPARE_GRANULES
    assert (ntt - 1) * SPARE_GRANULES >= 2 * TILE_GRANULES
    zero_granule = spare((ntt - 1) * SPARE_GRANULES)
    trash = spare((tile % 2)[:, None] * TILE_GRANULES + jnp.arange(TILE_GRANULES, dtype=I32)[None, :])
    gsrc = jnp.where(valid, src, zero_granule).astype(I32).reshape(-1)
    gdst = jnp.where(valid, src, trash).astype(I32).reshape(-1)
    n_valid = jnp.sum(valid.astype(I32), axis=1).astype(I32)
    return te.astype(I32), n_used, gsrc, gdst, n_valid


def _grouped_kernel(te_ref, nu_ref, gsrc_ref, gdst_ref, nv_ref, xin_ref, wg_ref, wu_ref, wd_ref, xout_ref,
                    xbuf, ybuf, wg16, wu16, wd16, sem_in, sem_out):
    i = pl.program_id(0)
    nu = nu_ref[0]
    tm = MOE_ROW_TILE

    def hbm_rows(table_ref, tile, g):
        return pl.ds(pl.multiple_of(table_ref[tile * TILE_GRANULES + g] * GRANULE, GRANULE), GRANULE)

    def fetch_start(tile, slot):
        for g in range(TILE_GRANULES):
            pltpu.make_async_copy(xin_ref.at[hbm_rows(gsrc_ref, tile, g)],
                                  xbuf.at[slot, pl.ds(g * GRANULE, GRANULE)], sem_in.at[slot]).start()

    def store_start(tile, slot):
        for g in range(TILE_GRANULES):
            pltpu.make_async_copy(ybuf.at[slot, pl.ds(g * GRANULE, GRANULE)],
                                  xout_ref.at[hbm_rows(gdst_ref, tile, g)], sem_out.at[slot]).start()

    def fetch_wait(slot):
        pltpu.make_async_copy(xin_ref.at[pl.ds(0, tm)], xbuf.at[slot], sem_in.at[slot]).wait()

    def store_wait(slot):
        pltpu.make_async_copy(ybuf.at[slot], xout_ref.at[pl.ds(0, tm)], sem_out.at[slot]).wait()

    def step(slot):
        @pl.when(i == 0)
        def _():
            ybuf[...] = jnp.zeros_like(ybuf)
            fetch_start(0, 0)

        @pl.when(i + 1 < nu)
        def _():
            fetch_start(i + 1, 1 - slot)

        @pl.when(i >= 2)
        def _():
            store_wait(slot)

        @pl.when((i == 0) | (te_ref[i] != te_ref[jnp.maximum(i - 1, 0)]))
        def _():
            wg16[...] = wg_ref[0].astype(BF)
            wu16[...] = wu_ref[0].astype(BF)
            wd16[...] = wd_ref[0].astype(BF)

        fetch_wait(slot)
        n_valid = nv_ref[i]
        quarter = TILE_GRANULES // 4
        for part in range(1, 5):
            rows = part * quarter * GRANULE

            @pl.when((n_valid > (part - 1) * quarter) & (n_valid <= part * quarter))
            def _(rows=rows):
                x = xbuf[slot, 0:rows, :].astype(BF)
                gate = _dot(x, wg16[...])
                up = _dot(x, wu16[...])
                ybuf[slot, 0:rows, :] = _dot((_silu(gate) * up).astype(BF), wd16[...])

        store_start(i, slot)

        @pl.when(i == nu - 1)
        def _():
            store_wait(slot)

            @pl.when(i >= 1)
            def _():
                store_wait(1 - slot)

    for slot in (0, 1):
        @pl.when((i < nu) & (lax.rem(i, 2) == slot))
        def _(slot=slot):
            step(slot)


def _grouped(x_local, te, n_used, gsrc, gdst, n_valid, wg, wu, wd):
    tm = MOE_ROW_TILE
    n_tiles = te.shape[0]
    used = lambda i, nu: jnp.minimum(i, jnp.maximum(nu[0] - 1, 0))
    wspec = lambda shape: pl.BlockSpec((1,) + shape, lambda i, te, nu, *_: (te[used(i, nu)], 0, 0))
    grid_spec = pltpu.PrefetchScalarGridSpec(
        num_scalar_prefetch=5,
        grid=(n_tiles,),
        in_specs=[pl.BlockSpec(memory_space=pl.ANY), wspec((D_MODEL, MOE_D_FF)), wspec((D_MODEL, MOE_D_FF)),
                  wspec((MOE_D_FF, D_MODEL))],
        out_specs=pl.BlockSpec(memory_space=pl.ANY),
        scratch_shapes=[pltpu.VMEM((2, tm, D_MODEL), F32), pltpu.VMEM((2, tm, D_MODEL), F32),
                        pltpu.VMEM((D_MODEL, MOE_D_FF), BF), pltpu.VMEM((D_MODEL, MOE_D_FF), BF),
                        pltpu.VMEM((MOE_D_FF, D_MODEL), BF),
                        pltpu.SemaphoreType.DMA((2,)), pltpu.SemaphoreType.DMA((2,))],
    )
    return pl.pallas_call(
        _grouped_kernel,
        grid_spec=grid_spec,
        out_shape=jax.ShapeDtypeStruct(x_local.shape, F32),
        input_output_aliases={5: 0},
        compiler_params=_params(1),
        name="grouped",
    )(te, n_used, gsrc, gdst, n_valid, x_local, wg, wu, wd)


def _merge_kernel(h_ref, idx_ref, wt_ref, g_ref, yl_ref, o_ref):
    tm = h_ref.shape[0]
    idx = idx_ref[...]
    wt = wt_ref[...]
    r_id = lax.broadcasted_iota(I32, (tm, LOCAL_ROWS), 1)
    sel = jnp.where(r_id == idx[:, 0:1], wt[:, 0:1], 0.0) + jnp.where(r_id == idx[:, 1:2], wt[:, 1:2], 0.0)
    moe = _dot(sel.astype(BF), yl_ref[...].astype(BF))
    o_ref[...] = _rms(h_ref[...] + moe, g_ref[...])


def _merge(h2d, idx, wt, g, y_local, *, tile_offset):
    t = h2d.shape[0]
    tm = TOKEN_TILE
    return pl.pallas_call(
        _merge_kernel,
        grid=(t // tm,),
        in_specs=[pl.BlockSpec((tm, D_MODEL), lambda i: (i, 0)),
                  pl.BlockSpec((tm, LANES), lambda i: (i + tile_offset, 0)),
                  pl.BlockSpec((tm, LANES), lambda i: (i + tile_offset, 0)),
                  _const_spec(g.shape),
                  pl.BlockSpec((LOCAL_ROWS, D_MODEL), lambda i: (i + tile_offset, 0))],
        out_specs=pl.BlockSpec((tm, D_MODEL), lambda i: (i, 0)),
        out_shape=jax.ShapeDtypeStruct((t, D_MODEL), F32),
        compiler_params=_params(1),
        name="merge",
    )(h2d, idx, wt, g, y_local)


def _perm_matrix(bt, tl):
    m = bt * tl
    p = np.zeros((m, m), np.float32)
    for l in range(tl):
        for b in range(bt):
            p[l * bt + b, b * tl + l] = 1.0
    return p


def _s5_operators(lam_re, lam_im, log_step, b_re, b_im, c_re, c_im):
    lam = lax.complex(lam_re, lam_im)
    step = jnp.exp(log_step)[:, None]
    lam_bar = jnp.exp(lam * step)
    b_bar = ((lam_bar - 1.0) / lam)[..., None] * lax.complex(b_re, b_im)
    gl = LANES // S5_GROUP
    eye = jnp.eye(gl, dtype=F32)

    def in_block(bpart):
        bp = bpart.reshape(S5_SLABS, gl, S5_STATE, S5_GROUP)
        blk = jnp.einsum("sgnk,gh->sgkhn", bp, eye)
        return blk.reshape(S5_SLABS, LANES, gl * S5_STATE)

    def out_block(cpart):
        cp = cpart.reshape(S5_SLABS, gl, S5_GROUP, S5_STATE)
        blk = jnp.einsum("sgkn,gh->sgnhk", cp, eye)
        return blk.reshape(S5_SLABS, gl * S5_STATE, LANES)

    bblk = jnp.concatenate([in_block(jnp.real(b_bar)), in_block(jnp.imag(b_bar))], axis=2).astype(BF)
    cblk = jnp.concatenate([out_block(c_re), out_block(-c_im)], axis=1).astype(BF)
    lre = jnp.broadcast_to(jnp.real(lam_bar).reshape(1, S5_FLAT), (S5_SEQ_TILE, S5_FLAT))
    lim = jnp.broadcast_to(jnp.imag(lam_bar).reshape(1, S5_FLAT), (S5_SEQ_TILE, S5_FLAT))
    return bblk, cblk, lre, lim


def _pad_lanes(v, width=LANES):
    return jnp.pad(v, [(0, 0)] * (v.ndim - 1) + [(0, width - v.shape[-1])])


def kernel(x_prompt, x_sample, state_conv, state_ssd, state_s5_re, state_s5_im, cache_mem_k, cache_mem_v, mem_prompt, norm_mix_g, w_in, conv_w, conv_b, ssd_dt_bias, ssd_a_log, ssd_d, ssd_norm_g, w_ssd_branch, s5_lambda_re, s5_lambda_im, s5_log_step, s5_b_re, s5_b_im, s5_c_re, s5_c_im, s5_d, w_glu, b_glu, w_mix_out, norm_mem_q_g, norm_mem_kv_g, w_mem_q, w_mem_k, w_mem_v, w_mem_o, norm_ffn_g, w_router_group, b_router_group, w_router_expert, b_router_expert, w_exp_gate, w_exp_up, w_exp_down, norm_final_g):
    bp, lp, _ = x_prompt.shape
    bs, ls, _ = x_sample.shape
    tp, ts = bp * lp, bs * ls
    row1 = lambda v: v.reshape(1, -1).astype(F32)

    w_in0 = w_in[0]
    o_xbc = D_MODEL
    o_dt = o_xbc + SSD_CONV_DIM
    o_u5 = o_dt + SSD_HEADS
    w_main = jnp.concatenate([w_in0[:, :o_dt], w_in0[:, o_u5:]], axis=1).astype(BF)
    w_dt = _pad_lanes(w_in0[:, o_dt:o_u5]).astype(BF)
    dt_bias = _pad_lanes(row1(ssd_dt_bias[0]))
    a_log = _pad_lanes(row1(ssd_a_log[0]))
    dskip = jnp.repeat(ssd_d[0].astype(F32), SSD_HEADDIM).reshape(1, D_MODEL)
    eexp_np = np.zeros((LANES, D_MODEL), np.float32)
    for h in range(SSD_HEADS):
        eexp_np[h, h * SSD_HEADDIM:(h + 1) * SSD_HEADDIM] = 1.0
    eexp = jnp.asarray(eexp_np).astype(BF)
    eye16 = jnp.asarray(np.eye(SSD_HEADS, LANES, dtype=np.float32)).astype(BF)
    bblk, cblk, lre, lim = _s5_operators(s5_lambda_re[0], s5_lambda_im[0], s5_log_step[0], s5_b_re[0],
                                         s5_b_im[0], s5_c_re[0], s5_c_im[0])
    s5d = row1(s5_d[0])
    w_kv = jnp.concatenate([w_mem_k[0], w_mem_v[0]], axis=1).astype(BF)
    w_r = _pad_lanes(jnp.concatenate([w_router_group[0], w_router_expert[0]], axis=1).astype(F32))
    wr_h, wr_m, wr_l = _split3(w_r)
    b_r = _pad_lanes(row1(jnp.concatenate([b_router_group[0], b_router_expert[0]])))
    wg = w_exp_gate[0].reshape(MOE_EXPERTS, D_MODEL, MOE_D_FF)
    wu = w_exp_up[0].reshape(MOE_EXPERTS, D_MODEL, MOE_D_FF)
    wd = w_exp_down[0].reshape(MOE_EXPERTS, MOE_D_FF, D_MODEL)

    def mixer(x2d, batch, seqlen, conv0, h0, s5re0, s5im0):
        z, xbc, u5, ga, gb, dt = _inproj(x2d, row1(norm_mix_g[0]), w_main, w_dt, dt_bias)
        y_ssd, new_conv, new_ssd = _ssd(xbc, dt, conv_w[0].astype(F32), row1(conv_b[0]), a_log, dskip, eexp, eye16,
                                        conv0, h0, batch=batch, seqlen=seqlen)
        if h0 is None:
            tl = S5_TIME_TILE
            u_in = u5.reshape(batch, seqlen, D_MODEL)
        else:
            tl = seqlen
            u_in = u5
        perm = _perm_matrix(S5_SEQ_TILE, tl)
        y5, new_re, new_im = _s5(u_in, jnp.asarray(perm).astype(BF), jnp.asarray(perm.T).astype(BF), bblk, cblk,
                                 lre, lim, s5d, s5re0, s5im0, batch=batch, seqlen=seqlen, tl=tl)
        h1 = _mixout(y_ssd, z, y5.reshape(-1, D_MODEL), ga, gb, x2d, row1(ssd_norm_g[0]),
                     w_ssd_branch[0].astype(BF), w_glu[0].astype(BF), row1(b_glu[0]), w_mix_out[0].astype(BF))
        return h1, new_conv, new_ssd, new_re, new_im

    xp2 = x_prompt.reshape(tp, D_MODEL)
    xs2 = x_sample.reshape(ts, D_MODEL)
    h1p, conv_p, ssd_p, re_p, im_p = mixer(xp2, bp, lp, None, None, None, None)
    h1s, conv_s, ssd_s, re_s, im_s = mixer(xs2, bs, ls, state_conv[0], state_ssd[0],
                                           state_s5_re[0].reshape(bs, S5_FLAT), state_s5_im[0].reshape(bs, S5_FLAT))

    mk_p, mv_p = _memkv(mem_prompt.reshape(bp * MEM_LEN, D_MODEL), row1(norm_mem_kv_g[0]), w_kv)
    gq = row1(norm_mem_q_g[0])
    wq = w_mem_q[0].astype(BF)
    wo = w_mem_o[0].astype(BF)
    h2p = _attn(h1p, gq, wq, wo, mk_p.reshape(bp, MEM_LEN, D_MODEL), mv_p.reshape(bp, MEM_LEN, D_MODEL),
                rows=TOKEN_TILE, nkv=1, rows_per_seq=lp)
    sample_seqs = 8
    h2s = _attn(h1s, gq, wq, wo, cache_mem_k[0], cache_mem_v[0], rows=sample_seqs * ls, nkv=sample_seqs,
                rows_per_seq=ls)

    g_ffn = row1(norm_ffn_g[0])
    x_local, r_idx, r_wt, seg = _route(h2p, h2s, g_ffn, jnp.concatenate([wr_h, wr_m], axis=1), wr_h, b_r)
    token_tiles = (tp + ts) // TOKEN_TILE
    max_granules = (2 * (tp + ts) + token_tiles * MOE_EXPERTS * (GRANULE - 1)) // GRANULE
    n_tiles = max_granules // TILE_GRANULES + MOE_EXPERTS
    te, n_used, gsrc, gdst, n_valid = _granule_plan(seg, n_tiles)
    y_local = _grouped(x_local, te, n_used, gsrc, gdst, n_valid, wg, wu, wd)
    gf = row1(norm_final_g)
    y_prompt = _merge(h2p, r_idx, r_wt, gf, y_local, tile_offset=0)
    y_sample = _merge(h2s, r_idx, r_wt, gf, y_local, tile_offset=tp // TOKEN_TILE)

    return (y_prompt.reshape(bp, lp, D_MODEL), y_sample.reshape(bs, ls, D_MODEL),
            conv_p[None], ssd_p[None],
            re_p.reshape(1, bp, S5_GROUPS, S5_STATE), im_p.reshape(1, bp, S5_GROUPS, S5_STATE),
            mk_p.reshape(1, bp, MEM_LEN, MEM_HEADS, MEM_HEAD_DIM), mv_p.reshape(1, bp, MEM_LEN, MEM_HEADS, MEM_HEAD_DIM),
            conv_s[None], ssd_s[None],
            re_s.reshape(1, bs, S5_GROUPS, S5_STATE), im_s.reshape(1, bs, S5_GROUPS, S5_STATE))
```

```python
import functools
import math

import numpy as np
import jax
import jax.numpy as jnp
from jax import lax
from jax.experimental import pallas as pl
from jax.experimental.pallas import tpu as pltpu

F32 = jnp.float32
BF = jnp.bfloat16
I32 = jnp.int32

D_MODEL = 1024
SSD_HEADS = 16
SSD_HEADDIM = 64
SSD_GROUPS = 2
SSD_STATE = 128
SSD_CONV = 4
SSD_CONV_DIM = 1536
HEADS_PER_GROUP = SSD_HEADS // SSD_GROUPS
S5_GROUPS = 64
S5_GROUP = 16
S5_STATE = 64
S5_FLAT = S5_GROUPS * S5_STATE
S5_SLABS = D_MODEL // 128
MEM_LEN = 256
MEM_HEADS = 4
MEM_HEAD_DIM = 256
MOE_GROUPS = 4
MOE_EXPERTS_PER_GROUP = 8
MOE_EXPERTS = MOE_GROUPS * MOE_EXPERTS_PER_GROUP
MOE_D_FF = 512
NORM_EPS = 1e-6

LANES = 128
VMEM_LIMIT_BYTES = 56 * 1024 * 1024
TOKEN_TILE = 512
SSD_CHUNK = 128
S5_TIME_TILE = 64
MOE_ROW_TILE = 512
NEG_BIG = -1e30


def _params(n_axes):
    return pltpu.CompilerParams(dimension_semantics=("arbitrary",) * n_axes,
                                vmem_limit_bytes=VMEM_LIMIT_BYTES)


def _const_spec(shape):
    nd = len(shape)
    return pl.BlockSpec(shape, lambda *_: (0,) * nd)


def _dot(a, b):
    return jnp.dot(a, b, preferred_element_type=F32)


def _dot_nt(a, b):
    return lax.dot_general(a, b, (((1,), (1,)), ((), ())), preferred_element_type=F32)


def _split3(v):
    h = v.astype(BF)
    r = v - h.astype(F32)
    m = r.astype(BF)
    l = (r - m.astype(F32)).astype(BF)
    return h, m, l


def _dot_exact(a_bf, v, nt=False):
    f = _dot_nt if nt else _dot
    h, m, l = _split3(v)
    return f(a_bf, h) + f(a_bf, m) + f(a_bf, l)


def _rms(x, g):
    return x * lax.rsqrt(jnp.mean(x * x, axis=-1, keepdims=True) + NORM_EPS) * g


def _sigmoid(x):
    return 0.5 * jnp.tanh(0.5 * x) + 0.5


def _silu(x):
    return x * _sigmoid(x)


def _proj_cols(xb, w_ref, o_ref, c0, width):
    for c in range(0, width, 512):
        ce = min(c + 512, width)
        o_ref[:, c:ce] = _dot(xb, w_ref[:, c0 + c:c0 + ce]).astype(o_ref.dtype)


def _inproj_kernel(x_ref, g_ref, w_ref, wdt_ref, dtb_ref, z_ref, xbc_ref, u5_ref, ga_ref, gb_ref, dt_ref):
    xb = _rms(x_ref[...], g_ref[...]).astype(BF)
    c0 = 0
    for o_ref in (z_ref, xbc_ref, u5_ref, ga_ref, gb_ref):
        width = o_ref.shape[1]
        _proj_cols(xb, w_ref, o_ref, c0, width)
        c0 += width
    raw = _dot(xb, wdt_ref[...]) + dtb_ref[...]
    dt_ref[...] = jnp.maximum(raw, 0.0) + jnp.log1p(jnp.exp(-jnp.abs(raw)))


def _inproj(x2d, g, w_main, w_dt, dt_bias):
    t = x2d.shape[0]
    tm = TOKEN_TILE
    widths = (D_MODEL, SSD_CONV_DIM, D_MODEL, D_MODEL, D_MODEL)
    row = lambda w: pl.BlockSpec((tm, w), lambda i: (i, 0))
    return pl.pallas_call(
        _inproj_kernel,
        grid=(t // tm,),
        in_specs=[row(D_MODEL), _const_spec(g.shape), _const_spec(w_main.shape), _const_spec(w_dt.shape),
                  _const_spec(dt_bias.shape)],
        out_specs=[row(w) for w in widths] + [row(LANES)],
        out_shape=[jax.ShapeDtypeStruct((t, w), BF) for w in widths] + [jax.ShapeDtypeStruct((t, LANES), F32)],
        compiler_params=_params(1),
        name="inproj",
    )(x2d, g, w_main, w_dt, dt_bias)


def _memkv_kernel(x_ref, g_ref, w_ref, k_ref, v_ref):
    xb = _rms(x_ref[...], g_ref[...]).astype(BF)
    _proj_cols(xb, w_ref, k_ref, 0, D_MODEL)
    _proj_cols(xb, w_ref, v_ref, D_MODEL, D_MODEL)


def _memkv(mem2d, g, w_kv):
    t = mem2d.shape[0]
    tm = TOKEN_TILE
    row = pl.BlockSpec((tm, D_MODEL), lambda i: (i, 0))
    return pl.pallas_call(
        _memkv_kernel,
        grid=(t // tm,),
        in_specs=[row, _const_spec(g.shape), _const_spec(w_kv.shape)],
        out_specs=[row, row],
        out_shape=[jax.ShapeDtypeStruct((t, D_MODEL), F32)] * 2,
        compiler_params=_params(1),
        name="memkv",
    )(mem2d, g, w_kv)


def _ssd_kernel(*refs, lq, has_h0):
    q = SSD_CHUNK
    nseq = q // lq
    if has_h0:
        (xbc_ref, dt_ref, cw_ref, cb_ref, alog_ref, dsk_ref, eexp_ref, eye_ref, shift_ref, conv0_ref, h0_ref,
         y_ref, convo_ref, ho_ref, hbuf, cacc, yacc, yint, xwt, tot_s) = refs
        hin_ref = h0_ref
        hbuf[...] = jnp.zeros_like(hbuf)
        hbuf[:, 0:SSD_CONV - 1, :] = conv0_ref[...]
    else:
        (xbc_ref, dt_ref, cw_ref, cb_ref, alog_ref, dsk_ref, eexp_ref, eye_ref, shift_ref,
         y_ref, convo_ref, ho_ref, hbuf, cacc, yacc, yint, xwt, tot_s) = refs
        hin_ref = ho_ref

        @pl.when(pl.program_id(1) == 0)
        def _():
            ho_ref[...] = jnp.zeros_like(ho_ref)
            hbuf[...] = jnp.zeros_like(hbuf)

    x16 = xbc_ref[...]
    xraw = x16.astype(F32)
    acc = cb_ref[...] + xraw * cw_ref[SSD_CONV - 1:SSD_CONV, :]
    for k in range(SSD_CONV - 1):
        acc = acc + _dot(shift_ref[k], x16) * cw_ref[k:k + 1, :]
    cacc[...] = acc
    for i in range(nseq):
        tail = xraw[(i + 1) * lq - (SSD_CONV - 1):(i + 1) * lq]
        convo_ref[i] = tail
        corr = hbuf[i, 0:8, :] * cw_ref[0:1, :]
        for k in range(1, SSD_CONV - 1):
            corr = corr + hbuf[i, k:k + 8, :] * cw_ref[k:k + 1, :]
        cacc[i * lq:i * lq + 8, :] += corr
        if not has_h0:
            hbuf[i, 0:SSD_CONV - 1, :] = tail
    xc = _silu(cacc[...])
    xs = xc[:, :D_MODEL]
    xs_bf = xs.astype(BF)
    bm_bf = xc[:, D_MODEL:D_MODEL + SSD_GROUPS * SSD_STATE].astype(BF)
    cm = xc[:, D_MODEL + SSD_GROUPS * SSD_STATE:]
    cm_bf = cm.astype(BF)

    dt = dt_ref[...]
    da = dt * (-jnp.exp(alog_ref[...]))
    ri = lax.broadcasted_iota(I32, (q, q), 0)
    ci = lax.broadcasted_iota(I32, (q, q), 1)
    if nseq == 1:
        causal = ci <= ri
    else:
        sh = int(math.log2(lq))
        same = lax.shift_right_logical(ri, sh) == lax.shift_right_logical(ci, sh)
        causal = same & (ci <= ri)
    lmat = jnp.where(causal, 1.0, 0.0).astype(BF)
    cs = _dot_exact(lmat, da)
    if nseq == 1:
        tot = jnp.broadcast_to(cs[q - 1:q, :], (q, LANES))
    else:
        tot = _dot_exact(jnp.where(same, 1.0, 0.0).astype(BF), da)
    tot_s[...] = tot
    eye = eye_ref[...]
    cs_t = _dot_exact(eye, cs, nt=True)
    dt_t = _dot_exact(eye, dt, nt=True)
    eexp = eexp_ref[...]
    ecs_h, ecs_m, _ = _split3(jnp.exp(cs))
    ecs_x = _dot(ecs_h, eexp) + _dot(ecs_m, eexp)
    wend_x = _dot((jnp.exp(tot - cs) * dt).astype(BF), eexp)

    lane = lax.broadcasted_iota(I32, (q, LANES), 1)
    for g in range(SSD_GROUPS):
        cbg = _dot_nt(cm_bf[:, g * SSD_STATE:(g + 1) * SSD_STATE], bm_bf[:, g * SSD_STATE:(g + 1) * SSD_STATE])
        for jp in range(HEADS_PER_GROUP // 2):
            j = g * (HEADS_PER_GROUP // 2) + jp
            ms = []
            for h in (2 * j, 2 * j + 1):
                col = jnp.broadcast_to(cs[:, h:h + 1], (q, q))
                row = jnp.broadcast_to(cs_t[h:h + 1, :], (q, q))
                dtr = jnp.broadcast_to(dt_t[h:h + 1, :], (q, q))
                ms.append((jnp.where(causal, jnp.exp(col - row), 0.0) * cbg * dtr).astype(BF))
            mp = jnp.concatenate(ms, axis=1)
            xp = xs_bf[:, j * LANES:(j + 1) * LANES]
            zero = jnp.zeros_like(xp)
            x2 = jnp.concatenate([jnp.where(lane < SSD_HEADDIM, xp, zero),
                                  jnp.where(lane >= SSD_HEADDIM, xp, zero)], axis=0)
            yacc[:, j * LANES:(j + 1) * LANES] = _dot(mp, x2)

    gw = HEADS_PER_GROUP * SSD_HEADDIM
    xw = xs * wend_x
    for g in range(SSD_GROUPS):
        xwt[g * gw:(g + 1) * gw, :] = xw[:, g * gw:(g + 1) * gw].T.astype(BF)

    if nseq == 1:
        for g in range(SSD_GROUPS):
            hin_g = hin_ref[0, g * HEADS_PER_GROUP:(g + 1) * HEADS_PER_GROUP].reshape(gw, SSD_STATE)
            yint[:, g * gw:(g + 1) * gw] = _dot_nt(cm_bf[:, g * SSD_STATE:(g + 1) * SSD_STATE], hin_g.astype(BF))
            s_new = _dot(xwt[g * gw:(g + 1) * gw, :], bm_bf[:, g * SSD_STATE:(g + 1) * SSD_STATE])
            for hl in range(HEADS_PER_GROUP):
                h = g * HEADS_PER_GROUP + hl
                dec = jnp.exp(jnp.broadcast_to(tot[0:1, h:h + 1], (SSD_HEADDIM, SSD_STATE)))
                ho_ref[0, h] = dec * hin_ref[0, h] + s_new[hl * SSD_HEADDIM:(hl + 1) * SSD_HEADDIM]
    else:
        yint[...] = jnp.zeros_like(yint)
        sh = int(math.log2(lq))
        rowseq = lax.shift_right_logical(lax.broadcasted_iota(I32, (q, SSD_STATE), 0), sh)
        colseq = lax.shift_right_logical(lax.broadcasted_iota(I32, (gw, q), 1), sh)

        def seq_body(i, carry):
            trow = tot_s[pl.ds(i * lq, 1), :]
            for g in range(SSD_GROUPS):
                hin_g = hin_ref[i, g * HEADS_PER_GROUP:(g + 1) * HEADS_PER_GROUP].reshape(gw, SSD_STATE)
                cmg = cm[:, g * SSD_STATE:(g + 1) * SSD_STATE]
                lhs = jnp.where(rowseq == i, cmg, 0.0).astype(BF)
                yint[:, g * gw:(g + 1) * gw] += _dot_nt(lhs, hin_g.astype(BF))
                xg = xwt[g * gw:(g + 1) * gw, :]
                xi = jnp.where(colseq == i, xg, jnp.zeros_like(xg))
                s_new = _dot(xi, bm_bf[:, g * SSD_STATE:(g + 1) * SSD_STATE])
                for hl in range(HEADS_PER_GROUP):
                    h = g * HEADS_PER_GROUP + hl
                    dec = jnp.exp(jnp.broadcast_to(trow[:, h:h + 1], (SSD_HEADDIM, SSD_STATE)))
                    ho_ref[i, h] = dec * hin_ref[i, h] + s_new[hl * SSD_HEADDIM:(hl + 1) * SSD_HEADDIM]
            return carry

        lax.fori_loop(0, nseq, seq_body, 0)

    y_ref[...] = (yacc[...] + yint[...] * ecs_x + dsk_ref[...] * xs).astype(y_ref.dtype)


def _ssd(xbc, dt, conv_w, conv_b, a_log, dskip, eexp, eye, conv0, h0, *, batch, seqlen):
    q = SSD_CHUNK
    t = xbc.shape[0]
    has_h0 = h0 is not None
    if has_h0:
        lq = seqlen
        nseq = q // lq
        grid = (t // q,)
        rmap = lambda i: (i, 0)
        smap3 = lambda i: (i, 0, 0)
        smap4 = lambda i: (i, 0, 0, 0)
    else:
        lq = q
        nseq = 1
        nc = seqlen // q
        grid = (batch, nc)
        rmap = lambda b, c: (b * nc + c, 0)
        smap3 = lambda b, c: (b, 0, 0)
        smap4 = lambda b, c: (b, 0, 0, 0)
    shifts = np.zeros((SSD_CONV - 1, q, q), np.float32)
    for k in range(SSD_CONV - 1):
        for r in range(q):
            if r % lq + k - (SSD_CONV - 1) >= 0:
                shifts[k, r, r + k - (SSD_CONV - 1)] = 1.0
    consts = [conv_w, conv_b, a_log, dskip, eexp, eye, jnp.asarray(shifts).astype(BF)]
    in_specs = [pl.BlockSpec((q, SSD_CONV_DIM), rmap), pl.BlockSpec((q, LANES), rmap)]
    in_specs += [_const_spec(c.shape) for c in consts]
    args = [xbc, dt] + consts
    if has_h0:
        in_specs += [pl.BlockSpec((nseq, SSD_CONV - 1, SSD_CONV_DIM), smap3),
                     pl.BlockSpec((nseq, SSD_HEADS, SSD_HEADDIM, SSD_STATE), smap4)]
        args += [conv0, h0]
    return pl.pallas_call(
        functools.partial(_ssd_kernel, lq=lq, has_h0=has_h0),
        grid=grid,
        in_specs=in_specs,
        out_specs=[pl.BlockSpec((q, D_MODEL), rmap),
                   pl.BlockSpec((nseq, SSD_CONV - 1, SSD_CONV_DIM), smap3),
                   pl.BlockSpec((nseq, SSD_HEADS, SSD_HEADDIM, SSD_STATE), smap4)],
        out_shape=[jax.ShapeDtypeStruct((t, D_MODEL), BF),
                   jax.ShapeDtypeStruct((batch, SSD_CONV - 1, SSD_CONV_DIM), F32),
                   jax.ShapeDtypeStruct((batch, SSD_HEADS, SSD_HEADDIM, SSD_STATE), F32)],
        scratch_shapes=[pltpu.VMEM((nseq, 16, SSD_CONV_DIM), F32),
                        pltpu.VMEM((q, SSD_CONV_DIM), F32),
                        pltpu.VMEM((q, D_MODEL), F32),
                        pltpu.VMEM((q, D_MODEL), F32),
                        pltpu.VMEM((D_MODEL, q), BF),
                        pltpu.VMEM((q, LANES), F32)],
        compiler_params=_params(len(grid)),
        name="ssd_sample" if has_h0 else "ssd_prompt",
    )(*args)


S5_SEQ_TILE = 8
S5_LANE_CHUNK = 1024


def _s5_kernel(*refs, tl, has_h0):
    bt = S5_SEQ_TILE
    m = bt * tl
    if has_h0:
        (u_ref, perm_ref, permt_ref, bblk_ref, cblk_ref, lre_ref, lim_ref, d_ref, h0re_ref, h0im_ref,
         y_ref, ore_ref, oim_ref, bre, bim, sre, sim) = refs
    else:
        (u_ref, perm_ref, permt_ref, bblk_ref, cblk_ref, lre_ref, lim_ref, d_ref,
         y_ref, ore_ref, oim_ref, bre, bim, sre, sim) = refs

    @pl.when(pl.program_id(1) == 0)
    def _():
        if has_h0:
            sre[...] = h0re_ref[...]
            sim[...] = h0im_ref[...]
        else:
            sre[...] = jnp.zeros_like(sre)
            sim[...] = jnp.zeros_like(sim)

    u = u_ref[...].reshape(m, D_MODEL)
    u_tm = _dot(perm_ref[...], u).astype(BF)
    half = S5_FLAT // S5_SLABS
    for j in range(S5_SLABS):
        r = _dot(u_tm[:, j * LANES:(j + 1) * LANES], bblk_ref[j])
        bre[:, j * half:(j + 1) * half] = r[:, :half]
        bim[:, j * half:(j + 1) * half] = r[:, half:]

    for jc in range(S5_FLAT // S5_LANE_CHUNK):
        sl = slice(jc * S5_LANE_CHUNK, (jc + 1) * S5_LANE_CHUNK)
        lr = lre_ref[:, sl]
        li = lim_ref[:, sl]

        def step(l, carry, sl=sl, lr=lr, li=li):
            sr, si = carry
            rows = pl.ds(pl.multiple_of(l * bt, bt), bt)
            nr = lr * sr - li * si + bre[rows, sl]
            ni = lr * si + li * sr + bim[rows, sl]
            bre[rows, sl] = nr
            bim[rows, sl] = ni
            return nr, ni

        sr, si = lax.fori_loop(0, tl, step, (sre[:, sl], sim[:, sl]), unroll=2)
        sre[:, sl] = sr
        sim[:, sl] = si

    ys = []
    for j in range(S5_SLABS):
        st = jnp.concatenate([bre[:, j * half:(j + 1) * half], bim[:, j * half:(j + 1) * half]], axis=1)
        ys.append(_dot(st.astype(BF), cblk_ref[j]))
    y_tm = jnp.concatenate(ys, axis=1).astype(BF)
    y_bm = _dot(permt_ref[...], y_tm) + d_ref[...] * u.astype(F32)
    y_ref[...] = y_bm.astype(y_ref.dtype).reshape(y_ref.shape)
    ore_ref[...] = sre[...]
    oim_ref[...] = sim[...]


def _s5(u, perm, permt, bblk, cblk, lre, lim, dvec, h0re, h0im, *, batch, seqlen, tl):
    bt = S5_SEQ_TILE
    m = bt * tl
    has_h0 = h0re is not None
    nb = batch // bt
    nt = seqlen // tl
    if u.ndim == 3:
        u_spec = pl.BlockSpec((bt, tl, D_MODEL), lambda b, t: (b, t, 0))
    else:
        u_spec = pl.BlockSpec((m, D_MODEL), lambda b, t: (b * nt + t, 0))
    st_spec = pl.BlockSpec((bt, S5_FLAT), lambda b, t: (b, 0))
    consts = [perm, permt, bblk, cblk, lre, lim, dvec]
    in_specs = [u_spec] + [_const_spec(c.shape) for c in consts]
    args = [u] + consts
    if has_h0:
        in_specs += [st_spec, st_spec]
        args += [h0re, h0im]
    return pl.pallas_call(
        functools.partial(_s5_kernel, tl=tl, has_h0=has_h0),
        grid=(nb, nt),
        in_specs=in_specs,
        out_specs=[u_spec, st_spec, st_spec],
        out_shape=[jax.ShapeDtypeStruct(u.shape, BF),
                   jax.ShapeDtypeStruct((batch, S5_FLAT), F32),
                   jax.ShapeDtypeStruct((batch, S5_FLAT), F32)],
        scratch_shapes=[pltpu.VMEM((m, S5_FLAT), F32), pltpu.VMEM((m, S5_FLAT), F32),
                        pltpu.VMEM((bt, S5_FLAT), F32), pltpu.VMEM((bt, S5_FLAT), F32)],
        compiler_params=_params(2),
        name="s5_sample" if has_h0 else "s5_prompt",
    )(*args)


def _mixout_kernel(y_ref, z_ref, y5_ref, ga_ref, gb_ref, x_ref, ng_ref, wa_ref, wglu_ref, bglu_ref, wmix_ref, o_ref):
    y = y_ref[...].astype(F32) * _silu(z_ref[...].astype(F32))
    yn = _rms(y, ng_ref[...]).astype(BF)
    branch_a = _dot(yn, wa_ref[...])
    glu = _dot(y5_ref[...], wglu_ref[...]) + bglu_ref[...]
    branch_b = glu[:, :D_MODEL] * _sigmoid(glu[:, D_MODEL:])
    merged = _sigmoid(ga_ref[...].astype(F32)) * branch_a + _sigmoid(gb_ref[...].astype(F32)) * branch_b
    o_ref[...] = x_ref[...] + _dot(merged.astype(BF), wmix_ref[...])


def _mixout(y, z, y5, ga, gb, x2d, ng, wa, wglu, bglu, wmix):
    t = x2d.shape[0]
    tm = TOKEN_TILE
    row = pl.BlockSpec((tm, D_MODEL), lambda i: (i, 0))
    consts = [ng, wa, wglu, bglu, wmix]
    return pl.pallas_call(
        _mixout_kernel,
        grid=(t // tm,),
        in_specs=[row] * 6 + [_const_spec(c.shape) for c in consts],
        out_specs=row,
        out_shape=jax.ShapeDtypeStruct((t, D_MODEL), F32),
        compiler_params=_params(1),
        name="mixout",
    )(y, z, y5, ga, gb, x2d, *consts)


def _attn_kernel(h_ref, g_ref, wq_ref, wo_ref, k_ref, v_ref, o_ref, obuf, *, nkv, rows_per_seq):
    h = h_ref[...]
    r = h.shape[0]
    q32 = _dot(_rms(h, g_ref[...]).astype(BF), wq_ref[...])
    scale = MEM_HEAD_DIM ** -0.5
    if len(k_ref.shape) == 4:
        rq = rows_per_seq
        nrow = MEM_HEADS * rq
        row_head = lax.shift_right_logical(lax.broadcasted_iota(I32, (nrow, MEM_LEN * MEM_HEADS), 0), int(math.log2(rq)))
        col_head = lax.broadcasted_iota(I32, (nrow, MEM_LEN * MEM_HEADS), 1) & (MEM_HEADS - 1)
        visible = row_head == col_head
        for i in range(nkv):
            qs = jnp.concatenate([q32[i * rq:(i + 1) * rq, hd * MEM_HEAD_DIM:(hd + 1) * MEM_HEAD_DIM]
                                  for hd in range(MEM_HEADS)], axis=0).astype(BF)
            kf = k_ref[i].reshape(MEM_LEN * MEM_HEADS, MEM_HEAD_DIM).astype(BF)
            vf = v_ref[i].reshape(MEM_LEN * MEM_HEADS, MEM_HEAD_DIM).astype(BF)
            s = jnp.where(visible, _dot_nt(qs, kf) * scale, NEG_BIG)
            p = jnp.exp(s - jnp.max(s, axis=-1, keepdims=True))
            p = p / jnp.sum(p, axis=-1, keepdims=True)
            o = _dot(p.astype(BF), vf)
            for hd in range(MEM_HEADS):
                obuf[i * rq:(i + 1) * rq, hd * MEM_HEAD_DIM:(hd + 1) * MEM_HEAD_DIM] = o[hd * rq:(hd + 1) * rq]
    else:
        qv = q32.astype(BF)
        for hd in range(MEM_HEADS):
            sl = slice(hd * MEM_HEAD_DIM, (hd + 1) * MEM_HEAD_DIM)
            s = _dot_nt(qv[:, sl], k_ref[0, :, sl].astype(BF)) * scale
            p = jnp.exp(s - jnp.max(s, axis=-1, keepdims=True))
            p = p / jnp.sum(p, axis=-1, keepdims=True)
            obuf[:, sl] = _dot(p.astype(BF), v_ref[0, :, sl].astype(BF))
    o_ref[...] = h + _dot(obuf[...].astype(BF), wo_ref[...])


def _attn(h2d, g, wq, wo, k3, v3, *, rows, nkv, rows_per_seq):
    t = h2d.shape[0]
    nsteps = t // rows
    steps_per_kv = nsteps // (k3.shape[0] // nkv)
    row = pl.BlockSpec((rows, D_MODEL), lambda i: (i, 0))
    if k3.ndim == 4:
        kv = pl.BlockSpec((nkv, MEM_LEN, MEM_HEADS, MEM_HEAD_DIM), lambda i: (i // steps_per_kv, 0, 0, 0))
    else:
        kv = pl.BlockSpec((nkv, MEM_LEN, D_MODEL), lambda i: (i // steps_per_kv, 0, 0))
    consts = [g, wq, wo]
    return pl.pallas_call(
        functools.partial(_attn_kernel, nkv=nkv, rows_per_seq=rows_per_seq),
        grid=(nsteps,),
        in_specs=[row] + [_const_spec(c.shape) for c in consts] + [kv, kv],
        out_specs=row,
        out_shape=jax.ShapeDtypeStruct((t, D_MODEL), F32),
        scratch_shapes=[pltpu.VMEM((rows, D_MODEL), F32)],
        compiler_params=_params(1),
        name="attn",
    )(h2d, *consts, k3, v3)


ROUTER_LANE0 = MOE_GROUPS
ROUTE_ROWS = 8


def _router_kernel(hp_ref, hs_ref, g_ref, wh_ref, wm_ref, wl_ref, b_ref, tri_ref, eye_ref, rows_ref, wt_ref, cnt_ref,
                   carry, *, np_steps):
    i = pl.program_id(0)

    @pl.when(i == 0)
    def _():
        carry[...] = jnp.zeros_like(carry)

    h = jnp.where(i < np_steps, hp_ref[...], hs_ref[...])
    xn = _rms(h, g_ref[...])
    xh, xm, xl = _split3(xn)
    wh, wm, wl = wh_ref[...], wm_ref[...], wl_ref[...]
    logits = (_dot(xh, wh) + _dot(xh, wm) + _dot(xm, wh) + _dot(xh, wl) + _dot(xm, wm) + _dot(xl, wh)) + b_ref[...]
    lane = lax.broadcasted_iota(I32, logits.shape, 1)
    is_g = lane < MOE_GROUPS
    lg = jnp.where(is_g, logits, NEG_BIG)
    mg = jnp.max(lg, axis=-1, keepdims=True)
    gidx = jnp.min(jnp.where(lg == mg, lane, LANES), axis=-1, keepdims=True)
    g_w = 1.0 / jnp.sum(jnp.where(is_g, jnp.exp(lg - mg), 0.0), axis=-1, keepdims=True)
    e_lane = lane - ROUTER_LANE0
    in_grp = ((e_lane >= 0) & (e_lane < MOE_EXPERTS)
              & (lax.shift_right_logical(jnp.maximum(e_lane, 0), 3) == gidx))
    le = jnp.where(in_grp, logits, NEG_BIG)
    m1 = jnp.max(le, axis=-1, keepdims=True)
    i1 = jnp.min(jnp.where(le == m1, lane, LANES), axis=-1, keepdims=True)
    le2 = jnp.where(lane == i1, NEG_BIG, le)
    m2 = jnp.max(le2, axis=-1, keepdims=True)
    i2 = jnp.min(jnp.where(le2 == m2, lane, LANES), axis=-1, keepdims=True)
    ratio = jnp.exp(m2 - m1)
    w1 = g_w / (1.0 + ratio)
    w2 = g_w * ratio / (1.0 + ratio)
    wt_ref[...] = jnp.where(lane == 0, w1, jnp.where(lane == 1, w2, 0.0))

    oh1 = jnp.where(lane == i1, 1.0, 0.0)
    oh2 = jnp.where(lane == i2, 1.0, 0.0)
    tri = tri_ref[...]
    c0 = carry[...]
    tot1 = jnp.sum(oh1, axis=0, keepdims=True)
    r1 = jnp.sum(oh1 * (_dot(tri, oh1.astype(BF)) + c0), axis=-1, keepdims=True)
    r2 = jnp.sum(oh2 * (_dot(tri, oh2.astype(BF)) + (c0 + tot1)), axis=-1, keepdims=True)
    c1 = c0 + tot1 + jnp.sum(oh2, axis=0, keepdims=True)
    carry[...] = c1
    cols = jnp.where(lane == 0, (i1 - ROUTER_LANE0).astype(F32),
                     jnp.where(lane == 1, (i2 - ROUTER_LANE0).astype(F32),
                               jnp.where(lane == 2, r1, jnp.where(lane == 3, r2, 0.0))))
    rows_ref[0] = _dot_exact(eye_ref[...], cols, nt=True).astype(I32)
    cnt_ref[...] = jnp.broadcast_to(c1, cnt_ref.shape).astype(I32)


def _router(hp, hs, g, wh, wm, wl, b):
    tm = TOKEN_TILE
    np_steps = hp.shape[0] // tm
    ns_steps = hs.shape[0] // tm
    t = hp.shape[0] + hs.shape[0]
    tri = jnp.asarray(np.tril(np.ones((tm, tm), np.float32), -1)).astype(BF)
    eye8 = jnp.asarray(np.eye(ROUTE_ROWS, LANES, dtype=np.float32)).astype(BF)
    consts = [g, wh, wm, wl, b, tri, eye8]
    return pl.pallas_call(
        functools.partial(_router_kernel, np_steps=np_steps),
        grid=(np_steps + ns_steps,),
        in_specs=[pl.BlockSpec((tm, D_MODEL), lambda i: (jnp.minimum(i, np_steps - 1), 0)),
                  pl.BlockSpec((tm, D_MODEL), lambda i: (jnp.maximum(i - np_steps, 0), 0))]
                 + [_const_spec(c.shape) for c in consts],
        out_specs=[pl.BlockSpec((1, ROUTE_ROWS, tm), lambda i: (i, 0, 0)),
                   pl.BlockSpec((tm, LANES), lambda i: (i, 0)), _const_spec((8, LANES))],
        out_shape=[jax.ShapeDtypeStruct((t // tm, ROUTE_ROWS, tm), I32), jax.ShapeDtypeStruct((t, LANES), F32),
                   jax.ShapeDtypeStruct((8, LANES), I32)],
        scratch_shapes=[pltpu.VMEM((1, LANES), F32)],
        compiler_params=_params(1),
        name="router",
    )(hp, hs, *consts)


SLABS = D_MODEL // LANES


def _token_rows(t):
    return pl.ds(pl.multiple_of(t * SLABS, SLABS), SLABS)


def _to_token_major(ref, x, base=0):
    tm = x.shape[0]
    for s in range(SLABS):
        ref[pl.ds(base + s, tm, stride=SLABS), :] = x[:, s * LANES:(s + 1) * LANES]


def _from_token_major(ref, tm, base=0):
    return jnp.concatenate([ref[pl.ds(base + s, tm, stride=SLABS), :] for s in range(SLABS)], axis=1)


def _sorted_row(ps_ref, rt_ref, k, t):
    return ps_ref[rt_ref[0, k, t]] + rt_ref[0, 2 + k, t]


def _scatter_kernel(ps_ref, ve_ref, pe_ref, hp_ref, hs_ref, g_ref, rt_ref, xs_ref, xbuf, zbuf, sem, zsem, *, np_steps):
    i = pl.program_id(0)
    tm = TOKEN_TILE

    tile_rows = MOE_ROW_TILE * SLABS
    n_tiles = xs_ref.shape[0] // tile_rows

    def pad_copy(r):
        return pltpu.make_async_copy(zbuf.at[pl.ds(0, SLABS)], xs_ref.at[_token_rows(r)], zsem)

    def tile_copy(j):
        return pltpu.make_async_copy(zbuf, xs_ref.at[pl.ds(pl.multiple_of(j * tile_rows, tile_rows), tile_rows)], zsem)

    @pl.when(i == 0)
    def _():
        zbuf[...] = jnp.zeros_like(zbuf)
        first_unused = pe_ref[MOE_EXPERTS - 1] // MOE_ROW_TILE

        def tile_start(j, c):
            tile_copy(j).start()
            return c

        def tile_wait(j, c):
            tile_copy(0).wait()
            return c

        lax.fori_loop(first_unused, n_tiles, tile_start, 0)
        lax.fori_loop(first_unused, n_tiles, tile_wait, 0)

        def per_expert(e, n):
            lo = ve_ref[e]
            hi = pe_ref[e]

            def body(r, c):
                pad_copy(r).start()
                return c

            lax.fori_loop(lo, hi, body, 0)
            return n + (hi - lo)

        n_pad = lax.fori_loop(0, MOE_EXPERTS, per_expert, 0)

        def pad_wait(r, c):
            pad_copy(0).wait()
            return c

        lax.fori_loop(0, n_pad, pad_wait, 0)

    h = jnp.where(i < np_steps, hp_ref[...], hs_ref[...])
    _to_token_major(xbuf, _rms(h, g_ref[...]))

    def row_copy(t, dst_row):
        return pltpu.make_async_copy(xbuf.at[_token_rows(t)], xs_ref.at[_token_rows(dst_row)], sem)

    def issue(t, c):
        row_copy(t, _sorted_row(ps_ref, rt_ref, 0, t)).start()
        row_copy(t, _sorted_row(ps_ref, rt_ref, 1, t)).start(priority=1)
        return c

    def wait(t, c):
        row_copy(0, 0).wait()
        return c

    lax.fori_loop(0, tm, issue, 0, unroll=8)
    lax.fori_loop(0, 2 * tm, wait, 0, unroll=16)


def _scatter(hp, hs, g, route, pstart, valid_end, pend, n_rows):
    tm = TOKEN_TILE
    np_steps = hp.shape[0] // tm
    ns_steps = hs.shape[0] // tm
    grid_spec = pltpu.PrefetchScalarGridSpec(
        num_scalar_prefetch=3,
        grid=(np_steps + ns_steps,),
        in_specs=[pl.BlockSpec((tm, D_MODEL), lambda i, *_: (jnp.minimum(i, np_steps - 1), 0)),
                  pl.BlockSpec((tm, D_MODEL), lambda i, *_: (jnp.maximum(i - np_steps, 0), 0)),
                  pl.BlockSpec(g.shape, lambda i, *_: (0, 0)),
                  pl.BlockSpec((1, ROUTE_ROWS, tm), lambda i, *_: (i, 0, 0), memory_space=pltpu.SMEM)],
        out_specs=pl.BlockSpec(memory_space=pl.ANY),
        scratch_shapes=[pltpu.VMEM((tm * SLABS, LANES), F32), pltpu.VMEM((MOE_ROW_TILE * SLABS, LANES), F32),
                        pltpu.SemaphoreType.DMA(()), pltpu.SemaphoreType.DMA(())],
    )
    return pl.pallas_call(
        functools.partial(_scatter_kernel, np_steps=np_steps),
        grid_spec=grid_spec,
        out_shape=jax.ShapeDtypeStruct((n_rows * SLABS, LANES), F32),
        compiler_params=_params(1),
        name="scatter",
    )(pstart, valid_end, pend, hp, hs, g, route)


def _expert_kernel(te_ref, nu_ref, x_ref, wg_ref, wu_ref, wd_ref, y_ref, wg16, wu16, wd16):
    i = pl.program_id(0)
    tm = MOE_ROW_TILE

    @pl.when(i < nu_ref[0])
    def _():
        @pl.when((i == 0) | (te_ref[i] != te_ref[jnp.maximum(i - 1, 0)]))
        def _():
            wg16[...] = wg_ref[0].astype(BF)
            wu16[...] = wu_ref[0].astype(BF)
            wd16[...] = wd_ref[0].astype(BF)

        x = _from_token_major(x_ref, tm).astype(BF)
        gate = _dot(x, wg16[...])
        up = _dot(x, wu16[...])
        _to_token_major(y_ref, _dot((_silu(gate) * up).astype(BF), wd16[...]))

    @pl.when(i >= nu_ref[0])
    def _():
        y_ref[...] = jnp.zeros_like(y_ref)


def _experts(x_sorted, tile_expert, n_used, wg, wu, wd):
    tm = MOE_ROW_TILE
    n_tiles = x_sorted.shape[0] // (tm * SLABS)
    used = lambda i, nu: jnp.minimum(i, nu[0] - 1)
    grid_spec = pltpu.PrefetchScalarGridSpec(
        num_scalar_prefetch=2,
        grid=(n_tiles,),
        in_specs=[pl.BlockSpec((tm * SLABS, LANES), lambda i, te, nu: (used(i, nu), 0)),
                  pl.BlockSpec((1, D_MODEL, MOE_D_FF), lambda i, te, nu: (te[used(i, nu)], 0, 0)),
                  pl.BlockSpec((1, D_MODEL, MOE_D_FF), lambda i, te, nu: (te[used(i, nu)], 0, 0)),
                  pl.BlockSpec((1, MOE_D_FF, D_MODEL), lambda i, te, nu: (te[used(i, nu)], 0, 0))],
        out_specs=pl.BlockSpec((tm * SLABS, LANES), lambda i, te, nu: (i, 0)),
        scratch_shapes=[pltpu.VMEM((D_MODEL, MOE_D_FF), BF), pltpu.VMEM((D_MODEL, MOE_D_FF), BF),
                        pltpu.VMEM((MOE_D_FF, D_MODEL), BF)],
    )
    return pl.pallas_call(
        _expert_kernel,
        grid_spec=grid_spec,
        out_shape=jax.ShapeDtypeStruct(x_sorted.shape, F32),
        compiler_params=_params(1),
        name="experts",
    )(tile_expert, n_used, x_sorted, wg, wu, wd)


def _combine_kernel(ps_ref, h_ref, wt_ref, g_ref, pcur_ref, pnext_ref, ys_ref, o_ref, ybuf, sem, *, n):
    i = pl.program_id(0)
    tm = TOKEN_TILE

    def row_copy(src_row, j, slot):
        return pltpu.make_async_copy(ys_ref.at[_token_rows(src_row)], ybuf.at[slot, _token_rows(j)], sem.at[slot])

    def issue(rt_ref, slot):
        def body(t, c):
            row_copy(_sorted_row(ps_ref, rt_ref, 0, t), t, slot).start()
            row_copy(_sorted_row(ps_ref, rt_ref, 1, t), tm + t, slot).start(priority=1)
            return c

        lax.fori_loop(0, tm, body, 0, unroll=8)

    def finish(slot):
        def wait(j, c):
            row_copy(0, 0, slot).wait()
            return c

        lax.fori_loop(0, 2 * tm, wait, 0, unroll=16)
        wt = wt_ref[...]
        y1 = _from_token_major(ybuf.at[slot], tm)
        y2 = _from_token_major(ybuf.at[slot], tm, base=tm * SLABS)
        o_ref[...] = _rms(h_ref[...] + wt[:, 0:1] * y1 + wt[:, 1:2] * y2, g_ref[...])

    @pl.when(i == 0)
    def _():
        issue(pcur_ref, 0)

    for slot in (0, 1):
        @pl.when(lax.rem(i, 2) == slot)
        def _(slot=slot):
            @pl.when(i + 1 < n)
            def _():
                issue(pnext_ref, 1 - slot)

            finish(slot)


def _combine(h2d, wt, g, route, pstart, y_sorted, *, tile_offset):
    t = h2d.shape[0]
    tm = TOKEN_TILE
    nsteps = t // tm
    route_spec = lambda f: pl.BlockSpec((1, ROUTE_ROWS, tm), f, memory_space=pltpu.SMEM)
    grid_spec = pltpu.PrefetchScalarGridSpec(
        num_scalar_prefetch=1,
        grid=(nsteps,),
        in_specs=[pl.BlockSpec((tm, D_MODEL), lambda i, ps: (i, 0)),
                  pl.BlockSpec((tm, LANES), lambda i, ps: (i + tile_offset, 0)),
                  pl.BlockSpec(g.shape, lambda i, ps: (0, 0)),
                  route_spec(lambda i, ps: (i + tile_offset, 0, 0)),
                  route_spec(lambda i, ps: (jnp.minimum(i + 1, nsteps - 1) + tile_offset, 0, 0)),
                  pl.BlockSpec(memory_space=pl.ANY)],
        out_specs=pl.BlockSpec((tm, D_MODEL), lambda i, ps: (i, 0)),
        scratch_shapes=[pltpu.VMEM((2, 2 * tm * SLABS, LANES), F32), pltpu.SemaphoreType.DMA((2,))],
    )
    return pl.pallas_call(
        functools.partial(_combine_kernel, n=nsteps),
        grid_spec=grid_spec,
        out_shape=jax.ShapeDtypeStruct((t, D_MODEL), F32),
        compiler_params=_params(1),
        name="combine",
    )(pstart, h2d, wt, g, route, route, y_sorted)


def _moe_plan(cnt, t):
    rt = MOE_ROW_TILE
    n_tiles = (2 * t + MOE_EXPERTS * (rt - 1) + rt - 1) // rt
    counts = cnt[0, ROUTER_LANE0:ROUTER_LANE0 + MOE_EXPERTS]
    padded = ((counts + rt - 1) // rt) * rt
    pend = jnp.cumsum(padded).astype(I32)
    pstart = pend - padded
    valid_end = (pstart + counts).astype(I32)
    tile_start = jnp.arange(n_tiles, dtype=I32) * rt
    tile_expert = jnp.sum((tile_start[:, None] >= pend[None, :]).astype(I32), axis=1)
    tile_expert = jnp.minimum(tile_expert, MOE_EXPERTS - 1).astype(I32)
    n_used = (pend[-1] // rt).astype(I32).reshape(1)
    return pstart, valid_end, pend, tile_expert, n_used, n_tiles * rt


U32 = jnp.uint32
HALF = D_MODEL // 2


def _pack_rows(x):
    xr = x.astype(BF).astype(F32)
    lo = lax.shift_right_logical(pltpu.bitcast(xr[:, :HALF], U32), jnp.uint32(16))
    hi = pltpu.bitcast(xr[:, HALF:], U32) & jnp.uint32(0xFFFF0000)
    return hi | lo


def _unpack_rows(w):
    lo = pltpu.bitcast(lax.shift_left(w, jnp.uint32(16)), F32).astype(BF)
    hi = pltpu.bitcast(w & jnp.uint32(0xFFFF0000), F32).astype(BF)
    return lo, hi


GRANULE = 8
LOCAL_USED = 2 * TOKEN_TILE + MOE_EXPERTS * (GRANULE - 1)
LOCAL_ROWS = 1280
SPARE_GRANULES = (LOCAL_ROWS - LOCAL_USED) // GRANULE
TILE_GRANULES = MOE_ROW_TILE // GRANULE


def _route_kernel(hp_ref, hs_ref, g_ref, wcat_ref, wh_ref, b_ref, tri_ref, eye_ref, upper_ref,
                  xl_ref, idx_ref, wt_ref, seg_ref, *, np_steps):
    i = pl.program_id(0)
    h = jnp.where(i < np_steps, hp_ref[...], hs_ref[...])
    xn = _rms(h, g_ref[...])
    tm = xn.shape[0]
    xh = xn.astype(BF)
    xm = (xn - xh.astype(F32)).astype(BF)
    both = _dot(xh, wcat_ref[...])
    logits = both[:, :LANES] + both[:, LANES:] + _dot(xm, wh_ref[...]) + b_ref[...]
    lane = lax.broadcasted_iota(I32, logits.shape, 1)
    lane_f = lane.astype(F32)
    no_lane = float(LANES)
    is_g = lane < MOE_GROUPS
    lg = jnp.where(is_g, logits, NEG_BIG)
    mg = jnp.max(lg, axis=-1, keepdims=True)
    gidx = jnp.min(jnp.where(lg == mg, lane_f, no_lane), axis=-1, keepdims=True).astype(I32)
    g_w = 1.0 / jnp.sum(jnp.where(is_g, jnp.exp(lg - mg), 0.0), axis=-1, keepdims=True)
    e_lane = lane - ROUTER_LANE0
    in_grp = ((e_lane >= 0) & (e_lane < MOE_EXPERTS)
              & (lax.shift_right_logical(jnp.maximum(e_lane, 0), 3) == gidx))
    le = jnp.where(in_grp, logits, NEG_BIG)
    m1 = jnp.max(le, axis=-1, keepdims=True)
    i1 = jnp.min(jnp.where(le == m1, lane_f, no_lane), axis=-1, keepdims=True)
    le2 = jnp.where(lane_f == i1, NEG_BIG, le)
    m2 = jnp.max(le2, axis=-1, keepdims=True)
    i2 = jnp.min(jnp.where(le2 == m2, lane_f, no_lane), axis=-1, keepdims=True)
    ratio = jnp.exp(m2 - m1)
    w1 = g_w / (1.0 + ratio)
    w2 = g_w * ratio / (1.0 + ratio)
    wt_ref[...] = jnp.where(lane == 0, w1, jnp.where(lane == 1, w2, 0.0))

    oh1 = jnp.where(lane_f == i1, 1.0, 0.0)
    oh2 = jnp.where(lane_f == i2, 1.0, 0.0)
    tot1 = jnp.sum(oh1, axis=0, keepdims=True)
    cnt = (tot1 + jnp.sum(oh2, axis=0, keepdims=True)).astype(I32)
    padded = lax.shift_left(lax.shift_right_logical(cnt + (GRANULE - 1), 3), 3)
    pad8 = jnp.broadcast_to(padded.astype(F32), (8, LANES)).astype(BF)
    lstart = _dot(pad8, upper_ref[...])[0:1, :]
    tri = tri_ref[...]
    l1 = jnp.sum(oh1 * (_dot(tri, oh1.astype(BF)) + lstart), axis=-1, keepdims=True)
    l2 = jnp.sum(oh2 * (_dot(tri, oh2.astype(BF)) + (lstart + tot1)), axis=-1, keepdims=True)
    idx_ref[...] = jnp.where(lane == 0, l1.astype(I32), jnp.where(lane == 1, l2.astype(I32), 0))
    sub = lax.broadcasted_iota(I32, (8, LANES), 0)
    seg_ref[0] = jnp.where(sub == 0, jnp.broadcast_to(cnt, (8, LANES)),
                           jnp.where(sub == 1, jnp.broadcast_to(lstart.astype(I32), (8, LANES)), 0))

    cols = jnp.where(lane == 0, l1, jnp.where(lane == 1, l2, 0.0))
    rows = _dot_exact(eye_ref[...], cols, nt=True)
    r_id = lax.broadcasted_iota(I32, (LOCAL_ROWS, tm), 0).astype(F32)
    perm = jnp.where(r_id == rows[0:1, :], 1.0, 0.0) + jnp.where(r_id == rows[1:2, :], 1.0, 0.0)
    xl_ref[...] = _pack_rows(_dot(perm.astype(BF), xh))


def _route(hp, hs, g, wcat, wh, b):
    tm = TOKEN_TILE
    np_steps = hp.shape[0] // tm
    ns_steps = hs.shape[0] // tm
    nt = np_steps + ns_steps
    tri = jnp.asarray(np.tril(np.ones((tm, tm), np.float32), -1)).astype(BF)
    eye8 = jnp.asarray(np.eye(8, LANES, dtype=np.float32)).astype(BF)
    upper = jnp.asarray(np.triu(np.ones((LANES, LANES), np.float32), 1)).astype(BF)
    consts = [g, wcat, wh, b, tri, eye8, upper]
    return pl.pallas_call(
        functools.partial(_route_kernel, np_steps=np_steps),
        grid=(nt,),
        in_specs=[pl.BlockSpec((tm, D_MODEL), lambda i: (jnp.minimum(i, np_steps - 1), 0)),
                  pl.BlockSpec((tm, D_MODEL), lambda i: (jnp.maximum(i - np_steps, 0), 0))]
                 + [_const_spec(c.shape) for c in consts],
        out_specs=[pl.BlockSpec((LOCAL_ROWS, HALF), lambda i: (i, 0)),
                   pl.BlockSpec((tm, LANES), lambda i: (i, 0)),
                   pl.BlockSpec((tm, LANES), lambda i: (i, 0)),
                   pl.BlockSpec((1, 8, LANES), lambda i: (i, 0, 0))],
        out_shape=[jax.ShapeDtypeStruct((nt * LOCAL_ROWS, HALF), U32),
                   jax.ShapeDtypeStruct((nt * tm, LANES), I32),
                   jax.ShapeDtypeStruct((nt * tm, LANES), F32),
                   jax.ShapeDtypeStruct((nt, 8, LANES), I32)],
        compiler_params=_params(1),
        name="route",
    )(hp, hs, *consts)


def _granule_plan(seg, n_tiles):
    cnt = seg[:, 0, ROUTER_LANE0:ROUTER_LANE0 + MOE_EXPERTS]
    lstart = seg[:, 1, ROUTER_LANE0:ROUTER_LANE0 + MOE_EXPERTS]
    ntt = cnt.shape[0]
    ng = (cnt + GRANULE - 1) // GRANULE
    cum = jnp.cumsum(ng, axis=0)
    first = (jnp.arange(ntt, dtype=I32)[:, None] * LOCAL_ROWS + lstart) // GRANULE - (cum - ng)
    total = cum[-1]
    tiles_e = (total + TILE_GRANULES - 1) // TILE_GRANULES
    tend = jnp.cumsum(tiles_e)
    tstart = tend - tiles_e
    tile = jnp.arange(n_tiles, dtype=I32)
    te = jnp.minimum(jnp.sum((tile[:, None] >= tend[None, :]).astype(I32), axis=1), MOE_EXPERTS - 1)
    n_used = tend[-1].astype(I32).reshape(1)
    sel = (te[:, None] == jnp.arange(MOE_EXPERTS, dtype=I32)[None, :]).astype(I32)
    pick = lambda table: jnp.sum(sel[:, None, :] * table[None, :, :], axis=2)
    cum_t, first_t = pick(cum), pick(first)
    off = (tile - jnp.sum(sel * tstart[None, :], axis=1))[:, None] * TILE_GRANULES + jnp.arange(TILE_GRANULES, dtype=I32)[None, :]
    valid = (off < jnp.sum(sel * total[None, :], axis=1)[:, None]) & (tile < n_used[0])[:, None]
    j = jnp.sum((cum_t[:, None, :] <= off[:, :, None]).astype(I32), axis=2)
    jsel = (j[:, :, None] == jnp.arange(ntt, dtype=I32)[None, None, :]).astype(I32)
    src = jnp.sum(jsel * first_t[:, None, :], axis=2) + off
    spare = lambda k: ((k // SPARE_GRANULES) * LOCAL_ROWS + LOCAL_USED) // GRANULE + k % SPARE_GRANULES
    assert (ntt - 1) * SPARE_GRANULES >= 2 * TILE_GRANULES
    zero_granule = spare((ntt - 1) * SPARE_GRANULES)
    trash = spare((tile % 2)[:, None] * TILE_GRANULES + jnp.arange(TILE_GRANULES, dtype=I32)[None, :])
    gsrc = jnp.where(valid, src, zero_granule).astype(I32).reshape(-1)
    gdst = jnp.where(valid, src, trash).astype(I32).reshape(-1)
    n_valid = jnp.sum(valid.astype(I32), axis=1).astype(I32)
    return te.astype(I32), n_used, gsrc, gdst, n_valid


def _grouped_kernel(te_ref, nu_ref, gsrc_ref, gdst_ref, nv_ref, xin_ref, wg_ref, wu_ref, wd_ref, xout_ref,
                    xbuf, ybuf, wg16, wu16, wd16, sem_in, sem_out):
    i = pl.program_id(0)
    nu = nu_ref[0]
    tm = MOE_ROW_TILE

    def hbm_rows(table_ref, tile, g):
        return pl.ds(pl.multiple_of(table_ref[tile * TILE_GRANULES + g] * GRANULE, GRANULE), GRANULE)

    def fetch_start(tile, slot):
        for g in range(TILE_GRANULES):
            pltpu.make_async_copy(xin_ref.at[hbm_rows(gsrc_ref, tile, g)],
                                  xbuf.at[slot, pl.ds(g * GRANULE, GRANULE)], sem_in.at[slot]).start()

    def store_start(tile, slot):
        for g in range(TILE_GRANULES):
            pltpu.make_async_copy(ybuf.at[slot, pl.ds(g * GRANULE, GRANULE)],
                                  xout_ref.at[hbm_rows(gdst_ref, tile, g)], sem_out.at[slot]).start()

    def fetch_wait(slot):
        pltpu.make_async_copy(xin_ref.at[pl.ds(0, tm)], xbuf.at[slot], sem_in.at[slot]).wait()

    def store_wait(slot):
        pltpu.make_async_copy(ybuf.at[slot], xout_ref.at[pl.ds(0, tm)], sem_out.at[slot]).wait()

    def step(slot):
        @pl.when(i == 0)
        def _():
            ybuf[...] = jnp.zeros_like(ybuf)
            fetch_start(0, 0)

        @pl.when(i + 1 < nu)
        def _():
            fetch_start(i + 1, 1 - slot)

        @pl.when(i >= 2)
        def _():
            store_wait(slot)

        @pl.when((i == 0) | (te_ref[i] != te_ref[jnp.maximum(i - 1, 0)]))
        def _():
            wg16[...] = wg_ref[0].astype(BF)
            wu16[...] = wu_ref[0].astype(BF)
            wd16[...] = wd_ref[0].astype(BF)

        fetch_wait(slot)
        n_valid = nv_ref[i]
        quarter = TILE_GRANULES // 4
        for part in range(1, 5):
            rows = part * quarter * GRANULE

            @pl.when((n_valid > (part - 1) * quarter) & (n_valid <= part * quarter))
            def _(rows=rows):
                x_lo, x_hi = _unpack_rows(xbuf[slot, 0:rows, :])
                gate = _dot(x_lo, wg16[0:HALF, :]) + _dot(x_hi, wg16[HALF:, :])
                up = _dot(x_lo, wu16[0:HALF, :]) + _dot(x_hi, wu16[HALF:, :])
                ybuf[slot, 0:rows, :] = _pack_rows(_dot((_silu(gate) * up).astype(BF), wd16[...]))

        store_start(i, slot)

        @pl.when(i == nu - 1)
        def _():
            store_wait(slot)

            @pl.when(i >= 1)
            def _():
                store_wait(1 - slot)

    for slot in (0, 1):
        @pl.when((i < nu) & (lax.rem(i, 2) == slot))
        def _(slot=slot):
            step(slot)


def _grouped(x_local, te, n_used, gsrc, gdst, n_valid, wg, wu, wd):
    tm = MOE_ROW_TILE
    n_tiles = te.shape[0]
    used = lambda i, nu: jnp.minimum(i, jnp.maximum(nu[0] - 1, 0))
    wspec = lambda shape: pl.BlockSpec((1,) + shape, lambda i, te, nu, *_: (te[used(i, nu)], 0, 0))
    grid_spec = pltpu.PrefetchScalarGridSpec(
        num_scalar_prefetch=5,
        grid=(n_tiles,),
        in_specs=[pl.BlockSpec(memory_space=pl.ANY), wspec((D_MODEL, MOE_D_FF)), wspec((D_MODEL, MOE_D_FF)),
                  wspec((MOE_D_FF, D_MODEL))],
        out_specs=pl.BlockSpec(memory_space=pl.ANY),
        scratch_shapes=[pltpu.VMEM((2, tm, HALF), U32), pltpu.VMEM((2, tm, HALF), U32),
                        pltpu.VMEM((D_MODEL, MOE_D_FF), BF), pltpu.VMEM((D_MODEL, MOE_D_FF), BF),
                        pltpu.VMEM((MOE_D_FF, D_MODEL), BF),
                        pltpu.SemaphoreType.DMA((2,)), pltpu.SemaphoreType.DMA((2,))],
    )
    return pl.pallas_call(
        _grouped_kernel,
        grid_spec=grid_spec,
        out_shape=jax.ShapeDtypeStruct(x_local.shape, x_local.dtype),
        input_output_aliases={5: 0},
        compiler_params=_params(1),
        name="grouped",
    )(te, n_used, gsrc, gdst, n_valid, x_local, wg, wu, wd)


def _merge_kernel(h_ref, idx_ref, wt_ref, g_ref, yl_ref, o_ref):
    tm = h_ref.shape[0]
    idx = idx_ref[...]
    wt = wt_ref[...]
    r_id = lax.broadcasted_iota(I32, (tm, LOCAL_ROWS), 1)
    sel = jnp.where(r_id == idx[:, 0:1], wt[:, 0:1], 0.0) + jnp.where(r_id == idx[:, 1:2], wt[:, 1:2], 0.0)
    y_lo, y_hi = _unpack_rows(yl_ref[...])
    sel16 = sel.astype(BF)
    moe = jnp.concatenate([_dot(sel16, y_lo), _dot(sel16, y_hi)], axis=1)
    o_ref[...] = _rms(h_ref[...] + moe, g_ref[...])


def _merge(h2d, idx, wt, g, y_local, *, tile_offset):
    t = h2d.shape[0]
    tm = TOKEN_TILE
    return pl.pallas_call(
        _merge_kernel,
        grid=(t // tm,),
        in_specs=[pl.BlockSpec((tm, D_MODEL), lambda i: (i, 0)),
                  pl.BlockSpec((tm, LANES), lambda i: (i + tile_offset, 0)),
                  pl.BlockSpec((tm, LANES), lambda i: (i + tile_offset, 0)),
                  _const_spec(g.shape),
                  pl.BlockSpec((LOCAL_ROWS, HALF), lambda i: (i + tile_offset, 0))],
        out_specs=pl.BlockSpec((tm, D_MODEL), lambda i: (i, 0)),
        out_shape=jax.ShapeDtypeStruct((t, D_MODEL), F32),
        compiler_params=_params(1),
        name="merge",
    )(h2d, idx, wt, g, y_local)


def _perm_matrix(bt, tl):
    m = bt * tl
    p = np.zeros((m, m), np.float32)
    for l in range(tl):
        for b in range(bt):
            p[l * bt + b, b * tl + l] = 1.0
    return p


def _s5_operators(lam_re, lam_im, log_step, b_re, b_im, c_re, c_im):
    lam = lax.complex(lam_re, lam_im)
    step = jnp.exp(log_step)[:, None]
    lam_bar = jnp.exp(lam * step)
    b_bar = ((lam_bar - 1.0) / lam)[..., None] * lax.complex(b_re, b_im)
    gl = LANES // S5_GROUP
    eye = jnp.eye(gl, dtype=F32)

    def in_block(bpart):
        bp = bpart.reshape(S5_SLABS, gl, S5_STATE, S5_GROUP)
        blk = jnp.einsum("sgnk,gh->sgkhn", bp, eye)
        return blk.reshape(S5_SLABS, LANES, gl * S5_STATE)

    def out_block(cpart):
        cp = cpart.reshape(S5_SLABS, gl, S5_GROUP, S5_STATE)
        blk = jnp.einsum("sgkn,gh->sgnhk", cp, eye)
        return blk.reshape(S5_SLABS, gl * S5_STATE, LANES)

    bblk = jnp.concatenate([in_block(jnp.real(b_bar)), in_block(jnp.imag(b_bar))], axis=2).astype(BF)
    cblk = jnp.concatenate([out_block(c_re), out_block(-c_im)], axis=1).astype(BF)
    lre = jnp.broadcast_to(jnp.real(lam_bar).reshape(1, S5_FLAT), (S5_SEQ_TILE, S5_FLAT))
    lim = jnp.broadcast_to(jnp.imag(lam_bar).reshape(1, S5_FLAT), (S5_SEQ_TILE, S5_FLAT))
    return bblk, cblk, lre, lim


def _pad_lanes(v, width=LANES):
    return jnp.pad(v, [(0, 0)] * (v.ndim - 1) + [(0, width - v.shape[-1])])


def kernel(x_prompt, x_sample, state_conv, state_ssd, state_s5_re, state_s5_im, cache_mem_k, cache_mem_v, mem_prompt, norm_mix_g, w_in, conv_w, conv_b, ssd_dt_bias, ssd_a_log, ssd_d, ssd_norm_g, w_ssd_branch, s5_lambda_re, s5_lambda_im, s5_log_step, s5_b_re, s5_b_im, s5_c_re, s5_c_im, s5_d, w_glu, b_glu, w_mix_out, norm_mem_q_g, norm_mem_kv_g, w_mem_q, w_mem_k, w_mem_v, w_mem_o, norm_ffn_g, w_router_group, b_router_group, w_router_expert, b_router_expert, w_exp_gate, w_exp_up, w_exp_down, norm_final_g):
    bp, lp, _ = x_prompt.shape
    bs, ls, _ = x_sample.shape
    tp, ts = bp * lp, bs * ls
    row1 = lambda v: v.reshape(1, -1).astype(F32)

    w_in0 = w_in[0]
    o_xbc = D_MODEL
    o_dt = o_xbc + SSD_CONV_DIM
    o_u5 = o_dt + SSD_HEADS
    w_main = jnp.concatenate([w_in0[:, :o_dt], w_in0[:, o_u5:]], axis=1).astype(BF)
    w_dt = _pad_lanes(w_in0[:, o_dt:o_u5]).astype(BF)
    dt_bias = _pad_lanes(row1(ssd_dt_bias[0]))
    a_log = _pad_lanes(row1(ssd_a_log[0]))
    dskip = jnp.repeat(ssd_d[0].astype(F32), SSD_HEADDIM).reshape(1, D_MODEL)
    eexp_np = np.zeros((LANES, D_MODEL), np.float32)
    for h in range(SSD_HEADS):
        eexp_np[h, h * SSD_HEADDIM:(h + 1) * SSD_HEADDIM] = 1.0
    eexp = jnp.asarray(eexp_np).astype(BF)
    eye16 = jnp.asarray(np.eye(SSD_HEADS, LANES, dtype=np.float32)).astype(BF)
    bblk, cblk, lre, lim = _s5_operators(s5_lambda_re[0], s5_lambda_im[0], s5_log_step[0], s5_b_re[0],
                                         s5_b_im[0], s5_c_re[0], s5_c_im[0])
    s5d = row1(s5_d[0])
    w_kv = jnp.concatenate([w_mem_k[0], w_mem_v[0]], axis=1).astype(BF)
    w_r = _pad_lanes(jnp.concatenate([w_router_group[0], w_router_expert[0]], axis=1).astype(F32))
    wr_h, wr_m, wr_l = _split3(w_r)
    b_r = _pad_lanes(row1(jnp.concatenate([b_router_group[0], b_router_expert[0]])))
    wg = w_exp_gate[0].reshape(MOE_EXPERTS, D_MODEL, MOE_D_FF)
    wu = w_exp_up[0].reshape(MOE_EXPERTS, D_MODEL, MOE_D_FF)
    wd = w_exp_down[0].reshape(MOE_EXPERTS, MOE_D_FF, D_MODEL)

    def mixer(x2d, batch, seqlen, conv0, h0, s5re0, s5im0):
        z, xbc, u5, ga, gb, dt = _inproj(x2d, row1(norm_mix_g[0]), w_main, w_dt, dt_bias)
        y_ssd, new_conv, new_ssd = _ssd(xbc, dt, conv_w[0].astype(F32), row1(conv_b[0]), a_log, dskip, eexp, eye16,
                                        conv0, h0, batch=batch, seqlen=seqlen)
        if h0 is None:
            tl = S5_TIME_TILE
            u_in = u5.reshape(batch, seqlen, D_MODEL)
        else:
            tl = seqlen
            u_in = u5
        perm = _perm_matrix(S5_SEQ_TILE, tl)
        y5, new_re, new_im = _s5(u_in, jnp.asarray(perm).astype(BF), jnp.asarray(perm.T).astype(BF), bblk, cblk,
                                 lre, lim, s5d, s5re0, s5im0, batch=batch, seqlen=seqlen, tl=tl)
        h1 = _mixout(y_ssd, z, y5.reshape(-1, D_MODEL), ga, gb, x2d, row1(ssd_norm_g[0]),
                     w_ssd_branch[0].astype(BF), w_glu[0].astype(BF), row1(b_glu[0]), w_mix_out[0].astype(BF))
        return h1, new_conv, new_ssd, new_re, new_im

    xp2 = x_prompt.reshape(tp, D_MODEL)
    xs2 = x_sample.reshape(ts, D_MODEL)
    h1p, conv_p, ssd_p, re_p, im_p = mixer(xp2, bp, lp, None, None, None, None)
    h1s, conv_s, ssd_s, re_s, im_s = mixer(xs2, bs, ls, state_conv[0], state_ssd[0],
                                           state_s5_re[0].reshape(bs, S5_FLAT), state_s5_im[0].reshape(bs, S5_FLAT))

    mk_p, mv_p = _memkv(mem_prompt.reshape(bp * MEM_LEN, D_MODEL), row1(norm_mem_kv_g[0]), w_kv)
    gq = row1(norm_mem_q_g[0])
    wq = w_mem_q[0].astype(BF)
    wo = w_mem_o[0].astype(BF)
    h2p = _attn(h1p, gq, wq, wo, mk_p.reshape(bp, MEM_LEN, D_MODEL), mv_p.reshape(bp, MEM_LEN, D_MODEL),
                rows=TOKEN_TILE, nkv=1, rows_per_seq=lp)
    sample_seqs = 8
    h2s = _attn(h1s, gq, wq, wo, cache_mem_k[0], cache_mem_v[0], rows=sample_seqs * ls, nkv=sample_seqs,
                rows_per_seq=ls)

    g_ffn = row1(norm_ffn_g[0])
    x_local, r_idx, r_wt, seg = _route(h2p, h2s, g_ffn, jnp.concatenate([wr_h, wr_m], axis=1), wr_h, b_r)
    token_tiles = (tp + ts) // TOKEN_TILE
    max_granules = (2 * (tp + ts) + token_tiles * MOE_EXPERTS * (GRANULE - 1)) // GRANULE
    n_tiles = max_granules // TILE_GRANULES + MOE_EXPERTS
    te, n_used, gsrc, gdst, n_valid = _granule_plan(seg, n_tiles)
    y_local = _grouped(x_local, te, n_used, gsrc, gdst, n_valid, wg, wu, wd)
    gf = row1(norm_final_g)
    y_prompt = _merge(h2p, r_idx, r_wt, gf, y_local, tile_offset=0)
    y_sample = _merge(h2s, r_idx, r_wt, gf, y_local, tile_offset=tp // TOKEN_TILE)

    return (y_prompt.reshape(bp, lp, D_MODEL), y_sample.reshape(bs, ls, D_MODEL),
            conv_p[None], ssd_p[None],
            re_p.reshape(1, bp, S5_GROUPS, S5_STATE), im_p.reshape(1, bp, S5_GROUPS, S5_STATE),
            mk_p.reshape(1, bp, MEM_LEN, MEM_HEADS, MEM_HEAD_DIM), mv_p.reshape(1, bp, MEM_LEN, MEM_HEADS, MEM_HEAD_DIM),
            conv_s[None], ssd_s[None],
            re_s.reshape(1, bs, S5_GROUPS, S5_STATE), im_s.reshape(1, bs, S5_GROUPS, S5_STATE))
```

```python
import functools
import math

import numpy as np
import jax
import jax.numpy as jnp
from jax import lax
from jax.experimental import pallas as pl
from jax.experimental.pallas import tpu as pltpu

F32 = jnp.float32
BF = jnp.bfloat16
I32 = jnp.int32

D_MODEL = 1024
SSD_HEADS = 16
SSD_HEADDIM = 64
SSD_GROUPS = 2
SSD_STATE = 128
SSD_CONV = 4
SSD_CONV_DIM = 1536
HEADS_PER_GROUP = SSD_HEADS // SSD_GROUPS
S5_GROUPS = 64
S5_GROUP = 16
S5_STATE = 64
S5_FLAT = S5_GROUPS * S5_STATE
S5_SLABS = D_MODEL // 128
MEM_LEN = 256
MEM_HEADS = 4
MEM_HEAD_DIM = 256
MOE_GROUPS = 4
MOE_EXPERTS_PER_GROUP = 8
MOE_EXPERTS = MOE_GROUPS * MOE_EXPERTS_PER_GROUP
MOE_D_FF = 512
NORM_EPS = 1e-6

LANES = 128
VMEM_LIMIT_BYTES = 56 * 1024 * 1024
TOKEN_TILE = 512
MIX_TILE = 1024
SSD_CHUNK = 128
S5_TIME_TILE = 64
MOE_ROW_TILE = 512
NEG_BIG = -1e30


def _params(n_axes):
    return pltpu.CompilerParams(dimension_semantics=("arbitrary",) * n_axes,
                                vmem_limit_bytes=VMEM_LIMIT_BYTES)


def _const_spec(shape):
    nd = len(shape)
    return pl.BlockSpec(shape, lambda *_: (0,) * nd)


def _resident_spec(shape):
    nd = len(shape)
    return pl.BlockSpec(shape, lambda *_: (0,) * nd, pipeline_mode=pl.Buffered(1))


def _dot(a, b):
    return jnp.dot(a, b, preferred_element_type=F32)


def _dot_nt(a, b):
    return lax.dot_general(a, b, (((1,), (1,)), ((), ())), preferred_element_type=F32)


def _split3(v):
    h = v.astype(BF)
    r = v - h.astype(F32)
    m = r.astype(BF)
    l = (r - m.astype(F32)).astype(BF)
    return h, m, l


def _dot_exact(a_bf, v, nt=False):
    f = _dot_nt if nt else _dot
    h, m, l = _split3(v)
    return f(a_bf, h) + f(a_bf, m) + f(a_bf, l)


def _rms(x, g):
    return x * lax.rsqrt(jnp.mean(x * x, axis=-1, keepdims=True) + NORM_EPS) * g


def _sigmoid(x):
    return 0.5 * jnp.tanh(0.5 * x) + 0.5


def _silu(x):
    return x * _sigmoid(x)


def _proj_cols(xb, w_ref, o_ref, c0, width):
    for c in range(0, width, 512):
        ce = min(c + 512, width)
        o_ref[:, c:ce] = _dot(xb, w_ref[:, c0 + c:c0 + ce]).astype(o_ref.dtype)


def _inproj_kernel(x_ref, g_ref, w_ref, wdt_ref, dtb_ref, z_ref, xbc_ref, u5_ref, ga_ref, gb_ref, dt_ref):
    xb = _rms(x_ref[...], g_ref[...]).astype(BF)
    c0 = 0
    for o_ref in (z_ref, xbc_ref, u5_ref, ga_ref, gb_ref):
        width = o_ref.shape[1]
        _proj_cols(xb, w_ref, o_ref, c0, width)
        c0 += width
    raw = _dot(xb, wdt_ref[...]) + dtb_ref[...]
    dt_ref[...] = jnp.maximum(raw, 0.0) + jnp.log1p(jnp.exp(-jnp.abs(raw)))


def _inproj(x2d, g, w_main, w_dt, dt_bias):
    t = x2d.shape[0]
    tm = MIX_TILE
    widths = (D_MODEL, SSD_CONV_DIM, D_MODEL, D_MODEL, D_MODEL)
    row = lambda w: pl.BlockSpec((tm, w), lambda i: (i, 0))
    return pl.pallas_call(
        _inproj_kernel,
        grid=(t // tm,),
        in_specs=[row(D_MODEL), _const_spec(g.shape), _resident_spec(w_main.shape), _resident_spec(w_dt.shape),
                  _const_spec(dt_bias.shape)],
        out_specs=[row(w) for w in widths] + [row(LANES)],
        out_shape=[jax.ShapeDtypeStruct((t, w), BF) for w in widths] + [jax.ShapeDtypeStruct((t, LANES), F32)],
        compiler_params=_params(1),
        name="inproj",
    )(x2d, g, w_main, w_dt, dt_bias)


def _memkv_kernel(x_ref, g_ref, w_ref, k_ref, v_ref):
    xb = _rms(x_ref[...], g_ref[...]).astype(BF)
    _proj_cols(xb, w_ref, k_ref, 0, D_MODEL)
    _proj_cols(xb, w_ref, v_ref, D_MODEL, D_MODEL)


def _memkv(mem2d, g, w_kv):
    t = mem2d.shape[0]
    tm = TOKEN_TILE
    row = pl.BlockSpec((tm, D_MODEL), lambda i: (i, 0))
    return pl.pallas_call(
        _memkv_kernel,
        grid=(t // tm,),
        in_specs=[row, _const_spec(g.shape), _const_spec(w_kv.shape)],
        out_specs=[row, row],
        out_shape=[jax.ShapeDtypeStruct((t, D_MODEL), F32)] * 2,
        compiler_params=_params(1),
        name="memkv",
    )(mem2d, g, w_kv)


def _ssd_kernel(*refs, lq, has_h0):
    q = SSD_CHUNK
    nseq = q // lq
    if has_h0:
        (xbc_ref, dt_ref, cw_ref, cb_ref, alog_ref, dsk_ref, eexp_ref, eye_ref, shift_ref, conv0_ref, h0_ref,
         y_ref, convo_ref, ho_ref, hbuf, cacc, yacc, yint, xwt, tot_s) = refs
        hin_ref = h0_ref
        hbuf[...] = jnp.zeros_like(hbuf)
        hbuf[:, 0:SSD_CONV - 1, :] = conv0_ref[...]
    else:
        (xbc_ref, dt_ref, cw_ref, cb_ref, alog_ref, dsk_ref, eexp_ref, eye_ref, shift_ref,
         y_ref, convo_ref, ho_ref, hbuf, cacc, yacc, yint, xwt, tot_s) = refs
        hin_ref = ho_ref

        @pl.when(pl.program_id(1) == 0)
        def _():
            ho_ref[...] = jnp.zeros_like(ho_ref)
            hbuf[...] = jnp.zeros_like(hbuf)

    x16 = xbc_ref[...]
    xraw = x16.astype(F32)
    acc = cb_ref[...] + xraw * cw_ref[SSD_CONV - 1:SSD_CONV, :]
    for k in range(SSD_CONV - 1):
        acc = acc + _dot(shift_ref[k], x16) * cw_ref[k:k + 1, :]
    cacc[...] = acc
    for i in range(nseq):
        tail = xraw[(i + 1) * lq - (SSD_CONV - 1):(i + 1) * lq]
        convo_ref[i] = tail
        corr = hbuf[i, 0:8, :] * cw_ref[0:1, :]
        for k in range(1, SSD_CONV - 1):
            corr = corr + hbuf[i, k:k + 8, :] * cw_ref[k:k + 1, :]
        cacc[i * lq:i * lq + 8, :] += corr
        if not has_h0:
            hbuf[i, 0:SSD_CONV - 1, :] = tail
    xc = _silu(cacc[...])
    xs = xc[:, :D_MODEL]
    xs_bf = xs.astype(BF)
    bm_bf = xc[:, D_MODEL:D_MODEL + SSD_GROUPS * SSD_STATE].astype(BF)
    cm = xc[:, D_MODEL + SSD_GROUPS * SSD_STATE:]
    cm_bf = cm.astype(BF)

    dt = dt_ref[...]
    da = dt * (-jnp.exp(alog_ref[...]))
    ri = lax.broadcasted_iota(I32, (q, q), 0)
    ci = lax.broadcasted_iota(I32, (q, q), 1)
    if nseq == 1:
        causal = ci <= ri
    else:
        sh = int(math.log2(lq))
        same = lax.shift_right_logical(ri, sh) == lax.shift_right_logical(ci, sh)
        causal = same & (ci <= ri)
    lmat = jnp.where(causal, 1.0, 0.0).astype(BF)
    cs = _dot_exact(lmat, da)
    if nseq == 1:
        tot = jnp.broadcast_to(cs[q - 1:q, :], (q, LANES))
    else:
        tot = _dot_exact(jnp.where(same, 1.0, 0.0).astype(BF), da)
    tot_s[...] = tot
    eye = eye_ref[...]
    cs_t = _dot_exact(eye, cs, nt=True)
    dt_t = _dot_exact(eye, dt, nt=True)
    eexp = eexp_ref[...]
    ecs_h, ecs_m, _ = _split3(jnp.exp(cs))
    ecs_x = _dot(ecs_h, eexp) + _dot(ecs_m, eexp)
    wend_x = _dot((jnp.exp(tot - cs) * dt).astype(BF), eexp)

    lane = lax.broadcasted_iota(I32, (q, LANES), 1)
    for g in range(SSD_GROUPS):
        cbg = _dot_nt(cm_bf[:, g * SSD_STATE:(g + 1) * SSD_STATE], bm_bf[:, g * SSD_STATE:(g + 1) * SSD_STATE])
        for jp in range(HEADS_PER_GROUP // 2):
            j = g * (HEADS_PER_GROUP // 2) + jp
            ms = []
            for h in (2 * j, 2 * j + 1):
                col = jnp.broadcast_to(cs[:, h:h + 1], (q, q))
                row = jnp.broadcast_to(cs_t[h:h + 1, :], (q, q))
                dtr = jnp.broadcast_to(dt_t[h:h + 1, :], (q, q))
                ms.append((jnp.where(causal, jnp.exp(col - row), 0.0) * cbg * dtr).astype(BF))
            mp = jnp.concatenate(ms, axis=1)
            xp = xs_bf[:, j * LANES:(j + 1) * LANES]
            zero = jnp.zeros_like(xp)
            x2 = jnp.concatenate([jnp.where(lane < SSD_HEADDIM, xp, zero),
                                  jnp.where(lane >= SSD_HEADDIM, xp, zero)], axis=0)
            yacc[:, j * LANES:(j + 1) * LANES] = _dot(mp, x2)

    gw = HEADS_PER_GROUP * SSD_HEADDIM
    xw = xs * wend_x
    for g in range(SSD_GROUPS):
        xwt[g * gw:(g + 1) * gw, :] = xw[:, g * gw:(g + 1) * gw].T.astype(BF)

    if nseq == 1:
        for g in range(SSD_GROUPS):
            hin_g = hin_ref[0, g * HEADS_PER_GROUP:(g + 1) * HEADS_PER_GROUP].reshape(gw, SSD_STATE)
            yint[:, g * gw:(g + 1) * gw] = _dot_nt(cm_bf[:, g * SSD_STATE:(g + 1) * SSD_STATE], hin_g.astype(BF))
            s_new = _dot(xwt[g * gw:(g + 1) * gw, :], bm_bf[:, g * SSD_STATE:(g + 1) * SSD_STATE])
            for hl in range(HEADS_PER_GROUP):
                h = g * HEADS_PER_GROUP + hl
                dec = jnp.exp(jnp.broadcast_to(tot[0:1, h:h + 1], (SSD_HEADDIM, SSD_STATE)))
                ho_ref[0, h] = dec * hin_ref[0, h] + s_new[hl * SSD_HEADDIM:(hl + 1) * SSD_HEADDIM]
    else:
        yint[...] = jnp.zeros_like(yint)
        sh = int(math.log2(lq))
        rowseq = lax.shift_right_logical(lax.broadcasted_iota(I32, (q, SSD_STATE), 0), sh)
        colseq = lax.shift_right_logical(lax.broadcasted_iota(I32, (gw, q), 1), sh)

        def seq_body(i, carry):
            trow = tot_s[pl.ds(i * lq, 1), :]
            for g in range(SSD_GROUPS):
                hin_g = hin_ref[i, g * HEADS_PER_GROUP:(g + 1) * HEADS_PER_GROUP].reshape(gw, SSD_STATE)
                cmg = cm[:, g * SSD_STATE:(g + 1) * SSD_STATE]
                lhs = jnp.where(rowseq == i, cmg, 0.0).astype(BF)
                yint[:, g * gw:(g + 1) * gw] += _dot_nt(lhs, hin_g.astype(BF))
                xg = xwt[g * gw:(g + 1) * gw, :]
                xi = jnp.where(colseq == i, xg, jnp.zeros_like(xg))
                s_new = _dot(xi, bm_bf[:, g * SSD_STATE:(g + 1) * SSD_STATE])
                for hl in range(HEADS_PER_GROUP):
                    h = g * HEADS_PER_GROUP + hl
                    dec = jnp.exp(jnp.broadcast_to(trow[:, h:h + 1], (SSD_HEADDIM, SSD_STATE)))
                    ho_ref[i, h] = dec * hin_ref[i, h] + s_new[hl * SSD_HEADDIM:(hl + 1) * SSD_HEADDIM]
            return carry

        lax.fori_loop(0, nseq, seq_body, 0)

    y_ref[...] = (yacc[...] + yint[...] * ecs_x + dsk_ref[...] * xs).astype(y_ref.dtype)


def _ssd(xbc, dt, conv_w, conv_b, a_log, dskip, eexp, eye, conv0, h0, *, batch, seqlen):
    q = SSD_CHUNK
    t = xbc.shape[0]
    has_h0 = h0 is not None
    if has_h0:
        lq = seqlen
        nseq = q // lq
        grid = (t // q,)
        rmap = lambda i: (i, 0)
        smap3 = lambda i: (i, 0, 0)
        smap4 = lambda i: (i, 0, 0, 0)
    else:
        lq = q
        nseq = 1
        nc = seqlen // q
        grid = (batch, nc)
        rmap = lambda b, c: (b * nc + c, 0)
        smap3 = lambda b, c: (b, 0, 0)
        smap4 = lambda b, c: (b, 0, 0, 0)
    shifts = np.zeros((SSD_CONV - 1, q, q), np.float32)
    for k in range(SSD_CONV - 1):
        for r in range(q):
            if r % lq + k - (SSD_CONV - 1) >= 0:
                shifts[k, r, r + k - (SSD_CONV - 1)] = 1.0
    consts = [conv_w, conv_b, a_log, dskip, eexp, eye, jnp.asarray(shifts).astype(BF)]
    in_specs = [pl.BlockSpec((q, SSD_CONV_DIM), rmap), pl.BlockSpec((q, LANES), rmap)]
    in_specs += [_const_spec(c.shape) for c in consts]
    args = [xbc, dt] + consts
    if has_h0:
        in_specs += [pl.BlockSpec((nseq, SSD_CONV - 1, SSD_CONV_DIM), smap3),
                     pl.BlockSpec((nseq, SSD_HEADS, SSD_HEADDIM, SSD_STATE), smap4)]
        args += [conv0, h0]
    return pl.pallas_call(
        functools.partial(_ssd_kernel, lq=lq, has_h0=has_h0),
        grid=grid,
        in_specs=in_specs,
        out_specs=[pl.BlockSpec((q, D_MODEL), rmap),
                   pl.BlockSpec((nseq, SSD_CONV - 1, SSD_CONV_DIM), smap3),
                   pl.BlockSpec((nseq, SSD_HEADS, SSD_HEADDIM, SSD_STATE), smap4)],
        out_shape=[jax.ShapeDtypeStruct((t, D_MODEL), BF),
                   jax.ShapeDtypeStruct((batch, SSD_CONV - 1, SSD_CONV_DIM), F32),
                   jax.ShapeDtypeStruct((batch, SSD_HEADS, SSD_HEADDIM, SSD_STATE), F32)],
        scratch_shapes=[pltpu.VMEM((nseq, 16, SSD_CONV_DIM), F32),
                        pltpu.VMEM((q, SSD_CONV_DIM), F32),
                        pltpu.VMEM((q, D_MODEL), F32),
                        pltpu.VMEM((q, D_MODEL), F32),
                        pltpu.VMEM((D_MODEL, q), BF),
                        pltpu.VMEM((q, LANES), F32)],
        compiler_params=_params(len(grid)),
        name="ssd_sample" if has_h0 else "ssd_prompt",
    )(*args)


S5_SEQ_TILE = 8
S5_LANE_CHUNK = 1024


def _s5_kernel(*refs, tl, has_h0):
    bt = S5_SEQ_TILE
    m = bt * tl
    if has_h0:
        (u_ref, perm_ref, permt_ref, bblk_ref, cblk_ref, lre_ref, lim_ref, d_ref, h0re_ref, h0im_ref,
         y_ref, ore_ref, oim_ref, bre, bim, sre, sim) = refs
    else:
        (u_ref, perm_ref, permt_ref, bblk_ref, cblk_ref, lre_ref, lim_ref, d_ref,
         y_ref, ore_ref, oim_ref, bre, bim, sre, sim) = refs

    @pl.when(pl.program_id(1) == 0)
    def _():
        if has_h0:
            sre[...] = h0re_ref[...]
            sim[...] = h0im_ref[...]
        else:
            sre[...] = jnp.zeros_like(sre)
            sim[...] = jnp.zeros_like(sim)

    u = u_ref[...].reshape(m, D_MODEL)
    u_tm = _dot(perm_ref[...], u).astype(BF)
    half = S5_FLAT // S5_SLABS
    for j in range(S5_SLABS):
        r = _dot(u_tm[:, j * LANES:(j + 1) * LANES], bblk_ref[j])
        bre[:, j * half:(j + 1) * half] = r[:, :half]
        bim[:, j * half:(j + 1) * half] = r[:, half:]

    for jc in range(S5_FLAT // S5_LANE_CHUNK):
        sl = slice(jc * S5_LANE_CHUNK, (jc + 1) * S5_LANE_CHUNK)
        lr = lre_ref[:, sl]
        li = lim_ref[:, sl]

        def step(l, carry, sl=sl, lr=lr, li=li):
            sr, si = carry
            rows = pl.ds(pl.multiple_of(l * bt, bt), bt)
            nr = lr * sr - li * si + bre[rows, sl]
            ni = lr * si + li * sr + bim[rows, sl]
            bre[rows, sl] = nr
            bim[rows, sl] = ni
            return nr, ni

        sr, si = lax.fori_loop(0, tl, step, (sre[:, sl], sim[:, sl]), unroll=2)
        sre[:, sl] = sr
        sim[:, sl] = si

    ys = []
    for j in range(S5_SLABS):
        st = jnp.concatenate([bre[:, j * half:(j + 1) * half], bim[:, j * half:(j + 1) * half]], axis=1)
        ys.append(_dot(st.astype(BF), cblk_ref[j]))
    y_tm = jnp.concatenate(ys, axis=1).astype(BF)
    y_bm = _dot(permt_ref[...], y_tm) + d_ref[...] * u.astype(F32)
    y_ref[...] = y_bm.astype(y_ref.dtype).reshape(y_ref.shape)
    ore_ref[...] = sre[...]
    oim_ref[...] = sim[...]


def _s5(u, perm, permt, bblk, cblk, lre, lim, dvec, h0re, h0im, *, batch, seqlen, tl):
    bt = S5_SEQ_TILE
    m = bt * tl
    has_h0 = h0re is not None
    nb = batch // bt
    nt = seqlen // tl
    if u.ndim == 3:
        u_spec = pl.BlockSpec((bt, tl, D_MODEL), lambda b, t: (b, t, 0))
    else:
        u_spec = pl.BlockSpec((m, D_MODEL), lambda b, t: (b * nt + t, 0))
    st_spec = pl.BlockSpec((bt, S5_FLAT), lambda b, t: (b, 0))
    consts = [perm, permt, bblk, cblk, lre, lim, dvec]
    in_specs = [u_spec] + [_const_spec(c.shape) for c in consts]
    args = [u] + consts
    if has_h0:
        in_specs += [st_spec, st_spec]
        args += [h0re, h0im]
    return pl.pallas_call(
        functools.partial(_s5_kernel, tl=tl, has_h0=has_h0),
        grid=(nb, nt),
        in_specs=in_specs,
        out_specs=[u_spec, st_spec, st_spec],
        out_shape=[jax.ShapeDtypeStruct(u.shape, BF),
                   jax.ShapeDtypeStruct((batch, S5_FLAT), F32),
                   jax.ShapeDtypeStruct((batch, S5_FLAT), F32)],
        scratch_shapes=[pltpu.VMEM((m, S5_FLAT), F32), pltpu.VMEM((m, S5_FLAT), F32),
                        pltpu.VMEM((bt, S5_FLAT), F32), pltpu.VMEM((bt, S5_FLAT), F32)],
        compiler_params=_params(2),
        name="s5_sample" if has_h0 else "s5_prompt",
    )(*args)


def _mixout_kernel(y_ref, z_ref, y5_ref, ga_ref, gb_ref, x_ref, ng_ref, wa_ref, wglu_ref, bglu_ref, wmix_ref, o_ref):
    y = y_ref[...].astype(F32) * _silu(z_ref[...].astype(F32))
    yn = _rms(y, ng_ref[...]).astype(BF)
    branch_a = _dot(yn, wa_ref[...])
    glu = _dot(y5_ref[...], wglu_ref[...]) + bglu_ref[...]
    branch_b = glu[:, :D_MODEL] * _sigmoid(glu[:, D_MODEL:])
    merged = _sigmoid(ga_ref[...].astype(F32)) * branch_a + _sigmoid(gb_ref[...].astype(F32)) * branch_b
    o_ref[...] = x_ref[...] + _dot(merged.astype(BF), wmix_ref[...])


def _mixout(y, z, y5, ga, gb, x2d, ng, wa, wglu, bglu, wmix):
    t = x2d.shape[0]
    tm = MIX_TILE
    row = pl.BlockSpec((tm, D_MODEL), lambda i: (i, 0))
    consts = [ng, wa, wglu, bglu, wmix]
    return pl.pallas_call(
        _mixout_kernel,
        grid=(t // tm,),
        in_specs=[row] * 6 + [_resident_spec(c.shape) for c in consts],
        out_specs=row,
        out_shape=jax.ShapeDtypeStruct((t, D_MODEL), F32),
        compiler_params=_params(1),
        name="mixout",
    )(y, z, y5, ga, gb, x2d, *consts)


def _attn_kernel(h_ref, g_ref, wq_ref, wo_ref, k_ref, v_ref, o_ref, obuf, *, nkv, rows_per_seq):
    h = h_ref[...]
    r = h.shape[0]
    q32 = _dot(_rms(h, g_ref[...]).astype(BF), wq_ref[...])
    scale = MEM_HEAD_DIM ** -0.5
    if len(k_ref.shape) == 4:
        rq = rows_per_seq
        nrow = MEM_HEADS * rq
        row_head = lax.shift_right_logical(lax.broadcasted_iota(I32, (nrow, MEM_LEN * MEM_HEADS), 0), int(math.log2(rq)))
        col_head = lax.broadcasted_iota(I32, (nrow, MEM_LEN * MEM_HEADS), 1) & (MEM_HEADS - 1)
        visible = row_head == col_head
        for i in range(nkv):
            qs = jnp.concatenate([q32[i * rq:(i + 1) * rq, hd * MEM_HEAD_DIM:(hd + 1) * MEM_HEAD_DIM]
                                  for hd in range(MEM_HEADS)], axis=0).astype(BF)
            kf = k_ref[i].reshape(MEM_LEN * MEM_HEADS, MEM_HEAD_DIM).astype(BF)
            vf = v_ref[i].reshape(MEM_LEN * MEM_HEADS, MEM_HEAD_DIM).astype(BF)
            s = jnp.where(visible, _dot_nt(qs, kf) * scale, NEG_BIG)
            p = jnp.exp(s - jnp.max(s, axis=-1, keepdims=True))
            p = p / jnp.sum(p, axis=-1, keepdims=True)
            o = _dot(p.astype(BF), vf)
            for hd in range(MEM_HEADS):
                obuf[i * rq:(i + 1) * rq, hd * MEM_HEAD_DIM:(hd + 1) * MEM_HEAD_DIM] = o[hd * rq:(hd + 1) * rq]
    else:
        qv = q32.astype(BF)
        for hd in range(MEM_HEADS):
            sl = slice(hd * MEM_HEAD_DIM, (hd + 1) * MEM_HEAD_DIM)
            s = _dot_nt(qv[:, sl], k_ref[0, :, sl].astype(BF)) * scale
            p = jnp.exp(s - jnp.max(s, axis=-1, keepdims=True))
            p = p / jnp.sum(p, axis=-1, keepdims=True)
            obuf[:, sl] = _dot(p.astype(BF), v_ref[0, :, sl].astype(BF))
    o_ref[...] = h + _dot(obuf[...].astype(BF), wo_ref[...])


def _attn(h2d, g, wq, wo, k3, v3, *, rows, nkv, rows_per_seq):
    t = h2d.shape[0]
    nsteps = t // rows
    steps_per_kv = nsteps // (k3.shape[0] // nkv)
    row = pl.BlockSpec((rows, D_MODEL), lambda i: (i, 0))
    if k3.ndim == 4:
        kv = pl.BlockSpec((nkv, MEM_LEN, MEM_HEADS, MEM_HEAD_DIM), lambda i: (i // steps_per_kv, 0, 0, 0))
    else:
        kv = pl.BlockSpec((nkv, MEM_LEN, D_MODEL), lambda i: (i // steps_per_kv, 0, 0))
    consts = [g, wq, wo]
    return pl.pallas_call(
        functools.partial(_attn_kernel, nkv=nkv, rows_per_seq=rows_per_seq),
        grid=(nsteps,),
        in_specs=[row] + [_const_spec(c.shape) for c in consts] + [kv, kv],
        out_specs=row,
        out_shape=jax.ShapeDtypeStruct((t, D_MODEL), F32),
        scratch_shapes=[pltpu.VMEM((rows, D_MODEL), F32)],
        compiler_params=_params(1),
        name="attn",
    )(h2d, *consts, k3, v3)


ROUTER_LANE0 = MOE_GROUPS

U32 = jnp.uint32
HALF = D_MODEL // 2


def _pack_rows(x):
    xr = x.astype(BF).astype(F32)
    lo = lax.shift_right_logical(pltpu.bitcast(xr[:, :HALF], U32), jnp.uint32(16))
    hi = pltpu.bitcast(xr[:, HALF:], U32) & jnp.uint32(0xFFFF0000)
    return hi | lo


def _unpack_rows(w):
    lo = pltpu.bitcast(lax.shift_left(w, jnp.uint32(16)), F32).astype(BF)
    hi = pltpu.bitcast(w & jnp.uint32(0xFFFF0000), F32).astype(BF)
    return lo, hi


GRANULE = 8
LOCAL_USED = 2 * TOKEN_TILE + MOE_EXPERTS * (GRANULE - 1)
LOCAL_ROWS = 1280
SPARE_GRANULES = (LOCAL_ROWS - LOCAL_USED) // GRANULE
TILE_GRANULES = MOE_ROW_TILE // GRANULE


def _route_kernel(hp_ref, hs_ref, g_ref, wcat_ref, wh_ref, b_ref, tri_ref, eye_ref, upper_ref,
                  xl_ref, idx_ref, wt_ref, seg_ref, *, np_steps):
    i = pl.program_id(0)
    h = jnp.where(i < np_steps, hp_ref[...], hs_ref[...])
    xn = _rms(h, g_ref[...])
    tm = xn.shape[0]
    xh = xn.astype(BF)
    xm = (xn - xh.astype(F32)).astype(BF)
    both = _dot(xh, wcat_ref[...])
    logits = both[:, :LANES] + both[:, LANES:] + _dot(xm, wh_ref[...]) + b_ref[...]
    lane = lax.broadcasted_iota(I32, logits.shape, 1)
    lane_f = lane.astype(F32)
    no_lane = float(LANES)
    is_g = lane < MOE_GROUPS
    lg = jnp.where(is_g, logits, NEG_BIG)
    mg = jnp.max(lg, axis=-1, keepdims=True)
    gidx = jnp.min(jnp.where(lg == mg, lane_f, no_lane), axis=-1, keepdims=True).astype(I32)
    g_w = 1.0 / jnp.sum(jnp.where(is_g, jnp.exp(lg - mg), 0.0), axis=-1, keepdims=True)
    e_lane = lane - ROUTER_LANE0
    in_grp = ((e_lane >= 0) & (e_lane < MOE_EXPERTS)
              & (lax.shift_right_logical(jnp.maximum(e_lane, 0), 3) == gidx))
    le = jnp.where(in_grp, logits, NEG_BIG)
    m1 = jnp.max(le, axis=-1, keepdims=True)
    i1 = jnp.min(jnp.where(le == m1, lane_f, no_lane), axis=-1, keepdims=True)
    le2 = jnp.where(lane_f == i1, NEG_BIG, le)
    m2 = jnp.max(le2, axis=-1, keepdims=True)
    i2 = jnp.min(jnp.where(le2 == m2, lane_f, no_lane), axis=-1, keepdims=True)
    ratio = jnp.exp(m2 - m1)
    w1 = g_w / (1.0 + ratio)
    w2 = g_w * ratio / (1.0 + ratio)
    wt_ref[...] = jnp.where(lane == 0, w1, jnp.where(lane == 1, w2, 0.0))

    oh1 = jnp.where(lane_f == i1, 1.0, 0.0)
    oh2 = jnp.where(lane_f == i2, 1.0, 0.0)
    tot1 = jnp.sum(oh1, axis=0, keepdims=True)
    cnt = (tot1 + jnp.sum(oh2, axis=0, keepdims=True)).astype(I32)
    padded = lax.shift_left(lax.shift_right_logical(cnt + (GRANULE - 1), 3), 3)
    pad8 = jnp.broadcast_to(padded.astype(F32), (8, LANES)).astype(BF)
    lstart = _dot(pad8, upper_ref[...])[0:1, :]
    tri = tri_ref[...]
    l1 = jnp.sum(oh1 * (_dot(tri, oh1.astype(BF)) + lstart), axis=-1, keepdims=True)
    l2 = jnp.sum(oh2 * (_dot(tri, oh2.astype(BF)) + (lstart + tot1)), axis=-1, keepdims=True)
    idx_ref[...] = jnp.where(lane == 0, l1.astype(I32), jnp.where(lane == 1, l2.astype(I32), 0))
    sub = lax.broadcasted_iota(I32, (8, LANES), 0)
    seg_ref[0] = jnp.where(sub == 0, jnp.broadcast_to(cnt, (8, LANES)),
                           jnp.where(sub == 1, jnp.broadcast_to(lstart.astype(I32), (8, LANES)), 0))

    cols = jnp.where(lane == 0, l1, jnp.where(lane == 1, l2, 0.0))
    rows = _dot_exact(eye_ref[...], cols, nt=True)
    r_id = lax.broadcasted_iota(I32, (LOCAL_ROWS, tm), 0).astype(F32)
    perm = jnp.where(r_id == rows[0:1, :], 1.0, 0.0) + jnp.where(r_id == rows[1:2, :], 1.0, 0.0)
    xl_ref[...] = _pack_rows(_dot(perm.astype(BF), xh))


def _route(hp, hs, g, wcat, wh, b):
    tm = TOKEN_TILE
    np_steps = hp.shape[0] // tm
    ns_steps = hs.shape[0] // tm
    nt = np_steps + ns_steps
    tri = jnp.asarray(np.tril(np.ones((tm, tm), np.float32), -1)).astype(BF)
    eye8 = jnp.asarray(np.eye(8, LANES, dtype=np.float32)).astype(BF)
    upper = jnp.asarray(np.triu(np.ones((LANES, LANES), np.float32), 1)).astype(BF)
    consts = [g, wcat, wh, b, tri, eye8, upper]
    return pl.pallas_call(
        functools.partial(_route_kernel, np_steps=np_steps),
        grid=(nt,),
        in_specs=[pl.BlockSpec((tm, D_MODEL), lambda i: (jnp.minimum(i, np_steps - 1), 0)),
                  pl.BlockSpec((tm, D_MODEL), lambda i: (jnp.maximum(i - np_steps, 0), 0))]
                 + [_const_spec(c.shape) for c in consts],
        out_specs=[pl.BlockSpec((LOCAL_ROWS, HALF), lambda i: (i, 0)),
                   pl.BlockSpec((tm, LANES), lambda i: (i, 0)),
                   pl.BlockSpec((tm, LANES), lambda i: (i, 0)),
                   pl.BlockSpec((1, 8, LANES), lambda i: (i, 0, 0))],
        out_shape=[jax.ShapeDtypeStruct((nt * LOCAL_ROWS, HALF), U32),
                   jax.ShapeDtypeStruct((nt * tm, LANES), I32),
                   jax.ShapeDtypeStruct((nt * tm, LANES), F32),
                   jax.ShapeDtypeStruct((nt, 8, LANES), I32)],
        compiler_params=_params(1),
        name="route",
    )(hp, hs, *consts)


def _granule_plan(seg, n_tiles):
    cnt = seg[:, 0, ROUTER_LANE0:ROUTER_LANE0 + MOE_EXPERTS]
    lstart = seg[:, 1, ROUTER_LANE0:ROUTER_LANE0 + MOE_EXPERTS]
    ntt = cnt.shape[0]
    ng = (cnt + GRANULE - 1) // GRANULE
    cum = jnp.cumsum(ng, axis=0)
    first = (jnp.arange(ntt, dtype=I32)[:, None] * LOCAL_ROWS + lstart) // GRANULE - (cum - ng)
    total = cum[-1]
    tiles_e = (total + TILE_GRANULES - 1) // TILE_GRANULES
    tend = jnp.cumsum(tiles_e)
    tstart = tend - tiles_e
    tile = jnp.arange(n_tiles, dtype=I32)
    te = jnp.minimum(jnp.sum((tile[:, None] >= tend[None, :]).astype(I32), axis=1), MOE_EXPERTS - 1)
    n_used = tend[-1].astype(I32).reshape(1)
    sel = (te[:, None] == jnp.arange(MOE_EXPERTS, dtype=I32)[None, :]).astype(I32)
    pick = lambda table: jnp.sum(sel[:, None, :] * table[None, :, :], axis=2)
    cum_t, first_t = pick(cum), pick(first)
    off = (tile - jnp.sum(sel * tstart[None, :], axis=1))[:, None] * TILE_GRANULES + jnp.arange(TILE_GRANULES, dtype=I32)[None, :]
    valid = (off < jnp.sum(sel * total[None, :], axis=1)[:, None]) & (tile < n_used[0])[:, None]
    j = jnp.sum((cum_t[:, None, :] <= off[:, :, None]).astype(I32), axis=2)
    jsel = (j[:, :, None] == jnp.arange(ntt, dtype=I32)[None, None, :]).astype(I32)
    src = jnp.sum(jsel * first_t[:, None, :], axis=2) + off
    spare = lambda k: ((k // SPARE_GRANULES) * LOCAL_ROWS + LOCAL_USED) // GRANULE + k % SPARE_GRANULES
    assert (ntt - 1) * SPARE_GRANULES >= 2 * TILE_GRANULES
    zero_granule = spare((ntt - 1) * SPARE_GRANULES)
    trash = spare((tile % 2)[:, None] * TILE_GRANULES + jnp.arange(TILE_GRANULES, dtype=I32)[None, :])
    gsrc = jnp.where(valid, src, zero_granule).astype(I32).reshape(-1)
    gdst = jnp.where(valid, src, trash).astype(I32).reshape(-1)
    n_valid = jnp.sum(valid.astype(I32), axis=1).astype(I32)
    return te.astype(I32), n_used, gsrc, gdst, n_valid


def _grouped_kernel(te_ref, nu_ref, gsrc_ref, gdst_ref, nv_ref, xin_ref, wg_ref, wu_ref, wd_ref, xout_ref,
                    xbuf, ybuf, wg16, wu16, wd16, sem_in, sem_out):
    i = pl.program_id(0)
    nu = nu_ref[0]
    tm = MOE_ROW_TILE

    def hbm_rows(table_ref, tile, g):
        return pl.ds(pl.multiple_of(table_ref[tile * TILE_GRANULES + g] * GRANULE, GRANULE), GRANULE)

    def fetch_start(tile, slot):
        for g in range(TILE_GRANULES):
            pltpu.make_async_copy(xin_ref.at[hbm_rows(gsrc_ref, tile, g)],
                                  xbuf.at[slot, pl.ds(g * GRANULE, GRANULE)], sem_in.at[slot]).start()

    def store_start(tile, slot):
        for g in range(TILE_GRANULES):
            pltpu.make_async_copy(ybuf.at[slot, pl.ds(g * GRANULE, GRANULE)],
                                  xout_ref.at[hbm_rows(gdst_ref, tile, g)], sem_out.at[slot]).start()

    def fetch_wait(slot):
        pltpu.make_async_copy(xin_ref.at[pl.ds(0, tm)], xbuf.at[slot], sem_in.at[slot]).wait()

    def store_wait(slot):
        pltpu.make_async_copy(ybuf.at[slot], xout_ref.at[pl.ds(0, tm)], sem_out.at[slot]).wait()

    def step(slot):
        @pl.when(i == 0)
        def _():
            ybuf[...] = jnp.zeros_like(ybuf)
            fetch_start(0, 0)

        @pl.when(i + 1 < nu)
        def _():
            fetch_start(i + 1, 1 - slot)

        @pl.when(i >= 2)
        def _():
            store_wait(slot)

        @pl.when((i == 0) | (te_ref[i] != te_ref[jnp.maximum(i - 1, 0)]))
        def _():
            wg16[...] = wg_ref[0].astype(BF)
            wu16[...] = wu_ref[0].astype(BF)
            wd16[...] = wd_ref[0].astype(BF)

        fetch_wait(slot)
        n_valid = nv_ref[i]
        quarter = TILE_GRANULES // 4
        for part in range(1, 5):
            rows = part * quarter * GRANULE

            @pl.when((n_valid > (part - 1) * quarter) & (n_valid <= part * quarter))
            def _(rows=rows):
                x_lo, x_hi = _unpack_rows(xbuf[slot, 0:rows, :])
                gate = _dot(x_lo, wg16[0:HALF, :]) + _dot(x_hi, wg16[HALF:, :])
                up = _dot(x_lo, wu16[0:HALF, :]) + _dot(x_hi, wu16[HALF:, :])
                ybuf[slot, 0:rows, :] = _pack_rows(_dot((_silu(gate) * up).astype(BF), wd16[...]))

        store_start(i, slot)

        @pl.when(i == nu - 1)
        def _():
            store_wait(slot)

            @pl.when(i >= 1)
            def _():
                store_wait(1 - slot)

    for slot in (0, 1):
        @pl.when((i < nu) & (lax.rem(i, 2) == slot))
        def _(slot=slot):
            step(slot)


def _grouped(x_local, te, n_used, gsrc, gdst, n_valid, wg, wu, wd):
    tm = MOE_ROW_TILE
    n_tiles = te.shape[0]
    used = lambda i, nu: jnp.minimum(i, jnp.maximum(nu[0] - 1, 0))
    wspec = lambda shape: pl.BlockSpec((1,) + shape, lambda i, te, nu, *_: (te[used(i, nu)], 0, 0))
    grid_spec = pltpu.PrefetchScalarGridSpec(
        num_scalar_prefetch=5,
        grid=(n_tiles,),
        in_specs=[pl.BlockSpec(memory_space=pl.ANY), wspec((D_MODEL, MOE_D_FF)), wspec((D_MODEL, MOE_D_FF)),
                  wspec((MOE_D_FF, D_MODEL))],
        out_specs=pl.BlockSpec(memory_space=pl.ANY),
        scratch_shapes=[pltpu.VMEM((2, tm, HALF), U32), pltpu.VMEM((2, tm, HALF), U32),
                        pltpu.VMEM((D_MODEL, MOE_D_FF), BF), pltpu.VMEM((D_MODEL, MOE_D_FF), BF),
                        pltpu.VMEM((MOE_D_FF, D_MODEL), BF),
                        pltpu.SemaphoreType.DMA((2,)), pltpu.SemaphoreType.DMA((2,))],
    )
    return pl.pallas_call(
        _grouped_kernel,
        grid_spec=grid_spec,
        out_shape=jax.ShapeDtypeStruct(x_local.shape, x_local.dtype),
        input_output_aliases={5: 0},
        compiler_params=_params(1),
        name="grouped",
    )(te, n_used, gsrc, gdst, n_valid, x_local, wg, wu, wd)


def _merge_kernel(h_ref, idx_ref, wt_ref, g_ref, yl_ref, o_ref):
    tm = h_ref.shape[0]
    idx = idx_ref[...]
    wt = wt_ref[...]
    r_id = lax.broadcasted_iota(I32, (tm, LOCAL_ROWS), 1)
    sel = jnp.where(r_id == idx[:, 0:1], wt[:, 0:1], 0.0) + jnp.where(r_id == idx[:, 1:2], wt[:, 1:2], 0.0)
    y_lo, y_hi = _unpack_rows(yl_ref[...])
    sel16 = sel.astype(BF)
    moe = jnp.concatenate([_dot(sel16, y_lo), _dot(sel16, y_hi)], axis=1)
    o_ref[...] = _rms(h_ref[...] + moe, g_ref[...])


def _merge(h2d, idx, wt, g, y_local, *, tile_offset):
    t = h2d.shape[0]
    tm = TOKEN_TILE
    return pl.pallas_call(
        _merge_kernel,
        grid=(t // tm,),
        in_specs=[pl.BlockSpec((tm, D_MODEL), lambda i: (i, 0)),
                  pl.BlockSpec((tm, LANES), lambda i: (i + tile_offset, 0)),
                  pl.BlockSpec((tm, LANES), lambda i: (i + tile_offset, 0)),
                  _const_spec(g.shape),
                  pl.BlockSpec((LOCAL_ROWS, HALF), lambda i: (i + tile_offset, 0))],
        out_specs=pl.BlockSpec((tm, D_MODEL), lambda i: (i, 0)),
        out_shape=jax.ShapeDtypeStruct((t, D_MODEL), F32),
        compiler_params=_params(1),
        name="merge",
    )(h2d, idx, wt, g, y_local)


def _perm_matrix(bt, tl):
    m = bt * tl
    p = np.zeros((m, m), np.float32)
    for l in range(tl):
        for b in range(bt):
            p[l * bt + b, b * tl + l] = 1.0
    return p


def _s5_operators(lam_re, lam_im, log_step, b_re, b_im, c_re, c_im):
    lam = lax.complex(lam_re, lam_im)
    step = jnp.exp(log_step)[:, None]
    lam_bar = jnp.exp(lam * step)
    b_bar = ((lam_bar - 1.0) / lam)[..., None] * lax.complex(b_re, b_im)
    gl = LANES // S5_GROUP
    eye = jnp.eye(gl, dtype=F32)

    def in_block(bpart):
        bp = bpart.reshape(S5_SLABS, gl, S5_STATE, S5_GROUP)
        blk = jnp.einsum("sgnk,gh->sgkhn", bp, eye)
        return blk.reshape(S5_SLABS, LANES, gl * S5_STATE)

    def out_block(cpart):
        cp = cpart.reshape(S5_SLABS, gl, S5_GROUP, S5_STATE)
        blk = jnp.einsum("sgkn,gh->sgnhk", cp, eye)
        return blk.reshape(S5_SLABS, gl * S5_STATE, LANES)

    bblk = jnp.concatenate([in_block(jnp.real(b_bar)), in_block(jnp.imag(b_bar))], axis=2).astype(BF)
    cblk = jnp.concatenate([out_block(c_re), out_block(-c_im)], axis=1).astype(BF)
    lre = jnp.broadcast_to(jnp.real(lam_bar).reshape(1, S5_FLAT), (S5_SEQ_TILE, S5_FLAT))
    lim = jnp.broadcast_to(jnp.imag(lam_bar).reshape(1, S5_FLAT), (S5_SEQ_TILE, S5_FLAT))
    return bblk, cblk, lre, lim


def _pad_lanes(v, width=LANES):
    return jnp.pad(v, [(0, 0)] * (v.ndim - 1) + [(0, width - v.shape[-1])])


def kernel(x_prompt, x_sample, state_conv, state_ssd, state_s5_re, state_s5_im, cache_mem_k, cache_mem_v, mem_prompt, norm_mix_g, w_in, conv_w, conv_b, ssd_dt_bias, ssd_a_log, ssd_d, ssd_norm_g, w_ssd_branch, s5_lambda_re, s5_lambda_im, s5_log_step, s5_b_re, s5_b_im, s5_c_re, s5_c_im, s5_d, w_glu, b_glu, w_mix_out, norm_mem_q_g, norm_mem_kv_g, w_mem_q, w_mem_k, w_mem_v, w_mem_o, norm_ffn_g, w_router_group, b_router_group, w_router_expert, b_router_expert, w_exp_gate, w_exp_up, w_exp_down, norm_final_g):
    bp, lp, _ = x_prompt.shape
    bs, ls, _ = x_sample.shape
    tp, ts = bp * lp, bs * ls
    row1 = lambda v: v.reshape(1, -1).astype(F32)

    w_in0 = w_in[0]
    o_xbc = D_MODEL
    o_dt = o_xbc + SSD_CONV_DIM
    o_u5 = o_dt + SSD_HEADS
    w_main = jnp.concatenate([w_in0[:, :o_dt], w_in0[:, o_u5:]], axis=1).astype(BF)
    w_dt = _pad_lanes(w_in0[:, o_dt:o_u5]).astype(BF)
    dt_bias = _pad_lanes(row1(ssd_dt_bias[0]))
    a_log = _pad_lanes(row1(ssd_a_log[0]))
    dskip = jnp.repeat(ssd_d[0].astype(F32), SSD_HEADDIM).reshape(1, D_MODEL)
    eexp_np = np.zeros((LANES, D_MODEL), np.float32)
    for h in range(SSD_HEADS):
        eexp_np[h, h * SSD_HEADDIM:(h + 1) * SSD_HEADDIM] = 1.0
    eexp = jnp.asarray(eexp_np).astype(BF)
    eye16 = jnp.asarray(np.eye(SSD_HEADS, LANES, dtype=np.float32)).astype(BF)
    bblk, cblk, lre, lim = _s5_operators(s5_lambda_re[0], s5_lambda_im[0], s5_log_step[0], s5_b_re[0],
                                         s5_b_im[0], s5_c_re[0], s5_c_im[0])
    s5d = row1(s5_d[0])
    w_kv = jnp.concatenate([w_mem_k[0], w_mem_v[0]], axis=1).astype(BF)
    w_r = _pad_lanes(jnp.concatenate([w_router_group[0], w_router_expert[0]], axis=1).astype(F32))
    wr_h, wr_m, _ = _split3(w_r)
    b_r = _pad_lanes(row1(jnp.concatenate([b_router_group[0], b_router_expert[0]])))
    wg = w_exp_gate[0].reshape(MOE_EXPERTS, D_MODEL, MOE_D_FF)
    wu = w_exp_up[0].reshape(MOE_EXPERTS, D_MODEL, MOE_D_FF)
    wd = w_exp_down[0].reshape(MOE_EXPERTS, MOE_D_FF, D_MODEL)

    def mixer(x2d, batch, seqlen, conv0, h0, s5re0, s5im0):
        z, xbc, u5, ga, gb, dt = _inproj(x2d, row1(norm_mix_g[0]), w_main, w_dt, dt_bias)
        y_ssd, new_conv, new_ssd = _ssd(xbc, dt, conv_w[0].astype(F32), row1(conv_b[0]), a_log, dskip, eexp, eye16,
                                        conv0, h0, batch=batch, seqlen=seqlen)
        if h0 is None:
            tl = S5_TIME_TILE
            u_in = u5.reshape(batch, seqlen, D_MODEL)
        else:
            tl = seqlen
            u_in = u5
        perm = _perm_matrix(S5_SEQ_TILE, tl)
        y5, new_re, new_im = _s5(u_in, jnp.asarray(perm).astype(BF), jnp.asarray(perm.T).astype(BF), bblk, cblk,
                                 lre, lim, s5d, s5re0, s5im0, batch=batch, seqlen=seqlen, tl=tl)
        h1 = _mixout(y_ssd, z, y5.reshape(-1, D_MODEL), ga, gb, x2d, row1(ssd_norm_g[0]),
                     w_ssd_branch[0].astype(BF), w_glu[0].astype(BF), row1(b_glu[0]), w_mix_out[0].astype(BF))
        return h1, new_conv, new_ssd, new_re, new_im

    xp2 = x_prompt.reshape(tp, D_MODEL)
    xs2 = x_sample.reshape(ts, D_MODEL)
    h1p, conv_p, ssd_p, re_p, im_p = mixer(xp2, bp, lp, None, None, None, None)
    h1s, conv_s, ssd_s, re_s, im_s = mixer(xs2, bs, ls, state_conv[0], state_ssd[0],
                                           state_s5_re[0].reshape(bs, S5_FLAT), state_s5_im[0].reshape(bs, S5_FLAT))

    mk_p, mv_p = _memkv(mem_prompt.reshape(bp * MEM_LEN, D_MODEL), row1(norm_mem_kv_g[0]), w_kv)
    gq = row1(norm_mem_q_g[0])
    wq = w_mem_q[0].astype(BF)
    wo = w_mem_o[0].astype(BF)
    h2p = _attn(h1p, gq, wq, wo, mk_p.reshape(bp, MEM_LEN, D_MODEL), mv_p.reshape(bp, MEM_LEN, D_MODEL),
                rows=TOKEN_TILE, nkv=1, rows_per_seq=lp)
    sample_seqs = 8
    h2s = _attn(h1s, gq, wq, wo, cache_mem_k[0], cache_mem_v[0], rows=sample_seqs * ls, nkv=sample_seqs,
                rows_per_seq=ls)

    g_ffn = row1(norm_ffn_g[0])
    x_local, r_idx, r_wt, seg = _route(h2p, h2s, g_ffn, jnp.concatenate([wr_h, wr_m], axis=1), wr_h, b_r)
    token_tiles = (tp + ts) // TOKEN_TILE
    max_granules = (2 * (tp + ts) + token_tiles * MOE_EXPERTS * (GRANULE - 1)) // GRANULE
    n_tiles = max_granules // TILE_GRANULES + MOE_EXPERTS
    te, n_used, gsrc, gdst, n_valid = _granule_plan(seg, n_tiles)
    y_local = _grouped(x_local, te, n_used, gsrc, gdst, n_valid, wg, wu, wd)
    gf = row1(norm_final_g)
    y_prompt = _merge(h2p, r_idx, r_wt, gf, y_local, tile_offset=0)
    y_sample = _merge(h2s, r_idx, r_wt, gf, y_local, tile_offset=tp // TOKEN_TILE)

    return (y_prompt.reshape(bp, lp, D_MODEL), y_sample.reshape(bs, ls, D_MODEL),
            conv_p[None], ssd_p[None],
            re_p.reshape(1, bp, S5_GROUPS, S5_STATE), im_p.reshape(1, bp, S5_GROUPS, S5_STATE),
            mk_p.reshape(1, bp, MEM_LEN, MEM_HEADS, MEM_HEAD_DIM), mv_p.reshape(1, bp, MEM_LEN, MEM_HEADS, MEM_HEAD_DIM),
            conv_s[None], ssd_s[None],
            re_s.reshape(1, bs, S5_GROUPS, S5_STATE), im_s.reshape(1, bs, S5_GROUPS, S5_STATE))
```

```python
import functools
import math

import numpy as np
import jax
import jax.numpy as jnp
from jax import lax
from jax.experimental import pallas as pl
from jax.experimental.pallas import tpu as pltpu

F32 = jnp.float32
BF = jnp.bfloat16
I32 = jnp.int32

D_MODEL = 1024
SSD_HEADS = 16
SSD_HEADDIM = 64
SSD_GROUPS = 2
SSD_STATE = 128
SSD_CONV = 4
SSD_CONV_DIM = 1536
HEADS_PER_GROUP = SSD_HEADS // SSD_GROUPS
S5_GROUPS = 64
S5_GROUP = 16
S5_STATE = 64
S5_FLAT = S5_GROUPS * S5_STATE
S5_SLABS = D_MODEL // 128
MEM_LEN = 256
MEM_HEADS = 4
MEM_HEAD_DIM = 256
MOE_GROUPS = 4
MOE_EXPERTS_PER_GROUP = 8
MOE_EXPERTS = MOE_GROUPS * MOE_EXPERTS_PER_GROUP
MOE_D_FF = 512
NORM_EPS = 1e-6

LANES = 128
VMEM_LIMIT_BYTES = 56 * 1024 * 1024
TOKEN_TILE = 512
MIX_TILE = 1024
SSD_CHUNK = 128
S5_TIME_TILE = 64
MOE_ROW_TILE = 512
NEG_BIG = -1e30


def _params(n_axes):
    return pltpu.CompilerParams(dimension_semantics=("arbitrary",) * n_axes,
                                vmem_limit_bytes=VMEM_LIMIT_BYTES)


def _const_spec(shape):
    nd = len(shape)
    return pl.BlockSpec(shape, lambda *_: (0,) * nd)


def _resident_spec(shape):
    nd = len(shape)
    return pl.BlockSpec(shape, lambda *_: (0,) * nd, pipeline_mode=pl.Buffered(1))


def _dot(a, b):
    return jnp.dot(a, b, preferred_element_type=F32)


def _dot_nt(a, b):
    return lax.dot_general(a, b, (((1,), (1,)), ((), ())), preferred_element_type=F32)


def _split3(v):
    h = v.astype(BF)
    r = v - h.astype(F32)
    m = r.astype(BF)
    l = (r - m.astype(F32)).astype(BF)
    return h, m, l


def _dot_exact(a_bf, v, nt=False):
    f = _dot_nt if nt else _dot
    h, m, l = _split3(v)
    return f(a_bf, h) + f(a_bf, m) + f(a_bf, l)


def _rms(x, g):
    return x * lax.rsqrt(jnp.mean(x * x, axis=-1, keepdims=True) + NORM_EPS) * g


def _sigmoid(x):
    return 0.5 * jnp.tanh(0.5 * x) + 0.5


def _silu(x):
    return x * _sigmoid(x)


def _proj_cols(xb, w_ref, o_ref, c0, width):
    for c in range(0, width, 512):
        ce = min(c + 512, width)
        o_ref[:, c:ce] = _dot(xb, w_ref[:, c0 + c:c0 + ce]).astype(o_ref.dtype)


def _inproj_kernel(x_ref, g_ref, w_ref, wdt_ref, dtb_ref, z_ref, xbc_ref, u5_ref, ga_ref, gb_ref, dt_ref):
    xb = _rms(x_ref[...], g_ref[...]).astype(BF)
    c0 = 0
    for o_ref in (z_ref, xbc_ref, u5_ref, ga_ref, gb_ref):
        width = o_ref.shape[1]
        _proj_cols(xb, w_ref, o_ref, c0, width)
        c0 += width
    raw = _dot(xb, wdt_ref[...]) + dtb_ref[...]
    dt_ref[...] = jnp.maximum(raw, 0.0) + jnp.log1p(jnp.exp(-jnp.abs(raw)))


def _inproj(x2d, g, w_main, w_dt, dt_bias):
    t = x2d.shape[0]
    tm = MIX_TILE
    widths = (D_MODEL, SSD_CONV_DIM, D_MODEL, D_MODEL, D_MODEL)
    row = lambda w: pl.BlockSpec((tm, w), lambda i: (i, 0))
    return pl.pallas_call(
        _inproj_kernel,
        grid=(t // tm,),
        in_specs=[row(D_MODEL), _const_spec(g.shape), _resident_spec(w_main.shape), _resident_spec(w_dt.shape),
                  _const_spec(dt_bias.shape)],
        out_specs=[row(w) for w in widths] + [row(LANES)],
        out_shape=[jax.ShapeDtypeStruct((t, w), BF) for w in widths] + [jax.ShapeDtypeStruct((t, LANES), F32)],
        compiler_params=_params(1),
        name="inproj",
    )(x2d, g, w_main, w_dt, dt_bias)


def _memkv_kernel(x_ref, g_ref, w_ref, k_ref, v_ref):
    xb = _rms(x_ref[...], g_ref[...]).astype(BF)
    _proj_cols(xb, w_ref, k_ref, 0, D_MODEL)
    _proj_cols(xb, w_ref, v_ref, D_MODEL, D_MODEL)


def _memkv(mem2d, g, w_kv):
    t = mem2d.shape[0]
    tm = TOKEN_TILE
    row = pl.BlockSpec((tm, D_MODEL), lambda i: (i, 0))
    return pl.pallas_call(
        _memkv_kernel,
        grid=(t // tm,),
        in_specs=[row, _const_spec(g.shape), _const_spec(w_kv.shape)],
        out_specs=[row, row],
        out_shape=[jax.ShapeDtypeStruct((t, D_MODEL), F32)] * 2,
        compiler_params=_params(1),
        name="memkv",
    )(mem2d, g, w_kv)


def _ssd_kernel(*refs, lq, has_h0):
    q = SSD_CHUNK
    nseq = q // lq
    if has_h0:
        (xbc_ref, dt_ref, cw_ref, cb_ref, alog_ref, dsk_ref, eexp_ref, eye_ref, shift_ref, conv0_ref, h0_ref,
         y_ref, convo_ref, ho_ref, hbuf, cacc, yacc, yint, xwt, tot_s) = refs
        hin_ref = h0_ref
        hbuf[...] = jnp.zeros_like(hbuf)
        hbuf[:, 0:SSD_CONV - 1, :] = conv0_ref[...]
    else:
        (xbc_ref, dt_ref, cw_ref, cb_ref, alog_ref, dsk_ref, eexp_ref, eye_ref, shift_ref,
         y_ref, convo_ref, ho_ref, hbuf, cacc, yacc, yint, xwt, tot_s) = refs
        hin_ref = ho_ref

        @pl.when(pl.program_id(1) == 0)
        def _():
            ho_ref[...] = jnp.zeros_like(ho_ref)
            hbuf[...] = jnp.zeros_like(hbuf)

    x16 = xbc_ref[...]
    xraw = x16.astype(F32)
    acc = cb_ref[...] + xraw * cw_ref[SSD_CONV - 1:SSD_CONV, :]
    for k in range(SSD_CONV - 1):
        acc = acc + _dot(shift_ref[k], x16) * cw_ref[k:k + 1, :]
    cacc[...] = acc
    for i in range(nseq):
        tail = xraw[(i + 1) * lq - (SSD_CONV - 1):(i + 1) * lq]
        convo_ref[i] = tail
        corr = hbuf[i, 0:8, :] * cw_ref[0:1, :]
        for k in range(1, SSD_CONV - 1):
            corr = corr + hbuf[i, k:k + 8, :] * cw_ref[k:k + 1, :]
        cacc[i * lq:i * lq + 8, :] += corr
        if not has_h0:
            hbuf[i, 0:SSD_CONV - 1, :] = tail
    xc = _silu(cacc[...])
    xs = xc[:, :D_MODEL]
    xs_bf = xs.astype(BF)
    bm_bf = xc[:, D_MODEL:D_MODEL + SSD_GROUPS * SSD_STATE].astype(BF)
    cm = xc[:, D_MODEL + SSD_GROUPS * SSD_STATE:]
    cm_bf = cm.astype(BF)

    dt = dt_ref[...]
    da = dt * (-jnp.exp(alog_ref[...]))
    ri = lax.broadcasted_iota(I32, (q, q), 0)
    ci = lax.broadcasted_iota(I32, (q, q), 1)
    if nseq == 1:
        causal = ci <= ri
    else:
        sh = int(math.log2(lq))
        same = lax.shift_right_logical(ri, sh) == lax.shift_right_logical(ci, sh)
        causal = same & (ci <= ri)
    lmat = jnp.where(causal, 1.0, 0.0).astype(BF)
    cs = _dot_exact(lmat, da)
    if nseq == 1:
        tot = jnp.broadcast_to(cs[q - 1:q, :], (q, LANES))
    else:
        tot = _dot_exact(jnp.where(same, 1.0, 0.0).astype(BF), da)
    tot_s[...] = tot
    eye = eye_ref[...]
    cs_t = _dot_exact(eye, cs, nt=True)
    dt_t = _dot_exact(eye, dt, nt=True)
    eexp = eexp_ref[...]
    ecs_h, ecs_m, _ = _split3(jnp.exp(cs))
    ecs_x = _dot(ecs_h, eexp) + _dot(ecs_m, eexp)
    wend_x = _dot((jnp.exp(tot - cs) * dt).astype(BF), eexp)

    lane = lax.broadcasted_iota(I32, (q, LANES), 1)
    for g in range(SSD_GROUPS):
        cbg = _dot_nt(cm_bf[:, g * SSD_STATE:(g + 1) * SSD_STATE], bm_bf[:, g * SSD_STATE:(g + 1) * SSD_STATE])
        for jp in range(HEADS_PER_GROUP // 2):
            j = g * (HEADS_PER_GROUP // 2) + jp
            ms = []
            for h in (2 * j, 2 * j + 1):
                col = jnp.broadcast_to(cs[:, h:h + 1], (q, q))
                row = jnp.broadcast_to(cs_t[h:h + 1, :], (q, q))
                dtr = jnp.broadcast_to(dt_t[h:h + 1, :], (q, q))
                ms.append((jnp.where(causal, jnp.exp(col - row), 0.0) * cbg * dtr).astype(BF))
            mp = jnp.concatenate(ms, axis=1)
            xp = xs_bf[:, j * LANES:(j + 1) * LANES]
            zero = jnp.zeros_like(xp)
            x2 = jnp.concatenate([jnp.where(lane < SSD_HEADDIM, xp, zero),
                                  jnp.where(lane >= SSD_HEADDIM, xp, zero)], axis=0)
            yacc[:, j * LANES:(j + 1) * LANES] = _dot(mp, x2)

    gw = HEADS_PER_GROUP * SSD_HEADDIM
    xw = xs * wend_x
    for g in range(SSD_GROUPS):
        xwt[g * gw:(g + 1) * gw, :] = xw[:, g * gw:(g + 1) * gw].T.astype(BF)

    if nseq == 1:
        for g in range(SSD_GROUPS):
            hin_g = hin_ref[0, g * HEADS_PER_GROUP:(g + 1) * HEADS_PER_GROUP].reshape(gw, SSD_STATE)
            yint[:, g * gw:(g + 1) * gw] = _dot_nt(cm_bf[:, g * SSD_STATE:(g + 1) * SSD_STATE], hin_g.astype(BF))
            s_new = _dot(xwt[g * gw:(g + 1) * gw, :], bm_bf[:, g * SSD_STATE:(g + 1) * SSD_STATE])
            for hl in range(HEADS_PER_GROUP):
                h = g * HEADS_PER_GROUP + hl
                dec = jnp.exp(jnp.broadcast_to(tot[0:1, h:h + 1], (SSD_HEADDIM, SSD_STATE)))
                ho_ref[0, h] = dec * hin_ref[0, h] + s_new[hl * SSD_HEADDIM:(hl + 1) * SSD_HEADDIM]
    else:
        yint[...] = jnp.zeros_like(yint)
        sh = int(math.log2(lq))
        rowseq = lax.shift_right_logical(lax.broadcasted_iota(I32, (q, SSD_STATE), 0), sh)
        colseq = lax.shift_right_logical(lax.broadcasted_iota(I32, (gw, q), 1), sh)

        def seq_body(i, carry):
            trow = tot_s[pl.ds(i * lq, 1), :]
            for g in range(SSD_GROUPS):
                hin_g = hin_ref[i, g * HEADS_PER_GROUP:(g + 1) * HEADS_PER_GROUP].reshape(gw, SSD_STATE)
                cmg = cm[:, g * SSD_STATE:(g + 1) * SSD_STATE]
                lhs = jnp.where(rowseq == i, cmg, 0.0).astype(BF)
                yint[:, g * gw:(g + 1) * gw] += _dot_nt(lhs, hin_g.astype(BF))
                xg = xwt[g * gw:(g + 1) * gw, :]
                xi = jnp.where(colseq == i, xg, jnp.zeros_like(xg))
                s_new = _dot(xi, bm_bf[:, g * SSD_STATE:(g + 1) * SSD_STATE])
                for hl in range(HEADS_PER_GROUP):
                    h = g * HEADS_PER_GROUP + hl
                    dec = jnp.exp(jnp.broadcast_to(trow[:, h:h + 1], (SSD_HEADDIM, SSD_STATE)))
                    ho_ref[i, h] = dec * hin_ref[i, h] + s_new[hl * SSD_HEADDIM:(hl + 1) * SSD_HEADDIM]
            return carry

        lax.fori_loop(0, nseq, seq_body, 0)

    y_ref[...] = (yacc[...] + yint[...] * ecs_x + dsk_ref[...] * xs).astype(y_ref.dtype)


def _ssd(xbc, dt, conv_w, conv_b, a_log, dskip, eexp, eye, conv0, h0, *, batch, seqlen):
    q = SSD_CHUNK
    t = xbc.shape[0]
    has_h0 = h0 is not None
    if has_h0:
        lq = seqlen
        nseq = q // lq
        grid = (t // q,)
        rmap = lambda i: (i, 0)
        smap3 = lambda i: (i, 0, 0)
        smap4 = lambda i: (i, 0, 0, 0)
    else:
        lq = q
        nseq = 1
        nc = seqlen // q
        grid = (batch, nc)
        rmap = lambda b, c: (b * nc + c, 0)
        smap3 = lambda b, c: (b, 0, 0)
        smap4 = lambda b, c: (b, 0, 0, 0)
    shifts = np.zeros((SSD_CONV - 1, q, q), np.float32)
    for k in range(SSD_CONV - 1):
        for r in range(q):
            if r % lq + k - (SSD_CONV - 1) >= 0:
                shifts[k, r, r + k - (SSD_CONV - 1)] = 1.0
    consts = [conv_w, conv_b, a_log, dskip, eexp, eye, jnp.asarray(shifts).astype(BF)]
    in_specs = [pl.BlockSpec((q, SSD_CONV_DIM), rmap), pl.BlockSpec((q, LANES), rmap)]
    in_specs += [_const_spec(c.shape) for c in consts]
    args = [xbc, dt] + consts
    if has_h0:
        in_specs += [pl.BlockSpec((nseq, SSD_CONV - 1, SSD_CONV_DIM), smap3),
                     pl.BlockSpec((nseq, SSD_HEADS, SSD_HEADDIM, SSD_STATE), smap4)]
        args += [conv0, h0]
    return pl.pallas_call(
        functools.partial(_ssd_kernel, lq=lq, has_h0=has_h0),
        grid=grid,
        in_specs=in_specs,
        out_specs=[pl.BlockSpec((q, D_MODEL), rmap),
                   pl.BlockSpec((nseq, SSD_CONV - 1, SSD_CONV_DIM), smap3),
                   pl.BlockSpec((nseq, SSD_HEADS, SSD_HEADDIM, SSD_STATE), smap4)],
        out_shape=[jax.ShapeDtypeStruct((t, D_MODEL), BF),
                   jax.ShapeDtypeStruct((batch, SSD_CONV - 1, SSD_CONV_DIM), F32),
                   jax.ShapeDtypeStruct((batch, SSD_HEADS, SSD_HEADDIM, SSD_STATE), F32)],
        scratch_shapes=[pltpu.VMEM((nseq, 16, SSD_CONV_DIM), F32),
                        pltpu.VMEM((q, SSD_CONV_DIM), F32),
                        pltpu.VMEM((q, D_MODEL), F32),
                        pltpu.VMEM((q, D_MODEL), F32),
                        pltpu.VMEM((D_MODEL, q), BF),
                        pltpu.VMEM((q, LANES), F32)],
        compiler_params=_params(len(grid)),
        name="ssd_sample" if has_h0 else "ssd_prompt",
    )(*args)


S5_SEQ_TILE = 8


def _s5_kernel(*refs, tl, has_h0):
    bt = S5_SEQ_TILE
    m = bt * tl
    if has_h0:
        (u_ref, perm_ref, permt_ref, bblk_ref, cblk_ref, lre_ref, lim_ref, d_ref, h0re_ref, h0im_ref,
         y_ref, ore_ref, oim_ref, bre, bim, sre, sim) = refs
    else:
        (u_ref, perm_ref, permt_ref, bblk_ref, cblk_ref, lre_ref, lim_ref, d_ref,
         y_ref, ore_ref, oim_ref, bre, bim, sre, sim) = refs

    @pl.when(pl.program_id(1) == 0)
    def _():
        if has_h0:
            sre[...] = h0re_ref[...]
            sim[...] = h0im_ref[...]
        else:
            sre[...] = jnp.zeros_like(sre)
            sim[...] = jnp.zeros_like(sim)

    u = u_ref[...].reshape(m, D_MODEL)
    u_tm = _dot(perm_ref[...], u).astype(BF)
    half = S5_FLAT // S5_SLABS
    ys = []
    for j in range(S5_SLABS):
        sl = slice(j * half, (j + 1) * half)
        r = _dot(u_tm[:, j * LANES:(j + 1) * LANES], bblk_ref[j])
        bre[:, sl] = r[:, :half]
        bim[:, sl] = r[:, half:]
        lr = lre_ref[:, sl]
        li = lim_ref[:, sl]
        sr = sre[:, sl]
        si = sim[:, sl]
        for l in range(tl):
            rows = slice(l * bt, (l + 1) * bt)
            sr, si = lr * sr - li * si + bre[rows, sl], lr * si + li * sr + bim[rows, sl]
            bre[rows, sl] = sr
            bim[rows, sl] = si
        sre[:, sl] = sr
        sim[:, sl] = si
        st = jnp.concatenate([bre[:, sl], bim[:, sl]], axis=1)
        ys.append(_dot(st.astype(BF), cblk_ref[j]))
    y_tm = jnp.concatenate(ys, axis=1).astype(BF)
    y_bm = _dot(permt_ref[...], y_tm) + d_ref[...] * u.astype(F32)
    y_ref[...] = y_bm.astype(y_ref.dtype).reshape(y_ref.shape)
    ore_ref[...] = sre[...]
    oim_ref[...] = sim[...]


def _s5(u, perm, permt, bblk, cblk, lre, lim, dvec, h0re, h0im, *, batch, seqlen, tl):
    bt = S5_SEQ_TILE
    m = bt * tl
    has_h0 = h0re is not None
    nb = batch // bt
    nt = seqlen // tl
    if u.ndim == 3:
        u_spec = pl.BlockSpec((bt, tl, D_MODEL), lambda b, t: (b, t, 0))
    else:
        u_spec = pl.BlockSpec((m, D_MODEL), lambda b, t: (b * nt + t, 0))
    st_spec = pl.BlockSpec((bt, S5_FLAT), lambda b, t: (b, 0))
    consts = [perm, permt, bblk, cblk, lre, lim, dvec]
    in_specs = [u_spec] + [_const_spec(c.shape) for c in consts]
    args = [u] + consts
    if has_h0:
        in_specs += [st_spec, st_spec]
        args += [h0re, h0im]
    return pl.pallas_call(
        functools.partial(_s5_kernel, tl=tl, has_h0=has_h0),
        grid=(nb, nt),
        in_specs=in_specs,
        out_specs=[u_spec, st_spec, st_spec],
        out_shape=[jax.ShapeDtypeStruct(u.shape, BF),
                   jax.ShapeDtypeStruct((batch, S5_FLAT), F32),
                   jax.ShapeDtypeStruct((batch, S5_FLAT), F32)],
        scratch_shapes=[pltpu.VMEM((m, S5_FLAT), F32), pltpu.VMEM((m, S5_FLAT), F32),
                        pltpu.VMEM((bt, S5_FLAT), F32), pltpu.VMEM((bt, S5_FLAT), F32)],
        compiler_params=_params(2),
        name="s5_sample" if has_h0 else "s5_prompt",
    )(*args)


def _mixout_kernel(y_ref, z_ref, y5_ref, ga_ref, gb_ref, x_ref, ng_ref, wa_ref, wglu_ref, bglu_ref, wmix_ref, o_ref):
    y = y_ref[...].astype(F32) * _silu(z_ref[...].astype(F32))
    yn = _rms(y, ng_ref[...]).astype(BF)
    branch_a = _dot(yn, wa_ref[...])
    glu = _dot(y5_ref[...], wglu_ref[...]) + bglu_ref[...]
    branch_b = glu[:, :D_MODEL] * _sigmoid(glu[:, D_MODEL:])
    merged = _sigmoid(ga_ref[...].astype(F32)) * branch_a + _sigmoid(gb_ref[...].astype(F32)) * branch_b
    o_ref[...] = x_ref[...] + _dot(merged.astype(BF), wmix_ref[...])


def _mixout(y, z, y5, ga, gb, x2d, ng, wa, wglu, bglu, wmix):
    t = x2d.shape[0]
    tm = MIX_TILE
    row = pl.BlockSpec((tm, D_MODEL), lambda i: (i, 0))
    consts = [ng, wa, wglu, bglu, wmix]
    return pl.pallas_call(
        _mixout_kernel,
        grid=(t // tm,),
        in_specs=[row] * 6 + [_resident_spec(c.shape) for c in consts],
        out_specs=row,
        out_shape=jax.ShapeDtypeStruct((t, D_MODEL), F32),
        compiler_params=_params(1),
        name="mixout",
    )(y, z, y5, ga, gb, x2d, *consts)


def _attn_kernel(h_ref, g_ref, wq_ref, wo_ref, k_ref, v_ref, o_ref, obuf, *, nkv, rows_per_seq):
    h = h_ref[...]
    r = h.shape[0]
    q32 = _dot(_rms(h, g_ref[...]).astype(BF), wq_ref[...])
    scale = MEM_HEAD_DIM ** -0.5
    if len(k_ref.shape) == 4:
        rq = rows_per_seq
        nrow = MEM_HEADS * rq
        row_head = lax.shift_right_logical(lax.broadcasted_iota(I32, (nrow, MEM_LEN * MEM_HEADS), 0), int(math.log2(rq)))
        col_head = lax.broadcasted_iota(I32, (nrow, MEM_LEN * MEM_HEADS), 1) & (MEM_HEADS - 1)
        visible = row_head == col_head
        for i in range(nkv):
            qs = jnp.concatenate([q32[i * rq:(i + 1) * rq, hd * MEM_HEAD_DIM:(hd + 1) * MEM_HEAD_DIM]
                                  for hd in range(MEM_HEADS)], axis=0).astype(BF)
            kf = k_ref[i].reshape(MEM_LEN * MEM_HEADS, MEM_HEAD_DIM).astype(BF)
            vf = v_ref[i].reshape(MEM_LEN * MEM_HEADS, MEM_HEAD_DIM).astype(BF)
            s = jnp.where(visible, _dot_nt(qs, kf) * scale, NEG_BIG)
            p = jnp.exp(s - jnp.max(s, axis=-1, keepdims=True))
            p = p / jnp.sum(p, axis=-1, keepdims=True)
            o = _dot(p.astype(BF), vf)
            for hd in range(MEM_HEADS):
                obuf[i * rq:(i + 1) * rq, hd * MEM_HEAD_DIM:(hd + 1) * MEM_HEAD_DIM] = o[hd * rq:(hd + 1) * rq]
    else:
        qv = q32.astype(BF)
        for hd in range(MEM_HEADS):
            sl = slice(hd * MEM_HEAD_DIM, (hd + 1) * MEM_HEAD_DIM)
            s = _dot_nt(qv[:, sl], k_ref[0, :, sl].astype(BF)) * scale
            p = jnp.exp(s - jnp.max(s, axis=-1, keepdims=True))
            p = p / jnp.sum(p, axis=-1, keepdims=True)
            obuf[:, sl] = _dot(p.astype(BF), v_ref[0, :, sl].astype(BF))
    o_ref[...] = h + _dot(obuf[...].astype(BF), wo_ref[...])


def _attn(h2d, g, wq, wo, k3, v3, *, rows, nkv, rows_per_seq):
    t = h2d.shape[0]
    nsteps = t // rows
    steps_per_kv = nsteps // (k3.shape[0] // nkv)
    row = pl.BlockSpec((rows, D_MODEL), lambda i: (i, 0))
    if k3.ndim == 4:
        kv = pl.BlockSpec((nkv, MEM_LEN, MEM_HEADS, MEM_HEAD_DIM), lambda i: (i // steps_per_kv, 0, 0, 0))
    else:
        kv = pl.BlockSpec((nkv, MEM_LEN, D_MODEL), lambda i: (i // steps_per_kv, 0, 0))
    consts = [g, wq, wo]
    return pl.pallas_call(
        functools.partial(_attn_kernel, nkv=nkv, rows_per_seq=rows_per_seq),
        grid=(nsteps,),
        in_specs=[row] + [_const_spec(c.shape) for c in consts] + [kv, kv],
        out_specs=row,
        out_shape=jax.ShapeDtypeStruct((t, D_MODEL), F32),
        scratch_shapes=[pltpu.VMEM((rows, D_MODEL), F32)],
        compiler_params=_params(1),
        name="attn",
    )(h2d, *consts, k3, v3)


ROUTER_LANE0 = MOE_GROUPS

U32 = jnp.uint32
HALF = D_MODEL // 2


def _pack_rows(x):
    xr = x.astype(BF).astype(F32)
    lo = lax.shift_right_logical(pltpu.bitcast(xr[:, :HALF], U32), jnp.uint32(16))
    hi = pltpu.bitcast(xr[:, HALF:], U32) & jnp.uint32(0xFFFF0000)
    return hi | lo


def _unpack_rows(w):
    lo = pltpu.bitcast(lax.shift_left(w, jnp.uint32(16)), F32).astype(BF)
    hi = pltpu.bitcast(w & jnp.uint32(0xFFFF0000), F32).astype(BF)
    return lo, hi


GRANULE = 8
LOCAL_USED = 2 * TOKEN_TILE + MOE_EXPERTS * (GRANULE - 1)
LOCAL_ROWS = 1280
SPARE_GRANULES = (LOCAL_ROWS - LOCAL_USED) // GRANULE
TILE_GRANULES = MOE_ROW_TILE // GRANULE


def _route_kernel(hp_ref, hs_ref, g_ref, wcat_ref, wh_ref, b_ref, tri_ref, eye_ref, upper_ref,
                  xl_ref, idx_ref, wt_ref, seg_ref, *, np_steps):
    i = pl.program_id(0)
    h = jnp.where(i < np_steps, hp_ref[...], hs_ref[...])
    xn = _rms(h, g_ref[...])
    tm = xn.shape[0]
    xh = xn.astype(BF)
    xm = (xn - xh.astype(F32)).astype(BF)
    both = _dot(xh, wcat_ref[...])
    logits = both[:, :LANES] + both[:, LANES:] + _dot(xm, wh_ref[...]) + b_ref[...]
    lane = lax.broadcasted_iota(I32, logits.shape, 1)
    lane_f = lane.astype(F32)
    no_lane = float(LANES)
    is_g = lane < MOE_GROUPS
    lg = jnp.where(is_g, logits, NEG_BIG)
    mg = jnp.max(lg, axis=-1, keepdims=True)
    gidx = jnp.min(jnp.where(lg == mg, lane_f, no_lane), axis=-1, keepdims=True).astype(I32)
    g_w = 1.0 / jnp.sum(jnp.where(is_g, jnp.exp(lg - mg), 0.0), axis=-1, keepdims=True)
    e_lane = lane - ROUTER_LANE0
    in_grp = ((e_lane >= 0) & (e_lane < MOE_EXPERTS)
              & (lax.shift_right_logical(jnp.maximum(e_lane, 0), 3) == gidx))
    le = jnp.where(in_grp, logits, NEG_BIG)
    m1 = jnp.max(le, axis=-1, keepdims=True)
    i1 = jnp.min(jnp.where(le == m1, lane_f, no_lane), axis=-1, keepdims=True)
    le2 = jnp.where(lane_f == i1, NEG_BIG, le)
    m2 = jnp.max(le2, axis=-1, keepdims=True)
    i2 = jnp.min(jnp.where(le2 == m2, lane_f, no_lane), axis=-1, keepdims=True)
    ratio = jnp.exp(m2 - m1)
    w1 = g_w / (1.0 + ratio)
    w2 = g_w * ratio / (1.0 + ratio)
    wt_ref[...] = jnp.where(lane == 0, w1, jnp.where(lane == 1, w2, 0.0))

    oh1 = jnp.where(lane_f == i1, 1.0, 0.0)
    oh2 = jnp.where(lane_f == i2, 1.0, 0.0)
    tot1 = jnp.sum(oh1, axis=0, keepdims=True)
    cnt = (tot1 + jnp.sum(oh2, axis=0, keepdims=True)).astype(I32)
    padded = lax.shift_left(lax.shift_right_logical(cnt + (GRANULE - 1), 3), 3)
    pad8 = jnp.broadcast_to(padded.astype(F32), (8, LANES)).astype(BF)
    lstart = _dot(pad8, upper_ref[...])[0:1, :]
    tri = tri_ref[...]
    l1 = jnp.sum(oh1 * (_dot(tri, oh1.astype(BF)) + lstart), axis=-1, keepdims=True)
    l2 = jnp.sum(oh2 * (_dot(tri, oh2.astype(BF)) + (lstart + tot1)), axis=-1, keepdims=True)
    idx_ref[...] = jnp.where(lane == 0, l1.astype(I32), jnp.where(lane == 1, l2.astype(I32), 0))
    sub = lax.broadcasted_iota(I32, (8, LANES), 0)
    seg_ref[0] = jnp.where(sub == 0, jnp.broadcast_to(cnt, (8, LANES)),
                           jnp.where(sub == 1, jnp.broadcast_to(lstart.astype(I32), (8, LANES)), 0))

    cols = jnp.where(lane == 0, l1, jnp.where(lane == 1, l2, 0.0))
    rows = _dot_exact(eye_ref[...], cols, nt=True)
    r_id = lax.broadcasted_iota(I32, (LOCAL_ROWS, tm), 0).astype(F32)
    perm = jnp.where(r_id == rows[0:1, :], 1.0, 0.0) + jnp.where(r_id == rows[1:2, :], 1.0, 0.0)
    xl_ref[...] = _pack_rows(_dot(perm.astype(BF), xh))


def _route(hp, hs, g, wcat, wh, b):
    tm = TOKEN_TILE
    np_steps = hp.shape[0] // tm
    ns_steps = hs.shape[0] // tm
    nt = np_steps + ns_steps
    tri = jnp.asarray(np.tril(np.ones((tm, tm), np.float32), -1)).astype(BF)
    eye8 = jnp.asarray(np.eye(8, LANES, dtype=np.float32)).astype(BF)
    upper = jnp.asarray(np.triu(np.ones((LANES, LANES), np.float32), 1)).astype(BF)
    consts = [g, wcat, wh, b, tri, eye8, upper]
    return pl.pallas_call(
        functools.partial(_route_kernel, np_steps=np_steps),
        grid=(nt,),
        in_specs=[pl.BlockSpec((tm, D_MODEL), lambda i: (jnp.minimum(i, np_steps - 1), 0)),
                  pl.BlockSpec((tm, D_MODEL), lambda i: (jnp.maximum(i - np_steps, 0), 0))]
                 + [_const_spec(c.shape) for c in consts],
        out_specs=[pl.BlockSpec((LOCAL_ROWS, HALF), lambda i: (i, 0)),
                   pl.BlockSpec((tm, LANES), lambda i: (i, 0)),
                   pl.BlockSpec((tm, LANES), lambda i: (i, 0)),
                   pl.BlockSpec((1, 8, LANES), lambda i: (i, 0, 0))],
        out_shape=[jax.ShapeDtypeStruct((nt * LOCAL_ROWS, HALF), U32),
                   jax.ShapeDtypeStruct((nt * tm, LANES), I32),
                   jax.ShapeDtypeStruct((nt * tm, LANES), F32),
                   jax.ShapeDtypeStruct((nt, 8, LANES), I32)],
        compiler_params=_params(1),
        name="route",
    )(hp, hs, *consts)


def _granule_plan(seg, n_tiles):
    cnt = seg[:, 0, ROUTER_LANE0:ROUTER_LANE0 + MOE_EXPERTS]
    lstart = seg[:, 1, ROUTER_LANE0:ROUTER_LANE0 + MOE_EXPERTS]
    ntt = cnt.shape[0]
    ng = (cnt + GRANULE - 1) // GRANULE
    cum = jnp.cumsum(ng, axis=0)
    first = (jnp.arange(ntt, dtype=I32)[:, None] * LOCAL_ROWS + lstart) // GRANULE - (cum - ng)
    total = cum[-1]
    tiles_e = (total + TILE_GRANULES - 1) // TILE_GRANULES
    tend = jnp.cumsum(tiles_e)
    tstart = tend - tiles_e
    tile = jnp.arange(n_tiles, dtype=I32)
    te = jnp.minimum(jnp.sum((tile[:, None] >= tend[None, :]).astype(I32), axis=1), MOE_EXPERTS - 1)
    n_used = tend[-1].astype(I32).reshape(1)
    sel = (te[:, None] == jnp.arange(MOE_EXPERTS, dtype=I32)[None, :]).astype(I32)
    pick = lambda table: jnp.sum(sel[:, None, :] * table[None, :, :], axis=2)
    cum_t, first_t = pick(cum), pick(first)
    off = (tile - jnp.sum(sel * tstart[None, :], axis=1))[:, None] * TILE_GRANULES + jnp.arange(TILE_GRANULES, dtype=I32)[None, :]
    valid = (off < jnp.sum(sel * total[None, :], axis=1)[:, None]) & (tile < n_used[0])[:, None]
    j = jnp.sum((cum_t[:, None, :] <= off[:, :, None]).astype(I32), axis=2)
    jsel = (j[:, :, None] == jnp.arange(ntt, dtype=I32)[None, None, :]).astype(I32)
    src = jnp.sum(jsel * first_t[:, None, :], axis=2) + off
    spare = lambda k: ((k // SPARE_GRANULES) * LOCAL_ROWS + LOCAL_USED) // GRANULE + k % SPARE_GRANULES
    assert (ntt - 1) * SPARE_GRANULES >= 2 * TILE_GRANULES
    zero_granule = spare((ntt - 1) * SPARE_GRANULES)
    trash = spare((tile % 2)[:, None] * TILE_GRANULES + jnp.arange(TILE_GRANULES, dtype=I32)[None, :])
    gsrc = jnp.where(valid, src, zero_granule).astype(I32).reshape(-1)
    gdst = jnp.where(valid, src, trash).astype(I32).reshape(-1)
    n_valid = jnp.sum(valid.astype(I32), axis=1).astype(I32)
    return te.astype(I32), n_used, gsrc, gdst, n_valid


def _grouped_kernel(te_ref, nu_ref, gsrc_ref, gdst_ref, nv_ref, xin_ref, wg_ref, wu_ref, wd_ref, xout_ref,
                    xbuf, ybuf, wg16, wu16, wd16, sem_in, sem_out):
    i = pl.program_id(0)
    nu = nu_ref[0]
    tm = MOE_ROW_TILE

    def hbm_rows(table_ref, tile, g):
        return pl.ds(pl.multiple_of(table_ref[tile * TILE_GRANULES + g] * GRANULE, GRANULE), GRANULE)

    def fetch_start(tile, slot):
        for g in range(TILE_GRANULES):
            pltpu.make_async_copy(xin_ref.at[hbm_rows(gsrc_ref, tile, g)],
                                  xbuf.at[slot, pl.ds(g * GRANULE, GRANULE)], sem_in.at[slot]).start()

    def store_start(tile, slot):
        for g in range(TILE_GRANULES):
            pltpu.make_async_copy(ybuf.at[slot, pl.ds(g * GRANULE, GRANULE)],
                                  xout_ref.at[hbm_rows(gdst_ref, tile, g)], sem_out.at[slot]).start()

    def fetch_wait(slot):
        pltpu.make_async_copy(xin_ref.at[pl.ds(0, tm)], xbuf.at[slot], sem_in.at[slot]).wait()

    def store_wait(slot):
        pltpu.make_async_copy(ybuf.at[slot], xout_ref.at[pl.ds(0, tm)], sem_out.at[slot]).wait()

    def step(slot):
        @pl.when(i == 0)
        def _():
            ybuf[...] = jnp.zeros_like(ybuf)
            fetch_start(0, 0)

        @pl.when(i + 1 < nu)
        def _():
            fetch_start(i + 1, 1 - slot)

        @pl.when(i >= 2)
        def _():
            store_wait(slot)

        @pl.when((i == 0) | (te_ref[i] != te_ref[jnp.maximum(i - 1, 0)]))
        def _():
            wg16[...] = wg_ref[0].astype(BF)
            wu16[...] = wu_ref[0].astype(BF)
            wd16[...] = wd_ref[0].astype(BF)

        fetch_wait(slot)
        n_valid = nv_ref[i]
        quarter = TILE_GRANULES // 4
        for part in range(1, 5):
            rows = part * quarter * GRANULE

            @pl.when((n_valid > (part - 1) * quarter) & (n_valid <= part * quarter))
            def _(rows=rows):
                x_lo, x_hi = _unpack_rows(xbuf[slot, 0:rows, :])
                gate = _dot(x_lo, wg16[0:HALF, :]) + _dot(x_hi, wg16[HALF:, :])
                up = _dot(x_lo, wu16[0:HALF, :]) + _dot(x_hi, wu16[HALF:, :])
                ybuf[slot, 0:rows, :] = _pack_rows(_dot((_silu(gate) * up).astype(BF), wd16[...]))

        store_start(i, slot)

        @pl.when(i == nu - 1)
        def _():
            store_wait(slot)

            @pl.when(i >= 1)
            def _():
                store_wait(1 - slot)

    for slot in (0, 1):
        @pl.when((i < nu) & (lax.rem(i, 2) == slot))
        def _(slot=slot):
            step(slot)


def _grouped(x_local, te, n_used, gsrc, gdst, n_valid, wg, wu, wd):
    tm = MOE_ROW_TILE
    n_tiles = te.shape[0]
    used = lambda i, nu: jnp.minimum(i, jnp.maximum(nu[0] - 1, 0))
    wspec = lambda shape: pl.BlockSpec((1,) + shape, lambda i, te, nu, *_: (te[used(i, nu)], 0, 0))
    grid_spec = pltpu.PrefetchScalarGridSpec(
        num_scalar_prefetch=5,
        grid=(n_tiles,),
        in_specs=[pl.BlockSpec(memory_space=pl.ANY), wspec((D_MODEL, MOE_D_FF)), wspec((D_MODEL, MOE_D_FF)),
                  wspec((MOE_D_FF, D_MODEL))],
        out_specs=pl.BlockSpec(memory_space=pl.ANY),
        scratch_shapes=[pltpu.VMEM((2, tm, HALF), U32), pltpu.VMEM((2, tm, HALF), U32),
                        pltpu.VMEM((D_MODEL, MOE_D_FF), BF), pltpu.VMEM((D_MODEL, MOE_D_FF), BF),
                        pltpu.VMEM((MOE_D_FF, D_MODEL), BF),
                        pltpu.SemaphoreType.DMA((2,)), pltpu.SemaphoreType.DMA((2,))],
    )
    return pl.pallas_call(
        _grouped_kernel,
        grid_spec=grid_spec,
        out_shape=jax.ShapeDtypeStruct(x_local.shape, x_local.dtype),
        input_output_aliases={5: 0},
        compiler_params=_params(1),
        name="grouped",
    )(te, n_used, gsrc, gdst, n_valid, x_local, wg, wu, wd)


def _merge_kernel(h_ref, idx_ref, wt_ref, g_ref, yl_ref, o_ref):
    tm = h_ref.shape[0]
    idx = idx_ref[...]
    wt = wt_ref[...]
    r_id = lax.broadcasted_iota(I32, (tm, LOCAL_ROWS), 1)
    sel = jnp.where(r_id == idx[:, 0:1], wt[:, 0:1], 0.0) + jnp.where(r_id == idx[:, 1:2], wt[:, 1:2], 0.0)
    y_lo, y_hi = _unpack_rows(yl_ref[...])
    sel16 = sel.astype(BF)
    moe = jnp.concatenate([_dot(sel16, y_lo), _dot(sel16, y_hi)], axis=1)
    o_ref[...] = _rms(h_ref[...] + moe, g_ref[...])


def _merge(h2d, idx, wt, g, y_local, *, tile_offset):
    t = h2d.shape[0]
    tm = TOKEN_TILE
    return pl.pallas_call(
        _merge_kernel,
        grid=(t // tm,),
        in_specs=[pl.BlockSpec((tm, D_MODEL), lambda i: (i, 0)),
                  pl.BlockSpec((tm, LANES), lambda i: (i + tile_offset, 0)),
                  pl.BlockSpec((tm, LANES), lambda i: (i + tile_offset, 0)),
                  _const_spec(g.shape),
                  pl.BlockSpec((LOCAL_ROWS, HALF), lambda i: (i + tile_offset, 0))],
        out_specs=pl.BlockSpec((tm, D_MODEL), lambda i: (i, 0)),
        out_shape=jax.ShapeDtypeStruct((t, D_MODEL), F32),
        compiler_params=_params(1),
        name="merge",
    )(h2d, idx, wt, g, y_local)


def _perm_matrix(bt, tl):
    m = bt * tl
    p = np.zeros((m, m), np.float32)
    for l in range(tl):
        for b in range(bt):
            p[l * bt + b, b * tl + l] = 1.0
    return p


def _s5_operators(lam_re, lam_im, log_step, b_re, b_im, c_re, c_im):
    lam = lax.complex(lam_re, lam_im)
    step = jnp.exp(log_step)[:, None]
    lam_bar = jnp.exp(lam * step)
    b_bar = ((lam_bar - 1.0) / lam)[..., None] * lax.complex(b_re, b_im)
    gl = LANES // S5_GROUP
    eye = jnp.eye(gl, dtype=F32)

    def in_block(bpart):
        bp = bpart.reshape(S5_SLABS, gl, S5_STATE, S5_GROUP)
        blk = jnp.einsum("sgnk,gh->sgkhn", bp, eye)
        return blk.reshape(S5_SLABS, LANES, gl * S5_STATE)

    def out_block(cpart):
        cp = cpart.reshape(S5_SLABS, gl, S5_GROUP, S5_STATE)
        blk = jnp.einsum("sgkn,gh->sgnhk", cp, eye)
        return blk.reshape(S5_SLABS, gl * S5_STATE, LANES)

    bblk = jnp.concatenate([in_block(jnp.real(b_bar)), in_block(jnp.imag(b_bar))], axis=2).astype(BF)
    cblk = jnp.concatenate([out_block(c_re), out_block(-c_im)], axis=1).astype(BF)
    lre = jnp.broadcast_to(jnp.real(lam_bar).reshape(1, S5_FLAT), (S5_SEQ_TILE, S5_FLAT))
    lim = jnp.broadcast_to(jnp.imag(lam_bar).reshape(1, S5_FLAT), (S5_SEQ_TILE, S5_FLAT))
    return bblk, cblk, lre, lim


def _pad_lanes(v, width=LANES):
    return jnp.pad(v, [(0, 0)] * (v.ndim - 1) + [(0, width - v.shape[-1])])


def kernel(x_prompt, x_sample, state_conv, state_ssd, state_s5_re, state_s5_im, cache_mem_k, cache_mem_v, mem_prompt, norm_mix_g, w_in, conv_w, conv_b, ssd_dt_bias, ssd_a_log, ssd_d, ssd_norm_g, w_ssd_branch, s5_lambda_re, s5_lambda_im, s5_log_step, s5_b_re, s5_b_im, s5_c_re, s5_c_im, s5_d, w_glu, b_glu, w_mix_out, norm_mem_q_g, norm_mem_kv_g, w_mem_q, w_mem_k, w_mem_v, w_mem_o, norm_ffn_g, w_router_group, b_router_group, w_router_expert, b_router_expert, w_exp_gate, w_exp_up, w_exp_down, norm_final_g):
    bp, lp, _ = x_prompt.shape
    bs, ls, _ = x_sample.shape
    tp, ts = bp * lp, bs * ls
    row1 = lambda v: v.reshape(1, -1).astype(F32)

    w_in0 = w_in[0]
    o_xbc = D_MODEL
    o_dt = o_xbc + SSD_CONV_DIM
    o_u5 = o_dt + SSD_HEADS
    w_main = jnp.concatenate([w_in0[:, :o_dt], w_in0[:, o_u5:]], axis=1).astype(BF)
    w_dt = _pad_lanes(w_in0[:, o_dt:o_u5]).astype(BF)
    dt_bias = _pad_lanes(row1(ssd_dt_bias[0]))
    a_log = _pad_lanes(row1(ssd_a_log[0]))
    dskip = jnp.repeat(ssd_d[0].astype(F32), SSD_HEADDIM).reshape(1, D_MODEL)
    eexp_np = np.zeros((LANES, D_MODEL), np.float32)
    for h in range(SSD_HEADS):
        eexp_np[h, h * SSD_HEADDIM:(h + 1) * SSD_HEADDIM] = 1.0
    eexp = jnp.asarray(eexp_np).astype(BF)
    eye16 = jnp.asarray(np.eye(SSD_HEADS, LANES, dtype=np.float32)).astype(BF)
    bblk, cblk, lre, lim = _s5_operators(s5_lambda_re[0], s5_lambda_im[0], s5_log_step[0], s5_b_re[0],
                                         s5_b_im[0], s5_c_re[0], s5_c_im[0])
    s5d = row1(s5_d[0])
    w_kv = jnp.concatenate([w_mem_k[0], w_mem_v[0]], axis=1).astype(BF)
    w_r = _pad_lanes(jnp.concatenate([w_router_group[0], w_router_expert[0]], axis=1).astype(F32))
    wr_h, wr_m, _ = _split3(w_r)
    b_r = _pad_lanes(row1(jnp.concatenate([b_router_group[0], b_router_expert[0]])))
    wg = w_exp_gate[0].reshape(MOE_EXPERTS, D_MODEL, MOE_D_FF)
    wu = w_exp_up[0].reshape(MOE_EXPERTS, D_MODEL, MOE_D_FF)
    wd = w_exp_down[0].reshape(MOE_EXPERTS, MOE_D_FF, D_MODEL)

    def mixer(x2d, batch, seqlen, conv0, h0, s5re0, s5im0):
        z, xbc, u5, ga, gb, dt = _inproj(x2d, row1(norm_mix_g[0]), w_main, w_dt, dt_bias)
        y_ssd, new_conv, new_ssd = _ssd(xbc, dt, conv_w[0].astype(F32), row1(conv_b[0]), a_log, dskip, eexp, eye16,
                                        conv0, h0, batch=batch, seqlen=seqlen)
        if h0 is None:
            tl = S5_TIME_TILE
            u_in = u5.reshape(batch, seqlen, D_MODEL)
        else:
            tl = seqlen
            u_in = u5
        perm = _perm_matrix(S5_SEQ_TILE, tl)
        y5, new_re, new_im = _s5(u_in, jnp.asarray(perm).astype(BF), jnp.asarray(perm.T).astype(BF), bblk, cblk,
                                 lre, lim, s5d, s5re0, s5im0, batch=batch, seqlen=seqlen, tl=tl)
        h1 = _mixout(y_ssd, z, y5.reshape(-1, D_MODEL), ga, gb, x2d, row1(ssd_norm_g[0]),
                     w_ssd_branch[0].astype(BF), w_glu[0].astype(BF), row1(b_glu[0]), w_mix_out[0].astype(BF))
        return h1, new_conv, new_ssd, new_re, new_im

    xp2 = x_prompt.reshape(tp, D_MODEL)
    xs2 = x_sample.reshape(ts, D_MODEL)
    h1p, conv_p, ssd_p, re_p, im_p = mixer(xp2, bp, lp, None, None, None, None)
    h1s, conv_s, ssd_s, re_s, im_s = mixer(xs2, bs, ls, state_conv[0], state_ssd[0],
                                           state_s5_re[0].reshape(bs, S5_FLAT), state_s5_im[0].reshape(bs, S5_FLAT))

    mk_p, mv_p = _memkv(mem_prompt.reshape(bp * MEM_LEN, D_MODEL), row1(norm_mem_kv_g[0]), w_kv)
    gq = row1(norm_mem_q_g[0])
    wq = w_mem_q[0].astype(BF)
    wo = w_mem_o[0].astype(BF)
    h2p = _attn(h1p, gq, wq, wo, mk_p.reshape(bp, MEM_LEN, D_MODEL), mv_p.reshape(bp, MEM_LEN, D_MODEL),
                rows=TOKEN_TILE, nkv=1, rows_per_seq=lp)
    sample_seqs = 8
    h2s = _attn(h1s, gq, wq, wo, cache_mem_k[0], cache_mem_v[0], rows=sample_seqs * ls, nkv=sample_seqs,
                rows_per_seq=ls)

    g_ffn = row1(norm_ffn_g[0])
    x_local, r_idx, r_wt, seg = _route(h2p, h2s, g_ffn, jnp.concatenate([wr_h, wr_m], axis=1), wr_h, b_r)
    token_tiles = (tp + ts) // TOKEN_TILE
    max_granules = (2 * (tp + ts) + token_tiles * MOE_EXPERTS * (GRANULE - 1)) // GRANULE
    n_tiles = max_granules // TILE_GRANULES + MOE_EXPERTS
    te, n_used, gsrc, gdst, n_valid = _granule_plan(seg, n_tiles)
    y_local = _grouped(x_local, te, n_used, gsrc, gdst, n_valid, wg, wu, wd)
    gf = row1(norm_final_g)
    y_prompt = _merge(h2p, r_idx, r_wt, gf, y_local, tile_offset=0)
    y_sample = _merge(h2s, r_idx, r_wt, gf, y_local, tile_offset=tp // TOKEN_TILE)

    return (y_prompt.reshape(bp, lp, D_MODEL), y_sample.reshape(bs, ls, D_MODEL),
            conv_p[None], ssd_p[None],
            re_p.reshape(1, bp, S5_GROUPS, S5_STATE), im_p.reshape(1, bp, S5_GROUPS, S5_STATE),
            mk_p.reshape(1, bp, MEM_LEN, MEM_HEADS, MEM_HEAD_DIM), mv_p.reshape(1, bp, MEM_LEN, MEM_HEADS, MEM_HEAD_DIM),
            conv_s[None], ssd_s[None],
            re_s.reshape(1, bs, S5_GROUPS, S5_STATE), im_s.reshape(1, bs, S5_GROUPS, S5_STATE))
```

```python
import functools
import math

import numpy as np
import jax
import jax.numpy as jnp
from jax import lax
from jax.experimental import pallas as pl
from jax.experimental.pallas import tpu as pltpu

F32 = jnp.float32
BF = jnp.bfloat16
I32 = jnp.int32

D_MODEL = 1024
SSD_HEADS = 16
SSD_HEADDIM = 64
SSD_GROUPS = 2
SSD_STATE = 128
SSD_CONV = 4
SSD_CONV_DIM = 1536
HEADS_PER_GROUP = SSD_HEADS // SSD_GROUPS
S5_GROUPS = 64
S5_GROUP = 16
S5_STATE = 64
S5_FLAT = S5_GROUPS * S5_STATE
S5_SLABS = D_MODEL // 128
MEM_LEN = 256
MEM_HEADS = 4
MEM_HEAD_DIM = 256
MOE_GROUPS = 4
MOE_EXPERTS_PER_GROUP = 8
MOE_EXPERTS = MOE_GROUPS * MOE_EXPERTS_PER_GROUP
MOE_D_FF = 512
NORM_EPS = 1e-6

LANES = 128
VMEM_LIMIT_BYTES = 56 * 1024 * 1024
TOKEN_TILE = 512
MIX_TILE = 1024
SSD_CHUNK = 128
S5_TIME_TILE = 64
MOE_ROW_TILE = 512
NEG_BIG = -1e30


def _params(n_axes):
    return pltpu.CompilerParams(dimension_semantics=("arbitrary",) * n_axes,
                                vmem_limit_bytes=VMEM_LIMIT_BYTES)


def _const_spec(shape):
    nd = len(shape)
    return pl.BlockSpec(shape, lambda *_: (0,) * nd)


def _resident_spec(shape):
    nd = len(shape)
    return pl.BlockSpec(shape, lambda *_: (0,) * nd, pipeline_mode=pl.Buffered(1))


def _dot(a, b):
    return jnp.dot(a, b, preferred_element_type=F32)


def _dot_nt(a, b):
    return lax.dot_general(a, b, (((1,), (1,)), ((), ())), preferred_element_type=F32)


def _split3(v):
    h = v.astype(BF)
    r = v - h.astype(F32)
    m = r.astype(BF)
    l = (r - m.astype(F32)).astype(BF)
    return h, m, l


def _dot_exact(a_bf, v, nt=False):
    f = _dot_nt if nt else _dot
    h, m, l = _split3(v)
    return f(a_bf, h) + f(a_bf, m) + f(a_bf, l)


def _rms(x, g):
    return x * lax.rsqrt(jnp.mean(x * x, axis=-1, keepdims=True) + NORM_EPS) * g


def _sigmoid(x):
    return 0.5 * jnp.tanh(0.5 * x) + 0.5


def _silu(x):
    return x * _sigmoid(x)


def _proj_cols(xb, w_ref, o_ref, c0, width):
    for c in range(0, width, 512):
        ce = min(c + 512, width)
        o_ref[:, c:ce] = _dot(xb, w_ref[:, c0 + c:c0 + ce]).astype(o_ref.dtype)


def _inproj_kernel(x_ref, g_ref, w_ref, wdt_ref, dtb_ref, z_ref, xbc_ref, u5_ref, ga_ref, gb_ref, dt_ref):
    xb = _rms(x_ref[...], g_ref[...]).astype(BF)
    c0 = 0
    for o_ref in (z_ref, xbc_ref, u5_ref, ga_ref, gb_ref):
        width = o_ref.shape[1]
        _proj_cols(xb, w_ref, o_ref, c0, width)
        c0 += width
    raw = _dot(xb, wdt_ref[...]) + dtb_ref[...]
    dt_ref[...] = jnp.maximum(raw, 0.0) + jnp.log1p(jnp.exp(-jnp.abs(raw)))


def _inproj(x2d, g, w_main, w_dt, dt_bias):
    t = x2d.shape[0]
    tm = MIX_TILE
    widths = (D_MODEL, SSD_CONV_DIM, D_MODEL, D_MODEL, D_MODEL)
    row = lambda w: pl.BlockSpec((tm, w), lambda i: (i, 0))
    return pl.pallas_call(
        _inproj_kernel,
        grid=(t // tm,),
        in_specs=[row(D_MODEL), _const_spec(g.shape), _resident_spec(w_main.shape), _resident_spec(w_dt.shape),
                  _const_spec(dt_bias.shape)],
        out_specs=[row(w) for w in widths] + [row(LANES)],
        out_shape=[jax.ShapeDtypeStruct((t, w), BF) for w in widths] + [jax.ShapeDtypeStruct((t, LANES), F32)],
        compiler_params=_params(1),
        name="inproj",
    )(x2d, g, w_main, w_dt, dt_bias)


def _memkv_kernel(x_ref, g_ref, w_ref, k_ref, v_ref):
    xb = _rms(x_ref[...], g_ref[...]).astype(BF)
    _proj_cols(xb, w_ref, k_ref, 0, D_MODEL)
    _proj_cols(xb, w_ref, v_ref, D_MODEL, D_MODEL)


def _memkv(mem2d, g, w_kv):
    t = mem2d.shape[0]
    tm = TOKEN_TILE
    row = pl.BlockSpec((tm, D_MODEL), lambda i: (i, 0))
    return pl.pallas_call(
        _memkv_kernel,
        grid=(t // tm,),
        in_specs=[row, _const_spec(g.shape), _const_spec(w_kv.shape)],
        out_specs=[row, row],
        out_shape=[jax.ShapeDtypeStruct((t, D_MODEL), F32)] * 2,
        compiler_params=_params(1),
        name="memkv",
    )(mem2d, g, w_kv)


def _ssd_kernel(*refs, lq, has_h0):
    q = SSD_CHUNK
    nseq = q // lq
    if has_h0:
        (xbc_ref, dt_ref, cw_ref, cb_ref, alog_ref, dsk_ref, eexp_ref, eye_ref, shift_ref, conv0_ref, h0_ref,
         y_ref, convo_ref, ho_ref, hbuf, cacc, yacc, yint, xwt, tot_s) = refs
        hin_ref = h0_ref
        hbuf[...] = jnp.zeros_like(hbuf)
        hbuf[:, 0:SSD_CONV - 1, :] = conv0_ref[...]
    else:
        (xbc_ref, dt_ref, cw_ref, cb_ref, alog_ref, dsk_ref, eexp_ref, eye_ref, shift_ref,
         y_ref, convo_ref, ho_ref, hbuf, cacc, yacc, yint, xwt, tot_s) = refs
        hin_ref = ho_ref

        @pl.when(pl.program_id(1) == 0)
        def _():
            ho_ref[...] = jnp.zeros_like(ho_ref)
            hbuf[...] = jnp.zeros_like(hbuf)

    x16 = xbc_ref[...]
    xraw = x16.astype(F32)
    acc = cb_ref[...] + xraw * cw_ref[SSD_CONV - 1:SSD_CONV, :]
    for k in range(SSD_CONV - 1):
        acc = acc + _dot(shift_ref[k], x16) * cw_ref[k:k + 1, :]
    cacc[...] = acc
    for i in range(nseq):
        tail = xraw[(i + 1) * lq - (SSD_CONV - 1):(i + 1) * lq]
        convo_ref[i] = tail
        corr = hbuf[i, 0:8, :] * cw_ref[0:1, :]
        for k in range(1, SSD_CONV - 1):
            corr = corr + hbuf[i, k:k + 8, :] * cw_ref[k:k + 1, :]
        cacc[i * lq:i * lq + 8, :] += corr
        if not has_h0:
            hbuf[i, 0:SSD_CONV - 1, :] = tail
    xc = _silu(cacc[...])
    xs = xc[:, :D_MODEL]
    xs_bf = xs.astype(BF)
    bm_bf = xc[:, D_MODEL:D_MODEL + SSD_GROUPS * SSD_STATE].astype(BF)
    cm = xc[:, D_MODEL + SSD_GROUPS * SSD_STATE:]
    cm_bf = cm.astype(BF)

    dt = dt_ref[...]
    da = dt * (-jnp.exp(alog_ref[...]))
    ri = lax.broadcasted_iota(I32, (q, q), 0)
    ci = lax.broadcasted_iota(I32, (q, q), 1)
    if nseq == 1:
        causal = ci <= ri
    else:
        sh = int(math.log2(lq))
        same = lax.shift_right_logical(ri, sh) == lax.shift_right_logical(ci, sh)
        causal = same & (ci <= ri)
    lmat = jnp.where(causal, 1.0, 0.0).astype(BF)
    cs = _dot_exact(lmat, da)
    if nseq == 1:
        tot = jnp.broadcast_to(cs[q - 1:q, :], (q, LANES))
    else:
        tot = _dot_exact(jnp.where(same, 1.0, 0.0).astype(BF), da)
    tot_s[...] = tot
    eye = eye_ref[...]
    cs_t = _dot_exact(eye, cs, nt=True)
    dt_t = _dot_exact(eye, dt, nt=True)
    eexp = eexp_ref[...]
    ecs_h, ecs_m, _ = _split3(jnp.exp(cs))
    ecs_x = _dot(ecs_h, eexp) + _dot(ecs_m, eexp)
    wend_x = _dot((jnp.exp(tot - cs) * dt).astype(BF), eexp)

    lane = lax.broadcasted_iota(I32, (q, LANES), 1)
    for g in range(SSD_GROUPS):
        cbg = _dot_nt(cm_bf[:, g * SSD_STATE:(g + 1) * SSD_STATE], bm_bf[:, g * SSD_STATE:(g + 1) * SSD_STATE])
        for jp in range(HEADS_PER_GROUP // 2):
            j = g * (HEADS_PER_GROUP // 2) + jp
            ms = []
            for h in (2 * j, 2 * j + 1):
                col = jnp.broadcast_to(cs[:, h:h + 1], (q, q))
                row = jnp.broadcast_to(cs_t[h:h + 1, :], (q, q))
                dtr = jnp.broadcast_to(dt_t[h:h + 1, :], (q, q))
                ms.append((jnp.where(causal, jnp.exp(col - row), 0.0) * cbg * dtr).astype(BF))
            mp = jnp.concatenate(ms, axis=1)
            xp = xs_bf[:, j * LANES:(j + 1) * LANES]
            zero = jnp.zeros_like(xp)
            x2 = jnp.concatenate([jnp.where(lane < SSD_HEADDIM, xp, zero),
                                  jnp.where(lane >= SSD_HEADDIM, xp, zero)], axis=0)
            yacc[:, j * LANES:(j + 1) * LANES] = _dot(mp, x2)

    gw = HEADS_PER_GROUP * SSD_HEADDIM
    xw = xs * wend_x
    for g in range(SSD_GROUPS):
        xwt[g * gw:(g + 1) * gw, :] = xw[:, g * gw:(g + 1) * gw].T.astype(BF)

    if nseq == 1:
        for g in range(SSD_GROUPS):
            hin_g = hin_ref[0, g * HEADS_PER_GROUP:(g + 1) * HEADS_PER_GROUP].reshape(gw, SSD_STATE)
            yint[:, g * gw:(g + 1) * gw] = _dot_nt(cm_bf[:, g * SSD_STATE:(g + 1) * SSD_STATE], hin_g.astype(BF))
            s_new = _dot(xwt[g * gw:(g + 1) * gw, :], bm_bf[:, g * SSD_STATE:(g + 1) * SSD_STATE])
            for hl in range(HEADS_PER_GROUP):
                h = g * HEADS_PER_GROUP + hl
                dec = jnp.exp(jnp.broadcast_to(tot[0:1, h:h + 1], (SSD_HEADDIM, SSD_STATE)))
                ho_ref[0, h] = dec * hin_ref[0, h] + s_new[hl * SSD_HEADDIM:(hl + 1) * SSD_HEADDIM]
    else:
        yint[...] = jnp.zeros_like(yint)
        sh = int(math.log2(lq))
        rowseq = lax.shift_right_logical(lax.broadcasted_iota(I32, (q, SSD_STATE), 0), sh)
        colseq = lax.shift_right_logical(lax.broadcasted_iota(I32, (gw, q), 1), sh)

        def seq_body(i, carry):
            trow = tot_s[pl.ds(i * lq, 1), :]
            for g in range(SSD_GROUPS):
                hin_g = hin_ref[i, g * HEADS_PER_GROUP:(g + 1) * HEADS_PER_GROUP].reshape(gw, SSD_STATE)
                cmg = cm[:, g * SSD_STATE:(g + 1) * SSD_STATE]
                lhs = jnp.where(rowseq == i, cmg, 0.0).astype(BF)
                yint[:, g * gw:(g + 1) * gw] += _dot_nt(lhs, hin_g.astype(BF))
                xg = xwt[g * gw:(g + 1) * gw, :]
                xi = jnp.where(colseq == i, xg, jnp.zeros_like(xg))
                s_new = _dot(xi, bm_bf[:, g * SSD_STATE:(g + 1) * SSD_STATE])
                for hl in range(HEADS_PER_GROUP):
                    h = g * HEADS_PER_GROUP + hl
                    dec = jnp.exp(jnp.broadcast_to(trow[:, h:h + 1], (SSD_HEADDIM, SSD_STATE)))
                    ho_ref[i, h] = dec * hin_ref[i, h] + s_new[hl * SSD_HEADDIM:(hl + 1) * SSD_HEADDIM]
            return carry

        lax.fori_loop(0, nseq, seq_body, 0)

    y_ref[...] = (yacc[...] + yint[...] * ecs_x + dsk_ref[...] * xs).astype(y_ref.dtype)


def _ssd(xbc, dt, conv_w, conv_b, a_log, dskip, eexp, eye, conv0, h0, *, batch, seqlen):
    q = SSD_CHUNK
    t = xbc.shape[0]
    has_h0 = h0 is not None
    if has_h0:
        lq = seqlen
        nseq = q // lq
        grid = (t // q,)
        rmap = lambda i: (i, 0)
        smap3 = lambda i: (i, 0, 0)
        smap4 = lambda i: (i, 0, 0, 0)
    else:
        lq = q
        nseq = 1
        nc = seqlen // q
        grid = (batch, nc)
        rmap = lambda b, c: (b * nc + c, 0)
        smap3 = lambda b, c: (b, 0, 0)
        smap4 = lambda b, c: (b, 0, 0, 0)
    shifts = np.zeros((SSD_CONV - 1, q, q), np.float32)
    for k in range(SSD_CONV - 1):
        for r in range(q):
            if r % lq + k - (SSD_CONV - 1) >= 0:
                shifts[k, r, r + k - (SSD_CONV - 1)] = 1.0
    consts = [conv_w, conv_b, a_log, dskip, eexp, eye, jnp.asarray(shifts).astype(BF)]
    in_specs = [pl.BlockSpec((q, SSD_CONV_DIM), rmap), pl.BlockSpec((q, LANES), rmap)]
    in_specs += [_const_spec(c.shape) for c in consts]
    args = [xbc, dt] + consts
    if has_h0:
        in_specs += [pl.BlockSpec((nseq, SSD_CONV - 1, SSD_CONV_DIM), smap3),
                     pl.BlockSpec((nseq, SSD_HEADS, SSD_HEADDIM, SSD_STATE), smap4)]
        args += [conv0, h0]
    return pl.pallas_call(
        functools.partial(_ssd_kernel, lq=lq, has_h0=has_h0),
        grid=grid,
        in_specs=in_specs,
        out_specs=[pl.BlockSpec((q, D_MODEL), rmap),
                   pl.BlockSpec((nseq, SSD_CONV - 1, SSD_CONV_DIM), smap3),
                   pl.BlockSpec((nseq, SSD_HEADS, SSD_HEADDIM, SSD_STATE), smap4)],
        out_shape=[jax.ShapeDtypeStruct((t, D_MODEL), BF),
                   jax.ShapeDtypeStruct((batch, SSD_CONV - 1, SSD_CONV_DIM), F32),
                   jax.ShapeDtypeStruct((batch, SSD_HEADS, SSD_HEADDIM, SSD_STATE), F32)],
        scratch_shapes=[pltpu.VMEM((nseq, 16, SSD_CONV_DIM), F32),
                        pltpu.VMEM((q, SSD_CONV_DIM), F32),
                        pltpu.VMEM((q, D_MODEL), F32),
                        pltpu.VMEM((q, D_MODEL), F32),
                        pltpu.VMEM((D_MODEL, q), BF),
                        pltpu.VMEM((q, LANES), F32)],
        compiler_params=_params(len(grid)),
        name="ssd_sample" if has_h0 else "ssd_prompt",
    )(*args)


S5_SEQ_TILE = 8


def _s5_kernel(*refs, tl, has_h0):
    bt = S5_SEQ_TILE
    m = bt * tl
    if has_h0:
        (u_ref, perm_ref, permt_ref, bblk_ref, cblk_ref, lre_ref, lim_ref, d_ref, h0re_ref, h0im_ref,
         y_ref, ore_ref, oim_ref, bre, bim, sre, sim, utm, ytm) = refs
    else:
        (u_ref, perm_ref, permt_ref, bblk_ref, cblk_ref, lre_ref, lim_ref, d_ref,
         y_ref, ore_ref, oim_ref, bre, bim, sre, sim, utm, ytm) = refs

    @pl.when(pl.program_id(1) == 0)
    def _():
        if has_h0:
            sre[...] = h0re_ref[...]
            sim[...] = h0im_ref[...]
        else:
            sre[...] = jnp.zeros_like(sre)
            sim[...] = jnp.zeros_like(sim)

    strided = len(u_ref.shape) == 3
    if strided:
        for b in range(bt):
            ub = u_ref[b].astype(F32)
            for s in range(S5_SLABS):
                utm[s, pl.ds(b, tl, stride=bt), :] = ub[:, s * LANES:(s + 1) * LANES]
        slab_in = lambda j: utm[j].astype(BF)
    else:
        u = u_ref[...]
        u_tm = _dot(perm_ref[...], u).astype(BF)
        slab_in = lambda j: u_tm[:, j * LANES:(j + 1) * LANES]
    half = S5_FLAT // S5_SLABS
    ys = []
    for j in range(S5_SLABS):
        sl = slice(j * half, (j + 1) * half)
        r = _dot(slab_in(j), bblk_ref[j])
        bre[:, sl] = r[:, :half]
        bim[:, sl] = r[:, half:]
        lr = lre_ref[:, sl]
        li = lim_ref[:, sl]
        sr = sre[:, sl]
        si = sim[:, sl]
        for l in range(tl):
            rows = slice(l * bt, (l + 1) * bt)
            sr, si = lr * sr - li * si + bre[rows, sl], lr * si + li * sr + bim[rows, sl]
            bre[rows, sl] = sr
            bim[rows, sl] = si
        sre[:, sl] = sr
        sim[:, sl] = si
        st = jnp.concatenate([bre[:, sl], bim[:, sl]], axis=1)
        ys.append(_dot(st.astype(BF), cblk_ref[j]))
    if strided:
        for j in range(S5_SLABS):
            ytm[j] = ys[j]
        for b in range(bt):
            yb = jnp.concatenate([ytm[s, pl.ds(b, tl, stride=bt), :] for s in range(S5_SLABS)], axis=1)
            y_ref[b] = (yb + d_ref[...] * u_ref[b].astype(F32)).astype(y_ref.dtype)
    else:
        y_tm = jnp.concatenate(ys, axis=1).astype(BF)
        y_ref[...] = (_dot(permt_ref[...], y_tm) + d_ref[...] * u.astype(F32)).astype(y_ref.dtype)
    ore_ref[...] = sre[...]
    oim_ref[...] = sim[...]


def _s5(u, perm, permt, bblk, cblk, lre, lim, dvec, h0re, h0im, *, batch, seqlen, tl):
    bt = S5_SEQ_TILE
    m = bt * tl
    has_h0 = h0re is not None
    nb = batch // bt
    nt = seqlen // tl
    if u.ndim == 3:
        u_spec = pl.BlockSpec((bt, tl, D_MODEL), lambda b, t: (b, t, 0))
    else:
        u_spec = pl.BlockSpec((m, D_MODEL), lambda b, t: (b * nt + t, 0))
    st_spec = pl.BlockSpec((bt, S5_FLAT), lambda b, t: (b, 0))
    consts = [perm, permt, bblk, cblk, lre, lim, dvec]
    in_specs = [u_spec] + [_const_spec(c.shape) for c in consts]
    args = [u] + consts
    if has_h0:
        in_specs += [st_spec, st_spec]
        args += [h0re, h0im]
    return pl.pallas_call(
        functools.partial(_s5_kernel, tl=tl, has_h0=has_h0),
        grid=(nb, nt),
        in_specs=in_specs,
        out_specs=[u_spec, st_spec, st_spec],
        out_shape=[jax.ShapeDtypeStruct(u.shape, BF),
                   jax.ShapeDtypeStruct((batch, S5_FLAT), F32),
                   jax.ShapeDtypeStruct((batch, S5_FLAT), F32)],
        scratch_shapes=[pltpu.VMEM((m, S5_FLAT), F32), pltpu.VMEM((m, S5_FLAT), F32),
                        pltpu.VMEM((bt, S5_FLAT), F32), pltpu.VMEM((bt, S5_FLAT), F32),
                        pltpu.VMEM((S5_SLABS, m, LANES), F32), pltpu.VMEM((S5_SLABS, m, LANES), F32)],
        compiler_params=_params(2),
        name="s5_sample" if has_h0 else "s5_prompt",
    )(*args)


def _mixout_kernel(y_ref, z_ref, y5_ref, ga_ref, gb_ref, x_ref, ng_ref, wa_ref, wglu_ref, bglu_ref, wmix_ref, o_ref):
    y = y_ref[...].astype(F32) * _silu(z_ref[...].astype(F32))
    yn = _rms(y, ng_ref[...]).astype(BF)
    branch_a = _dot(yn, wa_ref[...])
    glu = _dot(y5_ref[...], wglu_ref[...]) + bglu_ref[...]
    branch_b = glu[:, :D_MODEL] * _sigmoid(glu[:, D_MODEL:])
    merged = _sigmoid(ga_ref[...].astype(F32)) * branch_a + _sigmoid(gb_ref[...].astype(F32)) * branch_b
    o_ref[...] = x_ref[...] + _dot(merged.astype(BF), wmix_ref[...])


def _mixout(y, z, y5, ga, gb, x2d, ng, wa, wglu, bglu, wmix):
    t = x2d.shape[0]
    tm = MIX_TILE
    row = pl.BlockSpec((tm, D_MODEL), lambda i: (i, 0))
    consts = [ng, wa, wglu, bglu, wmix]
    return pl.pallas_call(
        _mixout_kernel,
        grid=(t // tm,),
        in_specs=[row] * 6 + [_resident_spec(c.shape) for c in consts],
        out_specs=row,
        out_shape=jax.ShapeDtypeStruct((t, D_MODEL), F32),
        compiler_params=_params(1),
        name="mixout",
    )(y, z, y5, ga, gb, x2d, *consts)


def _attn_kernel(h_ref, g_ref, wq_ref, wo_ref, k_ref, v_ref, o_ref, obuf, *, nkv, rows_per_seq):
    h = h_ref[...]
    r = h.shape[0]
    q32 = _dot(_rms(h, g_ref[...]).astype(BF), wq_ref[...])
    scale = MEM_HEAD_DIM ** -0.5
    if len(k_ref.shape) == 4:
        rq = rows_per_seq
        nrow = MEM_HEADS * rq
        row_head = lax.shift_right_logical(lax.broadcasted_iota(I32, (nrow, MEM_LEN * MEM_HEADS), 0), int(math.log2(rq)))
        col_head = lax.broadcasted_iota(I32, (nrow, MEM_LEN * MEM_HEADS), 1) & (MEM_HEADS - 1)
        visible = row_head == col_head
        for i in range(nkv):
            qs = jnp.concatenate([q32[i * rq:(i + 1) * rq, hd * MEM_HEAD_DIM:(hd + 1) * MEM_HEAD_DIM]
                                  for hd in range(MEM_HEADS)], axis=0).astype(BF)
            kf = k_ref[i].reshape(MEM_LEN * MEM_HEADS, MEM_HEAD_DIM).astype(BF)
            vf = v_ref[i].reshape(MEM_LEN * MEM_HEADS, MEM_HEAD_DIM).astype(BF)
            s = jnp.where(visible, _dot_nt(qs, kf) * scale, NEG_BIG)
            p = jnp.exp(s - jnp.max(s, axis=-1, keepdims=True))
            p = p / jnp.sum(p, axis=-1, keepdims=True)
            o = _dot(p.astype(BF), vf)
            for hd in range(MEM_HEADS):
                obuf[i * rq:(i + 1) * rq, hd * MEM_HEAD_DIM:(hd + 1) * MEM_HEAD_DIM] = o[hd * rq:(hd + 1) * rq]
    else:
        qv = q32.astype(BF)
        for hd in range(MEM_HEADS):
            sl = slice(hd * MEM_HEAD_DIM, (hd + 1) * MEM_HEAD_DIM)
            s = _dot_nt(qv[:, sl], k_ref[0, :, sl].astype(BF)) * scale
            p = jnp.exp(s - jnp.max(s, axis=-1, keepdims=True))
            p = p / jnp.sum(p, axis=-1, keepdims=True)
            obuf[:, sl] = _dot(p.astype(BF), v_ref[0, :, sl].astype(BF))
    o_ref[...] = h + _dot(obuf[...].astype(BF), wo_ref[...])


def _attn(h2d, g, wq, wo, k3, v3, *, rows, nkv, rows_per_seq):
    t = h2d.shape[0]
    nsteps = t // rows
    steps_per_kv = nsteps // (k3.shape[0] // nkv)
    row = pl.BlockSpec((rows, D_MODEL), lambda i: (i, 0))
    if k3.ndim == 4:
        kv = pl.BlockSpec((nkv, MEM_LEN, MEM_HEADS, MEM_HEAD_DIM), lambda i: (i // steps_per_kv, 0, 0, 0))
    else:
        kv = pl.BlockSpec((nkv, MEM_LEN, D_MODEL), lambda i: (i // steps_per_kv, 0, 0))
    consts = [g, wq, wo]
    return pl.pallas_call(
        functools.partial(_attn_kernel, nkv=nkv, rows_per_seq=rows_per_seq),
        grid=(nsteps,),
        in_specs=[row] + [_const_spec(c.shape) for c in consts] + [kv, kv],
        out_specs=row,
        out_shape=jax.ShapeDtypeStruct((t, D_MODEL), F32),
        scratch_shapes=[pltpu.VMEM((rows, D_MODEL), F32)],
        compiler_params=_params(1),
        name="attn",
    )(h2d, *consts, k3, v3)


ROUTER_LANE0 = MOE_GROUPS

U32 = jnp.uint32
HALF = D_MODEL // 2


def _pack_rows(x):
    xr = x.astype(BF).astype(F32)
    lo = lax.shift_right_logical(pltpu.bitcast(xr[:, :HALF], U32), jnp.uint32(16))
    hi = pltpu.bitcast(xr[:, HALF:], U32) & jnp.uint32(0xFFFF0000)
    return hi | lo


def _unpack_rows(w):
    lo = pltpu.bitcast(lax.shift_left(w, jnp.uint32(16)), F32).astype(BF)
    hi = pltpu.bitcast(w & jnp.uint32(0xFFFF0000), F32).astype(BF)
    return lo, hi


GRANULE = 8
LOCAL_USED = 2 * TOKEN_TILE + MOE_EXPERTS * (GRANULE - 1)
LOCAL_ROWS = 1280
SPARE_GRANULES = (LOCAL_ROWS - LOCAL_USED) // GRANULE
TILE_GRANULES = MOE_ROW_TILE // GRANULE


def _route_kernel(hp_ref, hs_ref, g_ref, wcat_ref, wh_ref, b_ref, tri_ref, eye_ref, upper_ref,
                  xl_ref, idx_ref, wt_ref, seg_ref, *, np_steps):
    i = pl.program_id(0)
    h = jnp.where(i < np_steps, hp_ref[...], hs_ref[...])
    xn = _rms(h, g_ref[...])
    tm = xn.shape[0]
    xh = xn.astype(BF)
    xm = (xn - xh.astype(F32)).astype(BF)
    both = _dot(xh, wcat_ref[...])
    logits = both[:, :LANES] + both[:, LANES:] + _dot(xm, wh_ref[...]) + b_ref[...]
    lane = lax.broadcasted_iota(I32, logits.shape, 1)
    lane_f = lane.astype(F32)
    no_lane = float(LANES)
    is_g = lane < MOE_GROUPS
    lg = jnp.where(is_g, logits, NEG_BIG)
    mg = jnp.max(lg, axis=-1, keepdims=True)
    gidx = jnp.min(jnp.where(lg == mg, lane_f, no_lane), axis=-1, keepdims=True).astype(I32)
    g_w = 1.0 / jnp.sum(jnp.where(is_g, jnp.exp(lg - mg), 0.0), axis=-1, keepdims=True)
    e_lane = lane - ROUTER_LANE0
    in_grp = ((e_lane >= 0) & (e_lane < MOE_EXPERTS)
              & (lax.shift_right_logical(jnp.maximum(e_lane, 0), 3) == gidx))
    le = jnp.where(in_grp, logits, NEG_BIG)
    m1 = jnp.max(le, axis=-1, keepdims=True)
    i1 = jnp.min(jnp.where(le == m1, lane_f, no_lane), axis=-1, keepdims=True)
    le2 = jnp.where(lane_f == i1, NEG_BIG, le)
    m2 = jnp.max(le2, axis=-1, keepdims=True)
    i2 = jnp.min(jnp.where(le2 == m2, lane_f, no_lane), axis=-1, keepdims=True)
    ratio = jnp.exp(m2 - m1)
    w1 = g_w / (1.0 + ratio)
    w2 = g_w * ratio / (1.0 + ratio)
    wt_ref[...] = jnp.where(lane == 0, w1, jnp.where(lane == 1, w2, 0.0))

    oh1 = jnp.where(lane_f == i1, 1.0, 0.0)
    oh2 = jnp.where(lane_f == i2, 1.0, 0.0)
    tot1 = jnp.sum(oh1, axis=0, keepdims=True)
    cnt = (tot1 + jnp.sum(oh2, axis=0, keepdims=True)).astype(I32)
    padded = lax.shift_left(lax.shift_right_logical(cnt + (GRANULE - 1), 3), 3)
    pad8 = jnp.broadcast_to(padded.astype(F32), (8, LANES)).astype(BF)
    lstart = _dot(pad8, upper_ref[...])[0:1, :]
    tri = tri_ref[...]
    l1 = jnp.sum(oh1 * (_dot(tri, oh1.astype(BF)) + lstart), axis=-1, keepdims=True)
    l2 = jnp.sum(oh2 * (_dot(tri, oh2.astype(BF)) + (lstart + tot1)), axis=-1, keepdims=True)
    idx_ref[...] = jnp.where(lane == 0, l1.astype(I32), jnp.where(lane == 1, l2.astype(I32), 0))
    sub = lax.broadcasted_iota(I32, (8, LANES), 0)
    seg_ref[0] = jnp.where(sub == 0, jnp.broadcast_to(cnt, (8, LANES)),
                           jnp.where(sub == 1, jnp.broadcast_to(lstart.astype(I32), (8, LANES)), 0))

    cols = jnp.where(lane == 0, l1, jnp.where(lane == 1, l2, 0.0))
    rows = _dot_exact(eye_ref[...], cols, nt=True)
    r_id = lax.broadcasted_iota(I32, (LOCAL_ROWS, tm), 0).astype(F32)
    perm = jnp.where(r_id == rows[0:1, :], 1.0, 0.0) + jnp.where(r_id == rows[1:2, :], 1.0, 0.0)
    xl_ref[...] = _pack_rows(_dot(perm.astype(BF), xh))


def _route(hp, hs, g, wcat, wh, b):
    tm = TOKEN_TILE
    np_steps = hp.shape[0] // tm
    ns_steps = hs.shape[0] // tm
    nt = np_steps + ns_steps
    tri = jnp.asarray(np.tril(np.ones((tm, tm), np.float32), -1)).astype(BF)
    eye8 = jnp.asarray(np.eye(8, LANES, dtype=np.float32)).astype(BF)
    upper = jnp.asarray(np.triu(np.ones((LANES, LANES), np.float32), 1)).astype(BF)
    consts = [g, wcat, wh, b, tri, eye8, upper]
    return pl.pallas_call(
        functools.partial(_route_kernel, np_steps=np_steps),
        grid=(nt,),
        in_specs=[pl.BlockSpec((tm, D_MODEL), lambda i: (jnp.minimum(i, np_steps - 1), 0)),
                  pl.BlockSpec((tm, D_MODEL), lambda i: (jnp.maximum(i - np_steps, 0), 0))]
                 + [_const_spec(c.shape) for c in consts],
        out_specs=[pl.BlockSpec((LOCAL_ROWS, HALF), lambda i: (i, 0)),
                   pl.BlockSpec((tm, LANES), lambda i: (i, 0)),
                   pl.BlockSpec((tm, LANES), lambda i: (i, 0)),
                   pl.BlockSpec((1, 8, LANES), lambda i: (i, 0, 0))],
        out_shape=[jax.ShapeDtypeStruct((nt * LOCAL_ROWS, HALF), U32),
                   jax.ShapeDtypeStruct((nt * tm, LANES), I32),
                   jax.ShapeDtypeStruct((nt * tm, LANES), F32),
                   jax.ShapeDtypeStruct((nt, 8, LANES), I32)],
        compiler_params=_params(1),
        name="route",
    )(hp, hs, *consts)


def _granule_plan(seg, n_tiles):
    cnt = seg[:, 0, ROUTER_LANE0:ROUTER_LANE0 + MOE_EXPERTS]
    lstart = seg[:, 1, ROUTER_LANE0:ROUTER_LANE0 + MOE_EXPERTS]
    ntt = cnt.shape[0]
    ng = (cnt + GRANULE - 1) // GRANULE
    cum = jnp.cumsum(ng, axis=0)
    first = (jnp.arange(ntt, dtype=I32)[:, None] * LOCAL_ROWS + lstart) // GRANULE - (cum - ng)
    total = cum[-1]
    tiles_e = (total + TILE_GRANULES - 1) // TILE_GRANULES
    tend = jnp.cumsum(tiles_e)
    tstart = tend - tiles_e
    tile = jnp.arange(n_tiles, dtype=I32)
    te = jnp.minimum(jnp.sum((tile[:, None] >= tend[None, :]).astype(I32), axis=1), MOE_EXPERTS - 1)
    n_used = tend[-1].astype(I32).reshape(1)
    sel = (te[:, None] == jnp.arange(MOE_EXPERTS, dtype=I32)[None, :]).astype(I32)
    pick = lambda table: jnp.sum(sel[:, None, :] * table[None, :, :], axis=2)
    cum_t, first_t = pick(cum), pick(first)
    off = (tile - jnp.sum(sel * tstart[None, :], axis=1))[:, None] * TILE_GRANULES + jnp.arange(TILE_GRANULES, dtype=I32)[None, :]
    valid = (off < jnp.sum(sel * total[None, :], axis=1)[:, None]) & (tile < n_used[0])[:, None]
    j = jnp.sum((cum_t[:, None, :] <= off[:, :, None]).astype(I32), axis=2)
    jsel = (j[:, :, None] == jnp.arange(ntt, dtype=I32)[None, None, :]).astype(I32)
    src = jnp.sum(jsel * first_t[:, None, :], axis=2) + off
    spare = lambda k: ((k // SPARE_GRANULES) * LOCAL_ROWS + LOCAL_USED) // GRANULE + k % SPARE_GRANULES
    assert (ntt - 1) * SPARE_GRANULES >= 2 * TILE_GRANULES
    zero_granule = spare((ntt - 1) * SPARE_GRANULES)
    trash = spare((tile % 2)[:, None] * TILE_GRANULES + jnp.arange(TILE_GRANULES, dtype=I32)[None, :])
    gsrc = jnp.where(valid, src, zero_granule).astype(I32).reshape(-1)
    gdst = jnp.where(valid, src, trash).astype(I32).reshape(-1)
    n_valid = jnp.sum(valid.astype(I32), axis=1).astype(I32)
    return te.astype(I32), n_used, gsrc, gdst, n_valid


def _grouped_kernel(te_ref, nu_ref, gsrc_ref, gdst_ref, nv_ref, xin_ref, wg_ref, wu_ref, wd_ref, xout_ref,
                    xbuf, ybuf, wg16, wu16, wd16, sem_in, sem_out):
    i = pl.program_id(0)
    nu = nu_ref[0]
    tm = MOE_ROW_TILE

    def hbm_rows(table_ref, tile, g):
        return pl.ds(pl.multiple_of(table_ref[tile * TILE_GRANULES + g] * GRANULE, GRANULE), GRANULE)

    def fetch_start(tile, slot):
        for g in range(TILE_GRANULES):
            pltpu.make_async_copy(xin_ref.at[hbm_rows(gsrc_ref, tile, g)],
                                  xbuf.at[slot, pl.ds(g * GRANULE, GRANULE)], sem_in.at[slot]).start()

    def store_start(tile, slot):
        for g in range(TILE_GRANULES):
            pltpu.make_async_copy(ybuf.at[slot, pl.ds(g * GRANULE, GRANULE)],
                                  xout_ref.at[hbm_rows(gdst_ref, tile, g)], sem_out.at[slot]).start()

    def fetch_wait(slot):
        pltpu.make_async_copy(xin_ref.at[pl.ds(0, tm)], xbuf.at[slot], sem_in.at[slot]).wait()

    def store_wait(slot):
        pltpu.make_async_copy(ybuf.at[slot], xout_ref.at[pl.ds(0, tm)], sem_out.at[slot]).wait()

    def step(slot):
        @pl.when(i == 0)
        def _():
            ybuf[...] = jnp.zeros_like(ybuf)
            fetch_start(0, 0)

        @pl.when(i + 1 < nu)
        def _():
            fetch_start(i + 1, 1 - slot)

        @pl.when(i >= 2)
        def _():
            store_wait(slot)

        @pl.when((i == 0) | (te_ref[i] != te_ref[jnp.maximum(i - 1, 0)]))
        def _():
            wg16[...] = wg_ref[0].astype(BF)
            wu16[...] = wu_ref[0].astype(BF)
            wd16[...] = wd_ref[0].astype(BF)

        fetch_wait(slot)
        n_valid = nv_ref[i]
        quarter = TILE_GRANULES // 4
        for part in range(1, 5):
            rows = part * quarter * GRANULE

            @pl.when((n_valid > (part - 1) * quarter) & (n_valid <= part * quarter))
            def _(rows=rows):
                x_lo, x_hi = _unpack_rows(xbuf[slot, 0:rows, :])
                gate = _dot(x_lo, wg16[0:HALF, :]) + _dot(x_hi, wg16[HALF:, :])
                up = _dot(x_lo, wu16[0:HALF, :]) + _dot(x_hi, wu16[HALF:, :])
                ybuf[slot, 0:rows, :] = _pack_rows(_dot((_silu(gate) * up).astype(BF), wd16[...]))

        store_start(i, slot)

        @pl.when(i == nu - 1)
        def _():
            store_wait(slot)

            @pl.when(i >= 1)
            def _():
                store_wait(1 - slot)

    for slot in (0, 1):
        @pl.when((i < nu) & (lax.rem(i, 2) == slot))
        def _(slot=slot):
            step(slot)


def _grouped(x_local, te, n_used, gsrc, gdst, n_valid, wg, wu, wd):
    tm = MOE_ROW_TILE
    n_tiles = te.shape[0]
    used = lambda i, nu: jnp.minimum(i, jnp.maximum(nu[0] - 1, 0))
    wspec = lambda shape: pl.BlockSpec((1,) + shape, lambda i, te, nu, *_: (te[used(i, nu)], 0, 0))
    grid_spec = pltpu.PrefetchScalarGridSpec(
        num_scalar_prefetch=5,
        grid=(n_tiles,),
        in_specs=[pl.BlockSpec(memory_space=pl.ANY), wspec((D_MODEL, MOE_D_FF)), wspec((D_MODEL, MOE_D_FF)),
                  wspec((MOE_D_FF, D_MODEL))],
        out_specs=pl.BlockSpec(memory_space=pl.ANY),
        scratch_shapes=[pltpu.VMEM((2, tm, HALF), U32), pltpu.VMEM((2, tm, HALF), U32),
                        pltpu.VMEM((D_MODEL, MOE_D_FF), BF), pltpu.VMEM((D_MODEL, MOE_D_FF), BF),
                        pltpu.VMEM((MOE_D_FF, D_MODEL), BF),
                        pltpu.SemaphoreType.DMA((2,)), pltpu.SemaphoreType.DMA((2,))],
    )
    return pl.pallas_call(
        _grouped_kernel,
        grid_spec=grid_spec,
        out_shape=jax.ShapeDtypeStruct(x_local.shape, x_local.dtype),
        input_output_aliases={5: 0},
        compiler_params=_params(1),
        name="grouped",
    )(te, n_used, gsrc, gdst, n_valid, x_local, wg, wu, wd)


def _merge_kernel(h_ref, idx_ref, wt_ref, g_ref, yl_ref, o_ref):
    tm = h_ref.shape[0]
    idx = idx_ref[...]
    wt = wt_ref[...]
    r_id = lax.broadcasted_iota(I32, (tm, LOCAL_ROWS), 1)
    sel = jnp.where(r_id == idx[:, 0:1], wt[:, 0:1], 0.0) + jnp.where(r_id == idx[:, 1:2], wt[:, 1:2], 0.0)
    y_lo, y_hi = _unpack_rows(yl_ref[...])
    sel16 = sel.astype(BF)
    moe = jnp.concatenate([_dot(sel16, y_lo), _dot(sel16, y_hi)], axis=1)
    o_ref[...] = _rms(h_ref[...] + moe, g_ref[...])


def _merge(h2d, idx, wt, g, y_local, *, tile_offset):
    t = h2d.shape[0]
    tm = TOKEN_TILE
    return pl.pallas_call(
        _merge_kernel,
        grid=(t // tm,),
        in_specs=[pl.BlockSpec((tm, D_MODEL), lambda i: (i, 0)),
                  pl.BlockSpec((tm, LANES), lambda i: (i + tile_offset, 0)),
                  pl.BlockSpec((tm, LANES), lambda i: (i + tile_offset, 0)),
                  _const_spec(g.shape),
                  pl.BlockSpec((LOCAL_ROWS, HALF), lambda i: (i + tile_offset, 0))],
        out_specs=pl.BlockSpec((tm, D_MODEL), lambda i: (i, 0)),
        out_shape=jax.ShapeDtypeStruct((t, D_MODEL), F32),
        compiler_params=_params(1),
        name="merge",
    )(h2d, idx, wt, g, y_local)


def _perm_matrix(bt, tl):
    m = bt * tl
    p = np.zeros((m, m), np.float32)
    for l in range(tl):
        for b in range(bt):
            p[l * bt + b, b * tl + l] = 1.0
    return p


def _s5_operators(lam_re, lam_im, log_step, b_re, b_im, c_re, c_im):
    lam = lax.complex(lam_re, lam_im)
    step = jnp.exp(log_step)[:, None]
    lam_bar = jnp.exp(lam * step)
    b_bar = ((lam_bar - 1.0) / lam)[..., None] * lax.complex(b_re, b_im)
    gl = LANES // S5_GROUP
    eye = jnp.eye(gl, dtype=F32)

    def in_block(bpart):
        bp = bpart.reshape(S5_SLABS, gl, S5_STATE, S5_GROUP)
        blk = jnp.einsum("sgnk,gh->sgkhn", bp, eye)
        return blk.reshape(S5_SLABS, LANES, gl * S5_STATE)

    def out_block(cpart):
        cp = cpart.reshape(S5_SLABS, gl, S5_GROUP, S5_STATE)
        blk = jnp.einsum("sgkn,gh->sgnhk", cp, eye)
        return blk.reshape(S5_SLABS, gl * S5_STATE, LANES)

    bblk = jnp.concatenate([in_block(jnp.real(b_bar)), in_block(jnp.imag(b_bar))], axis=2).astype(BF)
    cblk = jnp.concatenate([out_block(c_re), out_block(-c_im)], axis=1).astype(BF)
    lre = jnp.broadcast_to(jnp.real(lam_bar).reshape(1, S5_FLAT), (S5_SEQ_TILE, S5_FLAT))
    lim = jnp.broadcast_to(jnp.imag(lam_bar).reshape(1, S5_FLAT), (S5_SEQ_TILE, S5_FLAT))
    return bblk, cblk, lre, lim


def _pad_lanes(v, width=LANES):
    return jnp.pad(v, [(0, 0)] * (v.ndim - 1) + [(0, width - v.shape[-1])])


def kernel(x_prompt, x_sample, state_conv, state_ssd, state_s5_re, state_s5_im, cache_mem_k, cache_mem_v, mem_prompt, norm_mix_g, w_in, conv_w, conv_b, ssd_dt_bias, ssd_a_log, ssd_d, ssd_norm_g, w_ssd_branch, s5_lambda_re, s5_lambda_im, s5_log_step, s5_b_re, s5_b_im, s5_c_re, s5_c_im, s5_d, w_glu, b_glu, w_mix_out, norm_mem_q_g, norm_mem_kv_g, w_mem_q, w_mem_k, w_mem_v, w_mem_o, norm_ffn_g, w_router_group, b_router_group, w_router_expert, b_router_expert, w_exp_gate, w_exp_up, w_exp_down, norm_final_g):
    bp, lp, _ = x_prompt.shape
    bs, ls, _ = x_sample.shape
    tp, ts = bp * lp, bs * ls
    row1 = lambda v: v.reshape(1, -1).astype(F32)

    w_in0 = w_in[0]
    o_xbc = D_MODEL
    o_dt = o_xbc + SSD_CONV_DIM
    o_u5 = o_dt + SSD_HEADS
    w_main = jnp.concatenate([w_in0[:, :o_dt], w_in0[:, o_u5:]], axis=1).astype(BF)
    w_dt = _pad_lanes(w_in0[:, o_dt:o_u5]).astype(BF)
    dt_bias = _pad_lanes(row1(ssd_dt_bias[0]))
    a_log = _pad_lanes(row1(ssd_a_log[0]))
    dskip = jnp.repeat(ssd_d[0].astype(F32), SSD_HEADDIM).reshape(1, D_MODEL)
    eexp_np = np.zeros((LANES, D_MODEL), np.float32)
    for h in range(SSD_HEADS):
        eexp_np[h, h * SSD_HEADDIM:(h + 1) * SSD_HEADDIM] = 1.0
    eexp = jnp.asarray(eexp_np).astype(BF)
    eye16 = jnp.asarray(np.eye(SSD_HEADS, LANES, dtype=np.float32)).astype(BF)
    bblk, cblk, lre, lim = _s5_operators(s5_lambda_re[0], s5_lambda_im[0], s5_log_step[0], s5_b_re[0],
                                         s5_b_im[0], s5_c_re[0], s5_c_im[0])
    s5d = row1(s5_d[0])
    w_kv = jnp.concatenate([w_mem_k[0], w_mem_v[0]], axis=1).astype(BF)
    w_r = _pad_lanes(jnp.concatenate([w_router_group[0], w_router_expert[0]], axis=1).astype(F32))
    wr_h, wr_m, _ = _split3(w_r)
    b_r = _pad_lanes(row1(jnp.concatenate([b_router_group[0], b_router_expert[0]])))
    wg = w_exp_gate[0].reshape(MOE_EXPERTS, D_MODEL, MOE_D_FF)
    wu = w_exp_up[0].reshape(MOE_EXPERTS, D_MODEL, MOE_D_FF)
    wd = w_exp_down[0].reshape(MOE_EXPERTS, MOE_D_FF, D_MODEL)

    def mixer(x2d, batch, seqlen, conv0, h0, s5re0, s5im0):
        z, xbc, u5, ga, gb, dt = _inproj(x2d, row1(norm_mix_g[0]), w_main, w_dt, dt_bias)
        y_ssd, new_conv, new_ssd = _ssd(xbc, dt, conv_w[0].astype(F32), row1(conv_b[0]), a_log, dskip, eexp, eye16,
                                        conv0, h0, batch=batch, seqlen=seqlen)
        if h0 is None:
            tl = S5_TIME_TILE
            u_in = u5.reshape(batch, seqlen, D_MODEL)
        else:
            tl = seqlen
            u_in = u5
        perm = _perm_matrix(S5_SEQ_TILE, tl)
        y5, new_re, new_im = _s5(u_in, jnp.asarray(perm).astype(BF), jnp.asarray(perm.T).astype(BF), bblk, cblk,
                                 lre, lim, s5d, s5re0, s5im0, batch=batch, seqlen=seqlen, tl=tl)
        h1 = _mixout(y_ssd, z, y5.reshape(-1, D_MODEL), ga, gb, x2d, row1(ssd_norm_g[0]),
                     w_ssd_branch[0].astype(BF), w_glu[0].astype(BF), row1(b_glu[0]), w_mix_out[0].astype(BF))
        return h1, new_conv, new_ssd, new_re, new_im

    xp2 = x_prompt.reshape(tp, D_MODEL)
    xs2 = x_sample.reshape(ts, D_MODEL)
    h1p, conv_p, ssd_p, re_p, im_p = mixer(xp2, bp, lp, None, None, None, None)
    h1s, conv_s, ssd_s, re_s, im_s = mixer(xs2, bs, ls, state_conv[0], state_ssd[0],
                                           state_s5_re[0].reshape(bs, S5_FLAT), state_s5_im[0].reshape(bs, S5_FLAT))

    mk_p, mv_p = _memkv(mem_prompt.reshape(bp * MEM_LEN, D_MODEL), row1(norm_mem_kv_g[0]), w_kv)
    gq = row1(norm_mem_q_g[0])
    wq = w_mem_q[0].astype(BF)
    wo = w_mem_o[0].astype(BF)
    h2p = _attn(h1p, gq, wq, wo, mk_p.reshape(bp, MEM_LEN, D_MODEL), mv_p.reshape(bp, MEM_LEN, D_MODEL),
                rows=TOKEN_TILE, nkv=1, rows_per_seq=lp)
    sample_seqs = 8
    h2s = _attn(h1s, gq, wq, wo, cache_mem_k[0], cache_mem_v[0], rows=sample_seqs * ls, nkv=sample_seqs,
                rows_per_seq=ls)

    g_ffn = row1(norm_ffn_g[0])
    x_local, r_idx, r_wt, seg = _route(h2p, h2s, g_ffn, jnp.concatenate([wr_h, wr_m], axis=1), wr_h, b_r)
    token_tiles = (tp + ts) // TOKEN_TILE
    max_granules = (2 * (tp + ts) + token_tiles * MOE_EXPERTS * (GRANULE - 1)) // GRANULE
    n_tiles = max_granules // TILE_GRANULES + MOE_EXPERTS
    te, n_used, gsrc, gdst, n_valid = _granule_plan(seg, n_tiles)
    y_local = _grouped(x_local, te, n_used, gsrc, gdst, n_valid, wg, wu, wd)
    gf = row1(norm_final_g)
    y_prompt = _merge(h2p, r_idx, r_wt, gf, y_local, tile_offset=0)
    y_sample = _merge(h2s, r_idx, r_wt, gf, y_local, tile_offset=tp // TOKEN_TILE)

    return (y_prompt.reshape(bp, lp, D_MODEL), y_sample.reshape(bs, ls, D_MODEL),
            conv_p[None], ssd_p[None],
            re_p.reshape(1, bp, S5_GROUPS, S5_STATE), im_p.reshape(1, bp, S5_GROUPS, S5_STATE),
            mk_p.reshape(1, bp, MEM_LEN, MEM_HEADS, MEM_HEAD_DIM), mv_p.reshape(1, bp, MEM_LEN, MEM_HEADS, MEM_HEAD_DIM),
            conv_s[None], ssd_s[None],
            re_s.reshape(1, bs, S5_GROUPS, S5_STATE), im_s.reshape(1, bs, S5_GROUPS, S5_STATE))
```

```python
import functools
import math

import numpy as np
import jax
import jax.numpy as jnp
from jax import lax
from jax.experimental import pallas as pl
from jax.experimental.pallas import tpu as pltpu

F32 = jnp.float32
BF = jnp.bfloat16
I32 = jnp.int32

D_MODEL = 1024
SSD_HEADS = 16
SSD_HEADDIM = 64
SSD_GROUPS = 2
SSD_STATE = 128
SSD_CONV = 4
SSD_CONV_DIM = 1536
HEADS_PER_GROUP = SSD_HEADS // SSD_GROUPS
S5_GROUPS = 64
S5_GROUP = 16
S5_STATE = 64
S5_FLAT = S5_GROUPS * S5_STATE
S5_SLABS = D_MODEL // 128
MEM_LEN = 256
MEM_HEADS = 4
MEM_HEAD_DIM = 256
MOE_GROUPS = 4
MOE_EXPERTS_PER_GROUP = 8
MOE_EXPERTS = MOE_GROUPS * MOE_EXPERTS_PER_GROUP
MOE_D_FF = 512
NORM_EPS = 1e-6

LANES = 128
VMEM_LIMIT_BYTES = 56 * 1024 * 1024
TOKEN_TILE = 512
MIX_TILE = 1024
SSD_CHUNK = 128
SSD_SEQS_PER_STEP = 2
S5_TIME_TILE = 64
MOE_ROW_TILE = 512
NEG_BIG = -1e30


def _params(n_axes):
    return pltpu.CompilerParams(dimension_semantics=("arbitrary",) * n_axes,
                                vmem_limit_bytes=VMEM_LIMIT_BYTES)


def _const_spec(shape):
    nd = len(shape)
    return pl.BlockSpec(shape, lambda *_: (0,) * nd)


def _resident_spec(shape):
    nd = len(shape)
    return pl.BlockSpec(shape, lambda *_: (0,) * nd, pipeline_mode=pl.Buffered(1))


def _dot(a, b):
    return jnp.dot(a, b, preferred_element_type=F32)


def _dot_nt(a, b):
    return lax.dot_general(a, b, (((1,), (1,)), ((), ())), preferred_element_type=F32)


def _split3(v):
    h = v.astype(BF)
    r = v - h.astype(F32)
    m = r.astype(BF)
    l = (r - m.astype(F32)).astype(BF)
    return h, m, l


def _dot_exact(a_bf, v, nt=False):
    f = _dot_nt if nt else _dot
    h, m, l = _split3(v)
    return f(a_bf, h) + f(a_bf, m) + f(a_bf, l)


def _rms(x, g):
    return x * lax.rsqrt(jnp.mean(x * x, axis=-1, keepdims=True) + NORM_EPS) * g


def _sigmoid(x):
    return 0.5 * jnp.tanh(0.5 * x) + 0.5


def _silu(x):
    return x * _sigmoid(x)


def _proj_cols(xb, w_ref, o_ref, c0, width):
    for c in range(0, width, 512):
        ce = min(c + 512, width)
        o_ref[:, c:ce] = _dot(xb, w_ref[:, c0 + c:c0 + ce]).astype(o_ref.dtype)


def _inproj_kernel(x_ref, g_ref, w_ref, wdt_ref, dtb_ref, z_ref, xbc_ref, u5_ref, ga_ref, gb_ref, dt_ref):
    xb = _rms(x_ref[...], g_ref[...]).astype(BF)
    c0 = 0
    for o_ref in (z_ref, xbc_ref, u5_ref, ga_ref, gb_ref):
        width = o_ref.shape[1]
        _proj_cols(xb, w_ref, o_ref, c0, width)
        c0 += width
    raw = _dot(xb, wdt_ref[...]) + dtb_ref[...]
    dt_ref[...] = jnp.maximum(raw, 0.0) + jnp.log1p(jnp.exp(-jnp.abs(raw)))


def _inproj(x2d, g, w_main, w_dt, dt_bias):
    t = x2d.shape[0]
    tm = MIX_TILE
    widths = (D_MODEL, SSD_CONV_DIM, D_MODEL, D_MODEL, D_MODEL)
    row = lambda w: pl.BlockSpec((tm, w), lambda i: (i, 0))
    return pl.pallas_call(
        _inproj_kernel,
        grid=(t // tm,),
        in_specs=[row(D_MODEL), _const_spec(g.shape), _resident_spec(w_main.shape), _resident_spec(w_dt.shape),
                  _const_spec(dt_bias.shape)],
        out_specs=[row(w) for w in widths] + [row(LANES)],
        out_shape=[jax.ShapeDtypeStruct((t, w), BF) for w in widths] + [jax.ShapeDtypeStruct((t, LANES), F32)],
        compiler_params=_params(1),
        name="inproj",
    )(x2d, g, w_main, w_dt, dt_bias)


def _memkv_kernel(x_ref, g_ref, w_ref, k_ref, v_ref):
    xb = _rms(x_ref[...], g_ref[...]).astype(BF)
    _proj_cols(xb, w_ref, k_ref, 0, D_MODEL)
    _proj_cols(xb, w_ref, v_ref, D_MODEL, D_MODEL)


def _memkv(mem2d, g, w_kv):
    t = mem2d.shape[0]
    tm = TOKEN_TILE
    row = pl.BlockSpec((tm, D_MODEL), lambda i: (i, 0))
    return pl.pallas_call(
        _memkv_kernel,
        grid=(t // tm,),
        in_specs=[row, _const_spec(g.shape), _const_spec(w_kv.shape)],
        out_specs=[row, row],
        out_shape=[jax.ShapeDtypeStruct((t, D_MODEL), F32)] * 2,
        compiler_params=_params(1),
        name="memkv",
    )(mem2d, g, w_kv)


def _ssd_kernel(*refs, lq, has_h0, nb):
    nseq = SSD_CHUNK // lq
    n_in = 11 if has_h0 else 9
    xbc_ref, dt_ref = refs[0], refs[1]
    consts = refs[2:9]
    state_in = refs[9:n_in]
    y_ref, convo_ref, ho_ref = refs[n_in:n_in + 3]
    hbuf, cacc, yacc, yint, xwt, tot_s = refs[n_in + 3:]
    for s in range(nb):
        seqs = pl.ds(s * nseq, nseq)
        _ssd_chunk(xbc_ref.at[s], dt_ref.at[s], *consts, *[r.at[seqs] for r in state_in],
                   y_ref.at[s], convo_ref.at[seqs], ho_ref.at[seqs], hbuf.at[seqs],
                   cacc.at[s], yacc.at[s], yint.at[s], xwt.at[s], tot_s.at[s], lq=lq, has_h0=has_h0)


def _ssd_chunk(*refs, lq, has_h0):
    q = SSD_CHUNK
    nseq = q // lq
    if has_h0:
        (xbc_ref, dt_ref, cw_ref, cb_ref, alog_ref, dsk_ref, eexp_ref, eye_ref, shift_ref, conv0_ref, h0_ref,
         y_ref, convo_ref, ho_ref, hbuf, cacc, yacc, yint, xwt, tot_s) = refs
        hin_ref = h0_ref
        hbuf[...] = jnp.zeros_like(hbuf)
        hbuf[:, 0:SSD_CONV - 1, :] = conv0_ref[...]
    else:
        (xbc_ref, dt_ref, cw_ref, cb_ref, alog_ref, dsk_ref, eexp_ref, eye_ref, shift_ref,
         y_ref, convo_ref, ho_ref, hbuf, cacc, yacc, yint, xwt, tot_s) = refs
        hin_ref = ho_ref

        @pl.when(pl.program_id(1) == 0)
        def _():
            ho_ref[...] = jnp.zeros_like(ho_ref)
            hbuf[...] = jnp.zeros_like(hbuf)

    x16 = xbc_ref[...]
    xraw = x16.astype(F32)
    acc = cb_ref[...] + xraw * cw_ref[SSD_CONV - 1:SSD_CONV, :]
    for k in range(SSD_CONV - 1):
        acc = acc + _dot(shift_ref[k], x16) * cw_ref[k:k + 1, :]
    cacc[...] = acc
    for i in range(nseq):
        tail = xraw[(i + 1) * lq - (SSD_CONV - 1):(i + 1) * lq]
        convo_ref[i] = tail
        corr = hbuf[i, 0:8, :] * cw_ref[0:1, :]
        for k in range(1, SSD_CONV - 1):
            corr = corr + hbuf[i, k:k + 8, :] * cw_ref[k:k + 1, :]
        cacc[i * lq:i * lq + 8, :] += corr
        if not has_h0:
            hbuf[i, 0:SSD_CONV - 1, :] = tail
    xc = _silu(cacc[...])
    xs = xc[:, :D_MODEL]
    xs_bf = xs.astype(BF)
    bm_bf = xc[:, D_MODEL:D_MODEL + SSD_GROUPS * SSD_STATE].astype(BF)
    cm = xc[:, D_MODEL + SSD_GROUPS * SSD_STATE:]
    cm_bf = cm.astype(BF)

    dt = dt_ref[...]
    da = dt * (-jnp.exp(alog_ref[...]))
    ri = lax.broadcasted_iota(I32, (q, q), 0)
    ci = lax.broadcasted_iota(I32, (q, q), 1)
    if nseq == 1:
        causal = ci <= ri
    else:
        sh = int(math.log2(lq))
        same = lax.shift_right_logical(ri, sh) == lax.shift_right_logical(ci, sh)
        causal = same & (ci <= ri)
    lmat = jnp.where(causal, 1.0, 0.0).astype(BF)
    cs = _dot_exact(lmat, da)
    if nseq == 1:
        tot = jnp.broadcast_to(cs[q - 1:q, :], (q, LANES))
    else:
        tot = _dot_exact(jnp.where(same, 1.0, 0.0).astype(BF), da)
    tot_s[...] = tot
    eye = eye_ref[...]
    cs_t = _dot_exact(eye, cs, nt=True)
    dt_t = _dot_exact(eye, dt, nt=True)
    eexp = eexp_ref[...]
    ecs_h, ecs_m, _ = _split3(jnp.exp(cs))
    ecs_x = _dot(ecs_h, eexp) + _dot(ecs_m, eexp)
    wend_x = _dot((jnp.exp(tot - cs) * dt).astype(BF), eexp)

    lane = lax.broadcasted_iota(I32, (q, LANES), 1)
    for g in range(SSD_GROUPS):
        cbg = _dot_nt(cm_bf[:, g * SSD_STATE:(g + 1) * SSD_STATE], bm_bf[:, g * SSD_STATE:(g + 1) * SSD_STATE])
        for jp in range(HEADS_PER_GROUP // 2):
            j = g * (HEADS_PER_GROUP // 2) + jp
            ms = []
            for h in (2 * j, 2 * j + 1):
                col = jnp.broadcast_to(cs[:, h:h + 1], (q, q))
                row = jnp.broadcast_to(cs_t[h:h + 1, :], (q, q))
                dtr = jnp.broadcast_to(dt_t[h:h + 1, :], (q, q))
                ms.append((jnp.where(causal, jnp.exp(col - row), 0.0) * cbg * dtr).astype(BF))
            mp = jnp.concatenate(ms, axis=1)
            xp = xs_bf[:, j * LANES:(j + 1) * LANES]
            zero = jnp.zeros_like(xp)
            x2 = jnp.concatenate([jnp.where(lane < SSD_HEADDIM, xp, zero),
                                  jnp.where(lane >= SSD_HEADDIM, xp, zero)], axis=0)
            yacc[:, j * LANES:(j + 1) * LANES] = _dot(mp, x2)

    gw = HEADS_PER_GROUP * SSD_HEADDIM
    xw = xs * wend_x
    for g in range(SSD_GROUPS):
        xwt[g * gw:(g + 1) * gw, :] = xw[:, g * gw:(g + 1) * gw].T.astype(BF)

    if nseq == 1:
        for g in range(SSD_GROUPS):
            hin_g = hin_ref[0, g * HEADS_PER_GROUP:(g + 1) * HEADS_PER_GROUP].reshape(gw, SSD_STATE)
            yint[:, g * gw:(g + 1) * gw] = _dot_nt(cm_bf[:, g * SSD_STATE:(g + 1) * SSD_STATE], hin_g.astype(BF))
            s_new = _dot(xwt[g * gw:(g + 1) * gw, :], bm_bf[:, g * SSD_STATE:(g + 1) * SSD_STATE])
            for hl in range(HEADS_PER_GROUP):
                h = g * HEADS_PER_GROUP + hl
                dec = jnp.exp(jnp.broadcast_to(tot[0:1, h:h + 1], (SSD_HEADDIM, SSD_STATE)))
                ho_ref[0, h] = dec * hin_ref[0, h] + s_new[hl * SSD_HEADDIM:(hl + 1) * SSD_HEADDIM]
    else:
        yint[...] = jnp.zeros_like(yint)
        sh = int(math.log2(lq))
        rowseq = lax.shift_right_logical(lax.broadcasted_iota(I32, (q, SSD_STATE), 0), sh)
        colseq = lax.shift_right_logical(lax.broadcasted_iota(I32, (gw, q), 1), sh)

        def seq_body(i, carry):
            trow = tot_s[pl.ds(i * lq, 1), :]
            for g in range(SSD_GROUPS):
                hin_g = hin_ref[i, g * HEADS_PER_GROUP:(g + 1) * HEADS_PER_GROUP].reshape(gw, SSD_STATE)
                cmg = cm[:, g * SSD_STATE:(g + 1) * SSD_STATE]
                lhs = jnp.where(rowseq == i, cmg, 0.0).astype(BF)
                yint[:, g * gw:(g + 1) * gw] += _dot_nt(lhs, hin_g.astype(BF))
                xg = xwt[g * gw:(g + 1) * gw, :]
                xi = jnp.where(colseq == i, xg, jnp.zeros_like(xg))
                s_new = _dot(xi, bm_bf[:, g * SSD_STATE:(g + 1) * SSD_STATE])
                for hl in range(HEADS_PER_GROUP):
                    h = g * HEADS_PER_GROUP + hl
                    dec = jnp.exp(jnp.broadcast_to(trow[:, h:h + 1], (SSD_HEADDIM, SSD_STATE)))
                    ho_ref[i, h] = dec * hin_ref[i, h] + s_new[hl * SSD_HEADDIM:(hl + 1) * SSD_HEADDIM]
            return carry

        lax.fori_loop(0, nseq, seq_body, 0, unroll=2)

    y_ref[...] = (yacc[...] + yint[...] * ecs_x + dsk_ref[...] * xs).astype(y_ref.dtype)


def _ssd(xbc, dt, conv_w, conv_b, a_log, dskip, eexp, eye, conv0, h0, *, batch, seqlen):
    q = SSD_CHUNK
    t = xbc.shape[0]
    has_h0 = h0 is not None
    if has_h0:
        lq = seqlen
        nseq = q // lq
        nb = 1
        lead = t // q
        grid = (t // q,)
        rmap = lambda i: (i, 0, 0)
        smap3 = lambda i: (i, 0, 0)
        smap4 = lambda i: (i, 0, 0, 0)
    else:
        lq = q
        nseq = 1
        nb = SSD_SEQS_PER_STEP
        lead = batch
        nc = seqlen // q
        grid = (batch // nb, nc)
        rmap = lambda b, c: (b, c, 0)
        smap3 = lambda b, c: (b, 0, 0)
        smap4 = lambda b, c: (b, 0, 0, 0)
    xbc = xbc.reshape(lead, t // lead, SSD_CONV_DIM)
    dt = dt.reshape(lead, t // lead, LANES)
    shifts = np.zeros((SSD_CONV - 1, q, q), np.float32)
    for k in range(SSD_CONV - 1):
        for r in range(q):
            if r % lq + k - (SSD_CONV - 1) >= 0:
                shifts[k, r, r + k - (SSD_CONV - 1)] = 1.0
    consts = [conv_w, conv_b, a_log, dskip, eexp, eye, jnp.asarray(shifts).astype(BF)]
    in_specs = [pl.BlockSpec((nb, q, SSD_CONV_DIM), rmap), pl.BlockSpec((nb, q, LANES), rmap)]
    in_specs += [_const_spec(c.shape) for c in consts]
    args = [xbc, dt] + consts
    ns = nb * nseq
    if has_h0:
        in_specs += [pl.BlockSpec((ns, SSD_CONV - 1, SSD_CONV_DIM), smap3),
                     pl.BlockSpec((ns, SSD_HEADS, SSD_HEADDIM, SSD_STATE), smap4)]
        args += [conv0, h0]
    y, new_conv, new_state = pl.pallas_call(
        functools.partial(_ssd_kernel, lq=lq, has_h0=has_h0, nb=nb),
        grid=grid,
        in_specs=in_specs,
        out_specs=[pl.BlockSpec((nb, q, D_MODEL), rmap),
                   pl.BlockSpec((ns, SSD_CONV - 1, SSD_CONV_DIM), smap3),
                   pl.BlockSpec((ns, SSD_HEADS, SSD_HEADDIM, SSD_STATE), smap4)],
        out_shape=[jax.ShapeDtypeStruct((lead, t // lead, D_MODEL), BF),
                   jax.ShapeDtypeStruct((batch, SSD_CONV - 1, SSD_CONV_DIM), F32),
                   jax.ShapeDtypeStruct((batch, SSD_HEADS, SSD_HEADDIM, SSD_STATE), F32)],
        scratch_shapes=[pltpu.VMEM((ns, 16, SSD_CONV_DIM), F32),
                        pltpu.VMEM((nb, q, SSD_CONV_DIM), F32),
                        pltpu.VMEM((nb, q, D_MODEL), F32),
                        pltpu.VMEM((nb, q, D_MODEL), F32),
                        pltpu.VMEM((nb, D_MODEL, q), BF),
                        pltpu.VMEM((nb, q, LANES), F32)],
        compiler_params=_params(len(grid)),
        name="ssd_sample" if has_h0 else "ssd_prompt",
    )(*args)
    return y.reshape(t, D_MODEL), new_conv, new_state


S5_SEQ_TILE = 8


def _s5_kernel(*refs, tl, has_h0):
    bt = S5_SEQ_TILE
    m = bt * tl
    if has_h0:
        (u_ref, perm_ref, permt_ref, bblk_ref, cblk_ref, lre_ref, lim_ref, d_ref, h0re_ref, h0im_ref,
         y_ref, ore_ref, oim_ref, bre, bim, sre, sim, utm, ytm) = refs
    else:
        (u_ref, perm_ref, permt_ref, bblk_ref, cblk_ref, lre_ref, lim_ref, d_ref,
         y_ref, ore_ref, oim_ref, bre, bim, sre, sim, utm, ytm) = refs

    @pl.when(pl.program_id(1) == 0)
    def _():
        if has_h0:
            sre[...] = h0re_ref[...]
            sim[...] = h0im_ref[...]
        else:
            sre[...] = jnp.zeros_like(sre)
            sim[...] = jnp.zeros_like(sim)

    strided = len(u_ref.shape) == 3
    if strided:
        for b in range(bt):
            ub = u_ref[b].astype(F32)
            for s in range(S5_SLABS):
                utm[s, pl.ds(b, tl, stride=bt), :] = ub[:, s * LANES:(s + 1) * LANES]
        slab_in = lambda j: utm[j].astype(BF)
    else:
        u = u_ref[...]
        u_tm = _dot(perm_ref[...], u).astype(BF)
        slab_in = lambda j: u_tm[:, j * LANES:(j + 1) * LANES]
    half = S5_FLAT // S5_SLABS
    ys = []
    for j in range(S5_SLABS):
        sl = slice(j * half, (j + 1) * half)
        r = _dot(slab_in(j), bblk_ref[j])
        bre[:, sl] = r[:, :half]
        bim[:, sl] = r[:, half:]
        lr = lre_ref[:, sl]
        li = lim_ref[:, sl]
        sr = sre[:, sl]
        si = sim[:, sl]
        for l in range(tl):
            rows = slice(l * bt, (l + 1) * bt)
            sr, si = lr * sr - li * si + bre[rows, sl], lr * si + li * sr + bim[rows, sl]
            bre[rows, sl] = sr
            bim[rows, sl] = si
        sre[:, sl] = sr
        sim[:, sl] = si
        st = jnp.concatenate([bre[:, sl], bim[:, sl]], axis=1)
        ys.append(_dot(st.astype(BF), cblk_ref[j]))
    if strided:
        for j in range(S5_SLABS):
            ytm[j] = ys[j]
        for b in range(bt):
            yb = jnp.concatenate([ytm[s, pl.ds(b, tl, stride=bt), :] for s in range(S5_SLABS)], axis=1)
            y_ref[b] = (yb + d_ref[...] * u_ref[b].astype(F32)).astype(y_ref.dtype)
    else:
        y_tm = jnp.concatenate(ys, axis=1).astype(BF)
        y_ref[...] = (_dot(permt_ref[...], y_tm) + d_ref[...] * u.astype(F32)).astype(y_ref.dtype)
    ore_ref[...] = sre[...]
    oim_ref[...] = sim[...]


def _s5(u, perm, permt, bblk, cblk, lre, lim, dvec, h0re, h0im, *, batch, seqlen, tl):
    bt = S5_SEQ_TILE
    m = bt * tl
    has_h0 = h0re is not None
    nb = batch // bt
    nt = seqlen // tl
    if u.ndim == 3:
        u_spec = pl.BlockSpec((bt, tl, D_MODEL), lambda b, t: (b, t, 0))
    else:
        u_spec = pl.BlockSpec((m, D_MODEL), lambda b, t: (b * nt + t, 0))
    st_spec = pl.BlockSpec((bt, S5_FLAT), lambda b, t: (b, 0))
    consts = [perm, permt, bblk, cblk, lre, lim, dvec]
    in_specs = [u_spec] + [_const_spec(c.shape) for c in consts]
    args = [u] + consts
    if has_h0:
        in_specs += [st_spec, st_spec]
        args += [h0re, h0im]
    return pl.pallas_call(
        functools.partial(_s5_kernel, tl=tl, has_h0=has_h0),
        grid=(nb, nt),
        in_specs=in_specs,
        out_specs=[u_spec, st_spec, st_spec],
        out_shape=[jax.ShapeDtypeStruct(u.shape, BF),
                   jax.ShapeDtypeStruct((batch, S5_FLAT), F32),
                   jax.ShapeDtypeStruct((batch, S5_FLAT), F32)],
        scratch_shapes=[pltpu.VMEM((m, S5_FLAT), F32), pltpu.VMEM((m, S5_FLAT), F32),
                        pltpu.VMEM((bt, S5_FLAT), F32), pltpu.VMEM((bt, S5_FLAT), F32),
                        pltpu.VMEM((S5_SLABS, m, LANES), F32), pltpu.VMEM((S5_SLABS, m, LANES), F32)],
        compiler_params=_params(2),
        name="s5_sample" if has_h0 else "s5_prompt",
    )(*args)


def _mixout_kernel(y_ref, z_ref, y5_ref, ga_ref, gb_ref, x_ref, ng_ref, wa_ref, wglu_ref, bglu_ref, wmix_ref, o_ref):
    y = y_ref[...].astype(F32) * _silu(z_ref[...].astype(F32))
    yn = _rms(y, ng_ref[...]).astype(BF)
    branch_a = _dot(yn, wa_ref[...])
    glu = _dot(y5_ref[...], wglu_ref[...]) + bglu_ref[...]
    branch_b = glu[:, :D_MODEL] * _sigmoid(glu[:, D_MODEL:])
    merged = _sigmoid(ga_ref[...].astype(F32)) * branch_a + _sigmoid(gb_ref[...].astype(F32)) * branch_b
    o_ref[...] = x_ref[...] + _dot(merged.astype(BF), wmix_ref[...])


def _mixout(y, z, y5, ga, gb, x2d, ng, wa, wglu, bglu, wmix):
    t = x2d.shape[0]
    tm = MIX_TILE
    row = pl.BlockSpec((tm, D_MODEL), lambda i: (i, 0))
    consts = [ng, wa, wglu, bglu, wmix]
    return pl.pallas_call(
        _mixout_kernel,
        grid=(t // tm,),
        in_specs=[row] * 6 + [_resident_spec(c.shape) for c in consts],
        out_specs=row,
        out_shape=jax.ShapeDtypeStruct((t, D_MODEL), F32),
        compiler_params=_params(1),
        name="mixout",
    )(y, z, y5, ga, gb, x2d, *consts)


def _attn_kernel(h_ref, g_ref, wq_ref, wo_ref, k_ref, v_ref, o_ref, obuf, *, nkv, rows_per_seq):
    h = h_ref[...]
    r = h.shape[0]
    q32 = _dot(_rms(h, g_ref[...]).astype(BF), wq_ref[...])
    scale = MEM_HEAD_DIM ** -0.5
    if len(k_ref.shape) == 4:
        rq = rows_per_seq
        nrow = MEM_HEADS * rq
        row_head = lax.shift_right_logical(lax.broadcasted_iota(I32, (nrow, MEM_LEN * MEM_HEADS), 0), int(math.log2(rq)))
        col_head = lax.broadcasted_iota(I32, (nrow, MEM_LEN * MEM_HEADS), 1) & (MEM_HEADS - 1)
        visible = row_head == col_head
        for i in range(nkv):
            qs = jnp.concatenate([q32[i * rq:(i + 1) * rq, hd * MEM_HEAD_DIM:(hd + 1) * MEM_HEAD_DIM]
                                  for hd in range(MEM_HEADS)], axis=0).astype(BF)
            kf = k_ref[i].reshape(MEM_LEN * MEM_HEADS, MEM_HEAD_DIM).astype(BF)
            vf = v_ref[i].reshape(MEM_LEN * MEM_HEADS, MEM_HEAD_DIM).astype(BF)
            s = jnp.where(visible, _dot_nt(qs, kf) * scale, NEG_BIG)
            p = jnp.exp(s - jnp.max(s, axis=-1, keepdims=True))
            p = p / jnp.sum(p, axis=-1, keepdims=True)
            o = _dot(p.astype(BF), vf)
            for hd in range(MEM_HEADS):
                obuf[i * rq:(i + 1) * rq, hd * MEM_HEAD_DIM:(hd + 1) * MEM_HEAD_DIM] = o[hd * rq:(hd + 1) * rq]
    else:
        qv = q32.astype(BF)
        for hd in range(MEM_HEADS):
            sl = slice(hd * MEM_HEAD_DIM, (hd + 1) * MEM_HEAD_DIM)
            s = _dot_nt(qv[:, sl], k_ref[0, :, sl].astype(BF)) * scale
            p = jnp.exp(s - jnp.max(s, axis=-1, keepdims=True))
            p = p / jnp.sum(p, axis=-1, keepdims=True)
            obuf[:, sl] = _dot(p.astype(BF), v_ref[0, :, sl].astype(BF))
    o_ref[...] = h + _dot(obuf[...].astype(BF), wo_ref[...])


def _attn(h2d, g, wq, wo, k3, v3, *, rows, nkv, rows_per_seq):
    t = h2d.shape[0]
    nsteps = t // rows
    steps_per_kv = nsteps // (k3.shape[0] // nkv)
    row = pl.BlockSpec((rows, D_MODEL), lambda i: (i, 0))
    if k3.ndim == 4:
        kv = pl.BlockSpec((nkv, MEM_LEN, MEM_HEADS, MEM_HEAD_DIM), lambda i: (i // steps_per_kv, 0, 0, 0))
    else:
        kv = pl.BlockSpec((nkv, MEM_LEN, D_MODEL), lambda i: (i // steps_per_kv, 0, 0))
    consts = [g, wq, wo]
    return pl.pallas_call(
        functools.partial(_attn_kernel, nkv=nkv, rows_per_seq=rows_per_seq),
        grid=(nsteps,),
        in_specs=[row] + [_const_spec(c.shape) for c in consts] + [kv, kv],
        out_specs=row,
        out_shape=jax.ShapeDtypeStruct((t, D_MODEL), F32),
        scratch_shapes=[pltpu.VMEM((rows, D_MODEL), F32)],
        compiler_params=_params(1),
        name="attn",
    )(h2d, *consts, k3, v3)


ROUTER_LANE0 = MOE_GROUPS

U32 = jnp.uint32
HALF = D_MODEL // 2


def _pack_rows(x):
    xr = x.astype(BF).astype(F32)
    lo = lax.shift_right_logical(pltpu.bitcast(xr[:, :HALF], U32), jnp.uint32(16))
    hi = pltpu.bitcast(xr[:, HALF:], U32) & jnp.uint32(0xFFFF0000)
    return hi | lo


def _unpack_rows(w):
    lo = pltpu.bitcast(lax.shift_left(w, jnp.uint32(16)), F32).astype(BF)
    hi = pltpu.bitcast(w & jnp.uint32(0xFFFF0000), F32).astype(BF)
    return lo, hi


GRANULE = 8
LOCAL_USED = 2 * TOKEN_TILE + MOE_EXPERTS * (GRANULE - 1)
LOCAL_ROWS = 1280
SPARE_GRANULES = (LOCAL_ROWS - LOCAL_USED) // GRANULE
TILE_GRANULES = MOE_ROW_TILE // GRANULE


def _route_kernel(hp_ref, hs_ref, g_ref, wcat_ref, wh_ref, b_ref, tri_ref, eye_ref, upper_ref,
                  xl_ref, idx_ref, wt_ref, seg_ref, *, np_steps):
    i = pl.program_id(0)
    h = jnp.where(i < np_steps, hp_ref[...], hs_ref[...])
    xn = _rms(h, g_ref[...])
    tm = xn.shape[0]
    xh = xn.astype(BF)
    xm = (xn - xh.astype(F32)).astype(BF)
    both = _dot(xh, wcat_ref[...])
    logits = both[:, :LANES] + both[:, LANES:] + _dot(xm, wh_ref[...]) + b_ref[...]
    lane = lax.broadcasted_iota(I32, logits.shape, 1)
    lane_f = lane.astype(F32)
    no_lane = float(LANES)
    is_g = lane < MOE_GROUPS
    lg = jnp.where(is_g, logits, NEG_BIG)
    mg = jnp.max(lg, axis=-1, keepdims=True)
    gidx = jnp.min(jnp.where(lg == mg, lane_f, no_lane), axis=-1, keepdims=True).astype(I32)
    g_w = 1.0 / jnp.sum(jnp.where(is_g, jnp.exp(lg - mg), 0.0), axis=-1, keepdims=True)
    e_lane = lane - ROUTER_LANE0
    in_grp = ((e_lane >= 0) & (e_lane < MOE_EXPERTS)
              & (lax.shift_right_logical(jnp.maximum(e_lane, 0), 3) == gidx))
    le = jnp.where(in_grp, logits, NEG_BIG)
    m1 = jnp.max(le, axis=-1, keepdims=True)
    i1 = jnp.min(jnp.where(le == m1, lane_f, no_lane), axis=-1, keepdims=True)
    le2 = jnp.where(lane_f == i1, NEG_BIG, le)
    m2 = jnp.max(le2, axis=-1, keepdims=True)
    i2 = jnp.min(jnp.where(le2 == m2, lane_f, no_lane), axis=-1, keepdims=True)
    ratio = jnp.exp(m2 - m1)
    w1 = g_w / (1.0 + ratio)
    w2 = g_w * ratio / (1.0 + ratio)
    wt_ref[...] = jnp.where(lane == 0, w1, jnp.where(lane == 1, w2, 0.0))

    oh1 = jnp.where(lane_f == i1, 1.0, 0.0)
    oh2 = jnp.where(lane_f == i2, 1.0, 0.0)
    tot1 = jnp.sum(oh1, axis=0, keepdims=True)
    cnt = (tot1 + jnp.sum(oh2, axis=0, keepdims=True)).astype(I32)
    padded = lax.shift_left(lax.shift_right_logical(cnt + (GRANULE - 1), 3), 3)
    pad8 = jnp.broadcast_to(padded.astype(F32), (8, LANES)).astype(BF)
    lstart = _dot(pad8, upper_ref[...])[0:1, :]
    tri = tri_ref[...]
    l1 = jnp.sum(oh1 * (_dot(tri, oh1.astype(BF)) + lstart), axis=-1, keepdims=True)
    l2 = jnp.sum(oh2 * (_dot(tri, oh2.astype(BF)) + (lstart + tot1)), axis=-1, keepdims=True)
    idx_ref[...] = jnp.where(lane == 0, l1.astype(I32), jnp.where(lane == 1, l2.astype(I32), 0))
    sub = lax.broadcasted_iota(I32, (8, LANES), 0)
    seg_ref[0] = jnp.where(sub == 0, jnp.broadcast_to(cnt, (8, LANES)),
                           jnp.where(sub == 1, jnp.broadcast_to(lstart.astype(I32), (8, LANES)), 0))

    cols = jnp.where(lane == 0, l1, jnp.where(lane == 1, l2, 0.0))
    rows = _dot_exact(eye_ref[...], cols, nt=True)
    r_id = lax.broadcasted_iota(I32, (LOCAL_ROWS, tm), 0).astype(F32)
    perm = jnp.where(r_id == rows[0:1, :], 1.0, 0.0) + jnp.where(r_id == rows[1:2, :], 1.0, 0.0)
    xl_ref[...] = _pack_rows(_dot(perm.astype(BF), xh))


def _route(hp, hs, g, wcat, wh, b):
    tm = TOKEN_TILE
    np_steps = hp.shape[0] // tm
    ns_steps = hs.shape[0] // tm
    nt = np_steps + ns_steps
    tri = jnp.asarray(np.tril(np.ones((tm, tm), np.float32), -1)).astype(BF)
    eye8 = jnp.asarray(np.eye(8, LANES, dtype=np.float32)).astype(BF)
    upper = jnp.asarray(np.triu(np.ones((LANES, LANES), np.float32), 1)).astype(BF)
    consts = [g, wcat, wh, b, tri, eye8, upper]
    return pl.pallas_call(
        functools.partial(_route_kernel, np_steps=np_steps),
        grid=(nt,),
        in_specs=[pl.BlockSpec((tm, D_MODEL), lambda i: (jnp.minimum(i, np_steps - 1), 0)),
                  pl.BlockSpec((tm, D_MODEL), lambda i: (jnp.maximum(i - np_steps, 0), 0))]
                 + [_const_spec(c.shape) for c in consts],
        out_specs=[pl.BlockSpec((LOCAL_ROWS, HALF), lambda i: (i, 0)),
                   pl.BlockSpec((tm, LANES), lambda i: (i, 0)),
                   pl.BlockSpec((tm, LANES), lambda i: (i, 0)),
                   pl.BlockSpec((1, 8, LANES), lambda i: (i, 0, 0))],
        out_shape=[jax.ShapeDtypeStruct((nt * LOCAL_ROWS, HALF), U32),
                   jax.ShapeDtypeStruct((nt * tm, LANES), I32),
                   jax.ShapeDtypeStruct((nt * tm, LANES), F32),
                   jax.ShapeDtypeStruct((nt, 8, LANES), I32)],
        compiler_params=_params(1),
        name="route",
    )(hp, hs, *consts)


def _granule_plan(seg, n_tiles):
    cnt = seg[:, 0, ROUTER_LANE0:ROUTER_LANE0 + MOE_EXPERTS]
    lstart = seg[:, 1, ROUTER_LANE0:ROUTER_LANE0 + MOE_EXPERTS]
    ntt = cnt.shape[0]
    ng = (cnt + GRANULE - 1) // GRANULE
    cum = jnp.cumsum(ng, axis=0)
    first = (jnp.arange(ntt, dtype=I32)[:, None] * LOCAL_ROWS + lstart) // GRANULE - (cum - ng)
    total = cum[-1]
    tiles_e = (total + TILE_GRANULES - 1) // TILE_GRANULES
    tend = jnp.cumsum(tiles_e)
    tstart = tend - tiles_e
    tile = jnp.arange(n_tiles, dtype=I32)
    te = jnp.minimum(jnp.sum((tile[:, None] >= tend[None, :]).astype(I32), axis=1), MOE_EXPERTS - 1)
    n_used = tend[-1].astype(I32).reshape(1)
    sel = (te[:, None] == jnp.arange(MOE_EXPERTS, dtype=I32)[None, :]).astype(I32)
    pick = lambda table: jnp.sum(sel[:, None, :] * table[None, :, :], axis=2)
    cum_t, first_t = pick(cum), pick(first)
    off = (tile - jnp.sum(sel * tstart[None, :], axis=1))[:, None] * TILE_GRANULES + jnp.arange(TILE_GRANULES, dtype=I32)[None, :]
    valid = (off < jnp.sum(sel * total[None, :], axis=1)[:, None]) & (tile < n_used[0])[:, None]
    j = jnp.sum((cum_t[:, None, :] <= off[:, :, None]).astype(I32), axis=2)
    jsel = (j[:, :, None] == jnp.arange(ntt, dtype=I32)[None, None, :]).astype(I32)
    src = jnp.sum(jsel * first_t[:, None, :], axis=2) + off
    spare = lambda k: ((k // SPARE_GRANULES) * LOCAL_ROWS + LOCAL_USED) // GRANULE + k % SPARE_GRANULES
    assert (ntt - 1) * SPARE_GRANULES >= 2 * TILE_GRANULES
    zero_granule = spare((ntt - 1) * SPARE_GRANULES)
    trash = spare((tile % 2)[:, None] * TILE_GRANULES + jnp.arange(TILE_GRANULES, dtype=I32)[None, :])
    gsrc = jnp.where(valid, src, zero_granule).astype(I32).reshape(-1)
    gdst = jnp.where(valid, src, trash).astype(I32).reshape(-1)
    n_valid = jnp.sum(valid.astype(I32), axis=1).astype(I32)
    return te.astype(I32), n_used, gsrc, gdst, n_valid


def _grouped_kernel(te_ref, nu_ref, gsrc_ref, gdst_ref, nv_ref, xin_ref, wg_ref, wu_ref, wd_ref, xout_ref,
                    xbuf, ybuf, wg16, wu16, wd16, sem_in, sem_out):
    i = pl.program_id(0)
    nu = nu_ref[0]
    tm = MOE_ROW_TILE

    def hbm_rows(table_ref, tile, g):
        return pl.ds(pl.multiple_of(table_ref[tile * TILE_GRANULES + g] * GRANULE, GRANULE), GRANULE)

    def fetch_start(tile, slot):
        for g in range(TILE_GRANULES):
            pltpu.make_async_copy(xin_ref.at[hbm_rows(gsrc_ref, tile, g)],
                                  xbuf.at[slot, pl.ds(g * GRANULE, GRANULE)], sem_in.at[slot]).start()

    def store_start(tile, slot):
        for g in range(TILE_GRANULES):
            pltpu.make_async_copy(ybuf.at[slot, pl.ds(g * GRANULE, GRANULE)],
                                  xout_ref.at[hbm_rows(gdst_ref, tile, g)], sem_out.at[slot]).start()

    def fetch_wait(slot):
        pltpu.make_async_copy(xin_ref.at[pl.ds(0, tm)], xbuf.at[slot], sem_in.at[slot]).wait()

    def store_wait(slot):
        pltpu.make_async_copy(ybuf.at[slot], xout_ref.at[pl.ds(0, tm)], sem_out.at[slot]).wait()

    def step(slot):
        @pl.when(i == 0)
        def _():
            ybuf[...] = jnp.zeros_like(ybuf)
            fetch_start(0, 0)

        @pl.when(i + 1 < nu)
        def _():
            fetch_start(i + 1, 1 - slot)

        @pl.when(i >= 2)
        def _():
            store_wait(slot)

        @pl.when((i == 0) | (te_ref[i] != te_ref[jnp.maximum(i - 1, 0)]))
        def _():
            wg16[...] = wg_ref[0].astype(BF)
            wu16[...] = wu_ref[0].astype(BF)
            wd16[...] = wd_ref[0].astype(BF)

        fetch_wait(slot)
        n_valid = nv_ref[i]
        quarter = TILE_GRANULES // 4
        for part in range(1, 5):
            rows = part * quarter * GRANULE

            @pl.when((n_valid > (part - 1) * quarter) & (n_valid <= part * quarter))
            def _(rows=rows):
                x_lo, x_hi = _unpack_rows(xbuf[slot, 0:rows, :])
                gate = _dot(x_lo, wg16[0:HALF, :]) + _dot(x_hi, wg16[HALF:, :])
                up = _dot(x_lo, wu16[0:HALF, :]) + _dot(x_hi, wu16[HALF:, :])
                ybuf[slot, 0:rows, :] = _pack_rows(_dot((_silu(gate) * up).astype(BF), wd16[...]))

        store_start(i, slot)

        @pl.when(i == nu - 1)
        def _():
            store_wait(slot)

            @pl.when(i >= 1)
            def _():
                store_wait(1 - slot)

    for slot in (0, 1):
        @pl.when((i < nu) & (lax.rem(i, 2) == slot))
        def _(slot=slot):
            step(slot)


def _grouped(x_local, te, n_used, gsrc, gdst, n_valid, wg, wu, wd):
    tm = MOE_ROW_TILE
    n_tiles = te.shape[0]
    used = lambda i, nu: jnp.minimum(i, jnp.maximum(nu[0] - 1, 0))
    wspec = lambda shape: pl.BlockSpec((1,) + shape, lambda i, te, nu, *_: (te[used(i, nu)], 0, 0))
    grid_spec = pltpu.PrefetchScalarGridSpec(
        num_scalar_prefetch=5,
        grid=(n_tiles,),
        in_specs=[pl.BlockSpec(memory_space=pl.ANY), wspec((D_MODEL, MOE_D_FF)), wspec((D_MODEL, MOE_D_FF)),
                  wspec((MOE_D_FF, D_MODEL))],
        out_specs=pl.BlockSpec(memory_space=pl.ANY),
        scratch_shapes=[pltpu.VMEM((2, tm, HALF), U32), pltpu.VMEM((2, tm, HALF), U32),
                        pltpu.VMEM((D_MODEL, MOE_D_FF), BF), pltpu.VMEM((D_MODEL, MOE_D_FF), BF),
                        pltpu.VMEM((MOE_D_FF, D_MODEL), BF),
                        pltpu.SemaphoreType.DMA((2,)), pltpu.SemaphoreType.DMA((2,))],
    )
    return pl.pallas_call(
        _grouped_kernel,
        grid_spec=grid_spec,
        out_shape=jax.ShapeDtypeStruct(x_local.shape, x_local.dtype),
        input_output_aliases={5: 0},
        compiler_params=_params(1),
        name="grouped",
    )(te, n_used, gsrc, gdst, n_valid, x_local, wg, wu, wd)


def _merge_kernel(h_ref, idx_ref, wt_ref, g_ref, yl_ref, o_ref):
    tm = h_ref.shape[0]
    idx = idx_ref[...]
    wt = wt_ref[...]
    r_id = lax.broadcasted_iota(I32, (tm, LOCAL_ROWS), 1)
    sel = jnp.where(r_id == idx[:, 0:1], wt[:, 0:1], 0.0) + jnp.where(r_id == idx[:, 1:2], wt[:, 1:2], 0.0)
    y_lo, y_hi = _unpack_rows(yl_ref[...])
    sel16 = sel.astype(BF)
    moe = jnp.concatenate([_dot(sel16, y_lo), _dot(sel16, y_hi)], axis=1)
    o_ref[...] = _rms(h_ref[...] + moe, g_ref[...])


def _merge(h2d, idx, wt, g, y_local, *, tile_offset):
    t = h2d.shape[0]
    tm = TOKEN_TILE
    return pl.pallas_call(
        _merge_kernel,
        grid=(t // tm,),
        in_specs=[pl.BlockSpec((tm, D_MODEL), lambda i: (i, 0)),
                  pl.BlockSpec((tm, LANES), lambda i: (i + tile_offset, 0)),
                  pl.BlockSpec((tm, LANES), lambda i: (i + tile_offset, 0)),
                  _const_spec(g.shape),
                  pl.BlockSpec((LOCAL_ROWS, HALF), lambda i: (i + tile_offset, 0))],
        out_specs=pl.BlockSpec((tm, D_MODEL), lambda i: (i, 0)),
        out_shape=jax.ShapeDtypeStruct((t, D_MODEL), F32),
        compiler_params=_params(1),
        name="merge",
    )(h2d, idx, wt, g, y_local)


def _perm_matrix(bt, tl):
    m = bt * tl
    p = np.zeros((m, m), np.float32)
    for l in range(tl):
        for b in range(bt):
            p[l * bt + b, b * tl + l] = 1.0
    return p


def _s5_operators(lam_re, lam_im, log_step, b_re, b_im, c_re, c_im):
    lam = lax.complex(lam_re, lam_im)
    step = jnp.exp(log_step)[:, None]
    lam_bar = jnp.exp(lam * step)
    b_bar = ((lam_bar - 1.0) / lam)[..., None] * lax.complex(b_re, b_im)
    gl = LANES // S5_GROUP
    eye = jnp.eye(gl, dtype=F32)

    def in_block(bpart):
        bp = bpart.reshape(S5_SLABS, gl, S5_STATE, S5_GROUP)
        blk = jnp.einsum("sgnk,gh->sgkhn", bp, eye)
        return blk.reshape(S5_SLABS, LANES, gl * S5_STATE)

    def out_block(cpart):
        cp = cpart.reshape(S5_SLABS, gl, S5_GROUP, S5_STATE)
        blk = jnp.einsum("sgkn,gh->sgnhk", cp, eye)
        return blk.reshape(S5_SLABS, gl * S5_STATE, LANES)

    bblk = jnp.concatenate([in_block(jnp.real(b_bar)), in_block(jnp.imag(b_bar))], axis=2).astype(BF)
    cblk = jnp.concatenate([out_block(c_re), out_block(-c_im)], axis=1).astype(BF)
    lre = jnp.broadcast_to(jnp.real(lam_bar).reshape(1, S5_FLAT), (S5_SEQ_TILE, S5_FLAT))
    lim = jnp.broadcast_to(jnp.imag(lam_bar).reshape(1, S5_FLAT), (S5_SEQ_TILE, S5_FLAT))
    return bblk, cblk, lre, lim


def _pad_lanes(v, width=LANES):
    return jnp.pad(v, [(0, 0)] * (v.ndim - 1) + [(0, width - v.shape[-1])])


def kernel(x_prompt, x_sample, state_conv, state_ssd, state_s5_re, state_s5_im, cache_mem_k, cache_mem_v, mem_prompt, norm_mix_g, w_in, conv_w, conv_b, ssd_dt_bias, ssd_a_log, ssd_d, ssd_norm_g, w_ssd_branch, s5_lambda_re, s5_lambda_im, s5_log_step, s5_b_re, s5_b_im, s5_c_re, s5_c_im, s5_d, w_glu, b_glu, w_mix_out, norm_mem_q_g, norm_mem_kv_g, w_mem_q, w_mem_k, w_mem_v, w_mem_o, norm_ffn_g, w_router_group, b_router_group, w_router_expert, b_router_expert, w_exp_gate, w_exp_up, w_exp_down, norm_final_g):
    bp, lp, _ = x_prompt.shape
    bs, ls, _ = x_sample.shape
    tp, ts = bp * lp, bs * ls
    row1 = lambda v: v.reshape(1, -1).astype(F32)

    w_in0 = w_in[0]
    o_xbc = D_MODEL
    o_dt = o_xbc + SSD_CONV_DIM
    o_u5 = o_dt + SSD_HEADS
    w_main = jnp.concatenate([w_in0[:, :o_dt], w_in0[:, o_u5:]], axis=1).astype(BF)
    w_dt = _pad_lanes(w_in0[:, o_dt:o_u5]).astype(BF)
    dt_bias = _pad_lanes(row1(ssd_dt_bias[0]))
    a_log = _pad_lanes(row1(ssd_a_log[0]))
    dskip = jnp.repeat(ssd_d[0].astype(F32), SSD_HEADDIM).reshape(1, D_MODEL)
    eexp_np = np.zeros((LANES, D_MODEL), np.float32)
    for h in range(SSD_HEADS):
        eexp_np[h, h * SSD_HEADDIM:(h + 1) * SSD_HEADDIM] = 1.0
    eexp = jnp.asarray(eexp_np).astype(BF)
    eye16 = jnp.asarray(np.eye(SSD_HEADS, LANES, dtype=np.float32)).astype(BF)
    bblk, cblk, lre, lim = _s5_operators(s5_lambda_re[0], s5_lambda_im[0], s5_log_step[0], s5_b_re[0],
                                         s5_b_im[0], s5_c_re[0], s5_c_im[0])
    s5d = row1(s5_d[0])
    w_kv = jnp.concatenate([w_mem_k[0], w_mem_v[0]], axis=1).astype(BF)
    w_r = _pad_lanes(jnp.concatenate([w_router_group[0], w_router_expert[0]], axis=1).astype(F32))
    wr_h, wr_m, _ = _split3(w_r)
    b_r = _pad_lanes(row1(jnp.concatenate([b_router_group[0], b_router_expert[0]])))
    wg = w_exp_gate[0].reshape(MOE_EXPERTS, D_MODEL, MOE_D_FF)
    wu = w_exp_up[0].reshape(MOE_EXPERTS, D_MODEL, MOE_D_FF)
    wd = w_exp_down[0].reshape(MOE_EXPERTS, MOE_D_FF, D_MODEL)

    def mixer(x2d, batch, seqlen, conv0, h0, s5re0, s5im0):
        z, xbc, u5, ga, gb, dt = _inproj(x2d, row1(norm_mix_g[0]), w_main, w_dt, dt_bias)
        y_ssd, new_conv, new_ssd = _ssd(xbc, dt, conv_w[0].astype(F32), row1(conv_b[0]), a_log, dskip, eexp, eye16,
                                        conv0, h0, batch=batch, seqlen=seqlen)
        if h0 is None:
            tl = S5_TIME_TILE
            u_in = u5.reshape(batch, seqlen, D_MODEL)
        else:
            tl = seqlen
            u_in = u5
        perm = _perm_matrix(S5_SEQ_TILE, tl)
        y5, new_re, new_im = _s5(u_in, jnp.asarray(perm).astype(BF), jnp.asarray(perm.T).astype(BF), bblk, cblk,
                                 lre, lim, s5d, s5re0, s5im0, batch=batch, seqlen=seqlen, tl=tl)
        h1 = _mixout(y_ssd, z, y5.reshape(-1, D_MODEL), ga, gb, x2d, row1(ssd_norm_g[0]),
                     w_ssd_branch[0].astype(BF), w_glu[0].astype(BF), row1(b_glu[0]), w_mix_out[0].astype(BF))
        return h1, new_conv, new_ssd, new_re, new_im

    xp2 = x_prompt.reshape(tp, D_MODEL)
    xs2 = x_sample.reshape(ts, D_MODEL)
    h1p, conv_p, ssd_p, re_p, im_p = mixer(xp2, bp, lp, None, None, None, None)
    h1s, conv_s, ssd_s, re_s, im_s = mixer(xs2, bs, ls, state_conv[0], state_ssd[0],
                                           state_s5_re[0].reshape(bs, S5_FLAT), state_s5_im[0].reshape(bs, S5_FLAT))

    mk_p, mv_p = _memkv(mem_prompt.reshape(bp * MEM_LEN, D_MODEL), row1(norm_mem_kv_g[0]), w_kv)
    gq = row1(norm_mem_q_g[0])
    wq = w_mem_q[0].astype(BF)
    wo = w_mem_o[0].astype(BF)
    h2p = _attn(h1p, gq, wq, wo, mk_p.reshape(bp, MEM_LEN, D_MODEL), mv_p.reshape(bp, MEM_LEN, D_MODEL),
                rows=TOKEN_TILE, nkv=1, rows_per_seq=lp)
    sample_seqs = 8
    h2s = _attn(h1s, gq, wq, wo, cache_mem_k[0], cache_mem_v[0], rows=sample_seqs * ls, nkv=sample_seqs,
                rows_per_seq=ls)

    g_ffn = row1(norm_ffn_g[0])
    x_local, r_idx, r_wt, seg = _route(h2p, h2s, g_ffn, jnp.concatenate([wr_h, wr_m], axis=1), wr_h, b_r)
    token_tiles = (tp + ts) // TOKEN_TILE
    max_granules = (2 * (tp + ts) + token_tiles * MOE_EXPERTS * (GRANULE - 1)) // GRANULE
    n_tiles = max_granules // TILE_GRANULES + MOE_EXPERTS
    te, n_used, gsrc, gdst, n_valid = _granule_plan(seg, n_tiles)
    y_local = _grouped(x_local, te, n_used, gsrc, gdst, n_valid, wg, wu, wd)
    gf = row1(norm_final_g)
    y_prompt = _merge(h2p, r_idx, r_wt, gf, y_local, tile_offset=0)
    y_sample = _merge(h2s, r_idx, r_wt, gf, y_local, tile_offset=tp // TOKEN_TILE)

    return (y_prompt.reshape(bp, lp, D_MODEL), y_sample.reshape(bs, ls, D_MODEL),
            conv_p[None], ssd_p[None],
            re_p.reshape(1, bp, S5_GROUPS, S5_STATE), im_p.reshape(1, bp, S5_GROUPS, S5_STATE),
            mk_p.reshape(1, bp, MEM_LEN, MEM_HEADS, MEM_HEAD_DIM), mv_p.reshape(1, bp, MEM_LEN, MEM_HEADS, MEM_HEAD_DIM),
            conv_s[None], ssd_s[None],
            re_s.reshape(1, bs, S5_GROUPS, S5_STATE), im_s.reshape(1, bs, S5_GROUPS, S5_STATE))
```

```python
import functools
import math

import numpy as np
import jax
import jax.numpy as jnp
from jax import lax
from jax.experimental import pallas as pl
from jax.experimental.pallas import tpu as pltpu

F32 = jnp.float32
BF = jnp.bfloat16
I32 = jnp.int32

D_MODEL = 1024
SSD_HEADS = 16
SSD_HEADDIM = 64
SSD_GROUPS = 2
SSD_STATE = 128
SSD_CONV = 4
SSD_CONV_DIM = 1536
HEADS_PER_GROUP = SSD_HEADS // SSD_GROUPS
S5_GROUPS = 64
S5_GROUP = 16
S5_STATE = 64
S5_FLAT = S5_GROUPS * S5_STATE
S5_SLABS = D_MODEL // 128
MEM_LEN = 256
MEM_HEADS = 4
MEM_HEAD_DIM = 256
MOE_GROUPS = 4
MOE_EXPERTS_PER_GROUP = 8
MOE_EXPERTS = MOE_GROUPS * MOE_EXPERTS_PER_GROUP
MOE_D_FF = 512
NORM_EPS = 1e-6

LANES = 128
VMEM_LIMIT_BYTES = 56 * 1024 * 1024
TOKEN_TILE = 512
MIX_TILE = 1024
SSD_CHUNK = 128
SSD_SEQS_PER_STEP = 4
S5_TIME_TILE = 64
MOE_ROW_TILE = 512
NEG_BIG = -1e30


def _params(n_axes):
    return pltpu.CompilerParams(dimension_semantics=("arbitrary",) * n_axes,
                                vmem_limit_bytes=VMEM_LIMIT_BYTES)


def _const_spec(shape):
    nd = len(shape)
    return pl.BlockSpec(shape, lambda *_: (0,) * nd)


def _resident_spec(shape):
    nd = len(shape)
    return pl.BlockSpec(shape, lambda *_: (0,) * nd, pipeline_mode=pl.Buffered(1))


def _dot(a, b):
    return jnp.dot(a, b, preferred_element_type=F32)


def _dot_nt(a, b):
    return lax.dot_general(a, b, (((1,), (1,)), ((), ())), preferred_element_type=F32)


def _split3(v):
    h = v.astype(BF)
    r = v - h.astype(F32)
    m = r.astype(BF)
    l = (r - m.astype(F32)).astype(BF)
    return h, m, l


def _dot_exact(a_bf, v, nt=False):
    f = _dot_nt if nt else _dot
    h, m, l = _split3(v)
    return f(a_bf, h) + f(a_bf, m) + f(a_bf, l)


def _rms(x, g):
    return x * lax.rsqrt(jnp.mean(x * x, axis=-1, keepdims=True) + NORM_EPS) * g


def _sigmoid(x):
    return 0.5 * jnp.tanh(0.5 * x) + 0.5


def _silu(x):
    return x * _sigmoid(x)


def _proj_cols(xb, w_ref, o_ref, c0, width):
    for c in range(0, width, 512):
        ce = min(c + 512, width)
        o_ref[:, c:ce] = _dot(xb, w_ref[:, c0 + c:c0 + ce]).astype(o_ref.dtype)


def _inproj_kernel(x_ref, g_ref, w_ref, wdt_ref, dtb_ref, z_ref, xbc_ref, u5_ref, ga_ref, gb_ref, dt_ref):
    xb = _rms(x_ref[...], g_ref[...]).astype(BF)
    c0 = 0
    for o_ref in (z_ref, xbc_ref, u5_ref, ga_ref, gb_ref):
        width = o_ref.shape[1]
        _proj_cols(xb, w_ref, o_ref, c0, width)
        c0 += width
    raw = _dot(xb, wdt_ref[...]) + dtb_ref[...]
    dt_ref[...] = jnp.maximum(raw, 0.0) + jnp.log1p(jnp.exp(-jnp.abs(raw)))


def _inproj(x2d, g, w_main, w_dt, dt_bias):
    t = x2d.shape[0]
    tm = MIX_TILE
    widths = (D_MODEL, SSD_CONV_DIM, D_MODEL, D_MODEL, D_MODEL)
    row = lambda w: pl.BlockSpec((tm, w), lambda i: (i, 0))
    return pl.pallas_call(
        _inproj_kernel,
        grid=(t // tm,),
        in_specs=[row(D_MODEL), _const_spec(g.shape), _resident_spec(w_main.shape), _resident_spec(w_dt.shape),
                  _const_spec(dt_bias.shape)],
        out_specs=[row(w) for w in widths] + [row(LANES)],
        out_shape=[jax.ShapeDtypeStruct((t, w), BF) for w in widths] + [jax.ShapeDtypeStruct((t, LANES), F32)],
        compiler_params=_params(1),
        name="inproj",
    )(x2d, g, w_main, w_dt, dt_bias)


def _memkv_kernel(x_ref, g_ref, w_ref, k_ref, v_ref):
    xb = _rms(x_ref[...], g_ref[...]).astype(BF)
    _proj_cols(xb, w_ref, k_ref, 0, D_MODEL)
    _proj_cols(xb, w_ref, v_ref, D_MODEL, D_MODEL)


def _memkv(mem2d, g, w_kv):
    t = mem2d.shape[0]
    tm = TOKEN_TILE
    row = pl.BlockSpec((tm, D_MODEL), lambda i: (i, 0))
    return pl.pallas_call(
        _memkv_kernel,
        grid=(t // tm,),
        in_specs=[row, _const_spec(g.shape), _const_spec(w_kv.shape)],
        out_specs=[row, row],
        out_shape=[jax.ShapeDtypeStruct((t, D_MODEL), F32)] * 2,
        compiler_params=_params(1),
        name="memkv",
    )(mem2d, g, w_kv)


def _ssd_kernel(*refs, lq, has_h0, nb):
    nseq = SSD_CHUNK // lq
    n_in = 11 if has_h0 else 9
    xbc_ref, dt_ref = refs[0], refs[1]
    consts = refs[2:9]
    state_in = refs[9:n_in]
    y_ref, convo_ref, ho_ref = refs[n_in:n_in + 3]
    hbuf, cacc, yacc, yint, xwt, tot_s = refs[n_in + 3:]
    for s in range(nb):
        seqs = pl.ds(s * nseq, nseq)
        _ssd_chunk(xbc_ref.at[s], dt_ref.at[s], *consts, *[r.at[seqs] for r in state_in],
                   y_ref.at[s], convo_ref.at[seqs], ho_ref.at[seqs], hbuf.at[seqs],
                   cacc.at[s], yacc.at[s], yint.at[s], xwt.at[s], tot_s.at[s], lq=lq, has_h0=has_h0)


def _ssd_chunk(*refs, lq, has_h0):
    q = SSD_CHUNK
    nseq = q // lq
    if has_h0:
        (xbc_ref, dt_ref, cw_ref, cb_ref, alog_ref, dsk_ref, eexp_ref, eye_ref, shift_ref, conv0_ref, h0_ref,
         y_ref, convo_ref, ho_ref, hbuf, cacc, yacc, yint, xwt, tot_s) = refs
        hin_ref = h0_ref
        hbuf[...] = jnp.zeros_like(hbuf)
        hbuf[:, 0:SSD_CONV - 1, :] = conv0_ref[...]
    else:
        (xbc_ref, dt_ref, cw_ref, cb_ref, alog_ref, dsk_ref, eexp_ref, eye_ref, shift_ref,
         y_ref, convo_ref, ho_ref, hbuf, cacc, yacc, yint, xwt, tot_s) = refs
        hin_ref = ho_ref

        @pl.when(pl.program_id(1) == 0)
        def _():
            ho_ref[...] = jnp.zeros_like(ho_ref)
            hbuf[...] = jnp.zeros_like(hbuf)

    x16 = xbc_ref[...]
    xraw = x16.astype(F32)
    acc = cb_ref[...] + xraw * cw_ref[SSD_CONV - 1:SSD_CONV, :]
    for k in range(SSD_CONV - 1):
        acc = acc + _dot(shift_ref[k], x16) * cw_ref[k:k + 1, :]
    cacc[...] = acc
    for i in range(nseq):
        tail = xraw[(i + 1) * lq - (SSD_CONV - 1):(i + 1) * lq]
        convo_ref[i] = tail
        corr = hbuf[i, 0:8, :] * cw_ref[0:1, :]
        for k in range(1, SSD_CONV - 1):
            corr = corr + hbuf[i, k:k + 8, :] * cw_ref[k:k + 1, :]
        cacc[i * lq:i * lq + 8, :] += corr
        if not has_h0:
            hbuf[i, 0:SSD_CONV - 1, :] = tail
    xc = _silu(cacc[...])
    xs = xc[:, :D_MODEL]
    xs_bf = xs.astype(BF)
    bm_bf = xc[:, D_MODEL:D_MODEL + SSD_GROUPS * SSD_STATE].astype(BF)
    cm = xc[:, D_MODEL + SSD_GROUPS * SSD_STATE:]
    cm_bf = cm.astype(BF)

    dt = dt_ref[...]
    da = dt * (-jnp.exp(alog_ref[...]))
    ri = lax.broadcasted_iota(I32, (q, q), 0)
    ci = lax.broadcasted_iota(I32, (q, q), 1)
    if nseq == 1:
        causal = ci <= ri
    else:
        sh = int(math.log2(lq))
        same = lax.shift_right_logical(ri, sh) == lax.shift_right_logical(ci, sh)
        causal = same & (ci <= ri)
    lmat = jnp.where(causal, 1.0, 0.0).astype(BF)
    cs = _dot_exact(lmat, da)
    if nseq == 1:
        tot = jnp.broadcast_to(cs[q - 1:q, :], (q, LANES))
    else:
        tot = _dot_exact(jnp.where(same, 1.0, 0.0).astype(BF), da)
    tot_s[...] = tot
    eye = eye_ref[...]
    cs_t = _dot_exact(eye, cs, nt=True)
    dt_t = _dot_exact(eye, dt, nt=True)
    eexp = eexp_ref[...]
    ecs_h, ecs_m, _ = _split3(jnp.exp(cs))
    ecs_x = _dot(ecs_h, eexp) + _dot(ecs_m, eexp)
    wend_x = _dot((jnp.exp(tot - cs) * dt).astype(BF), eexp)

    lane = lax.broadcasted_iota(I32, (q, LANES), 1)
    for g in range(SSD_GROUPS):
        cbg = _dot_nt(cm_bf[:, g * SSD_STATE:(g + 1) * SSD_STATE], bm_bf[:, g * SSD_STATE:(g + 1) * SSD_STATE])
        for jp in range(HEADS_PER_GROUP // 2):
            j = g * (HEADS_PER_GROUP // 2) + jp
            ms = []
            for h in (2 * j, 2 * j + 1):
                col = jnp.broadcast_to(cs[:, h:h + 1], (q, q))
                row = jnp.broadcast_to(cs_t[h:h + 1, :], (q, q))
                dtr = jnp.broadcast_to(dt_t[h:h + 1, :], (q, q))
                ms.append((jnp.where(causal, jnp.exp(col - row), 0.0) * cbg * dtr).astype(BF))
            mp = jnp.concatenate(ms, axis=1)
            xp = xs_bf[:, j * LANES:(j + 1) * LANES]
            zero = jnp.zeros_like(xp)
            x2 = jnp.concatenate([jnp.where(lane < SSD_HEADDIM, xp, zero),
                                  jnp.where(lane >= SSD_HEADDIM, xp, zero)], axis=0)
            yacc[:, j * LANES:(j + 1) * LANES] = _dot(mp, x2)

    gw = HEADS_PER_GROUP * SSD_HEADDIM
    xw = xs * wend_x
    for g in range(SSD_GROUPS):
        xwt[g * gw:(g + 1) * gw, :] = xw[:, g * gw:(g + 1) * gw].T.astype(BF)

    if nseq == 1:
        for g in range(SSD_GROUPS):
            hin_g = hin_ref[0, g * HEADS_PER_GROUP:(g + 1) * HEADS_PER_GROUP].reshape(gw, SSD_STATE)
            yint[:, g * gw:(g + 1) * gw] = _dot_nt(cm_bf[:, g * SSD_STATE:(g + 1) * SSD_STATE], hin_g.astype(BF))
            s_new = _dot(xwt[g * gw:(g + 1) * gw, :], bm_bf[:, g * SSD_STATE:(g + 1) * SSD_STATE])
            for hl in range(HEADS_PER_GROUP):
                h = g * HEADS_PER_GROUP + hl
                dec = jnp.exp(jnp.broadcast_to(tot[0:1, h:h + 1], (SSD_HEADDIM, SSD_STATE)))
                ho_ref[0, h] = dec * hin_ref[0, h] + s_new[hl * SSD_HEADDIM:(hl + 1) * SSD_HEADDIM]
    else:
        yint[...] = jnp.zeros_like(yint)
        sh = int(math.log2(lq))
        rowseq = lax.shift_right_logical(lax.broadcasted_iota(I32, (q, SSD_STATE), 0), sh)
        colseq = lax.shift_right_logical(lax.broadcasted_iota(I32, (gw, q), 1), sh)

        def seq_body(i, carry):
            trow = tot_s[pl.ds(i * lq, 1), :]
            for g in range(SSD_GROUPS):
                hin_g = hin_ref[i, g * HEADS_PER_GROUP:(g + 1) * HEADS_PER_GROUP].reshape(gw, SSD_STATE)
                cmg = cm[:, g * SSD_STATE:(g + 1) * SSD_STATE]
                lhs = jnp.where(rowseq == i, cmg, 0.0).astype(BF)
                yint[:, g * gw:(g + 1) * gw] += _dot_nt(lhs, hin_g.astype(BF))
                xg = xwt[g * gw:(g + 1) * gw, :]
                xi = jnp.where(colseq == i, xg, jnp.zeros_like(xg))
                s_new = _dot(xi, bm_bf[:, g * SSD_STATE:(g + 1) * SSD_STATE])
                for hl in range(HEADS_PER_GROUP):
                    h = g * HEADS_PER_GROUP + hl
                    dec = jnp.exp(jnp.broadcast_to(trow[:, h:h + 1], (SSD_HEADDIM, SSD_STATE)))
                    ho_ref[i, h] = dec * hin_ref[i, h] + s_new[hl * SSD_HEADDIM:(hl + 1) * SSD_HEADDIM]
            return carry

        lax.fori_loop(0, nseq, seq_body, 0, unroll=2)

    y_ref[...] = (yacc[...] + yint[...] * ecs_x + dsk_ref[...] * xs).astype(y_ref.dtype)


def _ssd(xbc, dt, conv_w, conv_b, a_log, dskip, eexp, eye, conv0, h0, *, batch, seqlen):
    q = SSD_CHUNK
    t = xbc.shape[0]
    has_h0 = h0 is not None
    if has_h0:
        lq = seqlen
        nseq = q // lq
        nb = 1
        lead = t // q
        grid = (t // q,)
        rmap = lambda i: (i, 0, 0)
        smap3 = lambda i: (i, 0, 0)
        smap4 = lambda i: (i, 0, 0, 0)
    else:
        lq = q
        nseq = 1
        nb = SSD_SEQS_PER_STEP
        lead = batch
        nc = seqlen // q
        grid = (batch // nb, nc)
        rmap = lambda b, c: (b, c, 0)
        smap3 = lambda b, c: (b, 0, 0)
        smap4 = lambda b, c: (b, 0, 0, 0)
    xbc = xbc.reshape(lead, t // lead, SSD_CONV_DIM)
    dt = dt.reshape(lead, t // lead, LANES)
    shifts = np.zeros((SSD_CONV - 1, q, q), np.float32)
    for k in range(SSD_CONV - 1):
        for r in range(q):
            if r % lq + k - (SSD_CONV - 1) >= 0:
                shifts[k, r, r + k - (SSD_CONV - 1)] = 1.0
    consts = [conv_w, conv_b, a_log, dskip, eexp, eye, jnp.asarray(shifts).astype(BF)]
    in_specs = [pl.BlockSpec((nb, q, SSD_CONV_DIM), rmap), pl.BlockSpec((nb, q, LANES), rmap)]
    in_specs += [_const_spec(c.shape) for c in consts]
    args = [xbc, dt] + consts
    ns = nb * nseq
    if has_h0:
        in_specs += [pl.BlockSpec((ns, SSD_CONV - 1, SSD_CONV_DIM), smap3),
                     pl.BlockSpec((ns, SSD_HEADS, SSD_HEADDIM, SSD_STATE), smap4)]
        args += [conv0, h0]
    y, new_conv, new_state = pl.pallas_call(
        functools.partial(_ssd_kernel, lq=lq, has_h0=has_h0, nb=nb),
        grid=grid,
        in_specs=in_specs,
        out_specs=[pl.BlockSpec((nb, q, D_MODEL), rmap),
                   pl.BlockSpec((ns, SSD_CONV - 1, SSD_CONV_DIM), smap3),
                   pl.BlockSpec((ns, SSD_HEADS, SSD_HEADDIM, SSD_STATE), smap4)],
        out_shape=[jax.ShapeDtypeStruct((lead, t // lead, D_MODEL), BF),
                   jax.ShapeDtypeStruct((batch, SSD_CONV - 1, SSD_CONV_DIM), F32),
                   jax.ShapeDtypeStruct((batch, SSD_HEADS, SSD_HEADDIM, SSD_STATE), F32)],
        scratch_shapes=[pltpu.VMEM((ns, 16, SSD_CONV_DIM), F32),
                        pltpu.VMEM((nb, q, SSD_CONV_DIM), F32),
                        pltpu.VMEM((nb, q, D_MODEL), F32),
                        pltpu.VMEM((nb, q, D_MODEL), F32),
                        pltpu.VMEM((nb, D_MODEL, q), BF),
                        pltpu.VMEM((nb, q, LANES), F32)],
        compiler_params=_params(len(grid)),
        name="ssd_sample" if has_h0 else "ssd_prompt",
    )(*args)
    return y.reshape(t, D_MODEL), new_conv, new_state


S5_SEQ_TILE = 8


def _s5_kernel(*refs, tl, has_h0):
    bt = S5_SEQ_TILE
    m = bt * tl
    if has_h0:
        (u_ref, perm_ref, permt_ref, bblk_ref, cblk_ref, lre_ref, lim_ref, d_ref, h0re_ref, h0im_ref,
         y_ref, ore_ref, oim_ref, bre, bim, sre, sim, utm, ytm) = refs
    else:
        (u_ref, perm_ref, permt_ref, bblk_ref, cblk_ref, lre_ref, lim_ref, d_ref,
         y_ref, ore_ref, oim_ref, bre, bim, sre, sim, utm, ytm) = refs

    @pl.when(pl.program_id(1) == 0)
    def _():
        if has_h0:
            sre[...] = h0re_ref[...]
            sim[...] = h0im_ref[...]
        else:
            sre[...] = jnp.zeros_like(sre)
            sim[...] = jnp.zeros_like(sim)

    strided = len(u_ref.shape) == 3
    if strided:
        for b in range(bt):
            ub = u_ref[b].astype(F32)
            for s in range(S5_SLABS):
                utm[s, pl.ds(b, tl, stride=bt), :] = ub[:, s * LANES:(s + 1) * LANES]
        slab_in = lambda j: utm[j].astype(BF)
    else:
        u = u_ref[...]
        u_tm = _dot(perm_ref[...], u).astype(BF)
        slab_in = lambda j: u_tm[:, j * LANES:(j + 1) * LANES]
    half = S5_FLAT // S5_SLABS
    ys = []
    for j in range(S5_SLABS):
        sl = slice(j * half, (j + 1) * half)
        r = _dot(slab_in(j), bblk_ref[j])
        bre[:, sl] = r[:, :half]
        bim[:, sl] = r[:, half:]
        lr = lre_ref[:, sl]
        li = lim_ref[:, sl]
        sr = sre[:, sl]
        si = sim[:, sl]
        for l in range(tl):
            rows = slice(l * bt, (l + 1) * bt)
            sr, si = lr * sr - li * si + bre[rows, sl], lr * si + li * sr + bim[rows, sl]
            bre[rows, sl] = sr
            bim[rows, sl] = si
        sre[:, sl] = sr
        sim[:, sl] = si
        st = jnp.concatenate([bre[:, sl], bim[:, sl]], axis=1)
        ys.append(_dot(st.astype(BF), cblk_ref[j]))
    if strided:
        for j in range(S5_SLABS):
            ytm[j] = ys[j]
        for b in range(bt):
            yb = jnp.concatenate([ytm[s, pl.ds(b, tl, stride=bt), :] for s in range(S5_SLABS)], axis=1)
            y_ref[b] = (yb + d_ref[...] * u_ref[b].astype(F32)).astype(y_ref.dtype)
    else:
        y_tm = jnp.concatenate(ys, axis=1).astype(BF)
        y_ref[...] = (_dot(permt_ref[...], y_tm) + d_ref[...] * u.astype(F32)).astype(y_ref.dtype)
    ore_ref[...] = sre[...]
    oim_ref[...] = sim[...]


def _s5(u, perm, permt, bblk, cblk, lre, lim, dvec, h0re, h0im, *, batch, seqlen, tl):
    bt = S5_SEQ_TILE
    m = bt * tl
    has_h0 = h0re is not None
    nb = batch // bt
    nt = seqlen // tl
    if u.ndim == 3:
        u_spec = pl.BlockSpec((bt, tl, D_MODEL), lambda b, t: (b, t, 0))
    else:
        u_spec = pl.BlockSpec((m, D_MODEL), lambda b, t: (b * nt + t, 0))
    st_spec = pl.BlockSpec((bt, S5_FLAT), lambda b, t: (b, 0))
    consts = [perm, permt, bblk, cblk, lre, lim, dvec]
    in_specs = [u_spec] + [_const_spec(c.shape) for c in consts]
    args = [u] + consts
    if has_h0:
        in_specs += [st_spec, st_spec]
        args += [h0re, h0im]
    return pl.pallas_call(
        functools.partial(_s5_kernel, tl=tl, has_h0=has_h0),
        grid=(nb, nt),
        in_specs=in_specs,
        out_specs=[u_spec, st_spec, st_spec],
        out_shape=[jax.ShapeDtypeStruct(u.shape, BF),
                   jax.ShapeDtypeStruct((batch, S5_FLAT), F32),
                   jax.ShapeDtypeStruct((batch, S5_FLAT), F32)],
        scratch_shapes=[pltpu.VMEM((m, S5_FLAT), F32), pltpu.VMEM((m, S5_FLAT), F32),
                        pltpu.VMEM((bt, S5_FLAT), F32), pltpu.VMEM((bt, S5_FLAT), F32),
                        pltpu.VMEM((S5_SLABS, m, LANES), F32), pltpu.VMEM((S5_SLABS, m, LANES), F32)],
        compiler_params=_params(2),
        name="s5_sample" if has_h0 else "s5_prompt",
    )(*args)


def _mixout_kernel(y_ref, z_ref, y5_ref, ga_ref, gb_ref, x_ref, ng_ref, wa_ref, wglu_ref, bglu_ref, wmix_ref, o_ref):
    y = y_ref[...].astype(F32) * _silu(z_ref[...].astype(F32))
    yn = _rms(y, ng_ref[...]).astype(BF)
    branch_a = _dot(yn, wa_ref[...])
    glu = _dot(y5_ref[...], wglu_ref[...]) + bglu_ref[...]
    branch_b = glu[:, :D_MODEL] * _sigmoid(glu[:, D_MODEL:])
    merged = _sigmoid(ga_ref[...].astype(F32)) * branch_a + _sigmoid(gb_ref[...].astype(F32)) * branch_b
    o_ref[...] = x_ref[...] + _dot(merged.astype(BF), wmix_ref[...])


def _mixout(y, z, y5, ga, gb, x2d, ng, wa, wglu, bglu, wmix):
    t = x2d.shape[0]
    tm = MIX_TILE
    row = pl.BlockSpec((tm, D_MODEL), lambda i: (i, 0))
    consts = [ng, wa, wglu, bglu, wmix]
    return pl.pallas_call(
        _mixout_kernel,
        grid=(t // tm,),
        in_specs=[row] * 6 + [_resident_spec(c.shape) for c in consts],
        out_specs=row,
        out_shape=jax.ShapeDtypeStruct((t, D_MODEL), F32),
        compiler_params=_params(1),
        name="mixout",
    )(y, z, y5, ga, gb, x2d, *consts)


def _attn_kernel(h_ref, g_ref, wq_ref, wo_ref, k_ref, v_ref, o_ref, obuf, *, nkv, rows_per_seq):
    h = h_ref[...]
    r = h.shape[0]
    q32 = _dot(_rms(h, g_ref[...]).astype(BF), wq_ref[...])
    scale = MEM_HEAD_DIM ** -0.5
    if len(k_ref.shape) == 4:
        rq = rows_per_seq
        nrow = MEM_HEADS * rq
        row_head = lax.shift_right_logical(lax.broadcasted_iota(I32, (nrow, MEM_LEN * MEM_HEADS), 0), int(math.log2(rq)))
        col_head = lax.broadcasted_iota(I32, (nrow, MEM_LEN * MEM_HEADS), 1) & (MEM_HEADS - 1)
        visible = row_head == col_head
        for i in range(nkv):
            qs = jnp.concatenate([q32[i * rq:(i + 1) * rq, hd * MEM_HEAD_DIM:(hd + 1) * MEM_HEAD_DIM]
                                  for hd in range(MEM_HEADS)], axis=0).astype(BF)
            kf = k_ref[i].reshape(MEM_LEN * MEM_HEADS, MEM_HEAD_DIM).astype(BF)
            vf = v_ref[i].reshape(MEM_LEN * MEM_HEADS, MEM_HEAD_DIM).astype(BF)
            s = jnp.where(visible, _dot_nt(qs, kf) * scale, NEG_BIG)
            p = jnp.exp(s - jnp.max(s, axis=-1, keepdims=True))
            p = p / jnp.sum(p, axis=-1, keepdims=True)
            o = _dot(p.astype(BF), vf)
            for hd in range(MEM_HEADS):
                obuf[i * rq:(i + 1) * rq, hd * MEM_HEAD_DIM:(hd + 1) * MEM_HEAD_DIM] = o[hd * rq:(hd + 1) * rq]
    else:
        qv = q32.astype(BF)
        for hd in range(MEM_HEADS):
            sl = slice(hd * MEM_HEAD_DIM, (hd + 1) * MEM_HEAD_DIM)
            s = _dot_nt(qv[:, sl], k_ref[0, :, sl].astype(BF)) * scale
            p = jnp.exp(s - jnp.max(s, axis=-1, keepdims=True))
            p = p / jnp.sum(p, axis=-1, keepdims=True)
            obuf[:, sl] = _dot(p.astype(BF), v_ref[0, :, sl].astype(BF))
    o_ref[...] = h + _dot(obuf[...].astype(BF), wo_ref[...])


def _attn(h2d, g, wq, wo, k3, v3, *, rows, nkv, rows_per_seq):
    t = h2d.shape[0]
    nsteps = t // rows
    steps_per_kv = nsteps // (k3.shape[0] // nkv)
    row = pl.BlockSpec((rows, D_MODEL), lambda i: (i, 0))
    if k3.ndim == 4:
        kv = pl.BlockSpec((nkv, MEM_LEN, MEM_HEADS, MEM_HEAD_DIM), lambda i: (i // steps_per_kv, 0, 0, 0))
    else:
        kv = pl.BlockSpec((nkv, MEM_LEN, D_MODEL), lambda i: (i // steps_per_kv, 0, 0))
    consts = [g, wq, wo]
    return pl.pallas_call(
        functools.partial(_attn_kernel, nkv=nkv, rows_per_seq=rows_per_seq),
        grid=(nsteps,),
        in_specs=[row] + [_const_spec(c.shape) for c in consts] + [kv, kv],
        out_specs=row,
        out_shape=jax.ShapeDtypeStruct((t, D_MODEL), F32),
        scratch_shapes=[pltpu.VMEM((rows, D_MODEL), F32)],
        compiler_params=_params(1),
        name="attn",
    )(h2d, *consts, k3, v3)


ROUTER_LANE0 = MOE_GROUPS

U32 = jnp.uint32
HALF = D_MODEL // 2


def _pack_rows(x):
    xr = x.astype(BF).astype(F32)
    lo = lax.shift_right_logical(pltpu.bitcast(xr[:, :HALF], U32), jnp.uint32(16))
    hi = pltpu.bitcast(xr[:, HALF:], U32) & jnp.uint32(0xFFFF0000)
    return hi | lo


def _unpack_rows(w):
    lo = pltpu.bitcast(lax.shift_left(w, jnp.uint32(16)), F32).astype(BF)
    hi = pltpu.bitcast(w & jnp.uint32(0xFFFF0000), F32).astype(BF)
    return lo, hi


GRANULE = 8
LOCAL_USED = 2 * TOKEN_TILE + MOE_EXPERTS * (GRANULE - 1)
LOCAL_ROWS = 1280
SPARE_GRANULES = (LOCAL_ROWS - LOCAL_USED) // GRANULE
TILE_GRANULES = MOE_ROW_TILE // GRANULE


def _route_kernel(hp_ref, hs_ref, g_ref, wcat_ref, wh_ref, b_ref, tri_ref, eye_ref, upper_ref,
                  xl_ref, idx_ref, wt_ref, seg_ref, *, np_steps):
    i = pl.program_id(0)
    h = jnp.where(i < np_steps, hp_ref[...], hs_ref[...])
    xn = _rms(h, g_ref[...])
    tm = xn.shape[0]
    xh = xn.astype(BF)
    xm = (xn - xh.astype(F32)).astype(BF)
    both = _dot(xh, wcat_ref[...])
    logits = both[:, :LANES] + both[:, LANES:] + _dot(xm, wh_ref[...]) + b_ref[...]
    lane = lax.broadcasted_iota(I32, logits.shape, 1)
    lane_f = lane.astype(F32)
    no_lane = float(LANES)
    is_g = lane < MOE_GROUPS
    lg = jnp.where(is_g, logits, NEG_BIG)
    mg = jnp.max(lg, axis=-1, keepdims=True)
    gidx = jnp.min(jnp.where(lg == mg, lane_f, no_lane), axis=-1, keepdims=True).astype(I32)
    g_w = 1.0 / jnp.sum(jnp.where(is_g, jnp.exp(lg - mg), 0.0), axis=-1, keepdims=True)
    e_lane = lane - ROUTER_LANE0
    in_grp = ((e_lane >= 0) & (e_lane < MOE_EXPERTS)
              & (lax.shift_right_logical(jnp.maximum(e_lane, 0), 3) == gidx))
    le = jnp.where(in_grp, logits, NEG_BIG)
    m1 = jnp.max(le, axis=-1, keepdims=True)
    i1 = jnp.min(jnp.where(le == m1, lane_f, no_lane), axis=-1, keepdims=True)
    le2 = jnp.where(lane_f == i1, NEG_BIG, le)
    m2 = jnp.max(le2, axis=-1, keepdims=True)
    i2 = jnp.min(jnp.where(le2 == m2, lane_f, no_lane), axis=-1, keepdims=True)
    ratio = jnp.exp(m2 - m1)
    w1 = g_w / (1.0 + ratio)
    w2 = g_w * ratio / (1.0 + ratio)
    wt_ref[...] = jnp.where(lane == 0, w1, jnp.where(lane == 1, w2, 0.0))

    oh1 = jnp.where(lane_f == i1, 1.0, 0.0)
    oh2 = jnp.where(lane_f == i2, 1.0, 0.0)
    tot1 = jnp.sum(oh1, axis=0, keepdims=True)
    cnt = (tot1 + jnp.sum(oh2, axis=0, keepdims=True)).astype(I32)
    padded = lax.shift_left(lax.shift_right_logical(cnt + (GRANULE - 1), 3), 3)
    pad8 = jnp.broadcast_to(padded.astype(F32), (8, LANES)).astype(BF)
    lstart = _dot(pad8, upper_ref[...])[0:1, :]
    tri = tri_ref[...]
    l1 = jnp.sum(oh1 * (_dot(tri, oh1.astype(BF)) + lstart), axis=-1, keepdims=True)
    l2 = jnp.sum(oh2 * (_dot(tri, oh2.astype(BF)) + (lstart + tot1)), axis=-1, keepdims=True)
    idx_ref[...] = jnp.where(lane == 0, l1.astype(I32), jnp.where(lane == 1, l2.astype(I32), 0))
    sub = lax.broadcasted_iota(I32, (8, LANES), 0)
    seg_ref[0] = jnp.where(sub == 0, jnp.broadcast_to(cnt, (8, LANES)),
                           jnp.where(sub == 1, jnp.broadcast_to(lstart.astype(I32), (8, LANES)), 0))

    cols = jnp.where(lane == 0, l1, jnp.where(lane == 1, l2, 0.0))
    rows = _dot_exact(eye_ref[...], cols, nt=True)
    r_id = lax.broadcasted_iota(I32, (LOCAL_ROWS, tm), 0).astype(F32)
    perm = jnp.where(r_id == rows[0:1, :], 1.0, 0.0) + jnp.where(r_id == rows[1:2, :], 1.0, 0.0)
    xl_ref[...] = _pack_rows(_dot(perm.astype(BF), xh))


def _route(hp, hs, g, wcat, wh, b):
    tm = TOKEN_TILE
    np_steps = hp.shape[0] // tm
    ns_steps = hs.shape[0] // tm
    nt = np_steps + ns_steps
    tri = jnp.asarray(np.tril(np.ones((tm, tm), np.float32), -1)).astype(BF)
    eye8 = jnp.asarray(np.eye(8, LANES, dtype=np.float32)).astype(BF)
    upper = jnp.asarray(np.triu(np.ones((LANES, LANES), np.float32), 1)).astype(BF)
    consts = [g, wcat, wh, b, tri, eye8, upper]
    return pl.pallas_call(
        functools.partial(_route_kernel, np_steps=np_steps),
        grid=(nt,),
        in_specs=[pl.BlockSpec((tm, D_MODEL), lambda i: (jnp.minimum(i, np_steps - 1), 0)),
                  pl.BlockSpec((tm, D_MODEL), lambda i: (jnp.maximum(i - np_steps, 0), 0))]
                 + [_const_spec(c.shape) for c in consts],
        out_specs=[pl.BlockSpec((LOCAL_ROWS, HALF), lambda i: (i, 0)),
                   pl.BlockSpec((tm, LANES), lambda i: (i, 0)),
                   pl.BlockSpec((tm, LANES), lambda i: (i, 0)),
                   pl.BlockSpec((1, 8, LANES), lambda i: (i, 0, 0))],
        out_shape=[jax.ShapeDtypeStruct((nt * LOCAL_ROWS, HALF), U32),
                   jax.ShapeDtypeStruct((nt * tm, LANES), I32),
                   jax.ShapeDtypeStruct((nt * tm, LANES), F32),
                   jax.ShapeDtypeStruct((nt, 8, LANES), I32)],
        compiler_params=_params(1),
        name="route",
    )(hp, hs, *consts)


def _granule_plan(seg, n_tiles):
    cnt = seg[:, 0, ROUTER_LANE0:ROUTER_LANE0 + MOE_EXPERTS]
    lstart = seg[:, 1, ROUTER_LANE0:ROUTER_LANE0 + MOE_EXPERTS]
    ntt = cnt.shape[0]
    ng = (cnt + GRANULE - 1) // GRANULE
    cum = jnp.cumsum(ng, axis=0)
    first = (jnp.arange(ntt, dtype=I32)[:, None] * LOCAL_ROWS + lstart) // GRANULE - (cum - ng)
    total = cum[-1]
    tiles_e = (total + TILE_GRANULES - 1) // TILE_GRANULES
    tend = jnp.cumsum(tiles_e)
    tstart = tend - tiles_e
    tile = jnp.arange(n_tiles, dtype=I32)
    te = jnp.minimum(jnp.sum((tile[:, None] >= tend[None, :]).astype(I32), axis=1), MOE_EXPERTS - 1)
    n_used = tend[-1].astype(I32).reshape(1)
    sel = (te[:, None] == jnp.arange(MOE_EXPERTS, dtype=I32)[None, :]).astype(I32)
    pick = lambda table: jnp.sum(sel[:, None, :] * table[None, :, :], axis=2)
    cum_t, first_t = pick(cum), pick(first)
    off = (tile - jnp.sum(sel * tstart[None, :], axis=1))[:, None] * TILE_GRANULES + jnp.arange(TILE_GRANULES, dtype=I32)[None, :]
    valid = (off < jnp.sum(sel * total[None, :], axis=1)[:, None]) & (tile < n_used[0])[:, None]
    j = jnp.sum((cum_t[:, None, :] <= off[:, :, None]).astype(I32), axis=2)
    jsel = (j[:, :, None] == jnp.arange(ntt, dtype=I32)[None, None, :]).astype(I32)
    src = jnp.sum(jsel * first_t[:, None, :], axis=2) + off
    spare = lambda k: ((k // SPARE_GRANULES) * LOCAL_ROWS + LOCAL_USED) // GRANULE + k % SPARE_GRANULES
    assert (ntt - 1) * SPARE_GRANULES >= 2 * TILE_GRANULES
    zero_granule = spare((ntt - 1) * SPARE_GRANULES)
    trash = spare((tile % 2)[:, None] * TILE_GRANULES + jnp.arange(TILE_GRANULES, dtype=I32)[None, :])
    gsrc = jnp.where(valid, src, zero_granule).astype(I32).reshape(-1)
    gdst = jnp.where(valid, src, trash).astype(I32).reshape(-1)
    n_valid = jnp.sum(valid.astype(I32), axis=1).astype(I32)
    return te.astype(I32), n_used, gsrc, gdst, n_valid


def _grouped_kernel(te_ref, nu_ref, gsrc_ref, gdst_ref, nv_ref, xin_ref, wg_ref, wu_ref, wd_ref, xout_ref,
                    xbuf, ybuf, wg16, wu16, wd16, sem_in, sem_out):
    i = pl.program_id(0)
    nu = nu_ref[0]
    tm = MOE_ROW_TILE

    def hbm_rows(table_ref, tile, g):
        return pl.ds(pl.multiple_of(table_ref[tile * TILE_GRANULES + g] * GRANULE, GRANULE), GRANULE)

    def fetch_start(tile, slot):
        for g in range(TILE_GRANULES):
            pltpu.make_async_copy(xin_ref.at[hbm_rows(gsrc_ref, tile, g)],
                                  xbuf.at[slot, pl.ds(g * GRANULE, GRANULE)], sem_in.at[slot]).start()

    def store_start(tile, slot):
        for g in range(TILE_GRANULES):
            pltpu.make_async_copy(ybuf.at[slot, pl.ds(g * GRANULE, GRANULE)],
                                  xout_ref.at[hbm_rows(gdst_ref, tile, g)], sem_out.at[slot]).start()

    def fetch_wait(slot):
        pltpu.make_async_copy(xin_ref.at[pl.ds(0, tm)], xbuf.at[slot], sem_in.at[slot]).wait()

    def store_wait(slot):
        pltpu.make_async_copy(ybuf.at[slot], xout_ref.at[pl.ds(0, tm)], sem_out.at[slot]).wait()

    def step(slot):
        @pl.when(i == 0)
        def _():
            ybuf[...] = jnp.zeros_like(ybuf)
            fetch_start(0, 0)

        @pl.when(i + 1 < nu)
        def _():
            fetch_start(i + 1, 1 - slot)

        @pl.when(i >= 2)
        def _():
            store_wait(slot)

        @pl.when((i == 0) | (te_ref[i] != te_ref[jnp.maximum(i - 1, 0)]))
        def _():
            wg16[...] = wg_ref[0].astype(BF)
            wu16[...] = wu_ref[0].astype(BF)
            wd16[...] = wd_ref[0].astype(BF)

        fetch_wait(slot)
        n_valid = nv_ref[i]
        quarter = TILE_GRANULES // 4
        for part in range(1, 5):
            rows = part * quarter * GRANULE

            @pl.when((n_valid > (part - 1) * quarter) & (n_valid <= part * quarter))
            def _(rows=rows):
                x_lo, x_hi = _unpack_rows(xbuf[slot, 0:rows, :])
                gate = _dot(x_lo, wg16[0:HALF, :]) + _dot(x_hi, wg16[HALF:, :])
                up = _dot(x_lo, wu16[0:HALF, :]) + _dot(x_hi, wu16[HALF:, :])
                ybuf[slot, 0:rows, :] = _pack_rows(_dot((_silu(gate) * up).astype(BF), wd16[...]))

        store_start(i, slot)

        @pl.when(i == nu - 1)
        def _():
            store_wait(slot)

            @pl.when(i >= 1)
            def _():
                store_wait(1 - slot)

    for slot in (0, 1):
        @pl.when((i < nu) & (lax.rem(i, 2) == slot))
        def _(slot=slot):
            step(slot)


def _grouped(x_local, te, n_used, gsrc, gdst, n_valid, wg, wu, wd):
    tm = MOE_ROW_TILE
    n_tiles = te.shape[0]
    used = lambda i, nu: jnp.minimum(i, jnp.maximum(nu[0] - 1, 0))
    wspec = lambda shape: pl.BlockSpec((1,) + shape, lambda i, te, nu, *_: (te[used(i, nu)], 0, 0))
    grid_spec = pltpu.PrefetchScalarGridSpec(
        num_scalar_prefetch=5,
        grid=(n_tiles,),
        in_specs=[pl.BlockSpec(memory_space=pl.ANY), wspec((D_MODEL, MOE_D_FF)), wspec((D_MODEL, MOE_D_FF)),
                  wspec((MOE_D_FF, D_MODEL))],
        out_specs=pl.BlockSpec(memory_space=pl.ANY),
        scratch_shapes=[pltpu.VMEM((2, tm, HALF), U32), pltpu.VMEM((2, tm, HALF), U32),
                        pltpu.VMEM((D_MODEL, MOE_D_FF), BF), pltpu.VMEM((D_MODEL, MOE_D_FF), BF),
                        pltpu.VMEM((MOE_D_FF, D_MODEL), BF),
                        pltpu.SemaphoreType.DMA((2,)), pltpu.SemaphoreType.DMA((2,))],
    )
    return pl.pallas_call(
        _grouped_kernel,
        grid_spec=grid_spec,
        out_shape=jax.ShapeDtypeStruct(x_local.shape, x_local.dtype),
        input_output_aliases={5: 0},
        compiler_params=_params(1),
        name="grouped",
    )(te, n_used, gsrc, gdst, n_valid, x_local, wg, wu, wd)


def _merge_kernel(h_ref, idx_ref, wt_ref, g_ref, yl_ref, o_ref):
    tm = h_ref.shape[0]
    idx = idx_ref[...]
    wt = wt_ref[...]
    r_id = lax.broadcasted_iota(I32, (tm, LOCAL_ROWS), 1)
    sel = jnp.where(r_id == idx[:, 0:1], wt[:, 0:1], 0.0) + jnp.where(r_id == idx[:, 1:2], wt[:, 1:2], 0.0)
    y_lo, y_hi = _unpack_rows(yl_ref[...])
    sel16 = sel.astype(BF)
    moe = jnp.concatenate([_dot(sel16, y_lo), _dot(sel16, y_hi)], axis=1)
    o_ref[...] = _rms(h_ref[...] + moe, g_ref[...])


def _merge(h2d, idx, wt, g, y_local, *, tile_offset):
    t = h2d.shape[0]
    tm = TOKEN_TILE
    return pl.pallas_call(
        _merge_kernel,
        grid=(t // tm,),
        in_specs=[pl.BlockSpec((tm, D_MODEL), lambda i: (i, 0)),
                  pl.BlockSpec((tm, LANES), lambda i: (i + tile_offset, 0)),
                  pl.BlockSpec((tm, LANES), lambda i: (i + tile_offset, 0)),
                  _const_spec(g.shape),
                  pl.BlockSpec((LOCAL_ROWS, HALF), lambda i: (i + tile_offset, 0))],
        out_specs=pl.BlockSpec((tm, D_MODEL), lambda i: (i, 0)),
        out_shape=jax.ShapeDtypeStruct((t, D_MODEL), F32),
        compiler_params=_params(1),
        name="merge",
    )(h2d, idx, wt, g, y_local)


def _perm_matrix(bt, tl):
    m = bt * tl
    p = np.zeros((m, m), np.float32)
    for l in range(tl):
        for b in range(bt):
            p[l * bt + b, b * tl + l] = 1.0
    return p


def _s5_operators(lam_re, lam_im, log_step, b_re, b_im, c_re, c_im):
    lam = lax.complex(lam_re, lam_im)
    step = jnp.exp(log_step)[:, None]
    lam_bar = jnp.exp(lam * step)
    b_bar = ((lam_bar - 1.0) / lam)[..., None] * lax.complex(b_re, b_im)
    gl = LANES // S5_GROUP
    eye = jnp.eye(gl, dtype=F32)

    def in_block(bpart):
        bp = bpart.reshape(S5_SLABS, gl, S5_STATE, S5_GROUP)
        blk = jnp.einsum("sgnk,gh->sgkhn", bp, eye)
        return blk.reshape(S5_SLABS, LANES, gl * S5_STATE)

    def out_block(cpart):
        cp = cpart.reshape(S5_SLABS, gl, S5_GROUP, S5_STATE)
        blk = jnp.einsum("sgkn,gh->sgnhk", cp, eye)
        return blk.reshape(S5_SLABS, gl * S5_STATE, LANES)

    bblk = jnp.concatenate([in_block(jnp.real(b_bar)), in_block(jnp.imag(b_bar))], axis=2).astype(BF)
    cblk = jnp.concatenate([out_block(c_re), out_block(-c_im)], axis=1).astype(BF)
    lre = jnp.broadcast_to(jnp.real(lam_bar).reshape(1, S5_FLAT), (S5_SEQ_TILE, S5_FLAT))
    lim = jnp.broadcast_to(jnp.imag(lam_bar).reshape(1, S5_FLAT), (S5_SEQ_TILE, S5_FLAT))
    return bblk, cblk, lre, lim


def _pad_lanes(v, width=LANES):
    return jnp.pad(v, [(0, 0)] * (v.ndim - 1) + [(0, width - v.shape[-1])])


def kernel(x_prompt, x_sample, state_conv, state_ssd, state_s5_re, state_s5_im, cache_mem_k, cache_mem_v, mem_prompt, norm_mix_g, w_in, conv_w, conv_b, ssd_dt_bias, ssd_a_log, ssd_d, ssd_norm_g, w_ssd_branch, s5_lambda_re, s5_lambda_im, s5_log_step, s5_b_re, s5_b_im, s5_c_re, s5_c_im, s5_d, w_glu, b_glu, w_mix_out, norm_mem_q_g, norm_mem_kv_g, w_mem_q, w_mem_k, w_mem_v, w_mem_o, norm_ffn_g, w_router_group, b_router_group, w_router_expert, b_router_expert, w_exp_gate, w_exp_up, w_exp_down, norm_final_g):
    bp, lp, _ = x_prompt.shape
    bs, ls, _ = x_sample.shape
    tp, ts = bp * lp, bs * ls
    row1 = lambda v: v.reshape(1, -1).astype(F32)

    w_in0 = w_in[0]
    o_xbc = D_MODEL
    o_dt = o_xbc + SSD_CONV_DIM
    o_u5 = o_dt + SSD_HEADS
    w_main = jnp.concatenate([w_in0[:, :o_dt], w_in0[:, o_u5:]], axis=1).astype(BF)
    w_dt = _pad_lanes(w_in0[:, o_dt:o_u5]).astype(BF)
    dt_bias = _pad_lanes(row1(ssd_dt_bias[0]))
    a_log = _pad_lanes(row1(ssd_a_log[0]))
    dskip = jnp.repeat(ssd_d[0].astype(F32), SSD_HEADDIM).reshape(1, D_MODEL)
    eexp_np = np.zeros((LANES, D_MODEL), np.float32)
    for h in range(SSD_HEADS):
        eexp_np[h, h * SSD_HEADDIM:(h + 1) * SSD_HEADDIM] = 1.0
    eexp = jnp.asarray(eexp_np).astype(BF)
    eye16 = jnp.asarray(np.eye(SSD_HEADS, LANES, dtype=np.float32)).astype(BF)
    bblk, cblk, lre, lim = _s5_operators(s5_lambda_re[0], s5_lambda_im[0], s5_log_step[0], s5_b_re[0],
                                         s5_b_im[0], s5_c_re[0], s5_c_im[0])
    s5d = row1(s5_d[0])
    w_kv = jnp.concatenate([w_mem_k[0], w_mem_v[0]], axis=1).astype(BF)
    w_r = _pad_lanes(jnp.concatenate([w_router_group[0], w_router_expert[0]], axis=1).astype(F32))
    wr_h, wr_m, _ = _split3(w_r)
    b_r = _pad_lanes(row1(jnp.concatenate([b_router_group[0], b_router_expert[0]])))
    wg = w_exp_gate[0].reshape(MOE_EXPERTS, D_MODEL, MOE_D_FF)
    wu = w_exp_up[0].reshape(MOE_EXPERTS, D_MODEL, MOE_D_FF)
    wd = w_exp_down[0].reshape(MOE_EXPERTS, MOE_D_FF, D_MODEL)

    def mixer(x2d, batch, seqlen, conv0, h0, s5re0, s5im0):
        z, xbc, u5, ga, gb, dt = _inproj(x2d, row1(norm_mix_g[0]), w_main, w_dt, dt_bias)
        y_ssd, new_conv, new_ssd = _ssd(xbc, dt, conv_w[0].astype(F32), row1(conv_b[0]), a_log, dskip, eexp, eye16,
                                        conv0, h0, batch=batch, seqlen=seqlen)
        if h0 is None:
            tl = S5_TIME_TILE
            u_in = u5.reshape(batch, seqlen, D_MODEL)
        else:
            tl = seqlen
            u_in = u5
        perm = _perm_matrix(S5_SEQ_TILE, tl)
        y5, new_re, new_im = _s5(u_in, jnp.asarray(perm).astype(BF), jnp.asarray(perm.T).astype(BF), bblk, cblk,
                                 lre, lim, s5d, s5re0, s5im0, batch=batch, seqlen=seqlen, tl=tl)
        h1 = _mixout(y_ssd, z, y5.reshape(-1, D_MODEL), ga, gb, x2d, row1(ssd_norm_g[0]),
                     w_ssd_branch[0].astype(BF), w_glu[0].astype(BF), row1(b_glu[0]), w_mix_out[0].astype(BF))
        return h1, new_conv, new_ssd, new_re, new_im

    xp2 = x_prompt.reshape(tp, D_MODEL)
    xs2 = x_sample.reshape(ts, D_MODEL)
    h1p, conv_p, ssd_p, re_p, im_p = mixer(xp2, bp, lp, None, None, None, None)
    h1s, conv_s, ssd_s, re_s, im_s = mixer(xs2, bs, ls, state_conv[0], state_ssd[0],
                                           state_s5_re[0].reshape(bs, S5_FLAT), state_s5_im[0].reshape(bs, S5_FLAT))

    mk_p, mv_p = _memkv(mem_prompt.reshape(bp * MEM_LEN, D_MODEL), row1(norm_mem_kv_g[0]), w_kv)
    gq = row1(norm_mem_q_g[0])
    wq = w_mem_q[0].astype(BF)
    wo = w_mem_o[0].astype(BF)
    h2p = _attn(h1p, gq, wq, wo, mk_p.reshape(bp, MEM_LEN, D_MODEL), mv_p.reshape(bp, MEM_LEN, D_MODEL),
                rows=MIX_TILE, nkv=1, rows_per_seq=lp)
    sample_seqs = 8
    h2s = _attn(h1s, gq, wq, wo, cache_mem_k[0], cache_mem_v[0], rows=sample_seqs * ls, nkv=sample_seqs,
                rows_per_seq=ls)

    g_ffn = row1(norm_ffn_g[0])
    x_local, r_idx, r_wt, seg = _route(h2p, h2s, g_ffn, jnp.concatenate([wr_h, wr_m], axis=1), wr_h, b_r)
    token_tiles = (tp + ts) // TOKEN_TILE
    max_granules = (2 * (tp + ts) + token_tiles * MOE_EXPERTS * (GRANULE - 1)) // GRANULE
    n_tiles = max_granules // TILE_GRANULES + MOE_EXPERTS
    te, n_used, gsrc, gdst, n_valid = _granule_plan(seg, n_tiles)
    y_local = _grouped(x_local, te, n_used, gsrc, gdst, n_valid, wg, wu, wd)
    gf = row1(norm_final_g)
    y_prompt = _merge(h2p, r_idx, r_wt, gf, y_local, tile_offset=0)
    y_sample = _merge(h2s, r_idx, r_wt, gf, y_local, tile_offset=tp // TOKEN_TILE)

    return (y_prompt.reshape(bp, lp, D_MODEL), y_sample.reshape(bs, ls, D_MODEL),
            conv_p[None], ssd_p[None],
            re_p.reshape(1, bp, S5_GROUPS, S5_STATE), im_p.reshape(1, bp, S5_GROUPS, S5_STATE),
            mk_p.reshape(1, bp, MEM_LEN, MEM_HEADS, MEM_HEAD_DIM), mv_p.reshape(1, bp, MEM_LEN, MEM_HEADS, MEM_HEAD_DIM),
            conv_s[None], ssd_s[None],
            re_s.reshape(1, bs, S5_GROUPS, S5_STATE), im_s.reshape(1, bs, S5_GROUPS, S5_STATE))
```

```python
import functools
import math

import numpy as np
import jax
import jax.numpy as jnp
from jax import lax
from jax.experimental import pallas as pl
from jax.experimental.pallas import tpu as pltpu

F32 = jnp.float32
BF = jnp.bfloat16
I32 = jnp.int32

D_MODEL = 1024
SSD_HEADS = 16
SSD_HEADDIM = 64
SSD_GROUPS = 2
SSD_STATE = 128
SSD_CONV = 4
SSD_CONV_DIM = 1536
HEADS_PER_GROUP = SSD_HEADS // SSD_GROUPS
S5_GROUPS = 64
S5_GROUP = 16
S5_STATE = 64
S5_FLAT = S5_GROUPS * S5_STATE
S5_SLABS = D_MODEL // 128
MEM_LEN = 256
MEM_HEADS = 4
MEM_HEAD_DIM = 256
MOE_GROUPS = 4
MOE_EXPERTS_PER_GROUP = 8
MOE_EXPERTS = MOE_GROUPS * MOE_EXPERTS_PER_GROUP
MOE_D_FF = 512
NORM_EPS = 1e-6

LANES = 128
VMEM_LIMIT_BYTES = 56 * 1024 * 1024
TOKEN_TILE = 512
MIX_TILE = 1024
SSD_CHUNK = 128
SSD_SEQS_PER_STEP = 4
S5_TIME_TILE = 64
MOE_ROW_TILE = 512
NEG_BIG = -1e30


def _params(n_axes):
    return pltpu.CompilerParams(dimension_semantics=("arbitrary",) * n_axes,
                                vmem_limit_bytes=VMEM_LIMIT_BYTES)


def _const_spec(shape):
    nd = len(shape)
    return pl.BlockSpec(shape, lambda *_: (0,) * nd)


def _resident_spec(shape):
    nd = len(shape)
    return pl.BlockSpec(shape, lambda *_: (0,) * nd, pipeline_mode=pl.Buffered(1))


def _dot(a, b):
    return jnp.dot(a, b, preferred_element_type=F32)


def _dot_nt(a, b):
    return lax.dot_general(a, b, (((1,), (1,)), ((), ())), preferred_element_type=F32)


def _split3(v):
    h = v.astype(BF)
    r = v - h.astype(F32)
    m = r.astype(BF)
    l = (r - m.astype(F32)).astype(BF)
    return h, m, l


def _dot_exact(a_bf, v, nt=False):
    f = _dot_nt if nt else _dot
    h, m, l = _split3(v)
    return f(a_bf, h) + f(a_bf, m) + f(a_bf, l)


def _rms(x, g):
    return x * lax.rsqrt(jnp.mean(x * x, axis=-1, keepdims=True) + NORM_EPS) * g


def _sigmoid(x):
    return 0.5 * jnp.tanh(0.5 * x) + 0.5


def _silu(x):
    return x * _sigmoid(x)


def _proj_cols(xb, w_ref, o_ref, c0, width):
    for c in range(0, width, 512):
        ce = min(c + 512, width)
        o_ref[:, c:ce] = _dot(xb, w_ref[:, c0 + c:c0 + ce]).astype(o_ref.dtype)


def _inproj_kernel(x_ref, g_ref, w_ref, wdt_ref, dtb_ref, z_ref, xbc_ref, u5_ref, ga_ref, gb_ref, dt_ref):
    xb = _rms(x_ref[...], g_ref[...]).astype(BF)
    c0 = 0
    for o_ref in (z_ref, xbc_ref, u5_ref, ga_ref, gb_ref):
        width = o_ref.shape[1]
        _proj_cols(xb, w_ref, o_ref, c0, width)
        c0 += width
    raw = _dot(xb, wdt_ref[...]) + dtb_ref[...]
    dt_ref[...] = jnp.maximum(raw, 0.0) + jnp.log1p(jnp.exp(-jnp.abs(raw)))


def _inproj(x2d, g, w_main, w_dt, dt_bias):
    t = x2d.shape[0]
    tm = MIX_TILE
    widths = (D_MODEL, SSD_CONV_DIM, D_MODEL, D_MODEL, D_MODEL)
    row = lambda w: pl.BlockSpec((tm, w), lambda i: (i, 0))
    return pl.pallas_call(
        _inproj_kernel,
        grid=(t // tm,),
        in_specs=[row(D_MODEL), _const_spec(g.shape), _resident_spec(w_main.shape), _resident_spec(w_dt.shape),
                  _const_spec(dt_bias.shape)],
        out_specs=[row(w) for w in widths] + [row(LANES)],
        out_shape=[jax.ShapeDtypeStruct((t, w), BF) for w in widths] + [jax.ShapeDtypeStruct((t, LANES), F32)],
        compiler_params=_params(1),
        name="inproj",
    )(x2d, g, w_main, w_dt, dt_bias)


def _memkv_kernel(x_ref, g_ref, w_ref, k_ref, v_ref):
    xb = _rms(x_ref[...], g_ref[...]).astype(BF)
    _proj_cols(xb, w_ref, k_ref, 0, D_MODEL)
    _proj_cols(xb, w_ref, v_ref, D_MODEL, D_MODEL)


def _memkv(mem2d, g, w_kv):
    t = mem2d.shape[0]
    tm = TOKEN_TILE
    row = pl.BlockSpec((tm, D_MODEL), lambda i: (i, 0))
    return pl.pallas_call(
        _memkv_kernel,
        grid=(t // tm,),
        in_specs=[row, _const_spec(g.shape), _const_spec(w_kv.shape)],
        out_specs=[row, row],
        out_shape=[jax.ShapeDtypeStruct((t, D_MODEL), F32)] * 2,
        compiler_params=_params(1),
        name="memkv",
    )(mem2d, g, w_kv)


def _ssd_kernel(*refs, lq, has_h0, nb):
    nseq = SSD_CHUNK // lq
    n_in = 11 if has_h0 else 9
    xbc_ref, dt_ref = refs[0], refs[1]
    consts = refs[2:9]
    state_in = refs[9:n_in]
    y_ref, convo_ref, ho_ref = refs[n_in:n_in + 3]
    hbuf, cacc, yacc, yint, xwt, tot_s = refs[n_in + 3:]
    for s in range(nb):
        seqs = pl.ds(s * nseq, nseq)
        _ssd_chunk(xbc_ref.at[s], dt_ref.at[s], *consts, *[r.at[seqs] for r in state_in],
                   y_ref.at[s], convo_ref.at[seqs], ho_ref.at[seqs], hbuf.at[seqs],
                   cacc.at[s], yacc.at[s], yint.at[s], xwt.at[s], tot_s.at[s], lq=lq, has_h0=has_h0)


def _ssd_chunk(*refs, lq, has_h0):
    q = SSD_CHUNK
    nseq = q // lq
    if has_h0:
        (xbc_ref, dt_ref, cw_ref, cb_ref, alog_ref, dsk_ref, eexp_ref, eye_ref, shift_ref, conv0_ref, h0_ref,
         y_ref, convo_ref, ho_ref, hbuf, cacc, yacc, yint, xwt, tot_s) = refs
        hin_ref = h0_ref
        hbuf[...] = jnp.zeros_like(hbuf)
        hbuf[:, 0:SSD_CONV - 1, :] = conv0_ref[...]
    else:
        (xbc_ref, dt_ref, cw_ref, cb_ref, alog_ref, dsk_ref, eexp_ref, eye_ref, shift_ref,
         y_ref, convo_ref, ho_ref, hbuf, cacc, yacc, yint, xwt, tot_s) = refs
        hin_ref = ho_ref

        @pl.when(pl.program_id(1) == 0)
        def _():
            ho_ref[...] = jnp.zeros_like(ho_ref)
            hbuf[...] = jnp.zeros_like(hbuf)

    x16 = xbc_ref[...]
    xraw = x16.astype(F32)
    acc = cb_ref[...] + xraw * cw_ref[SSD_CONV - 1:SSD_CONV, :]
    for k in range(SSD_CONV - 1):
        acc = acc + _dot(shift_ref[k], x16) * cw_ref[k:k + 1, :]
    cacc[...] = acc
    for i in range(nseq):
        tail = xraw[(i + 1) * lq - (SSD_CONV - 1):(i + 1) * lq]
        convo_ref[i] = tail
        corr = hbuf[i, 0:8, :] * cw_ref[0:1, :]
        for k in range(1, SSD_CONV - 1):
            corr = corr + hbuf[i, k:k + 8, :] * cw_ref[k:k + 1, :]
        cacc[i * lq:i * lq + 8, :] += corr
        if not has_h0:
            hbuf[i, 0:SSD_CONV - 1, :] = tail
    xc = _silu(cacc[...])
    xs = xc[:, :D_MODEL]
    xs_bf = xs.astype(BF)
    bm_bf = xc[:, D_MODEL:D_MODEL + SSD_GROUPS * SSD_STATE].astype(BF)
    cm = xc[:, D_MODEL + SSD_GROUPS * SSD_STATE:]
    cm_bf = cm.astype(BF)

    dt = dt_ref[...]
    da = dt * (-jnp.exp(alog_ref[...]))
    ri = lax.broadcasted_iota(I32, (q, q), 0)
    ci = lax.broadcasted_iota(I32, (q, q), 1)
    if nseq == 1:
        causal = ci <= ri
    else:
        sh = int(math.log2(lq))
        same = lax.shift_right_logical(ri, sh) == lax.shift_right_logical(ci, sh)
        causal = same & (ci <= ri)
    lmat = jnp.where(causal, 1.0, 0.0).astype(BF)
    cs = _dot_exact(lmat, da)
    if nseq == 1:
        tot = jnp.broadcast_to(cs[q - 1:q, :], (q, LANES))
    else:
        tot = _dot_exact(jnp.where(same, 1.0, 0.0).astype(BF), da)
    tot_s[...] = tot
    eye = eye_ref[...]
    cs_t = _dot_exact(eye, cs, nt=True)
    dt_t = _dot_exact(eye, dt, nt=True)
    eexp = eexp_ref[...]
    ecs_h, ecs_m, _ = _split3(jnp.exp(cs))
    ecs_x = _dot(ecs_h, eexp) + _dot(ecs_m, eexp)
    wend_x = _dot((jnp.exp(tot - cs) * dt).astype(BF), eexp)

    lane = lax.broadcasted_iota(I32, (q, LANES), 1)
    for g in range(SSD_GROUPS):
        cbg = _dot_nt(cm_bf[:, g * SSD_STATE:(g + 1) * SSD_STATE], bm_bf[:, g * SSD_STATE:(g + 1) * SSD_STATE])
        for jp in range(HEADS_PER_GROUP // 2):
            j = g * (HEADS_PER_GROUP // 2) + jp
            ms = []
            for h in (2 * j, 2 * j + 1):
                col = jnp.broadcast_to(cs[:, h:h + 1], (q, q))
                row = jnp.broadcast_to(cs_t[h:h + 1, :], (q, q))
                dtr = jnp.broadcast_to(dt_t[h:h + 1, :], (q, q))
                ms.append((jnp.where(causal, jnp.exp(col - row), 0.0) * cbg * dtr).astype(BF))
            mp = jnp.concatenate(ms, axis=1)
            xp = xs_bf[:, j * LANES:(j + 1) * LANES]
            zero = jnp.zeros_like(xp)
            x2 = jnp.concatenate([jnp.where(lane < SSD_HEADDIM, xp, zero),
                                  jnp.where(lane >= SSD_HEADDIM, xp, zero)], axis=0)
            yacc[:, j * LANES:(j + 1) * LANES] = _dot(mp, x2)

    gw = HEADS_PER_GROUP * SSD_HEADDIM
    xw = xs * wend_x
    for g in range(SSD_GROUPS):
        xwt[g * gw:(g + 1) * gw, :] = xw[:, g * gw:(g + 1) * gw].T.astype(BF)

    if nseq == 1:
        for g in range(SSD_GROUPS):
            hin_g = hin_ref[0, g * HEADS_PER_GROUP:(g + 1) * HEADS_PER_GROUP].reshape(gw, SSD_STATE)
            yint[:, g * gw:(g + 1) * gw] = _dot_nt(cm_bf[:, g * SSD_STATE:(g + 1) * SSD_STATE], hin_g.astype(BF))
            s_new = _dot(xwt[g * gw:(g + 1) * gw, :], bm_bf[:, g * SSD_STATE:(g + 1) * SSD_STATE])
            for hl in range(HEADS_PER_GROUP):
                h = g * HEADS_PER_GROUP + hl
                dec = jnp.exp(jnp.broadcast_to(tot[0:1, h:h + 1], (SSD_HEADDIM, SSD_STATE)))
                ho_ref[0, h] = dec * hin_ref[0, h] + s_new[hl * SSD_HEADDIM:(hl + 1) * SSD_HEADDIM]
    else:
        yint[...] = jnp.zeros_like(yint)
        sh = int(math.log2(lq))
        rowseq = lax.shift_right_logical(lax.broadcasted_iota(I32, (q, SSD_STATE), 0), sh)
        colseq = lax.shift_right_logical(lax.broadcasted_iota(I32, (gw, q), 1), sh)

        def seq_body(i, carry):
            trow = tot_s[pl.ds(i * lq, 1), :]
            for g in range(SSD_GROUPS):
                hin_g = hin_ref[i, g * HEADS_PER_GROUP:(g + 1) * HEADS_PER_GROUP].reshape(gw, SSD_STATE)
                cmg = cm[:, g * SSD_STATE:(g + 1) * SSD_STATE]
                lhs = jnp.where(rowseq == i, cmg, 0.0).astype(BF)
                yint[:, g * gw:(g + 1) * gw] += _dot_nt(lhs, hin_g.astype(BF))
                xg = xwt[g * gw:(g + 1) * gw, :]
                xi = jnp.where(colseq == i, xg, jnp.zeros_like(xg))
                s_new = _dot(xi, bm_bf[:, g * SSD_STATE:(g + 1) * SSD_STATE])
                for hl in range(HEADS_PER_GROUP):
                    h = g * HEADS_PER_GROUP + hl
                    dec = jnp.exp(jnp.broadcast_to(trow[:, h:h + 1], (SSD_HEADDIM, SSD_STATE)))
                    ho_ref[i, h] = dec * hin_ref[i, h] + s_new[hl * SSD_HEADDIM:(hl + 1) * SSD_HEADDIM]
            return carry

        lax.fori_loop(0, nseq, seq_body, 0, unroll=2)

    y_ref[...] = (yacc[...] + yint[...] * ecs_x + dsk_ref[...] * xs).astype(y_ref.dtype)


def _ssd(xbc, dt, conv_w, conv_b, a_log, dskip, eexp, eye, conv0, h0, *, batch, seqlen):
    q = SSD_CHUNK
    t = xbc.shape[0]
    has_h0 = h0 is not None
    if has_h0:
        lq = seqlen
        nseq = q // lq
        nb = 1
        lead = t // q
        grid = (t // q,)
        rmap = lambda i: (i, 0, 0)
        smap3 = lambda i: (i, 0, 0)
        smap4 = lambda i: (i, 0, 0, 0)
    else:
        lq = q
        nseq = 1
        nb = SSD_SEQS_PER_STEP
        lead = batch
        nc = seqlen // q
        grid = (batch // nb, nc)
        rmap = lambda b, c: (b, c, 0)
        smap3 = lambda b, c: (b, 0, 0)
        smap4 = lambda b, c: (b, 0, 0, 0)
    xbc = xbc.reshape(lead, t // lead, SSD_CONV_DIM)
    dt = dt.reshape(lead, t // lead, LANES)
    shifts = np.zeros((SSD_CONV - 1, q, q), np.float32)
    for k in range(SSD_CONV - 1):
        for r in range(q):
            if r % lq + k - (SSD_CONV - 1) >= 0:
                shifts[k, r, r + k - (SSD_CONV - 1)] = 1.0
    consts = [conv_w, conv_b, a_log, dskip, eexp, eye, jnp.asarray(shifts).astype(BF)]
    in_specs = [pl.BlockSpec((nb, q, SSD_CONV_DIM), rmap), pl.BlockSpec((nb, q, LANES), rmap)]
    in_specs += [_const_spec(c.shape) for c in consts]
    args = [xbc, dt] + consts
    ns = nb * nseq
    if has_h0:
        in_specs += [pl.BlockSpec((ns, SSD_CONV - 1, SSD_CONV_DIM), smap3),
                     pl.BlockSpec((ns, SSD_HEADS, SSD_HEADDIM, SSD_STATE), smap4)]
        args += [conv0, h0]
    y, new_conv, new_state = pl.pallas_call(
        functools.partial(_ssd_kernel, lq=lq, has_h0=has_h0, nb=nb),
        grid=grid,
        in_specs=in_specs,
        out_specs=[pl.BlockSpec((nb, q, D_MODEL), rmap),
                   pl.BlockSpec((ns, SSD_CONV - 1, SSD_CONV_DIM), smap3),
                   pl.BlockSpec((ns, SSD_HEADS, SSD_HEADDIM, SSD_STATE), smap4)],
        out_shape=[jax.ShapeDtypeStruct((lead, t // lead, D_MODEL), BF),
                   jax.ShapeDtypeStruct((batch, SSD_CONV - 1, SSD_CONV_DIM), F32),
                   jax.ShapeDtypeStruct((batch, SSD_HEADS, SSD_HEADDIM, SSD_STATE), F32)],
        scratch_shapes=[pltpu.VMEM((ns, 16, SSD_CONV_DIM), F32),
                        pltpu.VMEM((nb, q, SSD_CONV_DIM), F32),
                        pltpu.VMEM((nb, q, D_MODEL), F32),
                        pltpu.VMEM((nb, q, D_MODEL), F32),
                        pltpu.VMEM((nb, D_MODEL, q), BF),
                        pltpu.VMEM((nb, q, LANES), F32)],
        compiler_params=_params(len(grid)),
        name="ssd_sample" if has_h0 else "ssd_prompt",
    )(*args)
    return y.reshape(t, D_MODEL), new_conv, new_state


S5_PROMPT_SEQS = 8
S5_SAMPLE_SEQS = 32


def _s5_kernel(*refs, tl, has_h0, bt):
    m = bt * tl
    if has_h0:
        (u_ref, perm_ref, permt_ref, bblk_ref, cblk_ref, lre_ref, lim_ref, d_ref, h0re_ref, h0im_ref,
         y_ref, ore_ref, oim_ref, bre, bim, sre, sim, utm, ytm) = refs
    else:
        (u_ref, perm_ref, permt_ref, bblk_ref, cblk_ref, lre_ref, lim_ref, d_ref,
         y_ref, ore_ref, oim_ref, bre, bim, sre, sim, utm, ytm) = refs

    @pl.when(pl.program_id(1) == 0)
    def _():
        if has_h0:
            sre[...] = h0re_ref[...]
            sim[...] = h0im_ref[...]
        else:
            sre[...] = jnp.zeros_like(sre)
            sim[...] = jnp.zeros_like(sim)

    strided = len(u_ref.shape) == 3
    if strided:
        for b in range(bt):
            ub = u_ref[b].astype(F32)
            for s in range(S5_SLABS):
                utm[s, pl.ds(b, tl, stride=bt), :] = ub[:, s * LANES:(s + 1) * LANES]
        slab_in = lambda j: utm[j].astype(BF)
    else:
        u = u_ref[...]
        u_tm = _dot(perm_ref[...], u).astype(BF)
        slab_in = lambda j: u_tm[:, j * LANES:(j + 1) * LANES]
    half = S5_FLAT // S5_SLABS
    ys = []
    for j in range(S5_SLABS):
        sl = slice(j * half, (j + 1) * half)
        r = _dot(slab_in(j), bblk_ref[j])
        bre[:, sl] = r[:, :half]
        bim[:, sl] = r[:, half:]
        lr = lre_ref[:, sl]
        li = lim_ref[:, sl]
        sr = sre[:, sl]
        si = sim[:, sl]
        for l in range(tl):
            rows = slice(l * bt, (l + 1) * bt)
            sr, si = lr * sr - li * si + bre[rows, sl], lr * si + li * sr + bim[rows, sl]
            bre[rows, sl] = sr
            bim[rows, sl] = si
        sre[:, sl] = sr
        sim[:, sl] = si
        st = jnp.concatenate([bre[:, sl], bim[:, sl]], axis=1)
        ys.append(_dot(st.astype(BF), cblk_ref[j]))
    if strided:
        for j in range(S5_SLABS):
            ytm[j] = ys[j]
        for b in range(bt):
            yb = jnp.concatenate([ytm[s, pl.ds(b, tl, stride=bt), :] for s in range(S5_SLABS)], axis=1)
            y_ref[b] = (yb + d_ref[...] * u_ref[b].astype(F32)).astype(y_ref.dtype)
    else:
        y_tm = jnp.concatenate(ys, axis=1).astype(BF)
        y_ref[...] = (_dot(permt_ref[...], y_tm) + d_ref[...] * u.astype(F32)).astype(y_ref.dtype)
    ore_ref[...] = sre[...]
    oim_ref[...] = sim[...]


def _s5(u, perm, permt, bblk, cblk, lre, lim, dvec, h0re, h0im, *, batch, seqlen, tl, bt):
    m = bt * tl
    has_h0 = h0re is not None
    nb = batch // bt
    nt = seqlen // tl
    if u.ndim == 3:
        u_spec = pl.BlockSpec((bt, tl, D_MODEL), lambda b, t: (b, t, 0))
    else:
        u_spec = pl.BlockSpec((m, D_MODEL), lambda b, t: (b * nt + t, 0))
    st_spec = pl.BlockSpec((bt, S5_FLAT), lambda b, t: (b, 0))
    consts = [perm, permt, bblk, cblk, lre, lim, dvec]
    in_specs = [u_spec] + [_const_spec(c.shape) for c in consts]
    args = [u] + consts
    if has_h0:
        in_specs += [st_spec, st_spec]
        args += [h0re, h0im]
    return pl.pallas_call(
        functools.partial(_s5_kernel, tl=tl, has_h0=has_h0, bt=bt),
        grid=(nb, nt),
        in_specs=in_specs,
        out_specs=[u_spec, st_spec, st_spec],
        out_shape=[jax.ShapeDtypeStruct(u.shape, BF),
                   jax.ShapeDtypeStruct((batch, S5_FLAT), F32),
                   jax.ShapeDtypeStruct((batch, S5_FLAT), F32)],
        scratch_shapes=[pltpu.VMEM((m, S5_FLAT), F32), pltpu.VMEM((m, S5_FLAT), F32),
                        pltpu.VMEM((bt, S5_FLAT), F32), pltpu.VMEM((bt, S5_FLAT), F32),
                        pltpu.VMEM((S5_SLABS, m, LANES), F32), pltpu.VMEM((S5_SLABS, m, LANES), F32)],
        compiler_params=_params(2),
        name="s5_sample" if has_h0 else "s5_prompt",
    )(*args)


def _mixout_kernel(y_ref, z_ref, y5_ref, ga_ref, gb_ref, x_ref, ng_ref, wa_ref, wglu_ref, bglu_ref, wmix_ref, o_ref):
    y = y_ref[...].astype(F32) * _silu(z_ref[...].astype(F32))
    yn = _rms(y, ng_ref[...]).astype(BF)
    branch_a = _dot(yn, wa_ref[...])
    glu = _dot(y5_ref[...], wglu_ref[...]) + bglu_ref[...]
    branch_b = glu[:, :D_MODEL] * _sigmoid(glu[:, D_MODEL:])
    merged = _sigmoid(ga_ref[...].astype(F32)) * branch_a + _sigmoid(gb_ref[...].astype(F32)) * branch_b
    o_ref[...] = x_ref[...] + _dot(merged.astype(BF), wmix_ref[...])


def _mixout(y, z, y5, ga, gb, x2d, ng, wa, wglu, bglu, wmix):
    t = x2d.shape[0]
    tm = MIX_TILE
    row = pl.BlockSpec((tm, D_MODEL), lambda i: (i, 0))
    consts = [ng, wa, wglu, bglu, wmix]
    return pl.pallas_call(
        _mixout_kernel,
        grid=(t // tm,),
        in_specs=[row] * 6 + [_resident_spec(c.shape) for c in consts],
        out_specs=row,
        out_shape=jax.ShapeDtypeStruct((t, D_MODEL), F32),
        compiler_params=_params(1),
        name="mixout",
    )(y, z, y5, ga, gb, x2d, *consts)


def _attn_kernel(h_ref, g_ref, wq_ref, wo_ref, k_ref, v_ref, o_ref, obuf, *, nkv, rows_per_seq):
    h = h_ref[...]
    r = h.shape[0]
    q32 = _dot(_rms(h, g_ref[...]).astype(BF), wq_ref[...])
    scale = MEM_HEAD_DIM ** -0.5
    if len(k_ref.shape) == 4:
        rq = rows_per_seq
        nrow = MEM_HEADS * rq
        row_head = lax.shift_right_logical(lax.broadcasted_iota(I32, (nrow, MEM_LEN * MEM_HEADS), 0), int(math.log2(rq)))
        col_head = lax.broadcasted_iota(I32, (nrow, MEM_LEN * MEM_HEADS), 1) & (MEM_HEADS - 1)
        visible = row_head == col_head
        for i in range(nkv):
            qs = jnp.concatenate([q32[i * rq:(i + 1) * rq, hd * MEM_HEAD_DIM:(hd + 1) * MEM_HEAD_DIM]
                                  for hd in range(MEM_HEADS)], axis=0).astype(BF)
            kf = k_ref[i].reshape(MEM_LEN * MEM_HEADS, MEM_HEAD_DIM).astype(BF)
            vf = v_ref[i].reshape(MEM_LEN * MEM_HEADS, MEM_HEAD_DIM).astype(BF)
            s = jnp.where(visible, _dot_nt(qs, kf) * scale, NEG_BIG)
            p = jnp.exp(s - jnp.max(s, axis=-1, keepdims=True))
            p = p / jnp.sum(p, axis=-1, keepdims=True)
            o = _dot(p.astype(BF), vf)
            for hd in range(MEM_HEADS):
                obuf[i * rq:(i + 1) * rq, hd * MEM_HEAD_DIM:(hd + 1) * MEM_HEAD_DIM] = o[hd * rq:(hd + 1) * rq]
    else:
        qv = q32.astype(BF)
        for hd in range(MEM_HEADS):
            sl = slice(hd * MEM_HEAD_DIM, (hd + 1) * MEM_HEAD_DIM)
            s = _dot_nt(qv[:, sl], k_ref[0, :, sl].astype(BF)) * scale
            p = jnp.exp(s - jnp.max(s, axis=-1, keepdims=True))
            p = p / jnp.sum(p, axis=-1, keepdims=True)
            obuf[:, sl] = _dot(p.astype(BF), v_ref[0, :, sl].astype(BF))
    o_ref[...] = h + _dot(obuf[...].astype(BF), wo_ref[...])


def _attn(h2d, g, wq, wo, k3, v3, *, rows, nkv, rows_per_seq):
    t = h2d.shape[0]
    nsteps = t // rows
    steps_per_kv = nsteps // (k3.shape[0] // nkv)
    row = pl.BlockSpec((rows, D_MODEL), lambda i: (i, 0))
    if k3.ndim == 4:
        kv = pl.BlockSpec((nkv, MEM_LEN, MEM_HEADS, MEM_HEAD_DIM), lambda i: (i // steps_per_kv, 0, 0, 0))
    else:
        kv = pl.BlockSpec((nkv, MEM_LEN, D_MODEL), lambda i: (i // steps_per_kv, 0, 0))
    consts = [g, wq, wo]
    return pl.pallas_call(
        functools.partial(_attn_kernel, nkv=nkv, rows_per_seq=rows_per_seq),
        grid=(nsteps,),
        in_specs=[row] + [_const_spec(c.shape) for c in consts] + [kv, kv],
        out_specs=row,
        out_shape=jax.ShapeDtypeStruct((t, D_MODEL), F32),
        scratch_shapes=[pltpu.VMEM((rows, D_MODEL), F32)],
        compiler_params=_params(1),
        name="attn",
    )(h2d, *consts, k3, v3)


ROUTER_LANE0 = MOE_GROUPS

U32 = jnp.uint32
HALF = D_MODEL // 2


def _pack_rows(x):
    xr = x.astype(BF).astype(F32)
    lo = lax.shift_right_logical(pltpu.bitcast(xr[:, :HALF], U32), jnp.uint32(16))
    hi = pltpu.bitcast(xr[:, HALF:], U32) & jnp.uint32(0xFFFF0000)
    return hi | lo


def _unpack_rows(w):
    lo = pltpu.bitcast(lax.shift_left(w, jnp.uint32(16)), F32).astype(BF)
    hi = pltpu.bitcast(w & jnp.uint32(0xFFFF0000), F32).astype(BF)
    return lo, hi


GRANULE = 8
LOCAL_USED = 2 * TOKEN_TILE + MOE_EXPERTS * (GRANULE - 1)
LOCAL_ROWS = 1280
SPARE_GRANULES = (LOCAL_ROWS - LOCAL_USED) // GRANULE
TILE_GRANULES = MOE_ROW_TILE // GRANULE


def _route_kernel(hp_ref, hs_ref, g_ref, wcat_ref, wh_ref, b_ref, tri_ref, eye_ref, upper_ref,
                  xl_ref, idx_ref, wt_ref, seg_ref, *, np_steps):
    i = pl.program_id(0)
    h = jnp.where(i < np_steps, hp_ref[...], hs_ref[...])
    xn = _rms(h, g_ref[...])
    tm = xn.shape[0]
    xh = xn.astype(BF)
    xm = (xn - xh.astype(F32)).astype(BF)
    both = _dot(xh, wcat_ref[...])
    logits = both[:, :LANES] + both[:, LANES:] + _dot(xm, wh_ref[...]) + b_ref[...]
    lane = lax.broadcasted_iota(I32, logits.shape, 1)
    lane_f = lane.astype(F32)
    no_lane = float(LANES)
    is_g = lane < MOE_GROUPS
    lg = jnp.where(is_g, logits, NEG_BIG)
    mg = jnp.max(lg, axis=-1, keepdims=True)
    gidx = jnp.min(jnp.where(lg == mg, lane_f, no_lane), axis=-1, keepdims=True).astype(I32)
    g_w = 1.0 / jnp.sum(jnp.where(is_g, jnp.exp(lg - mg), 0.0), axis=-1, keepdims=True)
    e_lane = lane - ROUTER_LANE0
    in_grp = ((e_lane >= 0) & (e_lane < MOE_EXPERTS)
              & (lax.shift_right_logical(jnp.maximum(e_lane, 0), 3) == gidx))
    le = jnp.where(in_grp, logits, NEG_BIG)
    m1 = jnp.max(le, axis=-1, keepdims=True)
    i1 = jnp.min(jnp.where(le == m1, lane_f, no_lane), axis=-1, keepdims=True)
    le2 = jnp.where(lane_f == i1, NEG_BIG, le)
    m2 = jnp.max(le2, axis=-1, keepdims=True)
    i2 = jnp.min(jnp.where(le2 == m2, lane_f, no_lane), axis=-1, keepdims=True)
    ratio = jnp.exp(m2 - m1)
    w1 = g_w / (1.0 + ratio)
    w2 = g_w * ratio / (1.0 + ratio)
    wt_ref[...] = jnp.where(lane == 0, w1, jnp.where(lane == 1, w2, 0.0))

    oh1 = jnp.where(lane_f == i1, 1.0, 0.0)
    oh2 = jnp.where(lane_f == i2, 1.0, 0.0)
    tot1 = jnp.sum(oh1, axis=0, keepdims=True)
    cnt = (tot1 + jnp.sum(oh2, axis=0, keepdims=True)).astype(I32)
    padded = lax.shift_left(lax.shift_right_logical(cnt + (GRANULE - 1), 3), 3)
    pad8 = jnp.broadcast_to(padded.astype(F32), (8, LANES)).astype(BF)
    lstart = _dot(pad8, upper_ref[...])[0:1, :]
    tri = tri_ref[...]
    l1 = jnp.sum(oh1 * (_dot(tri, oh1.astype(BF)) + lstart), axis=-1, keepdims=True)
    l2 = jnp.sum(oh2 * (_dot(tri, oh2.astype(BF)) + (lstart + tot1)), axis=-1, keepdims=True)
    idx_ref[...] = jnp.where(lane == 0, l1.astype(I32), jnp.where(lane == 1, l2.astype(I32), 0))
    sub = lax.broadcasted_iota(I32, (8, LANES), 0)
    seg_ref[0] = jnp.where(sub == 0, jnp.broadcast_to(cnt, (8, LANES)),
                           jnp.where(sub == 1, jnp.broadcast_to(lstart.astype(I32), (8, LANES)), 0))

    cols = jnp.where(lane == 0, l1, jnp.where(lane == 1, l2, 0.0))
    rows = _dot_exact(eye_ref[...], cols, nt=True)
    r_id = lax.broadcasted_iota(I32, (LOCAL_ROWS, tm), 0).astype(F32)
    perm = jnp.where(r_id == rows[0:1, :], 1.0, 0.0) + jnp.where(r_id == rows[1:2, :], 1.0, 0.0)
    xl_ref[...] = _pack_rows(_dot(perm.astype(BF), xh))


def _route(hp, hs, g, wcat, wh, b):
    tm = TOKEN_TILE
    np_steps = hp.shape[0] // tm
    ns_steps = hs.shape[0] // tm
    nt = np_steps + ns_steps
    tri = jnp.asarray(np.tril(np.ones((tm, tm), np.float32), -1)).astype(BF)
    eye8 = jnp.asarray(np.eye(8, LANES, dtype=np.float32)).astype(BF)
    upper = jnp.asarray(np.triu(np.ones((LANES, LANES), np.float32), 1)).astype(BF)
    consts = [g, wcat, wh, b, tri, eye8, upper]
    return pl.pallas_call(
        functools.partial(_route_kernel, np_steps=np_steps),
        grid=(nt,),
        in_specs=[pl.BlockSpec((tm, D_MODEL), lambda i: (jnp.minimum(i, np_steps - 1), 0)),
                  pl.BlockSpec((tm, D_MODEL), lambda i: (jnp.maximum(i - np_steps, 0), 0))]
                 + [_const_spec(c.shape) for c in consts],
        out_specs=[pl.BlockSpec((LOCAL_ROWS, HALF), lambda i: (i, 0)),
                   pl.BlockSpec((tm, LANES), lambda i: (i, 0)),
                   pl.BlockSpec((tm, LANES), lambda i: (i, 0)),
                   pl.BlockSpec((1, 8, LANES), lambda i: (i, 0, 0))],
        out_shape=[jax.ShapeDtypeStruct((nt * LOCAL_ROWS, HALF), U32),
                   jax.ShapeDtypeStruct((nt * tm, LANES), I32),
                   jax.ShapeDtypeStruct((nt * tm, LANES), F32),
                   jax.ShapeDtypeStruct((nt, 8, LANES), I32)],
        compiler_params=_params(1),
        name="route",
    )(hp, hs, *consts)


def _granule_plan(seg, n_tiles):
    cnt = seg[:, 0, ROUTER_LANE0:ROUTER_LANE0 + MOE_EXPERTS]
    lstart = seg[:, 1, ROUTER_LANE0:ROUTER_LANE0 + MOE_EXPERTS]
    ntt = cnt.shape[0]
    ng = (cnt + GRANULE - 1) // GRANULE
    cum = jnp.cumsum(ng, axis=0)
    first = (jnp.arange(ntt, dtype=I32)[:, None] * LOCAL_ROWS + lstart) // GRANULE - (cum - ng)
    total = cum[-1]
    tiles_e = (total + TILE_GRANULES - 1) // TILE_GRANULES
    tend = jnp.cumsum(tiles_e)
    tstart = tend - tiles_e
    tile = jnp.arange(n_tiles, dtype=I32)
    te = jnp.minimum(jnp.sum((tile[:, None] >= tend[None, :]).astype(I32), axis=1), MOE_EXPERTS - 1)
    n_used = tend[-1].astype(I32).reshape(1)
    sel = (te[:, None] == jnp.arange(MOE_EXPERTS, dtype=I32)[None, :]).astype(I32)
    pick = lambda table: jnp.sum(sel[:, None, :] * table[None, :, :], axis=2)
    cum_t, first_t = pick(cum), pick(first)
    off = (tile - jnp.sum(sel * tstart[None, :], axis=1))[:, None] * TILE_GRANULES + jnp.arange(TILE_GRANULES, dtype=I32)[None, :]
    valid = (off < jnp.sum(sel * total[None, :], axis=1)[:, None]) & (tile < n_used[0])[:, None]
    j = jnp.sum((cum_t[:, None, :] <= off[:, :, None]).astype(I32), axis=2)
    jsel = (j[:, :, None] == jnp.arange(ntt, dtype=I32)[None, None, :]).astype(I32)
    src = jnp.sum(jsel * first_t[:, None, :], axis=2) + off
    spare = lambda k: ((k // SPARE_GRANULES) * LOCAL_ROWS + LOCAL_USED) // GRANULE + k % SPARE_GRANULES
    assert (ntt - 1) * SPARE_GRANULES >= 2 * TILE_GRANULES
    zero_granule = spare((ntt - 1) * SPARE_GRANULES)
    trash = spare((tile % 2)[:, None] * TILE_GRANULES + jnp.arange(TILE_GRANULES, dtype=I32)[None, :])
    gsrc = jnp.where(valid, src, zero_granule).astype(I32).reshape(-1)
    gdst = jnp.where(valid, src, trash).astype(I32).reshape(-1)
    n_valid = jnp.sum(valid.astype(I32), axis=1).astype(I32)
    return te.astype(I32), n_used, gsrc, gdst, n_valid


def _grouped_kernel(te_ref, nu_ref, gsrc_ref, gdst_ref, nv_ref, xin_ref, wg_ref, wu_ref, wd_ref, xout_ref,
                    xbuf, ybuf, wg16, wu16, wd16, sem_in, sem_out):
    i = pl.program_id(0)
    nu = nu_ref[0]
    tm = MOE_ROW_TILE

    def hbm_rows(table_ref, tile, g):
        return pl.ds(pl.multiple_of(table_ref[tile * TILE_GRANULES + g] * GRANULE, GRANULE), GRANULE)

    def fetch_start(tile, slot):
        for g in range(TILE_GRANULES):
            pltpu.make_async_copy(xin_ref.at[hbm_rows(gsrc_ref, tile, g)],
                                  xbuf.at[slot, pl.ds(g * GRANULE, GRANULE)], sem_in.at[slot]).start()

    def store_start(tile, slot):
        for g in range(TILE_GRANULES):
            pltpu.make_async_copy(ybuf.at[slot, pl.ds(g * GRANULE, GRANULE)],
                                  xout_ref.at[hbm_rows(gdst_ref, tile, g)], sem_out.at[slot]).start()

    def fetch_wait(slot):
        pltpu.make_async_copy(xin_ref.at[pl.ds(0, tm)], xbuf.at[slot], sem_in.at[slot]).wait()

    def store_wait(slot):
        pltpu.make_async_copy(ybuf.at[slot], xout_ref.at[pl.ds(0, tm)], sem_out.at[slot]).wait()

    def step(slot):
        @pl.when(i == 0)
        def _():
            ybuf[...] = jnp.zeros_like(ybuf)
            fetch_start(0, 0)

        @pl.when(i + 1 < nu)
        def _():
            fetch_start(i + 1, 1 - slot)

        @pl.when(i >= 2)
        def _():
            store_wait(slot)

        @pl.when((i == 0) | (te_ref[i] != te_ref[jnp.maximum(i - 1, 0)]))
        def _():
            wg16[...] = wg_ref[0].astype(BF)
            wu16[...] = wu_ref[0].astype(BF)
            wd16[...] = wd_ref[0].astype(BF)

        fetch_wait(slot)
        n_valid = nv_ref[i]
        quarter = TILE_GRANULES // 4
        for part in range(1, 5):
            rows = part * quarter * GRANULE

            @pl.when((n_valid > (part - 1) * quarter) & (n_valid <= part * quarter))
            def _(rows=rows):
                x_lo, x_hi = _unpack_rows(xbuf[slot, 0:rows, :])
                gate = _dot(x_lo, wg16[0:HALF, :]) + _dot(x_hi, wg16[HALF:, :])
                up = _dot(x_lo, wu16[0:HALF, :]) + _dot(x_hi, wu16[HALF:, :])
                ybuf[slot, 0:rows, :] = _pack_rows(_dot((_silu(gate) * up).astype(BF), wd16[...]))

        store_start(i, slot)

        @pl.when(i == nu - 1)
        def _():
            store_wait(slot)

            @pl.when(i >= 1)
            def _():
                store_wait(1 - slot)

    for slot in (0, 1):
        @pl.when((i < nu) & (lax.rem(i, 2) == slot))
        def _(slot=slot):
            step(slot)


def _grouped(x_local, te, n_used, gsrc, gdst, n_valid, wg, wu, wd):
    tm = MOE_ROW_TILE
    n_tiles = te.shape[0]
    used = lambda i, nu: jnp.minimum(i, jnp.maximum(nu[0] - 1, 0))
    wspec = lambda shape: pl.BlockSpec((1,) + shape, lambda i, te, nu, *_: (te[used(i, nu)], 0, 0))
    grid_spec = pltpu.PrefetchScalarGridSpec(
        num_scalar_prefetch=5,
        grid=(n_tiles,),
        in_specs=[pl.BlockSpec(memory_space=pl.ANY), wspec((D_MODEL, MOE_D_FF)), wspec((D_MODEL, MOE_D_FF)),
                  wspec((MOE_D_FF, D_MODEL))],
        out_specs=pl.BlockSpec(memory_space=pl.ANY),
        scratch_shapes=[pltpu.VMEM((2, tm, HALF), U32), pltpu.VMEM((2, tm, HALF), U32),
                        pltpu.VMEM((D_MODEL, MOE_D_FF), BF), pltpu.VMEM((D_MODEL, MOE_D_FF), BF),
                        pltpu.VMEM((MOE_D_FF, D_MODEL), BF),
                        pltpu.SemaphoreType.DMA((2,)), pltpu.SemaphoreType.DMA((2,))],
    )
    return pl.pallas_call(
        _grouped_kernel,
        grid_spec=grid_spec,
        out_shape=jax.ShapeDtypeStruct(x_local.shape, x_local.dtype),
        input_output_aliases={5: 0},
        compiler_params=_params(1),
        name="grouped",
    )(te, n_used, gsrc, gdst, n_valid, x_local, wg, wu, wd)


def _merge_kernel(h_ref, idx_ref, wt_ref, g_ref, yl_ref, o_ref):
    tm = h_ref.shape[0]
    idx = idx_ref[...]
    wt = wt_ref[...]
    r_id = lax.broadcasted_iota(I32, (tm, LOCAL_ROWS), 1)
    sel = jnp.where(r_id == idx[:, 0:1], wt[:, 0:1], 0.0) + jnp.where(r_id == idx[:, 1:2], wt[:, 1:2], 0.0)
    y_lo, y_hi = _unpack_rows(yl_ref[...])
    sel16 = sel.astype(BF)
    moe = jnp.concatenate([_dot(sel16, y_lo), _dot(sel16, y_hi)], axis=1)
    o_ref[...] = _rms(h_ref[...] + moe, g_ref[...])


def _merge(h2d, idx, wt, g, y_local, *, tile_offset):
    t = h2d.shape[0]
    tm = TOKEN_TILE
    return pl.pallas_call(
        _merge_kernel,
        grid=(t // tm,),
        in_specs=[pl.BlockSpec((tm, D_MODEL), lambda i: (i, 0)),
                  pl.BlockSpec((tm, LANES), lambda i: (i + tile_offset, 0)),
                  pl.BlockSpec((tm, LANES), lambda i: (i + tile_offset, 0)),
                  _const_spec(g.shape),
                  pl.BlockSpec((LOCAL_ROWS, HALF), lambda i: (i + tile_offset, 0))],
        out_specs=pl.BlockSpec((tm, D_MODEL), lambda i: (i, 0)),
        out_shape=jax.ShapeDtypeStruct((t, D_MODEL), F32),
        compiler_params=_params(1),
        name="merge",
    )(h2d, idx, wt, g, y_local)


def _perm_matrix(bt, tl):
    m = bt * tl
    p = np.zeros((m, m), np.float32)
    for l in range(tl):
        for b in range(bt):
            p[l * bt + b, b * tl + l] = 1.0
    return p


def _s5_operators(lam_re, lam_im, log_step, b_re, b_im, c_re, c_im):
    lam = lax.complex(lam_re, lam_im)
    step = jnp.exp(log_step)[:, None]
    lam_bar = jnp.exp(lam * step)
    b_bar = ((lam_bar - 1.0) / lam)[..., None] * lax.complex(b_re, b_im)
    gl = LANES // S5_GROUP
    eye = jnp.eye(gl, dtype=F32)

    def in_block(bpart):
        bp = bpart.reshape(S5_SLABS, gl, S5_STATE, S5_GROUP)
        blk = jnp.einsum("sgnk,gh->sgkhn", bp, eye)
        return blk.reshape(S5_SLABS, LANES, gl * S5_STATE)

    def out_block(cpart):
        cp = cpart.reshape(S5_SLABS, gl, S5_GROUP, S5_STATE)
        blk = jnp.einsum("sgkn,gh->sgnhk", cp, eye)
        return blk.reshape(S5_SLABS, gl * S5_STATE, LANES)

    bblk = jnp.concatenate([in_block(jnp.real(b_bar)), in_block(jnp.imag(b_bar))], axis=2).astype(BF)
    cblk = jnp.concatenate([out_block(c_re), out_block(-c_im)], axis=1).astype(BF)
    return bblk, cblk, jnp.real(lam_bar).reshape(1, S5_FLAT), jnp.imag(lam_bar).reshape(1, S5_FLAT)


def _pad_lanes(v, width=LANES):
    return jnp.pad(v, [(0, 0)] * (v.ndim - 1) + [(0, width - v.shape[-1])])


def kernel(x_prompt, x_sample, state_conv, state_ssd, state_s5_re, state_s5_im, cache_mem_k, cache_mem_v, mem_prompt, norm_mix_g, w_in, conv_w, conv_b, ssd_dt_bias, ssd_a_log, ssd_d, ssd_norm_g, w_ssd_branch, s5_lambda_re, s5_lambda_im, s5_log_step, s5_b_re, s5_b_im, s5_c_re, s5_c_im, s5_d, w_glu, b_glu, w_mix_out, norm_mem_q_g, norm_mem_kv_g, w_mem_q, w_mem_k, w_mem_v, w_mem_o, norm_ffn_g, w_router_group, b_router_group, w_router_expert, b_router_expert, w_exp_gate, w_exp_up, w_exp_down, norm_final_g):
    bp, lp, _ = x_prompt.shape
    bs, ls, _ = x_sample.shape
    tp, ts = bp * lp, bs * ls
    row1 = lambda v: v.reshape(1, -1).astype(F32)

    w_in0 = w_in[0]
    o_xbc = D_MODEL
    o_dt = o_xbc + SSD_CONV_DIM
    o_u5 = o_dt + SSD_HEADS
    w_main = jnp.concatenate([w_in0[:, :o_dt], w_in0[:, o_u5:]], axis=1).astype(BF)
    w_dt = _pad_lanes(w_in0[:, o_dt:o_u5]).astype(BF)
    dt_bias = _pad_lanes(row1(ssd_dt_bias[0]))
    a_log = _pad_lanes(row1(ssd_a_log[0]))
    dskip = jnp.repeat(ssd_d[0].astype(F32), SSD_HEADDIM).reshape(1, D_MODEL)
    eexp_np = np.zeros((LANES, D_MODEL), np.float32)
    for h in range(SSD_HEADS):
        eexp_np[h, h * SSD_HEADDIM:(h + 1) * SSD_HEADDIM] = 1.0
    eexp = jnp.asarray(eexp_np).astype(BF)
    eye16 = jnp.asarray(np.eye(SSD_HEADS, LANES, dtype=np.float32)).astype(BF)
    bblk, cblk, lre, lim = _s5_operators(s5_lambda_re[0], s5_lambda_im[0], s5_log_step[0], s5_b_re[0],
                                         s5_b_im[0], s5_c_re[0], s5_c_im[0])
    s5d = row1(s5_d[0])
    w_kv = jnp.concatenate([w_mem_k[0], w_mem_v[0]], axis=1).astype(BF)
    w_r = _pad_lanes(jnp.concatenate([w_router_group[0], w_router_expert[0]], axis=1).astype(F32))
    wr_h, wr_m, _ = _split3(w_r)
    b_r = _pad_lanes(row1(jnp.concatenate([b_router_group[0], b_router_expert[0]])))
    wg = w_exp_gate[0].reshape(MOE_EXPERTS, D_MODEL, MOE_D_FF)
    wu = w_exp_up[0].reshape(MOE_EXPERTS, D_MODEL, MOE_D_FF)
    wd = w_exp_down[0].reshape(MOE_EXPERTS, MOE_D_FF, D_MODEL)

    def mixer(x2d, batch, seqlen, conv0, h0, s5re0, s5im0):
        z, xbc, u5, ga, gb, dt = _inproj(x2d, row1(norm_mix_g[0]), w_main, w_dt, dt_bias)
        y_ssd, new_conv, new_ssd = _ssd(xbc, dt, conv_w[0].astype(F32), row1(conv_b[0]), a_log, dskip, eexp, eye16,
                                        conv0, h0, batch=batch, seqlen=seqlen)
        if h0 is None:
            tl, bt = S5_TIME_TILE, S5_PROMPT_SEQS
            u_in = u5.reshape(batch, seqlen, D_MODEL)
        else:
            tl, bt = seqlen, S5_SAMPLE_SEQS
            u_in = u5
        perm = _perm_matrix(bt, tl)
        lam_rows = [jnp.broadcast_to(v, (bt, S5_FLAT)) for v in (lre, lim)]
        y5, new_re, new_im = _s5(u_in, jnp.asarray(perm).astype(BF), jnp.asarray(perm.T).astype(BF), bblk, cblk,
                                 *lam_rows, s5d, s5re0, s5im0, batch=batch, seqlen=seqlen, tl=tl, bt=bt)
        h1 = _mixout(y_ssd, z, y5.reshape(-1, D_MODEL), ga, gb, x2d, row1(ssd_norm_g[0]),
                     w_ssd_branch[0].astype(BF), w_glu[0].astype(BF), row1(b_glu[0]), w_mix_out[0].astype(BF))
        return h1, new_conv, new_ssd, new_re, new_im

    xp2 = x_prompt.reshape(tp, D_MODEL)
    xs2 = x_sample.reshape(ts, D_MODEL)
    h1p, conv_p, ssd_p, re_p, im_p = mixer(xp2, bp, lp, None, None, None, None)
    h1s, conv_s, ssd_s, re_s, im_s = mixer(xs2, bs, ls, state_conv[0], state_ssd[0],
                                           state_s5_re[0].reshape(bs, S5_FLAT), state_s5_im[0].reshape(bs, S5_FLAT))

    mk_p, mv_p = _memkv(mem_prompt.reshape(bp * MEM_LEN, D_MODEL), row1(norm_mem_kv_g[0]), w_kv)
    gq = row1(norm_mem_q_g[0])
    wq = w_mem_q[0].astype(BF)
    wo = w_mem_o[0].astype(BF)
    h2p = _attn(h1p, gq, wq, wo, mk_p.reshape(bp, MEM_LEN, D_MODEL), mv_p.reshape(bp, MEM_LEN, D_MODEL),
                rows=MIX_TILE, nkv=1, rows_per_seq=lp)
    sample_seqs = 8
    h2s = _attn(h1s, gq, wq, wo, cache_mem_k[0], cache_mem_v[0], rows=sample_seqs * ls, nkv=sample_seqs,
                rows_per_seq=ls)

    g_ffn = row1(norm_ffn_g[0])
    x_local, r_idx, r_wt, seg = _route(h2p, h2s, g_ffn, jnp.concatenate([wr_h, wr_m], axis=1), wr_h, b_r)
    token_tiles = (tp + ts) // TOKEN_TILE
    max_granules = (2 * (tp + ts) + token_tiles * MOE_EXPERTS * (GRANULE - 1)) // GRANULE
    n_tiles = max_granules // TILE_GRANULES + MOE_EXPERTS
    te, n_used, gsrc, gdst, n_valid = _granule_plan(seg, n_tiles)
    y_local = _grouped(x_local, te, n_used, gsrc, gdst, n_valid, wg, wu, wd)
    gf = row1(norm_final_g)
    y_prompt = _merge(h2p, r_idx, r_wt, gf, y_local, tile_offset=0)
    y_sample = _merge(h2s, r_idx, r_wt, gf, y_local, tile_offset=tp // TOKEN_TILE)

    return (y_prompt.reshape(bp, lp, D_MODEL), y_sample.reshape(bs, ls, D_MODEL),
            conv_p[None], ssd_p[None],
            re_p.reshape(1, bp, S5_GROUPS, S5_STATE), im_p.reshape(1, bp, S5_GROUPS, S5_STATE),
            mk_p.reshape(1, bp, MEM_LEN, MEM_HEADS, MEM_HEAD_DIM), mv_p.reshape(1, bp, MEM_LEN, MEM_HEADS, MEM_HEAD_DIM),
            conv_s[None], ssd_s[None],
            re_s.reshape(1, bs, S5_GROUPS, S5_STATE), im_s.reshape(1, bs, S5_GROUPS, S5_STATE))
```

```python
import functools
import math

import numpy as np
import jax
import jax.numpy as jnp
from jax import lax
from jax.experimental import pallas as pl
from jax.experimental.pallas import tpu as pltpu

F32 = jnp.float32
BF = jnp.bfloat16
I32 = jnp.int32

D_MODEL = 1024
SSD_HEADS = 16
SSD_HEADDIM = 64
SSD_GROUPS = 2
SSD_STATE = 128
SSD_CONV = 4
SSD_CONV_DIM = 1536
HEADS_PER_GROUP = SSD_HEADS // SSD_GROUPS
S5_GROUPS = 64
S5_GROUP = 16
S5_STATE = 64
S5_FLAT = S5_GROUPS * S5_STATE
S5_SLABS = D_MODEL // 128
MEM_LEN = 256
MEM_HEADS = 4
MEM_HEAD_DIM = 256
MOE_GROUPS = 4
MOE_EXPERTS_PER_GROUP = 8
MOE_EXPERTS = MOE_GROUPS * MOE_EXPERTS_PER_GROUP
MOE_D_FF = 512
NORM_EPS = 1e-6

LANES = 128
VMEM_LIMIT_BYTES = 56 * 1024 * 1024
TOKEN_TILE = 512
MIX_TILE = 1024
SSD_CHUNK = 128
SSD_SEQS_PER_STEP = 4
S5_TIME_TILE = 64
MOE_ROW_TILE = 512
NEG_BIG = -1e30


def _params(n_axes):
    return pltpu.CompilerParams(dimension_semantics=("arbitrary",) * n_axes,
                                vmem_limit_bytes=VMEM_LIMIT_BYTES)


def _const_spec(shape):
    nd = len(shape)
    return pl.BlockSpec(shape, lambda *_: (0,) * nd)


def _resident_spec(shape):
    nd = len(shape)
    return pl.BlockSpec(shape, lambda *_: (0,) * nd, pipeline_mode=pl.Buffered(1))


def _dot(a, b):
    return jnp.dot(a, b, preferred_element_type=F32)


def _dot_nt(a, b):
    return lax.dot_general(a, b, (((1,), (1,)), ((), ())), preferred_element_type=F32)


def _split3(v):
    h = v.astype(BF)
    r = v - h.astype(F32)
    m = r.astype(BF)
    l = (r - m.astype(F32)).astype(BF)
    return h, m, l


def _dot_exact(a_bf, v, nt=False):
    f = _dot_nt if nt else _dot
    h, m, l = _split3(v)
    return f(a_bf, h) + f(a_bf, m) + f(a_bf, l)


def _rms(x, g):
    return x * lax.rsqrt(jnp.mean(x * x, axis=-1, keepdims=True) + NORM_EPS) * g


def _sigmoid(x):
    return 0.5 * jnp.tanh(0.5 * x) + 0.5


def _silu(x):
    return x * _sigmoid(x)


def _proj_cols(xb, w_ref, o_ref, c0, width):
    for c in range(0, width, 512):
        ce = min(c + 512, width)
        o_ref[:, c:ce] = _dot(xb, w_ref[:, c0 + c:c0 + ce]).astype(o_ref.dtype)


def _inproj_kernel(x_ref, g_ref, wa_ref, wb_ref, wdt_ref, dtb_ref, z_ref, xbc_ref, u5_ref, ga_ref, gb_ref, dt_ref):
    xb = _rms(x_ref[...], g_ref[...]).astype(BF)
    for w_ref, outs in ((wa_ref, (z_ref, xbc_ref)), (wb_ref, (u5_ref, ga_ref, gb_ref))):
        c0 = 0
        for o_ref in outs:
            width = o_ref.shape[1]
            _proj_cols(xb, w_ref, o_ref, c0, width)
            c0 += width
    raw = _dot(xb, wdt_ref[...]) + dtb_ref[...]
    dt_ref[...] = jnp.maximum(raw, 0.0) + jnp.log1p(jnp.exp(-jnp.abs(raw)))


def _inproj(x2d, g, w_a, w_b, w_dt, dt_bias):
    t = x2d.shape[0]
    tm = MIX_TILE
    widths = (D_MODEL, SSD_CONV_DIM, D_MODEL, D_MODEL, D_MODEL)
    row = lambda w: pl.BlockSpec((tm, w), lambda i: (i, 0))
    return pl.pallas_call(
        _inproj_kernel,
        grid=(t // tm,),
        in_specs=[row(D_MODEL), _const_spec(g.shape), _resident_spec(w_a.shape), _resident_spec(w_b.shape),
                  _resident_spec(w_dt.shape), _const_spec(dt_bias.shape)],
        out_specs=[row(w) for w in widths] + [row(LANES)],
        out_shape=[jax.ShapeDtypeStruct((t, w), BF) for w in widths] + [jax.ShapeDtypeStruct((t, LANES), F32)],
        compiler_params=_params(1),
        name="inproj",
    )(x2d, g, w_a, w_b, w_dt, dt_bias)


def _memkv_kernel(x_ref, g_ref, w_ref, k_ref, v_ref):
    xb = _rms(x_ref[...], g_ref[...]).astype(BF)
    _proj_cols(xb, w_ref, k_ref, 0, D_MODEL)
    _proj_cols(xb, w_ref, v_ref, D_MODEL, D_MODEL)


def _memkv(mem2d, g, w_kv):
    t = mem2d.shape[0]
    tm = TOKEN_TILE
    row = pl.BlockSpec((tm, D_MODEL), lambda i: (i, 0))
    return pl.pallas_call(
        _memkv_kernel,
        grid=(t // tm,),
        in_specs=[row, _const_spec(g.shape), _const_spec(w_kv.shape)],
        out_specs=[row, row],
        out_shape=[jax.ShapeDtypeStruct((t, D_MODEL), F32)] * 2,
        compiler_params=_params(1),
        name="memkv",
    )(mem2d, g, w_kv)


def _ssd_kernel(*refs, lq, has_h0, nb):
    nseq = SSD_CHUNK // lq
    n_in = 11 if has_h0 else 9
    xbc_ref, dt_ref = refs[0], refs[1]
    consts = refs[2:9]
    state_in = refs[9:n_in]
    y_ref, convo_ref, ho_ref = refs[n_in:n_in + 3]
    hbuf, cacc, yacc, yint, xwt, tot_s = refs[n_in + 3:]
    for s in range(nb):
        seqs = pl.ds(s * nseq, nseq)
        _ssd_chunk(xbc_ref.at[s], dt_ref.at[s], *consts, *[r.at[seqs] for r in state_in],
                   y_ref.at[s], convo_ref.at[seqs], ho_ref.at[seqs], hbuf.at[seqs],
                   cacc.at[s], yacc.at[s], yint.at[s], xwt.at[s], tot_s.at[s], lq=lq, has_h0=has_h0)


def _ssd_chunk(*refs, lq, has_h0):
    q = SSD_CHUNK
    nseq = q // lq
    if has_h0:
        (xbc_ref, dt_ref, cw_ref, cb_ref, alog_ref, dsk_ref, eexp_ref, eye_ref, shift_ref, conv0_ref, h0_ref,
         y_ref, convo_ref, ho_ref, hbuf, cacc, yacc, yint, xwt, tot_s) = refs
        hin_ref = h0_ref
        hbuf[...] = jnp.zeros_like(hbuf)
        hbuf[:, 0:SSD_CONV - 1, :] = conv0_ref[...]
    else:
        (xbc_ref, dt_ref, cw_ref, cb_ref, alog_ref, dsk_ref, eexp_ref, eye_ref, shift_ref,
         y_ref, convo_ref, ho_ref, hbuf, cacc, yacc, yint, xwt, tot_s) = refs
        hin_ref = ho_ref

        @pl.when(pl.program_id(1) == 0)
        def _():
            ho_ref[...] = jnp.zeros_like(ho_ref)
            hbuf[...] = jnp.zeros_like(hbuf)

    x16 = xbc_ref[...]
    xraw = x16.astype(F32)
    acc = cb_ref[...] + xraw * cw_ref[SSD_CONV - 1:SSD_CONV, :]
    for k in range(SSD_CONV - 1):
        acc = acc + _dot(shift_ref[k], x16) * cw_ref[k:k + 1, :]
    cacc[...] = acc
    for i in range(nseq):
        tail = xraw[(i + 1) * lq - (SSD_CONV - 1):(i + 1) * lq]
        convo_ref[i] = tail
        corr = hbuf[i, 0:8, :] * cw_ref[0:1, :]
        for k in range(1, SSD_CONV - 1):
            corr = corr + hbuf[i, k:k + 8, :] * cw_ref[k:k + 1, :]
        cacc[i * lq:i * lq + 8, :] += corr
        if not has_h0:
            hbuf[i, 0:SSD_CONV - 1, :] = tail
    xc = _silu(cacc[...])
    xs = xc[:, :D_MODEL]
    xs_bf = xs.astype(BF)
    bm_bf = xc[:, D_MODEL:D_MODEL + SSD_GROUPS * SSD_STATE].astype(BF)
    cm = xc[:, D_MODEL + SSD_GROUPS * SSD_STATE:]
    cm_bf = cm.astype(BF)

    dt = dt_ref[...]
    da = dt * (-jnp.exp(alog_ref[...]))
    ri = lax.broadcasted_iota(I32, (q, q), 0)
    ci = lax.broadcasted_iota(I32, (q, q), 1)
    if nseq == 1:
        causal = ci <= ri
    else:
        sh = int(math.log2(lq))
        same = lax.shift_right_logical(ri, sh) == lax.shift_right_logical(ci, sh)
        causal = same & (ci <= ri)
    lmat = jnp.where(causal, 1.0, 0.0).astype(BF)
    cs = _dot_exact(lmat, da)
    if nseq == 1:
        tot = jnp.broadcast_to(cs[q - 1:q, :], (q, LANES))
    else:
        tot = _dot_exact(jnp.where(same, 1.0, 0.0).astype(BF), da)
    tot_s[...] = tot
    eye = eye_ref[...]
    cs_t = _dot_exact(eye, cs, nt=True)
    dt_t = _dot_exact(eye, dt, nt=True)
    eexp = eexp_ref[...]
    ecs_h, ecs_m, _ = _split3(jnp.exp(cs))
    ecs_x = _dot(ecs_h, eexp) + _dot(ecs_m, eexp)
    wend_x = _dot((jnp.exp(tot - cs) * dt).astype(BF), eexp)

    lane = lax.broadcasted_iota(I32, (q, LANES), 1)
    for g in range(SSD_GROUPS):
        cbg = _dot_nt(cm_bf[:, g * SSD_STATE:(g + 1) * SSD_STATE], bm_bf[:, g * SSD_STATE:(g + 1) * SSD_STATE])
        for jp in range(HEADS_PER_GROUP // 2):
            j = g * (HEADS_PER_GROUP // 2) + jp
            ms = []
            for h in (2 * j, 2 * j + 1):
                col = jnp.broadcast_to(cs[:, h:h + 1], (q, q))
                row = jnp.broadcast_to(cs_t[h:h + 1, :], (q, q))
                dtr = jnp.broadcast_to(dt_t[h:h + 1, :], (q, q))
                ms.append((jnp.where(causal, jnp.exp(col - row), 0.0) * cbg * dtr).astype(BF))
            mp = jnp.concatenate(ms, axis=1)
            xp = xs_bf[:, j * LANES:(j + 1) * LANES]
            zero = jnp.zeros_like(xp)
            x2 = jnp.concatenate([jnp.where(lane < SSD_HEADDIM, xp, zero),
                                  jnp.where(lane >= SSD_HEADDIM, xp, zero)], axis=0)
            yacc[:, j * LANES:(j + 1) * LANES] = _dot(mp, x2)

    gw = HEADS_PER_GROUP * SSD_HEADDIM
    xw = xs * wend_x
    for g in range(SSD_GROUPS):
        xwt[g * gw:(g + 1) * gw, :] = xw[:, g * gw:(g + 1) * gw].T.astype(BF)

    if nseq == 1:
        for g in range(SSD_GROUPS):
            hin_g = hin_ref[0, g * HEADS_PER_GROUP:(g + 1) * HEADS_PER_GROUP].reshape(gw, SSD_STATE)
            yint[:, g * gw:(g + 1) * gw] = _dot_nt(cm_bf[:, g * SSD_STATE:(g + 1) * SSD_STATE], hin_g.astype(BF))
            s_new = _dot(xwt[g * gw:(g + 1) * gw, :], bm_bf[:, g * SSD_STATE:(g + 1) * SSD_STATE])
            for hl in range(HEADS_PER_GROUP):
                h = g * HEADS_PER_GROUP + hl
                dec = jnp.exp(jnp.broadcast_to(tot[0:1, h:h + 1], (SSD_HEADDIM, SSD_STATE)))
                ho_ref[0, h] = dec * hin_ref[0, h] + s_new[hl * SSD_HEADDIM:(hl + 1) * SSD_HEADDIM]
    else:
        yint[...] = jnp.zeros_like(yint)
        sh = int(math.log2(lq))
        rowseq = lax.shift_right_logical(lax.broadcasted_iota(I32, (q, SSD_STATE), 0), sh)
        colseq = lax.shift_right_logical(lax.broadcasted_iota(I32, (gw, q), 1), sh)

        def seq_body(i, carry):
            trow = tot_s[pl.ds(i * lq, 1), :]
            for g in range(SSD_GROUPS):
                hin_g = hin_ref[i, g * HEADS_PER_GROUP:(g + 1) * HEADS_PER_GROUP].reshape(gw, SSD_STATE)
                cmg = cm[:, g * SSD_STATE:(g + 1) * SSD_STATE]
                lhs = jnp.where(rowseq == i, cmg, 0.0).astype(BF)
                yint[:, g * gw:(g + 1) * gw] += _dot_nt(lhs, hin_g.astype(BF))
                xg = xwt[g * gw:(g + 1) * gw, :]
                xi = jnp.where(colseq == i, xg, jnp.zeros_like(xg))
                s_new = _dot(xi, bm_bf[:, g * SSD_STATE:(g + 1) * SSD_STATE])
                for hl in range(HEADS_PER_GROUP):
                    h = g * HEADS_PER_GROUP + hl
                    dec = jnp.exp(jnp.broadcast_to(trow[:, h:h + 1], (SSD_HEADDIM, SSD_STATE)))
                    ho_ref[i, h] = dec * hin_ref[i, h] + s_new[hl * SSD_HEADDIM:(hl + 1) * SSD_HEADDIM]
            return carry

        lax.fori_loop(0, nseq, seq_body, 0, unroll=2)

    y_ref[...] = (yacc[...] + yint[...] * ecs_x + dsk_ref[...] * xs).astype(y_ref.dtype)


def _ssd(xbc, dt, conv_w, conv_b, a_log, dskip, eexp, eye, conv0, h0, *, batch, seqlen):
    q = SSD_CHUNK
    t = xbc.shape[0]
    has_h0 = h0 is not None
    if has_h0:
        lq = seqlen
        nseq = q // lq
        nb = 1
        lead = t // q
        grid = (t // q,)
        rmap = lambda i: (i, 0, 0)
        smap3 = lambda i: (i, 0, 0)
        smap4 = lambda i: (i, 0, 0, 0)
    else:
        lq = q
        nseq = 1
        nb = SSD_SEQS_PER_STEP
        lead = batch
        nc = seqlen // q
        grid = (batch // nb, nc)
        rmap = lambda b, c: (b, c, 0)
        smap3 = lambda b, c: (b, 0, 0)
        smap4 = lambda b, c: (b, 0, 0, 0)
    xbc = xbc.reshape(lead, t // lead, SSD_CONV_DIM)
    dt = dt.reshape(lead, t // lead, LANES)
    shifts = np.zeros((SSD_CONV - 1, q, q), np.float32)
    for k in range(SSD_CONV - 1):
        for r in range(q):
            if r % lq + k - (SSD_CONV - 1) >= 0:
                shifts[k, r, r + k - (SSD_CONV - 1)] = 1.0
    consts = [conv_w, conv_b, a_log, dskip, eexp, eye, jnp.asarray(shifts).astype(BF)]
    in_specs = [pl.BlockSpec((nb, q, SSD_CONV_DIM), rmap), pl.BlockSpec((nb, q, LANES), rmap)]
    in_specs += [_const_spec(c.shape) for c in consts]
    args = [xbc, dt] + consts
    ns = nb * nseq
    if has_h0:
        in_specs += [pl.BlockSpec((ns, SSD_CONV - 1, SSD_CONV_DIM), smap3),
                     pl.BlockSpec((ns, SSD_HEADS, SSD_HEADDIM, SSD_STATE), smap4)]
        args += [conv0, h0]
    y, new_conv, new_state = pl.pallas_call(
        functools.partial(_ssd_kernel, lq=lq, has_h0=has_h0, nb=nb),
        grid=grid,
        in_specs=in_specs,
        out_specs=[pl.BlockSpec((nb, q, D_MODEL), rmap),
                   pl.BlockSpec((ns, SSD_CONV - 1, SSD_CONV_DIM), smap3),
                   pl.BlockSpec((ns, SSD_HEADS, SSD_HEADDIM, SSD_STATE), smap4)],
        out_shape=[jax.ShapeDtypeStruct((lead, t // lead, D_MODEL), BF),
                   jax.ShapeDtypeStruct((batch, SSD_CONV - 1, SSD_CONV_DIM), F32),
                   jax.ShapeDtypeStruct((batch, SSD_HEADS, SSD_HEADDIM, SSD_STATE), F32)],
        scratch_shapes=[pltpu.VMEM((ns, 16, SSD_CONV_DIM), F32),
                        pltpu.VMEM((nb, q, SSD_CONV_DIM), F32),
                        pltpu.VMEM((nb, q, D_MODEL), F32),
                        pltpu.VMEM((nb, q, D_MODEL), F32),
                        pltpu.VMEM((nb, D_MODEL, q), BF),
                        pltpu.VMEM((nb, q, LANES), F32)],
        compiler_params=_params(len(grid)),
        name="ssd_sample" if has_h0 else "ssd_prompt",
    )(*args)
    return y.reshape(t, D_MODEL), new_conv, new_state


S5_PROMPT_SEQS = 8
S5_SAMPLE_SEQS = 32


def _s5_kernel(*refs, tl, has_h0, bt):
    m = bt * tl
    if has_h0:
        (u_ref, perm_ref, permt_ref, bblk_ref, cblk_ref, lre_ref, lim_ref, d_ref, h0re_ref, h0im_ref,
         y_ref, ore_ref, oim_ref, bre, bim, sre, sim, utm, ytm) = refs
    else:
        (u_ref, perm_ref, permt_ref, bblk_ref, cblk_ref, lre_ref, lim_ref, d_ref,
         y_ref, ore_ref, oim_ref, bre, bim, sre, sim, utm, ytm) = refs

    @pl.when(pl.program_id(1) == 0)
    def _():
        if has_h0:
            sre[...] = h0re_ref[...]
            sim[...] = h0im_ref[...]
        else:
            sre[...] = jnp.zeros_like(sre)
            sim[...] = jnp.zeros_like(sim)

    strided = len(u_ref.shape) == 3
    if strided:
        for b in range(bt):
            ub = u_ref[b].astype(F32)
            for s in range(S5_SLABS):
                utm[s, pl.ds(b, tl, stride=bt), :] = ub[:, s * LANES:(s + 1) * LANES]
        slab_in = lambda j: utm[j].astype(BF)
    else:
        u = u_ref[...]
        u_tm = _dot(perm_ref[...], u).astype(BF)
        slab_in = lambda j: u_tm[:, j * LANES:(j + 1) * LANES]
    half = S5_FLAT // S5_SLABS
    ys = []
    for j in range(S5_SLABS):
        sl = slice(j * half, (j + 1) * half)
        r = _dot(slab_in(j), bblk_ref[j])
        bre[:, sl] = r[:, :half]
        bim[:, sl] = r[:, half:]
        lr = lre_ref[:, sl]
        li = lim_ref[:, sl]
        sr = sre[:, sl]
        si = sim[:, sl]
        for l in range(tl):
            rows = slice(l * bt, (l + 1) * bt)
            sr, si = lr * sr - li * si + bre[rows, sl], lr * si + li * sr + bim[rows, sl]
            bre[rows, sl] = sr
            bim[rows, sl] = si
        sre[:, sl] = sr
        sim[:, sl] = si
        st = jnp.concatenate([bre[:, sl], bim[:, sl]], axis=1)
        ys.append(_dot(st.astype(BF), cblk_ref[j]))
    if strided:
        for j in range(S5_SLABS):
            ytm[j] = ys[j]
        for b in range(bt):
            yb = jnp.concatenate([ytm[s, pl.ds(b, tl, stride=bt), :] for s in range(S5_SLABS)], axis=1)
            y_ref[b] = (yb + d_ref[...] * u_ref[b].astype(F32)).astype(y_ref.dtype)
    else:
        y_tm = jnp.concatenate(ys, axis=1).astype(BF)
        y_ref[...] = (_dot(permt_ref[...], y_tm) + d_ref[...] * u.astype(F32)).astype(y_ref.dtype)
    ore_ref[...] = sre[...]
    oim_ref[...] = sim[...]


def _s5(u, perm, permt, bblk, cblk, lre, lim, dvec, h0re, h0im, *, batch, seqlen, tl, bt):
    m = bt * tl
    has_h0 = h0re is not None
    nb = batch // bt
    nt = seqlen // tl
    if u.ndim == 3:
        u_spec = pl.BlockSpec((bt, tl, D_MODEL), lambda b, t: (b, t, 0))
    else:
        u_spec = pl.BlockSpec((m, D_MODEL), lambda b, t: (b * nt + t, 0))
    st_spec = pl.BlockSpec((bt, S5_FLAT), lambda b, t: (b, 0))
    consts = [perm, permt, bblk, cblk, lre, lim, dvec]
    in_specs = [u_spec] + [_const_spec(c.shape) for c in consts]
    args = [u] + consts
    if has_h0:
        in_specs += [st_spec, st_spec]
        args += [h0re, h0im]
    return pl.pallas_call(
        functools.partial(_s5_kernel, tl=tl, has_h0=has_h0, bt=bt),
        grid=(nb, nt),
        in_specs=in_specs,
        out_specs=[u_spec, st_spec, st_spec],
        out_shape=[jax.ShapeDtypeStruct(u.shape, BF),
                   jax.ShapeDtypeStruct((batch, S5_FLAT), F32),
                   jax.ShapeDtypeStruct((batch, S5_FLAT), F32)],
        scratch_shapes=[pltpu.VMEM((m, S5_FLAT), F32), pltpu.VMEM((m, S5_FLAT), F32),
                        pltpu.VMEM((bt, S5_FLAT), F32), pltpu.VMEM((bt, S5_FLAT), F32),
                        pltpu.VMEM((S5_SLABS, m, LANES), F32), pltpu.VMEM((S5_SLABS, m, LANES), F32)],
        compiler_params=_params(2),
        name="s5_sample" if has_h0 else "s5_prompt",
    )(*args)


def _mixout_kernel(y_ref, z_ref, y5_ref, ga_ref, gb_ref, x_ref, ng_ref, wa_ref, wglu_ref, bglu_ref, wmix_ref, o_ref):
    y = y_ref[...].astype(F32) * _silu(z_ref[...].astype(F32))
    yn = _rms(y, ng_ref[...]).astype(BF)
    branch_a = _dot(yn, wa_ref[...])
    glu = _dot(y5_ref[...], wglu_ref[...]) + bglu_ref[...]
    branch_b = glu[:, :D_MODEL] * _sigmoid(glu[:, D_MODEL:])
    merged = _sigmoid(ga_ref[...].astype(F32)) * branch_a + _sigmoid(gb_ref[...].astype(F32)) * branch_b
    o_ref[...] = x_ref[...] + _dot(merged.astype(BF), wmix_ref[...])


def _mixout(y, z, y5, ga, gb, x2d, ng, wa, wglu, bglu, wmix):
    t = x2d.shape[0]
    tm = MIX_TILE
    row = pl.BlockSpec((tm, D_MODEL), lambda i: (i, 0))
    consts = [ng, wa, wglu, bglu, wmix]
    return pl.pallas_call(
        _mixout_kernel,
        grid=(t // tm,),
        in_specs=[row] * 6 + [_resident_spec(c.shape) for c in consts],
        out_specs=row,
        out_shape=jax.ShapeDtypeStruct((t, D_MODEL), F32),
        compiler_params=_params(1),
        name="mixout",
    )(y, z, y5, ga, gb, x2d, *consts)


def _attn_kernel(h_ref, g_ref, wq_ref, wo_ref, k_ref, v_ref, o_ref, obuf, *, nkv, rows_per_seq):
    h = h_ref[...]
    r = h.shape[0]
    q32 = _dot(_rms(h, g_ref[...]).astype(BF), wq_ref[...])
    scale = MEM_HEAD_DIM ** -0.5
    if len(k_ref.shape) == 4:
        rq = rows_per_seq
        nrow = MEM_HEADS * rq
        row_head = lax.shift_right_logical(lax.broadcasted_iota(I32, (nrow, MEM_LEN * MEM_HEADS), 0), int(math.log2(rq)))
        col_head = lax.broadcasted_iota(I32, (nrow, MEM_LEN * MEM_HEADS), 1) & (MEM_HEADS - 1)
        visible = row_head == col_head
        for i in range(nkv):
            qs = jnp.concatenate([q32[i * rq:(i + 1) * rq, hd * MEM_HEAD_DIM:(hd + 1) * MEM_HEAD_DIM]
                                  for hd in range(MEM_HEADS)], axis=0).astype(BF)
            kf = k_ref[i].reshape(MEM_LEN * MEM_HEADS, MEM_HEAD_DIM).astype(BF)
            vf = v_ref[i].reshape(MEM_LEN * MEM_HEADS, MEM_HEAD_DIM).astype(BF)
            s = jnp.where(visible, _dot_nt(qs, kf) * scale, NEG_BIG)
            p = jnp.exp(s - jnp.max(s, axis=-1, keepdims=True))
            p = p / jnp.sum(p, axis=-1, keepdims=True)
            o = _dot(p.astype(BF), vf)
            for hd in range(MEM_HEADS):
                obuf[i * rq:(i + 1) * rq, hd * MEM_HEAD_DIM:(hd + 1) * MEM_HEAD_DIM] = o[hd * rq:(hd + 1) * rq]
    else:
        qv = q32.astype(BF)
        for hd in range(MEM_HEADS):
            sl = slice(hd * MEM_HEAD_DIM, (hd + 1) * MEM_HEAD_DIM)
            s = _dot_nt(qv[:, sl], k_ref[0, :, sl].astype(BF)) * scale
            p = jnp.exp(s - jnp.max(s, axis=-1, keepdims=True))
            p = p / jnp.sum(p, axis=-1, keepdims=True)
            obuf[:, sl] = _dot(p.astype(BF), v_ref[0, :, sl].astype(BF))
    o_ref[...] = h + _dot(obuf[...].astype(BF), wo_ref[...])


def _attn(h2d, g, wq, wo, k3, v3, *, rows, nkv, rows_per_seq):
    t = h2d.shape[0]
    nsteps = t // rows
    steps_per_kv = nsteps // (k3.shape[0] // nkv)
    row = pl.BlockSpec((rows, D_MODEL), lambda i: (i, 0))
    if k3.ndim == 4:
        kv = pl.BlockSpec((nkv, MEM_LEN, MEM_HEADS, MEM_HEAD_DIM), lambda i: (i // steps_per_kv, 0, 0, 0))
    else:
        kv = pl.BlockSpec((nkv, MEM_LEN, D_MODEL), lambda i: (i // steps_per_kv, 0, 0))
    consts = [g, wq, wo]
    return pl.pallas_call(
        functools.partial(_attn_kernel, nkv=nkv, rows_per_seq=rows_per_seq),
        grid=(nsteps,),
        in_specs=[row] + [_const_spec(c.shape) for c in consts] + [kv, kv],
        out_specs=row,
        out_shape=jax.ShapeDtypeStruct((t, D_MODEL), F32),
        scratch_shapes=[pltpu.VMEM((rows, D_MODEL), F32)],
        compiler_params=_params(1),
        name="attn",
    )(h2d, *consts, k3, v3)


ROUTER_LANE0 = MOE_GROUPS

U32 = jnp.uint32
HALF = D_MODEL // 2


def _pack_rows(x):
    xr = x.astype(BF).astype(F32)
    lo = lax.shift_right_logical(pltpu.bitcast(xr[:, :HALF], U32), jnp.uint32(16))
    hi = pltpu.bitcast(xr[:, HALF:], U32) & jnp.uint32(0xFFFF0000)
    return hi | lo


def _unpack_rows(w):
    lo = pltpu.bitcast(lax.shift_left(w, jnp.uint32(16)), F32).astype(BF)
    hi = pltpu.bitcast(w & jnp.uint32(0xFFFF0000), F32).astype(BF)
    return lo, hi


GRANULE = 8
LOCAL_USED = 2 * TOKEN_TILE + MOE_EXPERTS * (GRANULE - 1)
LOCAL_ROWS = 1280
SPARE_GRANULES = (LOCAL_ROWS - LOCAL_USED) // GRANULE
TILE_GRANULES = MOE_ROW_TILE // GRANULE


def _route_kernel(hp_ref, hs_ref, g_ref, wcat_ref, wh_ref, b_ref, tri_ref, eye_ref, upper_ref,
                  xl_ref, idx_ref, wt_ref, seg_ref, *, np_steps):
    i = pl.program_id(0)
    h = jnp.where(i < np_steps, hp_ref[...], hs_ref[...])
    xn = _rms(h, g_ref[...])
    tm = xn.shape[0]
    xh = xn.astype(BF)
    xm = (xn - xh.astype(F32)).astype(BF)
    both = _dot(xh, wcat_ref[...])
    logits = both[:, :LANES] + both[:, LANES:] + _dot(xm, wh_ref[...]) + b_ref[...]
    lane = lax.broadcasted_iota(I32, logits.shape, 1)
    lane_f = lane.astype(F32)
    no_lane = float(LANES)
    is_g = lane < MOE_GROUPS
    lg = jnp.where(is_g, logits, NEG_BIG)
    mg = jnp.max(lg, axis=-1, keepdims=True)
    gidx = jnp.min(jnp.where(lg == mg, lane_f, no_lane), axis=-1, keepdims=True).astype(I32)
    g_w = 1.0 / jnp.sum(jnp.where(is_g, jnp.exp(lg - mg), 0.0), axis=-1, keepdims=True)
    e_lane = lane - ROUTER_LANE0
    in_grp = ((e_lane >= 0) & (e_lane < MOE_EXPERTS)
              & (lax.shift_right_logical(jnp.maximum(e_lane, 0), 3) == gidx))
    le = jnp.where(in_grp, logits, NEG_BIG)
    m1 = jnp.max(le, axis=-1, keepdims=True)
    i1 = jnp.min(jnp.where(le == m1, lane_f, no_lane), axis=-1, keepdims=True)
    le2 = jnp.where(lane_f == i1, NEG_BIG, le)
    m2 = jnp.max(le2, axis=-1, keepdims=True)
    i2 = jnp.min(jnp.where(le2 == m2, lane_f, no_lane), axis=-1, keepdims=True)
    ratio = jnp.exp(m2 - m1)
    w1 = g_w / (1.0 + ratio)
    w2 = g_w * ratio / (1.0 + ratio)
    wt_ref[...] = jnp.where(lane == 0, w1, jnp.where(lane == 1, w2, 0.0))

    oh1 = jnp.where(lane_f == i1, 1.0, 0.0)
    oh2 = jnp.where(lane_f == i2, 1.0, 0.0)
    tot1 = jnp.sum(oh1, axis=0, keepdims=True)
    cnt = (tot1 + jnp.sum(oh2, axis=0, keepdims=True)).astype(I32)
    padded = lax.shift_left(lax.shift_right_logical(cnt + (GRANULE - 1), 3), 3)
    pad8 = jnp.broadcast_to(padded.astype(F32), (8, LANES)).astype(BF)
    lstart = _dot(pad8, upper_ref[...])[0:1, :]
    tri = tri_ref[...]
    l1 = jnp.sum(oh1 * (_dot(tri, oh1.astype(BF)) + lstart), axis=-1, keepdims=True)
    l2 = jnp.sum(oh2 * (_dot(tri, oh2.astype(BF)) + (lstart + tot1)), axis=-1, keepdims=True)
    idx_ref[...] = jnp.where(lane == 0, l1.astype(I32), jnp.where(lane == 1, l2.astype(I32), 0))
    sub = lax.broadcasted_iota(I32, (8, LANES), 0)
    seg_ref[0] = jnp.where(sub == 0, jnp.broadcast_to(cnt, (8, LANES)),
                           jnp.where(sub == 1, jnp.broadcast_to(lstart.astype(I32), (8, LANES)), 0))

    cols = jnp.where(lane == 0, l1, jnp.where(lane == 1, l2, 0.0))
    rows = _dot_exact(eye_ref[...], cols, nt=True)
    r_id = lax.broadcasted_iota(I32, (LOCAL_ROWS, tm), 0).astype(F32)
    perm = jnp.where(r_id == rows[0:1, :], 1.0, 0.0) + jnp.where(r_id == rows[1:2, :], 1.0, 0.0)
    xl_ref[...] = _pack_rows(_dot(perm.astype(BF), xh))


def _route(hp, hs, g, wcat, wh, b):
    tm = TOKEN_TILE
    np_steps = hp.shape[0] // tm
    ns_steps = hs.shape[0] // tm
    nt = np_steps + ns_steps
    tri = jnp.asarray(np.tril(np.ones((tm, tm), np.float32), -1)).astype(BF)
    eye8 = jnp.asarray(np.eye(8, LANES, dtype=np.float32)).astype(BF)
    upper = jnp.asarray(np.triu(np.ones((LANES, LANES), np.float32), 1)).astype(BF)
    consts = [g, wcat, wh, b, tri, eye8, upper]
    return pl.pallas_call(
        functools.partial(_route_kernel, np_steps=np_steps),
        grid=(nt,),
        in_specs=[pl.BlockSpec((tm, D_MODEL), lambda i: (jnp.minimum(i, np_steps - 1), 0)),
                  pl.BlockSpec((tm, D_MODEL), lambda i: (jnp.maximum(i - np_steps, 0), 0))]
                 + [_const_spec(c.shape) for c in consts],
        out_specs=[pl.BlockSpec((LOCAL_ROWS, HALF), lambda i: (i, 0)),
                   pl.BlockSpec((tm, LANES), lambda i: (i, 0)),
                   pl.BlockSpec((tm, LANES), lambda i: (i, 0)),
                   pl.BlockSpec((1, 8, LANES), lambda i: (i, 0, 0))],
        out_shape=[jax.ShapeDtypeStruct((nt * LOCAL_ROWS, HALF), U32),
                   jax.ShapeDtypeStruct((nt * tm, LANES), I32),
                   jax.ShapeDtypeStruct((nt * tm, LANES), F32),
                   jax.ShapeDtypeStruct((nt, 8, LANES), I32)],
        compiler_params=_params(1),
        name="route",
    )(hp, hs, *consts)


def _granule_plan(seg, n_tiles):
    cnt = seg[:, 0, ROUTER_LANE0:ROUTER_LANE0 + MOE_EXPERTS]
    lstart = seg[:, 1, ROUTER_LANE0:ROUTER_LANE0 + MOE_EXPERTS]
    ntt = cnt.shape[0]
    ng = (cnt + GRANULE - 1) // GRANULE
    cum = jnp.cumsum(ng, axis=0)
    first = (jnp.arange(ntt, dtype=I32)[:, None] * LOCAL_ROWS + lstart) // GRANULE - (cum - ng)
    total = cum[-1]
    tiles_e = (total + TILE_GRANULES - 1) // TILE_GRANULES
    tend = jnp.cumsum(tiles_e)
    tstart = tend - tiles_e
    tile = jnp.arange(n_tiles, dtype=I32)
    te = jnp.minimum(jnp.sum((tile[:, None] >= tend[None, :]).astype(I32), axis=1), MOE_EXPERTS - 1)
    n_used = tend[-1].astype(I32).reshape(1)
    sel = (te[:, None] == jnp.arange(MOE_EXPERTS, dtype=I32)[None, :]).astype(I32)
    pick = lambda table: jnp.sum(sel[:, None, :] * table[None, :, :], axis=2)
    cum_t, first_t = pick(cum), pick(first)
    off = (tile - jnp.sum(sel * tstart[None, :], axis=1))[:, None] * TILE_GRANULES + jnp.arange(TILE_GRANULES, dtype=I32)[None, :]
    valid = (off < jnp.sum(sel * total[None, :], axis=1)[:, None]) & (tile < n_used[0])[:, None]
    j = jnp.sum((cum_t[:, None, :] <= off[:, :, None]).astype(I32), axis=2)
    jsel = (j[:, :, None] == jnp.arange(ntt, dtype=I32)[None, None, :]).astype(I32)
    src = jnp.sum(jsel * first_t[:, None, :], axis=2) + off
    spare = lambda k: ((k // SPARE_GRANULES) * LOCAL_ROWS + LOCAL_USED) // GRANULE + k % SPARE_GRANULES
    assert (ntt - 1) * SPARE_GRANULES >= 2 * TILE_GRANULES
    zero_granule = spare((ntt - 1) * SPARE_GRANULES)
    trash = spare((tile % 2)[:, None] * TILE_GRANULES + jnp.arange(TILE_GRANULES, dtype=I32)[None, :])
    gsrc = jnp.where(valid, src, zero_granule).astype(I32).reshape(-1)
    gdst = jnp.where(valid, src, trash).astype(I32).reshape(-1)
    n_valid = jnp.sum(valid.astype(I32), axis=1).astype(I32)
    return te.astype(I32), n_used, gsrc, gdst, n_valid


def _grouped_kernel(te_ref, nu_ref, gsrc_ref, gdst_ref, nv_ref, xin_ref, wg_ref, wu_ref, wd_ref, xout_ref,
                    xbuf, ybuf, wg16, wu16, wd16, sem_in, sem_out):
    i = pl.program_id(0)
    nu = nu_ref[0]
    tm = MOE_ROW_TILE

    def hbm_rows(table_ref, tile, g):
        return pl.ds(pl.multiple_of(table_ref[tile * TILE_GRANULES + g] * GRANULE, GRANULE), GRANULE)

    def fetch_start(tile, slot):
        for g in range(TILE_GRANULES):
            pltpu.make_async_copy(xin_ref.at[hbm_rows(gsrc_ref, tile, g)],
                                  xbuf.at[slot, pl.ds(g * GRANULE, GRANULE)], sem_in.at[slot]).start()

    def store_start(tile, slot):
        for g in range(TILE_GRANULES):
            pltpu.make_async_copy(ybuf.at[slot, pl.ds(g * GRANULE, GRANULE)],
                                  xout_ref.at[hbm_rows(gdst_ref, tile, g)], sem_out.at[slot]).start()

    def fetch_wait(slot):
        pltpu.make_async_copy(xin_ref.at[pl.ds(0, tm)], xbuf.at[slot], sem_in.at[slot]).wait()

    def store_wait(slot):
        pltpu.make_async_copy(ybuf.at[slot], xout_ref.at[pl.ds(0, tm)], sem_out.at[slot]).wait()

    def step(slot):
        @pl.when(i == 0)
        def _():
            ybuf[...] = jnp.zeros_like(ybuf)
            fetch_start(0, 0)

        @pl.when(i + 1 < nu)
        def _():
            fetch_start(i + 1, 1 - slot)

        @pl.when(i >= 2)
        def _():
            store_wait(slot)

        @pl.when((i == 0) | (te_ref[i] != te_ref[jnp.maximum(i - 1, 0)]))
        def _():
            wg16[...] = wg_ref[0].astype(BF)
            wu16[...] = wu_ref[0].astype(BF)
            wd16[...] = wd_ref[0].astype(BF)

        fetch_wait(slot)
        n_valid = nv_ref[i]
        quarter = TILE_GRANULES // 4
        for part in range(1, 5):
            rows = part * quarter * GRANULE

            @pl.when((n_valid > (part - 1) * quarter) & (n_valid <= part * quarter))
            def _(rows=rows):
                x_lo, x_hi = _unpack_rows(xbuf[slot, 0:rows, :])
                gate = _dot(x_lo, wg16[0:HALF, :]) + _dot(x_hi, wg16[HALF:, :])
                up = _dot(x_lo, wu16[0:HALF, :]) + _dot(x_hi, wu16[HALF:, :])
                ybuf[slot, 0:rows, :] = _pack_rows(_dot((_silu(gate) * up).astype(BF), wd16[...]))

        store_start(i, slot)

        @pl.when(i == nu - 1)
        def _():
            store_wait(slot)

            @pl.when(i >= 1)
            def _():
                store_wait(1 - slot)

    for slot in (0, 1):
        @pl.when((i < nu) & (lax.rem(i, 2) == slot))
        def _(slot=slot):
            step(slot)


def _grouped(x_local, te, n_used, gsrc, gdst, n_valid, wg, wu, wd):
    tm = MOE_ROW_TILE
    n_tiles = te.shape[0]
    used = lambda i, nu: jnp.minimum(i, jnp.maximum(nu[0] - 1, 0))
    wspec = lambda shape: pl.BlockSpec((1,) + shape, lambda i, te, nu, *_: (te[used(i, nu)], 0, 0))
    grid_spec = pltpu.PrefetchScalarGridSpec(
        num_scalar_prefetch=5,
        grid=(n_tiles,),
        in_specs=[pl.BlockSpec(memory_space=pl.ANY), wspec((D_MODEL, MOE_D_FF)), wspec((D_MODEL, MOE_D_FF)),
                  wspec((MOE_D_FF, D_MODEL))],
        out_specs=pl.BlockSpec(memory_space=pl.ANY),
        scratch_shapes=[pltpu.VMEM((2, tm, HALF), U32), pltpu.VMEM((2, tm, HALF), U32),
                        pltpu.VMEM((D_MODEL, MOE_D_FF), BF), pltpu.VMEM((D_MODEL, MOE_D_FF), BF),
                        pltpu.VMEM((MOE_D_FF, D_MODEL), BF),
                        pltpu.SemaphoreType.DMA((2,)), pltpu.SemaphoreType.DMA((2,))],
    )
    return pl.pallas_call(
        _grouped_kernel,
        grid_spec=grid_spec,
        out_shape=jax.ShapeDtypeStruct(x_local.shape, x_local.dtype),
        input_output_aliases={5: 0},
        compiler_params=_params(1),
        name="grouped",
    )(te, n_used, gsrc, gdst, n_valid, x_local, wg, wu, wd)


def _merge_kernel(h_ref, idx_ref, wt_ref, g_ref, yl_ref, o_ref):
    tm = h_ref.shape[0]
    idx = idx_ref[...]
    wt = wt_ref[...]
    r_id = lax.broadcasted_iota(I32, (tm, LOCAL_ROWS), 1)
    sel = jnp.where(r_id == idx[:, 0:1], wt[:, 0:1], 0.0) + jnp.where(r_id == idx[:, 1:2], wt[:, 1:2], 0.0)
    y_lo, y_hi = _unpack_rows(yl_ref[...])
    sel16 = sel.astype(BF)
    moe = jnp.concatenate([_dot(sel16, y_lo), _dot(sel16, y_hi)], axis=1)
    o_ref[...] = _rms(h_ref[...] + moe, g_ref[...])


def _merge(h2d, idx, wt, g, y_local, *, tile_offset):
    t = h2d.shape[0]
    tm = TOKEN_TILE
    return pl.pallas_call(
        _merge_kernel,
        grid=(t // tm,),
        in_specs=[pl.BlockSpec((tm, D_MODEL), lambda i: (i, 0)),
                  pl.BlockSpec((tm, LANES), lambda i: (i + tile_offset, 0)),
                  pl.BlockSpec((tm, LANES), lambda i: (i + tile_offset, 0)),
                  _const_spec(g.shape),
                  pl.BlockSpec((LOCAL_ROWS, HALF), lambda i: (i + tile_offset, 0))],
        out_specs=pl.BlockSpec((tm, D_MODEL), lambda i: (i, 0)),
        out_shape=jax.ShapeDtypeStruct((t, D_MODEL), F32),
        compiler_params=_params(1),
        name="merge",
    )(h2d, idx, wt, g, y_local)


def _perm_matrix(bt, tl):
    m = bt * tl
    p = np.zeros((m, m), np.float32)
    for l in range(tl):
        for b in range(bt):
            p[l * bt + b, b * tl + l] = 1.0
    return p


def _s5_operators(lam_re, lam_im, log_step, b_re, b_im, c_re, c_im):
    lam = lax.complex(lam_re, lam_im)
    step = jnp.exp(log_step)[:, None]
    lam_bar = jnp.exp(lam * step)
    b_bar = ((lam_bar - 1.0) / lam)[..., None] * lax.complex(b_re, b_im)
    gl = LANES // S5_GROUP
    eye = jnp.eye(gl, dtype=F32)

    def in_block(bpart):
        bp = bpart.reshape(S5_SLABS, gl, S5_STATE, S5_GROUP)
        blk = jnp.einsum("sgnk,gh->sgkhn", bp, eye)
        return blk.reshape(S5_SLABS, LANES, gl * S5_STATE)

    def out_block(cpart):
        cp = cpart.reshape(S5_SLABS, gl, S5_GROUP, S5_STATE)
        blk = jnp.einsum("sgkn,gh->sgnhk", cp, eye)
        return blk.reshape(S5_SLABS, gl * S5_STATE, LANES)

    bblk = jnp.concatenate([in_block(jnp.real(b_bar)), in_block(jnp.imag(b_bar))], axis=2).astype(BF)
    cblk = jnp.concatenate([out_block(c_re), out_block(-c_im)], axis=1).astype(BF)
    return bblk, cblk, jnp.real(lam_bar).reshape(1, S5_FLAT), jnp.imag(lam_bar).reshape(1, S5_FLAT)


def _pad_lanes(v, width=LANES):
    return jnp.pad(v, [(0, 0)] * (v.ndim - 1) + [(0, width - v.shape[-1])])


def kernel(x_prompt, x_sample, state_conv, state_ssd, state_s5_re, state_s5_im, cache_mem_k, cache_mem_v, mem_prompt, norm_mix_g, w_in, conv_w, conv_b, ssd_dt_bias, ssd_a_log, ssd_d, ssd_norm_g, w_ssd_branch, s5_lambda_re, s5_lambda_im, s5_log_step, s5_b_re, s5_b_im, s5_c_re, s5_c_im, s5_d, w_glu, b_glu, w_mix_out, norm_mem_q_g, norm_mem_kv_g, w_mem_q, w_mem_k, w_mem_v, w_mem_o, norm_ffn_g, w_router_group, b_router_group, w_router_expert, b_router_expert, w_exp_gate, w_exp_up, w_exp_down, norm_final_g):
    bp, lp, _ = x_prompt.shape
    bs, ls, _ = x_sample.shape
    tp, ts = bp * lp, bs * ls
    row1 = lambda v: v.reshape(1, -1).astype(F32)

    w_in0 = w_in[0]
    o_xbc = D_MODEL
    o_dt = o_xbc + SSD_CONV_DIM
    o_u5 = o_dt + SSD_HEADS
    w_a = w_in0[:, :o_dt].astype(BF)
    w_b = w_in0[:, o_u5:].astype(BF)
    w_dt = _pad_lanes(w_in0[:, o_dt:o_u5]).astype(BF)
    dt_bias = _pad_lanes(row1(ssd_dt_bias[0]))
    a_log = _pad_lanes(row1(ssd_a_log[0]))
    dskip = jnp.repeat(ssd_d[0].astype(F32), SSD_HEADDIM).reshape(1, D_MODEL)
    eexp_np = np.zeros((LANES, D_MODEL), np.float32)
    for h in range(SSD_HEADS):
        eexp_np[h, h * SSD_HEADDIM:(h + 1) * SSD_HEADDIM] = 1.0
    eexp = jnp.asarray(eexp_np).astype(BF)
    eye16 = jnp.asarray(np.eye(SSD_HEADS, LANES, dtype=np.float32)).astype(BF)
    bblk, cblk, lre, lim = _s5_operators(s5_lambda_re[0], s5_lambda_im[0], s5_log_step[0], s5_b_re[0],
                                         s5_b_im[0], s5_c_re[0], s5_c_im[0])
    s5d = row1(s5_d[0])
    w_kv = jnp.concatenate([w_mem_k[0], w_mem_v[0]], axis=1).astype(BF)
    w_r = _pad_lanes(jnp.concatenate([w_router_group[0], w_router_expert[0]], axis=1).astype(F32))
    wr_h, wr_m, _ = _split3(w_r)
    b_r = _pad_lanes(row1(jnp.concatenate([b_router_group[0], b_router_expert[0]])))
    wg = w_exp_gate[0].reshape(MOE_EXPERTS, D_MODEL, MOE_D_FF)
    wu = w_exp_up[0].reshape(MOE_EXPERTS, D_MODEL, MOE_D_FF)
    wd = w_exp_down[0].reshape(MOE_EXPERTS, MOE_D_FF, D_MODEL)

    def mixer(x2d, batch, seqlen, conv0, h0, s5re0, s5im0):
        z, xbc, u5, ga, gb, dt = _inproj(x2d, row1(norm_mix_g[0]), w_a, w_b, w_dt, dt_bias)
        y_ssd, new_conv, new_ssd = _ssd(xbc, dt, conv_w[0].astype(F32), row1(conv_b[0]), a_log, dskip, eexp, eye16,
                                        conv0, h0, batch=batch, seqlen=seqlen)
        if h0 is None:
            tl, bt = S5_TIME_TILE, S5_PROMPT_SEQS
            u_in = u5.reshape(batch, seqlen, D_MODEL)
        else:
            tl, bt = seqlen, S5_SAMPLE_SEQS
            u_in = u5
        perm = _perm_matrix(bt, tl)
        lam_rows = [jnp.broadcast_to(v, (bt, S5_FLAT)) for v in (lre, lim)]
        y5, new_re, new_im = _s5(u_in, jnp.asarray(perm).astype(BF), jnp.asarray(perm.T).astype(BF), bblk, cblk,
                                 *lam_rows, s5d, s5re0, s5im0, batch=batch, seqlen=seqlen, tl=tl, bt=bt)
        h1 = _mixout(y_ssd, z, y5.reshape(-1, D_MODEL), ga, gb, x2d, row1(ssd_norm_g[0]),
                     w_ssd_branch[0].astype(BF), w_glu[0].astype(BF), row1(b_glu[0]), w_mix_out[0].astype(BF))
        return h1, new_conv, new_ssd, new_re, new_im

    xp2 = x_prompt.reshape(tp, D_MODEL)
    xs2 = x_sample.reshape(ts, D_MODEL)
    h1p, conv_p, ssd_p, re_p, im_p = mixer(xp2, bp, lp, None, None, None, None)
    h1s, conv_s, ssd_s, re_s, im_s = mixer(xs2, bs, ls, state_conv[0], state_ssd[0],
                                           state_s5_re[0].reshape(bs, S5_FLAT), state_s5_im[0].reshape(bs, S5_FLAT))

    mk_p, mv_p = _memkv(mem_prompt.reshape(bp * MEM_LEN, D_MODEL), row1(norm_mem_kv_g[0]), w_kv)
    gq = row1(norm_mem_q_g[0])
    wq = w_mem_q[0].astype(BF)
    wo = w_mem_o[0].astype(BF)
    h2p = _attn(h1p, gq, wq, wo, mk_p.reshape(bp, MEM_LEN, D_MODEL), mv_p.reshape(bp, MEM_LEN, D_MODEL),
                rows=MIX_TILE, nkv=1, rows_per_seq=lp)
    sample_seqs = 8
    h2s = _attn(h1s, gq, wq, wo, cache_mem_k[0], cache_mem_v[0], rows=sample_seqs * ls, nkv=sample_seqs,
                rows_per_seq=ls)

    g_ffn = row1(norm_ffn_g[0])
    x_local, r_idx, r_wt, seg = _route(h2p, h2s, g_ffn, jnp.concatenate([wr_h, wr_m], axis=1), wr_h, b_r)
    token_tiles = (tp + ts) // TOKEN_TILE
    max_granules = (2 * (tp + ts) + token_tiles * MOE_EXPERTS * (GRANULE - 1)) // GRANULE
    n_tiles = max_granules // TILE_GRANULES + MOE_EXPERTS
    te, n_used, gsrc, gdst, n_valid = _granule_plan(seg, n_tiles)
    y_local = _grouped(x_local, te, n_used, gsrc, gdst, n_valid, wg, wu, wd)
    gf = row1(norm_final_g)
    y_prompt = _merge(h2p, r_idx, r_wt, gf, y_local, tile_offset=0)
    y_sample = _merge(h2s, r_idx, r_wt, gf, y_local, tile_offset=tp // TOKEN_TILE)

    return (y_prompt.reshape(bp, lp, D_MODEL), y_sample.reshape(bs, ls, D_MODEL),
            conv_p[None], ssd_p[None],
            re_p.reshape(1, bp, S5_GROUPS, S5_STATE), im_p.reshape(1, bp, S5_GROUPS, S5_STATE),
            mk_p.reshape(1, bp, MEM_LEN, MEM_HEADS, MEM_HEAD_DIM), mv_p.reshape(1, bp, MEM_LEN, MEM_HEADS, MEM_HEAD_DIM),
            conv_s[None], ssd_s[None],
            re_s.reshape(1, bs, S5_GROUPS, S5_STATE), im_s.reshape(1, bs, S5_GROUPS, S5_STATE))
```

```python
import functools
import math

import numpy as np
import jax
import jax.numpy as jnp
from jax import lax
from jax.experimental import pallas as pl
from jax.experimental.pallas import tpu as pltpu

F32 = jnp.float32
BF = jnp.bfloat16
I32 = jnp.int32

D_MODEL = 1024
SSD_HEADS = 16
SSD_HEADDIM = 64
SSD_GROUPS = 2
SSD_STATE = 128
SSD_CONV = 4
SSD_CONV_DIM = 1536
HEADS_PER_GROUP = SSD_HEADS // SSD_GROUPS
S5_GROUPS = 64
S5_GROUP = 16
S5_STATE = 64
S5_FLAT = S5_GROUPS * S5_STATE
S5_SLABS = D_MODEL // 128
MEM_LEN = 256
MEM_HEADS = 4
MEM_HEAD_DIM = 256
MOE_GROUPS = 4
MOE_EXPERTS_PER_GROUP = 8
MOE_EXPERTS = MOE_GROUPS * MOE_EXPERTS_PER_GROUP
MOE_D_FF = 512
NORM_EPS = 1e-6

LANES = 128
VMEM_LIMIT_BYTES = 56 * 1024 * 1024
TOKEN_TILE = 512
MIX_TILE = 1024
SSD_CHUNK = 128
SSD_SEQS_PER_STEP = 4
S5_TIME_TILE = 64
MOE_ROW_TILE = 512
NEG_BIG = -1e30


def _params(n_axes):
    return pltpu.CompilerParams(dimension_semantics=("arbitrary",) * n_axes,
                                vmem_limit_bytes=VMEM_LIMIT_BYTES)


def _const_spec(shape):
    nd = len(shape)
    return pl.BlockSpec(shape, lambda *_: (0,) * nd)


def _resident_spec(shape):
    nd = len(shape)
    return pl.BlockSpec(shape, lambda *_: (0,) * nd, pipeline_mode=pl.Buffered(1))


def _dot(a, b):
    return jnp.dot(a, b, preferred_element_type=F32)


def _dot_nt(a, b):
    return lax.dot_general(a, b, (((1,), (1,)), ((), ())), preferred_element_type=F32)


def _split3(v):
    h = v.astype(BF)
    r = v - h.astype(F32)
    m = r.astype(BF)
    l = (r - m.astype(F32)).astype(BF)
    return h, m, l


def _dot_exact(a_bf, v, nt=False):
    f = _dot_nt if nt else _dot
    h, m, l = _split3(v)
    return f(a_bf, h) + f(a_bf, m) + f(a_bf, l)


def _rms(x, g):
    return x * lax.rsqrt(jnp.mean(x * x, axis=-1, keepdims=True) + NORM_EPS) * g


def _sigmoid(x):
    return 0.5 * jnp.tanh(0.5 * x) + 0.5


def _silu(x):
    return x * _sigmoid(x)


def _proj_cols(xb, w_ref, o_ref, c0, width):
    for c in range(0, width, 512):
        ce = min(c + 512, width)
        o_ref[:, c:ce] = _dot(xb, w_ref[:, c0 + c:c0 + ce]).astype(o_ref.dtype)


def _inproj_kernel(x_ref, g_ref, wa_ref, wb_ref, wdt_ref, dtb_ref, z_ref, xbc_ref, u5_ref, ga_ref, gb_ref, dt_ref):
    xb = _rms(x_ref[...], g_ref[...]).astype(BF)
    for w_ref, outs in ((wa_ref, (z_ref, xbc_ref)), (wb_ref, (u5_ref, ga_ref, gb_ref))):
        c0 = 0
        for o_ref in outs:
            width = o_ref.shape[1]
            _proj_cols(xb, w_ref, o_ref, c0, width)
            c0 += width
    raw = _dot(xb, wdt_ref[...]) + dtb_ref[...]
    dt_ref[...] = jnp.maximum(raw, 0.0) + jnp.log1p(jnp.exp(-jnp.abs(raw)))


def _inproj(x2d, g, w_a, w_b, w_dt, dt_bias):
    t = x2d.shape[0]
    tm = MIX_TILE
    widths = (D_MODEL, SSD_CONV_DIM, D_MODEL, D_MODEL, D_MODEL)
    row = lambda w: pl.BlockSpec((tm, w), lambda i: (i, 0))
    return pl.pallas_call(
        _inproj_kernel,
        grid=(t // tm,),
        in_specs=[row(D_MODEL), _const_spec(g.shape), _resident_spec(w_a.shape), _resident_spec(w_b.shape),
                  _resident_spec(w_dt.shape), _const_spec(dt_bias.shape)],
        out_specs=[row(w) for w in widths] + [row(LANES)],
        out_shape=[jax.ShapeDtypeStruct((t, w), BF) for w in widths] + [jax.ShapeDtypeStruct((t, LANES), F32)],
        compiler_params=_params(1),
        name="inproj",
    )(x2d, g, w_a, w_b, w_dt, dt_bias)


def _memkv_kernel(x_ref, g_ref, w_ref, k_ref, v_ref, k4_ref, v4_ref):
    xb = _rms(x_ref[...], g_ref[...]).astype(BF)
    _proj_cols(xb, w_ref, k_ref, 0, D_MODEL)
    _proj_cols(xb, w_ref, v_ref, D_MODEL, D_MODEL)
    for flat_ref, split_ref in ((k_ref, k4_ref), (v_ref, v4_ref)):
        for s in range(split_ref.shape[0]):
            for hd in range(MEM_HEADS):
                split_ref[s, :, hd, :] = flat_ref[s * MEM_LEN:(s + 1) * MEM_LEN, hd * MEM_HEAD_DIM:(hd + 1) * MEM_HEAD_DIM]


def _memkv(mem2d, g, w_kv):
    t = mem2d.shape[0]
    tm = TOKEN_TILE
    seqs = tm // MEM_LEN
    row = pl.BlockSpec((tm, D_MODEL), lambda i: (i, 0))
    split = pl.BlockSpec((seqs, MEM_LEN, MEM_HEADS, MEM_HEAD_DIM), lambda i: (i, 0, 0, 0))
    split_shape = jax.ShapeDtypeStruct((t // MEM_LEN, MEM_LEN, MEM_HEADS, MEM_HEAD_DIM), F32)
    return pl.pallas_call(
        _memkv_kernel,
        grid=(t // tm,),
        in_specs=[row, _const_spec(g.shape), _const_spec(w_kv.shape)],
        out_specs=[row, row, split, split],
        out_shape=[jax.ShapeDtypeStruct((t, D_MODEL), F32)] * 2 + [split_shape] * 2,
        compiler_params=_params(1),
        name="memkv",
    )(mem2d, g, w_kv)


def _ssd_kernel(*refs, lq, has_h0, nb):
    nseq = SSD_CHUNK // lq
    n_in = 11 if has_h0 else 9
    xbc_ref, dt_ref = refs[0], refs[1]
    consts = refs[2:9]
    state_in = refs[9:n_in]
    y_ref, convo_ref, ho_ref = refs[n_in:n_in + 3]
    hbuf, cacc, yacc, yint, xwt, tot_s = refs[n_in + 3:]
    for s in range(nb):
        seqs = pl.ds(s * nseq, nseq)
        _ssd_chunk(xbc_ref.at[s], dt_ref.at[s], *consts, *[r.at[seqs] for r in state_in],
                   y_ref.at[s], convo_ref.at[seqs], ho_ref.at[seqs], hbuf.at[seqs],
                   cacc.at[s], yacc.at[s], yint.at[s], xwt.at[s], tot_s.at[s], lq=lq, has_h0=has_h0)


def _ssd_chunk(*refs, lq, has_h0):
    q = SSD_CHUNK
    nseq = q // lq
    if has_h0:
        (xbc_ref, dt_ref, cw_ref, cb_ref, alog_ref, dsk_ref, eexp_ref, eye_ref, shift_ref, conv0_ref, h0_ref,
         y_ref, convo_ref, ho_ref, hbuf, cacc, yacc, yint, xwt, tot_s) = refs
        hin_ref = h0_ref
        hbuf[...] = jnp.zeros_like(hbuf)
        hbuf[:, 0:SSD_CONV - 1, :] = conv0_ref[...]
    else:
        (xbc_ref, dt_ref, cw_ref, cb_ref, alog_ref, dsk_ref, eexp_ref, eye_ref, shift_ref,
         y_ref, convo_ref, ho_ref, hbuf, cacc, yacc, yint, xwt, tot_s) = refs
        hin_ref = ho_ref

        @pl.when(pl.program_id(1) == 0)
        def _():
            ho_ref[...] = jnp.zeros_like(ho_ref)
            hbuf[...] = jnp.zeros_like(hbuf)

    x16 = xbc_ref[...]
    xraw = x16.astype(F32)
    acc = cb_ref[...] + xraw * cw_ref[SSD_CONV - 1:SSD_CONV, :]
    for k in range(SSD_CONV - 1):
        acc = acc + _dot(shift_ref[k], x16) * cw_ref[k:k + 1, :]
    cacc[...] = acc
    for i in range(nseq):
        tail = xraw[(i + 1) * lq - (SSD_CONV - 1):(i + 1) * lq]
        convo_ref[i] = tail
        corr = hbuf[i, 0:8, :] * cw_ref[0:1, :]
        for k in range(1, SSD_CONV - 1):
            corr = corr + hbuf[i, k:k + 8, :] * cw_ref[k:k + 1, :]
        cacc[i * lq:i * lq + 8, :] += corr
        if not has_h0:
            hbuf[i, 0:SSD_CONV - 1, :] = tail
    xc = _silu(cacc[...])
    xs = xc[:, :D_MODEL]
    xs_bf = xs.astype(BF)
    bm_bf = xc[:, D_MODEL:D_MODEL + SSD_GROUPS * SSD_STATE].astype(BF)
    cm = xc[:, D_MODEL + SSD_GROUPS * SSD_STATE:]
    cm_bf = cm.astype(BF)

    dt = dt_ref[...]
    da = dt * (-jnp.exp(alog_ref[...]))
    ri = lax.broadcasted_iota(I32, (q, q), 0)
    ci = lax.broadcasted_iota(I32, (q, q), 1)
    if nseq == 1:
        causal = ci <= ri
    else:
        sh = int(math.log2(lq))
        same = lax.shift_right_logical(ri, sh) == lax.shift_right_logical(ci, sh)
        causal = same & (ci <= ri)
    lmat = jnp.where(causal, 1.0, 0.0).astype(BF)
    cs = _dot_exact(lmat, da)
    if nseq == 1:
        tot = jnp.broadcast_to(cs[q - 1:q, :], (q, LANES))
    else:
        tot = _dot_exact(jnp.where(same, 1.0, 0.0).astype(BF), da)
    tot_s[...] = tot
    eye = eye_ref[...]
    cs_t = _dot_exact(eye, cs, nt=True)
    dt_t = _dot_exact(eye, dt, nt=True)
    eexp = eexp_ref[...]
    ecs_h, ecs_m, _ = _split3(jnp.exp(cs))
    ecs_x = _dot(ecs_h, eexp) + _dot(ecs_m, eexp)
    wend_x = _dot((jnp.exp(tot - cs) * dt).astype(BF), eexp)

    lane = lax.broadcasted_iota(I32, (q, LANES), 1)
    for g in range(SSD_GROUPS):
        cbg = _dot_nt(cm_bf[:, g * SSD_STATE:(g + 1) * SSD_STATE], bm_bf[:, g * SSD_STATE:(g + 1) * SSD_STATE])
        for jp in range(HEADS_PER_GROUP // 2):
            j = g * (HEADS_PER_GROUP // 2) + jp
            ms = []
            for h in (2 * j, 2 * j + 1):
                col = jnp.broadcast_to(cs[:, h:h + 1], (q, q))
                row = jnp.broadcast_to(cs_t[h:h + 1, :], (q, q))
                dtr = jnp.broadcast_to(dt_t[h:h + 1, :], (q, q))
                ms.append((jnp.where(causal, jnp.exp(col - row), 0.0) * cbg * dtr).astype(BF))
            mp = jnp.concatenate(ms, axis=1)
            xp = xs_bf[:, j * LANES:(j + 1) * LANES]
            zero = jnp.zeros_like(xp)
            x2 = jnp.concatenate([jnp.where(lane < SSD_HEADDIM, xp, zero),
                                  jnp.where(lane >= SSD_HEADDIM, xp, zero)], axis=0)
            yacc[:, j * LANES:(j + 1) * LANES] = _dot(mp, x2)

    gw = HEADS_PER_GROUP * SSD_HEADDIM
    xw = xs * wend_x
    for g in range(SSD_GROUPS):
        xwt[g * gw:(g + 1) * gw, :] = xw[:, g * gw:(g + 1) * gw].T.astype(BF)

    if nseq == 1:
        for g in range(SSD_GROUPS):
            hin_g = hin_ref[0, g * HEADS_PER_GROUP:(g + 1) * HEADS_PER_GROUP].reshape(gw, SSD_STATE)
            yint[:, g * gw:(g + 1) * gw] = _dot_nt(cm_bf[:, g * SSD_STATE:(g + 1) * SSD_STATE], hin_g.astype(BF))
            s_new = _dot(xwt[g * gw:(g + 1) * gw, :], bm_bf[:, g * SSD_STATE:(g + 1) * SSD_STATE])
            for hl in range(HEADS_PER_GROUP):
                h = g * HEADS_PER_GROUP + hl
                dec = jnp.exp(jnp.broadcast_to(tot[0:1, h:h + 1], (SSD_HEADDIM, SSD_STATE)))
                ho_ref[0, h] = dec * hin_ref[0, h] + s_new[hl * SSD_HEADDIM:(hl + 1) * SSD_HEADDIM]
    else:
        yint[...] = jnp.zeros_like(yint)
        sh = int(math.log2(lq))
        rowseq = lax.shift_right_logical(lax.broadcasted_iota(I32, (q, SSD_STATE), 0), sh)
        colseq = lax.shift_right_logical(lax.broadcasted_iota(I32, (gw, q), 1), sh)

        def seq_body(i, carry):
            trow = tot_s[pl.ds(i * lq, 1), :]
            for g in range(SSD_GROUPS):
                hin_g = hin_ref[i, g * HEADS_PER_GROUP:(g + 1) * HEADS_PER_GROUP].reshape(gw, SSD_STATE)
                cmg = cm[:, g * SSD_STATE:(g + 1) * SSD_STATE]
                lhs = jnp.where(rowseq == i, cmg, 0.0).astype(BF)
                yint[:, g * gw:(g + 1) * gw] += _dot_nt(lhs, hin_g.astype(BF))
                xg = xwt[g * gw:(g + 1) * gw, :]
                xi = jnp.where(colseq == i, xg, jnp.zeros_like(xg))
                s_new = _dot(xi, bm_bf[:, g * SSD_STATE:(g + 1) * SSD_STATE])
                for hl in range(HEADS_PER_GROUP):
                    h = g * HEADS_PER_GROUP + hl
                    dec = jnp.exp(jnp.broadcast_to(trow[:, h:h + 1], (SSD_HEADDIM, SSD_STATE)))
                    ho_ref[i, h] = dec * hin_ref[i, h] + s_new[hl * SSD_HEADDIM:(hl + 1) * SSD_HEADDIM]
            return carry

        lax.fori_loop(0, nseq, seq_body, 0, unroll=2)

    y_ref[...] = (yacc[...] + yint[...] * ecs_x + dsk_ref[...] * xs).astype(y_ref.dtype)


def _ssd(xbc, dt, conv_w, conv_b, a_log, dskip, eexp, eye, conv0, h0, *, batch, seqlen):
    q = SSD_CHUNK
    t = xbc.shape[0]
    has_h0 = h0 is not None
    if has_h0:
        lq = seqlen
        nseq = q // lq
        nb = 1
        lead = t // q
        grid = (t // q,)
        rmap = lambda i: (i, 0, 0)
        smap3 = lambda i: (i, 0, 0)
        smap4 = lambda i: (i, 0, 0, 0)
    else:
        lq = q
        nseq = 1
        nb = SSD_SEQS_PER_STEP
        lead = batch
        nc = seqlen // q
        grid = (batch // nb, nc)
        rmap = lambda b, c: (b, c, 0)
        smap3 = lambda b, c: (b, 0, 0)
        smap4 = lambda b, c: (b, 0, 0, 0)
    xbc = xbc.reshape(lead, t // lead, SSD_CONV_DIM)
    dt = dt.reshape(lead, t // lead, LANES)
    shifts = np.zeros((SSD_CONV - 1, q, q), np.float32)
    for k in range(SSD_CONV - 1):
        for r in range(q):
            if r % lq + k - (SSD_CONV - 1) >= 0:
                shifts[k, r, r + k - (SSD_CONV - 1)] = 1.0
    consts = [conv_w, conv_b, a_log, dskip, eexp, eye, jnp.asarray(shifts).astype(BF)]
    in_specs = [pl.BlockSpec((nb, q, SSD_CONV_DIM), rmap), pl.BlockSpec((nb, q, LANES), rmap)]
    in_specs += [_const_spec(c.shape) for c in consts]
    args = [xbc, dt] + consts
    ns = nb * nseq
    if has_h0:
        in_specs += [pl.BlockSpec((ns, SSD_CONV - 1, SSD_CONV_DIM), smap3),
                     pl.BlockSpec((ns, SSD_HEADS, SSD_HEADDIM, SSD_STATE), smap4)]
        args += [conv0, h0]
    y, new_conv, new_state = pl.pallas_call(
        functools.partial(_ssd_kernel, lq=lq, has_h0=has_h0, nb=nb),
        grid=grid,
        in_specs=in_specs,
        out_specs=[pl.BlockSpec((nb, q, D_MODEL), rmap),
                   pl.BlockSpec((ns, SSD_CONV - 1, SSD_CONV_DIM), smap3),
                   pl.BlockSpec((ns, SSD_HEADS, SSD_HEADDIM, SSD_STATE), smap4)],
        out_shape=[jax.ShapeDtypeStruct((lead, t // lead, D_MODEL), BF),
                   jax.ShapeDtypeStruct((batch, SSD_CONV - 1, SSD_CONV_DIM), F32),
                   jax.ShapeDtypeStruct((batch, SSD_HEADS, SSD_HEADDIM, SSD_STATE), F32)],
        scratch_shapes=[pltpu.VMEM((ns, 16, SSD_CONV_DIM), F32),
                        pltpu.VMEM((nb, q, SSD_CONV_DIM), F32),
                        pltpu.VMEM((nb, q, D_MODEL), F32),
                        pltpu.VMEM((nb, q, D_MODEL), F32),
                        pltpu.VMEM((nb, D_MODEL, q), BF),
                        pltpu.VMEM((nb, q, LANES), F32)],
        compiler_params=_params(len(grid)),
        name="ssd_sample" if has_h0 else "ssd_prompt",
    )(*args)
    return y.reshape(t, D_MODEL), new_conv, new_state


S5_PROMPT_SEQS = 8
S5_SAMPLE_SEQS = 32


def _s5_kernel(*refs, tl, has_h0, bt):
    m = bt * tl
    if has_h0:
        (u_ref, perm_ref, permt_ref, bblk_ref, cblk_ref, lre_ref, lim_ref, d_ref, h0re_ref, h0im_ref,
         y_ref, ore_ref, oim_ref, bre, bim, sre, sim, utm, ytm) = refs
    else:
        (u_ref, perm_ref, permt_ref, bblk_ref, cblk_ref, lre_ref, lim_ref, d_ref,
         y_ref, ore_ref, oim_ref, bre, bim, sre, sim, utm, ytm) = refs

    @pl.when(pl.program_id(1) == 0)
    def _():
        if has_h0:
            sre[...] = h0re_ref[...]
            sim[...] = h0im_ref[...]
        else:
            sre[...] = jnp.zeros_like(sre)
            sim[...] = jnp.zeros_like(sim)

    strided = len(u_ref.shape) == 3
    if strided:
        for b in range(bt):
            ub = u_ref[b].astype(F32)
            for s in range(S5_SLABS):
                utm[s, pl.ds(b, tl, stride=bt), :] = ub[:, s * LANES:(s + 1) * LANES]
        slab_in = lambda j: utm[j].astype(BF)
    else:
        u = u_ref[...]
        u_tm = _dot(perm_ref[...], u).astype(BF)
        slab_in = lambda j: u_tm[:, j * LANES:(j + 1) * LANES]
    half = S5_FLAT // S5_SLABS
    ys = []
    for j in range(S5_SLABS):
        sl = slice(j * half, (j + 1) * half)
        r = _dot(slab_in(j), bblk_ref[j])
        bre[:, sl] = r[:, :half]
        bim[:, sl] = r[:, half:]
        lr = lre_ref[:, sl]
        li = lim_ref[:, sl]
        sr = sre[:, sl]
        si = sim[:, sl]
        for l in range(tl):
            rows = slice(l * bt, (l + 1) * bt)
            sr, si = lr * sr - li * si + bre[rows, sl], lr * si + li * sr + bim[rows, sl]
            bre[rows, sl] = sr
            bim[rows, sl] = si
        sre[:, sl] = sr
        sim[:, sl] = si
        st = jnp.concatenate([bre[:, sl], bim[:, sl]], axis=1)
        ys.append(_dot(st.astype(BF), cblk_ref[j]))
    if strided:
        for j in range(S5_SLABS):
            ytm[j] = ys[j]
        for b in range(bt):
            yb = jnp.concatenate([ytm[s, pl.ds(b, tl, stride=bt), :] for s in range(S5_SLABS)], axis=1)
            y_ref[b] = (yb + d_ref[...] * u_ref[b].astype(F32)).astype(y_ref.dtype)
    else:
        y_tm = jnp.concatenate(ys, axis=1).astype(BF)
        y_ref[...] = (_dot(permt_ref[...], y_tm) + d_ref[...] * u.astype(F32)).astype(y_ref.dtype)
    ore_ref[...] = sre[...]
    oim_ref[...] = sim[...]


def _s5(u, perm, permt, bblk, cblk, lre, lim, dvec, h0re, h0im, *, batch, seqlen, tl, bt):
    m = bt * tl
    has_h0 = h0re is not None
    nb = batch // bt
    nt = seqlen // tl
    if u.ndim == 3:
        u_spec = pl.BlockSpec((bt, tl, D_MODEL), lambda b, t: (b, t, 0))
    else:
        u_spec = pl.BlockSpec((m, D_MODEL), lambda b, t: (b * nt + t, 0))
    st_spec = pl.BlockSpec((bt, S5_FLAT), lambda b, t: (b, 0))
    consts = [perm, permt, bblk, cblk, lre, lim, dvec]
    in_specs = [u_spec] + [_const_spec(c.shape) for c in consts]
    args = [u] + consts
    if has_h0:
        in_specs += [st_spec, st_spec]
        args += [h0re, h0im]
    return pl.pallas_call(
        functools.partial(_s5_kernel, tl=tl, has_h0=has_h0, bt=bt),
        grid=(nb, nt),
        in_specs=in_specs,
        out_specs=[u_spec, st_spec, st_spec],
        out_shape=[jax.ShapeDtypeStruct(u.shape, BF),
                   jax.ShapeDtypeStruct((batch, S5_FLAT), F32),
                   jax.ShapeDtypeStruct((batch, S5_FLAT), F32)],
        scratch_shapes=[pltpu.VMEM((m, S5_FLAT), F32), pltpu.VMEM((m, S5_FLAT), F32),
                        pltpu.VMEM((bt, S5_FLAT), F32), pltpu.VMEM((bt, S5_FLAT), F32),
                        pltpu.VMEM((S5_SLABS, m, LANES), F32), pltpu.VMEM((S5_SLABS, m, LANES), F32)],
        compiler_params=_params(2),
        name="s5_sample" if has_h0 else "s5_prompt",
    )(*args)


def _mixout_kernel(y_ref, z_ref, y5_ref, ga_ref, gb_ref, x_ref, ng_ref, wa_ref, wglu_ref, bglu_ref, wmix_ref, o_ref):
    y = y_ref[...].astype(F32) * _silu(z_ref[...].astype(F32))
    yn = _rms(y, ng_ref[...]).astype(BF)
    branch_a = _dot(yn, wa_ref[...])
    glu = _dot(y5_ref[...], wglu_ref[...]) + bglu_ref[...]
    branch_b = glu[:, :D_MODEL] * _sigmoid(glu[:, D_MODEL:])
    merged = _sigmoid(ga_ref[...].astype(F32)) * branch_a + _sigmoid(gb_ref[...].astype(F32)) * branch_b
    o_ref[...] = x_ref[...] + _dot(merged.astype(BF), wmix_ref[...])


def _mixout(y, z, y5, ga, gb, x2d, ng, wa, wglu, bglu, wmix):
    t = x2d.shape[0]
    tm = MIX_TILE
    row = pl.BlockSpec((tm, D_MODEL), lambda i: (i, 0))
    consts = [ng, wa, wglu, bglu, wmix]
    return pl.pallas_call(
        _mixout_kernel,
        grid=(t // tm,),
        in_specs=[row] * 6 + [_resident_spec(c.shape) for c in consts],
        out_specs=row,
        out_shape=jax.ShapeDtypeStruct((t, D_MODEL), F32),
        compiler_params=_params(1),
        name="mixout",
    )(y, z, y5, ga, gb, x2d, *consts)


def _attn_kernel(h_ref, g_ref, wq_ref, wo_ref, k_ref, v_ref, o_ref, obuf, *, nkv, rows_per_seq):
    h = h_ref[...]
    r = h.shape[0]
    q32 = _dot(_rms(h, g_ref[...]).astype(BF), wq_ref[...])
    scale = MEM_HEAD_DIM ** -0.5
    if len(k_ref.shape) == 4:
        rq = rows_per_seq
        nrow = MEM_HEADS * rq
        row_head = lax.shift_right_logical(lax.broadcasted_iota(I32, (nrow, MEM_LEN * MEM_HEADS), 0), int(math.log2(rq)))
        col_head = lax.broadcasted_iota(I32, (nrow, MEM_LEN * MEM_HEADS), 1) & (MEM_HEADS - 1)
        visible = row_head == col_head
        for i in range(nkv):
            qs = jnp.concatenate([q32[i * rq:(i + 1) * rq, hd * MEM_HEAD_DIM:(hd + 1) * MEM_HEAD_DIM]
                                  for hd in range(MEM_HEADS)], axis=0).astype(BF)
            kf = k_ref[i].reshape(MEM_LEN * MEM_HEADS, MEM_HEAD_DIM).astype(BF)
            vf = v_ref[i].reshape(MEM_LEN * MEM_HEADS, MEM_HEAD_DIM).astype(BF)
            s = jnp.where(visible, _dot_nt(qs, kf) * scale, NEG_BIG)
            p = jnp.exp(s - jnp.max(s, axis=-1, keepdims=True))
            p = p / jnp.sum(p, axis=-1, keepdims=True)
            o = _dot(p.astype(BF), vf)
            for hd in range(MEM_HEADS):
                obuf[i * rq:(i + 1) * rq, hd * MEM_HEAD_DIM:(hd + 1) * MEM_HEAD_DIM] = o[hd * rq:(hd + 1) * rq]
    else:
        qv = q32.astype(BF)
        for hd in range(MEM_HEADS):
            sl = slice(hd * MEM_HEAD_DIM, (hd + 1) * MEM_HEAD_DIM)
            s = _dot_nt(qv[:, sl], k_ref[0, :, sl].astype(BF)) * scale
            p = jnp.exp(s - jnp.max(s, axis=-1, keepdims=True))
            p = p / jnp.sum(p, axis=-1, keepdims=True)
            obuf[:, sl] = _dot(p.astype(BF), v_ref[0, :, sl].astype(BF))
    o_ref[...] = h + _dot(obuf[...].astype(BF), wo_ref[...])


def _attn(h2d, g, wq, wo, k3, v3, *, rows, nkv, rows_per_seq):
    t = h2d.shape[0]
    nsteps = t // rows
    steps_per_kv = nsteps // (k3.shape[0] // nkv)
    row = pl.BlockSpec((rows, D_MODEL), lambda i: (i, 0))
    if k3.ndim == 4:
        kv = pl.BlockSpec((nkv, MEM_LEN, MEM_HEADS, MEM_HEAD_DIM), lambda i: (i // steps_per_kv, 0, 0, 0))
    else:
        kv = pl.BlockSpec((nkv, MEM_LEN, D_MODEL), lambda i: (i // steps_per_kv, 0, 0))
    consts = [g, wq, wo]
    return pl.pallas_call(
        functools.partial(_attn_kernel, nkv=nkv, rows_per_seq=rows_per_seq),
        grid=(nsteps,),
        in_specs=[row] + [_const_spec(c.shape) for c in consts] + [kv, kv],
        out_specs=row,
        out_shape=jax.ShapeDtypeStruct((t, D_MODEL), F32),
        scratch_shapes=[pltpu.VMEM((rows, D_MODEL), F32)],
        compiler_params=_params(1),
        name="attn",
    )(h2d, *consts, k3, v3)


ROUTER_LANE0 = MOE_GROUPS

U32 = jnp.uint32
HALF = D_MODEL // 2


def _pack_rows(x):
    xr = x.astype(BF).astype(F32)
    lo = lax.shift_right_logical(pltpu.bitcast(xr[:, :HALF], U32), jnp.uint32(16))
    hi = pltpu.bitcast(xr[:, HALF:], U32) & jnp.uint32(0xFFFF0000)
    return hi | lo


def _unpack_rows(w):
    lo = pltpu.bitcast(lax.shift_left(w, jnp.uint32(16)), F32).astype(BF)
    hi = pltpu.bitcast(w & jnp.uint32(0xFFFF0000), F32).astype(BF)
    return lo, hi


GRANULE = 8
LOCAL_USED = 2 * TOKEN_TILE + MOE_EXPERTS * (GRANULE - 1)
LOCAL_ROWS = 1280
SPARE_GRANULES = (LOCAL_ROWS - LOCAL_USED) // GRANULE
TILE_GRANULES = MOE_ROW_TILE // GRANULE


def _route_kernel(hp_ref, hs_ref, g_ref, wcat_ref, wh_ref, b_ref, tri_ref, eye_ref, upper_ref,
                  xl_ref, idx_ref, wt_ref, seg_ref, *, np_steps):
    i = pl.program_id(0)
    h = jnp.where(i < np_steps, hp_ref[...], hs_ref[...])
    xn = _rms(h, g_ref[...])
    tm = xn.shape[0]
    xh = xn.astype(BF)
    xm = (xn - xh.astype(F32)).astype(BF)
    both = _dot(xh, wcat_ref[...])
    logits = both[:, :LANES] + both[:, LANES:] + _dot(xm, wh_ref[...]) + b_ref[...]
    lane = lax.broadcasted_iota(I32, logits.shape, 1)
    lane_f = lane.astype(F32)
    no_lane = float(LANES)
    is_g = lane < MOE_GROUPS
    lg = jnp.where(is_g, logits, NEG_BIG)
    mg = jnp.max(lg, axis=-1, keepdims=True)
    gidx = jnp.min(jnp.where(lg == mg, lane_f, no_lane), axis=-1, keepdims=True).astype(I32)
    g_w = 1.0 / jnp.sum(jnp.where(is_g, jnp.exp(lg - mg), 0.0), axis=-1, keepdims=True)
    e_lane = lane - ROUTER_LANE0
    in_grp = ((e_lane >= 0) & (e_lane < MOE_EXPERTS)
              & (lax.shift_right_logical(jnp.maximum(e_lane, 0), 3) == gidx))
    le = jnp.where(in_grp, logits, NEG_BIG)
    m1 = jnp.max(le, axis=-1, keepdims=True)
    i1 = jnp.min(jnp.where(le == m1, lane_f, no_lane), axis=-1, keepdims=True)
    le2 = jnp.where(lane_f == i1, NEG_BIG, le)
    m2 = jnp.max(le2, axis=-1, keepdims=True)
    i2 = jnp.min(jnp.where(le2 == m2, lane_f, no_lane), axis=-1, keepdims=True)
    ratio = jnp.exp(m2 - m1)
    w1 = g_w / (1.0 + ratio)
    w2 = g_w * ratio / (1.0 + ratio)
    wt_ref[...] = jnp.where(lane == 0, w1, jnp.where(lane == 1, w2, 0.0))

    oh1 = jnp.where(lane_f == i1, 1.0, 0.0)
    oh2 = jnp.where(lane_f == i2, 1.0, 0.0)
    tot1 = jnp.sum(oh1, axis=0, keepdims=True)
    cnt = (tot1 + jnp.sum(oh2, axis=0, keepdims=True)).astype(I32)
    padded = lax.shift_left(lax.shift_right_logical(cnt + (GRANULE - 1), 3), 3)
    pad8 = jnp.broadcast_to(padded.astype(F32), (8, LANES)).astype(BF)
    lstart = _dot(pad8, upper_ref[...])[0:1, :]
    tri = tri_ref[...]
    l1 = jnp.sum(oh1 * (_dot(tri, oh1.astype(BF)) + lstart), axis=-1, keepdims=True)
    l2 = jnp.sum(oh2 * (_dot(tri, oh2.astype(BF)) + (lstart + tot1)), axis=-1, keepdims=True)
    idx_ref[...] = jnp.where(lane == 0, l1.astype(I32), jnp.where(lane == 1, l2.astype(I32), 0))
    sub = lax.broadcasted_iota(I32, (8, LANES), 0)
    seg_ref[0] = jnp.where(sub == 0, jnp.broadcast_to(cnt, (8, LANES)),
                           jnp.where(sub == 1, jnp.broadcast_to(lstart.astype(I32), (8, LANES)), 0))

    cols = jnp.where(lane == 0, l1, jnp.where(lane == 1, l2, 0.0))
    rows = _dot_exact(eye_ref[...], cols, nt=True)
    r_id = lax.broadcasted_iota(I32, (LOCAL_ROWS, tm), 0).astype(F32)
    perm = jnp.where(r_id == rows[0:1, :], 1.0, 0.0) + jnp.where(r_id == rows[1:2, :], 1.0, 0.0)
    xl_ref[...] = _pack_rows(_dot(perm.astype(BF), xh))


def _route(hp, hs, g, wcat, wh, b):
    tm = TOKEN_TILE
    np_steps = hp.shape[0] // tm
    ns_steps = hs.shape[0] // tm
    nt = np_steps + ns_steps
    tri = jnp.asarray(np.tril(np.ones((tm, tm), np.float32), -1)).astype(BF)
    eye8 = jnp.asarray(np.eye(8, LANES, dtype=np.float32)).astype(BF)
    upper = jnp.asarray(np.triu(np.ones((LANES, LANES), np.float32), 1)).astype(BF)
    consts = [g, wcat, wh, b, tri, eye8, upper]
    return pl.pallas_call(
        functools.partial(_route_kernel, np_steps=np_steps),
        grid=(nt,),
        in_specs=[pl.BlockSpec((tm, D_MODEL), lambda i: (jnp.minimum(i, np_steps - 1), 0)),
                  pl.BlockSpec((tm, D_MODEL), lambda i: (jnp.maximum(i - np_steps, 0), 0))]
                 + [_const_spec(c.shape) for c in consts],
        out_specs=[pl.BlockSpec((LOCAL_ROWS, HALF), lambda i: (i, 0)),
                   pl.BlockSpec((tm, LANES), lambda i: (i, 0)),
                   pl.BlockSpec((tm, LANES), lambda i: (i, 0)),
                   pl.BlockSpec((1, 8, LANES), lambda i: (i, 0, 0))],
        out_shape=[jax.ShapeDtypeStruct((nt * LOCAL_ROWS, HALF), U32),
                   jax.ShapeDtypeStruct((nt * tm, LANES), I32),
                   jax.ShapeDtypeStruct((nt * tm, LANES), F32),
                   jax.ShapeDtypeStruct((nt, 8, LANES), I32)],
        compiler_params=_params(1),
        name="route",
    )(hp, hs, *consts)


def _granule_plan(seg, n_tiles):
    cnt = seg[:, 0, ROUTER_LANE0:ROUTER_LANE0 + MOE_EXPERTS]
    lstart = seg[:, 1, ROUTER_LANE0:ROUTER_LANE0 + MOE_EXPERTS]
    ntt = cnt.shape[0]
    ng = (cnt + GRANULE - 1) // GRANULE
    cum = jnp.cumsum(ng, axis=0)
    first = (jnp.arange(ntt, dtype=I32)[:, None] * LOCAL_ROWS + lstart) // GRANULE - (cum - ng)
    total = cum[-1]
    tiles_e = (total + TILE_GRANULES - 1) // TILE_GRANULES
    tend = jnp.cumsum(tiles_e)
    tstart = tend - tiles_e
    tile = jnp.arange(n_tiles, dtype=I32)
    te = jnp.minimum(jnp.sum((tile[:, None] >= tend[None, :]).astype(I32), axis=1), MOE_EXPERTS - 1)
    n_used = tend[-1].astype(I32).reshape(1)
    sel = (te[:, None] == jnp.arange(MOE_EXPERTS, dtype=I32)[None, :]).astype(I32)
    pick = lambda table: jnp.sum(sel[:, None, :] * table[None, :, :], axis=2)
    cum_t, first_t = pick(cum), pick(first)
    off = (tile - jnp.sum(sel * tstart[None, :], axis=1))[:, None] * TILE_GRANULES + jnp.arange(TILE_GRANULES, dtype=I32)[None, :]
    valid = (off < jnp.sum(sel * total[None, :], axis=1)[:, None]) & (tile < n_used[0])[:, None]
    j = jnp.sum((cum_t[:, None, :] <= off[:, :, None]).astype(I32), axis=2)
    jsel = (j[:, :, None] == jnp.arange(ntt, dtype=I32)[None, None, :]).astype(I32)
    src = jnp.sum(jsel * first_t[:, None, :], axis=2) + off
    spare = lambda k: ((k // SPARE_GRANULES) * LOCAL_ROWS + LOCAL_USED) // GRANULE + k % SPARE_GRANULES
    assert (ntt - 1) * SPARE_GRANULES >= 2 * TILE_GRANULES
    zero_granule = spare((ntt - 1) * SPARE_GRANULES)
    trash = spare((tile % 2)[:, None] * TILE_GRANULES + jnp.arange(TILE_GRANULES, dtype=I32)[None, :])
    gsrc = jnp.where(valid, src, zero_granule).astype(I32).reshape(-1)
    gdst = jnp.where(valid, src, trash).astype(I32).reshape(-1)
    n_valid = jnp.sum(valid.astype(I32), axis=1).astype(I32)
    return te.astype(I32), n_used, gsrc, gdst, n_valid


def _grouped_kernel(te_ref, nu_ref, gsrc_ref, gdst_ref, nv_ref, xin_ref, wg_ref, wu_ref, wd_ref, xout_ref,
                    xbuf, ybuf, wg16, wu16, wd16, sem_in, sem_out):
    i = pl.program_id(0)
    nu = nu_ref[0]
    tm = MOE_ROW_TILE

    def hbm_rows(table_ref, tile, g):
        return pl.ds(pl.multiple_of(table_ref[tile * TILE_GRANULES + g] * GRANULE, GRANULE), GRANULE)

    def fetch_start(tile, slot):
        for g in range(TILE_GRANULES):
            pltpu.make_async_copy(xin_ref.at[hbm_rows(gsrc_ref, tile, g)],
                                  xbuf.at[slot, pl.ds(g * GRANULE, GRANULE)], sem_in.at[slot]).start()

    def store_start(tile, slot):
        for g in range(TILE_GRANULES):
            pltpu.make_async_copy(ybuf.at[slot, pl.ds(g * GRANULE, GRANULE)],
                                  xout_ref.at[hbm_rows(gdst_ref, tile, g)], sem_out.at[slot]).start()

    def fetch_wait(slot):
        pltpu.make_async_copy(xin_ref.at[pl.ds(0, tm)], xbuf.at[slot], sem_in.at[slot]).wait()

    def store_wait(slot):
        pltpu.make_async_copy(ybuf.at[slot], xout_ref.at[pl.ds(0, tm)], sem_out.at[slot]).wait()

    def step(slot):
        @pl.when(i == 0)
        def _():
            ybuf[...] = jnp.zeros_like(ybuf)
            fetch_start(0, 0)

        @pl.when(i + 1 < nu)
        def _():
            fetch_start(i + 1, 1 - slot)

        @pl.when(i >= 2)
        def _():
            store_wait(slot)

        @pl.when((i == 0) | (te_ref[i] != te_ref[jnp.maximum(i - 1, 0)]))
        def _():
            wg16[...] = wg_ref[0].astype(BF)
            wu16[...] = wu_ref[0].astype(BF)
            wd16[...] = wd_ref[0].astype(BF)

        fetch_wait(slot)
        n_valid = nv_ref[i]
        quarter = TILE_GRANULES // 4
        for part in range(1, 5):
            rows = part * quarter * GRANULE

            @pl.when((n_valid > (part - 1) * quarter) & (n_valid <= part * quarter))
            def _(rows=rows):
                x_lo, x_hi = _unpack_rows(xbuf[slot, 0:rows, :])
                gate = _dot(x_lo, wg16[0:HALF, :]) + _dot(x_hi, wg16[HALF:, :])
                up = _dot(x_lo, wu16[0:HALF, :]) + _dot(x_hi, wu16[HALF:, :])
                ybuf[slot, 0:rows, :] = _pack_rows(_dot((_silu(gate) * up).astype(BF), wd16[...]))

        store_start(i, slot)

        @pl.when(i == nu - 1)
        def _():
            store_wait(slot)

            @pl.when(i >= 1)
            def _():
                store_wait(1 - slot)

    for slot in (0, 1):
        @pl.when((i < nu) & (lax.rem(i, 2) == slot))
        def _(slot=slot):
            step(slot)


def _grouped(x_local, te, n_used, gsrc, gdst, n_valid, wg, wu, wd):
    tm = MOE_ROW_TILE
    n_tiles = te.shape[0]
    used = lambda i, nu: jnp.minimum(i, jnp.maximum(nu[0] - 1, 0))
    wspec = lambda shape: pl.BlockSpec((1,) + shape, lambda i, te, nu, *_: (te[used(i, nu)], 0, 0))
    grid_spec = pltpu.PrefetchScalarGridSpec(
        num_scalar_prefetch=5,
        grid=(n_tiles,),
        in_specs=[pl.BlockSpec(memory_space=pl.ANY), wspec((D_MODEL, MOE_D_FF)), wspec((D_MODEL, MOE_D_FF)),
                  wspec((MOE_D_FF, D_MODEL))],
        out_specs=pl.BlockSpec(memory_space=pl.ANY),
        scratch_shapes=[pltpu.VMEM((2, tm, HALF), U32), pltpu.VMEM((2, tm, HALF), U32),
                        pltpu.VMEM((D_MODEL, MOE_D_FF), BF), pltpu.VMEM((D_MODEL, MOE_D_FF), BF),
                        pltpu.VMEM((MOE_D_FF, D_MODEL), BF),
                        pltpu.SemaphoreType.DMA((2,)), pltpu.SemaphoreType.DMA((2,))],
    )
    return pl.pallas_call(
        _grouped_kernel,
        grid_spec=grid_spec,
        out_shape=jax.ShapeDtypeStruct(x_local.shape, x_local.dtype),
        input_output_aliases={5: 0},
        compiler_params=_params(1),
        name="grouped",
    )(te, n_used, gsrc, gdst, n_valid, x_local, wg, wu, wd)


def _merge_kernel(h_ref, idx_ref, wt_ref, g_ref, yl_ref, o_ref):
    tm = h_ref.shape[0]
    idx = idx_ref[...]
    wt = wt_ref[...]
    r_id = lax.broadcasted_iota(I32, (tm, LOCAL_ROWS), 1)
    sel = jnp.where(r_id == idx[:, 0:1], wt[:, 0:1], 0.0) + jnp.where(r_id == idx[:, 1:2], wt[:, 1:2], 0.0)
    y_lo, y_hi = _unpack_rows(yl_ref[...])
    sel16 = sel.astype(BF)
    moe = jnp.concatenate([_dot(sel16, y_lo), _dot(sel16, y_hi)], axis=1)
    o_ref[...] = _rms(h_ref[...] + moe, g_ref[...])


def _merge(h2d, idx, wt, g, y_local, *, tile_offset):
    t = h2d.shape[0]
    tm = TOKEN_TILE
    return pl.pallas_call(
        _merge_kernel,
        grid=(t // tm,),
        in_specs=[pl.BlockSpec((tm, D_MODEL), lambda i: (i, 0)),
                  pl.BlockSpec((tm, LANES), lambda i: (i + tile_offset, 0)),
                  pl.BlockSpec((tm, LANES), lambda i: (i + tile_offset, 0)),
                  _const_spec(g.shape),
                  pl.BlockSpec((LOCAL_ROWS, HALF), lambda i: (i + tile_offset, 0))],
        out_specs=pl.BlockSpec((tm, D_MODEL), lambda i: (i, 0)),
        out_shape=jax.ShapeDtypeStruct((t, D_MODEL), F32),
        compiler_params=_params(1),
        name="merge",
    )(h2d, idx, wt, g, y_local)


def _perm_matrix(bt, tl):
    m = bt * tl
    p = np.zeros((m, m), np.float32)
    for l in range(tl):
        for b in range(bt):
            p[l * bt + b, b * tl + l] = 1.0
    return p


def _s5_operators(lam_re, lam_im, log_step, b_re, b_im, c_re, c_im):
    lam = lax.complex(lam_re, lam_im)
    step = jnp.exp(log_step)[:, None]
    lam_bar = jnp.exp(lam * step)
    b_bar = ((lam_bar - 1.0) / lam)[..., None] * lax.complex(b_re, b_im)
    gl = LANES // S5_GROUP
    eye = jnp.eye(gl, dtype=F32)

    def in_block(bpart):
        bp = bpart.reshape(S5_SLABS, gl, S5_STATE, S5_GROUP)
        blk = jnp.einsum("sgnk,gh->sgkhn", bp, eye)
        return blk.reshape(S5_SLABS, LANES, gl * S5_STATE)

    def out_block(cpart):
        cp = cpart.reshape(S5_SLABS, gl, S5_GROUP, S5_STATE)
        blk = jnp.einsum("sgkn,gh->sgnhk", cp, eye)
        return blk.reshape(S5_SLABS, gl * S5_STATE, LANES)

    bblk = jnp.concatenate([in_block(jnp.real(b_bar)), in_block(jnp.imag(b_bar))], axis=2).astype(BF)
    cblk = jnp.concatenate([out_block(c_re), out_block(-c_im)], axis=1).astype(BF)
    return bblk, cblk, jnp.real(lam_bar).reshape(1, S5_FLAT), jnp.imag(lam_bar).reshape(1, S5_FLAT)


def _pad_lanes(v, width=LANES):
    return jnp.pad(v, [(0, 0)] * (v.ndim - 1) + [(0, width - v.shape[-1])])


def kernel(x_prompt, x_sample, state_conv, state_ssd, state_s5_re, state_s5_im, cache_mem_k, cache_mem_v, mem_prompt, norm_mix_g, w_in, conv_w, conv_b, ssd_dt_bias, ssd_a_log, ssd_d, ssd_norm_g, w_ssd_branch, s5_lambda_re, s5_lambda_im, s5_log_step, s5_b_re, s5_b_im, s5_c_re, s5_c_im, s5_d, w_glu, b_glu, w_mix_out, norm_mem_q_g, norm_mem_kv_g, w_mem_q, w_mem_k, w_mem_v, w_mem_o, norm_ffn_g, w_router_group, b_router_group, w_router_expert, b_router_expert, w_exp_gate, w_exp_up, w_exp_down, norm_final_g):
    bp, lp, _ = x_prompt.shape
    bs, ls, _ = x_sample.shape
    tp, ts = bp * lp, bs * ls
    row1 = lambda v: v.reshape(1, -1).astype(F32)

    w_in0 = w_in[0]
    o_xbc = D_MODEL
    o_dt = o_xbc + SSD_CONV_DIM
    o_u5 = o_dt + SSD_HEADS
    w_a = w_in0[:, :o_dt].astype(BF)
    w_b = w_in0[:, o_u5:].astype(BF)
    w_dt = _pad_lanes(w_in0[:, o_dt:o_u5]).astype(BF)
    dt_bias = _pad_lanes(row1(ssd_dt_bias[0]))
    a_log = _pad_lanes(row1(ssd_a_log[0]))
    dskip = jnp.repeat(ssd_d[0].astype(F32), SSD_HEADDIM).reshape(1, D_MODEL)
    eexp_np = np.zeros((LANES, D_MODEL), np.float32)
    for h in range(SSD_HEADS):
        eexp_np[h, h * SSD_HEADDIM:(h + 1) * SSD_HEADDIM] = 1.0
    eexp = jnp.asarray(eexp_np).astype(BF)
    eye16 = jnp.asarray(np.eye(SSD_HEADS, LANES, dtype=np.float32)).astype(BF)
    bblk, cblk, lre, lim = _s5_operators(s5_lambda_re[0], s5_lambda_im[0], s5_log_step[0], s5_b_re[0],
                                         s5_b_im[0], s5_c_re[0], s5_c_im[0])
    s5d = row1(s5_d[0])
    w_kv = jnp.concatenate([w_mem_k[0], w_mem_v[0]], axis=1).astype(BF)
    w_r = _pad_lanes(jnp.concatenate([w_router_group[0], w_router_expert[0]], axis=1).astype(F32))
    wr_h, wr_m, _ = _split3(w_r)
    b_r = _pad_lanes(row1(jnp.concatenate([b_router_group[0], b_router_expert[0]])))
    wg = w_exp_gate[0].reshape(MOE_EXPERTS, D_MODEL, MOE_D_FF)
    wu = w_exp_up[0].reshape(MOE_EXPERTS, D_MODEL, MOE_D_FF)
    wd = w_exp_down[0].reshape(MOE_EXPERTS, MOE_D_FF, D_MODEL)

    def mixer(x2d, batch, seqlen, conv0, h0, s5re0, s5im0):
        z, xbc, u5, ga, gb, dt = _inproj(x2d, row1(norm_mix_g[0]), w_a, w_b, w_dt, dt_bias)
        y_ssd, new_conv, new_ssd = _ssd(xbc, dt, conv_w[0].astype(F32), row1(conv_b[0]), a_log, dskip, eexp, eye16,
                                        conv0, h0, batch=batch, seqlen=seqlen)
        if h0 is None:
            tl, bt = S5_TIME_TILE, S5_PROMPT_SEQS
            u_in = u5.reshape(batch, seqlen, D_MODEL)
        else:
            tl, bt = seqlen, S5_SAMPLE_SEQS
            u_in = u5
        perm = _perm_matrix(bt, tl)
        lam_rows = [jnp.broadcast_to(v, (bt, S5_FLAT)) for v in (lre, lim)]
        y5, new_re, new_im = _s5(u_in, jnp.asarray(perm).astype(BF), jnp.asarray(perm.T).astype(BF), bblk, cblk,
                                 *lam_rows, s5d, s5re0, s5im0, batch=batch, seqlen=seqlen, tl=tl, bt=bt)
        h1 = _mixout(y_ssd, z, y5.reshape(-1, D_MODEL), ga, gb, x2d, row1(ssd_norm_g[0]),
                     w_ssd_branch[0].astype(BF), w_glu[0].astype(BF), row1(b_glu[0]), w_mix_out[0].astype(BF))
        return h1, new_conv, new_ssd, new_re, new_im

    xp2 = x_prompt.reshape(tp, D_MODEL)
    xs2 = x_sample.reshape(ts, D_MODEL)
    h1p, conv_p, ssd_p, re_p, im_p = mixer(xp2, bp, lp, None, None, None, None)
    h1s, conv_s, ssd_s, re_s, im_s = mixer(xs2, bs, ls, state_conv[0], state_ssd[0],
                                           state_s5_re[0].reshape(bs, S5_FLAT), state_s5_im[0].reshape(bs, S5_FLAT))

    mk_p, mv_p, mk_p4, mv_p4 = _memkv(mem_prompt.reshape(bp * MEM_LEN, D_MODEL), row1(norm_mem_kv_g[0]), w_kv)
    gq = row1(norm_mem_q_g[0])
    wq = w_mem_q[0].astype(BF)
    wo = w_mem_o[0].astype(BF)
    h2p = _attn(h1p, gq, wq, wo, mk_p.reshape(bp, MEM_LEN, D_MODEL), mv_p.reshape(bp, MEM_LEN, D_MODEL),
                rows=MIX_TILE, nkv=1, rows_per_seq=lp)
    sample_seqs = 8
    h2s = _attn(h1s, gq, wq, wo, cache_mem_k[0], cache_mem_v[0], rows=sample_seqs * ls, nkv=sample_seqs,
                rows_per_seq=ls)

    g_ffn = row1(norm_ffn_g[0])
    x_local, r_idx, r_wt, seg = _route(h2p, h2s, g_ffn, jnp.concatenate([wr_h, wr_m], axis=1), wr_h, b_r)
    token_tiles = (tp + ts) // TOKEN_TILE
    max_granules = (2 * (tp + ts) + token_tiles * MOE_EXPERTS * (GRANULE - 1)) // GRANULE
    n_tiles = max_granules // TILE_GRANULES + MOE_EXPERTS
    te, n_used, gsrc, gdst, n_valid = _granule_plan(seg, n_tiles)
    y_local = _grouped(x_local, te, n_used, gsrc, gdst, n_valid, wg, wu, wd)
    gf = row1(norm_final_g)
    y_prompt = _merge(h2p, r_idx, r_wt, gf, y_local, tile_offset=0)
    y_sample = _merge(h2s, r_idx, r_wt, gf, y_local, tile_offset=tp // TOKEN_TILE)

    return (y_prompt.reshape(bp, lp, D_MODEL), y_sample.reshape(bs, ls, D_MODEL),
            conv_p[None], ssd_p[None],
            re_p.reshape(1, bp, S5_GROUPS, S5_STATE), im_p.reshape(1, bp, S5_GROUPS, S5_STATE),
            mk_p4[None], mv_p4[None],
            conv_s[None], ssd_s[None],
            re_s.reshape(1, bs, S5_GROUPS, S5_STATE), im_s.reshape(1, bs, S5_GROUPS, S5_STATE))
```
